```python
import jax, jax.numpy as jnp
from jax import lax
import numpy as np

D_MODEL = 1024
BATCH = 8
SEQ = 8192
DEPTH = 2

N_MIXERS = 2
HEAD_DIM = 64
N_Q_HEADS = D_MODEL // HEAD_DIM
N_KV_HEADS = max(1, N_Q_HEADS // 8)
Q_PER_KV = N_Q_HEADS // N_KV_HEADS
QKV_DIM = (N_Q_HEADS + 2 * N_KV_HEADS) * HEAD_DIM
WINDOW = 128
BLOCK = 128
ROPE_THETA = 10000.0
CONV_WIDTH = 31
D_FF = -(-(8 * D_MODEL) // (3 * 256)) * 256
N_ATTN_LAYERS = (DEPTH + 1) // 2
N_CONV_LAYERS = DEPTH // 2
RMS_EPS = 1e-5
LN_EPS = 1e-5

kernel_name = "hybrid_swa_sink_conformer_conv"


def rmsnorm(x, g):
    xf = x.astype(jnp.float32)
    y = xf * lax.rsqrt(jnp.mean(xf * xf, axis=-1, keepdims=True) + RMS_EPS)
    return (y * g.astype(jnp.float32)).astype(x.dtype)


def layernorm(x, g, b):
    xf = x.astype(jnp.float32)
    mu = jnp.mean(xf, axis=-1, keepdims=True)
    xc = xf - mu
    var = jnp.mean(xc * xc, axis=-1, keepdims=True)
    y = xc * lax.rsqrt(var + LN_EPS)
    return (y * g.astype(jnp.float32) + b.astype(jnp.float32)).astype(x.dtype)


def rope_tables(seq_len):
    pos = jnp.arange(seq_len, dtype=jnp.float32)
    inv_freq = ROPE_THETA ** (-jnp.arange(0, HEAD_DIM, 2, dtype=jnp.float32) / HEAD_DIM)
    ang = pos[:, None] * inv_freq[None, :]
    return jnp.cos(ang)[:, None, :], jnp.sin(ang)[:, None, :]


def apply_rope(t, cos, sin):
    tf = t.astype(jnp.float32)
    t1, t2 = tf[..., : HEAD_DIM // 2], tf[..., HEAD_DIM // 2:]
    out = jnp.concatenate([t1 * cos - t2 * sin, t2 * cos + t1 * sin], axis=-1)
    return out.astype(t.dtype)


def band_mask(n_blocks):
    i = jnp.arange(BLOCK)[:, None]
    j = jnp.arange(2 * BLOCK)[None, :]
    rel = i + BLOCK - j
    band = (rel >= 0) & (rel < WINDOW)
    k_pos = jnp.arange(n_blocks)[:, None] * BLOCK - BLOCK + jnp.arange(2 * BLOCK)[None, :]
    return band[None, :, :] & (k_pos >= 0)[:, None, :]


def sliding_window_gqa(h, w_qkv, b_qkv, sinks, w_o, b_o, cos, sin):
    B, S, _ = h.shape
    nb = S // BLOCK
    qkv = h @ w_qkv + b_qkv
    q_end = N_Q_HEADS * HEAD_DIM
    k_end = q_end + N_KV_HEADS * HEAD_DIM
    q = qkv[..., :q_end].reshape(B, S, N_Q_HEADS, HEAD_DIM)
    k = qkv[..., q_end:k_end].reshape(B, S, N_KV_HEADS, HEAD_DIM)
    v = qkv[..., k_end:].reshape(B, S, N_KV_HEADS, HEAD_DIM)
    q = apply_rope(q, cos, sin).reshape(B, nb, BLOCK, N_KV_HEADS, Q_PER_KV, HEAD_DIM)
    k = apply_rope(k, cos, sin).reshape(B, nb, BLOCK, N_KV_HEADS, HEAD_DIM)
    v = v.reshape(B, nb, BLOCK, N_KV_HEADS, HEAD_DIM)

    def with_prev(t):
        prev = jnp.pad(t[:, :-1], ((0, 0), (1, 0), (0, 0), (0, 0), (0, 0)))
        return jnp.concatenate([prev, t], axis=2)

    kb, vb = with_prev(k), with_prev(v)
    scores = jnp.einsum('bnqkgd,bnskd->bnkgqs', q, kb,
                        preferred_element_type=jnp.float32) * (HEAD_DIM ** -0.5)
    mask = band_mask(nb)[None, :, None, None]
    scores = jnp.where(mask, scores, jnp.finfo(jnp.float32).min)
    sink = sinks.astype(jnp.float32).reshape(N_KV_HEADS, Q_PER_KV)[None, None, :, :, None, None]
    m = jnp.maximum(jnp.max(scores, axis=-1, keepdims=True), sink)
    p = jnp.exp(scores - m)
    probs = p / (jnp.sum(p, axis=-1, keepdims=True) + jnp.exp(sink - m))
    out = jnp.einsum('bnkgqs,bnskd->bnqkgd', probs.astype(vb.dtype), vb)
    out = out.reshape(B, S, N_Q_HEADS * HEAD_DIM)
    return out @ w_o + b_o


def conformer_conv(h, w_pw1, b_pw1, w_dw, b_dw, ln_g, ln_b, w_pw2, b_pw2):
    a = h @ w_pw1 + b_pw1
    u = a[..., :D_MODEL] * jax.nn.sigmoid(a[..., D_MODEL:])
    u = lax.conv_general_dilated(
        u, w_dw[:, None, :].astype(u.dtype), window_strides=(1,),
        padding=((CONV_WIDTH - 1, 0),),
        dimension_numbers=('NWC', 'WIO', 'NWC'),
        feature_group_count=D_MODEL) + b_dw
    u = jax.nn.silu(layernorm(u, ln_g, ln_b))
    return u @ w_pw2 + b_pw2


def swiglu(h, w1, w3, w2):
    return (jax.nn.silu(h @ w1) * (h @ w3)) @ w2


def _fwd_setup_inputs(seed: int = 0) -> dict:
    key = jax.random.key(seed)
    ks = jax.random.split(key, 24)
    f32 = jnp.float32
    nrm = lambda k, shape, s: jax.random.normal(k, shape, f32) * s
    D, NA, NC = D_MODEL, N_ATTN_LAYERS, N_CONV_LAYERS
    return {
        "x": nrm(ks[0], (BATCH, SEQ, D), 1.0),
        "norm_mix": 1.0 + nrm(ks[1], (DEPTH, D), 0.02),
        "norm_ffn": 1.0 + nrm(ks[2], (DEPTH, D), 0.02),
        "attn_w_qkv": nrm(ks[3], (NA, D, QKV_DIM), D ** -0.5),
        "attn_b_qkv": nrm(ks[4], (NA, QKV_DIM), 0.02),
        "attn_sinks": nrm(ks[5], (NA, N_Q_HEADS), 0.5),
        "attn_w_o": nrm(ks[6], (NA, N_Q_HEADS * HEAD_DIM, D), (N_Q_HEADS * HEAD_DIM) ** -0.5),
        "attn_b_o": nrm(ks[7], (NA, D), 0.02),
        "conv_w_pw1": nrm(ks[8], (NC, D, 2 * D), D ** -0.5),
        "conv_b_pw1": nrm(ks[9], (NC, 2 * D), 0.02),
        "conv_w_dw": nrm(ks[10], (NC, CONV_WIDTH, D), CONV_WIDTH ** -0.5),
        "conv_b_dw": nrm(ks[11], (NC, D), 0.02),
        "conv_ln_g": 1.0 + nrm(ks[12], (NC, D), 0.02),
        "conv_ln_b": nrm(ks[13], (NC, D), 0.02),
        "conv_w_pw2": nrm(ks[14], (NC, D, D), D ** -0.5),
        "conv_b_pw2": nrm(ks[15], (NC, D), 0.02),
        "ffn_w1": nrm(ks[16], (DEPTH, D, D_FF), D ** -0.5),
        "ffn_w3": nrm(ks[17], (DEPTH, D, D_FF), D ** -0.5),
        "ffn_w2": nrm(ks[18], (DEPTH, D_FF, D), D_FF ** -0.5),
        "norm_final": 1.0 + nrm(ks[19], (D,), 0.02),
    }


def _fwd_reference(x, norm_mix, norm_ffn, attn_w_qkv, attn_b_qkv, attn_sinks, attn_w_o, attn_b_o,
              conv_w_pw1, conv_b_pw1, conv_w_dw, conv_b_dw, conv_ln_g, conv_ln_b,
              conv_w_pw2, conv_b_pw2, ffn_w1, ffn_w3, ffn_w2, norm_final):
    cos, sin = rope_tables(x.shape[1])
    h = x
    for layer in range(DEPTH):
        y = rmsnorm(h, norm_mix[layer])
        idx = layer // N_MIXERS
        if layer % N_MIXERS == 0:
            y = sliding_window_gqa(y, attn_w_qkv[idx], attn_b_qkv[idx], attn_sinks[idx],
                                   attn_w_o[idx], attn_b_o[idx], cos, sin)
        else:
            y = conformer_conv(y, conv_w_pw1[idx], conv_b_pw1[idx], conv_w_dw[idx],
                               conv_b_dw[idx], conv_ln_g[idx], conv_ln_b[idx],
                               conv_w_pw2[idx], conv_b_pw2[idx])
        h = h + y
        h = h + swiglu(rmsnorm(h, norm_ffn[layer]), ffn_w1[layer], ffn_w3[layer], ffn_w2[layer])
    return rmsnorm(h, norm_final)


import jax as _jax
import jax.numpy as _jnp

TWIN_FORMAT = 'train_step'
FWD_PARAMS = ['x', 'norm_mix', 'norm_ffn', 'attn_w_qkv', 'attn_b_qkv', 'attn_sinks', 'attn_w_o', 'attn_b_o', 'conv_w_pw1', 'conv_b_pw1', 'conv_w_dw', 'conv_b_dw', 'conv_ln_g', 'conv_ln_b', 'conv_w_pw2', 'conv_b_pw2', 'ffn_w1', 'ffn_w3', 'ffn_w2', 'norm_final']
TWIN_WEIGHTS = ['norm_mix', 'norm_ffn', 'attn_w_qkv', 'attn_b_qkv', 'attn_sinks', 'attn_w_o', 'attn_b_o', 'conv_w_pw1', 'conv_b_pw1', 'conv_w_dw', 'conv_b_dw', 'conv_ln_g', 'conv_ln_b', 'conv_w_pw2', 'conv_b_pw2', 'ffn_w1', 'ffn_w3', 'ffn_w2', 'norm_final']
TWIN_DIFF_INPUT = 'x'
TWIN_INPUTS = ['x', 'norm_mix', 'norm_ffn', 'attn_w_qkv', 'attn_b_qkv', 'attn_sinks', 'attn_w_o', 'attn_b_o', 'conv_w_pw1', 'conv_b_pw1', 'conv_w_dw', 'conv_b_dw', 'conv_ln_g', 'conv_ln_b', 'conv_w_pw2', 'conv_b_pw2', 'ffn_w1', 'ffn_w3', 'ffn_w2', 'norm_final', 'loss_target', 'm_norm_mix', 'm_norm_ffn', 'm_attn_w_qkv', 'm_attn_b_qkv', 'm_attn_sinks', 'm_attn_w_o', 'm_attn_b_o', 'm_conv_w_pw1', 'm_conv_b_pw1', 'm_conv_w_dw', 'm_conv_b_dw', 'm_conv_ln_g', 'm_conv_ln_b', 'm_conv_w_pw2', 'm_conv_b_pw2', 'm_ffn_w1', 'm_ffn_w3', 'm_ffn_w2', 'm_norm_final', 'v_norm_mix', 'v_norm_ffn', 'v_attn_w_qkv', 'v_attn_b_qkv', 'v_attn_sinks', 'v_attn_w_o', 'v_attn_b_o', 'v_conv_w_pw1', 'v_conv_b_pw1', 'v_conv_w_dw', 'v_conv_b_dw', 'v_conv_ln_g', 'v_conv_ln_b', 'v_conv_w_pw2', 'v_conv_b_pw2', 'v_ffn_w1', 'v_ffn_w3', 'v_ffn_w2', 'v_norm_final']
TWIN_OUTPUTS = ['loss', 'grad_x', 'grad_norm_mix', 'grad_norm_ffn', 'grad_attn_w_qkv', 'grad_attn_b_qkv', 'grad_attn_sinks', 'grad_attn_w_o', 'grad_attn_b_o', 'grad_conv_w_pw1', 'grad_conv_b_pw1', 'grad_conv_w_dw', 'grad_conv_b_dw', 'grad_conv_ln_g', 'grad_conv_ln_b', 'grad_conv_w_pw2', 'grad_conv_b_pw2', 'grad_ffn_w1', 'grad_ffn_w3', 'grad_ffn_w2', 'grad_norm_final', 'delta_norm_mix', 'delta_norm_ffn', 'delta_attn_w_qkv', 'delta_attn_b_qkv', 'delta_attn_sinks', 'delta_attn_w_o', 'delta_attn_b_o', 'delta_conv_w_pw1', 'delta_conv_b_pw1', 'delta_conv_w_dw', 'delta_conv_b_dw', 'delta_conv_ln_g', 'delta_conv_ln_b', 'delta_conv_w_pw2', 'delta_conv_b_pw2', 'delta_ffn_w1', 'delta_ffn_w3', 'delta_ffn_w2', 'delta_norm_final', 'new_m_norm_mix', 'new_m_norm_ffn', 'new_m_attn_w_qkv', 'new_m_attn_b_qkv', 'new_m_attn_sinks', 'new_m_attn_w_o', 'new_m_attn_b_o', 'new_m_conv_w_pw1', 'new_m_conv_b_pw1', 'new_m_conv_w_dw', 'new_m_conv_b_dw', 'new_m_conv_ln_g', 'new_m_conv_ln_b', 'new_m_conv_w_pw2', 'new_m_conv_b_pw2', 'new_m_ffn_w1', 'new_m_ffn_w3', 'new_m_ffn_w2', 'new_m_norm_final', 'new_v_norm_mix', 'new_v_norm_ffn', 'new_v_attn_w_qkv', 'new_v_attn_b_qkv', 'new_v_attn_sinks', 'new_v_attn_w_o', 'new_v_attn_b_o', 'new_v_conv_w_pw1', 'new_v_conv_b_pw1', 'new_v_conv_w_dw', 'new_v_conv_b_dw', 'new_v_conv_ln_g', 'new_v_conv_ln_b', 'new_v_conv_w_pw2', 'new_v_conv_b_pw2', 'new_v_ffn_w1', 'new_v_ffn_w3', 'new_v_ffn_w2', 'new_v_norm_final']
TWIN_LEAF_KINDS = {'loss': 'loss', 'grad_x': 'grad_x', 'grad_norm_mix': 'grad_w', 'grad_norm_ffn': 'grad_w', 'grad_attn_w_qkv': 'grad_w', 'grad_attn_b_qkv': 'grad_w', 'grad_attn_sinks': 'grad_w', 'grad_attn_w_o': 'grad_w', 'grad_attn_b_o': 'grad_w', 'grad_conv_w_pw1': 'grad_w', 'grad_conv_b_pw1': 'grad_w', 'grad_conv_w_dw': 'grad_w', 'grad_conv_b_dw': 'grad_w', 'grad_conv_ln_g': 'grad_w', 'grad_conv_ln_b': 'grad_w', 'grad_conv_w_pw2': 'grad_w', 'grad_conv_b_pw2': 'grad_w', 'grad_ffn_w1': 'grad_w', 'grad_ffn_w3': 'grad_w', 'grad_ffn_w2': 'grad_w', 'grad_norm_final': 'grad_w', 'delta_norm_mix': 'delta_w', 'delta_norm_ffn': 'delta_w', 'delta_attn_w_qkv': 'delta_w', 'delta_attn_b_qkv': 'delta_w', 'delta_attn_sinks': 'delta_w', 'delta_attn_w_o': 'delta_w', 'delta_attn_b_o': 'delta_w', 'delta_conv_w_pw1': 'delta_w', 'delta_conv_b_pw1': 'delta_w', 'delta_conv_w_dw': 'delta_w', 'delta_conv_b_dw': 'delta_w', 'delta_conv_ln_g': 'delta_w', 'delta_conv_ln_b': 'delta_w', 'delta_conv_w_pw2': 'delta_w', 'delta_conv_b_pw2': 'delta_w', 'delta_ffn_w1': 'delta_w', 'delta_ffn_w3': 'delta_w', 'delta_ffn_w2': 'delta_w', 'delta_norm_final': 'delta_w', 'new_m_norm_mix': 'new_m', 'new_m_norm_ffn': 'new_m', 'new_m_attn_w_qkv': 'new_m', 'new_m_attn_b_qkv': 'new_m', 'new_m_attn_sinks': 'new_m', 'new_m_attn_w_o': 'new_m', 'new_m_attn_b_o': 'new_m', 'new_m_conv_w_pw1': 'new_m', 'new_m_conv_b_pw1': 'new_m', 'new_m_conv_w_dw': 'new_m', 'new_m_conv_b_dw': 'new_m', 'new_m_conv_ln_g': 'new_m', 'new_m_conv_ln_b': 'new_m', 'new_m_conv_w_pw2': 'new_m', 'new_m_conv_b_pw2': 'new_m', 'new_m_ffn_w1': 'new_m', 'new_m_ffn_w3': 'new_m', 'new_m_ffn_w2': 'new_m', 'new_m_norm_final': 'new_m', 'new_v_norm_mix': 'new_v', 'new_v_norm_ffn': 'new_v', 'new_v_attn_w_qkv': 'new_v', 'new_v_attn_b_qkv': 'new_v', 'new_v_attn_sinks': 'new_v', 'new_v_attn_w_o': 'new_v', 'new_v_attn_b_o': 'new_v', 'new_v_conv_w_pw1': 'new_v', 'new_v_conv_b_pw1': 'new_v', 'new_v_conv_w_dw': 'new_v', 'new_v_conv_b_dw': 'new_v', 'new_v_conv_ln_g': 'new_v', 'new_v_conv_ln_b': 'new_v', 'new_v_conv_w_pw2': 'new_v', 'new_v_conv_b_pw2': 'new_v', 'new_v_ffn_w1': 'new_v', 'new_v_ffn_w3': 'new_v', 'new_v_ffn_w2': 'new_v', 'new_v_norm_final': 'new_v'}


def _forward(args):
    return _fwd_reference(*[args[k] for k in FWD_PARAMS])


def _output_shape():
    out = _jax.eval_shape(lambda: _forward(_fwd_setup_inputs(0)))
    return out.shape, out.dtype

N_MICROBATCH = 1
ADAM_LR = 0.001
ADAM_B1 = 0.9
ADAM_B2 = 0.999
ADAM_EPS = 1e-08
ADAM_WD = 0.01
ADAM_STEP = 10
PER_EXAMPLE_BATCH_AXIS = {'x': 0, 'loss_target': 0}
SHARED_INPUTS = []
_WEIGHT_DTYPES = {'norm_mix': _jnp.float32, 'norm_ffn': _jnp.float32, 'attn_w_qkv': _jnp.float32, 'attn_b_qkv': _jnp.float32, 'attn_sinks': _jnp.float32, 'attn_w_o': _jnp.float32, 'attn_b_o': _jnp.float32, 'conv_w_pw1': _jnp.float32, 'conv_b_pw1': _jnp.float32, 'conv_w_dw': _jnp.float32, 'conv_b_dw': _jnp.float32, 'conv_ln_g': _jnp.float32, 'conv_ln_b': _jnp.float32, 'conv_w_pw2': _jnp.float32, 'conv_b_pw2': _jnp.float32, 'ffn_w1': _jnp.float32, 'ffn_w3': _jnp.float32, 'ffn_w2': _jnp.float32, 'norm_final': _jnp.float32}
MOMENT_SCALE = {'norm_mix': 1.207186e-01, 'norm_ffn': 1.824649e-01, 'attn_w_qkv': 9.000058e-02, 'attn_b_qkv': 3.543363e-01, 'attn_sinks': 5.328444e-02, 'attn_w_o': 6.330830e-02, 'attn_b_o': 3.431072e-01, 'conv_w_pw1': 9.816658e-02, 'conv_b_pw1': 1.165481e-01, 'conv_w_dw': 1.291527e-01, 'conv_b_dw': 2.700300e-01, 'conv_ln_g': 1.532858e-01, 'conv_ln_b': 1.436421e-01, 'conv_w_pw2': 1.285171e-01, 'conv_b_pw2': 2.514831e-01, 'ffn_w1': 7.784093e-02, 'ffn_w3': 7.557337e-02, 'ffn_w2': 1.249007e-01, 'norm_final': 6.389494e+01}


def _to_microbatches(a, axis):
    t = _jnp.moveaxis(a, axis, 0)
    t = t.reshape((N_MICROBATCH, t.shape[0] // N_MICROBATCH) + t.shape[1:])
    return _jnp.moveaxis(t, 1, axis + 1)


def setup_inputs(seed: int = 0) -> dict:
    inp = _fwd_setup_inputs(seed)
    key = _jax.random.fold_in(_jax.random.key(seed), 7919)
    shape, _ = _output_shape()
    out = dict(inp)
    out["loss_target"] = _jax.random.normal(_jax.random.fold_in(key, 0), shape, _jnp.float32)
    for i, name in enumerate(TWIN_WEIGHTS):
        w = inp[name].astype(_jnp.float32)
        if MOMENT_SCALE is None:
            s = _jnp.sqrt(_jnp.mean(_jnp.square(w)) + 1e-30)
        else:
            s = MOMENT_SCALE[name]
        km, kv = _jax.random.split(_jax.random.fold_in(key, i + 1))
        out[name] = w
        out["m_" + name] = s * _jax.random.normal(km, w.shape, _jnp.float32)
        out["v_" + name] = (s * s) * _jax.random.uniform(kv, w.shape, _jnp.float32, 0.5, 1.5)
    if N_MICROBATCH > 1:
        for name, axis in PER_EXAMPLE_BATCH_AXIS.items():
            out[name] = _to_microbatches(out[name], axis)
    return {'x': out['x'], 'norm_mix': out['norm_mix'], 'norm_ffn': out['norm_ffn'], 'attn_w_qkv': out['attn_w_qkv'], 'attn_b_qkv': out['attn_b_qkv'], 'attn_sinks': out['attn_sinks'], 'attn_w_o': out['attn_w_o'], 'attn_b_o': out['attn_b_o'], 'conv_w_pw1': out['conv_w_pw1'], 'conv_b_pw1': out['conv_b_pw1'], 'conv_w_dw': out['conv_w_dw'], 'conv_b_dw': out['conv_b_dw'], 'conv_ln_g': out['conv_ln_g'], 'conv_ln_b': out['conv_ln_b'], 'conv_w_pw2': out['conv_w_pw2'], 'conv_b_pw2': out['conv_b_pw2'], 'ffn_w1': out['ffn_w1'], 'ffn_w3': out['ffn_w3'], 'ffn_w2': out['ffn_w2'], 'norm_final': out['norm_final'], 'loss_target': out['loss_target'], 'm_norm_mix': out['m_norm_mix'], 'm_norm_ffn': out['m_norm_ffn'], 'm_attn_w_qkv': out['m_attn_w_qkv'], 'm_attn_b_qkv': out['m_attn_b_qkv'], 'm_attn_sinks': out['m_attn_sinks'], 'm_attn_w_o': out['m_attn_w_o'], 'm_attn_b_o': out['m_attn_b_o'], 'm_conv_w_pw1': out['m_conv_w_pw1'], 'm_conv_b_pw1': out['m_conv_b_pw1'], 'm_conv_w_dw': out['m_conv_w_dw'], 'm_conv_b_dw': out['m_conv_b_dw'], 'm_conv_ln_g': out['m_conv_ln_g'], 'm_conv_ln_b': out['m_conv_ln_b'], 'm_conv_w_pw2': out['m_conv_w_pw2'], 'm_conv_b_pw2': out['m_conv_b_pw2'], 'm_ffn_w1': out['m_ffn_w1'], 'm_ffn_w3': out['m_ffn_w3'], 'm_ffn_w2': out['m_ffn_w2'], 'm_norm_final': out['m_norm_final'], 'v_norm_mix': out['v_norm_mix'], 'v_norm_ffn': out['v_norm_ffn'], 'v_attn_w_qkv': out['v_attn_w_qkv'], 'v_attn_b_qkv': out['v_attn_b_qkv'], 'v_attn_sinks': out['v_attn_sinks'], 'v_attn_w_o': out['v_attn_w_o'], 'v_attn_b_o': out['v_attn_b_o'], 'v_conv_w_pw1': out['v_conv_w_pw1'], 'v_conv_b_pw1': out['v_conv_b_pw1'], 'v_conv_w_dw': out['v_conv_w_dw'], 'v_conv_b_dw': out['v_conv_b_dw'], 'v_conv_ln_g': out['v_conv_ln_g'], 'v_conv_ln_b': out['v_conv_ln_b'], 'v_conv_w_pw2': out['v_conv_w_pw2'], 'v_conv_b_pw2': out['v_conv_b_pw2'], 'v_ffn_w1': out['v_ffn_w1'], 'v_ffn_w3': out['v_ffn_w3'], 'v_ffn_w2': out['v_ffn_w2'], 'v_norm_final': out['v_norm_final']}


def _loss(weights, diff, rest, loss_target):
    with _jax.named_scope("forward"):
        args = {**rest, TWIN_DIFF_INPUT: diff, **{k: w.astype(_WEIGHT_DTYPES[k]) for k, w in weights.items()}}
        y = _forward(args)
    with _jax.named_scope("loss_head"):
        err = _jnp.square(y.astype(_jnp.float32) - loss_target)
        return 0.5 * _jnp.sum(_jnp.mean(err, axis=-1)) if err.ndim else 0.5 * err


def _adamw(w, g, m, v):
    m = ADAM_B1 * m + (1.0 - ADAM_B1) * g
    v = ADAM_B2 * v + (1.0 - ADAM_B2) * _jnp.square(g)
    m_hat = m / (1.0 - ADAM_B1 ** ADAM_STEP)
    v_hat = v / (1.0 - ADAM_B2 ** ADAM_STEP)
    delta = -ADAM_LR * (m_hat / (_jnp.sqrt(v_hat) + ADAM_EPS) + ADAM_WD * w)
    return delta, m, v


def reference(x, norm_mix, norm_ffn, attn_w_qkv, attn_b_qkv, attn_sinks, attn_w_o, attn_b_o, conv_w_pw1, conv_b_pw1, conv_w_dw, conv_b_dw, conv_ln_g, conv_ln_b, conv_w_pw2, conv_b_pw2, ffn_w1, ffn_w3, ffn_w2, norm_final, loss_target, m_norm_mix, m_norm_ffn, m_attn_w_qkv, m_attn_b_qkv, m_attn_sinks, m_attn_w_o, m_attn_b_o, m_conv_w_pw1, m_conv_b_pw1, m_conv_w_dw, m_conv_b_dw, m_conv_ln_g, m_conv_ln_b, m_conv_w_pw2, m_conv_b_pw2, m_ffn_w1, m_ffn_w3, m_ffn_w2, m_norm_final, v_norm_mix, v_norm_ffn, v_attn_w_qkv, v_attn_b_qkv, v_attn_sinks, v_attn_w_o, v_attn_b_o, v_conv_w_pw1, v_conv_b_pw1, v_conv_w_dw, v_conv_b_dw, v_conv_ln_g, v_conv_ln_b, v_conv_w_pw2, v_conv_b_pw2, v_ffn_w1, v_ffn_w3, v_ffn_w2, v_norm_final):
    given = dict(x=x, norm_mix=norm_mix, norm_ffn=norm_ffn, attn_w_qkv=attn_w_qkv, attn_b_qkv=attn_b_qkv, attn_sinks=attn_sinks, attn_w_o=attn_w_o, attn_b_o=attn_b_o, conv_w_pw1=conv_w_pw1, conv_b_pw1=conv_b_pw1, conv_w_dw=conv_w_dw, conv_b_dw=conv_b_dw, conv_ln_g=conv_ln_g, conv_ln_b=conv_ln_b, conv_w_pw2=conv_w_pw2, conv_b_pw2=conv_b_pw2, ffn_w1=ffn_w1, ffn_w3=ffn_w3, ffn_w2=ffn_w2, norm_final=norm_final, loss_target=loss_target, m_norm_mix=m_norm_mix, m_norm_ffn=m_norm_ffn, m_attn_w_qkv=m_attn_w_qkv, m_attn_b_qkv=m_attn_b_qkv, m_attn_sinks=m_attn_sinks, m_attn_w_o=m_attn_w_o, m_attn_b_o=m_attn_b_o, m_conv_w_pw1=m_conv_w_pw1, m_conv_b_pw1=m_conv_b_pw1, m_conv_w_dw=m_conv_w_dw, m_conv_b_dw=m_conv_b_dw, m_conv_ln_g=m_conv_ln_g, m_conv_ln_b=m_conv_ln_b, m_conv_w_pw2=m_conv_w_pw2, m_conv_b_pw2=m_conv_b_pw2, m_ffn_w1=m_ffn_w1, m_ffn_w3=m_ffn_w3, m_ffn_w2=m_ffn_w2, m_norm_final=m_norm_final, v_norm_mix=v_norm_mix, v_norm_ffn=v_norm_ffn, v_attn_w_qkv=v_attn_w_qkv, v_attn_b_qkv=v_attn_b_qkv, v_attn_sinks=v_attn_sinks, v_attn_w_o=v_attn_w_o, v_attn_b_o=v_attn_b_o, v_conv_w_pw1=v_conv_w_pw1, v_conv_b_pw1=v_conv_b_pw1, v_conv_w_dw=v_conv_w_dw, v_conv_b_dw=v_conv_b_dw, v_conv_ln_g=v_conv_ln_g, v_conv_ln_b=v_conv_ln_b, v_conv_w_pw2=v_conv_w_pw2, v_conv_b_pw2=v_conv_b_pw2, v_ffn_w1=v_ffn_w1, v_ffn_w3=v_ffn_w3, v_ffn_w2=v_ffn_w2, v_norm_final=v_norm_final)
    weights = {n: given[n] for n in TWIN_WEIGHTS}
    shared = {n: given[n] for n in SHARED_INPUTS}
    per_example = {n: given[n] for n in ['x']}
    grad_fn = _jax.value_and_grad(_loss, argnums=(0, 1))

    def one_microbatch(ex, loss_target):
        ex = dict(ex)
        diff = ex.pop(TWIN_DIFF_INPUT)
        return grad_fn(weights, diff, {**shared, **ex}, loss_target)

    if N_MICROBATCH == 1:
        loss, (grad_w, grad_x) = one_microbatch(per_example, given["loss_target"])
    else:
        def body(carry, xs):
            loss_sum, grad_sum = carry
            l_k, (gw_k, gx_k) = one_microbatch(xs[0], xs[1])
            with _jax.named_scope("update"):
                return (loss_sum + l_k, _jax.tree.map(_jnp.add, grad_sum, gw_k)), gx_k

        init = (_jnp.zeros((), _jnp.float32), _jax.tree.map(_jnp.zeros_like, weights))
        (loss, grad_w), grad_x = _jax.lax.scan(body, init, (per_example, given["loss_target"]))
    with _jax.named_scope("update"):
        delta_w, new_m, new_v = {}, {}, {}
        for n in TWIN_WEIGHTS:
            delta_w[n], new_m[n], new_v[n] = _adamw(weights[n], grad_w[n], given["m_" + n], given["v_" + n])
    return (loss, grad_x, *[grad_w[n] for n in TWIN_WEIGHTS], *[delta_w[n] for n in TWIN_WEIGHTS],
            *[new_m[n] for n in TWIN_WEIGHTS], *[new_v[n] for n in TWIN_WEIGHTS])
```

```python
import jax
import jax.numpy as jnp
from jax import lax
from jax.experimental import pallas as pl
from jax.experimental.pallas import tpu as pltpu

F32 = jnp.float32
BF16 = jnp.bfloat16

D_MODEL = 1024
HEAD_DIM = 64
N_Q_HEADS = 16
N_KV_HEADS = 2
Q_PER_KV = 8
Q_DIM = N_Q_HEADS * HEAD_DIM
KV_DIM = N_KV_HEADS * HEAD_DIM
QKV_DIM = Q_DIM + 2 * KV_DIM
BLOCK = 128
CONV_WIDTH = 31
D_FF = 2816
ROPE_THETA = 10000.0
RMS_EPS = 1e-5
LN_EPS = 1e-5
ADAM_LR = 0.001
ADAM_B1 = 0.9
ADAM_B2 = 0.999
ADAM_EPS = 1e-08
ADAM_WD = 0.01
ADAM_STEP = 10
N_DEV = 8

LANES = 128
SUBLANES = 8
TOKEN_TILE = 512
FF_TILE = 256
CONV_CHUNK = 64
CONV_HALO = 32
TAPS_PAD = 32
VMEM_LIMIT = 48 * 1024 * 1024
NEG_INF = float(jnp.finfo(jnp.float32).min)

ROWS_QKV = D_MODEL * QKV_DIM // N_DEV // D_MODEL
ROWS_WO = Q_DIM // N_DEV
ROWS_PW1 = 2 * D_MODEL // N_DEV
ROWS_PW2 = D_MODEL // N_DEV
ROWS_FFN = 2 * D_FF // N_DEV
ROWS_BIG = ROWS_QKV + ROWS_WO + ROWS_PW1 + ROWS_PW2 + 3 * ROWS_FFN
ROWS_SMALL = 16
ROWS_ALL = ROWS_BIG + ROWS_SMALL
SMALL_USED = 2 * D_MODEL // N_DEV + CONV_WIDTH * (D_MODEL // N_DEV) + 4 * (D_MODEL // N_DEV)
ADAM_ROW_TILE = 400
REPL_ROWS = 16


def _call(body, *, name, grid, in_specs, out_specs, out_shape, scratch=(), sem=None):
    return pl.pallas_call(
        body, name=name, grid=grid, in_specs=in_specs, out_specs=out_specs, out_shape=out_shape,
        scratch_shapes=list(scratch),
        compiler_params=pltpu.CompilerParams(dimension_semantics=sem, vmem_limit_bytes=VMEM_LIMIT))


def _full(shape):
    return pl.BlockSpec(shape, lambda *_: (0,) * len(shape))


def _rows(tm, n):
    return pl.BlockSpec((tm, n), lambda i, *_: (i, 0))


def _sig(x):
    return 1.0 / (1.0 + jnp.exp(-x))


def _sum_rows(x):
    return jnp.sum(x, axis=0, keepdims=True)


def _nt(a, b):
    return lax.dot_general(a, b, (((1,), (1,)), ((), ())), preferred_element_type=F32)


def _tn(a, b):
    return lax.dot_general(a, b, (((0,), (0,)), ((), ())), preferred_element_type=F32)


def _rms_stats(x):
    return lax.rsqrt(jnp.mean(x * x, axis=-1, keepdims=True) + RMS_EPS)


def _rms_bwd(dy, x, g, dres):
    r = _rms_stats(x)
    n = x * r
    dn = dy * g
    dx = dres + r * (dn - n * jnp.mean(dn * n, axis=-1, keepdims=True))
    return dx, _sum_rows(dy * n)


def _rope_tables(seq):
    pos = jnp.arange(seq, dtype=F32)
    inv_freq = ROPE_THETA ** (-jnp.arange(0, HEAD_DIM, 2, dtype=F32) / HEAD_DIM)
    ang = pos[:, None] * inv_freq[None, :]
    cos, sin = jnp.cos(ang), jnp.sin(ang)
    zero = jnp.zeros_like(sin)
    reps = LANES // HEAD_DIM
    c = jnp.tile(jnp.concatenate([cos, cos], axis=1), (1, reps))
    sa = jnp.tile(jnp.concatenate([-sin, zero], axis=1), (1, reps))
    sb = jnp.tile(jnp.concatenate([zero, sin], axis=1), (1, reps))
    return c, sa, sb


def _rope(t, c, sa, sb):
    half = HEAD_DIM // 2
    return t * c + pltpu.roll(t, LANES - half, 1) * sa + pltpu.roll(t, half, 1) * sb


def _rope_t(dt, c, sa, sb):
    half = HEAD_DIM // 2
    return dt * c + pltpu.roll(dt * sa, half, 1) + pltpu.roll(dt * sb, LANES - half, 1)


def _qkv_fwd(x, g, w, b, rc, rsa, rsb):
    seq = x.shape[0]
    tm = min(TOKEN_TILE, seq)

    def body(x_ref, g_ref, w_ref, b_ref, c_ref, sa_ref, sb_ref, y_ref, q_ref, k_ref, v_ref):
        xv = x_ref[...]
        y = (xv * _rms_stats(xv) * g_ref[...]).astype(BF16)
        y_ref[...] = y
        qkv = jnp.dot(y, w_ref[...], preferred_element_type=F32) + b_ref[...]
        c, sa, sb = c_ref[...], sa_ref[...], sb_ref[...]
        for i in range(Q_DIM // LANES):
            blk = _rope(qkv[:, i * LANES:(i + 1) * LANES], c, sa, sb)
            q_ref[:, i * LANES:(i + 1) * LANES] = (blk * (HEAD_DIM ** -0.5)).astype(BF16)
        k_ref[...] = _rope(qkv[:, Q_DIM:Q_DIM + KV_DIM], c, sa, sb).astype(BF16)
        v_ref[...] = qkv[:, Q_DIM + KV_DIM:].astype(BF16)

    return _call(
        body, name="qkv_fwd", grid=(seq // tm,),
        in_specs=[_rows(tm, D_MODEL), _full((1, D_MODEL)), _full((D_MODEL, QKV_DIM)), _full((1, QKV_DIM)),
                  _rows(tm, LANES), _rows(tm, LANES), _rows(tm, LANES)],
        out_specs=[_rows(tm, D_MODEL), _rows(tm, Q_DIM), _rows(tm, KV_DIM), _rows(tm, KV_DIM)],
        out_shape=[jax.ShapeDtypeStruct((seq, D_MODEL), BF16), jax.ShapeDtypeStruct((seq, Q_DIM), BF16),
                   jax.ShapeDtypeStruct((seq, KV_DIM), BF16), jax.ShapeDtypeStruct((seq, KV_DIM), BF16)],
        sem=("parallel",))(x, g, w, b, rc, rsa, rsb)


def _band_mask(n):
    row = lax.broadcasted_iota(jnp.int32, (BLOCK, 2 * BLOCK), 0)
    col = lax.broadcasted_iota(jnp.int32, (BLOCK, 2 * BLOCK), 1)
    rel = row + BLOCK - col
    return (rel >= 0) & (rel < BLOCK) & ((col >= BLOCK) | (n > 0))


def _softmax_with_sink(s, mask, sink):
    s = jnp.where(mask, s, NEG_INF)
    m = jnp.maximum(jnp.max(s, axis=-1, keepdims=True), sink)
    p = jnp.exp(s - m)
    e_sink = jnp.exp(sink - m)
    den = jnp.sum(p, axis=-1, keepdims=True) + e_sink
    return p / den, e_sink / den


def _kv_specs():
    cur = pl.BlockSpec((BLOCK, KV_DIM), lambda n: (n, 0))
    prev = pl.BlockSpec((BLOCK, KV_DIM), lambda n: (jnp.maximum(n - 1, 0), 0))
    return cur, prev


def _attn_fwd(sinks, q, k, v):
    seq = q.shape[0]
    cur, prev = _kv_specs()

    def body(sink_ref, q_ref, kc_ref, kp_ref, vc_ref, vp_ref, o_ref):
        mask = _band_mask(pl.program_id(0))
        for j in range(N_KV_HEADS):
            cs = slice(j * HEAD_DIM, (j + 1) * HEAD_DIM)
            kk = jnp.concatenate([kp_ref[:, cs], kc_ref[:, cs]], axis=0)
            vv = jnp.concatenate([vp_ref[:, cs], vc_ref[:, cs]], axis=0)
            for gq in range(Q_PER_KV):
                h = j * Q_PER_KV + gq
                hs = slice(h * HEAD_DIM, (h + 1) * HEAD_DIM)
                probs, _ = _softmax_with_sink(_nt(q_ref[:, hs], kk), mask, sink_ref[h])
                o = jnp.dot(probs.astype(BF16), vv, preferred_element_type=F32)
                o_ref[:, hs] = o.astype(BF16)

    return _call(
        body, name="attn_fwd", grid=(seq // BLOCK,),
        in_specs=[pl.BlockSpec(memory_space=pltpu.SMEM), _rows(BLOCK, Q_DIM), cur, prev, cur, prev],
        out_specs=_rows(BLOCK, Q_DIM), out_shape=jax.ShapeDtypeStruct((seq, Q_DIM), BF16),
        sem=("parallel",))(sinks, q, k, k, v, v)


def _mm_res(name, a, w, b, res):
    seq, kdim = a.shape
    n = w.shape[1]
    tm = min(TOKEN_TILE, seq)

    def body(a_ref, w_ref, b_ref, r_ref, o_ref):
        o_ref[...] = r_ref[...] + (jnp.dot(a_ref[...], w_ref[...], preferred_element_type=F32) + b_ref[...])

    return _call(
        body, name=name, grid=(seq // tm,),
        in_specs=[_rows(tm, kdim), _full((kdim, n)), _full((1, n)), _rows(tm, n)],
        out_specs=_rows(tm, n), out_shape=jax.ShapeDtypeStruct((seq, n), F32),
        sem=("parallel",))(a, w, b, res)


def _ffn_fwd(name, h, g, w1, w3, w2):
    seq = h.shape[0]
    tm = min(TOKEN_TILE, seq)
    tf = FF_TILE

    def body(h_ref, g_ref, w1_ref, w3_ref, w2_ref, f_ref, u_ref, w_ref, o_ref):
        @pl.when(pl.program_id(1) == 0)
        def _():
            hv = h_ref[...]
            f_ref[...] = (hv * _rms_stats(hv) * g_ref[...]).astype(BF16)
            o_ref[...] = hv

        f = f_ref[...]
        u = jnp.dot(f, w1_ref[...], preferred_element_type=F32)
        w = jnp.dot(f, w3_ref[...], preferred_element_type=F32)
        u_ref[...] = u.astype(BF16)
        w_ref[...] = w.astype(BF16)
        s = (u * _sig(u) * w).astype(BF16)
        o_ref[...] += jnp.dot(s, w2_ref[...], preferred_element_type=F32)

    tile_ff = pl.BlockSpec((tm, tf), lambda i, j: (i, j))
    return _call(
        body, name=name, grid=(seq // tm, D_FF // tf),
        in_specs=[_rows(tm, D_MODEL), _full((1, D_MODEL)),
                  pl.BlockSpec((D_MODEL, tf), lambda i, j: (0, j)), pl.BlockSpec((D_MODEL, tf), lambda i, j: (0, j)),
                  pl.BlockSpec((tf, D_MODEL), lambda i, j: (j, 0))],
        out_specs=[_rows(tm, D_MODEL), tile_ff, tile_ff, _rows(tm, D_MODEL)],
        out_shape=[jax.ShapeDtypeStruct((seq, D_MODEL), BF16), jax.ShapeDtypeStruct((seq, D_FF), BF16),
                   jax.ShapeDtypeStruct((seq, D_FF), BF16), jax.ShapeDtypeStruct((seq, D_MODEL), F32)],
        sem=("parallel", "arbitrary"))(h, g, w1, w3, w2)


def _conv_fwd(h, g, wpw1, bpw1, wd, bdw, lng, lnb):
    seq = h.shape[0]
    tm = min(TOKEN_TILE, seq)
    n_chunks = tm // CONV_CHUNK
    win = CONV_CHUNK + CONV_HALO

    def body(h_ref, g_ref, w_ref, b_ref, wd_ref, bdw_ref, lng_ref, lnb_ref,
             y_ref, a_ref, dwc_ref, z_ref, gbuf):
        i = pl.program_id(0)

        @pl.when(i == 0)
        def _():
            gbuf[0:CONV_HALO, :] = jnp.zeros((CONV_HALO, D_MODEL), F32)

        @pl.when(i > 0)
        def _():
            gbuf[0:CONV_HALO, :] = gbuf[tm:tm + CONV_HALO, :]

        hv = h_ref[...]
        y = (hv * _rms_stats(hv) * g_ref[...]).astype(BF16)
        y_ref[...] = y
        a = jnp.dot(y, w_ref[...], preferred_element_type=F32) + b_ref[...]
        a_ref[...] = a.astype(BF16)
        gbuf[CONV_HALO:CONV_HALO + tm, :] = a[:, :D_MODEL] * _sig(a[:, D_MODEL:])

        def chunk(c, carry):
            r0 = pl.multiple_of(c * CONV_CHUNK, CONV_CHUNK)
            for l in range(D_MODEL // LANES):
                ls = slice(l * LANES, (l + 1) * LANES)
                gw = gbuf[pl.ds(r0, win), ls]
                acc = jnp.zeros((CONV_CHUNK, LANES), F32) + bdw_ref[:, ls]
                for s in range(SUBLANES):
                    gs = gw if s == 0 else pltpu.roll(gw, s, 0)
                    for q in range(CONV_HALO // SUBLANES):
                        d = SUBLANES * q + s
                        if d < CONV_WIDTH:
                            lo = CONV_HALO - SUBLANES * q
                            acc = acc + wd_ref[d:d + 1, ls] * gs[lo:lo + CONV_CHUNK]
                dwc_ref[pl.ds(r0, CONV_CHUNK), ls] = acc
            return carry

        lax.fori_loop(0, n_chunks, chunk, 0)

        xv = dwc_ref[...]
        mu = jnp.mean(xv, axis=-1, keepdims=True)
        xc = xv - mu
        var = jnp.mean(xc * xc, axis=-1, keepdims=True)
        ln = xc * lax.rsqrt(var + LN_EPS) * lng_ref[...] + lnb_ref[...]
        z_ref[...] = (ln * _sig(ln)).astype(BF16)

    return _call(
        body, name="conv_fwd", grid=(seq // tm,),
        in_specs=[_rows(tm, D_MODEL), _full((1, D_MODEL)), _full((D_MODEL, 2 * D_MODEL)), _full((1, 2 * D_MODEL)),
                  _full((TAPS_PAD, D_MODEL)), _full((1, D_MODEL)), _full((1, D_MODEL)), _full((1, D_MODEL))],
        out_specs=[_rows(tm, D_MODEL), _rows(tm, 2 * D_MODEL), _rows(tm, D_MODEL), _rows(tm, D_MODEL)],
        out_shape=[jax.ShapeDtypeStruct((seq, D_MODEL), BF16), jax.ShapeDtypeStruct((seq, 2 * D_MODEL), BF16),
                   jax.ShapeDtypeStruct((seq, D_MODEL), F32), jax.ShapeDtypeStruct((seq, D_MODEL), BF16)],
        scratch=[pltpu.VMEM((tm + CONV_HALO, D_MODEL), F32)],
        sem=("arbitrary",))(h, g, wpw1, bpw1, wd, bdw, lng, lnb)


def _final_loss(h, target, g):
    seq = h.shape[0]
    tm = min(TOKEN_TILE, seq)

    def body(h_ref, t_ref, g_ref, dh_ref, loss_ref, dg_ref):
        @pl.when(pl.program_id(0) == 0)
        def _():
            loss_ref[...] = jnp.zeros_like(loss_ref)
            dg_ref[...] = jnp.zeros_like(dg_ref)

        hv = h_ref[...]
        gv = g_ref[...]
        err = hv * _rms_stats(hv) * gv - t_ref[...]
        sq = jnp.sum(jnp.sum(err * err, axis=-1, keepdims=True), axis=0, keepdims=True)
        loss_ref[...] += jnp.broadcast_to(sq, loss_ref.shape)
        dx, dg = _rms_bwd(err * (1.0 / D_MODEL), hv, gv, 0.0)
        dh_ref[...] = dx
        dg_ref[...] += dg

    return _call(
        body, name="final_loss", grid=(seq // tm,),
        in_specs=[_rows(tm, D_MODEL), _rows(tm, D_MODEL), _full((1, D_MODEL))],
        out_specs=[_rows(tm, D_MODEL), _full((SUBLANES, LANES)), _full((1, D_MODEL))],
        out_shape=[jax.ShapeDtypeStruct((seq, D_MODEL), F32), jax.ShapeDtypeStruct((SUBLANES, LANES), F32),
                   jax.ShapeDtypeStruct((1, D_MODEL), F32)],
        sem=("arbitrary",))(h, target, g)


def _ffn_bwd(name, dh, h_in, g, u, w, w1, w3, w2):
    seq = dh.shape[0]
    tm = min(TOKEN_TILE, seq)
    tf = FF_TILE
    n_ff = D_FF // tf

    def body(dh_ref, h_ref, g_ref, u_ref, w_ref, w1_ref, w3_ref, w2_ref,
             du_ref, dw_ref, s_ref, dx_ref, dg_ref, dhb):
        i, j = pl.program_id(0), pl.program_id(1)

        @pl.when((i == 0) & (j == 0))
        def _():
            dg_ref[...] = jnp.zeros_like(dg_ref)

        @pl.when(j == 0)
        def _():
            dhb[...] = dh_ref[...].astype(BF16)
            dx_ref[...] = jnp.zeros_like(dx_ref)

        ds = _nt(dhb[...], w2_ref[...])
        uv = u_ref[...].astype(F32)
        wv = w_ref[...].astype(F32)
        sg = _sig(uv)
        su = uv * sg
        s_ref[...] = (su * wv).astype(BF16)
        dwv = (ds * su).astype(BF16)
        duv = (ds * wv * (sg * (1.0 + uv * (1.0 - sg)))).astype(BF16)
        du_ref[...] = duv
        dw_ref[...] = dwv
        dx_ref[...] += _nt(duv, w1_ref[...]) + _nt(dwv, w3_ref[...])

        @pl.when(j == n_ff - 1)
        def _():
            dx, dg = _rms_bwd(dx_ref[...], h_ref[...], g_ref[...], dh_ref[...])
            dx_ref[...] = dx
            dg_ref[...] += dg

    tile_ff = pl.BlockSpec((tm, tf), lambda i, j: (i, j))
    return _call(
        body, name=name, grid=(seq // tm, n_ff),
        in_specs=[_rows(tm, D_MODEL), _rows(tm, D_MODEL), _full((1, D_MODEL)), tile_ff, tile_ff,
                  pl.BlockSpec((D_MODEL, tf), lambda i, j: (0, j)), pl.BlockSpec((D_MODEL, tf), lambda i, j: (0, j)),
                  pl.BlockSpec((tf, D_MODEL), lambda i, j: (j, 0))],
        out_specs=[tile_ff, tile_ff, tile_ff, _rows(tm, D_MODEL), _full((1, D_MODEL))],
        out_shape=[jax.ShapeDtypeStruct((seq, D_FF), BF16), jax.ShapeDtypeStruct((seq, D_FF), BF16),
                   jax.ShapeDtypeStruct((seq, D_FF), BF16), jax.ShapeDtypeStruct((seq, D_MODEL), F32),
                   jax.ShapeDtypeStruct((1, D_MODEL), F32)],
        scratch=[pltpu.VMEM((tm, D_MODEL), BF16)],
        sem=("arbitrary", "arbitrary"))(dh, h_in, g, u, w, w1, w3, w2)


def _mm_tn(name, a, b, *, tk, tn):
    seq, kdim = a.shape
    n = b.shape[1]
    tt = min(TOKEN_TILE, seq)

    def body(a_ref, b_ref, o_ref):
        @pl.when(pl.program_id(2) == 0)
        def _():
            o_ref[...] = jnp.zeros_like(o_ref)

        o_ref[...] += _tn(a_ref[...].astype(BF16), b_ref[...].astype(BF16))

    return _call(
        body, name=name, grid=(kdim // tk, n // tn, seq // tt),
        in_specs=[pl.BlockSpec((tt, tk), lambda k, j, t: (t, k)), pl.BlockSpec((tt, tn), lambda k, j, t: (t, j))],
        out_specs=pl.BlockSpec((tk, tn), lambda k, j, t: (k, j)),
        out_shape=jax.ShapeDtypeStruct((kdim, n), F32),
        sem=("parallel", "parallel", "arbitrary"))(a, b)


def _conv_bwd(dh, wpw2, dwc, a, lng, lnb, wd):
    seq = dh.shape[0]
    tm = min(TOKEN_TILE, seq)
    nt = seq // tm
    n_chunks = tm // CONV_CHUNK
    win = CONV_CHUNK + CONV_HALO
    halo_per_tile = tm // CONV_HALO

    def body(dh_ref, w_ref, dwc_ref, a_ref, ah_ref, lng_ref, lnb_ref, wd_ref,
             da_ref, dlg_ref, dlb_ref, dbdw_ref, dwd_ref, dbpw1_ref, dbpw2_ref,
             gbuf, dbuf, dglu, dwd_part):
        i = pl.program_id(0)
        r = nt - 1 - i

        @pl.when(i == 0)
        def _():
            dlg_ref[...] = jnp.zeros_like(dlg_ref)
            dlb_ref[...] = jnp.zeros_like(dlb_ref)
            dbdw_ref[...] = jnp.zeros_like(dbdw_ref)
            dbpw1_ref[...] = jnp.zeros_like(dbpw1_ref)
            dbpw2_ref[...] = jnp.zeros_like(dbpw2_ref)
            dwd_part[...] = jnp.zeros_like(dwd_part)
            dbuf[tm:tm + CONV_HALO, :] = jnp.zeros((CONV_HALO, D_MODEL), F32)

        @pl.when(i > 0)
        def _():
            dbuf[tm:tm + CONV_HALO, :] = dbuf[0:CONV_HALO, :]

        dhv = dh_ref[...]
        dbpw2_ref[...] += _sum_rows(dhv)
        dz = _nt(dhv.astype(BF16), w_ref[...])
        xv = dwc_ref[...]
        lg = lng_ref[...]
        mu = jnp.mean(xv, axis=-1, keepdims=True)
        xc = xv - mu
        rstd = lax.rsqrt(jnp.mean(xc * xc, axis=-1, keepdims=True) + LN_EPS)
        xhat = xc * rstd
        ln = xhat * lg + lnb_ref[...]
        sg = _sig(ln)
        dln = dz * (sg * (1.0 + ln * (1.0 - sg)))
        dlg_ref[...] += _sum_rows(dln * xhat)
        dlb_ref[...] += _sum_rows(dln)
        dxh = dln * lg
        ddw = rstd * (dxh - jnp.mean(dxh, axis=-1, keepdims=True)
                      - xhat * jnp.mean(dxh * xhat, axis=-1, keepdims=True))
        dbdw_ref[...] += _sum_rows(ddw)
        dbuf[0:tm, :] = ddw

        av = a_ref[...].astype(F32)
        a1 = av[:, :D_MODEL]
        s2 = _sig(av[:, D_MODEL:])
        gbuf[CONV_HALO:CONV_HALO + tm, :] = a1 * s2
        ah = ah_ref[...].astype(F32)
        gh = ah[:, :D_MODEL] * _sig(ah[:, D_MODEL:])
        gbuf[0:CONV_HALO, :] = jnp.where(r > 0, gh, 0.0)

        def chunk(c, carry):
            r0 = pl.multiple_of(c * CONV_CHUNK, CONV_CHUNK)
            for l in range(D_MODEL // LANES):
                ls = slice(l * LANES, (l + 1) * LANES)
                dw_ = dbuf[pl.ds(r0, win), ls]
                gw = gbuf[pl.ds(r0, win), ls]
                dc = dw_[0:CONV_CHUNK]
                acc = jnp.zeros((CONV_CHUNK, LANES), F32)
                for s in range(SUBLANES):
                    ds_ = dw_ if s == 0 else pltpu.roll(dw_, win - s, 0)
                    gs = gw if s == 0 else pltpu.roll(gw, s, 0)
                    for q in range(CONV_HALO // SUBLANES):
                        d = SUBLANES * q + s
                        if d < CONV_WIDTH:
                            acc = acc + wd_ref[d:d + 1, ls] * ds_[SUBLANES * q:SUBLANES * q + CONV_CHUNK]
                            lo = CONV_HALO - SUBLANES * q
                            prod = dc * gs[lo:lo + CONV_CHUNK]
                            dwd_part[d, :, ls] += jnp.sum(
                                prod.reshape(CONV_CHUNK // SUBLANES, SUBLANES, LANES), axis=0)
                dglu[pl.ds(r0, CONV_CHUNK), ls] = acc
            return carry

        lax.fori_loop(0, n_chunks, chunk, 0)

        dg_ = dglu[...]
        da1 = dg_ * s2
        da2 = dg_ * a1 * s2 * (1.0 - s2)
        da_ref[:, :D_MODEL] = da1.astype(BF16)
        da_ref[:, D_MODEL:] = da2.astype(BF16)
        dbpw1_ref[:, :D_MODEL] += _sum_rows(da1)
        dbpw1_ref[:, D_MODEL:] += _sum_rows(da2)

        @pl.when(i == nt - 1)
        def _():
            dwd_ref[...] = jnp.sum(dwd_part[...], axis=1)

    rev = lambda n: pl.BlockSpec((tm, n), lambda i: (nt - 1 - i, 0))
    halo = pl.BlockSpec((CONV_HALO, 2 * D_MODEL),
                        lambda i: (jnp.maximum((nt - 1 - i) * halo_per_tile - 1, 0), 0))
    vec = lambda n: _full((1, n))
    return _call(
        body, name="conv_bwd", grid=(nt,),
        in_specs=[rev(D_MODEL), _full((D_MODEL, D_MODEL)), rev(D_MODEL), rev(2 * D_MODEL), halo,
                  vec(D_MODEL), vec(D_MODEL), _full((TAPS_PAD, D_MODEL))],
        out_specs=[rev(2 * D_MODEL), vec(D_MODEL), vec(D_MODEL), vec(D_MODEL), _full((TAPS_PAD, D_MODEL)),
                   vec(2 * D_MODEL), vec(D_MODEL)],
        out_shape=[jax.ShapeDtypeStruct((seq, 2 * D_MODEL), BF16), jax.ShapeDtypeStruct((1, D_MODEL), F32),
                   jax.ShapeDtypeStruct((1, D_MODEL), F32), jax.ShapeDtypeStruct((1, D_MODEL), F32),
                   jax.ShapeDtypeStruct((TAPS_PAD, D_MODEL), F32), jax.ShapeDtypeStruct((1, 2 * D_MODEL), F32),
                   jax.ShapeDtypeStruct((1, D_MODEL), F32)],
        scratch=[pltpu.VMEM((tm + CONV_HALO, D_MODEL), F32), pltpu.VMEM((tm + CONV_HALO, D_MODEL), F32),
                 pltpu.VMEM((tm, D_MODEL), F32), pltpu.VMEM((TAPS_PAD, SUBLANES, D_MODEL), F32)],
        sem=("arbitrary",))(dh, wpw2, dwc, a, a, lng, lnb, wd)


def _nt_rms_bwd(name, dact, w, h_in, dres, g):
    seq, n = dact.shape
    tm = min(TOKEN_TILE, seq)

    def body(da_ref, w_ref, h_ref, dr_ref, g_ref, dx_ref, dg_ref):
        @pl.when(pl.program_id(0) == 0)
        def _():
            dg_ref[...] = jnp.zeros_like(dg_ref)

        dx, dg = _rms_bwd(_nt(da_ref[...], w_ref[...]), h_ref[...], g_ref[...], dr_ref[...])
        dx_ref[...] = dx
        dg_ref[...] += dg

    return _call(
        body, name=name, grid=(seq // tm,),
        in_specs=[_rows(tm, n), _full((D_MODEL, n)), _rows(tm, D_MODEL), _rows(tm, D_MODEL), _full((1, D_MODEL))],
        out_specs=[_rows(tm, D_MODEL), _full((1, D_MODEL))],
        out_shape=[jax.ShapeDtypeStruct((seq, D_MODEL), F32), jax.ShapeDtypeStruct((1, D_MODEL), F32)],
        sem=("arbitrary",))(dact, w, h_in, dres, g)


def _nt_bias(name, dy, w):
    seq, n = dy.shape
    kdim = w.shape[0]
    tm = min(TOKEN_TILE, seq)

    def body(dy_ref, w_ref, o_ref, db_ref):
        @pl.when(pl.program_id(0) == 0)
        def _():
            db_ref[...] = jnp.zeros_like(db_ref)

        dyv = dy_ref[...]
        db_ref[...] += _sum_rows(dyv)
        o_ref[...] = _nt(dyv.astype(BF16), w_ref[...]).astype(BF16)

    return _call(
        body, name=name, grid=(seq // tm,),
        in_specs=[_rows(tm, n), _full((kdim, n))],
        out_specs=[_rows(tm, kdim), _full((1, n))],
        out_shape=[jax.ShapeDtypeStruct((seq, kdim), BF16), jax.ShapeDtypeStruct((1, n), F32)],
        sem=("arbitrary",))(dy, w)


def _attn_bwd(sinks, q, k, v, do):
    seq = q.shape[0]
    cur, prev = _kv_specs()

    def body(sink_ref, q_ref, kc_ref, kp_ref, vc_ref, vp_ref, do_ref,
             dq_ref, dkc_ref, dkp_ref, dvc_ref, dvp_ref, dsink_ref):
        n = pl.program_id(0)

        @pl.when(n == 0)
        def _():
            dsink_ref[...] = jnp.zeros_like(dsink_ref)

        mask = _band_mask(n)
        for j in range(N_KV_HEADS):
            cs = slice(j * HEAD_DIM, (j + 1) * HEAD_DIM)
            kk = jnp.concatenate([kp_ref[:, cs], kc_ref[:, cs]], axis=0)
            vv = jnp.concatenate([vp_ref[:, cs], vc_ref[:, cs]], axis=0)
            dkk = jnp.zeros((2 * BLOCK, HEAD_DIM), F32)
            dvv = jnp.zeros((2 * BLOCK, HEAD_DIM), F32)
            for gq in range(Q_PER_KV):
                h = j * Q_PER_KV + gq
                hs = slice(h * HEAD_DIM, (h + 1) * HEAD_DIM)
                qh = q_ref[:, hs]
                doh = do_ref[:, hs]
                probs, p_sink = _softmax_with_sink(_nt(qh, kk), mask, sink_ref[h])
                dp = _nt(doh, vv)
                delta = jnp.sum(probs * dp, axis=-1, keepdims=True)
                dsb = (probs * (dp - delta)).astype(BF16)
                dq_ref[:, hs] = jnp.dot(dsb, kk, preferred_element_type=F32) * (HEAD_DIM ** -0.5)
                dkk = dkk + _tn(dsb, qh)
                dvv = dvv + _tn(probs.astype(BF16), doh)
                tot = _sum_rows(p_sink * delta)
                dsink_ref[h:h + 1, :] += jnp.broadcast_to(tot, (1, LANES))
            dkp_ref[:, cs] = dkk[:BLOCK]
            dkc_ref[:, cs] = dkk[BLOCK:]
            dvp_ref[:, cs] = dvv[:BLOCK]
            dvc_ref[:, cs] = dvv[BLOCK:]

    kv_out = _rows(BLOCK, KV_DIM)
    kv_shape = jax.ShapeDtypeStruct((seq, KV_DIM), F32)
    return _call(
        body, name="attn_bwd", grid=(seq // BLOCK,),
        in_specs=[pl.BlockSpec(memory_space=pltpu.SMEM), _rows(BLOCK, Q_DIM), cur, prev, cur, prev,
                  _rows(BLOCK, Q_DIM)],
        out_specs=[_rows(BLOCK, Q_DIM), kv_out, kv_out, kv_out, kv_out, _full((N_Q_HEADS, LANES))],
        out_shape=[jax.ShapeDtypeStruct((seq, Q_DIM), F32), kv_shape, kv_shape, kv_shape, kv_shape,
                   jax.ShapeDtypeStruct((N_Q_HEADS, LANES), F32)],
        sem=("arbitrary",))(sinks, q, k, k, v, v, do)


def _rope_bwd(dq, dkc, dkp, dvc, dvp, rc, rsa, rsb):
    seq = dq.shape[0]
    nb = seq // BLOCK
    nxt = pl.BlockSpec((BLOCK, KV_DIM), lambda n: (jnp.minimum(n + 1, nb - 1), 0))

    def body(dq_ref, dkc_ref, dkp_ref, dvc_ref, dvp_ref, c_ref, sa_ref, sb_ref, o_ref, db_ref):
        n = pl.program_id(0)

        @pl.when(n == 0)
        def _():
            db_ref[...] = jnp.zeros_like(db_ref)

        c, sa, sb = c_ref[...], sa_ref[...], sb_ref[...]
        last = n == nb - 1
        dk = dkc_ref[...] + jnp.where(last, 0.0, dkp_ref[...])
        dv = dvc_ref[...] + jnp.where(last, 0.0, dvp_ref[...])
        for i in range(Q_DIM // LANES):
            ls = slice(i * LANES, (i + 1) * LANES)
            blk = _rope_t(dq_ref[:, ls], c, sa, sb)
            o_ref[:, ls] = blk.astype(BF16)
            db_ref[:, ls] += _sum_rows(blk)
        dkr = _rope_t(dk, c, sa, sb)
        o_ref[:, Q_DIM:Q_DIM + KV_DIM] = dkr.astype(BF16)
        db_ref[:, Q_DIM:Q_DIM + KV_DIM] += _sum_rows(dkr)
        o_ref[:, Q_DIM + KV_DIM:] = dv.astype(BF16)
        db_ref[:, Q_DIM + KV_DIM:] += _sum_rows(dv)

    kv = _rows(BLOCK, KV_DIM)
    tab = _rows(BLOCK, LANES)
    return _call(
        body, name="rope_bwd", grid=(nb,),
        in_specs=[_rows(BLOCK, Q_DIM), kv, nxt, kv, nxt, tab, tab, tab],
        out_specs=[_rows(BLOCK, QKV_DIM), _full((1, QKV_DIM))],
        out_shape=[jax.ShapeDtypeStruct((seq, QKV_DIM), BF16), jax.ShapeDtypeStruct((1, QKV_DIM), F32)],
        sem=("arbitrary",))(dq, dkc, dkp, dvc, dvp, rc, rsa, rsb)


def _adamw(w, g, m, v):
    m = ADAM_B1 * m + (1.0 - ADAM_B1) * g
    v = ADAM_B2 * v + (1.0 - ADAM_B2) * (g * g)
    m_hat = m / (1.0 - ADAM_B1 ** ADAM_STEP)
    v_hat = v / (1.0 - ADAM_B2 ** ADAM_STEP)
    delta = -ADAM_LR * (m_hat / (jnp.sqrt(v_hat) + ADAM_EPS) + ADAM_WD * w)
    return delta, m, v


def _adamw_sharded(own, recv, w, m, v):
    rows = w.shape[0]
    tr = ADAM_ROW_TILE

    def body(own_ref, recv_ref, w_ref, m_ref, v_ref, g_ref, d_ref, nm_ref, nv_ref):
        g = own_ref[...].astype(F32)
        for j in range(3):
            g = g + recv_ref[j].astype(F32)
        g_ref[...] = g
        d_ref[...], nm_ref[...], nv_ref[...] = _adamw(w_ref[...], g, m_ref[...], v_ref[...])

    spec = _rows(tr, D_MODEL)
    shape = jax.ShapeDtypeStruct((rows, D_MODEL), F32)
    return _call(
        body, name="adamw_sharded", grid=(rows // tr,),
        in_specs=[spec, pl.BlockSpec((3, tr, D_MODEL), lambda i: (0, i, 0)), spec, spec, spec],
        out_specs=[spec, spec, spec, spec], out_shape=[shape, shape, shape, shape],
        sem=("parallel",))(own, recv, w, m, v)


def _adamw_replicated(parts, w, m, v):
    def body(p_ref, w_ref, m_ref, v_ref, g_ref, d_ref, nm_ref, nv_ref):
        g = p_ref[0]
        for j in range(1, N_DEV):
            g = g + p_ref[j]
        g_ref[...] = g
        d_ref[...], nm_ref[...], nv_ref[...] = _adamw(w_ref[...], g, m_ref[...], v_ref[...])

    spec = _full((REPL_ROWS, D_MODEL))
    shape = jax.ShapeDtypeStruct((REPL_ROWS, D_MODEL), F32)
    return _call(
        body, name="adamw_replicated", grid=(1,),
        in_specs=[_full((N_DEV, REPL_ROWS, D_MODEL)), spec, spec, spec],
        out_specs=[spec, spec, spec, spec], out_shape=[shape, shape, shape, shape],
        sem=("arbitrary",))(parts, w, m, v)


def _add_pair(keep, recv):
    n, rows, cols = keep.shape
    tr = ADAM_ROW_TILE

    def body(a_ref, b_ref, o_ref):
        o_ref[...] = (a_ref[...].astype(F32) + b_ref[...].astype(F32)).astype(BF16)

    spec = pl.BlockSpec((1, tr, cols), lambda c, i: (c, i, 0))
    return _call(
        body, name="add_pair", grid=(n, rows // tr), in_specs=[spec, spec], out_specs=spec,
        out_shape=jax.ShapeDtypeStruct((n, rows, cols), BF16), sem=("parallel", "parallel"))(keep, recv)


_MESH = pl.DeviceIdType.MESH
_ANY = pl.BlockSpec(memory_space=pl.ANY)


def _all_gather(name, xs):
    rows, cols = xs.shape

    def body(x_ref, out_ref, send_sems, recv_sems, local_sem):
        x, y, c = lax.axis_index("x"), lax.axis_index("y"), lax.axis_index("c")
        me, sibling = (x, y, c), (x, y, 1 - c)
        chips = [(1 - x, y), (x, 1 - y), (1 - x, 1 - y)]

        def slot(px, py, pc):
            return out_ref.at[4 * px + 2 * py + pc]

        def copy(k, block, to, src=None):
            return pltpu.make_async_remote_copy(
                src_ref=slot(*block) if src is None else src, dst_ref=slot(*block),
                send_sem=send_sems.at[k], recv_sem=recv_sems.at[k], device_id=to, device_id_type=_MESH)

        mine = pltpu.make_async_copy(x_ref, slot(*me), local_sem)
        mine.start()
        first = [copy(0, me, sibling, src=x_ref)]
        first += [copy(1 + j, me, (*chip, c), src=x_ref) for j, chip in enumerate(chips)]
        for cp in first:
            cp.start()
        passed = [copy(4 + j, (*chip, c), sibling) for j, chip in enumerate(chips)]
        for j, chip in enumerate(chips):
            copy(1 + j, (*chip, c), me).wait_recv()
            passed[j].start()
        copy(0, sibling, me).wait_recv()
        for j, chip in enumerate(chips):
            copy(4 + j, (*chip, 1 - c), me).wait_recv()
        for cp in first + passed:
            cp.wait_send()
        mine.wait()

    return pl.pallas_call(
        body, name=name, out_shape=jax.ShapeDtypeStruct((N_DEV, rows, cols), xs.dtype),
        in_specs=[_ANY], out_specs=_ANY,
        scratch_shapes=[pltpu.SemaphoreType.DMA((7,)), pltpu.SemaphoreType.DMA((7,)), pltpu.SemaphoreType.DMA],
    )(xs)


def _swap_with_sibling(send):
    def body(s_ref, out_ref, send_sem, recv_sem):
        x, y, c = lax.axis_index("x"), lax.axis_index("y"), lax.axis_index("c")
        cp = pltpu.make_async_remote_copy(src_ref=s_ref, dst_ref=out_ref, send_sem=send_sem, recv_sem=recv_sem,
                                          device_id=(x, y, 1 - c), device_id_type=_MESH)
        cp.start()
        cp.wait()

    return pl.pallas_call(
        body, name="rs_pair", out_shape=jax.ShapeDtypeStruct(send.shape, send.dtype),
        in_specs=[_ANY], out_specs=_ANY,
        scratch_shapes=[pltpu.SemaphoreType.DMA, pltpu.SemaphoreType.DMA],
    )(send)


def _exchange_chips(p):
    _, rows, cols = p.shape

    def body(p_ref, out_ref, send_sems, recv_sems):
        x, y, c = lax.axis_index("x"), lax.axis_index("y"), lax.axis_index("c")
        chips = [(1 - x, y), (x, 1 - y), (1 - x, 1 - y)]
        cps = [pltpu.make_async_remote_copy(
            src_ref=p_ref.at[2 * cx + cy], dst_ref=out_ref.at[j], send_sem=send_sems.at[j],
            recv_sem=recv_sems.at[j], device_id=(cx, cy, c), device_id_type=_MESH)
            for j, (cx, cy) in enumerate(chips)]
        for cp in cps:
            cp.start()
        for cp in cps:
            cp.wait()

    return pl.pallas_call(
        body, name="rs_chips", out_shape=jax.ShapeDtypeStruct((3, rows, cols), p.dtype),
        in_specs=[_ANY], out_specs=_ANY,
        scratch_shapes=[pltpu.SemaphoreType.DMA((3,)), pltpu.SemaphoreType.DMA((3,))],
    )(p)


def _pad_rows(flat, rows):
    return jnp.pad(flat, (0, rows * D_MODEL - flat.shape[0])).reshape(rows, D_MODEL)


def _pack_shard(w_qkv, w_o, w_pw1, b_pw1, w_dw, b_dw, ln_g, ln_b, w_pw2, b_pw2, w1, w3, w2):
    small = jnp.concatenate([b_pw1.reshape(-1), w_dw.reshape(-1), b_dw.reshape(-1), ln_g.reshape(-1),
                             ln_b.reshape(-1), b_pw2.reshape(-1)])
    parts = [w_qkv.reshape(ROWS_QKV, D_MODEL), w_o.reshape(ROWS_WO, D_MODEL), w_pw1.reshape(ROWS_PW1, D_MODEL),
             w_pw2.reshape(ROWS_PW2, D_MODEL), w1.reshape(ROWS_FFN, D_MODEL), w3.reshape(ROWS_FFN, D_MODEL),
             w2.reshape(ROWS_FFN, D_MODEL), _pad_rows(small, ROWS_SMALL)]
    return jnp.concatenate(parts, axis=0)


def _unpack_shard(p):
    o = 0
    out = []
    for rows, shape in ((ROWS_QKV, (1, D_MODEL, QKV_DIM // N_DEV)), (ROWS_WO, (1, Q_DIM // N_DEV, D_MODEL)),
                        (ROWS_PW1, (1, D_MODEL, 2 * D_MODEL // N_DEV)), (ROWS_PW2, (1, D_MODEL // N_DEV, D_MODEL)),
                        (ROWS_FFN, (2, D_MODEL, D_FF // N_DEV)), (ROWS_FFN, (2, D_MODEL, D_FF // N_DEV)),
                        (ROWS_FFN, (2, D_FF // N_DEV, D_MODEL))):
        out.append(p[o:o + rows].reshape(shape))
        o += rows
    w_qkv, w_o, w_pw1, w_pw2, w1, w3, w2 = out
    small = p[o:].reshape(-1)
    c = D_MODEL // N_DEV
    b_pw1 = small[:2 * c].reshape(1, 2 * c)
    o = 2 * c
    w_dw = small[o:o + CONV_WIDTH * c].reshape(1, CONV_WIDTH, c)
    o += CONV_WIDTH * c
    b_dw, ln_g, ln_b, b_pw2 = (small[o + i * c:o + (i + 1) * c].reshape(1, c) for i in range(4))
    return dict(attn_w_qkv=w_qkv, attn_w_o=w_o, conv_w_pw1=w_pw1, conv_b_pw1=b_pw1, conv_w_dw=w_dw,
                conv_b_dw=b_dw, conv_ln_g=ln_g, conv_ln_b=ln_b, conv_w_pw2=w_pw2, conv_b_pw2=b_pw2,
                ffn_w1=w1, ffn_w3=w3, ffn_w2=w2)


def _gather_payload(packed):
    big = packed[:ROWS_BIG].astype(BF16)
    small = packed[ROWS_BIG:].reshape(-1)[:ROWS_SMALL * D_MODEL // 2]
    small = lax.bitcast_convert_type(small, BF16).reshape(ROWS_SMALL, D_MODEL)
    return jnp.concatenate([big, small], axis=0)


def _unpack_gathered(full):
    def cols(lo, rows, ncol):
        blk = full[:, lo:lo + rows].reshape(N_DEV, D_MODEL, ncol)
        return blk.transpose(1, 0, 2).reshape(D_MODEL, N_DEV * ncol)

    o = 0
    w_qkv = cols(o, ROWS_QKV, QKV_DIM // N_DEV)
    o += ROWS_QKV
    w_o = full[:, o:o + ROWS_WO].reshape(Q_DIM, D_MODEL)
    o += ROWS_WO
    w_pw1 = cols(o, ROWS_PW1, 2 * D_MODEL // N_DEV)
    o += ROWS_PW1
    w_pw2 = full[:, o:o + ROWS_PW2].reshape(D_MODEL, D_MODEL)
    o += ROWS_PW2
    nf = D_FF // N_DEV
    w1 = full[:, o:o + ROWS_FFN].reshape(N_DEV, 2, D_MODEL, nf).transpose(1, 2, 0, 3).reshape(2, D_MODEL, D_FF)
    o += ROWS_FFN
    w3 = full[:, o:o + ROWS_FFN].reshape(N_DEV, 2, D_MODEL, nf).transpose(1, 2, 0, 3).reshape(2, D_MODEL, D_FF)
    o += ROWS_FFN
    w2 = full[:, o:o + ROWS_FFN].reshape(N_DEV, 2, nf, D_MODEL).transpose(1, 0, 2, 3).reshape(2, D_FF, D_MODEL)
    o += ROWS_FFN
    small = lax.bitcast_convert_type(full[:, o:].reshape(N_DEV, ROWS_SMALL * D_MODEL // 2, 2), F32)
    c = D_MODEL // N_DEV
    b_pw1 = small[:, :2 * c].reshape(1, 2 * D_MODEL)
    s = 2 * c
    w_dw = small[:, s:s + CONV_WIDTH * c].reshape(N_DEV, CONV_WIDTH, c).transpose(1, 0, 2).reshape(CONV_WIDTH, D_MODEL)
    s += CONV_WIDTH * c
    b_dw, ln_g, ln_b, b_pw2 = (small[:, s + i * c:s + (i + 1) * c].reshape(1, D_MODEL) for i in range(4))
    return dict(w_qkv=w_qkv, w_o=w_o, w_pw1=w_pw1, b_pw1=b_pw1, w_dw=w_dw, b_dw=b_dw, ln_g=ln_g, ln_b=ln_b,
                w_pw2=w_pw2, b_pw2=b_pw2, w1=w1, w3=w3, w2=w2)


def _pack_grads(g_qkv, g_o, g_pw1, g_bpw1, g_dw, g_bdw, g_lng, g_lnb, g_pw2, g_bpw2, g1, g3, g2):
    def cols(g, ncol):
        return g.reshape(D_MODEL, N_DEV, ncol).transpose(1, 0, 2).reshape(N_DEV, -1, D_MODEL)

    nf = D_FF // N_DEV
    c = D_MODEL // N_DEV
    small = jnp.concatenate(
        [g_bpw1.reshape(N_DEV, 2 * c), g_dw.reshape(CONV_WIDTH, N_DEV, c).transpose(1, 0, 2).reshape(N_DEV, -1),
         g_bdw.reshape(N_DEV, c), g_lng.reshape(N_DEV, c), g_lnb.reshape(N_DEV, c), g_bpw2.reshape(N_DEV, c)], axis=1)
    small = jnp.pad(small, ((0, 0), (0, ROWS_SMALL * D_MODEL - SMALL_USED))).reshape(N_DEV, ROWS_SMALL, D_MODEL)
    parts = [cols(g_qkv, QKV_DIM // N_DEV), g_o.reshape(N_DEV, ROWS_WO, D_MODEL), cols(g_pw1, 2 * c),
             g_pw2.reshape(N_DEV, ROWS_PW2, D_MODEL),
             g1.reshape(2, D_MODEL, N_DEV, nf).transpose(2, 0, 1, 3).reshape(N_DEV, ROWS_FFN, D_MODEL),
             g3.reshape(2, D_MODEL, N_DEV, nf).transpose(2, 0, 1, 3).reshape(N_DEV, ROWS_FFN, D_MODEL),
             g2.reshape(2, N_DEV, nf, D_MODEL).transpose(1, 0, 2, 3).reshape(N_DEV, ROWS_FFN, D_MODEL),
             small]
    return jnp.concatenate(parts, axis=1).astype(BF16)


def _pack_replicated(norm_mix, norm_ffn, b_qkv, sinks, b_o, norm_final):
    rows = [norm_mix.reshape(2, D_MODEL), norm_ffn.reshape(2, D_MODEL), _pad_rows(b_qkv.reshape(-1), 2),
            _pad_rows(sinks.reshape(-1), 1), b_o.reshape(1, D_MODEL), norm_final.reshape(1, D_MODEL)]
    p = jnp.concatenate(rows, axis=0)
    return jnp.pad(p, ((0, REPL_ROWS - p.shape[0]), (0, 0)))


def _unpack_replicated(p):
    return dict(norm_mix=p[0:2], norm_ffn=p[2:4], attn_b_qkv=p[4:6].reshape(-1)[:QKV_DIM].reshape(1, QKV_DIM),
                attn_sinks=p[6, :N_Q_HEADS].reshape(1, N_Q_HEADS), attn_b_o=p[7:8], norm_final=p[8])


WEIGHT_ORDER = ['norm_mix', 'norm_ffn', 'attn_w_qkv', 'attn_b_qkv', 'attn_sinks', 'attn_w_o', 'attn_b_o',
                'conv_w_pw1', 'conv_b_pw1', 'conv_w_dw', 'conv_b_dw', 'conv_ln_g', 'conv_ln_b', 'conv_w_pw2',
                'conv_b_pw2', 'ffn_w1', 'ffn_w3', 'ffn_w2', 'norm_final']


def kernel(x, norm_mix, norm_ffn, attn_w_qkv, attn_b_qkv, attn_sinks, attn_w_o, attn_b_o, conv_w_pw1, conv_b_pw1, conv_w_dw, conv_b_dw, conv_ln_g, conv_ln_b, conv_w_pw2, conv_b_pw2, ffn_w1, ffn_w3, ffn_w2, norm_final, loss_target, m_norm_mix, m_norm_ffn, m_attn_w_qkv, m_attn_b_qkv, m_attn_sinks, m_attn_w_o, m_attn_b_o, m_conv_w_pw1, m_conv_b_pw1, m_conv_w_dw, m_conv_b_dw, m_conv_ln_g, m_conv_ln_b, m_conv_w_pw2, m_conv_b_pw2, m_ffn_w1, m_ffn_w3, m_ffn_w2, m_norm_final, v_norm_mix, v_norm_ffn, v_attn_w_qkv, v_attn_b_qkv, v_attn_sinks, v_attn_w_o, v_attn_b_o, v_conv_w_pw1, v_conv_b_pw1, v_conv_w_dw, v_conv_b_dw, v_conv_ln_g, v_conv_ln_b, v_conv_w_pw2, v_conv_b_pw2, v_ffn_w1, v_ffn_w3, v_ffn_w2, v_norm_final):
    xs = x[0]
    target = loss_target[0]
    seq = xs.shape[0]
    my_x, my_y, my_c = lax.axis_index("x"), lax.axis_index("y"), lax.axis_index("c")

    w_packed = _pack_shard(attn_w_qkv, attn_w_o, conv_w_pw1, conv_b_pw1, conv_w_dw, conv_b_dw, conv_ln_g,
                           conv_ln_b, conv_w_pw2, conv_b_pw2, ffn_w1, ffn_w3, ffn_w2)
    wt = _unpack_gathered(_all_gather("ag_weights", _gather_payload(w_packed)))
    wd = jnp.concatenate([wt["w_dw"][::-1], jnp.zeros((TAPS_PAD - CONV_WIDTH, D_MODEL), F32)], axis=0)
    rc, rsa, rsb = _rope_tables(seq)
    sinks = attn_sinks.reshape(N_Q_HEADS)
    g_mix0, g_mix1 = norm_mix[0:1], norm_mix[1:2]
    g_ffn0, g_ffn1 = norm_ffn[0:1], norm_ffn[1:2]
    g_fin = norm_final.reshape(1, D_MODEL)

    y0, q, k, v = _qkv_fwd(xs, g_mix0, wt["w_qkv"], attn_b_qkv, rc, rsa, rsb)
    attn = _attn_fwd(sinks, q, k, v)
    h1 = _mm_res("attn_out_proj", attn, wt["w_o"], attn_b_o, xs)
    f0, u0, p0, h2 = _ffn_fwd("ffn0_fwd", h1, g_ffn0, wt["w1"][0], wt["w3"][0], wt["w2"][0])
    y1, a, dwc, z = _conv_fwd(h2, g_mix1, wt["w_pw1"], wt["b_pw1"], wd, wt["b_dw"], wt["ln_g"], wt["ln_b"])
    h3 = _mm_res("conv_out_proj", z, wt["w_pw2"], wt["b_pw2"], h2)
    f1, u1, p1, h4 = _ffn_fwd("ffn1_fwd", h3, g_ffn1, wt["w1"][1], wt["w3"][1], wt["w2"][1])
    dh4, sq, dg_fin = _final_loss(h4, target, g_fin)
    loss = lax.psum(sq[0, 0] * (0.5 / D_MODEL), ("x", "y", "c"))

    half_ff = D_FF // 2
    du1, dp1, s1, dh3, dg_ffn1 =_ffn_bwd("ffn1_bwd", dh4, h3, g_ffn1, u1, p1, wt["w1"][1], wt["w3"][1], wt["w2"][1])
    gw2_1 = _mm_tn("ffn1_dw2", s1, dh4, tk=half_ff, tn=D_MODEL)
    gw1_1 = _mm_tn("ffn1_dw1", f1, du1, tk=D_MODEL, tn=half_ff)
    gw3_1 = _mm_tn("ffn1_dw3", f1, dp1, tk=D_MODEL, tn=half_ff)

    da, dlg, dlb, dbdw, dwd, dbpw1, dbpw2 = _conv_bwd(dh3, wt["w_pw2"], dwc, a, wt["ln_g"], wt["ln_b"], wd)
    gpw2 = _mm_tn("conv_dw_pw2", z, dh3, tk=D_MODEL, tn=D_MODEL)
    dh2, dg_mix1 = _nt_rms_bwd("conv_in_bwd", da, wt["w_pw1"], h2, dh3, g_mix1)
    gpw1 = _mm_tn("conv_dw_pw1", y1, da, tk=D_MODEL, tn=D_MODEL)

    du0, dp0, s0, dh1, dg_ffn0 =_ffn_bwd("ffn0_bwd", dh2, h1, g_ffn0, u0, p0, wt["w1"][0], wt["w3"][0], wt["w2"][0])
    gw2_0 = _mm_tn("ffn0_dw2", s0, dh2, tk=half_ff, tn=D_MODEL)
    gw1_0 = _mm_tn("ffn0_dw1", f0, du0, tk=D_MODEL, tn=half_ff)
    gw3_0 = _mm_tn("ffn0_dw3", f0, dp0, tk=D_MODEL, tn=half_ff)

    dattn, dbo = _nt_bias("attn_out_bwd", dh1, wt["w_o"])
    gwo = _mm_tn("attn_dw_o", attn, dh1, tk=D_MODEL, tn=D_MODEL)
    dq, dkc, dkp, dvc, dvp, dsink = _attn_bwd(sinks, q, k, v, dattn)
    dqkv, dbqkv = _rope_bwd(dq, dkc, dkp, dvc, dvp, rc, rsa, rsb)
    dx, dg_mix0 = _nt_rms_bwd("qkv_in_bwd", dqkv, wt["w_qkv"], xs, dh1, g_mix0)
    gqkv = _mm_tn("attn_dw_qkv", y0, dqkv, tk=D_MODEL, tn=QKV_DIM)

    g_dw = dwd[:CONV_WIDTH][::-1]
    blocks = _pack_grads(gqkv, gwo, gpw1, dbpw1, g_dw, dbdw, dlg, dlb, gpw2, dbpw2,
                         jnp.stack([gw1_0, gw1_1]), jnp.stack([gw3_0, gw3_1]), jnp.stack([gw2_0, gw2_1]))
    blocks = blocks.reshape(4, 2, ROWS_ALL, D_MODEL)
    keep = lax.dynamic_index_in_dim(blocks, my_c, axis=1, keepdims=False)
    send = lax.dynamic_index_in_dim(blocks, 1 - my_c, axis=1, keepdims=False)
    pair = _add_pair(keep, _swap_with_sibling(send))
    recv = _exchange_chips(pair)
    own = lax.dynamic_index_in_dim(pair, 2 * my_x + my_y, axis=0, keepdims=False)

    w_sh = w_packed
    m_sh = _pack_shard(m_attn_w_qkv, m_attn_w_o, m_conv_w_pw1, m_conv_b_pw1, m_conv_w_dw, m_conv_b_dw,
                       m_conv_ln_g, m_conv_ln_b, m_conv_w_pw2, m_conv_b_pw2, m_ffn_w1, m_ffn_w3, m_ffn_w2)
    v_sh = _pack_shard(v_attn_w_qkv, v_attn_w_o, v_conv_w_pw1, v_conv_b_pw1, v_conv_w_dw, v_conv_b_dw,
                       v_conv_ln_g, v_conv_ln_b, v_conv_w_pw2, v_conv_b_pw2, v_ffn_w1, v_ffn_w3, v_ffn_w2)
    sharded = [_unpack_shard(t) for t in _adamw_sharded(own, recv, w_sh, m_sh, v_sh)]

    part = _pack_replicated(jnp.concatenate([dg_mix0, dg_mix1]), jnp.concatenate([dg_ffn0, dg_ffn1]),
                            dbqkv, -dsink[:, 0], dbo, dg_fin)
    parts = _all_gather("ag_replicated_grads", part)
    w_rep = _pack_replicated(norm_mix, norm_ffn, attn_b_qkv, attn_sinks, attn_b_o, norm_final)
    m_rep = _pack_replicated(m_norm_mix, m_norm_ffn, m_attn_b_qkv, m_attn_sinks, m_attn_b_o, m_norm_final)
    v_rep = _pack_replicated(v_norm_mix, v_norm_ffn, v_attn_b_qkv, v_attn_sinks, v_attn_b_o, v_norm_final)
    replicated = [_unpack_replicated(t) for t in _adamw_replicated(parts, w_rep, m_rep, v_rep)]

    outs = [loss, dx.reshape(1, seq, D_MODEL)]
    for sh, rp in zip(sharded, replicated):
        merged = {**sh, **rp}
        outs += [merged[n] for n in WEIGHT_ORDER]
    return tuple(outs)
```

```python
import jax
import jax.numpy as jnp
from jax import lax
from jax.experimental import pallas as pl
from jax.experimental.pallas import tpu as pltpu

F32 = jnp.float32
BF16 = jnp.bfloat16

D_MODEL = 1024
HEAD_DIM = 64
N_Q_HEADS = 16
N_KV_HEADS = 2
Q_PER_KV = 8
Q_DIM = N_Q_HEADS * HEAD_DIM
KV_DIM = N_KV_HEADS * HEAD_DIM
QKV_DIM = Q_DIM + 2 * KV_DIM
BLOCK = 128
CONV_WIDTH = 31
D_FF = 2816
ROPE_THETA = 10000.0
RMS_EPS = 1e-5
LN_EPS = 1e-5
ADAM_LR = 0.001
ADAM_B1 = 0.9
ADAM_B2 = 0.999
ADAM_EPS = 1e-08
ADAM_WD = 0.01
ADAM_STEP = 10
N_DEV = 8

LANES = 128
SUBLANES = 8
TOKEN_TILE = 512
CONV_CHUNK = 64
CONV_HALO = 32
TAPS_PAD = 32
VMEM_LIMIT = 56 * 1024 * 1024
NEG_INF = float(jnp.finfo(jnp.float32).min)

ROWS_FF = D_FF // N_DEV
FF_SLOTS = 6
ROWS_QKV = QKV_DIM // N_DEV
ROWS_WO = Q_DIM // N_DEV
ROWS_PW1 = 2 * D_MODEL // N_DEV
ROWS_PW2 = D_MODEL // N_DEV
OFF_QKV = FF_SLOTS * ROWS_FF
OFF_WO = OFF_QKV + ROWS_QKV
OFF_PW1 = OFF_WO + ROWS_WO
OFF_PW2 = OFF_PW1 + ROWS_PW1
ROWS_BIG = OFF_PW2 + ROWS_PW2
ROWS_SMALL = 16
ROWS_ALL = ROWS_BIG + ROWS_SMALL
SMALL_USED = 2 * D_MODEL // N_DEV + CONV_WIDTH * (D_MODEL // N_DEV) + 4 * (D_MODEL // N_DEV)
FF_SPLIT = 2
FF_TILE_DEVS = N_DEV // FF_SPLIT
FF_TILE = FF_TILE_DEVS * ROWS_FF
ROW_TILE = 400
REPL_ROWS = 16


def _call(body, *, name, grid, in_specs, out_specs, out_shape, scratch=(), sem=None):
    return pl.pallas_call(
        body, name=name, grid=grid, in_specs=in_specs, out_specs=out_specs, out_shape=out_shape,
        scratch_shapes=list(scratch),
        compiler_params=pltpu.CompilerParams(dimension_semantics=sem, vmem_limit_bytes=VMEM_LIMIT))


def _full(shape):
    return pl.BlockSpec(shape, lambda *_: (0,) * len(shape))


def _rows(tm, n):
    return pl.BlockSpec((tm, n), lambda i, *_: (i, 0))


def _sig(x):
    return 1.0 / (1.0 + jnp.exp(-x))


def _sum_rows(x):
    return jnp.sum(x, axis=0, keepdims=True)


def _nt(a, b):
    return lax.dot_general(a, b, (((1,), (1,)), ((), ())), preferred_element_type=F32)


def _tn(a, b):
    return lax.dot_general(a, b, (((0,), (0,)), ((), ())), preferred_element_type=F32)


def _rms_stats(x):
    return lax.rsqrt(jnp.mean(x * x, axis=-1, keepdims=True) + RMS_EPS)


def _rms_bwd(dy, x, g, dres):
    r = _rms_stats(x)
    n = x * r
    dn = dy * g
    dx = dres + r * (dn - n * jnp.mean(dn * n, axis=-1, keepdims=True))
    return dx, _sum_rows(dy * n)


def _rope_tables(seq):
    pos = jnp.arange(seq, dtype=F32)
    inv_freq = ROPE_THETA ** (-jnp.arange(0, HEAD_DIM, 2, dtype=F32) / HEAD_DIM)
    ang = pos[:, None] * inv_freq[None, :]
    cos, sin = jnp.cos(ang), jnp.sin(ang)
    zero = jnp.zeros_like(sin)
    reps = LANES // HEAD_DIM
    c = jnp.tile(jnp.concatenate([cos, cos], axis=1), (1, reps))
    sa = jnp.tile(jnp.concatenate([-sin, zero], axis=1), (1, reps))
    sb = jnp.tile(jnp.concatenate([zero, sin], axis=1), (1, reps))
    return c, sa, sb


def _rope(t, c, sa, sb):
    half = HEAD_DIM // 2
    return t * c + pltpu.roll(t, LANES - half, 1) * sa + pltpu.roll(t, half, 1) * sb


def _rope_t(dt, c, sa, sb):
    half = HEAD_DIM // 2
    return dt * c + pltpu.roll(dt * sa, half, 1) + pltpu.roll(dt * sb, LANES - half, 1)


def _qkv_fwd(x, g, w, b, rc, rsa, rsb):
    seq = x.shape[0]
    tm = min(TOKEN_TILE, seq)

    def body(x_ref, g_ref, w_ref, b_ref, c_ref, sa_ref, sb_ref, y_ref, q_ref, k_ref, v_ref):
        xv = x_ref[...]
        y = (xv * _rms_stats(xv) * g_ref[...]).astype(BF16)
        y_ref[...] = y
        qkv = _nt(y, w_ref[...]) + b_ref[...]
        c, sa, sb = c_ref[...], sa_ref[...], sb_ref[...]
        for i in range(Q_DIM // LANES):
            blk = _rope(qkv[:, i * LANES:(i + 1) * LANES], c, sa, sb)
            q_ref[:, i * LANES:(i + 1) * LANES] = (blk * (HEAD_DIM ** -0.5)).astype(BF16)
        k_ref[...] = _rope(qkv[:, Q_DIM:Q_DIM + KV_DIM], c, sa, sb).astype(BF16)
        v_ref[...] = qkv[:, Q_DIM + KV_DIM:].astype(BF16)

    return _call(
        body, name="qkv_fwd", grid=(seq // tm,),
        in_specs=[_rows(tm, D_MODEL), _full((1, D_MODEL)), _full((QKV_DIM, D_MODEL)), _full((1, QKV_DIM)),
                  _rows(tm, LANES), _rows(tm, LANES), _rows(tm, LANES)],
        out_specs=[_rows(tm, D_MODEL), _rows(tm, Q_DIM), _rows(tm, KV_DIM), _rows(tm, KV_DIM)],
        out_shape=[jax.ShapeDtypeStruct((seq, D_MODEL), BF16), jax.ShapeDtypeStruct((seq, Q_DIM), BF16),
                   jax.ShapeDtypeStruct((seq, KV_DIM), BF16), jax.ShapeDtypeStruct((seq, KV_DIM), BF16)],
        sem=("parallel",))(x, g, w, b, rc, rsa, rsb)


GROUP_ROWS = Q_PER_KV * BLOCK


def _band_mask(n):
    row = lax.broadcasted_iota(jnp.int32, (GROUP_ROWS, 2 * BLOCK), 0) & (BLOCK - 1)
    col = lax.broadcasted_iota(jnp.int32, (GROUP_ROWS, 2 * BLOCK), 1)
    rel = row + BLOCK - col
    return (rel >= 0) & (rel < BLOCK) & ((col >= BLOCK) | (n > 0))


def _softmax_with_sink(s, mask, sink):
    s = jnp.where(mask, s, NEG_INF)
    m = jnp.maximum(jnp.max(s, axis=-1, keepdims=True), sink)
    p = jnp.exp(s - m)
    e_sink = jnp.exp(sink - m)
    inv = 1.0 / (jnp.sum(p, axis=-1, keepdims=True) + e_sink)
    return p * inv, e_sink * inv


def _head_slice(j, gq):
    h = j * Q_PER_KV + gq
    return slice(h * HEAD_DIM, (h + 1) * HEAD_DIM)


def _stack_heads(ref, j):
    return jnp.concatenate([ref[:, _head_slice(j, gq)] for gq in range(Q_PER_KV)], axis=0)


def _kv_pair(prev_ref, cur_ref, j):
    cs = slice(j * HEAD_DIM, (j + 1) * HEAD_DIM)
    return jnp.concatenate([prev_ref[:, cs], cur_ref[:, cs]], axis=0)


def _kv_specs():
    cur = pl.BlockSpec((BLOCK, KV_DIM), lambda n: (n, 0))
    prev = pl.BlockSpec((BLOCK, KV_DIM), lambda n: (jnp.maximum(n - 1, 0), 0))
    return cur, prev


def _sink_rows(sinks):
    return jnp.repeat(sinks.reshape(N_KV_HEADS, Q_PER_KV), BLOCK, axis=1).reshape(N_KV_HEADS, GROUP_ROWS, 1)


def _attn_fwd(sink_rows, q, k, v):
    seq = q.shape[0]
    cur, prev = _kv_specs()

    def body(sink_ref, q_ref, kc_ref, kp_ref, vc_ref, vp_ref, o_ref):
        mask = _band_mask(pl.program_id(0))
        for j in range(N_KV_HEADS):
            kk = _kv_pair(kp_ref, kc_ref, j)
            vv = _kv_pair(vp_ref, vc_ref, j)
            probs, _ = _softmax_with_sink(_nt(_stack_heads(q_ref, j), kk), mask, sink_ref[j])
            o = jnp.dot(probs.astype(BF16), vv, preferred_element_type=F32).astype(BF16)
            for gq in range(Q_PER_KV):
                o_ref[:, _head_slice(j, gq)] = o[gq * BLOCK:(gq + 1) * BLOCK]

    return _call(
        body, name="attn_fwd", grid=(seq // BLOCK,),
        in_specs=[_full((N_KV_HEADS, GROUP_ROWS, 1)), _rows(BLOCK, Q_DIM), cur, prev, cur, prev],
        out_specs=_rows(BLOCK, Q_DIM), out_shape=jax.ShapeDtypeStruct((seq, Q_DIM), BF16),
        sem=("parallel",))(sink_rows, q, k, k, v, v)


def _mm_res(name, a, w, b, res):
    seq, kdim = a.shape
    n = w.shape[1]
    tm = min(TOKEN_TILE, seq)

    def body(a_ref, w_ref, b_ref, r_ref, o_ref):
        o_ref[...] = r_ref[...] + (jnp.dot(a_ref[...], w_ref[...], preferred_element_type=F32) + b_ref[...])

    return _call(
        body, name=name, grid=(seq // tm,),
        in_specs=[_rows(tm, kdim), _full((kdim, n)), _full((1, n)), _rows(tm, n)],
        out_specs=_rows(tm, n), out_shape=jax.ShapeDtypeStruct((seq, n), F32),
        sem=("parallel",))(a, w, b, res)


def _ff_tile_spec(slot):
    return pl.BlockSpec((FF_TILE_DEVS, ROWS_FF, D_MODEL), lambda i, j: (j, slot, 0))


def _ff_whole_spec(slot):
    return pl.BlockSpec((N_DEV, ROWS_FF, D_MODEL), lambda i: (0, slot, 0))


def _ffn_up(name, h, g, gathered, layer):
    seq = h.shape[0]
    tm = min(TOKEN_TILE, seq)

    def body(h_ref, g_ref, w1_ref, w3_ref, f_ref, u_ref, w_ref, s_ref):
        @pl.when(pl.program_id(1) == 0)
        def _():
            hv = h_ref[...]
            f_ref[...] = (hv * _rms_stats(hv) * g_ref[...]).astype(BF16)

        f = f_ref[...]
        u = _nt(f, w1_ref[...].reshape(FF_TILE, D_MODEL))
        w = _nt(f, w3_ref[...].reshape(FF_TILE, D_MODEL))
        u_ref[...] = u.astype(BF16)
        w_ref[...] = w.astype(BF16)
        s_ref[...] = (u * _sig(u) * w).astype(BF16)

    tile_ff = pl.BlockSpec((tm, FF_TILE), lambda i, j: (i, j))
    ff_shape = jax.ShapeDtypeStruct((seq, D_FF), BF16)
    return _call(
        body, name=name, grid=(seq // tm, FF_SPLIT),
        in_specs=[_rows(tm, D_MODEL), _full((1, D_MODEL)), _ff_tile_spec(layer), _ff_tile_spec(2 + layer)],
        out_specs=[_rows(tm, D_MODEL), tile_ff, tile_ff, tile_ff],
        out_shape=[jax.ShapeDtypeStruct((seq, D_MODEL), BF16), ff_shape, ff_shape, ff_shape],
        sem=("parallel", "arbitrary"))(h, g, gathered, gathered)


def _ffn_down(name, s, gathered, layer, res):
    seq = s.shape[0]
    tm = min(TOKEN_TILE, seq)

    def body(s_ref, w_ref, r_ref, o_ref):
        w2 = w_ref[...].reshape(D_FF, D_MODEL)
        o_ref[...] = r_ref[...] + jnp.dot(s_ref[...], w2, preferred_element_type=F32)

    return _call(
        body, name=name, grid=(seq // tm,),
        in_specs=[_rows(tm, D_FF), _ff_whole_spec(4 + layer), _rows(tm, D_MODEL)],
        out_specs=_rows(tm, D_MODEL), out_shape=jax.ShapeDtypeStruct((seq, D_MODEL), F32),
        sem=("parallel",))(s, gathered, res)


def _conv_fwd(h, g, wpw1, bpw1, wd, bdw, lng, lnb):
    seq = h.shape[0]
    tm = min(TOKEN_TILE, seq)
    n_chunks = tm // CONV_CHUNK
    win = CONV_CHUNK + CONV_HALO

    def body(h_ref, g_ref, w_ref, b_ref, wd_ref, bdw_ref, lng_ref, lnb_ref,
             y_ref, a_ref, dwc_ref, z_ref, gbuf):
        i = pl.program_id(0)

        @pl.when(i == 0)
        def _():
            gbuf[0:CONV_HALO, :] = jnp.zeros((CONV_HALO, D_MODEL), F32)

        @pl.when(i > 0)
        def _():
            gbuf[0:CONV_HALO, :] = gbuf[tm:tm + CONV_HALO, :]

        hv = h_ref[...]
        y = (hv * _rms_stats(hv) * g_ref[...]).astype(BF16)
        y_ref[...] = y
        a = _nt(y, w_ref[...]) + b_ref[...]
        a_ref[...] = a.astype(BF16)
        gbuf[CONV_HALO:CONV_HALO + tm, :] = a[:, :D_MODEL] * _sig(a[:, D_MODEL:])

        def chunk(c, carry):
            r0 = pl.multiple_of(c * CONV_CHUNK, CONV_CHUNK)
            for l in range(D_MODEL // LANES):
                ls = slice(l * LANES, (l + 1) * LANES)
                gw = gbuf[pl.ds(r0, win), ls]
                acc = jnp.zeros((CONV_CHUNK, LANES), F32) + bdw_ref[:, ls]
                for s in range(SUBLANES):
                    gs = gw if s == 0 else pltpu.roll(gw, s, 0)
                    for q in range(CONV_HALO // SUBLANES):
                        d = SUBLANES * q + s
                        if d < CONV_WIDTH:
                            lo = CONV_HALO - SUBLANES * q
                            acc = acc + wd_ref[d:d + 1, ls] * gs[lo:lo + CONV_CHUNK]
                dwc_ref[pl.ds(r0, CONV_CHUNK), ls] = acc
            return carry

        lax.fori_loop(0, n_chunks, chunk, 0)

        xv = dwc_ref[...]
        mu = jnp.mean(xv, axis=-1, keepdims=True)
        xc = xv - mu
        var = jnp.mean(xc * xc, axis=-1, keepdims=True)
        ln = xc * lax.rsqrt(var + LN_EPS) * lng_ref[...] + lnb_ref[...]
        z_ref[...] = (ln * _sig(ln)).astype(BF16)

    return _call(
        body, name="conv_fwd", grid=(seq // tm,),
        in_specs=[_rows(tm, D_MODEL), _full((1, D_MODEL)), _full((2 * D_MODEL, D_MODEL)), _full((1, 2 * D_MODEL)),
                  _full((TAPS_PAD, D_MODEL)), _full((1, D_MODEL)), _full((1, D_MODEL)), _full((1, D_MODEL))],
        out_specs=[_rows(tm, D_MODEL), _rows(tm, 2 * D_MODEL), _rows(tm, D_MODEL), _rows(tm, D_MODEL)],
        out_shape=[jax.ShapeDtypeStruct((seq, D_MODEL), BF16), jax.ShapeDtypeStruct((seq, 2 * D_MODEL), BF16),
                   jax.ShapeDtypeStruct((seq, D_MODEL), F32), jax.ShapeDtypeStruct((seq, D_MODEL), BF16)],
        scratch=[pltpu.VMEM((tm + CONV_HALO, D_MODEL), F32)],
        sem=("arbitrary",))(h, g, wpw1, bpw1, wd, bdw, lng, lnb)


def _final_loss(h, target, g):
    seq = h.shape[0]
    tm = min(TOKEN_TILE, seq)

    def body(h_ref, t_ref, g_ref, dh_ref, loss_ref, dg_ref):
        @pl.when(pl.program_id(0) == 0)
        def _():
            loss_ref[...] = jnp.zeros_like(loss_ref)
            dg_ref[...] = jnp.zeros_like(dg_ref)

        hv = h_ref[...]
        gv = g_ref[...]
        err = hv * _rms_stats(hv) * gv - t_ref[...]
        sq = jnp.sum(jnp.sum(err * err, axis=-1, keepdims=True), axis=0, keepdims=True)
        loss_ref[...] += jnp.broadcast_to(sq, loss_ref.shape)
        dx, dg = _rms_bwd(err * (1.0 / D_MODEL), hv, gv, 0.0)
        dh_ref[...] = dx
        dg_ref[...] += dg

    return _call(
        body, name="final_loss", grid=(seq // tm,),
        in_specs=[_rows(tm, D_MODEL), _rows(tm, D_MODEL), _full((1, D_MODEL))],
        out_specs=[_rows(tm, D_MODEL), _full((SUBLANES, LANES)), _full((1, D_MODEL))],
        out_shape=[jax.ShapeDtypeStruct((seq, D_MODEL), F32), jax.ShapeDtypeStruct((SUBLANES, LANES), F32),
                   jax.ShapeDtypeStruct((1, D_MODEL), F32)],
        sem=("arbitrary",))(h, target, g)


def _ffn_bwd_act(name, dh, u, w, gathered, layer):
    seq = dh.shape[0]
    tm = min(TOKEN_TILE, seq)

    def body(dh_ref, u_ref, w_ref, w2_ref, du_ref, dw_ref, dhb):
        @pl.when(pl.program_id(1) == 0)
        def _():
            dhb[...] = dh_ref[...].astype(BF16)

        ds = _nt(dhb[...], w2_ref[...].reshape(FF_TILE, D_MODEL))
        uv = u_ref[...].astype(F32)
        sg = _sig(uv)
        dw_ref[...] = (ds * (uv * sg)).astype(BF16)
        du_ref[...] = (ds * w_ref[...].astype(F32) * (sg * (1.0 + uv * (1.0 - sg)))).astype(BF16)

    tile_ff = pl.BlockSpec((tm, FF_TILE), lambda i, j: (i, j))
    ff_shape = jax.ShapeDtypeStruct((seq, D_FF), BF16)
    return _call(
        body, name=name, grid=(seq // tm, FF_SPLIT),
        in_specs=[_rows(tm, D_MODEL), tile_ff, tile_ff, _ff_tile_spec(4 + layer)],
        out_specs=[tile_ff, tile_ff], out_shape=[ff_shape, ff_shape],
        scratch=[pltpu.VMEM((tm, D_MODEL), BF16)],
        sem=("parallel", "arbitrary"))(dh, u, w, gathered)


def _ffn_bwd_in(name, du, dw, gathered, layer, h_in, dres, g):
    seq = du.shape[0]
    tm = min(TOKEN_TILE // 2, seq)

    def body(du_ref, dw_ref, w1_ref, w3_ref, h_ref, dr_ref, g_ref, dx_ref, dg_ref):
        @pl.when(pl.program_id(0) == 0)
        def _():
            dg_ref[...] = jnp.zeros_like(dg_ref)

        df = jnp.dot(du_ref[...], w1_ref[...].reshape(D_FF, D_MODEL), preferred_element_type=F32)
        df = df + jnp.dot(dw_ref[...], w3_ref[...].reshape(D_FF, D_MODEL), preferred_element_type=F32)
        dx, dg = _rms_bwd(df, h_ref[...], g_ref[...], dr_ref[...])
        dx_ref[...] = dx
        dg_ref[...] += dg

    return _call(
        body, name=name, grid=(seq // tm,),
        in_specs=[_rows(tm, D_FF), _rows(tm, D_FF), _ff_whole_spec(layer), _ff_whole_spec(2 + layer),
                  _rows(tm, D_MODEL), _rows(tm, D_MODEL), _full((1, D_MODEL))],
        out_specs=[_rows(tm, D_MODEL), _full((1, D_MODEL))],
        out_shape=[jax.ShapeDtypeStruct((seq, D_MODEL), F32), jax.ShapeDtypeStruct((1, D_MODEL), F32)],
        sem=("arbitrary",))(du, dw, gathered, gathered, h_in, dres, g)


def _mm_tn(name, a, b, *, tk):
    seq, kdim = a.shape
    n = b.shape[1]
    tt = min(2 * TOKEN_TILE, seq)
    n_t = seq // tt

    def body(a_ref, b_ref, o_ref, acc):
        t = pl.program_id(1)

        @pl.when(t == 0)
        def _():
            acc[...] = jnp.zeros_like(acc)

        acc[...] += _tn(a_ref[...].astype(BF16), b_ref[...].astype(BF16))

        @pl.when(t == n_t - 1)
        def _():
            o_ref[...] = acc[...].astype(BF16)

    return _call(
        body, name=name, grid=(kdim // tk, n_t),
        in_specs=[pl.BlockSpec((tt, tk), lambda k, t: (t, k)), pl.BlockSpec((tt, n), lambda k, t: (t, 0))],
        out_specs=pl.BlockSpec((tk, n), lambda k, t: (k, 0)),
        out_shape=jax.ShapeDtypeStruct((kdim, n), BF16),
        scratch=[pltpu.VMEM((tk, n), F32)],
        sem=("parallel", "arbitrary"))(a, b)


def _conv_bwd(dh, wpw2, dwc, a, lng, lnb, wd):
    seq = dh.shape[0]
    tm = min(TOKEN_TILE, seq)
    nt = seq // tm
    n_chunks = tm // CONV_CHUNK
    win = CONV_CHUNK + CONV_HALO
    halo_per_tile = tm // CONV_HALO

    def body(dh_ref, w_ref, dwc_ref, a_ref, ah_ref, lng_ref, lnb_ref, wd_ref,
             da_ref, dlg_ref, dlb_ref, dbdw_ref, dwd_ref, dbpw1_ref, dbpw2_ref,
             gbuf, dbuf, dglu, dwd_part):
        i = pl.program_id(0)
        r = nt - 1 - i

        @pl.when(i == 0)
        def _():
            dlg_ref[...] = jnp.zeros_like(dlg_ref)
            dlb_ref[...] = jnp.zeros_like(dlb_ref)
            dbdw_ref[...] = jnp.zeros_like(dbdw_ref)
            dbpw1_ref[...] = jnp.zeros_like(dbpw1_ref)
            dbpw2_ref[...] = jnp.zeros_like(dbpw2_ref)
            dwd_part[...] = jnp.zeros_like(dwd_part)
            dbuf[tm:tm + CONV_HALO, :] = jnp.zeros((CONV_HALO, D_MODEL), F32)

        @pl.when(i > 0)
        def _():
            dbuf[tm:tm + CONV_HALO, :] = dbuf[0:CONV_HALO, :]

        dhv = dh_ref[...]
        dbpw2_ref[...] += _sum_rows(dhv)
        dz = _nt(dhv.astype(BF16), w_ref[...])
        xv = dwc_ref[...]
        lg = lng_ref[...]
        mu = jnp.mean(xv, axis=-1, keepdims=True)
        xc = xv - mu
        rstd = lax.rsqrt(jnp.mean(xc * xc, axis=-1, keepdims=True) + LN_EPS)
        xhat = xc * rstd
        ln = xhat * lg + lnb_ref[...]
        sg = _sig(ln)
        dln = dz * (sg * (1.0 + ln * (1.0 - sg)))
        dlg_ref[...] += _sum_rows(dln * xhat)
        dlb_ref[...] += _sum_rows(dln)
        dxh = dln * lg
        ddw = rstd * (dxh - jnp.mean(dxh, axis=-1, keepdims=True)
                      - xhat * jnp.mean(dxh * xhat, axis=-1, keepdims=True))
        dbdw_ref[...] += _sum_rows(ddw)
        dbuf[0:tm, :] = ddw

        av = a_ref[...].astype(F32)
        a1 = av[:, :D_MODEL]
        s2 = _sig(av[:, D_MODEL:])
        gbuf[CONV_HALO:CONV_HALO + tm, :] = a1 * s2
        ah = ah_ref[...].astype(F32)
        gh = ah[:, :D_MODEL] * _sig(ah[:, D_MODEL:])
        gbuf[0:CONV_HALO, :] = jnp.where(r > 0, gh, 0.0)

        def chunk(c, carry):
            r0 = pl.multiple_of(c * CONV_CHUNK, CONV_CHUNK)
            for l in range(D_MODEL // LANES):
                ls = slice(l * LANES, (l + 1) * LANES)
                dw_ = dbuf[pl.ds(r0, win), ls]
                gw = gbuf[pl.ds(r0, win), ls]
                dc = dw_[0:CONV_CHUNK]
                acc = jnp.zeros((CONV_CHUNK, LANES), F32)
                for s in range(SUBLANES):
                    ds_ = dw_ if s == 0 else pltpu.roll(dw_, win - s, 0)
                    gs = gw if s == 0 else pltpu.roll(gw, s, 0)
                    for q in range(CONV_HALO // SUBLANES):
                        d = SUBLANES * q + s
                        if d < CONV_WIDTH:
                            acc = acc + wd_ref[d:d + 1, ls] * ds_[SUBLANES * q:SUBLANES * q + CONV_CHUNK]
                            lo = CONV_HALO - SUBLANES * q
                            prod = dc * gs[lo:lo + CONV_CHUNK]
                            dwd_part[d, :, ls] += jnp.sum(
                                prod.reshape(CONV_CHUNK // SUBLANES, SUBLANES, LANES), axis=0)
                dglu[pl.ds(r0, CONV_CHUNK), ls] = acc
            return carry

        lax.fori_loop(0, n_chunks, chunk, 0)

        dg_ = dglu[...]
        da1 = dg_ * s2
        da2 = dg_ * a1 * s2 * (1.0 - s2)
        da_ref[:, :D_MODEL] = da1.astype(BF16)
        da_ref[:, D_MODEL:] = da2.astype(BF16)
        dbpw1_ref[:, :D_MODEL] += _sum_rows(da1)
        dbpw1_ref[:, D_MODEL:] += _sum_rows(da2)

        @pl.when(i == nt - 1)
        def _():
            dwd_ref[...] = jnp.sum(dwd_part[...], axis=1)

    rev = lambda n: pl.BlockSpec((tm, n), lambda i: (nt - 1 - i, 0))
    halo = pl.BlockSpec((CONV_HALO, 2 * D_MODEL),
                        lambda i: (jnp.maximum((nt - 1 - i) * halo_per_tile - 1, 0), 0))
    vec = lambda n: _full((1, n))
    return _call(
        body, name="conv_bwd", grid=(nt,),
        in_specs=[rev(D_MODEL), _full((D_MODEL, D_MODEL)), rev(D_MODEL), rev(2 * D_MODEL), halo,
                  vec(D_MODEL), vec(D_MODEL), _full((TAPS_PAD, D_MODEL))],
        out_specs=[rev(2 * D_MODEL), vec(D_MODEL), vec(D_MODEL), vec(D_MODEL), _full((TAPS_PAD, D_MODEL)),
                   vec(2 * D_MODEL), vec(D_MODEL)],
        out_shape=[jax.ShapeDtypeStruct((seq, 2 * D_MODEL), BF16), jax.ShapeDtypeStruct((1, D_MODEL), F32),
                   jax.ShapeDtypeStruct((1, D_MODEL), F32), jax.ShapeDtypeStruct((1, D_MODEL), F32),
                   jax.ShapeDtypeStruct((TAPS_PAD, D_MODEL), F32), jax.ShapeDtypeStruct((1, 2 * D_MODEL), F32),
                   jax.ShapeDtypeStruct((1, D_MODEL), F32)],
        scratch=[pltpu.VMEM((tm + CONV_HALO, D_MODEL), F32), pltpu.VMEM((tm + CONV_HALO, D_MODEL), F32),
                 pltpu.VMEM((tm, D_MODEL), F32), pltpu.VMEM((TAPS_PAD, SUBLANES, D_MODEL), F32)],
        sem=("arbitrary",))(dh, wpw2, dwc, a, a, lng, lnb, wd)


def _mm_rms_bwd(name, dact, wt, h_in, dres, g):
    seq, n = dact.shape
    tm = min(TOKEN_TILE, seq)

    def body(da_ref, w_ref, h_ref, dr_ref, g_ref, dx_ref, dg_ref):
        @pl.when(pl.program_id(0) == 0)
        def _():
            dg_ref[...] = jnp.zeros_like(dg_ref)

        dy = jnp.dot(da_ref[...], w_ref[...], preferred_element_type=F32)
        dx, dg = _rms_bwd(dy, h_ref[...], g_ref[...], dr_ref[...])
        dx_ref[...] = dx
        dg_ref[...] += dg

    return _call(
        body, name=name, grid=(seq // tm,),
        in_specs=[_rows(tm, n), _full((n, D_MODEL)), _rows(tm, D_MODEL), _rows(tm, D_MODEL), _full((1, D_MODEL))],
        out_specs=[_rows(tm, D_MODEL), _full((1, D_MODEL))],
        out_shape=[jax.ShapeDtypeStruct((seq, D_MODEL), F32), jax.ShapeDtypeStruct((1, D_MODEL), F32)],
        sem=("arbitrary",))(dact, wt, h_in, dres, g)


def _nt_bias(name, dy, w):
    seq, n = dy.shape
    kdim = w.shape[0]
    tm = min(TOKEN_TILE, seq)

    def body(dy_ref, w_ref, o_ref, db_ref):
        @pl.when(pl.program_id(0) == 0)
        def _():
            db_ref[...] = jnp.zeros_like(db_ref)

        dyv = dy_ref[...]
        db_ref[...] += _sum_rows(dyv)
        o_ref[...] = _nt(dyv.astype(BF16), w_ref[...]).astype(BF16)

    return _call(
        body, name=name, grid=(seq // tm,),
        in_specs=[_rows(tm, n), _full((kdim, n))],
        out_specs=[_rows(tm, kdim), _full((1, n))],
        out_shape=[jax.ShapeDtypeStruct((seq, kdim), BF16), jax.ShapeDtypeStruct((1, n), F32)],
        sem=("arbitrary",))(dy, w)


def _attn_bwd(sink_rows, q, k, v, do):
    seq = q.shape[0]
    cur, prev = _kv_specs()

    def body(sink_ref, q_ref, kc_ref, kp_ref, vc_ref, vp_ref, do_ref,
             dq_ref, dkc_ref, dkp_ref, dvc_ref, dvp_ref, dsink_ref):
        n = pl.program_id(0)

        @pl.when(n == 0)
        def _():
            dsink_ref[...] = jnp.zeros_like(dsink_ref)

        mask = _band_mask(n)
        for j in range(N_KV_HEADS):
            cs = slice(j * HEAD_DIM, (j + 1) * HEAD_DIM)
            kk = _kv_pair(kp_ref, kc_ref, j)
            vv = _kv_pair(vp_ref, vc_ref, j)
            qs = _stack_heads(q_ref, j)
            dos = _stack_heads(do_ref, j)
            probs, p_sink = _softmax_with_sink(_nt(qs, kk), mask, sink_ref[j])
            dp = _nt(dos, vv)
            delta = jnp.sum(probs * dp, axis=-1, keepdims=True)
            dsb = (probs * (dp - delta)).astype(BF16)
            dqs = jnp.dot(dsb, kk, preferred_element_type=F32) * (HEAD_DIM ** -0.5)
            dkk = _tn(dsb, qs)
            dvv = _tn(probs.astype(BF16), dos)
            sink_term = p_sink * delta
            for gq in range(Q_PER_KV):
                rows = slice(gq * BLOCK, (gq + 1) * BLOCK)
                dq_ref[:, _head_slice(j, gq)] = dqs[rows]
                h = j * Q_PER_KV + gq
                dsink_ref[h:h + 1, :] += jnp.broadcast_to(_sum_rows(sink_term[rows]), (1, LANES))
            dkp_ref[:, cs] = dkk[:BLOCK]
            dkc_ref[:, cs] = dkk[BLOCK:]
            dvp_ref[:, cs] = dvv[:BLOCK]
            dvc_ref[:, cs] = dvv[BLOCK:]

    kv_out = _rows(BLOCK, KV_DIM)
    kv_shape = jax.ShapeDtypeStruct((seq, KV_DIM), F32)
    return _call(
        body, name="attn_bwd", grid=(seq // BLOCK,),
        in_specs=[_full((N_KV_HEADS, GROUP_ROWS, 1)), _rows(BLOCK, Q_DIM), cur, prev, cur, prev,
                  _rows(BLOCK, Q_DIM)],
        out_specs=[_rows(BLOCK, Q_DIM), kv_out, kv_out, kv_out, kv_out, _full((N_Q_HEADS, LANES))],
        out_shape=[jax.ShapeDtypeStruct((seq, Q_DIM), F32), kv_shape, kv_shape, kv_shape, kv_shape,
                   jax.ShapeDtypeStruct((N_Q_HEADS, LANES), F32)],
        sem=("arbitrary",))(sink_rows, q, k, k, v, v, do)


def _rope_bwd(dq, dkc, dkp, dvc, dvp, rc, rsa, rsb):
    seq = dq.shape[0]
    nb = seq // BLOCK
    nxt = pl.BlockSpec((BLOCK, KV_DIM), lambda n: (jnp.minimum(n + 1, nb - 1), 0))

    def body(dq_ref, dkc_ref, dkp_ref, dvc_ref, dvp_ref, c_ref, sa_ref, sb_ref, o_ref, db_ref):
        n = pl.program_id(0)

        @pl.when(n == 0)
        def _():
            db_ref[...] = jnp.zeros_like(db_ref)

        c, sa, sb = c_ref[...], sa_ref[...], sb_ref[...]
        last = n == nb - 1
        dk = dkc_ref[...] + jnp.where(last, 0.0, dkp_ref[...])
        dv = dvc_ref[...] + jnp.where(last, 0.0, dvp_ref[...])
        for i in range(Q_DIM // LANES):
            ls = slice(i * LANES, (i + 1) * LANES)
            blk = _rope_t(dq_ref[:, ls], c, sa, sb)
            o_ref[:, ls] = blk.astype(BF16)
            db_ref[:, ls] += _sum_rows(blk)
        dkr = _rope_t(dk, c, sa, sb)
        o_ref[:, Q_DIM:Q_DIM + KV_DIM] = dkr.astype(BF16)
        db_ref[:, Q_DIM:Q_DIM + KV_DIM] += _sum_rows(dkr)
        o_ref[:, Q_DIM + KV_DIM:] = dv.astype(BF16)
        db_ref[:, Q_DIM + KV_DIM:] += _sum_rows(dv)

    kv = _rows(BLOCK, KV_DIM)
    tab = _rows(BLOCK, LANES)
    return _call(
        body, name="rope_bwd", grid=(nb,),
        in_specs=[_rows(BLOCK, Q_DIM), kv, nxt, kv, nxt, tab, tab, tab],
        out_specs=[_rows(BLOCK, QKV_DIM), _full((1, QKV_DIM))],
        out_shape=[jax.ShapeDtypeStruct((seq, QKV_DIM), BF16), jax.ShapeDtypeStruct((1, QKV_DIM), F32)],
        sem=("arbitrary",))(dq, dkc, dkp, dvc, dvp, rc, rsa, rsb)


def _adamw(w, g, m, v):
    m = ADAM_B1 * m + (1.0 - ADAM_B1) * g
    v = ADAM_B2 * v + (1.0 - ADAM_B2) * (g * g)
    m_hat = m / (1.0 - ADAM_B1 ** ADAM_STEP)
    v_hat = v / (1.0 - ADAM_B2 ** ADAM_STEP)
    delta = -ADAM_LR * (m_hat / (jnp.sqrt(v_hat) + ADAM_EPS) + ADAM_WD * w)
    return delta, m, v


def _sum_partials(own, recv):
    rows = own.shape[0]
    tr = ROW_TILE

    def body(own_ref, recv_ref, g_ref):
        g = own_ref[...].astype(F32)
        for j in range(3):
            g = g + recv_ref[j].astype(F32)
        g_ref[...] = g

    spec = _rows(tr, D_MODEL)
    return _call(
        body, name="rs_sum", grid=(rows // tr,),
        in_specs=[spec, pl.BlockSpec((3, tr, D_MODEL), lambda i: (0, i, 0))],
        out_specs=spec, out_shape=jax.ShapeDtypeStruct((rows, D_MODEL), F32),
        sem=("parallel",))(own, recv)


def _adamw_native(name, g, w, m, v):
    layers, rows, cols = w.shape
    tr = rows if rows <= 512 else 256

    def body(g_ref, w_ref, m_ref, v_ref, d_ref, nm_ref, nv_ref):
        d_ref[...], nm_ref[...], nv_ref[...] = _adamw(w_ref[...], g_ref[...], m_ref[...], v_ref[...])

    spec = pl.BlockSpec((1, tr, cols), lambda l, i: (l, i, 0))
    shape = jax.ShapeDtypeStruct(w.shape, F32)
    return _call(
        body, name=name, grid=(layers, rows // tr), in_specs=[spec, spec, spec, spec],
        out_specs=[spec, spec, spec], out_shape=[shape, shape, shape],
        sem=("parallel", "parallel"))(g, w, m, v)


def _adamw_replicated(parts, w, m, v):
    def body(p_ref, w_ref, m_ref, v_ref, g_ref, d_ref, nm_ref, nv_ref):
        g = p_ref[0]
        for j in range(1, N_DEV):
            g = g + p_ref[j]
        g_ref[...] = g
        d_ref[...], nm_ref[...], nv_ref[...] = _adamw(w_ref[...], g, m_ref[...], v_ref[...])

    spec = _full((REPL_ROWS, D_MODEL))
    shape = jax.ShapeDtypeStruct((REPL_ROWS, D_MODEL), F32)
    return _call(
        body, name="adamw_replicated", grid=(1,),
        in_specs=[_full((N_DEV, REPL_ROWS, D_MODEL)), spec, spec, spec],
        out_specs=[spec, spec, spec, spec], out_shape=[shape, shape, shape, shape],
        sem=("arbitrary",))(parts, w, m, v)


def _add_pair(keep, recv):
    n, rows, cols = keep.shape
    tr = ROW_TILE

    def body(a_ref, b_ref, o_ref):
        o_ref[...] = (a_ref[...].astype(F32) + b_ref[...].astype(F32)).astype(BF16)

    spec = pl.BlockSpec((1, tr, cols), lambda c, i: (c, i, 0))
    return _call(
        body, name="add_pair", grid=(n, rows // tr), in_specs=[spec, spec], out_specs=spec,
        out_shape=jax.ShapeDtypeStruct((n, rows, cols), BF16), sem=("parallel", "parallel"))(keep, recv)


_MESH = pl.DeviceIdType.MESH
_ANY = pl.BlockSpec(memory_space=pl.ANY)


def _all_gather(name, xs):
    rows, cols = xs.shape

    def body(x_ref, out_ref, send_sems, recv_sems, local_sem):
        x, y, c = lax.axis_index("x"), lax.axis_index("y"), lax.axis_index("c")
        me, sibling = (x, y, c), (x, y, 1 - c)
        chips = [(1 - x, y), (x, 1 - y), (1 - x, 1 - y)]

        def slot(px, py, pc):
            return out_ref.at[4 * px + 2 * py + pc]

        def copy(k, block, to, src=None):
            return pltpu.make_async_remote_copy(
                src_ref=slot(*block) if src is None else src, dst_ref=slot(*block),
                send_sem=send_sems.at[k], recv_sem=recv_sems.at[k], device_id=to, device_id_type=_MESH)

        mine = pltpu.make_async_copy(x_ref, slot(*me), local_sem)
        mine.start()
        first = [copy(0, me, sibling, src=x_ref)]
        first += [copy(1 + j, me, (*chip, c), src=x_ref) for j, chip in enumerate(chips)]
        for cp in first:
            cp.start()
        passed = [copy(4 + j, (*chip, c), sibling) for j, chip in enumerate(chips)]
        for j, chip in enumerate(chips):
            copy(1 + j, (*chip, c), me).wait_recv()
            passed[j].start()
        copy(0, sibling, me).wait_recv()
        for j, chip in enumerate(chips):
            copy(4 + j, (*chip, 1 - c), me).wait_recv()
        for cp in first + passed:
            cp.wait_send()
        mine.wait()

    return pl.pallas_call(
        body, name=name, out_shape=jax.ShapeDtypeStruct((N_DEV, rows, cols), xs.dtype),
        in_specs=[_ANY], out_specs=_ANY,
        scratch_shapes=[pltpu.SemaphoreType.DMA((7,)), pltpu.SemaphoreType.DMA((7,)), pltpu.SemaphoreType.DMA],
    )(xs)


def _swap_with_sibling(send):
    def body(s_ref, out_ref, send_sem, recv_sem):
        x, y, c = lax.axis_index("x"), lax.axis_index("y"), lax.axis_index("c")
        cp = pltpu.make_async_remote_copy(src_ref=s_ref, dst_ref=out_ref, send_sem=send_sem, recv_sem=recv_sem,
                                          device_id=(x, y, 1 - c), device_id_type=_MESH)
        cp.start()
        cp.wait()

    return pl.pallas_call(
        body, name="rs_pair", out_shape=jax.ShapeDtypeStruct(send.shape, send.dtype),
        in_specs=[_ANY], out_specs=_ANY,
        scratch_shapes=[pltpu.SemaphoreType.DMA, pltpu.SemaphoreType.DMA],
    )(send)


def _exchange_chips(p):
    _, rows, cols = p.shape

    def body(p_ref, out_ref, send_sems, recv_sems):
        x, y, c = lax.axis_index("x"), lax.axis_index("y"), lax.axis_index("c")
        chips = [(1 - x, y), (x, 1 - y), (1 - x, 1 - y)]
        cps = [pltpu.make_async_remote_copy(
            src_ref=p_ref.at[2 * cx + cy], dst_ref=out_ref.at[j], send_sem=send_sems.at[j],
            recv_sem=recv_sems.at[j], device_id=(cx, cy, c), device_id_type=_MESH)
            for j, (cx, cy) in enumerate(chips)]
        for cp in cps:
            cp.start()
        for cp in cps:
            cp.wait()

    return pl.pallas_call(
        body, name="rs_chips", out_shape=jax.ShapeDtypeStruct((3, rows, cols), p.dtype),
        in_specs=[_ANY], out_specs=_ANY,
        scratch_shapes=[pltpu.SemaphoreType.DMA((3,)), pltpu.SemaphoreType.DMA((3,))],
    )(p)


def _pad_rows(flat, rows):
    return jnp.pad(flat, (0, rows * D_MODEL - flat.shape[0])).reshape(rows, D_MODEL)


SMALL_NAMES = ("conv_b_pw1", "conv_w_dw", "conv_b_dw", "conv_ln_g", "conv_ln_b", "conv_b_pw2")


def _pack_small(p):
    flat = jnp.concatenate([p[n].reshape(-1) for n in SMALL_NAMES])
    return _pad_rows(flat, ROWS_SMALL).reshape(1, ROWS_SMALL, D_MODEL)


def _unpack_small(packed):
    flat = packed.reshape(-1)
    c = D_MODEL // N_DEV
    shapes = ((1, 2 * c), (1, CONV_WIDTH, c), (1, c), (1, c), (1, c), (1, c))
    out, o = {}, 0
    for n, shape in zip(SMALL_NAMES, shapes):
        size = shape[-1] * (shape[1] if len(shape) == 3 else 1)
        out[n] = flat[o:o + size].reshape(shape)
        o += size
    return out


def _gather_payload(p):
    t = lambda a: jnp.swapaxes(a, -1, -2).astype(BF16)
    w1t, w3t, w2 = t(p["ffn_w1"]), t(p["ffn_w3"]), p["ffn_w2"].astype(BF16)
    small = _pack_small(p).reshape(-1)[:ROWS_SMALL * D_MODEL // 2]
    small = lax.bitcast_convert_type(small, BF16).reshape(ROWS_SMALL, D_MODEL)
    return jnp.concatenate([w1t[0], w1t[1], w3t[0], w3t[1], w2[0], w2[1], t(p["attn_w_qkv"][0]),
                            p["attn_w_o"][0].astype(BF16), t(p["conv_w_pw1"][0]), p["conv_w_pw2"][0].astype(BF16),
                            small], axis=0)


def _unpack_gathered(full):
    def rows(lo, n):
        return full[:, lo:lo + n].reshape(N_DEV * n, D_MODEL)

    small = lax.bitcast_convert_type(full[:, ROWS_BIG:].reshape(N_DEV, ROWS_SMALL * D_MODEL // 2, 2), F32)
    c = D_MODEL // N_DEV
    b_pw1 = small[:, :2 * c].reshape(1, 2 * D_MODEL)
    s = 2 * c
    w_dw = small[:, s:s + CONV_WIDTH * c].reshape(N_DEV, CONV_WIDTH, c).transpose(1, 0, 2).reshape(CONV_WIDTH, D_MODEL)
    s += CONV_WIDTH * c
    b_dw, ln_g, ln_b, b_pw2 = (small[:, s + i * c:s + (i + 1) * c].reshape(1, D_MODEL) for i in range(4))
    return dict(w_qkv_t=rows(OFF_QKV, ROWS_QKV), w_o=rows(OFF_WO, ROWS_WO), w_pw1_t=rows(OFF_PW1, ROWS_PW1),
                w_pw2=rows(OFF_PW2, ROWS_PW2), b_pw1=b_pw1, w_dw=w_dw, b_dw=b_dw, ln_g=ln_g, ln_b=ln_b, b_pw2=b_pw2)


def _pack_grads(mats, g_bpw1, g_dw, g_bdw, g_lng, g_lnb, g_bpw2, parity):
    def pick(a):
        blk = a.reshape(4, 2, -1, a.shape[-1])
        return lax.dynamic_index_in_dim(blk, parity, axis=1, keepdims=False)

    c = D_MODEL // N_DEV
    small = jnp.concatenate(
        [g_bpw1.reshape(N_DEV, 2 * c), g_dw.reshape(CONV_WIDTH, N_DEV, c).transpose(1, 0, 2).reshape(N_DEV, -1),
         g_bdw.reshape(N_DEV, c), g_lng.reshape(N_DEV, c), g_lnb.reshape(N_DEV, c), g_bpw2.reshape(N_DEV, c)], axis=1)
    small = jnp.pad(small, ((0, 0), (0, ROWS_SMALL * D_MODEL - SMALL_USED)))
    small = small.reshape(N_DEV * ROWS_SMALL, D_MODEL).astype(BF16)
    return jnp.concatenate([pick(a) for a in mats] + [pick(small)], axis=1)


def _pack_replicated(norm_mix, norm_ffn, b_qkv, sinks, b_o, norm_final):
    rows = [norm_mix.reshape(2, D_MODEL), norm_ffn.reshape(2, D_MODEL), _pad_rows(b_qkv.reshape(-1), 2),
            _pad_rows(sinks.reshape(-1), 1), b_o.reshape(1, D_MODEL), norm_final.reshape(1, D_MODEL)]
    p = jnp.concatenate(rows, axis=0)
    return jnp.pad(p, ((0, REPL_ROWS - p.shape[0]), (0, 0)))


def _unpack_replicated(p):
    return dict(norm_mix=p[0:2], norm_ffn=p[2:4], attn_b_qkv=p[4:6].reshape(-1)[:QKV_DIM].reshape(1, QKV_DIM),
                attn_sinks=p[6, :N_Q_HEADS].reshape(1, N_Q_HEADS), attn_b_o=p[7:8], norm_final=p[8])


WEIGHT_ORDER = ['norm_mix', 'norm_ffn', 'attn_w_qkv', 'attn_b_qkv', 'attn_sinks', 'attn_w_o', 'attn_b_o',
                'conv_w_pw1', 'conv_b_pw1', 'conv_w_dw', 'conv_b_dw', 'conv_ln_g', 'conv_ln_b', 'conv_w_pw2',
                'conv_b_pw2', 'ffn_w1', 'ffn_w3', 'ffn_w2', 'norm_final']


def kernel(x, norm_mix, norm_ffn, attn_w_qkv, attn_b_qkv, attn_sinks, attn_w_o, attn_b_o, conv_w_pw1, conv_b_pw1, conv_w_dw, conv_b_dw, conv_ln_g, conv_ln_b, conv_w_pw2, conv_b_pw2, ffn_w1, ffn_w3, ffn_w2, norm_final, loss_target, m_norm_mix, m_norm_ffn, m_attn_w_qkv, m_attn_b_qkv, m_attn_sinks, m_attn_w_o, m_attn_b_o, m_conv_w_pw1, m_conv_b_pw1, m_conv_w_dw, m_conv_b_dw, m_conv_ln_g, m_conv_ln_b, m_conv_w_pw2, m_conv_b_pw2, m_ffn_w1, m_ffn_w3, m_ffn_w2, m_norm_final, v_norm_mix, v_norm_ffn, v_attn_w_qkv, v_attn_b_qkv, v_attn_sinks, v_attn_w_o, v_attn_b_o, v_conv_w_pw1, v_conv_b_pw1, v_conv_w_dw, v_conv_b_dw, v_conv_ln_g, v_conv_ln_b, v_conv_w_pw2, v_conv_b_pw2, v_ffn_w1, v_ffn_w3, v_ffn_w2, v_norm_final):
    xs = x[0]
    target = loss_target[0]
    seq = xs.shape[0]
    my_x, my_y, my_c = lax.axis_index("x"), lax.axis_index("y"), lax.axis_index("c")

    w = dict(attn_w_qkv=attn_w_qkv, attn_w_o=attn_w_o, conv_w_pw1=conv_w_pw1, conv_b_pw1=conv_b_pw1,
             conv_w_dw=conv_w_dw, conv_b_dw=conv_b_dw, conv_ln_g=conv_ln_g, conv_ln_b=conv_ln_b,
             conv_w_pw2=conv_w_pw2, conv_b_pw2=conv_b_pw2, ffn_w1=ffn_w1, ffn_w3=ffn_w3, ffn_w2=ffn_w2)
    m = dict(attn_w_qkv=m_attn_w_qkv, attn_w_o=m_attn_w_o, conv_w_pw1=m_conv_w_pw1, conv_b_pw1=m_conv_b_pw1,
             conv_w_dw=m_conv_w_dw, conv_b_dw=m_conv_b_dw, conv_ln_g=m_conv_ln_g, conv_ln_b=m_conv_ln_b,
             conv_w_pw2=m_conv_w_pw2, conv_b_pw2=m_conv_b_pw2, ffn_w1=m_ffn_w1, ffn_w3=m_ffn_w3, ffn_w2=m_ffn_w2)
    v = dict(attn_w_qkv=v_attn_w_qkv, attn_w_o=v_attn_w_o, conv_w_pw1=v_conv_w_pw1, conv_b_pw1=v_conv_b_pw1,
             conv_w_dw=v_conv_w_dw, conv_b_dw=v_conv_b_dw, conv_ln_g=v_conv_ln_g, conv_ln_b=v_conv_ln_b,
             conv_w_pw2=v_conv_w_pw2, conv_b_pw2=v_conv_b_pw2, ffn_w1=v_ffn_w1, ffn_w3=v_ffn_w3, ffn_w2=v_ffn_w2)

    gathered = _all_gather("ag_weights", _gather_payload(w))
    wt = _unpack_gathered(gathered)
    wd = jnp.concatenate([wt["w_dw"][::-1], jnp.zeros((TAPS_PAD - CONV_WIDTH, D_MODEL), F32)], axis=0)
    rc, rsa, rsb = _rope_tables(seq)
    sink_rows = _sink_rows(attn_sinks)
    g_mix0, g_mix1 = norm_mix[0:1], norm_mix[1:2]
    g_ffn0, g_ffn1 = norm_ffn[0:1], norm_ffn[1:2]
    g_fin = norm_final.reshape(1, D_MODEL)

    y0, q, k, vv = _qkv_fwd(xs, g_mix0, wt["w_qkv_t"], attn_b_qkv, rc, rsa, rsb)
    attn = _attn_fwd(sink_rows, q, k, vv)
    h1 = _mm_res("attn_out_proj", attn, wt["w_o"], attn_b_o, xs)
    f0, u0, p0, s0 = _ffn_up("ffn0_up", h1, g_ffn0, gathered, 0)
    h2 = _ffn_down("ffn0_down", s0, gathered, 0, h1)
    y1, a, dwc, z = _conv_fwd(h2, g_mix1, wt["w_pw1_t"], wt["b_pw1"], wd, wt["b_dw"], wt["ln_g"], wt["ln_b"])
    h3 = _mm_res("conv_out_proj", z, wt["w_pw2"], wt["b_pw2"], h2)
    f1, u1, p1, s1 = _ffn_up("ffn1_up", h3, g_ffn1, gathered, 1)
    h4 = _ffn_down("ffn1_down", s1, gathered, 1, h3)
    dh4, sq, dg_fin = _final_loss(h4, target, g_fin)
    loss = lax.psum(sq[0, 0] * (0.5 / D_MODEL), ("x", "y", "c"))

    du1, dp1 = _ffn_bwd_act("ffn1_bwd_act", dh4, u1, p1, gathered, 1)
    dh3, dg_ffn1 = _ffn_bwd_in("ffn1_bwd_in", du1, dp1, gathered, 1, h3, dh4, g_ffn1)
    gw2_1 = _mm_tn("ffn1_dw2", s1, dh4, tk=FF_TILE)
    gw1t_1 = _mm_tn("ffn1_dw1", du1, f1, tk=FF_TILE)
    gw3t_1 = _mm_tn("ffn1_dw3", dp1, f1, tk=FF_TILE)

    da, dlg, dlb, dbdw, dwd, dbpw1, dbpw2 = _conv_bwd(dh3, wt["w_pw2"], dwc, a, wt["ln_g"], wt["ln_b"], wd)
    gpw2 = _mm_tn("conv_dw_pw2", z, dh3, tk=D_MODEL)
    dh2, dg_mix1 = _mm_rms_bwd("conv_in_bwd", da, wt["w_pw1_t"], h2, dh3, g_mix1)
    gpw1t = _mm_tn("conv_dw_pw1", da, y1, tk=D_MODEL)

    du0, dp0 = _ffn_bwd_act("ffn0_bwd_act", dh2, u0, p0, gathered, 0)
    dh1, dg_ffn0 = _ffn_bwd_in("ffn0_bwd_in", du0, dp0, gathered, 0, h1, dh2, g_ffn0)
    gw2_0 = _mm_tn("ffn0_dw2", s0, dh2, tk=FF_TILE)
    gw1t_0 = _mm_tn("ffn0_dw1", du0, f0, tk=FF_TILE)
    gw3t_0 = _mm_tn("ffn0_dw3", dp0, f0, tk=FF_TILE)

    dattn, dbo = _nt_bias("attn_out_bwd", dh1, wt["w_o"])
    gwo = _mm_tn("attn_dw_o", attn, dh1, tk=D_MODEL)
    dq, dkc, dkp, dvc, dvp, dsink = _attn_bwd(sink_rows, q, k, vv, dattn)
    dqkv, dbqkv = _rope_bwd(dq, dkc, dkp, dvc, dvp, rc, rsa, rsb)
    dx, dg_mix0 = _mm_rms_bwd("qkv_in_bwd", dqkv, wt["w_qkv_t"], xs, dh1, g_mix0)
    gqkvt = _mm_tn("attn_dw_qkv", dqkv, y0, tk=QKV_DIM)

    mats = [gw1t_0, gw1t_1, gw3t_0, gw3t_1, gw2_0, gw2_1, gqkvt, gwo, gpw1t, gpw2]
    smalls = (dbpw1, dwd[:CONV_WIDTH][::-1], dbdw, dlg, dlb, dbpw2)
    keep = _pack_grads(mats, *smalls, my_c)
    send = _pack_grads(mats, *smalls, 1 - my_c)
    pair = _add_pair(keep, _swap_with_sibling(send))
    recv = _exchange_chips(pair)
    own = lax.dynamic_index_in_dim(pair, 2 * my_x + my_y, axis=0, keepdims=False)
    g_sh = _sum_partials(own, recv)

    def ff_t(slot):
        return jnp.swapaxes(g_sh[slot * ROWS_FF:(slot + 2) * ROWS_FF].reshape(2, ROWS_FF, D_MODEL), 1, 2)

    grads = dict(
        ffn_w1=ff_t(0), ffn_w3=ff_t(2), ffn_w2=g_sh[4 * ROWS_FF:6 * ROWS_FF].reshape(2, ROWS_FF, D_MODEL),
        attn_w_qkv=g_sh[OFF_QKV:OFF_WO].T[None], attn_w_o=g_sh[OFF_WO:OFF_PW1][None],
        conv_w_pw1=g_sh[OFF_PW1:OFF_PW2].T[None], conv_w_pw2=g_sh[OFF_PW2:ROWS_BIG][None])
    sharded = [dict(grads), {}, {}, {}]
    for n in grads:
        for dst, t in zip(sharded[1:], _adamw_native("adamw_" + n, grads[n], w[n], m[n], v[n])):
            dst[n] = t
    g_small = g_sh[ROWS_BIG:][None]
    small_out = _adamw_native("adamw_small", g_small, _pack_small(w), _pack_small(m), _pack_small(v))
    for dst, t in zip(sharded, (g_small,) + tuple(small_out)):
        dst.update(_unpack_small(t))

    part = _pack_replicated(jnp.concatenate([dg_mix0, dg_mix1]), jnp.concatenate([dg_ffn0, dg_ffn1]),
                            dbqkv, -dsink[:, 0], dbo, dg_fin)
    parts = _all_gather("ag_replicated_grads", part)
    w_rep = _pack_replicated(norm_mix, norm_ffn, attn_b_qkv, attn_sinks, attn_b_o, norm_final)
    m_rep = _pack_replicated(m_norm_mix, m_norm_ffn, m_attn_b_qkv, m_attn_sinks, m_attn_b_o, m_norm_final)
    v_rep = _pack_replicated(v_norm_mix, v_norm_ffn, v_attn_b_qkv, v_attn_sinks, v_attn_b_o, v_norm_final)
    replicated = [_unpack_replicated(t) for t in _adamw_replicated(parts, w_rep, m_rep, v_rep)]

    outs = [loss, dx.reshape(1, seq, D_MODEL)]
    for sh, rp in zip(sharded, replicated):
        merged = {**sh, **rp}
        outs += [merged[n] for n in WEIGHT_ORDER]
    return tuple(outs)
```

```python
import jax
import jax.numpy as jnp
from jax import lax
from jax.experimental import pallas as pl
from jax.experimental.pallas import tpu as pltpu

F32 = jnp.float32
BF16 = jnp.bfloat16

D_MODEL = 1024
HEAD_DIM = 64
N_Q_HEADS = 16
N_KV_HEADS = 2
Q_PER_KV = 8
Q_DIM = N_Q_HEADS * HEAD_DIM
KV_DIM = N_KV_HEADS * HEAD_DIM
QKV_DIM = Q_DIM + 2 * KV_DIM
BLOCK = 128
CONV_WIDTH = 31
D_FF = 2816
ROPE_THETA = 10000.0
RMS_EPS = 1e-5
LN_EPS = 1e-5
ADAM_LR = 0.001
ADAM_B1 = 0.9
ADAM_B2 = 0.999
ADAM_EPS = 1e-08
ADAM_WD = 0.01
ADAM_STEP = 10
N_DEV = 8

LANES = 128
SUBLANES = 8
TOKEN_TILE = 512
CONV_CHUNK = 64
CONV_HALO = 32
TAPS_PAD = 32
VMEM_LIMIT = 56 * 1024 * 1024
NEG_INF = float(jnp.finfo(jnp.float32).min)

ROWS_FF = D_FF // N_DEV
W1T_SLOT, W3T_SLOT, W2_SLOT = 0, 1, 2
ROWS_QKV = QKV_DIM // N_DEV
ROWS_WO = Q_DIM // N_DEV
ROWS_PW1 = 2 * D_MODEL // N_DEV
ROWS_PW2 = D_MODEL // N_DEV
ROWS_SMALL = 16
SMALL_USED = 2 * D_MODEL // N_DEV + CONV_WIDTH * (D_MODEL // N_DEV) + 4 * (D_MODEL // N_DEV)
FF_SPLIT = 2
FF_TILE_DEVS = N_DEV // FF_SPLIT
FF_TILE = FF_TILE_DEVS * ROWS_FF
REPL_ROWS = 16


def _call(body, *, name, grid, in_specs, out_specs, out_shape, scratch=(), sem=None):
    return pl.pallas_call(
        body, name=name, grid=grid, in_specs=in_specs, out_specs=out_specs, out_shape=out_shape,
        scratch_shapes=list(scratch),
        compiler_params=pltpu.CompilerParams(dimension_semantics=sem, vmem_limit_bytes=VMEM_LIMIT))


def _full(shape):
    return pl.BlockSpec(shape, lambda *_: (0,) * len(shape))


def _rows(tm, n):
    return pl.BlockSpec((tm, n), lambda i, *_: (i, 0))


def _sig(x):
    return 1.0 / (1.0 + jnp.exp(-x))


def _sum_rows(x):
    return jnp.sum(x, axis=0, keepdims=True)


def _nt(a, b):
    return lax.dot_general(a, b, (((1,), (1,)), ((), ())), preferred_element_type=F32)


def _tn(a, b):
    return lax.dot_general(a, b, (((0,), (0,)), ((), ())), preferred_element_type=F32)


def _rms_stats(x):
    return lax.rsqrt(jnp.mean(x * x, axis=-1, keepdims=True) + RMS_EPS)


def _rms_bwd(dy, x, g, dres):
    r = _rms_stats(x)
    n = x * r
    dn = dy * g
    dx = dres + r * (dn - n * jnp.mean(dn * n, axis=-1, keepdims=True))
    return dx, _sum_rows(dy * n)


def _rope_tables(seq):
    pos = jnp.arange(seq, dtype=F32)
    inv_freq = ROPE_THETA ** (-jnp.arange(0, HEAD_DIM, 2, dtype=F32) / HEAD_DIM)
    ang = pos[:, None] * inv_freq[None, :]
    cos, sin = jnp.cos(ang), jnp.sin(ang)
    zero = jnp.zeros_like(sin)
    reps = LANES // HEAD_DIM
    c = jnp.tile(jnp.concatenate([cos, cos], axis=1), (1, reps))
    sa = jnp.tile(jnp.concatenate([-sin, zero], axis=1), (1, reps))
    sb = jnp.tile(jnp.concatenate([zero, sin], axis=1), (1, reps))
    return c, sa, sb


def _rope(t, c, sa, sb):
    half = HEAD_DIM // 2
    return t * c + pltpu.roll(t, LANES - half, 1) * sa + pltpu.roll(t, half, 1) * sb


def _rope_t(dt, c, sa, sb):
    half = HEAD_DIM // 2
    return dt * c + pltpu.roll(dt * sa, half, 1) + pltpu.roll(dt * sb, LANES - half, 1)


def _qkv_fwd(x, g, w, b, rc, rsa, rsb):
    seq = x.shape[0]
    tm = min(TOKEN_TILE, seq)

    def body(x_ref, g_ref, w_ref, b_ref, c_ref, sa_ref, sb_ref, y_ref, q_ref, k_ref, v_ref):
        xv = x_ref[...]
        y = (xv * _rms_stats(xv) * g_ref[...]).astype(BF16)
        y_ref[...] = y
        qkv = _nt(y, w_ref[...]) + b_ref[...]
        c, sa, sb = c_ref[...], sa_ref[...], sb_ref[...]
        for i in range(Q_DIM // LANES):
            blk = _rope(qkv[:, i * LANES:(i + 1) * LANES], c, sa, sb)
            q_ref[:, i * LANES:(i + 1) * LANES] = (blk * (HEAD_DIM ** -0.5)).astype(BF16)
        k_ref[...] = _rope(qkv[:, Q_DIM:Q_DIM + KV_DIM], c, sa, sb).astype(BF16)
        v_ref[...] = qkv[:, Q_DIM + KV_DIM:].astype(BF16)

    return _call(
        body, name="qkv_fwd", grid=(seq // tm,),
        in_specs=[_rows(tm, D_MODEL), _full((1, D_MODEL)), _full((QKV_DIM, D_MODEL)), _full((1, QKV_DIM)),
                  _rows(tm, LANES), _rows(tm, LANES), _rows(tm, LANES)],
        out_specs=[_rows(tm, D_MODEL), _rows(tm, Q_DIM), _rows(tm, KV_DIM), _rows(tm, KV_DIM)],
        out_shape=[jax.ShapeDtypeStruct((seq, D_MODEL), BF16), jax.ShapeDtypeStruct((seq, Q_DIM), BF16),
                   jax.ShapeDtypeStruct((seq, KV_DIM), BF16), jax.ShapeDtypeStruct((seq, KV_DIM), BF16)],
        sem=("parallel",))(x, g, w, b, rc, rsa, rsb)


GROUP_ROWS = Q_PER_KV * BLOCK


def _band_mask(n, rows=GROUP_ROWS):
    row = lax.broadcasted_iota(jnp.int32, (rows, 2 * BLOCK), 0) & (BLOCK - 1)
    col = lax.broadcasted_iota(jnp.int32, (rows, 2 * BLOCK), 1)
    rel = row + BLOCK - col
    return (rel >= 0) & (rel < BLOCK) & ((col >= BLOCK) | (n > 0))


def _softmax_with_sink(s, mask, sink):
    s = jnp.where(mask, s, NEG_INF)
    m = jnp.maximum(jnp.max(s, axis=-1, keepdims=True), sink)
    p = jnp.exp(s - m)
    e_sink = jnp.exp(sink - m)
    inv = 1.0 / (jnp.sum(p, axis=-1, keepdims=True) + e_sink)
    return p * inv, e_sink * inv


def _head_slice(j, gq):
    h = j * Q_PER_KV + gq
    return slice(h * HEAD_DIM, (h + 1) * HEAD_DIM)


def _stack_heads(ref, j):
    return jnp.concatenate([ref[:, _head_slice(j, gq)] for gq in range(Q_PER_KV)], axis=0)


def _kv_pair(prev_ref, cur_ref, j):
    cs = slice(j * HEAD_DIM, (j + 1) * HEAD_DIM)
    return jnp.concatenate([prev_ref[:, cs], cur_ref[:, cs]], axis=0)


def _kv_specs():
    cur = pl.BlockSpec((BLOCK, KV_DIM), lambda n: (n, 0))
    prev = pl.BlockSpec((BLOCK, KV_DIM), lambda n: (jnp.maximum(n - 1, 0), 0))
    return cur, prev


def _sink_rows(sinks):
    return jnp.repeat(sinks.reshape(N_KV_HEADS, Q_PER_KV), BLOCK, axis=1).reshape(N_KV_HEADS, GROUP_ROWS, 1)


def _attn_fwd(sinks, q, k, v):
    seq = q.shape[0]
    cur, prev = _kv_specs()

    def body(sink_ref, q_ref, kc_ref, kp_ref, vc_ref, vp_ref, o_ref):
        mask = _band_mask(pl.program_id(0), BLOCK)
        for j in range(N_KV_HEADS):
            kk = _kv_pair(kp_ref, kc_ref, j)
            vv = _kv_pair(vp_ref, vc_ref, j)
            for gq in range(Q_PER_KV):
                hs = _head_slice(j, gq)
                probs, _ = _softmax_with_sink(_nt(q_ref[:, hs], kk), mask, sink_ref[j * Q_PER_KV + gq])
                o_ref[:, hs] = jnp.dot(probs.astype(BF16), vv, preferred_element_type=F32).astype(BF16)

    return _call(
        body, name="attn_fwd", grid=(seq // BLOCK,),
        in_specs=[pl.BlockSpec(memory_space=pltpu.SMEM), _rows(BLOCK, Q_DIM), cur, prev, cur, prev],
        out_specs=_rows(BLOCK, Q_DIM), out_shape=jax.ShapeDtypeStruct((seq, Q_DIM), BF16),
        sem=("parallel",))(sinks, q, k, k, v, v)


def _mm_res(name, a, w, b, res):
    seq, kdim = a.shape
    n = w.shape[1]
    tm = min(TOKEN_TILE, seq)

    def body(a_ref, w_ref, b_ref, r_ref, o_ref):
        o_ref[...] = r_ref[...] + (jnp.dot(a_ref[...], w_ref[...], preferred_element_type=F32) + b_ref[...])

    return _call(
        body, name=name, grid=(seq // tm,),
        in_specs=[_rows(tm, kdim), _full((kdim, n)), _full((1, n)), _rows(tm, n)],
        out_specs=_rows(tm, n), out_shape=jax.ShapeDtypeStruct((seq, n), F32),
        sem=("parallel",))(a, w, b, res)


def _ff_tile_spec(slot):
    return pl.BlockSpec((FF_TILE_DEVS, ROWS_FF, D_MODEL), lambda i, j: (j, slot, 0))


def _ff_whole_spec(slot):
    return pl.BlockSpec((N_DEV, ROWS_FF, D_MODEL), lambda i: (0, slot, 0))


def _ffn_up(name, h, g, gathered):
    seq = h.shape[0]
    tm = min(TOKEN_TILE, seq)

    def body(h_ref, g_ref, w1_ref, w3_ref, f_ref, u_ref, w_ref, s_ref):
        @pl.when(pl.program_id(1) == 0)
        def _():
            hv = h_ref[...]
            f_ref[...] = (hv * _rms_stats(hv) * g_ref[...]).astype(BF16)

        f = f_ref[...]
        u = _nt(f, w1_ref[...].reshape(FF_TILE, D_MODEL))
        w = _nt(f, w3_ref[...].reshape(FF_TILE, D_MODEL))
        u_ref[...] = u.astype(BF16)
        w_ref[...] = w.astype(BF16)
        s_ref[...] = (u * _sig(u) * w).astype(BF16)

    tile_ff = pl.BlockSpec((tm, FF_TILE), lambda i, j: (i, j))
    ff_shape = jax.ShapeDtypeStruct((seq, D_FF), BF16)
    return _call(
        body, name=name, grid=(seq // tm, FF_SPLIT),
        in_specs=[_rows(tm, D_MODEL), _full((1, D_MODEL)), _ff_tile_spec(W1T_SLOT), _ff_tile_spec(W3T_SLOT)],
        out_specs=[_rows(tm, D_MODEL), tile_ff, tile_ff, tile_ff],
        out_shape=[jax.ShapeDtypeStruct((seq, D_MODEL), BF16), ff_shape, ff_shape, ff_shape],
        sem=("parallel", "arbitrary"))(h, g, gathered, gathered)


def _ffn_down(name, s, gathered, res):
    seq = s.shape[0]
    tm = min(TOKEN_TILE, seq)

    def body(s_ref, w_ref, r_ref, o_ref):
        w2 = w_ref[...].reshape(D_FF, D_MODEL)
        o_ref[...] = r_ref[...] + jnp.dot(s_ref[...], w2, preferred_element_type=F32)

    return _call(
        body, name=name, grid=(seq // tm,),
        in_specs=[_rows(tm, D_FF), _ff_whole_spec(W2_SLOT), _rows(tm, D_MODEL)],
        out_specs=_rows(tm, D_MODEL), out_shape=jax.ShapeDtypeStruct((seq, D_MODEL), F32),
        sem=("parallel",))(s, gathered, res)


def _conv_fwd(h, g, wpw1, bpw1, wd, bdw, lng, lnb):
    seq = h.shape[0]
    tm = min(TOKEN_TILE, seq)
    n_chunks = tm // CONV_CHUNK
    win = CONV_CHUNK + CONV_HALO

    def body(h_ref, g_ref, w_ref, b_ref, wd_ref, bdw_ref, lng_ref, lnb_ref,
             y_ref, a_ref, dwc_ref, z_ref, gbuf):
        i = pl.program_id(0)

        @pl.when(i == 0)
        def _():
            gbuf[0:CONV_HALO, :] = jnp.zeros((CONV_HALO, D_MODEL), F32)

        @pl.when(i > 0)
        def _():
            gbuf[0:CONV_HALO, :] = gbuf[tm:tm + CONV_HALO, :]

        hv = h_ref[...]
        y = (hv * _rms_stats(hv) * g_ref[...]).astype(BF16)
        y_ref[...] = y
        a = _nt(y, w_ref[...]) + b_ref[...]
        a_ref[...] = a.astype(BF16)
        gbuf[CONV_HALO:CONV_HALO + tm, :] = a[:, :D_MODEL] * _sig(a[:, D_MODEL:])

        def chunk(c, carry):
            r0 = pl.multiple_of(c * CONV_CHUNK, CONV_CHUNK)
            for l in range(D_MODEL // LANES):
                ls = slice(l * LANES, (l + 1) * LANES)
                gw = gbuf[pl.ds(r0, win), ls]
                acc = jnp.zeros((CONV_CHUNK, LANES), F32) + bdw_ref[:, ls]
                for s in range(SUBLANES):
                    gs = gw if s == 0 else pltpu.roll(gw, s, 0)
                    for q in range(CONV_HALO // SUBLANES):
                        d = SUBLANES * q + s
                        if d < CONV_WIDTH:
                            lo = CONV_HALO - SUBLANES * q
                            acc = acc + wd_ref[d:d + 1, ls] * gs[lo:lo + CONV_CHUNK]
                dwc_ref[pl.ds(r0, CONV_CHUNK), ls] = acc
            return carry

        lax.fori_loop(0, n_chunks, chunk, 0)

        xv = dwc_ref[...]
        mu = jnp.mean(xv, axis=-1, keepdims=True)
        xc = xv - mu
        var = jnp.mean(xc * xc, axis=-1, keepdims=True)
        ln = xc * lax.rsqrt(var + LN_EPS) * lng_ref[...] + lnb_ref[...]
        z_ref[...] = (ln * _sig(ln)).astype(BF16)

    return _call(
        body, name="conv_fwd", grid=(seq // tm,),
        in_specs=[_rows(tm, D_MODEL), _full((1, D_MODEL)), _full((2 * D_MODEL, D_MODEL)), _full((1, 2 * D_MODEL)),
                  _full((TAPS_PAD, D_MODEL)), _full((1, D_MODEL)), _full((1, D_MODEL)), _full((1, D_MODEL))],
        out_specs=[_rows(tm, D_MODEL), _rows(tm, 2 * D_MODEL), _rows(tm, D_MODEL), _rows(tm, D_MODEL)],
        out_shape=[jax.ShapeDtypeStruct((seq, D_MODEL), BF16), jax.ShapeDtypeStruct((seq, 2 * D_MODEL), BF16),
                   jax.ShapeDtypeStruct((seq, D_MODEL), F32), jax.ShapeDtypeStruct((seq, D_MODEL), BF16)],
        scratch=[pltpu.VMEM((tm + CONV_HALO, D_MODEL), F32)],
        sem=("arbitrary",))(h, g, wpw1, bpw1, wd, bdw, lng, lnb)


def _final_loss(h, target, g):
    seq = h.shape[0]
    tm = min(TOKEN_TILE, seq)

    def body(h_ref, t_ref, g_ref, dh_ref, loss_ref, dg_ref):
        @pl.when(pl.program_id(0) == 0)
        def _():
            loss_ref[...] = jnp.zeros_like(loss_ref)
            dg_ref[...] = jnp.zeros_like(dg_ref)

        hv = h_ref[...]
        gv = g_ref[...]
        err = hv * _rms_stats(hv) * gv - t_ref[...]
        sq = jnp.sum(jnp.sum(err * err, axis=-1, keepdims=True), axis=0, keepdims=True)
        loss_ref[...] += jnp.broadcast_to(sq, loss_ref.shape)
        dx, dg = _rms_bwd(err * (1.0 / D_MODEL), hv, gv, 0.0)
        dh_ref[...] = dx
        dg_ref[...] += dg

    return _call(
        body, name="final_loss", grid=(seq // tm,),
        in_specs=[_rows(tm, D_MODEL), _rows(tm, D_MODEL), _full((1, D_MODEL))],
        out_specs=[_rows(tm, D_MODEL), _full((SUBLANES, LANES)), _full((1, D_MODEL))],
        out_shape=[jax.ShapeDtypeStruct((seq, D_MODEL), F32), jax.ShapeDtypeStruct((SUBLANES, LANES), F32),
                   jax.ShapeDtypeStruct((1, D_MODEL), F32)],
        sem=("arbitrary",))(h, target, g)


def _ffn_bwd_act(name, dh, u, w, gathered):
    seq = dh.shape[0]
    tm = min(TOKEN_TILE, seq)

    def body(dh_ref, u_ref, w_ref, w2_ref, du_ref, dw_ref, dhb):
        @pl.when(pl.program_id(1) == 0)
        def _():
            dhb[...] = dh_ref[...].astype(BF16)

        ds = _nt(dhb[...], w2_ref[...].reshape(FF_TILE, D_MODEL))
        uv = u_ref[...].astype(F32)
        sg = _sig(uv)
        dw_ref[...] = (ds * (uv * sg)).astype(BF16)
        du_ref[...] = (ds * w_ref[...].astype(F32) * (sg * (1.0 + uv * (1.0 - sg)))).astype(BF16)

    tile_ff = pl.BlockSpec((tm, FF_TILE), lambda i, j: (i, j))
    ff_shape = jax.ShapeDtypeStruct((seq, D_FF), BF16)
    return _call(
        body, name=name, grid=(seq // tm, FF_SPLIT),
        in_specs=[_rows(tm, D_MODEL), tile_ff, tile_ff, _ff_tile_spec(W2_SLOT)],
        out_specs=[tile_ff, tile_ff], out_shape=[ff_shape, ff_shape],
        scratch=[pltpu.VMEM((tm, D_MODEL), BF16)],
        sem=("parallel", "arbitrary"))(dh, u, w, gathered)


def _ffn_bwd_in(name, du, dw, gathered, h_in, dres, g):
    seq = du.shape[0]
    tm = min(TOKEN_TILE // 2, seq)

    def body(du_ref, dw_ref, w1_ref, w3_ref, h_ref, dr_ref, g_ref, dx_ref, dg_ref):
        @pl.when(pl.program_id(0) == 0)
        def _():
            dg_ref[...] = jnp.zeros_like(dg_ref)

        df = jnp.dot(du_ref[...], w1_ref[...].reshape(D_FF, D_MODEL), preferred_element_type=F32)
        df = df + jnp.dot(dw_ref[...], w3_ref[...].reshape(D_FF, D_MODEL), preferred_element_type=F32)
        dx, dg = _rms_bwd(df, h_ref[...], g_ref[...], dr_ref[...])
        dx_ref[...] = dx
        dg_ref[...] += dg

    return _call(
        body, name=name, grid=(seq // tm,),
        in_specs=[_rows(tm, D_FF), _rows(tm, D_FF), _ff_whole_spec(W1T_SLOT), _ff_whole_spec(W3T_SLOT),
                  _rows(tm, D_MODEL), _rows(tm, D_MODEL), _full((1, D_MODEL))],
        out_specs=[_rows(tm, D_MODEL), _full((1, D_MODEL))],
        out_shape=[jax.ShapeDtypeStruct((seq, D_MODEL), F32), jax.ShapeDtypeStruct((1, D_MODEL), F32)],
        sem=("arbitrary",))(du, dw, gathered, gathered, h_in, dres, g)


def _mm_tn(name, a, b, *, tk):
    seq, kdim = a.shape
    n = b.shape[1]
    tt = min(2 * TOKEN_TILE, seq)
    n_t = seq // tt

    def body(a_ref, b_ref, o_ref, acc):
        t = pl.program_id(1)

        @pl.when(t == 0)
        def _():
            acc[...] = jnp.zeros_like(acc)

        acc[...] += _tn(a_ref[...].astype(BF16), b_ref[...].astype(BF16))

        @pl.when(t == n_t - 1)
        def _():
            o_ref[...] = acc[...].astype(BF16)

    return _call(
        body, name=name, grid=(kdim // tk, n_t),
        in_specs=[pl.BlockSpec((tt, tk), lambda k, t: (t, k)), pl.BlockSpec((tt, n), lambda k, t: (t, 0))],
        out_specs=pl.BlockSpec((tk, n), lambda k, t: (k, 0)),
        out_shape=jax.ShapeDtypeStruct((kdim, n), BF16),
        scratch=[pltpu.VMEM((tk, n), F32)],
        sem=("parallel", "arbitrary"))(a, b)


def _conv_bwd(dh, wpw2, dwc, a, lng, lnb, wd):
    seq = dh.shape[0]
    tm = min(TOKEN_TILE, seq)
    nt = seq // tm
    n_chunks = tm // CONV_CHUNK
    win = CONV_CHUNK + CONV_HALO
    halo_per_tile = tm // CONV_HALO

    def body(dh_ref, w_ref, dwc_ref, a_ref, ah_ref, lng_ref, lnb_ref, wd_ref,
             da_ref, dlg_ref, dlb_ref, dbdw_ref, dwd_ref, dbpw1_ref, dbpw2_ref,
             gbuf, dbuf, dglu, dwd_part):
        i = pl.program_id(0)
        r = nt - 1 - i

        @pl.when(i == 0)
        def _():
            dlg_ref[...] = jnp.zeros_like(dlg_ref)
            dlb_ref[...] = jnp.zeros_like(dlb_ref)
            dbdw_ref[...] = jnp.zeros_like(dbdw_ref)
            dbpw1_ref[...] = jnp.zeros_like(dbpw1_ref)
            dbpw2_ref[...] = jnp.zeros_like(dbpw2_ref)
            dwd_part[...] = jnp.zeros_like(dwd_part)
            dbuf[tm:tm + CONV_HALO, :] = jnp.zeros((CONV_HALO, D_MODEL), F32)

        @pl.when(i > 0)
        def _():
            dbuf[tm:tm + CONV_HALO, :] = dbuf[0:CONV_HALO, :]

        dhv = dh_ref[...]
        dbpw2_ref[...] += _sum_rows(dhv)
        dz = _nt(dhv.astype(BF16), w_ref[...])
        xv = dwc_ref[...]
        lg = lng_ref[...]
        mu = jnp.mean(xv, axis=-1, keepdims=True)
        xc = xv - mu
        rstd = lax.rsqrt(jnp.mean(xc * xc, axis=-1, keepdims=True) + LN_EPS)
        xhat = xc * rstd
        ln = xhat * lg + lnb_ref[...]
        sg = _sig(ln)
        dln = dz * (sg * (1.0 + ln * (1.0 - sg)))
        dlg_ref[...] += _sum_rows(dln * xhat)
        dlb_ref[...] += _sum_rows(dln)
        dxh = dln * lg
        ddw = rstd * (dxh - jnp.mean(dxh, axis=-1, keepdims=True)
                      - xhat * jnp.mean(dxh * xhat, axis=-1, keepdims=True))
        dbdw_ref[...] += _sum_rows(ddw)
        dbuf[0:tm, :] = ddw

        av = a_ref[...].astype(F32)
        a1 = av[:, :D_MODEL]
        s2 = _sig(av[:, D_MODEL:])
        gbuf[CONV_HALO:CONV_HALO + tm, :] = a1 * s2
        ah = ah_ref[...].astype(F32)
        gh = ah[:, :D_MODEL] * _sig(ah[:, D_MODEL:])
        gbuf[0:CONV_HALO, :] = jnp.where(r > 0, gh, 0.0)

        def chunk(c, carry):
            r0 = pl.multiple_of(c * CONV_CHUNK, CONV_CHUNK)
            for l in range(D_MODEL // LANES):
                ls = slice(l * LANES, (l + 1) * LANES)
                dw_ = dbuf[pl.ds(r0, win), ls]
                gw = gbuf[pl.ds(r0, win), ls]
                dc = dw_[0:CONV_CHUNK]
                acc = jnp.zeros((CONV_CHUNK, LANES), F32)
                for s in range(SUBLANES):
                    ds_ = dw_ if s == 0 else pltpu.roll(dw_, win - s, 0)
                    gs = gw if s == 0 else pltpu.roll(gw, s, 0)
                    for q in range(CONV_HALO // SUBLANES):
                        d = SUBLANES * q + s
                        if d < CONV_WIDTH:
                            acc = acc + wd_ref[d:d + 1, ls] * ds_[SUBLANES * q:SUBLANES * q + CONV_CHUNK]
                            lo = CONV_HALO - SUBLANES * q
                            prod = dc * gs[lo:lo + CONV_CHUNK]
                            dwd_part[d, :, ls] += jnp.sum(
                                prod.reshape(CONV_CHUNK // SUBLANES, SUBLANES, LANES), axis=0)
                dglu[pl.ds(r0, CONV_CHUNK), ls] = acc
            return carry

        lax.fori_loop(0, n_chunks, chunk, 0)

        dg_ = dglu[...]
        da1 = dg_ * s2
        da2 = dg_ * a1 * s2 * (1.0 - s2)
        da_ref[:, :D_MODEL] = da1.astype(BF16)
        da_ref[:, D_MODEL:] = da2.astype(BF16)
        dbpw1_ref[:, :D_MODEL] += _sum_rows(da1)
        dbpw1_ref[:, D_MODEL:] += _sum_rows(da2)

        @pl.when(i == nt - 1)
        def _():
            dwd_ref[...] = jnp.sum(dwd_part[...], axis=1)

    rev = lambda n: pl.BlockSpec((tm, n), lambda i: (nt - 1 - i, 0))
    halo = pl.BlockSpec((CONV_HALO, 2 * D_MODEL),
                        lambda i: (jnp.maximum((nt - 1 - i) * halo_per_tile - 1, 0), 0))
    vec = lambda n: _full((1, n))
    return _call(
        body, name="conv_bwd", grid=(nt,),
        in_specs=[rev(D_MODEL), _full((D_MODEL, D_MODEL)), rev(D_MODEL), rev(2 * D_MODEL), halo,
                  vec(D_MODEL), vec(D_MODEL), _full((TAPS_PAD, D_MODEL))],
        out_specs=[rev(2 * D_MODEL), vec(D_MODEL), vec(D_MODEL), vec(D_MODEL), _full((TAPS_PAD, D_MODEL)),
                   vec(2 * D_MODEL), vec(D_MODEL)],
        out_shape=[jax.ShapeDtypeStruct((seq, 2 * D_MODEL), BF16), jax.ShapeDtypeStruct((1, D_MODEL), F32),
                   jax.ShapeDtypeStruct((1, D_MODEL), F32), jax.ShapeDtypeStruct((1, D_MODEL), F32),
                   jax.ShapeDtypeStruct((TAPS_PAD, D_MODEL), F32), jax.ShapeDtypeStruct((1, 2 * D_MODEL), F32),
                   jax.ShapeDtypeStruct((1, D_MODEL), F32)],
        scratch=[pltpu.VMEM((tm + CONV_HALO, D_MODEL), F32), pltpu.VMEM((tm + CONV_HALO, D_MODEL), F32),
                 pltpu.VMEM((tm, D_MODEL), F32), pltpu.VMEM((TAPS_PAD, SUBLANES, D_MODEL), F32)],
        sem=("arbitrary",))(dh, wpw2, dwc, a, a, lng, lnb, wd)


def _mm_rms_bwd(name, dact, wt, h_in, dres, g):
    seq, n = dact.shape
    tm = min(TOKEN_TILE, seq)

    def body(da_ref, w_ref, h_ref, dr_ref, g_ref, dx_ref, dg_ref):
        @pl.when(pl.program_id(0) == 0)
        def _():
            dg_ref[...] = jnp.zeros_like(dg_ref)

        dy = jnp.dot(da_ref[...], w_ref[...], preferred_element_type=F32)
        dx, dg = _rms_bwd(dy, h_ref[...], g_ref[...], dr_ref[...])
        dx_ref[...] = dx
        dg_ref[...] += dg

    return _call(
        body, name=name, grid=(seq // tm,),
        in_specs=[_rows(tm, n), _full((n, D_MODEL)), _rows(tm, D_MODEL), _rows(tm, D_MODEL), _full((1, D_MODEL))],
        out_specs=[_rows(tm, D_MODEL), _full((1, D_MODEL))],
        out_shape=[jax.ShapeDtypeStruct((seq, D_MODEL), F32), jax.ShapeDtypeStruct((1, D_MODEL), F32)],
        sem=("arbitrary",))(dact, wt, h_in, dres, g)


def _nt_bias(name, dy, w):
    seq, n = dy.shape
    kdim = w.shape[0]
    tm = min(TOKEN_TILE, seq)

    def body(dy_ref, w_ref, o_ref, db_ref):
        @pl.when(pl.program_id(0) == 0)
        def _():
            db_ref[...] = jnp.zeros_like(db_ref)

        dyv = dy_ref[...]
        db_ref[...] += _sum_rows(dyv)
        o_ref[...] = _nt(dyv.astype(BF16), w_ref[...]).astype(BF16)

    return _call(
        body, name=name, grid=(seq // tm,),
        in_specs=[_rows(tm, n), _full((kdim, n))],
        out_specs=[_rows(tm, kdim), _full((1, n))],
        out_shape=[jax.ShapeDtypeStruct((seq, kdim), BF16), jax.ShapeDtypeStruct((1, n), F32)],
        sem=("arbitrary",))(dy, w)


def _attn_bwd(sink_rows, q, k, v, do):
    seq = q.shape[0]
    cur, prev = _kv_specs()

    def body(sink_ref, q_ref, kc_ref, kp_ref, vc_ref, vp_ref, do_ref,
             dq_ref, dkc_ref, dkp_ref, dvc_ref, dvp_ref, dsink_ref):
        n = pl.program_id(0)

        @pl.when(n == 0)
        def _():
            dsink_ref[...] = jnp.zeros_like(dsink_ref)

        mask = _band_mask(n)
        for j in range(N_KV_HEADS):
            cs = slice(j * HEAD_DIM, (j + 1) * HEAD_DIM)
            kk = _kv_pair(kp_ref, kc_ref, j)
            vv = _kv_pair(vp_ref, vc_ref, j)
            qs = _stack_heads(q_ref, j)
            dos = _stack_heads(do_ref, j)
            probs, p_sink = _softmax_with_sink(_nt(qs, kk), mask, sink_ref[j])
            dp = _nt(dos, vv)
            delta = jnp.sum(probs * dp, axis=-1, keepdims=True)
            dsb = (probs * (dp - delta)).astype(BF16)
            dqs = jnp.dot(dsb, kk, preferred_element_type=F32) * (HEAD_DIM ** -0.5)
            dkk = _tn(dsb, qs)
            dvv = _tn(probs.astype(BF16), dos)
            sink_term = p_sink * delta
            for gq in range(Q_PER_KV):
                rows = slice(gq * BLOCK, (gq + 1) * BLOCK)
                dq_ref[:, _head_slice(j, gq)] = dqs[rows]
                h = j * Q_PER_KV + gq
                dsink_ref[h:h + 1, :] += jnp.broadcast_to(_sum_rows(sink_term[rows]), (1, LANES))
            dkp_ref[:, cs] = dkk[:BLOCK]
            dkc_ref[:, cs] = dkk[BLOCK:]
            dvp_ref[:, cs] = dvv[:BLOCK]
            dvc_ref[:, cs] = dvv[BLOCK:]

    kv_out = _rows(BLOCK, KV_DIM)
    kv_shape = jax.ShapeDtypeStruct((seq, KV_DIM), F32)
    return _call(
        body, name="attn_bwd", grid=(seq // BLOCK,),
        in_specs=[_full((N_KV_HEADS, GROUP_ROWS, 1)), _rows(BLOCK, Q_DIM), cur, prev, cur, prev,
                  _rows(BLOCK, Q_DIM)],
        out_specs=[_rows(BLOCK, Q_DIM), kv_out, kv_out, kv_out, kv_out, _full((N_Q_HEADS, LANES))],
        out_shape=[jax.ShapeDtypeStruct((seq, Q_DIM), F32), kv_shape, kv_shape, kv_shape, kv_shape,
                   jax.ShapeDtypeStruct((N_Q_HEADS, LANES), F32)],
        sem=("arbitrary",))(sink_rows, q, k, k, v, v, do)


def _rope_bwd(dq, dkc, dkp, dvc, dvp, rc, rsa, rsb):
    seq = dq.shape[0]
    nb = seq // BLOCK
    nxt = pl.BlockSpec((BLOCK, KV_DIM), lambda n: (jnp.minimum(n + 1, nb - 1), 0))

    def body(dq_ref, dkc_ref, dkp_ref, dvc_ref, dvp_ref, c_ref, sa_ref, sb_ref, o_ref, db_ref):
        n = pl.program_id(0)

        @pl.when(n == 0)
        def _():
            db_ref[...] = jnp.zeros_like(db_ref)

        c, sa, sb = c_ref[...], sa_ref[...], sb_ref[...]
        last = n == nb - 1
        dk = dkc_ref[...] + jnp.where(last, 0.0, dkp_ref[...])
        dv = dvc_ref[...] + jnp.where(last, 0.0, dvp_ref[...])
        for i in range(Q_DIM // LANES):
            ls = slice(i * LANES, (i + 1) * LANES)
            blk = _rope_t(dq_ref[:, ls], c, sa, sb)
            o_ref[:, ls] = blk.astype(BF16)
            db_ref[:, ls] += _sum_rows(blk)
        dkr = _rope_t(dk, c, sa, sb)
        o_ref[:, Q_DIM:Q_DIM + KV_DIM] = dkr.astype(BF16)
        db_ref[:, Q_DIM:Q_DIM + KV_DIM] += _sum_rows(dkr)
        o_ref[:, Q_DIM + KV_DIM:] = dv.astype(BF16)
        db_ref[:, Q_DIM + KV_DIM:] += _sum_rows(dv)

    kv = _rows(BLOCK, KV_DIM)
    tab = _rows(BLOCK, LANES)
    return _call(
        body, name="rope_bwd", grid=(nb,),
        in_specs=[_rows(BLOCK, Q_DIM), kv, nxt, kv, nxt, tab, tab, tab],
        out_specs=[_rows(BLOCK, QKV_DIM), _full((1, QKV_DIM))],
        out_shape=[jax.ShapeDtypeStruct((seq, QKV_DIM), BF16), jax.ShapeDtypeStruct((1, QKV_DIM), F32)],
        sem=("arbitrary",))(dq, dkc, dkp, dvc, dvp, rc, rsa, rsb)


def _adamw(w, g, m, v):
    m = ADAM_B1 * m + (1.0 - ADAM_B1) * g
    v = ADAM_B2 * v + (1.0 - ADAM_B2) * (g * g)
    m_hat = m / (1.0 - ADAM_B1 ** ADAM_STEP)
    v_hat = v / (1.0 - ADAM_B2 ** ADAM_STEP)
    delta = -ADAM_LR * (m_hat / (jnp.sqrt(v_hat) + ADAM_EPS) + ADAM_WD * w)
    return delta, m, v


def _sum_slots(name, parts):
    _, rows, cols = parts.shape
    tr = rows if rows <= 512 else ROWS_FF

    def body(p_ref, g_ref):
        g = p_ref[0].astype(F32)
        for d in range(1, N_DEV):
            g = g + p_ref[d].astype(F32)
        g_ref[...] = g

    return _call(
        body, name=name, grid=(rows // tr,),
        in_specs=[pl.BlockSpec((N_DEV, tr, cols), lambda i: (0, i, 0))],
        out_specs=_rows(tr, cols), out_shape=jax.ShapeDtypeStruct((rows, cols), F32),
        sem=("parallel",))(parts)


def _adamw_native(name, g, w, m, v):
    layers, rows, cols = w.shape
    tr = rows if rows <= 512 else 256

    def body(g_ref, w_ref, m_ref, v_ref, d_ref, nm_ref, nv_ref):
        d_ref[...], nm_ref[...], nv_ref[...] = _adamw(w_ref[...], g_ref[...], m_ref[...], v_ref[...])

    spec = pl.BlockSpec((1, tr, cols), lambda l, i: (l, i, 0))
    shape = jax.ShapeDtypeStruct(w.shape, F32)
    return _call(
        body, name=name, grid=(layers, rows // tr), in_specs=[spec, spec, spec, spec],
        out_specs=[spec, spec, spec], out_shape=[shape, shape, shape],
        sem=("parallel", "parallel"))(g, w, m, v)


def _adamw_replicated(parts, w, m, v):
    def body(p_ref, w_ref, m_ref, v_ref, g_ref, d_ref, nm_ref, nv_ref):
        g = p_ref[0]
        for j in range(1, N_DEV):
            g = g + p_ref[j]
        g_ref[...] = g
        d_ref[...], nm_ref[...], nv_ref[...] = _adamw(w_ref[...], g, m_ref[...], v_ref[...])

    spec = _full((REPL_ROWS, D_MODEL))
    shape = jax.ShapeDtypeStruct((REPL_ROWS, D_MODEL), F32)
    return _call(
        body, name="adamw_replicated", grid=(1,),
        in_specs=[_full((N_DEV, REPL_ROWS, D_MODEL)), spec, spec, spec],
        out_specs=[spec, spec, spec, spec], out_shape=[shape, shape, shape, shape],
        sem=("arbitrary",))(parts, w, m, v)


_MESH = pl.DeviceIdType.MESH
_ANY = pl.BlockSpec(memory_space=pl.ANY)


def _all_gather(name, xs):
    rows, cols = xs.shape

    def body(x_ref, out_ref, send_sems, recv_sems, local_sem):
        x, y, c = lax.axis_index("x"), lax.axis_index("y"), lax.axis_index("c")
        me, sibling = (x, y, c), (x, y, 1 - c)
        chips = [(1 - x, y), (x, 1 - y), (1 - x, 1 - y)]

        def slot(px, py, pc):
            return out_ref.at[4 * px + 2 * py + pc]

        def copy(k, block, to, src=None):
            return pltpu.make_async_remote_copy(
                src_ref=slot(*block) if src is None else src, dst_ref=slot(*block),
                send_sem=send_sems.at[k], recv_sem=recv_sems.at[k], device_id=to, device_id_type=_MESH)

        mine = pltpu.make_async_copy(x_ref, slot(*me), local_sem)
        mine.start()
        first = [copy(0, me, sibling, src=x_ref)]
        first += [copy(1 + j, me, (*chip, c), src=x_ref) for j, chip in enumerate(chips)]
        for cp in first:
            cp.start()
        passed = [copy(4 + j, (*chip, c), sibling) for j, chip in enumerate(chips)]
        for j, chip in enumerate(chips):
            copy(1 + j, (*chip, c), me).wait_recv()
            passed[j].start()
        copy(0, sibling, me).wait_recv()
        for j, chip in enumerate(chips):
            copy(4 + j, (*chip, 1 - c), me).wait_recv()
        for cp in first + passed:
            cp.wait_send()
        mine.wait()

    return pl.pallas_call(
        body, name=name, out_shape=jax.ShapeDtypeStruct((N_DEV, rows, cols), xs.dtype),
        in_specs=[_ANY], out_specs=_ANY,
        scratch_shapes=[pltpu.SemaphoreType.DMA((7,)), pltpu.SemaphoreType.DMA((7,)), pltpu.SemaphoreType.DMA],
    )(xs)


N_PEERS = N_DEV - 1
_HBM = pl.BlockSpec(memory_space=pltpu.HBM)
_SEM = pl.BlockSpec(memory_space=pltpu.SEMAPHORE)
_DATAFLOW = pltpu.SideEffectType.DATAFLOW_SIDE_EFFECTING
_TOKEN = jax.ShapeDtypeStruct((SUBLANES, LANES), F32)


def _peers():
    x, y, c = lax.axis_index("x"), lax.axis_index("y"), lax.axis_index("c")
    out = []
    for k in range(1, N_DEV):
        px = 1 - x if k & 4 else x
        py = 1 - y if k & 2 else y
        pc = 1 - c if k & 1 else c
        out.append(((px, py, pc), 4 * px + 2 * py + pc))
    return 4 * x + 2 * y + c, out


def _in_hbm(a):
    return pltpu.with_memory_space_constraint(a, pltpu.HBM)


def _landing(rows):
    return _in_hbm(lax.empty((N_DEV, rows, D_MODEL), BF16))


def _sem_pair():
    return pltpu.SemaphoreType.DMA((N_PEERS,)), pltpu.SemaphoreType.DMA((N_PEERS,))


def _gather_start(payloads):
    n = len(payloads)

    def body(*refs):
        src, land = refs[:n], refs[n:2 * n]
        sems = refs[2 * n:4 * n]
        token = refs[-1]
        me, peers = _peers()
        for g in range(n):
            for k, (pos, _) in enumerate(peers):
                pltpu.make_async_remote_copy(
                    src_ref=src[g], dst_ref=land[g].at[me], send_sem=sems[2 * g].at[k],
                    recv_sem=sems[2 * g + 1].at[k], device_id=pos, device_id_type=_MESH).start()
        token[...] = jnp.zeros_like(token)

    lands = [_landing(p.shape[0]) for p in payloads]
    sem_shapes = [s for _ in payloads for s in _sem_pair()]
    hbm_shapes = [pltpu.HBM(a.shape, a.dtype) for a in list(payloads) + lands]
    out = pl.pallas_call(
        body, name="ag_start", out_shape=(*sem_shapes, *hbm_shapes, _TOKEN),
        in_specs=[_HBM] * (2 * n), out_specs=(*[_SEM] * (2 * n), *[_HBM] * (2 * n), pl.BlockSpec(memory_space=pltpu.VMEM)),
        input_output_aliases={i: 2 * n + i for i in range(2 * n)},
        compiler_params=pltpu.CompilerParams(has_side_effects=_DATAFLOW),
    )(*[_in_hbm(p) for p in payloads], *lands)
    sems, thru = out[:2 * n], out[2 * n:4 * n]
    return [(thru[g], thru[n + g], sems[2 * g], sems[2 * g + 1]) for g in range(n)]


def _gather_wait(name, group, after):
    payload, land, send_sems, recv_sems = group

    def body(src_ref, land_ref, send_ref, recv_ref, after_ref, src_out, land_out):
        _, peers = _peers()
        for k, (pos, idx) in enumerate(peers):
            cp = pltpu.make_async_remote_copy(
                src_ref=src_ref, dst_ref=land_ref.at[idx], send_sem=send_ref.at[k], recv_sem=recv_ref.at[k],
                device_id=pos, device_id_type=_MESH)
            cp.wait_send()
            cp.wait_recv()

    _, land = pl.pallas_call(
        body, name=name, out_shape=(pltpu.HBM(payload.shape, payload.dtype), pltpu.HBM(land.shape, land.dtype)),
        in_specs=[_HBM, _HBM, _SEM, _SEM, _ANY], out_specs=(_HBM, _HBM), input_output_aliases={0: 0, 1: 1},
        compiler_params=pltpu.CompilerParams(has_side_effects=_DATAFLOW),
    )(payload, land, send_sems, recv_sems, after)
    me = 4 * lax.axis_index("x") + 2 * lax.axis_index("y") + lax.axis_index("c")
    return lax.dynamic_update_slice(land, payload[None], (me, 0, 0))


def _scatter_start(name, blocks):
    rows = blocks.shape[1]

    def body(blocks_ref, land_ref, send_sems, recv_sems, blocks_out, land_out, token):
        me, peers = _peers()
        for k, (pos, idx) in enumerate(peers):
            pltpu.make_async_remote_copy(
                src_ref=blocks_ref.at[idx], dst_ref=land_ref.at[me], send_sem=send_sems.at[k],
                recv_sem=recv_sems.at[k], device_id=pos, device_id_type=_MESH).start()
        token[...] = jnp.zeros_like(token)

    land = _landing(rows)
    send_sems, recv_sems, blocks_thru, land_thru, token = pl.pallas_call(
        body, name=name,
        out_shape=(*_sem_pair(), pltpu.HBM(blocks.shape, blocks.dtype), pltpu.HBM(land.shape, land.dtype), _TOKEN),
        in_specs=[_HBM, _HBM], out_specs=(_SEM, _SEM, _HBM, _HBM, pl.BlockSpec(memory_space=pltpu.VMEM)),
        input_output_aliases={0: 2, 1: 3},
        compiler_params=pltpu.CompilerParams(has_side_effects=_DATAFLOW),
    )(_in_hbm(blocks), land)
    return (blocks_thru, land_thru, send_sems, recv_sems), token


def _scatter_wait(groups, after):
    n = len(groups)

    def body(*refs):
        blocks, land = refs[:n], refs[n:2 * n]
        sems = refs[2 * n:4 * n]
        _, peers = _peers()
        for g in range(n):
            for k, (pos, idx) in enumerate(peers):
                cp = pltpu.make_async_remote_copy(
                    src_ref=blocks[g].at[idx], dst_ref=land[g].at[idx], send_sem=sems[2 * g].at[k],
                    recv_sem=sems[2 * g + 1].at[k], device_id=pos, device_id_type=_MESH)
                cp.wait_send()
                cp.wait_recv()

    hbm = [grp[0] for grp in groups] + [grp[1] for grp in groups]
    sems = [s for grp in groups for s in grp[2:]]
    out = pl.pallas_call(
        body, name="rs_wait", out_shape=tuple(pltpu.HBM(a.shape, a.dtype) for a in hbm),
        in_specs=[_HBM] * (2 * n) + [_SEM] * (2 * n) + [_ANY], out_specs=tuple([_HBM] * (2 * n)),
        input_output_aliases={i: i for i in range(2 * n)},
        compiler_params=pltpu.CompilerParams(has_side_effects=_DATAFLOW),
    )(*hbm, *sems, after)
    me = 4 * lax.axis_index("x") + 2 * lax.axis_index("y") + lax.axis_index("c")
    lands = []
    for g in range(n):
        own = lax.dynamic_index_in_dim(out[g], me, axis=0, keepdims=True)
        lands.append(lax.dynamic_update_slice(out[n + g], own, (me, 0, 0)))
    return lands


def _pad_rows(flat, rows):
    return jnp.pad(flat, (0, rows * D_MODEL - flat.shape[0])).reshape(rows, D_MODEL)


SMALL_NAMES = ("conv_b_pw1", "conv_w_dw", "conv_b_dw", "conv_ln_g", "conv_ln_b", "conv_b_pw2")


def _pack_small(p):
    flat = jnp.concatenate([p[n].reshape(-1) for n in SMALL_NAMES])
    return _pad_rows(flat, ROWS_SMALL).reshape(1, ROWS_SMALL, D_MODEL)


def _unpack_small(packed):
    flat = packed.reshape(-1)
    c = D_MODEL // N_DEV
    shapes = ((1, 2 * c), (1, CONV_WIDTH, c), (1, c), (1, c), (1, c), (1, c))
    out, o = {}, 0
    for n, shape in zip(SMALL_NAMES, shapes):
        size = shape[-1] * (shape[1] if len(shape) == 3 else 1)
        out[n] = flat[o:o + size].reshape(shape)
        o += size
    return out


def _gather_payloads(p):
    t = lambda a: jnp.swapaxes(a, -1, -2).astype(BF16)
    w1t, w3t, w2 = t(p["ffn_w1"]), t(p["ffn_w3"]), p["ffn_w2"].astype(BF16)
    small = _pack_small(p).reshape(-1)[:ROWS_SMALL * D_MODEL // 2]
    small = lax.bitcast_convert_type(small, BF16).reshape(ROWS_SMALL, D_MODEL)
    attn = jnp.concatenate([t(p["attn_w_qkv"][0]), p["attn_w_o"][0].astype(BF16)], axis=0)
    conv = jnp.concatenate([t(p["conv_w_pw1"][0]), p["conv_w_pw2"][0].astype(BF16), small], axis=0)
    ffn = [jnp.concatenate([w1t[l], w3t[l], w2[l]], axis=0) for l in range(2)]
    return [attn, ffn[0], conv, ffn[1]]


def _device_rows(land, lo, n):
    return land[:, lo:lo + n].reshape(N_DEV * n, D_MODEL)


def _unpack_conv(land):
    small = lax.bitcast_convert_type(
        land[:, ROWS_PW1 + ROWS_PW2:].reshape(N_DEV, ROWS_SMALL * D_MODEL // 2, 2), F32)
    c = D_MODEL // N_DEV
    b_pw1 = small[:, :2 * c].reshape(1, 2 * D_MODEL)
    s = 2 * c
    w_dw = small[:, s:s + CONV_WIDTH * c].reshape(N_DEV, CONV_WIDTH, c).transpose(1, 0, 2).reshape(CONV_WIDTH, D_MODEL)
    s += CONV_WIDTH * c
    b_dw, ln_g, ln_b, b_pw2 = (small[:, s + i * c:s + (i + 1) * c].reshape(1, D_MODEL) for i in range(4))
    return dict(w_pw1_t=_device_rows(land, 0, ROWS_PW1), w_pw2=_device_rows(land, ROWS_PW1, ROWS_PW2),
                b_pw1=b_pw1, w_dw=w_dw, b_dw=b_dw, ln_g=ln_g, ln_b=ln_b, b_pw2=b_pw2)


def _dest_blocks(mats):
    return jnp.concatenate([a.reshape(N_DEV, -1, D_MODEL) for a in mats], axis=1)


def _small_grad_rows(g_bpw1, g_dw, g_bdw, g_lng, g_lnb, g_bpw2):
    c = D_MODEL // N_DEV
    small = jnp.concatenate(
        [g_bpw1.reshape(N_DEV, 2 * c), g_dw.reshape(CONV_WIDTH, N_DEV, c).transpose(1, 0, 2).reshape(N_DEV, -1),
         g_bdw.reshape(N_DEV, c), g_lng.reshape(N_DEV, c), g_lnb.reshape(N_DEV, c), g_bpw2.reshape(N_DEV, c)], axis=1)
    small = jnp.pad(small, ((0, 0), (0, ROWS_SMALL * D_MODEL - SMALL_USED)))
    return small.reshape(N_DEV * ROWS_SMALL, D_MODEL).astype(BF16)


def _pack_replicated(norm_mix, norm_ffn, b_qkv, sinks, b_o, norm_final):
    rows = [norm_mix.reshape(2, D_MODEL), norm_ffn.reshape(2, D_MODEL), _pad_rows(b_qkv.reshape(-1), 2),
            _pad_rows(sinks.reshape(-1), 1), b_o.reshape(1, D_MODEL), norm_final.reshape(1, D_MODEL)]
    p = jnp.concatenate(rows, axis=0)
    return jnp.pad(p, ((0, REPL_ROWS - p.shape[0]), (0, 0)))


def _unpack_replicated(p):
    return dict(norm_mix=p[0:2], norm_ffn=p[2:4], attn_b_qkv=p[4:6].reshape(-1)[:QKV_DIM].reshape(1, QKV_DIM),
                attn_sinks=p[6, :N_Q_HEADS].reshape(1, N_Q_HEADS), attn_b_o=p[7:8], norm_final=p[8])


WEIGHT_ORDER = ['norm_mix', 'norm_ffn', 'attn_w_qkv', 'attn_b_qkv', 'attn_sinks', 'attn_w_o', 'attn_b_o',
                'conv_w_pw1', 'conv_b_pw1', 'conv_w_dw', 'conv_b_dw', 'conv_ln_g', 'conv_ln_b', 'conv_w_pw2',
                'conv_b_pw2', 'ffn_w1', 'ffn_w3', 'ffn_w2', 'norm_final']


def kernel(x, norm_mix, norm_ffn, attn_w_qkv, attn_b_qkv, attn_sinks, attn_w_o, attn_b_o, conv_w_pw1, conv_b_pw1, conv_w_dw, conv_b_dw, conv_ln_g, conv_ln_b, conv_w_pw2, conv_b_pw2, ffn_w1, ffn_w3, ffn_w2, norm_final, loss_target, m_norm_mix, m_norm_ffn, m_attn_w_qkv, m_attn_b_qkv, m_attn_sinks, m_attn_w_o, m_attn_b_o, m_conv_w_pw1, m_conv_b_pw1, m_conv_w_dw, m_conv_b_dw, m_conv_ln_g, m_conv_ln_b, m_conv_w_pw2, m_conv_b_pw2, m_ffn_w1, m_ffn_w3, m_ffn_w2, m_norm_final, v_norm_mix, v_norm_ffn, v_attn_w_qkv, v_attn_b_qkv, v_attn_sinks, v_attn_w_o, v_attn_b_o, v_conv_w_pw1, v_conv_b_pw1, v_conv_w_dw, v_conv_b_dw, v_conv_ln_g, v_conv_ln_b, v_conv_w_pw2, v_conv_b_pw2, v_ffn_w1, v_ffn_w3, v_ffn_w2, v_norm_final):
    xs = x[0]
    target = loss_target[0]
    seq = xs.shape[0]
    my_x, my_y, my_c = lax.axis_index("x"), lax.axis_index("y"), lax.axis_index("c")

    w = dict(attn_w_qkv=attn_w_qkv, attn_w_o=attn_w_o, conv_w_pw1=conv_w_pw1, conv_b_pw1=conv_b_pw1,
             conv_w_dw=conv_w_dw, conv_b_dw=conv_b_dw, conv_ln_g=conv_ln_g, conv_ln_b=conv_ln_b,
             conv_w_pw2=conv_w_pw2, conv_b_pw2=conv_b_pw2, ffn_w1=ffn_w1, ffn_w3=ffn_w3, ffn_w2=ffn_w2)
    m = dict(attn_w_qkv=m_attn_w_qkv, attn_w_o=m_attn_w_o, conv_w_pw1=m_conv_w_pw1, conv_b_pw1=m_conv_b_pw1,
             conv_w_dw=m_conv_w_dw, conv_b_dw=m_conv_b_dw, conv_ln_g=m_conv_ln_g, conv_ln_b=m_conv_ln_b,
             conv_w_pw2=m_conv_w_pw2, conv_b_pw2=m_conv_b_pw2, ffn_w1=m_ffn_w1, ffn_w3=m_ffn_w3, ffn_w2=m_ffn_w2)
    v = dict(attn_w_qkv=v_attn_w_qkv, attn_w_o=v_attn_w_o, conv_w_pw1=v_conv_w_pw1, conv_b_pw1=v_conv_b_pw1,
             conv_w_dw=v_conv_w_dw, conv_b_dw=v_conv_b_dw, conv_ln_g=v_conv_ln_g, conv_ln_b=v_conv_ln_b,
             conv_w_pw2=v_conv_w_pw2, conv_b_pw2=v_conv_b_pw2, ffn_w1=v_ffn_w1, ffn_w3=v_ffn_w3, ffn_w2=v_ffn_w2)

    ag_attn, ag_ffn0, ag_conv, ag_ffn1 = _gather_start(_gather_payloads(w))
    rc, rsa, rsb = _rope_tables(seq)
    sink_rows = _sink_rows(attn_sinks)
    g_mix0, g_mix1 = norm_mix[0:1], norm_mix[1:2]
    g_ffn0, g_ffn1 = norm_ffn[0:1], norm_ffn[1:2]
    g_fin = norm_final.reshape(1, D_MODEL)

    w_attn = _gather_wait("ag_wait_attn", ag_attn, xs)
    w_qkv_t = _device_rows(w_attn, 0, ROWS_QKV)
    w_o = _device_rows(w_attn, ROWS_QKV, ROWS_WO)
    y0, q, k, vv = _qkv_fwd(xs, g_mix0, w_qkv_t, attn_b_qkv, rc, rsa, rsb)
    attn = _attn_fwd(attn_sinks.reshape(N_Q_HEADS), q, k, vv)
    h1 = _mm_res("attn_out_proj", attn, w_o, attn_b_o, xs)
    w_ffn0 = _gather_wait("ag_wait_ffn0", ag_ffn0, h1)
    f0, u0, p0, s0 = _ffn_up("ffn0_up", h1, g_ffn0, w_ffn0)
    h2 = _ffn_down("ffn0_down", s0, w_ffn0, h1)
    wt = _unpack_conv(_gather_wait("ag_wait_conv", ag_conv, h2))
    wd = jnp.concatenate([wt["w_dw"][::-1], jnp.zeros((TAPS_PAD - CONV_WIDTH, D_MODEL), F32)], axis=0)
    y1, a, dwc, z = _conv_fwd(h2, g_mix1, wt["w_pw1_t"], wt["b_pw1"], wd, wt["b_dw"], wt["ln_g"], wt["ln_b"])
    h3 = _mm_res("conv_out_proj", z, wt["w_pw2"], wt["b_pw2"], h2)
    w_ffn1 = _gather_wait("ag_wait_ffn1", ag_ffn1, h3)
    f1, u1, p1, s1 = _ffn_up("ffn1_up", h3, g_ffn1, w_ffn1)
    h4 = _ffn_down("ffn1_down", s1, w_ffn1, h3)
    dh4, sq, dg_fin = _final_loss(h4, target, g_fin)
    loss = lax.psum(sq[0, 0] * (0.5 / D_MODEL), ("x", "y", "c"))

    du1, dp1 = _ffn_bwd_act("ffn1_bwd_act", dh4, u1, p1, w_ffn1)
    dh3, dg_ffn1 = _ffn_bwd_in("ffn1_bwd_in", du1, dp1, w_ffn1, h3, dh4, g_ffn1)
    gw2_1 = _mm_tn("ffn1_dw2", s1, dh4, tk=FF_TILE)
    gw1t_1 = _mm_tn("ffn1_dw1", du1, f1, tk=FF_TILE)
    gw3t_1 = _mm_tn("ffn1_dw3", dp1, f1, tk=FF_TILE)
    rs_ffn1, tok = _scatter_start("rs_start_ffn1", _dest_blocks([gw1t_1, gw3t_1, gw2_1]))

    da, dlg, dlb, dbdw, dwd, dbpw1, dbpw2 = _conv_bwd(dh3, wt["w_pw2"], dwc, a, wt["ln_g"] + tok[0, 0],
                                                     wt["ln_b"], wd)
    gpw2 = _mm_tn("conv_dw_pw2", z, dh3, tk=D_MODEL)
    dh2, dg_mix1 = _mm_rms_bwd("conv_in_bwd", da, wt["w_pw1_t"], h2, dh3, g_mix1)
    gpw1t = _mm_tn("conv_dw_pw1", da, y1, tk=D_MODEL)
    small_rows = _small_grad_rows(dbpw1, dwd[:CONV_WIDTH][::-1], dbdw, dlg, dlb, dbpw2)
    rs_conv, tok = _scatter_start("rs_start_conv", _dest_blocks([gpw1t, gpw2, small_rows]))

    du0, dp0 = _ffn_bwd_act("ffn0_bwd_act", dh2, u0, p0, w_ffn0)
    dh1, dg_ffn0 = _ffn_bwd_in("ffn0_bwd_in", du0, dp0, w_ffn0, h1, dh2, g_ffn0 + tok[0, 0])
    gw2_0 = _mm_tn("ffn0_dw2", s0, dh2, tk=FF_TILE)
    gw1t_0 = _mm_tn("ffn0_dw1", du0, f0, tk=FF_TILE)
    gw3t_0 = _mm_tn("ffn0_dw3", dp0, f0, tk=FF_TILE)
    rs_ffn0, tok = _scatter_start("rs_start_ffn0", _dest_blocks([gw1t_0, gw3t_0, gw2_0]))

    dattn, dbo = _nt_bias("attn_out_bwd", dh1, w_o)
    gwo = _mm_tn("attn_dw_o", attn, dh1, tk=D_MODEL)
    dq, dkc, dkp, dvc, dvp, dsink = _attn_bwd(sink_rows + tok[0, 0], q, k, vv, dattn)
    dqkv, dbqkv = _rope_bwd(dq, dkc, dkp, dvc, dvp, rc, rsa, rsb)
    dx, dg_mix0 = _mm_rms_bwd("qkv_in_bwd", dqkv, w_qkv_t, xs, dh1, g_mix0)
    gqkvt = _mm_tn("attn_dw_qkv", dqkv, y0, tk=QKV_DIM)
    rs_attn, _ = _scatter_start("rs_start_attn", _dest_blocks([gqkvt, gwo]))

    p_ffn1, p_conv, p_ffn0, p_attn = _scatter_wait([rs_ffn1, rs_conv, rs_ffn0, rs_attn], dx)
    g_ffn = [_sum_slots("rs_sum_ffn0", p_ffn0), _sum_slots("rs_sum_ffn1", p_ffn1)]
    g_conv = _sum_slots("rs_sum_conv", p_conv)
    g_attn = _sum_slots("rs_sum_attn", p_attn)

    def ff(slot, transpose):
        blk = jnp.stack([g[slot * ROWS_FF:(slot + 1) * ROWS_FF] for g in g_ffn])
        return jnp.swapaxes(blk, 1, 2) if transpose else blk

    grads = dict(
        ffn_w1=ff(W1T_SLOT, True), ffn_w3=ff(W3T_SLOT, True), ffn_w2=ff(W2_SLOT, False),
        attn_w_qkv=g_attn[:ROWS_QKV].T[None], attn_w_o=g_attn[ROWS_QKV:][None],
        conv_w_pw1=g_conv[:ROWS_PW1].T[None], conv_w_pw2=g_conv[ROWS_PW1:ROWS_PW1 + ROWS_PW2][None])
    sharded = [dict(grads), {}, {}, {}]
    for n in grads:
        for dst, t in zip(sharded[1:], _adamw_native("adamw_" + n, grads[n], w[n], m[n], v[n])):
            dst[n] = t
    g_small = g_conv[ROWS_PW1 + ROWS_PW2:][None]
    small_out = _adamw_native("adamw_small", g_small, _pack_small(w), _pack_small(m), _pack_small(v))
    for dst, t in zip(sharded, (g_small,) + tuple(small_out)):
        dst.update(_unpack_small(t))

    part = _pack_replicated(jnp.concatenate([dg_mix0, dg_mix1]), jnp.concatenate([dg_ffn0, dg_ffn1]),
                            dbqkv, -dsink[:, 0], dbo, dg_fin)
    parts = _all_gather("ag_replicated_grads", part)
    w_rep = _pack_replicated(norm_mix, norm_ffn, attn_b_qkv, attn_sinks, attn_b_o, norm_final)
    m_rep = _pack_replicated(m_norm_mix, m_norm_ffn, m_attn_b_qkv, m_attn_sinks, m_attn_b_o, m_norm_final)
    v_rep = _pack_replicated(v_norm_mix, v_norm_ffn, v_attn_b_qkv, v_attn_sinks, v_attn_b_o, v_norm_final)
    replicated = [_unpack_replicated(t) for t in _adamw_replicated(parts, w_rep, m_rep, v_rep)]

    outs = [loss, dx.reshape(1, seq, D_MODEL)]
    for sh, rp in zip(sharded, replicated):
        merged = {**sh, **rp}
        outs += [merged[n] for n in WEIGHT_ORDER]
    return tuple(outs)
```

```python
import jax
import jax.numpy as jnp
from jax import lax
from jax.experimental import pallas as pl
from jax.experimental.pallas import tpu as pltpu

F32 = jnp.float32
BF16 = jnp.bfloat16

D_MODEL = 1024
HEAD_DIM = 64
N_Q_HEADS = 16
N_KV_HEADS = 2
Q_PER_KV = 8
Q_DIM = N_Q_HEADS * HEAD_DIM
KV_DIM = N_KV_HEADS * HEAD_DIM
QKV_DIM = Q_DIM + 2 * KV_DIM
BLOCK = 128
CONV_WIDTH = 31
D_FF = 2816
ROPE_THETA = 10000.0
RMS_EPS = 1e-5
LN_EPS = 1e-5
ADAM_LR = 0.001
ADAM_B1 = 0.9
ADAM_B2 = 0.999
ADAM_EPS = 1e-08
ADAM_WD = 0.01
ADAM_STEP = 10
N_DEV = 8

LANES = 128
SUBLANES = 8
TOKEN_TILE = 512
CONV_CHUNK = 64
CONV_HALO = 32
TAPS_PAD = 32
VMEM_LIMIT = 56 * 1024 * 1024
NEG_INF = float(jnp.finfo(jnp.float32).min)

ROWS_FF = D_FF // N_DEV
W1T_SLOT, W3T_SLOT, W2_SLOT = 0, 1, 2
ROWS_QKV = QKV_DIM // N_DEV
ROWS_WO = Q_DIM // N_DEV
ROWS_PW1 = 2 * D_MODEL // N_DEV
ROWS_PW2 = D_MODEL // N_DEV
ROWS_SMALL = 16
SMALL_USED = 2 * D_MODEL // N_DEV + CONV_WIDTH * (D_MODEL // N_DEV) + 4 * (D_MODEL // N_DEV)
FF_SPLIT = 2
FF_TILE_DEVS = N_DEV // FF_SPLIT
FF_TILE = FF_TILE_DEVS * ROWS_FF
REPL_ROWS = 16


def _call(body, *, name, grid, in_specs, out_specs, out_shape, scratch=(), sem=None):
    return pl.pallas_call(
        body, name=name, grid=grid, in_specs=in_specs, out_specs=out_specs, out_shape=out_shape,
        scratch_shapes=list(scratch),
        compiler_params=pltpu.CompilerParams(dimension_semantics=sem, vmem_limit_bytes=VMEM_LIMIT))


def _full(shape):
    return pl.BlockSpec(shape, lambda *_: (0,) * len(shape))


def _resident(shape):
    return pl.BlockSpec(shape, lambda *_: (0,) * len(shape), pipeline_mode=pl.Buffered(1))


def _rows(tm, n):
    return pl.BlockSpec((tm, n), lambda i, *_: (i, 0))


def _sig(x):
    return 1.0 / (1.0 + jnp.exp(-x))


def _sum_rows(x):
    return jnp.sum(x, axis=0, keepdims=True)


def _nt(a, b):
    return lax.dot_general(a, b, (((1,), (1,)), ((), ())), preferred_element_type=F32)


def _tn(a, b):
    return lax.dot_general(a, b, (((0,), (0,)), ((), ())), preferred_element_type=F32)


def _rms_stats(x):
    return lax.rsqrt(jnp.mean(x * x, axis=-1, keepdims=True) + RMS_EPS)


def _rms_bwd(dy, x, g, dres):
    r = _rms_stats(x)
    n = x * r
    dn = dy * g
    dx = dres + r * (dn - n * jnp.mean(dn * n, axis=-1, keepdims=True))
    return dx, _sum_rows(dy * n)


def _rope_tables(seq):
    pos = jnp.arange(seq, dtype=F32)
    inv_freq = ROPE_THETA ** (-jnp.arange(0, HEAD_DIM, 2, dtype=F32) / HEAD_DIM)
    ang = pos[:, None] * inv_freq[None, :]
    cos, sin = jnp.cos(ang), jnp.sin(ang)
    zero = jnp.zeros_like(sin)
    reps = LANES // HEAD_DIM
    c = jnp.tile(jnp.concatenate([cos, cos], axis=1), (1, reps))
    sa = jnp.tile(jnp.concatenate([-sin, zero], axis=1), (1, reps))
    sb = jnp.tile(jnp.concatenate([zero, sin], axis=1), (1, reps))
    return c, sa, sb


def _rope(t, c, sa, sb):
    half = HEAD_DIM // 2
    return t * c + pltpu.roll(t, LANES - half, 1) * sa + pltpu.roll(t, half, 1) * sb


def _rope_t(dt, c, sa, sb):
    half = HEAD_DIM // 2
    return dt * c + pltpu.roll(dt * sa, half, 1) + pltpu.roll(dt * sb, LANES - half, 1)


def _qkv_fwd(x, g, w, b, rc, rsa, rsb):
    seq = x.shape[0]
    tm = min(TOKEN_TILE, seq)

    def body(x_ref, g_ref, w_ref, b_ref, c_ref, sa_ref, sb_ref, y_ref, q_ref, k_ref, v_ref):
        xv = x_ref[...]
        y = (xv * _rms_stats(xv) * g_ref[...]).astype(BF16)
        y_ref[...] = y
        qkv = _nt(y, w_ref[...]) + b_ref[...]
        c, sa, sb = c_ref[...], sa_ref[...], sb_ref[...]
        for i in range(Q_DIM // LANES):
            blk = _rope(qkv[:, i * LANES:(i + 1) * LANES], c, sa, sb)
            q_ref[:, i * LANES:(i + 1) * LANES] = (blk * (HEAD_DIM ** -0.5)).astype(BF16)
        k_ref[...] = _rope(qkv[:, Q_DIM:Q_DIM + KV_DIM], c, sa, sb).astype(BF16)
        v_ref[...] = qkv[:, Q_DIM + KV_DIM:].astype(BF16)

    return _call(
        body, name="qkv_fwd", grid=(seq // tm,),
        in_specs=[_rows(tm, D_MODEL), _full((1, D_MODEL)), _full((QKV_DIM, D_MODEL)), _full((1, QKV_DIM)),
                  _rows(tm, LANES), _rows(tm, LANES), _rows(tm, LANES)],
        out_specs=[_rows(tm, D_MODEL), _rows(tm, Q_DIM), _rows(tm, KV_DIM), _rows(tm, KV_DIM)],
        out_shape=[jax.ShapeDtypeStruct((seq, D_MODEL), BF16), jax.ShapeDtypeStruct((seq, Q_DIM), BF16),
                   jax.ShapeDtypeStruct((seq, KV_DIM), BF16), jax.ShapeDtypeStruct((seq, KV_DIM), BF16)],
        sem=("parallel",))(x, g, w, b, rc, rsa, rsb)


GROUP_ROWS = Q_PER_KV * BLOCK


def _band_mask(n, rows=GROUP_ROWS):
    row = lax.broadcasted_iota(jnp.int32, (rows, 2 * BLOCK), 0) & (BLOCK - 1)
    col = lax.broadcasted_iota(jnp.int32, (rows, 2 * BLOCK), 1)
    rel = row + BLOCK - col
    return (rel >= 0) & (rel < BLOCK) & ((col >= BLOCK) | (n > 0))


def _softmax_with_sink(s, mask, sink):
    s = jnp.where(mask, s, NEG_INF)
    m = jnp.maximum(jnp.max(s, axis=-1, keepdims=True), sink)
    p = jnp.exp(s - m)
    e_sink = jnp.exp(sink - m)
    inv = 1.0 / (jnp.sum(p, axis=-1, keepdims=True) + e_sink)
    return p * inv, e_sink * inv


PAIRS_PER_KV = Q_PER_KV // 2


def _kv_specs():
    cur = pl.BlockSpec((BLOCK, KV_DIM), lambda n: (n, 0))
    prev = pl.BlockSpec((BLOCK, KV_DIM), lambda n: (jnp.maximum(n - 1, 0), 0))
    return cur, prev


def _low_lanes():
    return lax.broadcasted_iota(jnp.int32, (2 * BLOCK, KV_DIM), 1) < HEAD_DIM


def _kv_low_high(prev_ref, cur_ref, j, low):
    both = jnp.concatenate([prev_ref[...], cur_ref[...]], axis=0).astype(F32)
    swapped = pltpu.roll(both, HEAD_DIM, 1)
    at_low, at_high = (both, swapped) if j == 0 else (swapped, both)
    return jnp.where(low, at_low, 0.0).astype(BF16), jnp.where(low, 0.0, at_high).astype(BF16)


def _fold_pair_halves(acc, j, low):
    folded = acc + pltpu.roll(acc, HEAD_DIM, 1)
    return jnp.where(low, folded, 0.0) if j == 0 else jnp.where(low, 0.0, folded)


def _pair_lanes(j, i):
    g = j * PAIRS_PER_KV + i
    return slice(g * LANES, (g + 1) * LANES), 2 * g


def _attn_fwd(sinks, q, k, v):
    seq = q.shape[0]
    cur, prev = _kv_specs()

    def body(sink_ref, q_ref, kc_ref, kp_ref, vc_ref, vp_ref, o_ref):
        mask = _band_mask(pl.program_id(0), BLOCK)
        for j in range(N_KV_HEADS):
            cs = slice(j * HEAD_DIM, (j + 1) * HEAD_DIM)
            kk = jnp.concatenate([kp_ref[:, cs], kc_ref[:, cs]], axis=0)
            vv = jnp.concatenate([vp_ref[:, cs], vc_ref[:, cs]], axis=0)
            for gq in range(Q_PER_KV):
                h = j * Q_PER_KV + gq
                hs = slice(h * HEAD_DIM, (h + 1) * HEAD_DIM)
                probs, _ = _softmax_with_sink(_nt(q_ref[:, hs], kk), mask, sink_ref[h])
                o_ref[:, hs] = jnp.dot(probs.astype(BF16), vv, preferred_element_type=F32).astype(BF16)

    return _call(
        body, name="attn_fwd", grid=(seq // BLOCK,),
        in_specs=[pl.BlockSpec(memory_space=pltpu.SMEM), _rows(BLOCK, Q_DIM), cur, prev, cur, prev],
        out_specs=_rows(BLOCK, Q_DIM), out_shape=jax.ShapeDtypeStruct((seq, Q_DIM), BF16),
        sem=("parallel",))(sinks, q, k, k, v, v)


def _mm_res(name, a, w, b, res):
    seq, kdim = a.shape
    n = w.shape[1]
    tm = min(2 * TOKEN_TILE, seq)

    def body(a_ref, w_ref, b_ref, r_ref, o_ref):
        o_ref[...] = r_ref[...] + (jnp.dot(a_ref[...], w_ref[...], preferred_element_type=F32) + b_ref[...])

    return _call(
        body, name=name, grid=(seq // tm,),
        in_specs=[_rows(tm, kdim), _resident((kdim, n)), _full((1, n)), _rows(tm, n)],
        out_specs=_rows(tm, n), out_shape=jax.ShapeDtypeStruct((seq, n), F32),
        sem=("parallel",))(a, w, b, res)


def _ff_tile_spec(slot):
    return pl.BlockSpec((FF_TILE_DEVS, ROWS_FF, D_MODEL), lambda i, j: (j, slot, 0))


def _ff_whole_spec(slot):
    return pl.BlockSpec((N_DEV, ROWS_FF, D_MODEL), lambda i: (0, slot, 0), pipeline_mode=pl.Buffered(1))


def _ffn_up(name, h, g, gathered):
    seq = h.shape[0]
    tm = min(TOKEN_TILE, seq)

    def body(h_ref, g_ref, w1_ref, w3_ref, f_ref, u_ref, w_ref, s_ref):
        @pl.when(pl.program_id(1) == 0)
        def _():
            hv = h_ref[...]
            f_ref[...] = (hv * _rms_stats(hv) * g_ref[...]).astype(BF16)

        f = f_ref[...]
        u = _nt(f, w1_ref[...].reshape(FF_TILE, D_MODEL))
        w = _nt(f, w3_ref[...].reshape(FF_TILE, D_MODEL))
        u_ref[...] = u.astype(BF16)
        w_ref[...] = w.astype(BF16)
        s_ref[...] = (u * _sig(u) * w).astype(BF16)

    tile_ff = pl.BlockSpec((tm, FF_TILE), lambda i, j: (i, j))
    ff_shape = jax.ShapeDtypeStruct((seq, D_FF), BF16)
    return _call(
        body, name=name, grid=(seq // tm, FF_SPLIT),
        in_specs=[_rows(tm, D_MODEL), _full((1, D_MODEL)), _ff_tile_spec(W1T_SLOT), _ff_tile_spec(W3T_SLOT)],
        out_specs=[_rows(tm, D_MODEL), tile_ff, tile_ff, tile_ff],
        out_shape=[jax.ShapeDtypeStruct((seq, D_MODEL), BF16), ff_shape, ff_shape, ff_shape],
        sem=("parallel", "arbitrary"))(h, g, gathered, gathered)


def _ffn_down(name, s, gathered, res):
    seq = s.shape[0]
    tm = min(2 * TOKEN_TILE, seq)

    def body(s_ref, w_ref, r_ref, o_ref):
        w2 = w_ref[...].reshape(D_FF, D_MODEL)
        o_ref[...] = r_ref[...] + jnp.dot(s_ref[...], w2, preferred_element_type=F32)

    return _call(
        body, name=name, grid=(seq // tm,),
        in_specs=[_rows(tm, D_FF), _ff_whole_spec(W2_SLOT), _rows(tm, D_MODEL)],
        out_specs=_rows(tm, D_MODEL), out_shape=jax.ShapeDtypeStruct((seq, D_MODEL), F32),
        sem=("parallel",))(s, gathered, res)


def _conv_fwd(h, g, wpw1, bpw1, wd, bdw, lng, lnb):
    seq = h.shape[0]
    tm = min(TOKEN_TILE, seq)
    n_chunks = tm // CONV_CHUNK
    win = CONV_CHUNK + CONV_HALO

    def body(h_ref, g_ref, w_ref, b_ref, wd_ref, bdw_ref, lng_ref, lnb_ref,
             y_ref, a_ref, dwc_ref, z_ref, gbuf):
        i = pl.program_id(0)

        @pl.when(i == 0)
        def _():
            gbuf[0:CONV_HALO, :] = jnp.zeros((CONV_HALO, D_MODEL), F32)

        @pl.when(i > 0)
        def _():
            gbuf[0:CONV_HALO, :] = gbuf[tm:tm + CONV_HALO, :]

        hv = h_ref[...]
        y = (hv * _rms_stats(hv) * g_ref[...]).astype(BF16)
        y_ref[...] = y
        a = _nt(y, w_ref[...]) + b_ref[...]
        a_ref[...] = a.astype(BF16)
        gbuf[CONV_HALO:CONV_HALO + tm, :] = a[:, :D_MODEL] * _sig(a[:, D_MODEL:])

        def chunk(c, carry):
            r0 = pl.multiple_of(c * CONV_CHUNK, CONV_CHUNK)
            for l in range(D_MODEL // LANES):
                ls = slice(l * LANES, (l + 1) * LANES)
                gw = gbuf[pl.ds(r0, win), ls]
                acc = jnp.zeros((CONV_CHUNK, LANES), F32) + bdw_ref[:, ls]
                for s in range(SUBLANES):
                    gs = gw if s == 0 else pltpu.roll(gw, s, 0)
                    for q in range(CONV_HALO // SUBLANES):
                        d = SUBLANES * q + s
                        if d < CONV_WIDTH:
                            lo = CONV_HALO - SUBLANES * q
                            acc = acc + wd_ref[d:d + 1, ls] * gs[lo:lo + CONV_CHUNK]
                dwc_ref[pl.ds(r0, CONV_CHUNK), ls] = acc
            return carry

        lax.fori_loop(0, n_chunks, chunk, 0)

        xv = dwc_ref[...]
        mu = jnp.mean(xv, axis=-1, keepdims=True)
        xc = xv - mu
        var = jnp.mean(xc * xc, axis=-1, keepdims=True)
        ln = xc * lax.rsqrt(var + LN_EPS) * lng_ref[...] + lnb_ref[...]
        z_ref[...] = (ln * _sig(ln)).astype(BF16)

    return _call(
        body, name="conv_fwd", grid=(seq // tm,),
        in_specs=[_rows(tm, D_MODEL), _full((1, D_MODEL)), _full((2 * D_MODEL, D_MODEL)), _full((1, 2 * D_MODEL)),
                  _full((TAPS_PAD, D_MODEL)), _full((1, D_MODEL)), _full((1, D_MODEL)), _full((1, D_MODEL))],
        out_specs=[_rows(tm, D_MODEL), _rows(tm, 2 * D_MODEL), _rows(tm, D_MODEL), _rows(tm, D_MODEL)],
        out_shape=[jax.ShapeDtypeStruct((seq, D_MODEL), BF16), jax.ShapeDtypeStruct((seq, 2 * D_MODEL), BF16),
                   jax.ShapeDtypeStruct((seq, D_MODEL), F32), jax.ShapeDtypeStruct((seq, D_MODEL), BF16)],
        scratch=[pltpu.VMEM((tm + CONV_HALO, D_MODEL), F32)],
        sem=("arbitrary",))(h, g, wpw1, bpw1, wd, bdw, lng, lnb)


def _final_loss(h, target, g):
    seq = h.shape[0]
    tm = min(TOKEN_TILE, seq)

    def body(h_ref, t_ref, g_ref, dh_ref, loss_ref, dg_ref):
        @pl.when(pl.program_id(0) == 0)
        def _():
            loss_ref[...] = jnp.zeros_like(loss_ref)
            dg_ref[...] = jnp.zeros_like(dg_ref)

        hv = h_ref[...]
        gv = g_ref[...]
        err = hv * _rms_stats(hv) * gv - t_ref[...]
        sq = jnp.sum(jnp.sum(err * err, axis=-1, keepdims=True), axis=0, keepdims=True)
        loss_ref[...] += jnp.broadcast_to(sq, loss_ref.shape)
        dx, dg = _rms_bwd(err * (1.0 / D_MODEL), hv, gv, 0.0)
        dh_ref[...] = dx
        dg_ref[...] += dg

    return _call(
        body, name="final_loss", grid=(seq // tm,),
        in_specs=[_rows(tm, D_MODEL), _rows(tm, D_MODEL), _full((1, D_MODEL))],
        out_specs=[_rows(tm, D_MODEL), _full((SUBLANES, LANES)), _full((1, D_MODEL))],
        out_shape=[jax.ShapeDtypeStruct((seq, D_MODEL), F32), jax.ShapeDtypeStruct((SUBLANES, LANES), F32),
                   jax.ShapeDtypeStruct((1, D_MODEL), F32)],
        sem=("arbitrary",))(h, target, g)


def _ffn_bwd_act(name, dh, u, w, gathered):
    seq = dh.shape[0]
    tm = min(TOKEN_TILE, seq)

    def body(dh_ref, u_ref, w_ref, w2_ref, du_ref, dw_ref, dhb):
        @pl.when(pl.program_id(1) == 0)
        def _():
            dhb[...] = dh_ref[...].astype(BF16)

        ds = _nt(dhb[...], w2_ref[...].reshape(FF_TILE, D_MODEL))
        uv = u_ref[...].astype(F32)
        sg = _sig(uv)
        dw_ref[...] = (ds * (uv * sg)).astype(BF16)
        du_ref[...] = (ds * w_ref[...].astype(F32) * (sg * (1.0 + uv * (1.0 - sg)))).astype(BF16)

    tile_ff = pl.BlockSpec((tm, FF_TILE), lambda i, j: (i, j))
    ff_shape = jax.ShapeDtypeStruct((seq, D_FF), BF16)
    return _call(
        body, name=name, grid=(seq // tm, FF_SPLIT),
        in_specs=[_rows(tm, D_MODEL), tile_ff, tile_ff, _ff_tile_spec(W2_SLOT)],
        out_specs=[tile_ff, tile_ff], out_shape=[ff_shape, ff_shape],
        scratch=[pltpu.VMEM((tm, D_MODEL), BF16)],
        sem=("parallel", "arbitrary"))(dh, u, w, gathered)


def _ffn_bwd_in(name, du, dw, gathered, h_in, dres, g):
    seq = du.shape[0]
    tm = min(TOKEN_TILE, seq)

    def body(du_ref, dw_ref, w1_ref, w3_ref, h_ref, dr_ref, g_ref, dx_ref, dg_ref):
        @pl.when(pl.program_id(0) == 0)
        def _():
            dg_ref[...] = jnp.zeros_like(dg_ref)

        df = jnp.dot(du_ref[...], w1_ref[...].reshape(D_FF, D_MODEL), preferred_element_type=F32)
        df = df + jnp.dot(dw_ref[...], w3_ref[...].reshape(D_FF, D_MODEL), preferred_element_type=F32)
        dx, dg = _rms_bwd(df, h_ref[...], g_ref[...], dr_ref[...])
        dx_ref[...] = dx
        dg_ref[...] += dg

    return _call(
        body, name=name, grid=(seq // tm,),
        in_specs=[_rows(tm, D_FF), _rows(tm, D_FF), _ff_whole_spec(W1T_SLOT), _ff_whole_spec(W3T_SLOT),
                  _rows(tm, D_MODEL), _rows(tm, D_MODEL), _full((1, D_MODEL))],
        out_specs=[_rows(tm, D_MODEL), _full((1, D_MODEL))],
        out_shape=[jax.ShapeDtypeStruct((seq, D_MODEL), F32), jax.ShapeDtypeStruct((1, D_MODEL), F32)],
        sem=("arbitrary",))(du, dw, gathered, gathered, h_in, dres, g)


def _mm_tn(name, a, b, *, tk):
    seq, kdim = a.shape
    n = b.shape[1]
    tt = min(2 * TOKEN_TILE, seq)
    n_t = seq // tt

    def body(a_ref, b_ref, o_ref, acc):
        t = pl.program_id(1)

        @pl.when(t == 0)
        def _():
            acc[...] = jnp.zeros_like(acc)

        acc[...] += _tn(a_ref[...].astype(BF16), b_ref[...].astype(BF16))

        @pl.when(t == n_t - 1)
        def _():
            o_ref[...] = acc[...].astype(BF16)

    return _call(
        body, name=name, grid=(kdim // tk, n_t),
        in_specs=[pl.BlockSpec((tt, tk), lambda k, t: (t, k)), pl.BlockSpec((tt, n), lambda k, t: (t, 0))],
        out_specs=pl.BlockSpec((tk, n), lambda k, t: (k, 0)),
        out_shape=jax.ShapeDtypeStruct((kdim, n), BF16),
        scratch=[pltpu.VMEM((tk, n), F32)],
        sem=("parallel", "arbitrary"))(a, b)


def _conv_bwd(dh, wpw2, dwc, a, lng, lnb, wd):
    seq = dh.shape[0]
    tm = min(TOKEN_TILE, seq)
    nt = seq // tm
    n_chunks = tm // CONV_CHUNK
    win = CONV_CHUNK + CONV_HALO
    halo_per_tile = tm // CONV_HALO

    def body(dh_ref, w_ref, dwc_ref, a_ref, ah_ref, lng_ref, lnb_ref, wd_ref,
             da_ref, dlg_ref, dlb_ref, dbdw_ref, dwd_ref, dbpw1_ref, dbpw2_ref,
             gbuf, dbuf, dglu, dwd_part):
        i = pl.program_id(0)
        r = nt - 1 - i

        @pl.when(i == 0)
        def _():
            dlg_ref[...] = jnp.zeros_like(dlg_ref)
            dlb_ref[...] = jnp.zeros_like(dlb_ref)
            dbdw_ref[...] = jnp.zeros_like(dbdw_ref)
            dbpw1_ref[...] = jnp.zeros_like(dbpw1_ref)
            dbpw2_ref[...] = jnp.zeros_like(dbpw2_ref)
            dwd_part[...] = jnp.zeros_like(dwd_part)
            dbuf[tm:tm + CONV_HALO, :] = jnp.zeros((CONV_HALO, D_MODEL), F32)

        @pl.when(i > 0)
        def _():
            dbuf[tm:tm + CONV_HALO, :] = dbuf[0:CONV_HALO, :]

        dhv = dh_ref[...]
        dbpw2_ref[...] += _sum_rows(dhv)
        dz = _nt(dhv.astype(BF16), w_ref[...])
        xv = dwc_ref[...]
        lg = lng_ref[...]
        mu = jnp.mean(xv, axis=-1, keepdims=True)
        xc = xv - mu
        rstd = lax.rsqrt(jnp.mean(xc * xc, axis=-1, keepdims=True) + LN_EPS)
        xhat = xc * rstd
        ln = xhat * lg + lnb_ref[...]
        sg = _sig(ln)
        dln = dz * (sg * (1.0 + ln * (1.0 - sg)))
        dlg_ref[...] += _sum_rows(dln * xhat)
        dlb_ref[...] += _sum_rows(dln)
        dxh = dln * lg
        ddw = rstd * (dxh - jnp.mean(dxh, axis=-1, keepdims=True)
                      - xhat * jnp.mean(dxh * xhat, axis=-1, keepdims=True))
        dbdw_ref[...] += _sum_rows(ddw)
        dbuf[0:tm, :] = ddw

        av = a_ref[...].astype(F32)
        a1 = av[:, :D_MODEL]
        s2 = _sig(av[:, D_MODEL:])
        gbuf[CONV_HALO:CONV_HALO + tm, :] = a1 * s2
        ah = ah_ref[...].astype(F32)
        gh = ah[:, :D_MODEL] * _sig(ah[:, D_MODEL:])
        gbuf[0:CONV_HALO, :] = jnp.where(r > 0, gh, 0.0)

        def chunk(c, carry):
            r0 = pl.multiple_of(c * CONV_CHUNK, CONV_CHUNK)
            for l in range(D_MODEL // LANES):
                ls = slice(l * LANES, (l + 1) * LANES)
                dw_ = dbuf[pl.ds(r0, win), ls]
                gw = gbuf[pl.ds(r0, win), ls]
                dc = dw_[0:CONV_CHUNK]
                acc = jnp.zeros((CONV_CHUNK, LANES), F32)
                for s in range(SUBLANES):
                    ds_ = dw_ if s == 0 else pltpu.roll(dw_, win - s, 0)
                    gs = gw if s == 0 else pltpu.roll(gw, s, 0)
                    for q in range(CONV_HALO // SUBLANES):
                        d = SUBLANES * q + s
                        if d < CONV_WIDTH:
                            acc = acc + wd_ref[d:d + 1, ls] * ds_[SUBLANES * q:SUBLANES * q + CONV_CHUNK]
                            lo = CONV_HALO - SUBLANES * q
                            prod = dc * gs[lo:lo + CONV_CHUNK]
                            dwd_part[d, :, ls] += jnp.sum(
                                prod.reshape(CONV_CHUNK // SUBLANES, SUBLANES, LANES), axis=0)
                dglu[pl.ds(r0, CONV_CHUNK), ls] = acc
            return carry

        lax.fori_loop(0, n_chunks, chunk, 0)

        dg_ = dglu[...]
        da1 = dg_ * s2
        da2 = dg_ * a1 * s2 * (1.0 - s2)
        da_ref[:, :D_MODEL] = da1.astype(BF16)
        da_ref[:, D_MODEL:] = da2.astype(BF16)
        dbpw1_ref[:, :D_MODEL] += _sum_rows(da1)
        dbpw1_ref[:, D_MODEL:] += _sum_rows(da2)

        @pl.when(i == nt - 1)
        def _():
            dwd_ref[...] = jnp.sum(dwd_part[...], axis=1)

    rev = lambda n: pl.BlockSpec((tm, n), lambda i: (nt - 1 - i, 0))
    halo = pl.BlockSpec((CONV_HALO, 2 * D_MODEL),
                        lambda i: (jnp.maximum((nt - 1 - i) * halo_per_tile - 1, 0), 0))
    vec = lambda n: _full((1, n))
    return _call(
        body, name="conv_bwd", grid=(nt,),
        in_specs=[rev(D_MODEL), _full((D_MODEL, D_MODEL)), rev(D_MODEL), rev(2 * D_MODEL), halo,
                  vec(D_MODEL), vec(D_MODEL), _full((TAPS_PAD, D_MODEL))],
        out_specs=[rev(2 * D_MODEL), vec(D_MODEL), vec(D_MODEL), vec(D_MODEL), _full((TAPS_PAD, D_MODEL)),
                   vec(2 * D_MODEL), vec(D_MODEL)],
        out_shape=[jax.ShapeDtypeStruct((seq, 2 * D_MODEL), BF16), jax.ShapeDtypeStruct((1, D_MODEL), F32),
                   jax.ShapeDtypeStruct((1, D_MODEL), F32), jax.ShapeDtypeStruct((1, D_MODEL), F32),
                   jax.ShapeDtypeStruct((TAPS_PAD, D_MODEL), F32), jax.ShapeDtypeStruct((1, 2 * D_MODEL), F32),
                   jax.ShapeDtypeStruct((1, D_MODEL), F32)],
        scratch=[pltpu.VMEM((tm + CONV_HALO, D_MODEL), F32), pltpu.VMEM((tm + CONV_HALO, D_MODEL), F32),
                 pltpu.VMEM((tm, D_MODEL), F32), pltpu.VMEM((TAPS_PAD, SUBLANES, D_MODEL), F32)],
        sem=("arbitrary",))(dh, wpw2, dwc, a, a, lng, lnb, wd)


def _mm_rms_bwd(name, dact, wt, h_in, dres, g):
    seq, n = dact.shape
    tm = min(TOKEN_TILE, seq)

    def body(da_ref, w_ref, h_ref, dr_ref, g_ref, dx_ref, dg_ref):
        @pl.when(pl.program_id(0) == 0)
        def _():
            dg_ref[...] = jnp.zeros_like(dg_ref)

        dy = jnp.dot(da_ref[...], w_ref[...], preferred_element_type=F32)
        dx, dg = _rms_bwd(dy, h_ref[...], g_ref[...], dr_ref[...])
        dx_ref[...] = dx
        dg_ref[...] += dg

    return _call(
        body, name=name, grid=(seq // tm,),
        in_specs=[_rows(tm, n), _resident((n, D_MODEL)), _rows(tm, D_MODEL), _rows(tm, D_MODEL), _full((1, D_MODEL))],
        out_specs=[_rows(tm, D_MODEL), _full((1, D_MODEL))],
        out_shape=[jax.ShapeDtypeStruct((seq, D_MODEL), F32), jax.ShapeDtypeStruct((1, D_MODEL), F32)],
        sem=("arbitrary",))(dact, wt, h_in, dres, g)


def _nt_bias(name, dy, w):
    seq, n = dy.shape
    kdim = w.shape[0]
    tm = min(2 * TOKEN_TILE, seq)

    def body(dy_ref, w_ref, o_ref, db_ref):
        @pl.when(pl.program_id(0) == 0)
        def _():
            db_ref[...] = jnp.zeros_like(db_ref)

        dyv = dy_ref[...]
        db_ref[...] += _sum_rows(dyv)
        o_ref[...] = _nt(dyv.astype(BF16), w_ref[...]).astype(BF16)

    return _call(
        body, name=name, grid=(seq // tm,),
        in_specs=[_rows(tm, n), _resident((kdim, n))],
        out_specs=[_rows(tm, kdim), _full((1, n))],
        out_shape=[jax.ShapeDtypeStruct((seq, kdim), BF16), jax.ShapeDtypeStruct((1, n), F32)],
        sem=("arbitrary",))(dy, w)


def _attn_bwd(sinks, q, k, v, do):
    seq = q.shape[0]
    cur, prev = _kv_specs()

    def body(sink_ref, q_ref, kc_ref, kp_ref, vc_ref, vp_ref, do_ref,
             dq_ref, dkc_ref, dkp_ref, dvc_ref, dvp_ref, dsink_ref):
        n = pl.program_id(0)

        @pl.when(n == 0)
        def _():
            dsink_ref[...] = jnp.zeros_like(dsink_ref)

        mask = _band_mask(n, BLOCK)
        low = _low_lanes()
        dk_all = jnp.zeros((2 * BLOCK, KV_DIM), F32)
        dv_all = jnp.zeros((2 * BLOCK, KV_DIM), F32)
        for j in range(N_KV_HEADS):
            k_lo, k_hi = _kv_low_high(kp_ref, kc_ref, j, low)
            v_lo, v_hi = _kv_low_high(vp_ref, vc_ref, j, low)
            dk_acc = jnp.zeros((2 * BLOCK, KV_DIM), F32)
            dv_acc = jnp.zeros((2 * BLOCK, KV_DIM), F32)
            for i in range(PAIRS_PER_KV):
                ls, h = _pair_lanes(j, i)
                qp = q_ref[:, ls]
                dop = do_ref[:, ls]
                dsb, pb16 = [], []
                for t, (k_sel, v_sel) in enumerate(((k_lo, v_lo), (k_hi, v_hi))):
                    probs, p_sink = _softmax_with_sink(_nt(qp, k_sel), mask, sink_ref[h + t])
                    dp = _nt(dop, v_sel)
                    delta = jnp.sum(probs * dp, axis=-1, keepdims=True)
                    dsb.append((probs * (dp - delta)).astype(BF16))
                    pb16.append(probs.astype(BF16))
                    dsink_ref[h + t:h + t + 1, :] += jnp.broadcast_to(_sum_rows(p_sink * delta), (1, LANES))
                dq = (jnp.dot(dsb[0], k_lo, preferred_element_type=F32)
                      + jnp.dot(dsb[1], k_hi, preferred_element_type=F32))
                dq_ref[:, ls] = dq * (HEAD_DIM ** -0.5)
                dk_acc = dk_acc + jnp.where(low, _tn(dsb[0], qp), _tn(dsb[1], qp))
                dv_acc = dv_acc + jnp.where(low, _tn(pb16[0], dop), _tn(pb16[1], dop))
            dk_all = dk_all + _fold_pair_halves(dk_acc, j, low)
            dv_all = dv_all + _fold_pair_halves(dv_acc, j, low)
        dkp_ref[...] = dk_all[:BLOCK]
        dkc_ref[...] = dk_all[BLOCK:]
        dvp_ref[...] = dv_all[:BLOCK]
        dvc_ref[...] = dv_all[BLOCK:]

    kv_out = _rows(BLOCK, KV_DIM)
    kv_shape = jax.ShapeDtypeStruct((seq, KV_DIM), F32)
    return _call(
        body, name="attn_bwd", grid=(seq // BLOCK,),
        in_specs=[pl.BlockSpec(memory_space=pltpu.SMEM), _rows(BLOCK, Q_DIM), cur, prev, cur, prev,
                  _rows(BLOCK, Q_DIM)],
        out_specs=[_rows(BLOCK, Q_DIM), kv_out, kv_out, kv_out, kv_out, _full((N_Q_HEADS, LANES))],
        out_shape=[jax.ShapeDtypeStruct((seq, Q_DIM), F32), kv_shape, kv_shape, kv_shape, kv_shape,
                   jax.ShapeDtypeStruct((N_Q_HEADS, LANES), F32)],
        sem=("arbitrary",))(sinks, q, k, k, v, v, do)


def _rope_bwd(dq, dkc, dkp, dvc, dvp, rc, rsa, rsb):
    seq = dq.shape[0]
    nb = seq // BLOCK
    tm = min(TOKEN_TILE, seq)
    nt = seq // tm
    per = tm // BLOCK
    nxt = pl.BlockSpec((BLOCK, KV_DIM), lambda i: (jnp.minimum((i + 1) * per, nb - 1), 0))

    def body(dq_ref, dkc_ref, dkp_ref, dkn_ref, dvc_ref, dvp_ref, dvn_ref, c_ref, sa_ref, sb_ref, o_ref, db_ref):
        i = pl.program_id(0)

        @pl.when(i == 0)
        def _():
            db_ref[...] = jnp.zeros_like(db_ref)

        c, sa, sb = c_ref[...], sa_ref[...], sb_ref[...]
        last = i == nt - 1

        def from_next_block(prev_ref, next_ref):
            tail = jnp.where(last, 0.0, next_ref[...])
            return tail if per == 1 else jnp.concatenate([prev_ref[BLOCK:, :], tail], axis=0)

        dk = dkc_ref[...] + from_next_block(dkp_ref, dkn_ref)
        dv = dvc_ref[...] + from_next_block(dvp_ref, dvn_ref)
        for l in range(Q_DIM // LANES):
            ls = slice(l * LANES, (l + 1) * LANES)
            blk = _rope_t(dq_ref[:, ls], c, sa, sb)
            o_ref[:, ls] = blk.astype(BF16)
            db_ref[:, ls] += _sum_rows(blk)
        dkr = _rope_t(dk, c, sa, sb)
        o_ref[:, Q_DIM:Q_DIM + KV_DIM] = dkr.astype(BF16)
        db_ref[:, Q_DIM:Q_DIM + KV_DIM] += _sum_rows(dkr)
        o_ref[:, Q_DIM + KV_DIM:] = dv.astype(BF16)
        db_ref[:, Q_DIM + KV_DIM:] += _sum_rows(dv)

    kv = _rows(tm, KV_DIM)
    tab = _rows(tm, LANES)
    return _call(
        body, name="rope_bwd", grid=(nt,),
        in_specs=[_rows(tm, Q_DIM), kv, kv, nxt, kv, kv, nxt, tab, tab, tab],
        out_specs=[_rows(tm, QKV_DIM), _full((1, QKV_DIM))],
        out_shape=[jax.ShapeDtypeStruct((seq, QKV_DIM), BF16), jax.ShapeDtypeStruct((1, QKV_DIM), F32)],
        sem=("arbitrary",))(dq, dkc, dkp, dkp, dvc, dvp, dvp, rc, rsa, rsb)


def _adamw(w, g, m, v):
    m = ADAM_B1 * m + (1.0 - ADAM_B1) * g
    v = ADAM_B2 * v + (1.0 - ADAM_B2) * (g * g)
    m_hat = m / (1.0 - ADAM_B1 ** ADAM_STEP)
    v_hat = v / (1.0 - ADAM_B2 ** ADAM_STEP)
    delta = -ADAM_LR * (m_hat / (jnp.sqrt(v_hat) + ADAM_EPS) + ADAM_WD * w)
    return delta, m, v


def _sum_slots(name, parts):
    _, rows, cols = parts.shape
    tr = rows if rows <= 512 else ROWS_FF

    def body(p_ref, g_ref):
        g = p_ref[0].astype(F32)
        for d in range(1, N_DEV):
            g = g + p_ref[d].astype(F32)
        g_ref[...] = g

    return _call(
        body, name=name, grid=(rows // tr,),
        in_specs=[pl.BlockSpec((N_DEV, tr, cols), lambda i: (0, i, 0))],
        out_specs=_rows(tr, cols), out_shape=jax.ShapeDtypeStruct((rows, cols), F32),
        sem=("parallel",))(parts)


def _adamw_native(name, g, w, m, v):
    layers, rows, cols = w.shape
    tr = rows if rows <= 512 else 256

    def body(g_ref, w_ref, m_ref, v_ref, d_ref, nm_ref, nv_ref):
        d_ref[...], nm_ref[...], nv_ref[...] = _adamw(w_ref[...], g_ref[...], m_ref[...], v_ref[...])

    spec = pl.BlockSpec((1, tr, cols), lambda l, i: (l, i, 0))
    shape = jax.ShapeDtypeStruct(w.shape, F32)
    return _call(
        body, name=name, grid=(layers, rows // tr), in_specs=[spec, spec, spec, spec],
        out_specs=[spec, spec, spec], out_shape=[shape, shape, shape],
        sem=("parallel", "parallel"))(g, w, m, v)


def _adamw_replicated(parts, w, m, v):
    def body(p_ref, w_ref, m_ref, v_ref, g_ref, d_ref, nm_ref, nv_ref):
        g = p_ref[0]
        for j in range(1, N_DEV):
            g = g + p_ref[j]
        g_ref[...] = g
        d_ref[...], nm_ref[...], nv_ref[...] = _adamw(w_ref[...], g, m_ref[...], v_ref[...])

    spec = _full((REPL_ROWS, D_MODEL))
    shape = jax.ShapeDtypeStruct((REPL_ROWS, D_MODEL), F32)
    return _call(
        body, name="adamw_replicated", grid=(1,),
        in_specs=[_full((N_DEV, REPL_ROWS, D_MODEL)), spec, spec, spec],
        out_specs=[spec, spec, spec, spec], out_shape=[shape, shape, shape, shape],
        sem=("arbitrary",))(parts, w, m, v)


_MESH = pl.DeviceIdType.MESH
_ANY = pl.BlockSpec(memory_space=pl.ANY)


def _all_gather(name, xs):
    rows, cols = xs.shape

    def body(x_ref, out_ref, send_sems, recv_sems, local_sem):
        x, y, c = lax.axis_index("x"), lax.axis_index("y"), lax.axis_index("c")
        me, sibling = (x, y, c), (x, y, 1 - c)
        chips = [(1 - x, y), (x, 1 - y), (1 - x, 1 - y)]

        def slot(px, py, pc):
            return out_ref.at[4 * px + 2 * py + pc]

        def copy(k, block, to, src=None):
            return pltpu.make_async_remote_copy(
                src_ref=slot(*block) if src is None else src, dst_ref=slot(*block),
                send_sem=send_sems.at[k], recv_sem=recv_sems.at[k], device_id=to, device_id_type=_MESH)

        mine = pltpu.make_async_copy(x_ref, slot(*me), local_sem)
        mine.start()
        first = [copy(0, me, sibling, src=x_ref)]
        first += [copy(1 + j, me, (*chip, c), src=x_ref) for j, chip in enumerate(chips)]
        for cp in first:
            cp.start()
        passed = [copy(4 + j, (*chip, c), sibling) for j, chip in enumerate(chips)]
        for j, chip in enumerate(chips):
            copy(1 + j, (*chip, c), me).wait_recv()
            passed[j].start()
        copy(0, sibling, me).wait_recv()
        for j, chip in enumerate(chips):
            copy(4 + j, (*chip, 1 - c), me).wait_recv()
        for cp in first + passed:
            cp.wait_send()
        mine.wait()

    return pl.pallas_call(
        body, name=name, out_shape=jax.ShapeDtypeStruct((N_DEV, rows, cols), xs.dtype),
        in_specs=[_ANY], out_specs=_ANY,
        scratch_shapes=[pltpu.SemaphoreType.DMA((7,)), pltpu.SemaphoreType.DMA((7,)), pltpu.SemaphoreType.DMA],
    )(xs)


N_PEERS = N_DEV - 1
_HBM = pl.BlockSpec(memory_space=pltpu.HBM)
_SEM = pl.BlockSpec(memory_space=pltpu.SEMAPHORE)
_DATAFLOW = pltpu.SideEffectType.DATAFLOW_SIDE_EFFECTING
_TOKEN = jax.ShapeDtypeStruct((SUBLANES, LANES), F32)


def _peers():
    x, y, c = lax.axis_index("x"), lax.axis_index("y"), lax.axis_index("c")
    out = []
    for k in range(1, N_DEV):
        px = 1 - x if k & 4 else x
        py = 1 - y if k & 2 else y
        pc = 1 - c if k & 1 else c
        out.append(((px, py, pc), 4 * px + 2 * py + pc))
    return 4 * x + 2 * y + c, out


def _in_hbm(a):
    return pltpu.with_memory_space_constraint(a, pltpu.HBM)


def _landing(rows):
    return _in_hbm(lax.empty((N_DEV, rows, D_MODEL), BF16))


def _sem_pair():
    return pltpu.SemaphoreType.DMA((N_PEERS,)), pltpu.SemaphoreType.DMA((N_PEERS,))


def _gather_start(name, payloads):
    n = len(payloads)

    def body(*refs):
        src, land = refs[:n], refs[n:2 * n]
        sems = refs[2 * n:4 * n]
        token = refs[-1]
        me, peers = _peers()
        for g in range(n):
            for k, (pos, _) in enumerate(peers):
                pltpu.make_async_remote_copy(
                    src_ref=src[g], dst_ref=land[g].at[me], send_sem=sems[2 * g].at[k],
                    recv_sem=sems[2 * g + 1].at[k], device_id=pos, device_id_type=_MESH).start()
        token[...] = jnp.zeros_like(token)

    lands = [_landing(p.shape[0]) for p in payloads]
    sem_shapes = [s for _ in payloads for s in _sem_pair()]
    hbm_shapes = [pltpu.HBM(a.shape, a.dtype) for a in list(payloads) + lands]
    out = pl.pallas_call(
        body, name=name, out_shape=(*sem_shapes, *hbm_shapes, _TOKEN),
        in_specs=[_HBM] * (2 * n), out_specs=(*[_SEM] * (2 * n), *[_HBM] * (2 * n), pl.BlockSpec(memory_space=pltpu.VMEM)),
        input_output_aliases={i: 2 * n + i for i in range(2 * n)},
        compiler_params=pltpu.CompilerParams(has_side_effects=_DATAFLOW),
    )(*[_in_hbm(p) for p in payloads], *lands)
    sems, thru = out[:2 * n], out[2 * n:4 * n]
    return [(thru[g], thru[n + g], sems[2 * g], sems[2 * g + 1]) for g in range(n)], out[-1]


def _gather_wait(name, group, after):
    payload, land, send_sems, recv_sems = group

    def body(src_ref, land_ref, send_ref, recv_ref, after_ref, src_out, land_out):
        _, peers = _peers()
        for k, (pos, idx) in enumerate(peers):
            cp = pltpu.make_async_remote_copy(
                src_ref=src_ref, dst_ref=land_ref.at[idx], send_sem=send_ref.at[k], recv_sem=recv_ref.at[k],
                device_id=pos, device_id_type=_MESH)
            cp.wait_send()
            cp.wait_recv()

    _, land = pl.pallas_call(
        body, name=name, out_shape=(pltpu.HBM(payload.shape, payload.dtype), pltpu.HBM(land.shape, land.dtype)),
        in_specs=[_HBM, _HBM, _SEM, _SEM, _ANY], out_specs=(_HBM, _HBM), input_output_aliases={0: 0, 1: 1},
        compiler_params=pltpu.CompilerParams(has_side_effects=_DATAFLOW),
    )(payload, land, send_sems, recv_sems, after)
    me = 4 * lax.axis_index("x") + 2 * lax.axis_index("y") + lax.axis_index("c")
    return lax.dynamic_update_slice(land, payload[None], (me, 0, 0))


def _scatter_start(name, blocks):
    rows = blocks.shape[1]

    def body(blocks_ref, land_ref, send_sems, recv_sems, blocks_out, land_out, token):
        me, peers = _peers()
        for k, (pos, idx) in enumerate(peers):
            pltpu.make_async_remote_copy(
                src_ref=blocks_ref.at[idx], dst_ref=land_ref.at[me], send_sem=send_sems.at[k],
                recv_sem=recv_sems.at[k], device_id=pos, device_id_type=_MESH).start()
        token[...] = jnp.zeros_like(token)

    land = _landing(rows)
    send_sems, recv_sems, blocks_thru, land_thru, token = pl.pallas_call(
        body, name=name,
        out_shape=(*_sem_pair(), pltpu.HBM(blocks.shape, blocks.dtype), pltpu.HBM(land.shape, land.dtype), _TOKEN),
        in_specs=[_HBM, _HBM], out_specs=(_SEM, _SEM, _HBM, _HBM, pl.BlockSpec(memory_space=pltpu.VMEM)),
        input_output_aliases={0: 2, 1: 3},
        compiler_params=pltpu.CompilerParams(has_side_effects=_DATAFLOW),
    )(_in_hbm(blocks), land)
    return (blocks_thru, land_thru, send_sems, recv_sems), token


def _scatter_wait(groups, after):
    n = len(groups)

    def body(*refs):
        blocks, land = refs[:n], refs[n:2 * n]
        sems = refs[2 * n:4 * n]
        _, peers = _peers()
        for g in range(n):
            for k, (pos, idx) in enumerate(peers):
                cp = pltpu.make_async_remote_copy(
                    src_ref=blocks[g].at[idx], dst_ref=land[g].at[idx], send_sem=sems[2 * g].at[k],
                    recv_sem=sems[2 * g + 1].at[k], device_id=pos, device_id_type=_MESH)
                cp.wait_send()
                cp.wait_recv()

    hbm = [grp[0] for grp in groups] + [grp[1] for grp in groups]
    sems = [s for grp in groups for s in grp[2:]]
    out = pl.pallas_call(
        body, name="rs_wait", out_shape=tuple(pltpu.HBM(a.shape, a.dtype) for a in hbm),
        in_specs=[_HBM] * (2 * n) + [_SEM] * (2 * n) + [_ANY], out_specs=tuple([_HBM] * (2 * n)),
        input_output_aliases={i: i for i in range(2 * n)},
        compiler_params=pltpu.CompilerParams(has_side_effects=_DATAFLOW),
    )(*hbm, *sems, after)
    me = 4 * lax.axis_index("x") + 2 * lax.axis_index("y") + lax.axis_index("c")
    lands = []
    for g in range(n):
        own = lax.dynamic_index_in_dim(out[g], me, axis=0, keepdims=True)
        lands.append(lax.dynamic_update_slice(out[n + g], own, (me, 0, 0)))
    return lands


def _pad_rows(flat, rows):
    return jnp.pad(flat, (0, rows * D_MODEL - flat.shape[0])).reshape(rows, D_MODEL)


SMALL_NAMES = ("conv_b_pw1", "conv_w_dw", "conv_b_dw", "conv_ln_g", "conv_ln_b", "conv_b_pw2")


def _pack_small(p):
    flat = jnp.concatenate([p[n].reshape(-1) for n in SMALL_NAMES])
    return _pad_rows(flat, ROWS_SMALL).reshape(1, ROWS_SMALL, D_MODEL)


def _unpack_small(packed):
    flat = packed.reshape(-1)
    c = D_MODEL // N_DEV
    shapes = ((1, 2 * c), (1, CONV_WIDTH, c), (1, c), (1, c), (1, c), (1, c))
    out, o = {}, 0
    for n, shape in zip(SMALL_NAMES, shapes):
        size = shape[-1] * (shape[1] if len(shape) == 3 else 1)
        out[n] = flat[o:o + size].reshape(shape)
        o += size
    return out


def _gather_payloads(p):
    t = lambda a: jnp.swapaxes(a, -1, -2).astype(BF16)
    w1t, w3t, w2 = t(p["ffn_w1"]), t(p["ffn_w3"]), p["ffn_w2"].astype(BF16)
    small = _pack_small(p).reshape(-1)[:ROWS_SMALL * D_MODEL // 2]
    small = lax.bitcast_convert_type(small, BF16).reshape(ROWS_SMALL, D_MODEL)
    conv = jnp.concatenate([t(p["conv_w_pw1"][0]), p["conv_w_pw2"][0].astype(BF16), small], axis=0)
    ffn = [jnp.concatenate([w1t[l], w3t[l], w2[l]], axis=0) for l in range(2)]
    return [t(p["attn_w_qkv"][0]), p["attn_w_o"][0].astype(BF16), ffn[0], conv, ffn[1]]


def _device_rows(land, lo, n):
    return land[:, lo:lo + n].reshape(N_DEV * n, D_MODEL)


def _unpack_conv(land):
    small = lax.bitcast_convert_type(
        land[:, ROWS_PW1 + ROWS_PW2:].reshape(N_DEV, ROWS_SMALL * D_MODEL // 2, 2), F32)
    c = D_MODEL // N_DEV
    b_pw1 = small[:, :2 * c].reshape(1, 2 * D_MODEL)
    s = 2 * c
    w_dw = small[:, s:s + CONV_WIDTH * c].reshape(N_DEV, CONV_WIDTH, c).transpose(1, 0, 2).reshape(CONV_WIDTH, D_MODEL)
    s += CONV_WIDTH * c
    b_dw, ln_g, ln_b, b_pw2 = (small[:, s + i * c:s + (i + 1) * c].reshape(1, D_MODEL) for i in range(4))
    return dict(w_pw1_t=_device_rows(land, 0, ROWS_PW1), w_pw2=_device_rows(land, ROWS_PW1, ROWS_PW2),
                b_pw1=b_pw1, w_dw=w_dw, b_dw=b_dw, ln_g=ln_g, ln_b=ln_b, b_pw2=b_pw2)


def _dest_blocks(mats):
    return jnp.concatenate([a.reshape(N_DEV, -1, D_MODEL) for a in mats], axis=1)


def _small_grad_rows(g_bpw1, g_dw, g_bdw, g_lng, g_lnb, g_bpw2):
    c = D_MODEL // N_DEV
    small = jnp.concatenate(
        [g_bpw1.reshape(N_DEV, 2 * c), g_dw.reshape(CONV_WIDTH, N_DEV, c).transpose(1, 0, 2).reshape(N_DEV, -1),
         g_bdw.reshape(N_DEV, c), g_lng.reshape(N_DEV, c), g_lnb.reshape(N_DEV, c), g_bpw2.reshape(N_DEV, c)], axis=1)
    small = jnp.pad(small, ((0, 0), (0, ROWS_SMALL * D_MODEL - SMALL_USED)))
    return small.reshape(N_DEV * ROWS_SMALL, D_MODEL).astype(BF16)


def _pack_replicated(norm_mix, norm_ffn, b_qkv, sinks, b_o, norm_final):
    rows = [norm_mix.reshape(2, D_MODEL), norm_ffn.reshape(2, D_MODEL), _pad_rows(b_qkv.reshape(-1), 2),
            _pad_rows(sinks.reshape(-1), 1), b_o.reshape(1, D_MODEL), norm_final.reshape(1, D_MODEL)]
    p = jnp.concatenate(rows, axis=0)
    return jnp.pad(p, ((0, REPL_ROWS - p.shape[0]), (0, 0)))


def _unpack_replicated(p):
    return dict(norm_mix=p[0:2], norm_ffn=p[2:4], attn_b_qkv=p[4:6].reshape(-1)[:QKV_DIM].reshape(1, QKV_DIM),
                attn_sinks=p[6, :N_Q_HEADS].reshape(1, N_Q_HEADS), attn_b_o=p[7:8], norm_final=p[8])


WEIGHT_ORDER = ['norm_mix', 'norm_ffn', 'attn_w_qkv', 'attn_b_qkv', 'attn_sinks', 'attn_w_o', 'attn_b_o',
                'conv_w_pw1', 'conv_b_pw1', 'conv_w_dw', 'conv_b_dw', 'conv_ln_g', 'conv_ln_b', 'conv_w_pw2',
                'conv_b_pw2', 'ffn_w1', 'ffn_w3', 'ffn_w2', 'norm_final']


def kernel(x, norm_mix, norm_ffn, attn_w_qkv, attn_b_qkv, attn_sinks, attn_w_o, attn_b_o, conv_w_pw1, conv_b_pw1, conv_w_dw, conv_b_dw, conv_ln_g, conv_ln_b, conv_w_pw2, conv_b_pw2, ffn_w1, ffn_w3, ffn_w2, norm_final, loss_target, m_norm_mix, m_norm_ffn, m_attn_w_qkv, m_attn_b_qkv, m_attn_sinks, m_attn_w_o, m_attn_b_o, m_conv_w_pw1, m_conv_b_pw1, m_conv_w_dw, m_conv_b_dw, m_conv_ln_g, m_conv_ln_b, m_conv_w_pw2, m_conv_b_pw2, m_ffn_w1, m_ffn_w3, m_ffn_w2, m_norm_final, v_norm_mix, v_norm_ffn, v_attn_w_qkv, v_attn_b_qkv, v_attn_sinks, v_attn_w_o, v_attn_b_o, v_conv_w_pw1, v_conv_b_pw1, v_conv_w_dw, v_conv_b_dw, v_conv_ln_g, v_conv_ln_b, v_conv_w_pw2, v_conv_b_pw2, v_ffn_w1, v_ffn_w3, v_ffn_w2, v_norm_final):
    xs = x[0]
    target = loss_target[0]
    seq = xs.shape[0]
    my_x, my_y, my_c = lax.axis_index("x"), lax.axis_index("y"), lax.axis_index("c")

    w = dict(attn_w_qkv=attn_w_qkv, attn_w_o=attn_w_o, conv_w_pw1=conv_w_pw1, conv_b_pw1=conv_b_pw1,
             conv_w_dw=conv_w_dw, conv_b_dw=conv_b_dw, conv_ln_g=conv_ln_g, conv_ln_b=conv_ln_b,
             conv_w_pw2=conv_w_pw2, conv_b_pw2=conv_b_pw2, ffn_w1=ffn_w1, ffn_w3=ffn_w3, ffn_w2=ffn_w2)
    m = dict(attn_w_qkv=m_attn_w_qkv, attn_w_o=m_attn_w_o, conv_w_pw1=m_conv_w_pw1, conv_b_pw1=m_conv_b_pw1,
             conv_w_dw=m_conv_w_dw, conv_b_dw=m_conv_b_dw, conv_ln_g=m_conv_ln_g, conv_ln_b=m_conv_ln_b,
             conv_w_pw2=m_conv_w_pw2, conv_b_pw2=m_conv_b_pw2, ffn_w1=m_ffn_w1, ffn_w3=m_ffn_w3, ffn_w2=m_ffn_w2)
    v = dict(attn_w_qkv=v_attn_w_qkv, attn_w_o=v_attn_w_o, conv_w_pw1=v_conv_w_pw1, conv_b_pw1=v_conv_b_pw1,
             conv_w_dw=v_conv_w_dw, conv_b_dw=v_conv_b_dw, conv_ln_g=v_conv_ln_g, conv_ln_b=v_conv_ln_b,
             conv_w_pw2=v_conv_w_pw2, conv_b_pw2=v_conv_b_pw2, ffn_w1=v_ffn_w1, ffn_w3=v_ffn_w3, ffn_w2=v_ffn_w2)

    payloads = _gather_payloads(w)
    (ag_qkv, ag_wo), tok = _gather_start("ag_start_attn", payloads[:2])
    (ag_ffn0, ag_conv, ag_ffn1), _ = _gather_start(
        "ag_start_rest", [payloads[2] + tok[0, 0].astype(BF16), payloads[3], payloads[4]])
    rc, rsa, rsb = _rope_tables(seq)
    sinks = attn_sinks.reshape(N_Q_HEADS)
    g_mix0, g_mix1 = norm_mix[0:1], norm_mix[1:2]
    g_ffn0, g_ffn1 = norm_ffn[0:1], norm_ffn[1:2]
    g_fin = norm_final.reshape(1, D_MODEL)

    w_qkv_t = _gather_wait("ag_wait_qkv", ag_qkv, xs).reshape(QKV_DIM, D_MODEL)
    y0, q, k, vv = _qkv_fwd(xs, g_mix0, w_qkv_t, attn_b_qkv, rc, rsa, rsb)
    attn = _attn_fwd(sinks, q, k, vv)
    w_o = _gather_wait("ag_wait_wo", ag_wo, attn).reshape(Q_DIM, D_MODEL)
    h1 = _mm_res("attn_out_proj", attn, w_o, attn_b_o, xs)
    w_ffn0 = _gather_wait("ag_wait_ffn0", ag_ffn0, h1)
    f0, u0, p0, s0 = _ffn_up("ffn0_up", h1, g_ffn0, w_ffn0)
    h2 = _ffn_down("ffn0_down", s0, w_ffn0, h1)
    wt = _unpack_conv(_gather_wait("ag_wait_conv", ag_conv, h2))
    wd = jnp.concatenate([wt["w_dw"][::-1], jnp.zeros((TAPS_PAD - CONV_WIDTH, D_MODEL), F32)], axis=0)
    y1, a, dwc, z = _conv_fwd(h2, g_mix1, wt["w_pw1_t"], wt["b_pw1"], wd, wt["b_dw"], wt["ln_g"], wt["ln_b"])
    h3 = _mm_res("conv_out_proj", z, wt["w_pw2"], wt["b_pw2"], h2)
    w_ffn1 = _gather_wait("ag_wait_ffn1", ag_ffn1, h3)
    f1, u1, p1, s1 = _ffn_up("ffn1_up", h3, g_ffn1, w_ffn1)
    h4 = _ffn_down("ffn1_down", s1, w_ffn1, h3)
    dh4, sq, dg_fin = _final_loss(h4, target, g_fin)
    loss = lax.psum(sq[0, 0] * (0.5 / D_MODEL), ("x", "y", "c"))

    du1, dp1 = _ffn_bwd_act("ffn1_bwd_act", dh4, u1, p1, w_ffn1)
    dh3, dg_ffn1 = _ffn_bwd_in("ffn1_bwd_in", du1, dp1, w_ffn1, h3, dh4, g_ffn1)
    gw2_1 = _mm_tn("ffn1_dw2", s1, dh4, tk=FF_TILE)
    gw1t_1 = _mm_tn("ffn1_dw1", du1, f1, tk=FF_TILE)
    gw3t_1 = _mm_tn("ffn1_dw3", dp1, f1, tk=FF_TILE)
    rs_ffn1, tok = _scatter_start("rs_start_ffn1", _dest_blocks([gw1t_1, gw3t_1, gw2_1]))

    da, dlg, dlb, dbdw, dwd, dbpw1, dbpw2 = _conv_bwd(dh3, wt["w_pw2"], dwc, a, wt["ln_g"] + tok[0, 0],
                                                     wt["ln_b"], wd)
    gpw2 = _mm_tn("conv_dw_pw2", z, dh3, tk=D_MODEL)
    dh2, dg_mix1 = _mm_rms_bwd("conv_in_bwd", da, wt["w_pw1_t"], h2, dh3, g_mix1)
    gpw1t = _mm_tn("conv_dw_pw1", da, y1, tk=D_MODEL)
    small_rows = _small_grad_rows(dbpw1, dwd[:CONV_WIDTH][::-1], dbdw, dlg, dlb, dbpw2)
    rs_conv, tok = _scatter_start("rs_start_conv", _dest_blocks([gpw1t, gpw2, small_rows]))

    du0, dp0 = _ffn_bwd_act("ffn0_bwd_act", dh2, u0, p0, w_ffn0)
    dh1, dg_ffn0 = _ffn_bwd_in("ffn0_bwd_in", du0, dp0, w_ffn0, h1, dh2, g_ffn0 + tok[0, 0])
    gw2_0 = _mm_tn("ffn0_dw2", s0, dh2, tk=FF_TILE)
    gw1t_0 = _mm_tn("ffn0_dw1", du0, f0, tk=FF_TILE)
    gw3t_0 = _mm_tn("ffn0_dw3", dp0, f0, tk=FF_TILE)
    rs_ffn0, tok = _scatter_start("rs_start_ffn0", _dest_blocks([gw1t_0, gw3t_0, gw2_0]))

    gwo = _mm_tn("attn_dw_o", attn, dh1, tk=D_MODEL)
    rs_wo, tok2 = _scatter_start("rs_start_wo", _dest_blocks([gwo]))
    dattn, dbo = _nt_bias("attn_out_bwd", dh1, w_o)
    dq, dkc, dkp, dvc, dvp, dsink = _attn_bwd(sinks + tok[0, 0] + tok2[0, 0], q, k, vv, dattn)
    dqkv, dbqkv = _rope_bwd(dq, dkc, dkp, dvc, dvp, rc, rsa, rsb)
    gqkvt = _mm_tn("attn_dw_qkv", dqkv, y0, tk=QKV_DIM)
    rs_qkv, tok = _scatter_start("rs_start_qkv", _dest_blocks([gqkvt]))
    dx, dg_mix0 = _mm_rms_bwd("qkv_in_bwd", dqkv, w_qkv_t, xs, dh1, g_mix0 + tok[0, 0])

    p_ffn1, p_conv, p_ffn0, p_wo, p_qkv = _scatter_wait([rs_ffn1, rs_conv, rs_ffn0, rs_wo, rs_qkv], dx)
    g_ffn = [_sum_slots("rs_sum_ffn0", p_ffn0), _sum_slots("rs_sum_ffn1", p_ffn1)]
    g_conv = _sum_slots("rs_sum_conv", p_conv)
    g_wo = _sum_slots("rs_sum_wo", p_wo)
    g_qkv = _sum_slots("rs_sum_qkv", p_qkv)

    def ff(slot, transpose):
        blk = jnp.stack([g[slot * ROWS_FF:(slot + 1) * ROWS_FF] for g in g_ffn])
        return jnp.swapaxes(blk, 1, 2) if transpose else blk

    grads = dict(
        ffn_w1=ff(W1T_SLOT, True), ffn_w3=ff(W3T_SLOT, True), ffn_w2=ff(W2_SLOT, False),
        attn_w_qkv=g_qkv.T[None], attn_w_o=g_wo[None],
        conv_w_pw1=g_conv[:ROWS_PW1].T[None], conv_w_pw2=g_conv[ROWS_PW1:ROWS_PW1 + ROWS_PW2][None])
    sharded = [dict(grads), {}, {}, {}]
    for n in grads:
        for dst, t in zip(sharded[1:], _adamw_native("adamw_" + n, grads[n], w[n], m[n], v[n])):
            dst[n] = t
    g_small = g_conv[ROWS_PW1 + ROWS_PW2:][None]
    small_out = _adamw_native("adamw_small", g_small, _pack_small(w), _pack_small(m), _pack_small(v))
    for dst, t in zip(sharded, (g_small,) + tuple(small_out)):
        dst.update(_unpack_small(t))

    part = _pack_replicated(jnp.concatenate([dg_mix0, dg_mix1]), jnp.concatenate([dg_ffn0, dg_ffn1]),
                            dbqkv, -dsink[:, 0], dbo, dg_fin)
    parts = _all_gather("ag_replicated_grads", part)
    w_rep = _pack_replicated(norm_mix, norm_ffn, attn_b_qkv, attn_sinks, attn_b_o, norm_final)
    m_rep = _pack_replicated(m_norm_mix, m_norm_ffn, m_attn_b_qkv, m_attn_sinks, m_attn_b_o, m_norm_final)
    v_rep = _pack_replicated(v_norm_mix, v_norm_ffn, v_attn_b_qkv, v_attn_sinks, v_attn_b_o, v_norm_final)
    replicated = [_unpack_replicated(t) for t in _adamw_replicated(parts, w_rep, m_rep, v_rep)]

    outs = [loss, dx.reshape(1, seq, D_MODEL)]
    for sh, rp in zip(sharded, replicated):
        merged = {**sh, **rp}
        outs += [merged[n] for n in WEIGHT_ORDER]
    return tuple(outs)
```

```python
import jax
import jax.numpy as jnp
from jax import lax
from jax.experimental import pallas as pl
from jax.experimental.pallas import tpu as pltpu

F32 = jnp.float32
BF16 = jnp.bfloat16

D_MODEL = 1024
HEAD_DIM = 64
N_Q_HEADS = 16
N_KV_HEADS = 2
Q_PER_KV = 8
Q_DIM = N_Q_HEADS * HEAD_DIM
KV_DIM = N_KV_HEADS * HEAD_DIM
QKV_DIM = Q_DIM + 2 * KV_DIM
BLOCK = 128
CONV_WIDTH = 31
D_FF = 2816
ROPE_THETA = 10000.0
RMS_EPS = 1e-5
LN_EPS = 1e-5
ADAM_LR = 0.001
ADAM_B1 = 0.9
ADAM_B2 = 0.999
ADAM_EPS = 1e-08
ADAM_WD = 0.01
ADAM_STEP = 10
N_DEV = 8

LANES = 128
SUBLANES = 8
TOKEN_TILE = 512
CONV_CHUNK = 64
CONV_HALO = 32
TAPS_PAD = 32
VMEM_LIMIT = 56 * 1024 * 1024
NEG_INF = float(jnp.finfo(jnp.float32).min)

ROWS_FF = D_FF // N_DEV
W1T_SLOT, W3T_SLOT, W2_SLOT = 0, 1, 2
ROWS_QKV = QKV_DIM // N_DEV
ROWS_WO = Q_DIM // N_DEV
ROWS_PW1 = 2 * D_MODEL // N_DEV
ROWS_PW2 = D_MODEL // N_DEV
ROWS_SMALL = 16
SMALL_USED = 2 * D_MODEL // N_DEV + CONV_WIDTH * (D_MODEL // N_DEV) + 4 * (D_MODEL // N_DEV)
FF_SPLIT = 2
FF_TILE_DEVS = N_DEV // FF_SPLIT
FF_TILE = FF_TILE_DEVS * ROWS_FF
REPL_ROWS = 16


def _call(body, *, name, grid, in_specs, out_specs, out_shape, scratch=(), sem=None):
    return pl.pallas_call(
        body, name=name, grid=grid, in_specs=in_specs, out_specs=out_specs, out_shape=out_shape,
        scratch_shapes=list(scratch),
        compiler_params=pltpu.CompilerParams(dimension_semantics=sem, vmem_limit_bytes=VMEM_LIMIT))


def _full(shape):
    return pl.BlockSpec(shape, lambda *_: (0,) * len(shape))


def _resident(shape):
    return pl.BlockSpec(shape, lambda *_: (0,) * len(shape), pipeline_mode=pl.Buffered(1))


def _rows(tm, n):
    return pl.BlockSpec((tm, n), lambda i, *_: (i, 0))


def _sig(x):
    return 1.0 / (1.0 + jnp.exp(-x))


def _sum_rows(x):
    return jnp.sum(x, axis=0, keepdims=True)


def _nt(a, b):
    return lax.dot_general(a, b, (((1,), (1,)), ((), ())), preferred_element_type=F32)


def _tn(a, b):
    return lax.dot_general(a, b, (((0,), (0,)), ((), ())), preferred_element_type=F32)


def _rms_stats(x):
    return lax.rsqrt(jnp.mean(x * x, axis=-1, keepdims=True) + RMS_EPS)


def _rms_bwd(dy, x, g, dres):
    r = _rms_stats(x)
    n = x * r
    dn = dy * g
    dx = dres + r * (dn - n * jnp.mean(dn * n, axis=-1, keepdims=True))
    return dx, _sum_rows(dy * n)


def _rope_tables(seq):
    pos = jnp.arange(seq, dtype=F32)
    inv_freq = ROPE_THETA ** (-jnp.arange(0, HEAD_DIM, 2, dtype=F32) / HEAD_DIM)
    ang = pos[:, None] * inv_freq[None, :]
    cos, sin = jnp.cos(ang), jnp.sin(ang)
    zero = jnp.zeros_like(sin)
    reps = LANES // HEAD_DIM
    c = jnp.tile(jnp.concatenate([cos, cos], axis=1), (1, reps))
    sa = jnp.tile(jnp.concatenate([-sin, zero], axis=1), (1, reps))
    sb = jnp.tile(jnp.concatenate([zero, sin], axis=1), (1, reps))
    return c, sa, sb


def _rope(t, c, sa, sb):
    half = HEAD_DIM // 2
    return t * c + pltpu.roll(t, LANES - half, 1) * sa + pltpu.roll(t, half, 1) * sb


def _rope_t(dt, c, sa, sb):
    half = HEAD_DIM // 2
    return dt * c + pltpu.roll(dt * sa, half, 1) + pltpu.roll(dt * sb, LANES - half, 1)


def _qkv_fwd(x, g, w, b, rc, rsa, rsb):
    seq = x.shape[0]
    tm = min(TOKEN_TILE, seq)

    def body(x_ref, g_ref, w_ref, b_ref, c_ref, sa_ref, sb_ref, y_ref, q_ref, k_ref, v_ref):
        xv = x_ref[...]
        y = (xv * _rms_stats(xv) * g_ref[...]).astype(BF16)
        y_ref[...] = y
        qkv = _nt(y, w_ref[...]) + b_ref[...]
        c, sa, sb = c_ref[...], sa_ref[...], sb_ref[...]
        for i in range(Q_DIM // LANES):
            blk = _rope(qkv[:, i * LANES:(i + 1) * LANES], c, sa, sb)
            q_ref[:, i * LANES:(i + 1) * LANES] = (blk * (HEAD_DIM ** -0.5)).astype(BF16)
        k_ref[...] = _rope(qkv[:, Q_DIM:Q_DIM + KV_DIM], c, sa, sb).astype(BF16)
        v_ref[...] = qkv[:, Q_DIM + KV_DIM:].astype(BF16)

    return _call(
        body, name="qkv_fwd", grid=(seq // tm,),
        in_specs=[_rows(tm, D_MODEL), _full((1, D_MODEL)), _full((QKV_DIM, D_MODEL)), _full((1, QKV_DIM)),
                  _rows(tm, LANES), _rows(tm, LANES), _rows(tm, LANES)],
        out_specs=[_rows(tm, D_MODEL), _rows(tm, Q_DIM), _rows(tm, KV_DIM), _rows(tm, KV_DIM)],
        out_shape=[jax.ShapeDtypeStruct((seq, D_MODEL), BF16), jax.ShapeDtypeStruct((seq, Q_DIM), BF16),
                   jax.ShapeDtypeStruct((seq, KV_DIM), BF16), jax.ShapeDtypeStruct((seq, KV_DIM), BF16)],
        sem=("parallel",))(x, g, w, b, rc, rsa, rsb)


GROUP_ROWS = Q_PER_KV * BLOCK


def _band_mask(n, rows=GROUP_ROWS):
    row = lax.broadcasted_iota(jnp.int32, (rows, 2 * BLOCK), 0) & (BLOCK - 1)
    col = lax.broadcasted_iota(jnp.int32, (rows, 2 * BLOCK), 1)
    rel = row + BLOCK - col
    return (rel >= 0) & (rel < BLOCK) & ((col >= BLOCK) | (n > 0))


def _softmax_with_sink(s, mask, sink):
    s = jnp.where(mask, s, NEG_INF)
    m = jnp.maximum(jnp.max(s, axis=-1, keepdims=True), sink)
    p = jnp.exp(s - m)
    e_sink = jnp.exp(sink - m)
    inv = 1.0 / (jnp.sum(p, axis=-1, keepdims=True) + e_sink)
    return p * inv, e_sink * inv


PAIRS_PER_KV = Q_PER_KV // 2


def _kv_specs():
    cur = pl.BlockSpec((BLOCK, KV_DIM), lambda n: (n, 0))
    prev = pl.BlockSpec((BLOCK, KV_DIM), lambda n: (jnp.maximum(n - 1, 0), 0))
    return cur, prev


def _low_lanes():
    return lax.broadcasted_iota(jnp.int32, (2 * BLOCK, KV_DIM), 1) < HEAD_DIM


def _kv_low_high(prev_ref, cur_ref, j, low):
    both = jnp.concatenate([prev_ref[...], cur_ref[...]], axis=0).astype(F32)
    swapped = pltpu.roll(both, HEAD_DIM, 1)
    at_low, at_high = (both, swapped) if j == 0 else (swapped, both)
    return jnp.where(low, at_low, 0.0).astype(BF16), jnp.where(low, 0.0, at_high).astype(BF16)


def _fold_pair_halves(acc, j, low):
    folded = acc + pltpu.roll(acc, HEAD_DIM, 1)
    return jnp.where(low, folded, 0.0) if j == 0 else jnp.where(low, 0.0, folded)


def _pair_lanes(j, i):
    g = j * PAIRS_PER_KV + i
    return slice(g * LANES, (g + 1) * LANES), 2 * g


def _attn_fwd(sinks, q, k, v):
    seq = q.shape[0]
    cur, prev = _kv_specs()

    def body(sink_ref, q_ref, kc_ref, kp_ref, vc_ref, vp_ref, o_ref, p_ref, ps_ref):
        mask = _band_mask(pl.program_id(0), BLOCK)
        lane = lax.broadcasted_iota(jnp.int32, (BLOCK, LANES), 1)
        p_sinks = jnp.zeros((BLOCK, LANES), F32)
        for j in range(N_KV_HEADS):
            cs = slice(j * HEAD_DIM, (j + 1) * HEAD_DIM)
            kk = jnp.concatenate([kp_ref[:, cs], kc_ref[:, cs]], axis=0)
            vv = jnp.concatenate([vp_ref[:, cs], vc_ref[:, cs]], axis=0)
            for gq in range(Q_PER_KV):
                h = j * Q_PER_KV + gq
                hs = slice(h * HEAD_DIM, (h + 1) * HEAD_DIM)
                probs, p_sink = _softmax_with_sink(_nt(q_ref[:, hs], kk), mask, sink_ref[h])
                pb = probs.astype(BF16)
                p_ref[h] = pb
                p_sinks = jnp.where(lane == h, p_sink, p_sinks)
                o_ref[:, hs] = jnp.dot(pb, vv, preferred_element_type=F32).astype(BF16)
        ps_ref[...] = p_sinks

    return _call(
        body, name="attn_fwd", grid=(seq // BLOCK,),
        in_specs=[pl.BlockSpec(memory_space=pltpu.SMEM), _rows(BLOCK, Q_DIM), cur, prev, cur, prev],
        out_specs=[_rows(BLOCK, Q_DIM), pl.BlockSpec((N_Q_HEADS, BLOCK, 2 * BLOCK), lambda n: (0, n, 0)),
                   _rows(BLOCK, LANES)],
        out_shape=[jax.ShapeDtypeStruct((seq, Q_DIM), BF16),
                   jax.ShapeDtypeStruct((N_Q_HEADS, seq, 2 * BLOCK), BF16),
                   jax.ShapeDtypeStruct((seq, LANES), F32)],
        sem=("parallel",))(sinks, q, k, k, v, v)


def _mm_res(name, a, w, b, res):
    seq, kdim = a.shape
    n = w.shape[1]
    tm = min(2 * TOKEN_TILE, seq)

    def body(a_ref, w_ref, b_ref, r_ref, o_ref):
        o_ref[...] = r_ref[...] + (jnp.dot(a_ref[...], w_ref[...], preferred_element_type=F32) + b_ref[...])

    return _call(
        body, name=name, grid=(seq // tm,),
        in_specs=[_rows(tm, kdim), _resident((kdim, n)), _full((1, n)), _rows(tm, n)],
        out_specs=_rows(tm, n), out_shape=jax.ShapeDtypeStruct((seq, n), F32),
        sem=("parallel",))(a, w, b, res)


def _ff_tile_spec(slot):
    return pl.BlockSpec((FF_TILE_DEVS, ROWS_FF, D_MODEL), lambda i, j: (j, slot, 0))


def _ff_whole_spec(slot):
    return pl.BlockSpec((N_DEV, ROWS_FF, D_MODEL), lambda i: (0, slot, 0), pipeline_mode=pl.Buffered(1))


def _ffn_up(name, h, g, gathered):
    seq = h.shape[0]
    tm = min(TOKEN_TILE, seq)

    def body(h_ref, g_ref, w1_ref, w3_ref, f_ref, u_ref, w_ref, s_ref):
        @pl.when(pl.program_id(1) == 0)
        def _():
            hv = h_ref[...]
            f_ref[...] = (hv * _rms_stats(hv) * g_ref[...]).astype(BF16)

        f = f_ref[...]
        u = _nt(f, w1_ref[...].reshape(FF_TILE, D_MODEL))
        w = _nt(f, w3_ref[...].reshape(FF_TILE, D_MODEL))
        u_ref[...] = u.astype(BF16)
        w_ref[...] = w.astype(BF16)
        s_ref[...] = (u * _sig(u) * w).astype(BF16)

    tile_ff = pl.BlockSpec((tm, FF_TILE), lambda i, j: (i, j))
    ff_shape = jax.ShapeDtypeStruct((seq, D_FF), BF16)
    return _call(
        body, name=name, grid=(seq // tm, FF_SPLIT),
        in_specs=[_rows(tm, D_MODEL), _full((1, D_MODEL)), _ff_tile_spec(W1T_SLOT), _ff_tile_spec(W3T_SLOT)],
        out_specs=[_rows(tm, D_MODEL), tile_ff, tile_ff, tile_ff],
        out_shape=[jax.ShapeDtypeStruct((seq, D_MODEL), BF16), ff_shape, ff_shape, ff_shape],
        sem=("parallel", "arbitrary"))(h, g, gathered, gathered)


def _ffn_down(name, s, gathered, res):
    seq = s.shape[0]
    tm = min(2 * TOKEN_TILE, seq)

    def body(s_ref, w_ref, r_ref, o_ref):
        w2 = w_ref[...].reshape(D_FF, D_MODEL)
        o_ref[...] = r_ref[...] + jnp.dot(s_ref[...], w2, preferred_element_type=F32)

    return _call(
        body, name=name, grid=(seq // tm,),
        in_specs=[_rows(tm, D_FF), _ff_whole_spec(W2_SLOT), _rows(tm, D_MODEL)],
        out_specs=_rows(tm, D_MODEL), out_shape=jax.ShapeDtypeStruct((seq, D_MODEL), F32),
        sem=("parallel",))(s, gathered, res)


def _conv_fwd(h, g, wpw1, bpw1, wd, bdw, lng, lnb):
    seq = h.shape[0]
    tm = min(TOKEN_TILE, seq)
    n_chunks = tm // CONV_CHUNK
    win = CONV_CHUNK + CONV_HALO

    def body(h_ref, g_ref, w_ref, b_ref, wd_ref, bdw_ref, lng_ref, lnb_ref,
             y_ref, a_ref, dwc_ref, z_ref, gbuf):
        i = pl.program_id(0)

        @pl.when(i == 0)
        def _():
            gbuf[0:CONV_HALO, :] = jnp.zeros((CONV_HALO, D_MODEL), F32)

        @pl.when(i > 0)
        def _():
            gbuf[0:CONV_HALO, :] = gbuf[tm:tm + CONV_HALO, :]

        hv = h_ref[...]
        y = (hv * _rms_stats(hv) * g_ref[...]).astype(BF16)
        y_ref[...] = y
        a = _nt(y, w_ref[...]) + b_ref[...]
        a_ref[...] = a.astype(BF16)
        gbuf[CONV_HALO:CONV_HALO + tm, :] = a[:, :D_MODEL] * _sig(a[:, D_MODEL:])

        def chunk(c, carry):
            r0 = pl.multiple_of(c * CONV_CHUNK, CONV_CHUNK)
            for l in range(D_MODEL // LANES):
                ls = slice(l * LANES, (l + 1) * LANES)
                gw = gbuf[pl.ds(r0, win), ls]
                acc = jnp.zeros((CONV_CHUNK, LANES), F32) + bdw_ref[:, ls]
                for s in range(SUBLANES):
                    gs = gw if s == 0 else pltpu.roll(gw, s, 0)
                    for q in range(CONV_HALO // SUBLANES):
                        d = SUBLANES * q + s
                        if d < CONV_WIDTH:
                            lo = CONV_HALO - SUBLANES * q
                            acc = acc + wd_ref[d:d + 1, ls] * gs[lo:lo + CONV_CHUNK]
                dwc_ref[pl.ds(r0, CONV_CHUNK), ls] = acc
            return carry

        lax.fori_loop(0, n_chunks, chunk, 0)

        xv = dwc_ref[...]
        mu = jnp.mean(xv, axis=-1, keepdims=True)
        xc = xv - mu
        var = jnp.mean(xc * xc, axis=-1, keepdims=True)
        ln = xc * lax.rsqrt(var + LN_EPS) * lng_ref[...] + lnb_ref[...]
        z_ref[...] = (ln * _sig(ln)).astype(BF16)

    return _call(
        body, name="conv_fwd", grid=(seq // tm,),
        in_specs=[_rows(tm, D_MODEL), _full((1, D_MODEL)), _full((2 * D_MODEL, D_MODEL)), _full((1, 2 * D_MODEL)),
                  _full((TAPS_PAD, D_MODEL)), _full((1, D_MODEL)), _full((1, D_MODEL)), _full((1, D_MODEL))],
        out_specs=[_rows(tm, D_MODEL), _rows(tm, 2 * D_MODEL), _rows(tm, D_MODEL), _rows(tm, D_MODEL)],
        out_shape=[jax.ShapeDtypeStruct((seq, D_MODEL), BF16), jax.ShapeDtypeStruct((seq, 2 * D_MODEL), BF16),
                   jax.ShapeDtypeStruct((seq, D_MODEL), F32), jax.ShapeDtypeStruct((seq, D_MODEL), BF16)],
        scratch=[pltpu.VMEM((tm + CONV_HALO, D_MODEL), F32)],
        sem=("arbitrary",))(h, g, wpw1, bpw1, wd, bdw, lng, lnb)


def _final_loss(h, target, g):
    seq = h.shape[0]
    tm = min(TOKEN_TILE, seq)

    def body(h_ref, t_ref, g_ref, dh_ref, loss_ref, dg_ref):
        @pl.when(pl.program_id(0) == 0)
        def _():
            loss_ref[...] = jnp.zeros_like(loss_ref)
            dg_ref[...] = jnp.zeros_like(dg_ref)

        hv = h_ref[...]
        gv = g_ref[...]
        err = hv * _rms_stats(hv) * gv - t_ref[...]
        sq = jnp.sum(jnp.sum(err * err, axis=-1, keepdims=True), axis=0, keepdims=True)
        loss_ref[...] += jnp.broadcast_to(sq, loss_ref.shape)
        dx, dg = _rms_bwd(err * (1.0 / D_MODEL), hv, gv, 0.0)
        dh_ref[...] = dx
        dg_ref[...] += dg

    return _call(
        body, name="final_loss", grid=(seq // tm,),
        in_specs=[_rows(tm, D_MODEL), _rows(tm, D_MODEL), _full((1, D_MODEL))],
        out_specs=[_rows(tm, D_MODEL), _full((SUBLANES, LANES)), _full((1, D_MODEL))],
        out_shape=[jax.ShapeDtypeStruct((seq, D_MODEL), F32), jax.ShapeDtypeStruct((SUBLANES, LANES), F32),
                   jax.ShapeDtypeStruct((1, D_MODEL), F32)],
        sem=("arbitrary",))(h, target, g)


def _ffn_bwd_act(name, dh, u, w, gathered):
    seq = dh.shape[0]
    tm = min(TOKEN_TILE, seq)

    def body(dh_ref, u_ref, w_ref, w2_ref, du_ref, dw_ref, dhb):
        @pl.when(pl.program_id(1) == 0)
        def _():
            dhb[...] = dh_ref[...].astype(BF16)

        ds = _nt(dhb[...], w2_ref[...].reshape(FF_TILE, D_MODEL))
        uv = u_ref[...].astype(F32)
        sg = _sig(uv)
        dw_ref[...] = (ds * (uv * sg)).astype(BF16)
        du_ref[...] = (ds * w_ref[...].astype(F32) * (sg * (1.0 + uv * (1.0 - sg)))).astype(BF16)

    tile_ff = pl.BlockSpec((tm, FF_TILE), lambda i, j: (i, j))
    ff_shape = jax.ShapeDtypeStruct((seq, D_FF), BF16)
    return _call(
        body, name=name, grid=(seq // tm, FF_SPLIT),
        in_specs=[_rows(tm, D_MODEL), tile_ff, tile_ff, _ff_tile_spec(W2_SLOT)],
        out_specs=[tile_ff, tile_ff], out_shape=[ff_shape, ff_shape],
        scratch=[pltpu.VMEM((tm, D_MODEL), BF16)],
        sem=("parallel", "arbitrary"))(dh, u, w, gathered)


def _ffn_bwd_in(name, du, dw, gathered, h_in, dres, g):
    seq = du.shape[0]
    tm = min(TOKEN_TILE, seq)

    def body(du_ref, dw_ref, w1_ref, w3_ref, h_ref, dr_ref, g_ref, dx_ref, dg_ref):
        @pl.when(pl.program_id(0) == 0)
        def _():
            dg_ref[...] = jnp.zeros_like(dg_ref)

        df = jnp.dot(du_ref[...], w1_ref[...].reshape(D_FF, D_MODEL), preferred_element_type=F32)
        df = df + jnp.dot(dw_ref[...], w3_ref[...].reshape(D_FF, D_MODEL), preferred_element_type=F32)
        dx, dg = _rms_bwd(df, h_ref[...], g_ref[...], dr_ref[...])
        dx_ref[...] = dx
        dg_ref[...] += dg

    return _call(
        body, name=name, grid=(seq // tm,),
        in_specs=[_rows(tm, D_FF), _rows(tm, D_FF), _ff_whole_spec(W1T_SLOT), _ff_whole_spec(W3T_SLOT),
                  _rows(tm, D_MODEL), _rows(tm, D_MODEL), _full((1, D_MODEL))],
        out_specs=[_rows(tm, D_MODEL), _full((1, D_MODEL))],
        out_shape=[jax.ShapeDtypeStruct((seq, D_MODEL), F32), jax.ShapeDtypeStruct((1, D_MODEL), F32)],
        sem=("arbitrary",))(du, dw, gathered, gathered, h_in, dres, g)


def _mm_tn(name, a, b, *, tk):
    seq, kdim = a.shape
    n = b.shape[1]
    tt = min(2 * TOKEN_TILE, seq)
    n_t = seq // tt

    def body(a_ref, b_ref, o_ref, acc):
        t = pl.program_id(1)

        @pl.when(t == 0)
        def _():
            acc[...] = jnp.zeros_like(acc)

        acc[...] += _tn(a_ref[...].astype(BF16), b_ref[...].astype(BF16))

        @pl.when(t == n_t - 1)
        def _():
            o_ref[...] = acc[...].astype(BF16)

    return _call(
        body, name=name, grid=(kdim // tk, n_t),
        in_specs=[pl.BlockSpec((tt, tk), lambda k, t: (t, k)), pl.BlockSpec((tt, n), lambda k, t: (t, 0))],
        out_specs=pl.BlockSpec((tk, n), lambda k, t: (k, 0)),
        out_shape=jax.ShapeDtypeStruct((kdim, n), BF16),
        scratch=[pltpu.VMEM((tk, n), F32)],
        sem=("parallel", "arbitrary"))(a, b)


def _conv_bwd(dh, wpw2, dwc, a, lng, lnb, wd):
    seq = dh.shape[0]
    tm = min(TOKEN_TILE, seq)
    nt = seq // tm
    n_chunks = tm // CONV_CHUNK
    win = CONV_CHUNK + CONV_HALO
    halo_per_tile = tm // CONV_HALO

    def body(dh_ref, w_ref, dwc_ref, a_ref, ah_ref, lng_ref, lnb_ref, wd_ref,
             da_ref, dlg_ref, dlb_ref, dbdw_ref, dwd_ref, dbpw1_ref, dbpw2_ref,
             gbuf, dbuf, dglu, dwd_part):
        i = pl.program_id(0)
        r = nt - 1 - i

        @pl.when(i == 0)
        def _():
            dlg_ref[...] = jnp.zeros_like(dlg_ref)
            dlb_ref[...] = jnp.zeros_like(dlb_ref)
            dbdw_ref[...] = jnp.zeros_like(dbdw_ref)
            dbpw1_ref[...] = jnp.zeros_like(dbpw1_ref)
            dbpw2_ref[...] = jnp.zeros_like(dbpw2_ref)
            dwd_part[...] = jnp.zeros_like(dwd_part)
            dbuf[tm:tm + CONV_HALO, :] = jnp.zeros((CONV_HALO, D_MODEL), F32)

        @pl.when(i > 0)
        def _():
            dbuf[tm:tm + CONV_HALO, :] = dbuf[0:CONV_HALO, :]

        dhv = dh_ref[...]
        dbpw2_ref[...] += _sum_rows(dhv)
        dz = _nt(dhv.astype(BF16), w_ref[...])
        xv = dwc_ref[...]
        lg = lng_ref[...]
        mu = jnp.mean(xv, axis=-1, keepdims=True)
        xc = xv - mu
        rstd = lax.rsqrt(jnp.mean(xc * xc, axis=-1, keepdims=True) + LN_EPS)
        xhat = xc * rstd
        ln = xhat * lg + lnb_ref[...]
        sg = _sig(ln)
        dln = dz * (sg * (1.0 + ln * (1.0 - sg)))
        dlg_ref[...] += _sum_rows(dln * xhat)
        dlb_ref[...] += _sum_rows(dln)
        dxh = dln * lg
        ddw = rstd * (dxh - jnp.mean(dxh, axis=-1, keepdims=True)
                      - xhat * jnp.mean(dxh * xhat, axis=-1, keepdims=True))
        dbdw_ref[...] += _sum_rows(ddw)
        dbuf[0:tm, :] = ddw

        av = a_ref[...].astype(F32)
        a1 = av[:, :D_MODEL]
        s2 = _sig(av[:, D_MODEL:])
        gbuf[CONV_HALO:CONV_HALO + tm, :] = a1 * s2
        ah = ah_ref[...].astype(F32)
        gh = ah[:, :D_MODEL] * _sig(ah[:, D_MODEL:])
        gbuf[0:CONV_HALO, :] = jnp.where(r > 0, gh, 0.0)

        def chunk(c, carry):
            r0 = pl.multiple_of(c * CONV_CHUNK, CONV_CHUNK)
            for l in range(D_MODEL // LANES):
                ls = slice(l * LANES, (l + 1) * LANES)
                dw_ = dbuf[pl.ds(r0, win), ls]
                gw = gbuf[pl.ds(r0, win), ls]
                dc = dw_[0:CONV_CHUNK]
                acc = jnp.zeros((CONV_CHUNK, LANES), F32)
                for s in range(SUBLANES):
                    ds_ = dw_ if s == 0 else pltpu.roll(dw_, win - s, 0)
                    gs = gw if s == 0 else pltpu.roll(gw, s, 0)
                    for q in range(CONV_HALO // SUBLANES):
                        d = SUBLANES * q + s
                        if d < CONV_WIDTH:
                            acc = acc + wd_ref[d:d + 1, ls] * ds_[SUBLANES * q:SUBLANES * q + CONV_CHUNK]
                            lo = CONV_HALO - SUBLANES * q
                            prod = dc * gs[lo:lo + CONV_CHUNK]
                            dwd_part[d, :, ls] += jnp.sum(
                                prod.reshape(CONV_CHUNK // SUBLANES, SUBLANES, LANES), axis=0)
                dglu[pl.ds(r0, CONV_CHUNK), ls] = acc
            return carry

        lax.fori_loop(0, n_chunks, chunk, 0)

        dg_ = dglu[...]
        da1 = dg_ * s2
        da2 = dg_ * a1 * s2 * (1.0 - s2)
        da_ref[:, :D_MODEL] = da1.astype(BF16)
        da_ref[:, D_MODEL:] = da2.astype(BF16)
        dbpw1_ref[:, :D_MODEL] += _sum_rows(da1)
        dbpw1_ref[:, D_MODEL:] += _sum_rows(da2)

        @pl.when(i == nt - 1)
        def _():
            dwd_ref[...] = jnp.sum(dwd_part[...], axis=1)

    rev = lambda n: pl.BlockSpec((tm, n), lambda i: (nt - 1 - i, 0))
    halo = pl.BlockSpec((CONV_HALO, 2 * D_MODEL),
                        lambda i: (jnp.maximum((nt - 1 - i) * halo_per_tile - 1, 0), 0))
    vec = lambda n: _full((1, n))
    return _call(
        body, name="conv_bwd", grid=(nt,),
        in_specs=[rev(D_MODEL), _full((D_MODEL, D_MODEL)), rev(D_MODEL), rev(2 * D_MODEL), halo,
                  vec(D_MODEL), vec(D_MODEL), _full((TAPS_PAD, D_MODEL))],
        out_specs=[rev(2 * D_MODEL), vec(D_MODEL), vec(D_MODEL), vec(D_MODEL), _full((TAPS_PAD, D_MODEL)),
                   vec(2 * D_MODEL), vec(D_MODEL)],
        out_shape=[jax.ShapeDtypeStruct((seq, 2 * D_MODEL), BF16), jax.ShapeDtypeStruct((1, D_MODEL), F32),
                   jax.ShapeDtypeStruct((1, D_MODEL), F32), jax.ShapeDtypeStruct((1, D_MODEL), F32),
                   jax.ShapeDtypeStruct((TAPS_PAD, D_MODEL), F32), jax.ShapeDtypeStruct((1, 2 * D_MODEL), F32),
                   jax.ShapeDtypeStruct((1, D_MODEL), F32)],
        scratch=[pltpu.VMEM((tm + CONV_HALO, D_MODEL), F32), pltpu.VMEM((tm + CONV_HALO, D_MODEL), F32),
                 pltpu.VMEM((tm, D_MODEL), F32), pltpu.VMEM((TAPS_PAD, SUBLANES, D_MODEL), F32)],
        sem=("arbitrary",))(dh, wpw2, dwc, a, a, lng, lnb, wd)


def _mm_rms_bwd(name, dact, wt, h_in, dres, g):
    seq, n = dact.shape
    tm = min(TOKEN_TILE, seq)

    def body(da_ref, w_ref, h_ref, dr_ref, g_ref, dx_ref, dg_ref):
        @pl.when(pl.program_id(0) == 0)
        def _():
            dg_ref[...] = jnp.zeros_like(dg_ref)

        dy = jnp.dot(da_ref[...], w_ref[...], preferred_element_type=F32)
        dx, dg = _rms_bwd(dy, h_ref[...], g_ref[...], dr_ref[...])
        dx_ref[...] = dx
        dg_ref[...] += dg

    return _call(
        body, name=name, grid=(seq // tm,),
        in_specs=[_rows(tm, n), _resident((n, D_MODEL)), _rows(tm, D_MODEL), _rows(tm, D_MODEL), _full((1, D_MODEL))],
        out_specs=[_rows(tm, D_MODEL), _full((1, D_MODEL))],
        out_shape=[jax.ShapeDtypeStruct((seq, D_MODEL), F32), jax.ShapeDtypeStruct((1, D_MODEL), F32)],
        sem=("arbitrary",))(dact, wt, h_in, dres, g)


def _nt_bias(name, dy, w):
    seq, n = dy.shape
    kdim = w.shape[0]
    tm = min(2 * TOKEN_TILE, seq)

    def body(dy_ref, w_ref, o_ref, db_ref):
        @pl.when(pl.program_id(0) == 0)
        def _():
            db_ref[...] = jnp.zeros_like(db_ref)

        dyv = dy_ref[...]
        db_ref[...] += _sum_rows(dyv)
        o_ref[...] = _nt(dyv.astype(BF16), w_ref[...]).astype(BF16)

    return _call(
        body, name=name, grid=(seq // tm,),
        in_specs=[_rows(tm, n), _resident((kdim, n))],
        out_specs=[_rows(tm, kdim), _full((1, n))],
        out_shape=[jax.ShapeDtypeStruct((seq, kdim), BF16), jax.ShapeDtypeStruct((1, n), F32)],
        sem=("arbitrary",))(dy, w)


def _attn_bwd(probs, p_sinks, q, k, v, do):
    seq = q.shape[0]
    cur, prev = _kv_specs()

    def body(p_ref, ps_ref, q_ref, kc_ref, kp_ref, vc_ref, vp_ref, do_ref,
             dq_ref, dkc_ref, dkp_ref, dvc_ref, dvp_ref, dsink_ref):
        n = pl.program_id(0)

        @pl.when(n == 0)
        def _():
            dsink_ref[...] = jnp.zeros_like(dsink_ref)

        low = _low_lanes()
        lane = lax.broadcasted_iota(jnp.int32, (BLOCK, LANES), 1)
        p_sinks_blk = ps_ref[...]
        dk_all = jnp.zeros((2 * BLOCK, KV_DIM), F32)
        dv_all = jnp.zeros((2 * BLOCK, KV_DIM), F32)
        for j in range(N_KV_HEADS):
            k_lo, k_hi = _kv_low_high(kp_ref, kc_ref, j, low)
            v_lo, v_hi = _kv_low_high(vp_ref, vc_ref, j, low)
            dk_acc = jnp.zeros((2 * BLOCK, KV_DIM), F32)
            dv_acc = jnp.zeros((2 * BLOCK, KV_DIM), F32)
            for i in range(PAIRS_PER_KV):
                ls, h = _pair_lanes(j, i)
                qp = q_ref[:, ls]
                dop = do_ref[:, ls]
                dsb, pb16 = [], []
                for t, v_sel in enumerate((v_lo, v_hi)):
                    pb = p_ref[h + t]
                    pf = pb.astype(F32)
                    p_sink = jnp.sum(jnp.where(lane == h + t, p_sinks_blk, 0.0), axis=-1, keepdims=True)
                    dp = _nt(dop, v_sel)
                    delta = jnp.sum(pf * dp, axis=-1, keepdims=True)
                    dsb.append((pf * (dp - delta)).astype(BF16))
                    pb16.append(pb)
                    dsink_ref[h + t:h + t + 1, :] += jnp.broadcast_to(_sum_rows(p_sink * delta), (1, LANES))
                dq = (jnp.dot(dsb[0], k_lo, preferred_element_type=F32)
                      + jnp.dot(dsb[1], k_hi, preferred_element_type=F32))
                dq_ref[:, ls] = dq * (HEAD_DIM ** -0.5)
                dk_acc = dk_acc + jnp.where(low, _tn(dsb[0], qp), _tn(dsb[1], qp))
                dv_acc = dv_acc + jnp.where(low, _tn(pb16[0], dop), _tn(pb16[1], dop))
            dk_all = dk_all + _fold_pair_halves(dk_acc, j, low)
            dv_all = dv_all + _fold_pair_halves(dv_acc, j, low)
        dkp_ref[...] = dk_all[:BLOCK]
        dkc_ref[...] = dk_all[BLOCK:]
        dvp_ref[...] = dv_all[:BLOCK]
        dvc_ref[...] = dv_all[BLOCK:]

    kv_out = _rows(BLOCK, KV_DIM)
    kv_shape = jax.ShapeDtypeStruct((seq, KV_DIM), F32)
    return _call(
        body, name="attn_bwd", grid=(seq // BLOCK,),
        in_specs=[pl.BlockSpec((N_Q_HEADS, BLOCK, 2 * BLOCK), lambda n: (0, n, 0)), _rows(BLOCK, LANES),
                  _rows(BLOCK, Q_DIM), cur, prev, cur, prev, _rows(BLOCK, Q_DIM)],
        out_specs=[_rows(BLOCK, Q_DIM), kv_out, kv_out, kv_out, kv_out, _full((N_Q_HEADS, LANES))],
        out_shape=[jax.ShapeDtypeStruct((seq, Q_DIM), F32), kv_shape, kv_shape, kv_shape, kv_shape,
                   jax.ShapeDtypeStruct((N_Q_HEADS, LANES), F32)],
        sem=("arbitrary",))(probs, p_sinks, q, k, k, v, v, do)


def _rope_bwd(dq, dkc, dkp, dvc, dvp, rc, rsa, rsb):
    seq = dq.shape[0]
    nb = seq // BLOCK
    tm = min(TOKEN_TILE, seq)
    nt = seq // tm
    per = tm // BLOCK
    nxt = pl.BlockSpec((BLOCK, KV_DIM), lambda i: (jnp.minimum((i + 1) * per, nb - 1), 0))

    def body(dq_ref, dkc_ref, dkp_ref, dkn_ref, dvc_ref, dvp_ref, dvn_ref, c_ref, sa_ref, sb_ref, o_ref, db_ref):
        i = pl.program_id(0)

        @pl.when(i == 0)
        def _():
            db_ref[...] = jnp.zeros_like(db_ref)

        c, sa, sb = c_ref[...], sa_ref[...], sb_ref[...]
        last = i == nt - 1

        def from_next_block(prev_ref, next_ref):
            tail = jnp.where(last, 0.0, next_ref[...])
            return tail if per == 1 else jnp.concatenate([prev_ref[BLOCK:, :], tail], axis=0)

        dk = dkc_ref[...] + from_next_block(dkp_ref, dkn_ref)
        dv = dvc_ref[...] + from_next_block(dvp_ref, dvn_ref)
        for l in range(Q_DIM // LANES):
            ls = slice(l * LANES, (l + 1) * LANES)
            blk = _rope_t(dq_ref[:, ls], c, sa, sb)
            o_ref[:, ls] = blk.astype(BF16)
            db_ref[:, ls] += _sum_rows(blk)
        dkr = _rope_t(dk, c, sa, sb)
        o_ref[:, Q_DIM:Q_DIM + KV_DIM] = dkr.astype(BF16)
        db_ref[:, Q_DIM:Q_DIM + KV_DIM] += _sum_rows(dkr)
        o_ref[:, Q_DIM + KV_DIM:] = dv.astype(BF16)
        db_ref[:, Q_DIM + KV_DIM:] += _sum_rows(dv)

    kv = _rows(tm, KV_DIM)
    tab = _rows(tm, LANES)
    return _call(
        body, name="rope_bwd", grid=(nt,),
        in_specs=[_rows(tm, Q_DIM), kv, kv, nxt, kv, kv, nxt, tab, tab, tab],
        out_specs=[_rows(tm, QKV_DIM), _full((1, QKV_DIM))],
        out_shape=[jax.ShapeDtypeStruct((seq, QKV_DIM), BF16), jax.ShapeDtypeStruct((1, QKV_DIM), F32)],
        sem=("arbitrary",))(dq, dkc, dkp, dkp, dvc, dvp, dvp, rc, rsa, rsb)


def _adamw(w, g, m, v):
    m = ADAM_B1 * m + (1.0 - ADAM_B1) * g
    v = ADAM_B2 * v + (1.0 - ADAM_B2) * (g * g)
    m_hat = m / (1.0 - ADAM_B1 ** ADAM_STEP)
    v_hat = v / (1.0 - ADAM_B2 ** ADAM_STEP)
    delta = -ADAM_LR * (m_hat / (jnp.sqrt(v_hat) + ADAM_EPS) + ADAM_WD * w)
    return delta, m, v


def _sum_slots(name, parts):
    _, rows, cols = parts.shape
    tr = rows if rows <= 512 else ROWS_FF

    def body(p_ref, g_ref):
        g = p_ref[0].astype(F32)
        for d in range(1, N_DEV):
            g = g + p_ref[d].astype(F32)
        g_ref[...] = g

    return _call(
        body, name=name, grid=(rows // tr,),
        in_specs=[pl.BlockSpec((N_DEV, tr, cols), lambda i: (0, i, 0))],
        out_specs=_rows(tr, cols), out_shape=jax.ShapeDtypeStruct((rows, cols), F32),
        sem=("parallel",))(parts)


def _adamw_native(name, g, w, m, v):
    layers, rows, cols = w.shape
    tr = rows if rows <= 512 else 256

    def body(g_ref, w_ref, m_ref, v_ref, d_ref, nm_ref, nv_ref):
        d_ref[...], nm_ref[...], nv_ref[...] = _adamw(w_ref[...], g_ref[...], m_ref[...], v_ref[...])

    spec = pl.BlockSpec((1, tr, cols), lambda l, i: (l, i, 0))
    shape = jax.ShapeDtypeStruct(w.shape, F32)
    return _call(
        body, name=name, grid=(layers, rows // tr), in_specs=[spec, spec, spec, spec],
        out_specs=[spec, spec, spec], out_shape=[shape, shape, shape],
        sem=("parallel", "parallel"))(g, w, m, v)


def _adamw_replicated(parts, w, m, v):
    def body(p_ref, w_ref, m_ref, v_ref, g_ref, d_ref, nm_ref, nv_ref):
        g = p_ref[0]
        for j in range(1, N_DEV):
            g = g + p_ref[j]
        g_ref[...] = g
        d_ref[...], nm_ref[...], nv_ref[...] = _adamw(w_ref[...], g, m_ref[...], v_ref[...])

    spec = _full((REPL_ROWS, D_MODEL))
    shape = jax.ShapeDtypeStruct((REPL_ROWS, D_MODEL), F32)
    return _call(
        body, name="adamw_replicated", grid=(1,),
        in_specs=[_full((N_DEV, REPL_ROWS, D_MODEL)), spec, spec, spec],
        out_specs=[spec, spec, spec, spec], out_shape=[shape, shape, shape, shape],
        sem=("arbitrary",))(parts, w, m, v)


_MESH = pl.DeviceIdType.MESH
_ANY = pl.BlockSpec(memory_space=pl.ANY)


def _all_gather(name, xs):
    rows, cols = xs.shape

    def body(x_ref, out_ref, send_sems, recv_sems, local_sem):
        x, y, c = lax.axis_index("x"), lax.axis_index("y"), lax.axis_index("c")
        me, sibling = (x, y, c), (x, y, 1 - c)
        chips = [(1 - x, y), (x, 1 - y), (1 - x, 1 - y)]

        def slot(px, py, pc):
            return out_ref.at[4 * px + 2 * py + pc]

        def copy(k, block, to, src=None):
            return pltpu.make_async_remote_copy(
                src_ref=slot(*block) if src is None else src, dst_ref=slot(*block),
                send_sem=send_sems.at[k], recv_sem=recv_sems.at[k], device_id=to, device_id_type=_MESH)

        mine = pltpu.make_async_copy(x_ref, slot(*me), local_sem)
        mine.start()
        first = [copy(0, me, sibling, src=x_ref)]
        first += [copy(1 + j, me, (*chip, c), src=x_ref) for j, chip in enumerate(chips)]
        for cp in first:
            cp.start()
        passed = [copy(4 + j, (*chip, c), sibling) for j, chip in enumerate(chips)]
        for j, chip in enumerate(chips):
            copy(1 + j, (*chip, c), me).wait_recv()
            passed[j].start()
        copy(0, sibling, me).wait_recv()
        for j, chip in enumerate(chips):
            copy(4 + j, (*chip, 1 - c), me).wait_recv()
        for cp in first + passed:
            cp.wait_send()
        mine.wait()

    return pl.pallas_call(
        body, name=name, out_shape=jax.ShapeDtypeStruct((N_DEV, rows, cols), xs.dtype),
        in_specs=[_ANY], out_specs=_ANY,
        scratch_shapes=[pltpu.SemaphoreType.DMA((7,)), pltpu.SemaphoreType.DMA((7,)), pltpu.SemaphoreType.DMA],
    )(xs)


N_PEERS = N_DEV - 1
_HBM = pl.BlockSpec(memory_space=pltpu.HBM)
_SEM = pl.BlockSpec(memory_space=pltpu.SEMAPHORE)
_DATAFLOW = pltpu.SideEffectType.DATAFLOW_SIDE_EFFECTING
_TOKEN = jax.ShapeDtypeStruct((SUBLANES, LANES), F32)


def _peers():
    x, y, c = lax.axis_index("x"), lax.axis_index("y"), lax.axis_index("c")
    out = []
    for k in range(1, N_DEV):
        px = 1 - x if k & 4 else x
        py = 1 - y if k & 2 else y
        pc = 1 - c if k & 1 else c
        out.append(((px, py, pc), 4 * px + 2 * py + pc))
    return 4 * x + 2 * y + c, out


def _in_hbm(a):
    return pltpu.with_memory_space_constraint(a, pltpu.HBM)


def _landing(rows):
    return _in_hbm(lax.empty((N_DEV, rows, D_MODEL), BF16))


def _sem_pair():
    return pltpu.SemaphoreType.DMA((N_PEERS,)), pltpu.SemaphoreType.DMA((N_PEERS,))


def _gather_start(name, payloads):
    n = len(payloads)

    def body(*refs):
        src, land = refs[:n], refs[n:2 * n]
        sems = refs[2 * n:4 * n]
        token = refs[-1]
        me, peers = _peers()
        for g in range(n):
            for k, (pos, _) in enumerate(peers):
                pltpu.make_async_remote_copy(
                    src_ref=src[g], dst_ref=land[g].at[me], send_sem=sems[2 * g].at[k],
                    recv_sem=sems[2 * g + 1].at[k], device_id=pos, device_id_type=_MESH).start()
        token[...] = jnp.zeros_like(token)

    lands = [_landing(p.shape[0]) for p in payloads]
    sem_shapes = [s for _ in payloads for s in _sem_pair()]
    hbm_shapes = [pltpu.HBM(a.shape, a.dtype) for a in list(payloads) + lands]
    out = pl.pallas_call(
        body, name=name, out_shape=(*sem_shapes, *hbm_shapes, _TOKEN),
        in_specs=[_HBM] * (2 * n), out_specs=(*[_SEM] * (2 * n), *[_HBM] * (2 * n), pl.BlockSpec(memory_space=pltpu.VMEM)),
        input_output_aliases={i: 2 * n + i for i in range(2 * n)},
        compiler_params=pltpu.CompilerParams(has_side_effects=_DATAFLOW),
    )(*[_in_hbm(p) for p in payloads], *lands)
    sems, thru = out[:2 * n], out[2 * n:4 * n]
    return [(thru[g], thru[n + g], sems[2 * g], sems[2 * g + 1]) for g in range(n)], out[-1]


def _gather_wait(name, group, after):
    payload, land, send_sems, recv_sems = group

    def body(src_ref, land_ref, send_ref, recv_ref, after_ref, src_out, land_out):
        _, peers = _peers()
        for k, (pos, idx) in enumerate(peers):
            cp = pltpu.make_async_remote_copy(
                src_ref=src_ref, dst_ref=land_ref.at[idx], send_sem=send_ref.at[k], recv_sem=recv_ref.at[k],
                device_id=pos, device_id_type=_MESH)
            cp.wait_send()
            cp.wait_recv()

    _, land = pl.pallas_call(
        body, name=name, out_shape=(pltpu.HBM(payload.shape, payload.dtype), pltpu.HBM(land.shape, land.dtype)),
        in_specs=[_HBM, _HBM, _SEM, _SEM, _ANY], out_specs=(_HBM, _HBM), input_output_aliases={0: 0, 1: 1},
        compiler_params=pltpu.CompilerParams(has_side_effects=_DATAFLOW),
    )(payload, land, send_sems, recv_sems, after)
    me = 4 * lax.axis_index("x") + 2 * lax.axis_index("y") + lax.axis_index("c")
    return lax.dynamic_update_slice(land, payload[None], (me, 0, 0))


def _scatter_start(name, blocks):
    rows = blocks.shape[1]

    def body(blocks_ref, land_ref, send_sems, recv_sems, blocks_out, land_out, token):
        me, peers = _peers()
        for k, (pos, idx) in enumerate(peers):
            pltpu.make_async_remote_copy(
                src_ref=blocks_ref.at[idx], dst_ref=land_ref.at[me], send_sem=send_sems.at[k],
                recv_sem=recv_sems.at[k], device_id=pos, device_id_type=_MESH).start()
        token[...] = jnp.zeros_like(token)

    land = _landing(rows)
    send_sems, recv_sems, blocks_thru, land_thru, token = pl.pallas_call(
        body, name=name,
        out_shape=(*_sem_pair(), pltpu.HBM(blocks.shape, blocks.dtype), pltpu.HBM(land.shape, land.dtype), _TOKEN),
        in_specs=[_HBM, _HBM], out_specs=(_SEM, _SEM, _HBM, _HBM, pl.BlockSpec(memory_space=pltpu.VMEM)),
        input_output_aliases={0: 2, 1: 3},
        compiler_params=pltpu.CompilerParams(has_side_effects=_DATAFLOW),
    )(_in_hbm(blocks), land)
    return (blocks_thru, land_thru, send_sems, recv_sems), token


def _scatter_wait(groups, after):
    n = len(groups)

    def body(*refs):
        blocks, land = refs[:n], refs[n:2 * n]
        sems = refs[2 * n:4 * n]
        _, peers = _peers()
        for g in range(n):
            for k, (pos, idx) in enumerate(peers):
                cp = pltpu.make_async_remote_copy(
                    src_ref=blocks[g].at[idx], dst_ref=land[g].at[idx], send_sem=sems[2 * g].at[k],
                    recv_sem=sems[2 * g + 1].at[k], device_id=pos, device_id_type=_MESH)
                cp.wait_send()
                cp.wait_recv()

    hbm = [grp[0] for grp in groups] + [grp[1] for grp in groups]
    sems = [s for grp in groups for s in grp[2:]]
    out = pl.pallas_call(
        body, name="rs_wait", out_shape=tuple(pltpu.HBM(a.shape, a.dtype) for a in hbm),
        in_specs=[_HBM] * (2 * n) + [_SEM] * (2 * n) + [_ANY], out_specs=tuple([_HBM] * (2 * n)),
        input_output_aliases={i: i for i in range(2 * n)},
        compiler_params=pltpu.CompilerParams(has_side_effects=_DATAFLOW),
    )(*hbm, *sems, after)
    me = 4 * lax.axis_index("x") + 2 * lax.axis_index("y") + lax.axis_index("c")
    lands = []
    for g in range(n):
        own = lax.dynamic_index_in_dim(out[g], me, axis=0, keepdims=True)
        lands.append(lax.dynamic_update_slice(out[n + g], own, (me, 0, 0)))
    return lands


def _pad_rows(flat, rows):
    return jnp.pad(flat, (0, rows * D_MODEL - flat.shape[0])).reshape(rows, D_MODEL)


SMALL_NAMES = ("conv_b_pw1", "conv_w_dw", "conv_b_dw", "conv_ln_g", "conv_ln_b", "conv_b_pw2")


def _pack_small(p):
    flat = jnp.concatenate([p[n].reshape(-1) for n in SMALL_NAMES])
    return _pad_rows(flat, ROWS_SMALL).reshape(1, ROWS_SMALL, D_MODEL)


def _unpack_small(packed):
    flat = packed.reshape(-1)
    c = D_MODEL // N_DEV
    shapes = ((1, 2 * c), (1, CONV_WIDTH, c), (1, c), (1, c), (1, c), (1, c))
    out, o = {}, 0
    for n, shape in zip(SMALL_NAMES, shapes):
        size = shape[-1] * (shape[1] if len(shape) == 3 else 1)
        out[n] = flat[o:o + size].reshape(shape)
        o += size
    return out


def _gather_payloads(p):
    t = lambda a: jnp.swapaxes(a, -1, -2).astype(BF16)
    w1t, w3t, w2 = t(p["ffn_w1"]), t(p["ffn_w3"]), p["ffn_w2"].astype(BF16)
    small = _pack_small(p).reshape(-1)[:ROWS_SMALL * D_MODEL // 2]
    small = lax.bitcast_convert_type(small, BF16).reshape(ROWS_SMALL, D_MODEL)
    conv = jnp.concatenate([t(p["conv_w_pw1"][0]), p["conv_w_pw2"][0].astype(BF16), small], axis=0)
    ffn = [jnp.concatenate([w1t[l], w3t[l], w2[l]], axis=0) for l in range(2)]
    return [t(p["attn_w_qkv"][0]), p["attn_w_o"][0].astype(BF16), ffn[0], conv, ffn[1]]


def _device_rows(land, lo, n):
    return land[:, lo:lo + n].reshape(N_DEV * n, D_MODEL)


def _unpack_conv(land):
    small = lax.bitcast_convert_type(
        land[:, ROWS_PW1 + ROWS_PW2:].reshape(N_DEV, ROWS_SMALL * D_MODEL // 2, 2), F32)
    c = D_MODEL // N_DEV
    b_pw1 = small[:, :2 * c].reshape(1, 2 * D_MODEL)
    s = 2 * c
    w_dw = small[:, s:s + CONV_WIDTH * c].reshape(N_DEV, CONV_WIDTH, c).transpose(1, 0, 2).reshape(CONV_WIDTH, D_MODEL)
    s += CONV_WIDTH * c
    b_dw, ln_g, ln_b, b_pw2 = (small[:, s + i * c:s + (i + 1) * c].reshape(1, D_MODEL) for i in range(4))
    return dict(w_pw1_t=_device_rows(land, 0, ROWS_PW1), w_pw2=_device_rows(land, ROWS_PW1, ROWS_PW2),
                b_pw1=b_pw1, w_dw=w_dw, b_dw=b_dw, ln_g=ln_g, ln_b=ln_b, b_pw2=b_pw2)


def _dest_blocks(mats):
    return jnp.concatenate([a.reshape(N_DEV, -1, D_MODEL) for a in mats], axis=1)


def _small_grad_rows(g_bpw1, g_dw, g_bdw, g_lng, g_lnb, g_bpw2):
    c = D_MODEL // N_DEV
    small = jnp.concatenate(
        [g_bpw1.reshape(N_DEV, 2 * c), g_dw.reshape(CONV_WIDTH, N_DEV, c).transpose(1, 0, 2).reshape(N_DEV, -1),
         g_bdw.reshape(N_DEV, c), g_lng.reshape(N_DEV, c), g_lnb.reshape(N_DEV, c), g_bpw2.reshape(N_DEV, c)], axis=1)
    small = jnp.pad(small, ((0, 0), (0, ROWS_SMALL * D_MODEL - SMALL_USED)))
    return small.reshape(N_DEV * ROWS_SMALL, D_MODEL).astype(BF16)


LOSS_ROW = 9


def _pack_replicated(norm_mix, norm_ffn, b_qkv, sinks, b_o, norm_final, extra=None):
    rows = [norm_mix.reshape(2, D_MODEL), norm_ffn.reshape(2, D_MODEL), _pad_rows(b_qkv.reshape(-1), 2),
            _pad_rows(sinks.reshape(-1), 1), b_o.reshape(1, D_MODEL), norm_final.reshape(1, D_MODEL)]
    if extra is not None:
        rows.append(_pad_rows(extra.reshape(-1), 1))
    p = jnp.concatenate(rows, axis=0)
    return jnp.pad(p, ((0, REPL_ROWS - p.shape[0]), (0, 0)))


def _unpack_replicated(p):
    return dict(norm_mix=p[0:2], norm_ffn=p[2:4], attn_b_qkv=p[4:6].reshape(-1)[:QKV_DIM].reshape(1, QKV_DIM),
                attn_sinks=p[6, :N_Q_HEADS].reshape(1, N_Q_HEADS), attn_b_o=p[7:8], norm_final=p[8])


WEIGHT_ORDER = ['norm_mix', 'norm_ffn', 'attn_w_qkv', 'attn_b_qkv', 'attn_sinks', 'attn_w_o', 'attn_b_o',
                'conv_w_pw1', 'conv_b_pw1', 'conv_w_dw', 'conv_b_dw', 'conv_ln_g', 'conv_ln_b', 'conv_w_pw2',
                'conv_b_pw2', 'ffn_w1', 'ffn_w3', 'ffn_w2', 'norm_final']


def kernel(x, norm_mix, norm_ffn, attn_w_qkv, attn_b_qkv, attn_sinks, attn_w_o, attn_b_o, conv_w_pw1, conv_b_pw1, conv_w_dw, conv_b_dw, conv_ln_g, conv_ln_b, conv_w_pw2, conv_b_pw2, ffn_w1, ffn_w3, ffn_w2, norm_final, loss_target, m_norm_mix, m_norm_ffn, m_attn_w_qkv, m_attn_b_qkv, m_attn_sinks, m_attn_w_o, m_attn_b_o, m_conv_w_pw1, m_conv_b_pw1, m_conv_w_dw, m_conv_b_dw, m_conv_ln_g, m_conv_ln_b, m_conv_w_pw2, m_conv_b_pw2, m_ffn_w1, m_ffn_w3, m_ffn_w2, m_norm_final, v_norm_mix, v_norm_ffn, v_attn_w_qkv, v_attn_b_qkv, v_attn_sinks, v_attn_w_o, v_attn_b_o, v_conv_w_pw1, v_conv_b_pw1, v_conv_w_dw, v_conv_b_dw, v_conv_ln_g, v_conv_ln_b, v_conv_w_pw2, v_conv_b_pw2, v_ffn_w1, v_ffn_w3, v_ffn_w2, v_norm_final):
    xs = x[0]
    target = loss_target[0]
    seq = xs.shape[0]
    my_x, my_y, my_c = lax.axis_index("x"), lax.axis_index("y"), lax.axis_index("c")

    w = dict(attn_w_qkv=attn_w_qkv, attn_w_o=attn_w_o, conv_w_pw1=conv_w_pw1, conv_b_pw1=conv_b_pw1,
             conv_w_dw=conv_w_dw, conv_b_dw=conv_b_dw, conv_ln_g=conv_ln_g, conv_ln_b=conv_ln_b,
             conv_w_pw2=conv_w_pw2, conv_b_pw2=conv_b_pw2, ffn_w1=ffn_w1, ffn_w3=ffn_w3, ffn_w2=ffn_w2)
    m = dict(attn_w_qkv=m_attn_w_qkv, attn_w_o=m_attn_w_o, conv_w_pw1=m_conv_w_pw1, conv_b_pw1=m_conv_b_pw1,
             conv_w_dw=m_conv_w_dw, conv_b_dw=m_conv_b_dw, conv_ln_g=m_conv_ln_g, conv_ln_b=m_conv_ln_b,
             conv_w_pw2=m_conv_w_pw2, conv_b_pw2=m_conv_b_pw2, ffn_w1=m_ffn_w1, ffn_w3=m_ffn_w3, ffn_w2=m_ffn_w2)
    v = dict(attn_w_qkv=v_attn_w_qkv, attn_w_o=v_attn_w_o, conv_w_pw1=v_conv_w_pw1, conv_b_pw1=v_conv_b_pw1,
             conv_w_dw=v_conv_w_dw, conv_b_dw=v_conv_b_dw, conv_ln_g=v_conv_ln_g, conv_ln_b=v_conv_ln_b,
             conv_w_pw2=v_conv_w_pw2, conv_b_pw2=v_conv_b_pw2, ffn_w1=v_ffn_w1, ffn_w3=v_ffn_w3, ffn_w2=v_ffn_w2)

    payloads = _gather_payloads(w)
    (ag_qkv, ag_wo), tok = _gather_start("ag_start_attn", payloads[:2])
    (ag_ffn0, ag_conv, ag_ffn1), tok = _gather_start(
        "ag_start_rest", [payloads[2] + tok[0, 0].astype(BF16), payloads[3], payloads[4]])
    rc, rsa, rsb = _rope_tables(seq)
    sinks = attn_sinks.reshape(N_Q_HEADS)
    g_mix0, g_mix1 = norm_mix[0:1] + tok[0, 0], norm_mix[1:2]
    g_ffn0, g_ffn1 = norm_ffn[0:1], norm_ffn[1:2]
    g_fin = norm_final.reshape(1, D_MODEL)

    w_qkv_t = _gather_wait("ag_wait_qkv", ag_qkv, tok).reshape(QKV_DIM, D_MODEL)
    y0, q, k, vv = _qkv_fwd(xs, g_mix0, w_qkv_t, attn_b_qkv, rc, rsa, rsb)
    attn, probs, p_sinks = _attn_fwd(sinks, q, k, vv)
    w_o = _gather_wait("ag_wait_wo", ag_wo, attn).reshape(Q_DIM, D_MODEL)
    h1 = _mm_res("attn_out_proj", attn, w_o, attn_b_o, xs)
    w_ffn0 = _gather_wait("ag_wait_ffn0", ag_ffn0, h1)
    f0, u0, p0, s0 = _ffn_up("ffn0_up", h1, g_ffn0, w_ffn0)
    h2 = _ffn_down("ffn0_down", s0, w_ffn0, h1)
    wt = _unpack_conv(_gather_wait("ag_wait_conv", ag_conv, h2))
    wd = jnp.concatenate([wt["w_dw"][::-1], jnp.zeros((TAPS_PAD - CONV_WIDTH, D_MODEL), F32)], axis=0)
    y1, a, dwc, z = _conv_fwd(h2, g_mix1, wt["w_pw1_t"], wt["b_pw1"], wd, wt["b_dw"], wt["ln_g"], wt["ln_b"])
    h3 = _mm_res("conv_out_proj", z, wt["w_pw2"], wt["b_pw2"], h2)
    w_ffn1 = _gather_wait("ag_wait_ffn1", ag_ffn1, h3)
    f1, u1, p1, s1 = _ffn_up("ffn1_up", h3, g_ffn1, w_ffn1)
    h4 = _ffn_down("ffn1_down", s1, w_ffn1, h3)
    dh4, sq, dg_fin = _final_loss(h4, target, g_fin)

    du1, dp1 = _ffn_bwd_act("ffn1_bwd_act", dh4, u1, p1, w_ffn1)
    dh3, dg_ffn1 = _ffn_bwd_in("ffn1_bwd_in", du1, dp1, w_ffn1, h3, dh4, g_ffn1)
    gw2_1 = _mm_tn("ffn1_dw2", s1, dh4, tk=FF_TILE)
    gw1t_1 = _mm_tn("ffn1_dw1", du1, f1, tk=FF_TILE)
    gw3t_1 = _mm_tn("ffn1_dw3", dp1, f1, tk=FF_TILE)
    rs_ffn1, tok = _scatter_start("rs_start_ffn1", _dest_blocks([gw1t_1, gw3t_1, gw2_1]))

    da, dlg, dlb, dbdw, dwd, dbpw1, dbpw2 = _conv_bwd(dh3, wt["w_pw2"], dwc, a, wt["ln_g"] + tok[0, 0],
                                                     wt["ln_b"], wd)
    gpw2 = _mm_tn("conv_dw_pw2", z, dh3, tk=D_MODEL)
    dh2, dg_mix1 = _mm_rms_bwd("conv_in_bwd", da, wt["w_pw1_t"], h2, dh3, g_mix1)
    gpw1t = _mm_tn("conv_dw_pw1", da, y1, tk=D_MODEL)
    small_rows = _small_grad_rows(dbpw1, dwd[:CONV_WIDTH][::-1], dbdw, dlg, dlb, dbpw2)
    rs_conv, tok = _scatter_start("rs_start_conv", _dest_blocks([gpw1t, gpw2, small_rows]))

    du0, dp0 = _ffn_bwd_act("ffn0_bwd_act", dh2, u0, p0, w_ffn0)
    dh1, dg_ffn0 = _ffn_bwd_in("ffn0_bwd_in", du0, dp0, w_ffn0, h1, dh2, g_ffn0 + tok[0, 0])
    gw2_0 = _mm_tn("ffn0_dw2", s0, dh2, tk=FF_TILE)
    gw1t_0 = _mm_tn("ffn0_dw1", du0, f0, tk=FF_TILE)
    gw3t_0 = _mm_tn("ffn0_dw3", dp0, f0, tk=FF_TILE)
    rs_ffn0, tok = _scatter_start("rs_start_ffn0", _dest_blocks([gw1t_0, gw3t_0, gw2_0]))

    gwo = _mm_tn("attn_dw_o", attn, dh1, tk=D_MODEL)
    rs_wo, tok2 = _scatter_start("rs_start_wo", _dest_blocks([gwo]))
    dattn, dbo = _nt_bias("attn_out_bwd", dh1, w_o)
    dq, dkc, dkp, dvc, dvp, dsink = _attn_bwd(probs, p_sinks + (tok[0, 0] + tok2[0, 0]), q, k, vv, dattn)
    dqkv, dbqkv = _rope_bwd(dq, dkc, dkp, dvc, dvp, rc, rsa, rsb)
    gqkvt = _mm_tn("attn_dw_qkv", dqkv, y0, tk=QKV_DIM)
    rs_qkv, tok = _scatter_start("rs_start_qkv", _dest_blocks([gqkvt]))
    dx, dg_mix0 = _mm_rms_bwd("qkv_in_bwd", dqkv, w_qkv_t, xs, dh1, g_mix0 + tok[0, 0])

    p_ffn1, p_conv, p_ffn0, p_wo, p_qkv = _scatter_wait([rs_ffn1, rs_conv, rs_ffn0, rs_wo, rs_qkv], dx)
    g_ffn = [_sum_slots("rs_sum_ffn0", p_ffn0), _sum_slots("rs_sum_ffn1", p_ffn1)]
    g_conv = _sum_slots("rs_sum_conv", p_conv)
    g_wo = _sum_slots("rs_sum_wo", p_wo)
    g_qkv = _sum_slots("rs_sum_qkv", p_qkv)

    def ff(slot, transpose):
        blk = jnp.stack([g[slot * ROWS_FF:(slot + 1) * ROWS_FF] for g in g_ffn])
        return jnp.swapaxes(blk, 1, 2) if transpose else blk

    grads = dict(
        ffn_w1=ff(W1T_SLOT, True), ffn_w3=ff(W3T_SLOT, True), ffn_w2=ff(W2_SLOT, False),
        attn_w_qkv=g_qkv.T[None], attn_w_o=g_wo[None],
        conv_w_pw1=g_conv[:ROWS_PW1].T[None], conv_w_pw2=g_conv[ROWS_PW1:ROWS_PW1 + ROWS_PW2][None])
    sharded = [dict(grads), {}, {}, {}]
    for n in grads:
        for dst, t in zip(sharded[1:], _adamw_native("adamw_" + n, grads[n], w[n], m[n], v[n])):
            dst[n] = t
    g_small = g_conv[ROWS_PW1 + ROWS_PW2:][None]
    small_out = _adamw_native("adamw_small", g_small, _pack_small(w), _pack_small(m), _pack_small(v))
    for dst, t in zip(sharded, (g_small,) + tuple(small_out)):
        dst.update(_unpack_small(t))

    part = _pack_replicated(jnp.concatenate([dg_mix0, dg_mix1]), jnp.concatenate([dg_ffn0, dg_ffn1]),
                            dbqkv, -dsink[:, 0], dbo, dg_fin, extra=sq[0, 0:1])
    parts = _all_gather("ag_replicated_grads", part)
    w_rep = _pack_replicated(norm_mix, norm_ffn, attn_b_qkv, attn_sinks, attn_b_o, norm_final)
    m_rep = _pack_replicated(m_norm_mix, m_norm_ffn, m_attn_b_qkv, m_attn_sinks, m_attn_b_o, m_norm_final)
    v_rep = _pack_replicated(v_norm_mix, v_norm_ffn, v_attn_b_qkv, v_attn_sinks, v_attn_b_o, v_norm_final)
    rep_out = _adamw_replicated(parts, w_rep, m_rep, v_rep)
    replicated = [_unpack_replicated(t) for t in rep_out]
    loss = rep_out[0][LOSS_ROW, 0] * (0.5 / D_MODEL)

    outs = [loss, dx.reshape(1, seq, D_MODEL)]
    for sh, rp in zip(sharded, replicated):
        merged = {**sh, **rp}
        outs += [merged[n] for n in WEIGHT_ORDER]
    return tuple(outs)
```

```python
import jax
import jax.numpy as jnp
from jax import lax
from jax.experimental import pallas as pl
from jax.experimental.pallas import tpu as pltpu

F32 = jnp.float32
BF16 = jnp.bfloat16

D_MODEL = 1024
HEAD_DIM = 64
N_Q_HEADS = 16
N_KV_HEADS = 2
Q_PER_KV = 8
Q_DIM = N_Q_HEADS * HEAD_DIM
KV_DIM = N_KV_HEADS * HEAD_DIM
QKV_DIM = Q_DIM + 2 * KV_DIM
BLOCK = 128
CONV_WIDTH = 31
D_FF = 2816
ROPE_THETA = 10000.0
RMS_EPS = 1e-5
LN_EPS = 1e-5
ADAM_LR = 0.001
ADAM_B1 = 0.9
ADAM_B2 = 0.999
ADAM_EPS = 1e-08
ADAM_WD = 0.01
ADAM_STEP = 10
N_DEV = 8

LANES = 128
SUBLANES = 8
TOKEN_TILE = 512
CONV_CHUNK = 64
CONV_HALO = 32
TAPS_PAD = 32
VMEM_LIMIT = 56 * 1024 * 1024
NEG_INF = float(jnp.finfo(jnp.float32).min)

ROWS_FF = D_FF // N_DEV
W1T_SLOT, W3T_SLOT, W2_SLOT = 0, 1, 2
ROWS_QKV = QKV_DIM // N_DEV
ROWS_WO = Q_DIM // N_DEV
ROWS_PW1 = 2 * D_MODEL // N_DEV
ROWS_PW2 = D_MODEL // N_DEV
ROWS_SMALL = 16
SMALL_USED = 2 * D_MODEL // N_DEV + CONV_WIDTH * (D_MODEL // N_DEV) + 4 * (D_MODEL // N_DEV)
FF_SPLIT = 2
FF_TILE_DEVS = N_DEV // FF_SPLIT
FF_TILE = FF_TILE_DEVS * ROWS_FF
REPL_ROWS = 16


def _call(body, *, name, grid, in_specs, out_specs, out_shape, scratch=(), sem=None):
    return pl.pallas_call(
        body, name=name, grid=grid, in_specs=in_specs, out_specs=out_specs, out_shape=out_shape,
        scratch_shapes=list(scratch),
        compiler_params=pltpu.CompilerParams(dimension_semantics=sem, vmem_limit_bytes=VMEM_LIMIT))


def _full(shape):
    return pl.BlockSpec(shape, lambda *_: (0,) * len(shape))


def _resident(shape):
    return pl.BlockSpec(shape, lambda *_: (0,) * len(shape), pipeline_mode=pl.Buffered(1))


def _rows(tm, n):
    return pl.BlockSpec((tm, n), lambda i, *_: (i, 0))


def _sig(x):
    return 1.0 / (1.0 + jnp.exp(-x))


def _sum_rows(x):
    return jnp.sum(x, axis=0, keepdims=True)


def _nt(a, b):
    return lax.dot_general(a, b, (((1,), (1,)), ((), ())), preferred_element_type=F32)


def _tn(a, b):
    return lax.dot_general(a, b, (((0,), (0,)), ((), ())), preferred_element_type=F32)


def _rms_stats(x):
    return lax.rsqrt(jnp.mean(x * x, axis=-1, keepdims=True) + RMS_EPS)


def _rms_bwd(dy, x, g, dres):
    r = _rms_stats(x)
    n = x * r
    dn = dy * g
    dx = dres + r * (dn - n * jnp.mean(dn * n, axis=-1, keepdims=True))
    return dx, _sum_rows(dy * n)


def _rope_tables(seq):
    pos = jnp.arange(seq, dtype=F32)
    inv_freq = ROPE_THETA ** (-jnp.arange(0, HEAD_DIM, 2, dtype=F32) / HEAD_DIM)
    ang = pos[:, None] * inv_freq[None, :]
    cos, sin = jnp.cos(ang), jnp.sin(ang)
    zero = jnp.zeros_like(sin)
    reps = LANES // HEAD_DIM
    c = jnp.tile(jnp.concatenate([cos, cos], axis=1), (1, reps))
    sa = jnp.tile(jnp.concatenate([-sin, zero], axis=1), (1, reps))
    sb = jnp.tile(jnp.concatenate([zero, sin], axis=1), (1, reps))
    return c, sa, sb


def _rope(t, c, sa, sb):
    half = HEAD_DIM // 2
    return t * c + pltpu.roll(t, LANES - half, 1) * sa + pltpu.roll(t, half, 1) * sb


def _rope_t(dt, c, sa, sb):
    half = HEAD_DIM // 2
    return dt * c + pltpu.roll(dt * sa, half, 1) + pltpu.roll(dt * sb, LANES - half, 1)


def _qkv_fwd(x, g, w, b, rc, rsa, rsb):
    seq = x.shape[0]
    tm = min(TOKEN_TILE, seq)

    def body(x_ref, g_ref, w_ref, b_ref, c_ref, sa_ref, sb_ref, y_ref, q_ref, k_ref, v_ref):
        xv = x_ref[...]
        y = (xv * _rms_stats(xv) * g_ref[...]).astype(BF16)
        y_ref[...] = y
        qkv = _nt(y, w_ref[...]) + b_ref[...]
        c, sa, sb = c_ref[...], sa_ref[...], sb_ref[...]
        for i in range(Q_DIM // LANES):
            blk = _rope(qkv[:, i * LANES:(i + 1) * LANES], c, sa, sb)
            q_ref[:, i * LANES:(i + 1) * LANES] = (blk * (HEAD_DIM ** -0.5)).astype(BF16)
        k_ref[...] = _rope(qkv[:, Q_DIM:Q_DIM + KV_DIM], c, sa, sb).astype(BF16)
        v_ref[...] = qkv[:, Q_DIM + KV_DIM:].astype(BF16)

    return _call(
        body, name="qkv_fwd", grid=(seq // tm,),
        in_specs=[_rows(tm, D_MODEL), _full((1, D_MODEL)), _full((QKV_DIM, D_MODEL)), _full((1, QKV_DIM)),
                  _rows(tm, LANES), _rows(tm, LANES), _rows(tm, LANES)],
        out_specs=[_rows(tm, D_MODEL), _rows(tm, Q_DIM), _rows(tm, KV_DIM), _rows(tm, KV_DIM)],
        out_shape=[jax.ShapeDtypeStruct((seq, D_MODEL), BF16), jax.ShapeDtypeStruct((seq, Q_DIM), BF16),
                   jax.ShapeDtypeStruct((seq, KV_DIM), BF16), jax.ShapeDtypeStruct((seq, KV_DIM), BF16)],
        sem=("parallel",))(x, g, w, b, rc, rsa, rsb)


GROUP_ROWS = Q_PER_KV * BLOCK


def _band_mask(n, rows=GROUP_ROWS):
    row = lax.broadcasted_iota(jnp.int32, (rows, 2 * BLOCK), 0) & (BLOCK - 1)
    col = lax.broadcasted_iota(jnp.int32, (rows, 2 * BLOCK), 1)
    rel = row + BLOCK - col
    return (rel >= 0) & (rel < BLOCK) & ((col >= BLOCK) | (n > 0))


def _softmax_with_sink(s, mask, sink):
    s = jnp.where(mask, s, NEG_INF)
    m = jnp.maximum(jnp.max(s, axis=-1, keepdims=True), sink)
    p = jnp.exp(s - m)
    e_sink = jnp.exp(sink - m)
    inv = 1.0 / (jnp.sum(p, axis=-1, keepdims=True) + e_sink)
    return p * inv, e_sink * inv


PAIRS_PER_KV = Q_PER_KV // 2


def _kv_specs():
    cur = pl.BlockSpec((BLOCK, KV_DIM), lambda n: (n, 0))
    prev = pl.BlockSpec((BLOCK, KV_DIM), lambda n: (jnp.maximum(n - 1, 0), 0))
    return cur, prev


def _low_lanes():
    return lax.broadcasted_iota(jnp.int32, (2 * BLOCK, KV_DIM), 1) < HEAD_DIM


def _kv_low_high(prev_ref, cur_ref, j, low):
    both = jnp.concatenate([prev_ref[...], cur_ref[...]], axis=0).astype(F32)
    swapped = pltpu.roll(both, HEAD_DIM, 1)
    at_low, at_high = (both, swapped) if j == 0 else (swapped, both)
    return jnp.where(low, at_low, 0.0).astype(BF16), jnp.where(low, 0.0, at_high).astype(BF16)


def _fold_pair_halves(acc, j, low):
    folded = acc + pltpu.roll(acc, HEAD_DIM, 1)
    return jnp.where(low, folded, 0.0) if j == 0 else jnp.where(low, 0.0, folded)


def _pair_lanes(j, i):
    g = j * PAIRS_PER_KV + i
    return slice(g * LANES, (g + 1) * LANES), 2 * g


def _attn_fwd(sinks, q, k, v):
    seq = q.shape[0]
    cur, prev = _kv_specs()

    def body(sink_ref, q_ref, kc_ref, kp_ref, vc_ref, vp_ref, o_ref, p_ref, ps_ref):
        mask = _band_mask(pl.program_id(0), BLOCK)
        lane = lax.broadcasted_iota(jnp.int32, (BLOCK, LANES), 1)
        p_sinks = jnp.zeros((BLOCK, LANES), F32)
        for j in range(N_KV_HEADS):
            cs = slice(j * HEAD_DIM, (j + 1) * HEAD_DIM)
            kk = jnp.concatenate([kp_ref[:, cs], kc_ref[:, cs]], axis=0)
            vv = jnp.concatenate([vp_ref[:, cs], vc_ref[:, cs]], axis=0)
            for gq in range(Q_PER_KV):
                h = j * Q_PER_KV + gq
                hs = slice(h * HEAD_DIM, (h + 1) * HEAD_DIM)
                probs, p_sink = _softmax_with_sink(_nt(q_ref[:, hs], kk), mask, sink_ref[h])
                pb = probs.astype(BF16)
                p_ref[h] = pb
                p_sinks = jnp.where(lane == h, p_sink, p_sinks)
                o_ref[:, hs] = jnp.dot(pb, vv, preferred_element_type=F32).astype(BF16)
        ps_ref[...] = p_sinks

    return _call(
        body, name="attn_fwd", grid=(seq // BLOCK,),
        in_specs=[pl.BlockSpec(memory_space=pltpu.SMEM), _rows(BLOCK, Q_DIM), cur, prev, cur, prev],
        out_specs=[_rows(BLOCK, Q_DIM), pl.BlockSpec((N_Q_HEADS, BLOCK, 2 * BLOCK), lambda n: (0, n, 0)),
                   _rows(BLOCK, LANES)],
        out_shape=[jax.ShapeDtypeStruct((seq, Q_DIM), BF16),
                   jax.ShapeDtypeStruct((N_Q_HEADS, seq, 2 * BLOCK), BF16),
                   jax.ShapeDtypeStruct((seq, LANES), F32)],
        sem=("parallel",))(sinks, q, k, k, v, v)


def _mm_res(name, a, w, b, res):
    seq, kdim = a.shape
    n = w.shape[1]
    tm = min(2 * TOKEN_TILE, seq)

    def body(a_ref, w_ref, b_ref, r_ref, o_ref):
        o_ref[...] = r_ref[...] + (jnp.dot(a_ref[...], w_ref[...], preferred_element_type=F32) + b_ref[...])

    return _call(
        body, name=name, grid=(seq // tm,),
        in_specs=[_rows(tm, kdim), _resident((kdim, n)), _full((1, n)), _rows(tm, n)],
        out_specs=_rows(tm, n), out_shape=jax.ShapeDtypeStruct((seq, n), F32),
        sem=("parallel",))(a, w, b, res)


def _ff_tile_spec(slot):
    return pl.BlockSpec((FF_TILE_DEVS, ROWS_FF, D_MODEL), lambda i, j: (j, slot, 0))


def _ff_whole_spec(slot):
    return pl.BlockSpec((N_DEV, ROWS_FF, D_MODEL), lambda i: (0, slot, 0), pipeline_mode=pl.Buffered(1))


def _ffn_up(name, h, g, gathered):
    seq = h.shape[0]
    tm = min(TOKEN_TILE, seq)

    def body(h_ref, g_ref, w1_ref, w3_ref, f_ref, u_ref, w_ref, s_ref):
        @pl.when(pl.program_id(1) == 0)
        def _():
            hv = h_ref[...]
            f_ref[...] = (hv * _rms_stats(hv) * g_ref[...]).astype(BF16)

        f = f_ref[...]
        u = _nt(f, w1_ref[...].reshape(FF_TILE, D_MODEL))
        w = _nt(f, w3_ref[...].reshape(FF_TILE, D_MODEL))
        u_ref[...] = u.astype(BF16)
        w_ref[...] = w.astype(BF16)
        s_ref[...] = (u * _sig(u) * w).astype(BF16)

    tile_ff = pl.BlockSpec((tm, FF_TILE), lambda i, j: (i, j))
    ff_shape = jax.ShapeDtypeStruct((seq, D_FF), BF16)
    return _call(
        body, name=name, grid=(seq // tm, FF_SPLIT),
        in_specs=[_rows(tm, D_MODEL), _full((1, D_MODEL)), _ff_tile_spec(W1T_SLOT), _ff_tile_spec(W3T_SLOT)],
        out_specs=[_rows(tm, D_MODEL), tile_ff, tile_ff, tile_ff],
        out_shape=[jax.ShapeDtypeStruct((seq, D_MODEL), BF16), ff_shape, ff_shape, ff_shape],
        sem=("parallel", "arbitrary"))(h, g, gathered, gathered)


def _ffn_down(name, s, gathered, res):
    seq = s.shape[0]
    tm = min(2 * TOKEN_TILE, seq)

    def body(s_ref, w_ref, r_ref, o_ref):
        w2 = w_ref[...].reshape(D_FF, D_MODEL)
        o_ref[...] = r_ref[...] + jnp.dot(s_ref[...], w2, preferred_element_type=F32)

    return _call(
        body, name=name, grid=(seq // tm,),
        in_specs=[_rows(tm, D_FF), _ff_whole_spec(W2_SLOT), _rows(tm, D_MODEL)],
        out_specs=_rows(tm, D_MODEL), out_shape=jax.ShapeDtypeStruct((seq, D_MODEL), F32),
        sem=("parallel",))(s, gathered, res)


def _conv_fwd(h, g, wpw1, bpw1, wd, bdw, lng, lnb):
    seq = h.shape[0]
    tm = min(TOKEN_TILE, seq)
    n_chunks = tm // CONV_CHUNK
    win = CONV_CHUNK + CONV_HALO

    def body(h_ref, g_ref, w_ref, b_ref, wd_ref, bdw_ref, lng_ref, lnb_ref,
             y_ref, a_ref, dwc_ref, z_ref, gbuf):
        i = pl.program_id(0)

        @pl.when(i == 0)
        def _():
            gbuf[0:CONV_HALO, :] = jnp.zeros((CONV_HALO, D_MODEL), F32)

        @pl.when(i > 0)
        def _():
            gbuf[0:CONV_HALO, :] = gbuf[tm:tm + CONV_HALO, :]

        hv = h_ref[...]
        y = (hv * _rms_stats(hv) * g_ref[...]).astype(BF16)
        y_ref[...] = y
        a = _nt(y, w_ref[...]) + b_ref[...]
        a_ref[...] = a.astype(BF16)
        gbuf[CONV_HALO:CONV_HALO + tm, :] = a[:, :D_MODEL] * _sig(a[:, D_MODEL:])

        def chunk(c, carry):
            r0 = pl.multiple_of(c * CONV_CHUNK, CONV_CHUNK)
            for l in range(D_MODEL // LANES):
                ls = slice(l * LANES, (l + 1) * LANES)
                gw = gbuf[pl.ds(r0, win), ls]
                acc = jnp.zeros((CONV_CHUNK, LANES), F32) + bdw_ref[:, ls]
                for s in range(SUBLANES):
                    gs = gw if s == 0 else pltpu.roll(gw, s, 0)
                    for q in range(CONV_HALO // SUBLANES):
                        d = SUBLANES * q + s
                        if d < CONV_WIDTH:
                            lo = CONV_HALO - SUBLANES * q
                            acc = acc + wd_ref[d:d + 1, ls] * gs[lo:lo + CONV_CHUNK]
                dwc_ref[pl.ds(r0, CONV_CHUNK), ls] = acc
            return carry

        lax.fori_loop(0, n_chunks, chunk, 0)

        xv = dwc_ref[...]
        mu = jnp.mean(xv, axis=-1, keepdims=True)
        xc = xv - mu
        var = jnp.mean(xc * xc, axis=-1, keepdims=True)
        ln = xc * lax.rsqrt(var + LN_EPS) * lng_ref[...] + lnb_ref[...]
        z_ref[...] = (ln * _sig(ln)).astype(BF16)

    return _call(
        body, name="conv_fwd", grid=(seq // tm,),
        in_specs=[_rows(tm, D_MODEL), _full((1, D_MODEL)), _full((2 * D_MODEL, D_MODEL)), _full((1, 2 * D_MODEL)),
                  _full((TAPS_PAD, D_MODEL)), _full((1, D_MODEL)), _full((1, D_MODEL)), _full((1, D_MODEL))],
        out_specs=[_rows(tm, D_MODEL), _rows(tm, 2 * D_MODEL), _rows(tm, D_MODEL), _rows(tm, D_MODEL)],
        out_shape=[jax.ShapeDtypeStruct((seq, D_MODEL), BF16), jax.ShapeDtypeStruct((seq, 2 * D_MODEL), BF16),
                   jax.ShapeDtypeStruct((seq, D_MODEL), F32), jax.ShapeDtypeStruct((seq, D_MODEL), BF16)],
        scratch=[pltpu.VMEM((tm + CONV_HALO, D_MODEL), F32)],
        sem=("arbitrary",))(h, g, wpw1, bpw1, wd, bdw, lng, lnb)


def _ffn_down_loss(s, gathered, res, target, g):
    seq = s.shape[0]
    tm = min(TOKEN_TILE, seq)

    def body(s_ref, w_ref, r_ref, t_ref, g_ref, dh_ref, dhb_ref, loss_ref, dg_ref):
        @pl.when(pl.program_id(0) == 0)
        def _():
            loss_ref[...] = jnp.zeros_like(loss_ref)
            dg_ref[...] = jnp.zeros_like(dg_ref)

        hv = r_ref[...] + jnp.dot(s_ref[...], w_ref[...].reshape(D_FF, D_MODEL), preferred_element_type=F32)
        gv = g_ref[...]
        err = hv * _rms_stats(hv) * gv - t_ref[...]
        sq = jnp.sum(jnp.sum(err * err, axis=-1, keepdims=True), axis=0, keepdims=True)
        loss_ref[...] += jnp.broadcast_to(sq, loss_ref.shape)
        dx, dg = _rms_bwd(err * (1.0 / D_MODEL), hv, gv, 0.0)
        dh_ref[...] = dx
        dhb_ref[...] = dx.astype(BF16)
        dg_ref[...] += dg

    return _call(
        body, name="ffn1_down_loss", grid=(seq // tm,),
        in_specs=[_rows(tm, D_FF), _ff_whole_spec(W2_SLOT), _rows(tm, D_MODEL), _rows(tm, D_MODEL),
                  _full((1, D_MODEL))],
        out_specs=[_rows(tm, D_MODEL), _rows(tm, D_MODEL), _full((SUBLANES, LANES)), _full((1, D_MODEL))],
        out_shape=[jax.ShapeDtypeStruct((seq, D_MODEL), F32), jax.ShapeDtypeStruct((seq, D_MODEL), BF16),
                   jax.ShapeDtypeStruct((SUBLANES, LANES), F32), jax.ShapeDtypeStruct((1, D_MODEL), F32)],
        sem=("arbitrary",))(s, gathered, res, target, g)


def _ffn_bwd_act(name, dh, u, w, gathered):
    seq = dh.shape[0]
    tm = min(TOKEN_TILE, seq)

    def body(dh_ref, u_ref, w_ref, w2_ref, du_ref, dw_ref):
        ds = _nt(dh_ref[...], w2_ref[...].reshape(FF_TILE, D_MODEL))
        uv = u_ref[...].astype(F32)
        sg = _sig(uv)
        dw_ref[...] = (ds * (uv * sg)).astype(BF16)
        du_ref[...] = (ds * w_ref[...].astype(F32) * (sg * (1.0 + uv * (1.0 - sg)))).astype(BF16)

    tile_ff = pl.BlockSpec((tm, FF_TILE), lambda i, j: (i, j))
    ff_shape = jax.ShapeDtypeStruct((seq, D_FF), BF16)
    return _call(
        body, name=name, grid=(seq // tm, FF_SPLIT),
        in_specs=[_rows(tm, D_MODEL), tile_ff, tile_ff, _ff_tile_spec(W2_SLOT)],
        out_specs=[tile_ff, tile_ff], out_shape=[ff_shape, ff_shape],
        sem=("parallel", "arbitrary"))(dh, u, w, gathered)


def _ffn_bwd_in(name, du, dw, gathered, h_in, dres, g):
    seq = du.shape[0]
    tm = min(TOKEN_TILE, seq)

    def body(du_ref, dw_ref, w1_ref, w3_ref, h_ref, dr_ref, g_ref, dx_ref, dxb_ref, dg_ref):
        @pl.when(pl.program_id(0) == 0)
        def _():
            dg_ref[...] = jnp.zeros_like(dg_ref)

        df = jnp.dot(du_ref[...], w1_ref[...].reshape(D_FF, D_MODEL), preferred_element_type=F32)
        df = df + jnp.dot(dw_ref[...], w3_ref[...].reshape(D_FF, D_MODEL), preferred_element_type=F32)
        dx, dg = _rms_bwd(df, h_ref[...], g_ref[...], dr_ref[...])
        dx_ref[...] = dx
        dxb_ref[...] = dx.astype(BF16)
        dg_ref[...] += dg

    return _call(
        body, name=name, grid=(seq // tm,),
        in_specs=[_rows(tm, D_FF), _rows(tm, D_FF), _ff_whole_spec(W1T_SLOT), _ff_whole_spec(W3T_SLOT),
                  _rows(tm, D_MODEL), _rows(tm, D_MODEL), _full((1, D_MODEL))],
        out_specs=[_rows(tm, D_MODEL), _rows(tm, D_MODEL), _full((1, D_MODEL))],
        out_shape=[jax.ShapeDtypeStruct((seq, D_MODEL), F32), jax.ShapeDtypeStruct((seq, D_MODEL), BF16),
                   jax.ShapeDtypeStruct((1, D_MODEL), F32)],
        sem=("arbitrary",))(du, dw, gathered, gathered, h_in, dres, g)


def _mm_tn(name, a, b, *, tk):
    seq, kdim = a.shape
    n = b.shape[1]
    tt = min((4 if b.dtype == BF16 else 2) * TOKEN_TILE, seq)
    n_t = seq // tt

    def body(a_ref, b_ref, o_ref, acc):
        t = pl.program_id(1)

        @pl.when(t == 0)
        def _():
            acc[...] = jnp.zeros_like(acc)

        acc[...] += _tn(a_ref[...].astype(BF16), b_ref[...].astype(BF16))

        @pl.when(t == n_t - 1)
        def _():
            o_ref[...] = acc[...].astype(BF16)

    return _call(
        body, name=name, grid=(kdim // tk, n_t),
        in_specs=[pl.BlockSpec((tt, tk), lambda k, t: (t, k)), pl.BlockSpec((tt, n), lambda k, t: (t, 0))],
        out_specs=pl.BlockSpec((tk, n), lambda k, t: (k, 0)),
        out_shape=jax.ShapeDtypeStruct((kdim, n), BF16),
        scratch=[pltpu.VMEM((tk, n), F32)],
        sem=("parallel", "arbitrary"))(a, b)


def _conv_bwd(dh, wpw2, dwc, a, lng, lnb, wd):
    seq = dh.shape[0]
    tm = min(TOKEN_TILE, seq)
    nt = seq // tm
    n_chunks = tm // CONV_CHUNK
    win = CONV_CHUNK + CONV_HALO
    halo_per_tile = tm // CONV_HALO

    def body(dh_ref, w_ref, dwc_ref, a_ref, ah_ref, lng_ref, lnb_ref, wd_ref,
             da_ref, dlg_ref, dlb_ref, dbdw_ref, dwd_ref, dbpw1_ref, dbpw2_ref,
             gbuf, dbuf, dglu, dwd_part):
        i = pl.program_id(0)
        r = nt - 1 - i

        @pl.when(i == 0)
        def _():
            dlg_ref[...] = jnp.zeros_like(dlg_ref)
            dlb_ref[...] = jnp.zeros_like(dlb_ref)
            dbdw_ref[...] = jnp.zeros_like(dbdw_ref)
            dbpw1_ref[...] = jnp.zeros_like(dbpw1_ref)
            dbpw2_ref[...] = jnp.zeros_like(dbpw2_ref)
            dwd_part[...] = jnp.zeros_like(dwd_part)
            dbuf[tm:tm + CONV_HALO, :] = jnp.zeros((CONV_HALO, D_MODEL), F32)

        @pl.when(i > 0)
        def _():
            dbuf[tm:tm + CONV_HALO, :] = dbuf[0:CONV_HALO, :]

        dhv = dh_ref[...]
        dbpw2_ref[...] += _sum_rows(dhv)
        dz = _nt(dhv.astype(BF16), w_ref[...])
        xv = dwc_ref[...]
        lg = lng_ref[...]
        mu = jnp.mean(xv, axis=-1, keepdims=True)
        xc = xv - mu
        rstd = lax.rsqrt(jnp.mean(xc * xc, axis=-1, keepdims=True) + LN_EPS)
        xhat = xc * rstd
        ln = xhat * lg + lnb_ref[...]
        sg = _sig(ln)
        dln = dz * (sg * (1.0 + ln * (1.0 - sg)))
        dlg_ref[...] += _sum_rows(dln * xhat)
        dlb_ref[...] += _sum_rows(dln)
        dxh = dln * lg
        ddw = rstd * (dxh - jnp.mean(dxh, axis=-1, keepdims=True)
                      - xhat * jnp.mean(dxh * xhat, axis=-1, keepdims=True))
        dbdw_ref[...] += _sum_rows(ddw)
        dbuf[0:tm, :] = ddw

        av = a_ref[...].astype(F32)
        a1 = av[:, :D_MODEL]
        s2 = _sig(av[:, D_MODEL:])
        gbuf[CONV_HALO:CONV_HALO + tm, :] = a1 * s2
        ah = ah_ref[...].astype(F32)
        gh = ah[:, :D_MODEL] * _sig(ah[:, D_MODEL:])
        gbuf[0:CONV_HALO, :] = jnp.where(r > 0, gh, 0.0)

        def chunk(c, carry):
            r0 = pl.multiple_of(c * CONV_CHUNK, CONV_CHUNK)
            for l in range(D_MODEL // LANES):
                ls = slice(l * LANES, (l + 1) * LANES)
                dw_ = dbuf[pl.ds(r0, win), ls]
                gw = gbuf[pl.ds(r0, win), ls]
                dc = dw_[0:CONV_CHUNK]
                acc = jnp.zeros((CONV_CHUNK, LANES), F32)
                for s in range(SUBLANES):
                    ds_ = dw_ if s == 0 else pltpu.roll(dw_, win - s, 0)
                    gs = gw if s == 0 else pltpu.roll(gw, s, 0)
                    for q in range(CONV_HALO // SUBLANES):
                        d = SUBLANES * q + s
                        if d < CONV_WIDTH:
                            acc = acc + wd_ref[d:d + 1, ls] * ds_[SUBLANES * q:SUBLANES * q + CONV_CHUNK]
                            lo = CONV_HALO - SUBLANES * q
                            prod = dc * gs[lo:lo + CONV_CHUNK]
                            dwd_part[d, :, ls] += jnp.sum(
                                prod.reshape(CONV_CHUNK // SUBLANES, SUBLANES, LANES), axis=0)
                dglu[pl.ds(r0, CONV_CHUNK), ls] = acc
            return carry

        lax.fori_loop(0, n_chunks, chunk, 0)

        dg_ = dglu[...]
        da1 = dg_ * s2
        da2 = dg_ * a1 * s2 * (1.0 - s2)
        da_ref[:, :D_MODEL] = da1.astype(BF16)
        da_ref[:, D_MODEL:] = da2.astype(BF16)
        dbpw1_ref[:, :D_MODEL] += _sum_rows(da1)
        dbpw1_ref[:, D_MODEL:] += _sum_rows(da2)

        @pl.when(i == nt - 1)
        def _():
            dwd_ref[...] = jnp.sum(dwd_part[...], axis=1)

    rev = lambda n: pl.BlockSpec((tm, n), lambda i: (nt - 1 - i, 0))
    halo = pl.BlockSpec((CONV_HALO, 2 * D_MODEL),
                        lambda i: (jnp.maximum((nt - 1 - i) * halo_per_tile - 1, 0), 0))
    vec = lambda n: _full((1, n))
    return _call(
        body, name="conv_bwd", grid=(nt,),
        in_specs=[rev(D_MODEL), _full((D_MODEL, D_MODEL)), rev(D_MODEL), rev(2 * D_MODEL), halo,
                  vec(D_MODEL), vec(D_MODEL), _full((TAPS_PAD, D_MODEL))],
        out_specs=[rev(2 * D_MODEL), vec(D_MODEL), vec(D_MODEL), vec(D_MODEL), _full((TAPS_PAD, D_MODEL)),
                   vec(2 * D_MODEL), vec(D_MODEL)],
        out_shape=[jax.ShapeDtypeStruct((seq, 2 * D_MODEL), BF16), jax.ShapeDtypeStruct((1, D_MODEL), F32),
                   jax.ShapeDtypeStruct((1, D_MODEL), F32), jax.ShapeDtypeStruct((1, D_MODEL), F32),
                   jax.ShapeDtypeStruct((TAPS_PAD, D_MODEL), F32), jax.ShapeDtypeStruct((1, 2 * D_MODEL), F32),
                   jax.ShapeDtypeStruct((1, D_MODEL), F32)],
        scratch=[pltpu.VMEM((tm + CONV_HALO, D_MODEL), F32), pltpu.VMEM((tm + CONV_HALO, D_MODEL), F32),
                 pltpu.VMEM((tm, D_MODEL), F32), pltpu.VMEM((TAPS_PAD, SUBLANES, D_MODEL), F32)],
        sem=("arbitrary",))(dh, wpw2, dwc, a, a, lng, lnb, wd)


def _mm_rms_bwd(name, dact, wt, h_in, dres, g):
    seq, n = dact.shape
    tm = min(TOKEN_TILE, seq)

    def body(da_ref, w_ref, h_ref, dr_ref, g_ref, dx_ref, dxb_ref, dg_ref):
        @pl.when(pl.program_id(0) == 0)
        def _():
            dg_ref[...] = jnp.zeros_like(dg_ref)

        dy = jnp.dot(da_ref[...], w_ref[...], preferred_element_type=F32)
        dx, dg = _rms_bwd(dy, h_ref[...], g_ref[...], dr_ref[...])
        dx_ref[...] = dx
        dxb_ref[...] = dx.astype(BF16)
        dg_ref[...] += dg

    return _call(
        body, name=name, grid=(seq // tm,),
        in_specs=[_rows(tm, n), _resident((n, D_MODEL)), _rows(tm, D_MODEL), _rows(tm, D_MODEL), _full((1, D_MODEL))],
        out_specs=[_rows(tm, D_MODEL), _rows(tm, D_MODEL), _full((1, D_MODEL))],
        out_shape=[jax.ShapeDtypeStruct((seq, D_MODEL), F32), jax.ShapeDtypeStruct((seq, D_MODEL), BF16),
                   jax.ShapeDtypeStruct((1, D_MODEL), F32)],
        sem=("arbitrary",))(dact, wt, h_in, dres, g)


def _nt_bias(name, dy, w):
    seq, n = dy.shape
    kdim = w.shape[0]
    tm = min(2 * TOKEN_TILE, seq)

    def body(dy_ref, w_ref, o_ref, db_ref):
        @pl.when(pl.program_id(0) == 0)
        def _():
            db_ref[...] = jnp.zeros_like(db_ref)

        dyv = dy_ref[...]
        db_ref[...] += _sum_rows(dyv)
        o_ref[...] = _nt(dyv.astype(BF16), w_ref[...]).astype(BF16)

    return _call(
        body, name=name, grid=(seq // tm,),
        in_specs=[_rows(tm, n), _resident((kdim, n))],
        out_specs=[_rows(tm, kdim), _full((1, n))],
        out_shape=[jax.ShapeDtypeStruct((seq, kdim), BF16), jax.ShapeDtypeStruct((1, n), F32)],
        sem=("arbitrary",))(dy, w)


def _attn_bwd(probs, p_sinks, q, k, v, do):
    seq = q.shape[0]
    cur, prev = _kv_specs()

    def body(p_ref, ps_ref, q_ref, kc_ref, kp_ref, vc_ref, vp_ref, do_ref,
             dq_ref, dkc_ref, dkp_ref, dvc_ref, dvp_ref, dsink_ref):
        n = pl.program_id(0)

        @pl.when(n == 0)
        def _():
            dsink_ref[...] = jnp.zeros_like(dsink_ref)

        low = _low_lanes()
        lane = lax.broadcasted_iota(jnp.int32, (BLOCK, LANES), 1)
        p_sinks_blk = ps_ref[...]
        dk_all = jnp.zeros((2 * BLOCK, KV_DIM), F32)
        dv_all = jnp.zeros((2 * BLOCK, KV_DIM), F32)
        for j in range(N_KV_HEADS):
            k_lo, k_hi = _kv_low_high(kp_ref, kc_ref, j, low)
            v_lo, v_hi = _kv_low_high(vp_ref, vc_ref, j, low)
            dk_acc = jnp.zeros((2 * BLOCK, KV_DIM), F32)
            dv_acc = jnp.zeros((2 * BLOCK, KV_DIM), F32)
            for i in range(PAIRS_PER_KV):
                ls, h = _pair_lanes(j, i)
                qp = q_ref[:, ls]
                dop = do_ref[:, ls]
                dsb, pb16 = [], []
                for t, v_sel in enumerate((v_lo, v_hi)):
                    pb = p_ref[h + t]
                    pf = pb.astype(F32)
                    p_sink = jnp.sum(jnp.where(lane == h + t, p_sinks_blk, 0.0), axis=-1, keepdims=True)
                    dp = _nt(dop, v_sel)
                    delta = jnp.sum(pf * dp, axis=-1, keepdims=True)
                    dsb.append((pf * (dp - delta)).astype(BF16))
                    pb16.append(pb)
                    dsink_ref[h + t:h + t + 1, :] += jnp.broadcast_to(_sum_rows(p_sink * delta), (1, LANES))
                dq = (jnp.dot(dsb[0], k_lo, preferred_element_type=F32)
                      + jnp.dot(dsb[1], k_hi, preferred_element_type=F32))
                dq_ref[:, ls] = dq * (HEAD_DIM ** -0.5)
                dk_acc = dk_acc + jnp.where(low, _tn(dsb[0], qp), _tn(dsb[1], qp))
                dv_acc = dv_acc + jnp.where(low, _tn(pb16[0], dop), _tn(pb16[1], dop))
            dk_all = dk_all + _fold_pair_halves(dk_acc, j, low)
            dv_all = dv_all + _fold_pair_halves(dv_acc, j, low)
        dkp_ref[...] = dk_all[:BLOCK]
        dkc_ref[...] = dk_all[BLOCK:]
        dvp_ref[...] = dv_all[:BLOCK]
        dvc_ref[...] = dv_all[BLOCK:]

    kv_out = _rows(BLOCK, KV_DIM)
    kv_shape = jax.ShapeDtypeStruct((seq, KV_DIM), F32)
    return _call(
        body, name="attn_bwd", grid=(seq // BLOCK,),
        in_specs=[pl.BlockSpec((N_Q_HEADS, BLOCK, 2 * BLOCK), lambda n: (0, n, 0)), _rows(BLOCK, LANES),
                  _rows(BLOCK, Q_DIM), cur, prev, cur, prev, _rows(BLOCK, Q_DIM)],
        out_specs=[_rows(BLOCK, Q_DIM), kv_out, kv_out, kv_out, kv_out, _full((N_Q_HEADS, LANES))],
        out_shape=[jax.ShapeDtypeStruct((seq, Q_DIM), F32), kv_shape, kv_shape, kv_shape, kv_shape,
                   jax.ShapeDtypeStruct((N_Q_HEADS, LANES), F32)],
        sem=("arbitrary",))(probs, p_sinks, q, k, k, v, v, do)


def _rope_bwd(dq, dkc, dkp, dvc, dvp, rc, rsa, rsb):
    seq = dq.shape[0]
    nb = seq // BLOCK
    tm = min(TOKEN_TILE, seq)
    nt = seq // tm
    per = tm // BLOCK
    nxt = pl.BlockSpec((BLOCK, KV_DIM), lambda i: (jnp.minimum((i + 1) * per, nb - 1), 0))

    def body(dq_ref, dkc_ref, dkp_ref, dkn_ref, dvc_ref, dvp_ref, dvn_ref, c_ref, sa_ref, sb_ref, o_ref, db_ref):
        i = pl.program_id(0)

        @pl.when(i == 0)
        def _():
            db_ref[...] = jnp.zeros_like(db_ref)

        c, sa, sb = c_ref[...], sa_ref[...], sb_ref[...]
        last = i == nt - 1

        def from_next_block(prev_ref, next_ref):
            tail = jnp.where(last, 0.0, next_ref[...])
            return tail if per == 1 else jnp.concatenate([prev_ref[BLOCK:, :], tail], axis=0)

        dk = dkc_ref[...] + from_next_block(dkp_ref, dkn_ref)
        dv = dvc_ref[...] + from_next_block(dvp_ref, dvn_ref)
        for l in range(Q_DIM // LANES):
            ls = slice(l * LANES, (l + 1) * LANES)
            blk = _rope_t(dq_ref[:, ls], c, sa, sb)
            o_ref[:, ls] = blk.astype(BF16)
            db_ref[:, ls] += _sum_rows(blk)
        dkr = _rope_t(dk, c, sa, sb)
        o_ref[:, Q_DIM:Q_DIM + KV_DIM] = dkr.astype(BF16)
        db_ref[:, Q_DIM:Q_DIM + KV_DIM] += _sum_rows(dkr)
        o_ref[:, Q_DIM + KV_DIM:] = dv.astype(BF16)
        db_ref[:, Q_DIM + KV_DIM:] += _sum_rows(dv)

    kv = _rows(tm, KV_DIM)
    tab = _rows(tm, LANES)
    return _call(
        body, name="rope_bwd", grid=(nt,),
        in_specs=[_rows(tm, Q_DIM), kv, kv, nxt, kv, kv, nxt, tab, tab, tab],
        out_specs=[_rows(tm, QKV_DIM), _full((1, QKV_DIM))],
        out_shape=[jax.ShapeDtypeStruct((seq, QKV_DIM), BF16), jax.ShapeDtypeStruct((1, QKV_DIM), F32)],
        sem=("arbitrary",))(dq, dkc, dkp, dkp, dvc, dvp, dvp, rc, rsa, rsb)


def _adamw(w, g, m, v):
    m = ADAM_B1 * m + (1.0 - ADAM_B1) * g
    v = ADAM_B2 * v + (1.0 - ADAM_B2) * (g * g)
    m_hat = m / (1.0 - ADAM_B1 ** ADAM_STEP)
    v_hat = v / (1.0 - ADAM_B2 ** ADAM_STEP)
    delta = -ADAM_LR * (m_hat / (jnp.sqrt(v_hat) + ADAM_EPS) + ADAM_WD * w)
    return delta, m, v


def _sum_slots(name, parts):
    _, rows, cols = parts.shape
    tr = rows if rows <= 512 else ROWS_FF

    def body(p_ref, g_ref):
        g = p_ref[0].astype(F32)
        for d in range(1, N_DEV):
            g = g + p_ref[d].astype(F32)
        g_ref[...] = g

    return _call(
        body, name=name, grid=(rows // tr,),
        in_specs=[pl.BlockSpec((N_DEV, tr, cols), lambda i: (0, i, 0))],
        out_specs=_rows(tr, cols), out_shape=jax.ShapeDtypeStruct((rows, cols), F32),
        sem=("parallel",))(parts)


def _adamw_native(name, g, w, m, v):
    layers, rows, cols = w.shape
    tr = rows if rows <= 512 else 256

    def body(g_ref, w_ref, m_ref, v_ref, d_ref, nm_ref, nv_ref):
        d_ref[...], nm_ref[...], nv_ref[...] = _adamw(w_ref[...], g_ref[...], m_ref[...], v_ref[...])

    spec = pl.BlockSpec((1, tr, cols), lambda l, i: (l, i, 0))
    shape = jax.ShapeDtypeStruct(w.shape, F32)
    return _call(
        body, name=name, grid=(layers, rows // tr), in_specs=[spec, spec, spec, spec],
        out_specs=[spec, spec, spec], out_shape=[shape, shape, shape],
        sem=("parallel", "parallel"))(g, w, m, v)


def _adamw_replicated(parts, w, m, v):
    def body(p_ref, w_ref, m_ref, v_ref, g_ref, d_ref, nm_ref, nv_ref):
        g = p_ref[0]
        for j in range(1, N_DEV):
            g = g + p_ref[j]
        g_ref[...] = g
        d_ref[...], nm_ref[...], nv_ref[...] = _adamw(w_ref[...], g, m_ref[...], v_ref[...])

    spec = _full((REPL_ROWS, D_MODEL))
    shape = jax.ShapeDtypeStruct((REPL_ROWS, D_MODEL), F32)
    return _call(
        body, name="adamw_replicated", grid=(1,),
        in_specs=[_full((N_DEV, REPL_ROWS, D_MODEL)), spec, spec, spec],
        out_specs=[spec, spec, spec, spec], out_shape=[shape, shape, shape, shape],
        sem=("arbitrary",))(parts, w, m, v)


_MESH = pl.DeviceIdType.MESH
_ANY = pl.BlockSpec(memory_space=pl.ANY)


def _all_gather(name, xs):
    rows, cols = xs.shape

    def body(x_ref, out_ref, send_sems, recv_sems, local_sem):
        x, y, c = lax.axis_index("x"), lax.axis_index("y"), lax.axis_index("c")
        me, sibling = (x, y, c), (x, y, 1 - c)
        chips = [(1 - x, y), (x, 1 - y), (1 - x, 1 - y)]

        def slot(px, py, pc):
            return out_ref.at[4 * px + 2 * py + pc]

        def copy(k, block, to, src=None):
            return pltpu.make_async_remote_copy(
                src_ref=slot(*block) if src is None else src, dst_ref=slot(*block),
                send_sem=send_sems.at[k], recv_sem=recv_sems.at[k], device_id=to, device_id_type=_MESH)

        mine = pltpu.make_async_copy(x_ref, slot(*me), local_sem)
        mine.start()
        first = [copy(0, me, sibling, src=x_ref)]
        first += [copy(1 + j, me, (*chip, c), src=x_ref) for j, chip in enumerate(chips)]
        for cp in first:
            cp.start()
        passed = [copy(4 + j, (*chip, c), sibling) for j, chip in enumerate(chips)]
        for j, chip in enumerate(chips):
            copy(1 + j, (*chip, c), me).wait_recv()
            passed[j].start()
        copy(0, sibling, me).wait_recv()
        for j, chip in enumerate(chips):
            copy(4 + j, (*chip, 1 - c), me).wait_recv()
        for cp in first + passed:
            cp.wait_send()
        mine.wait()

    return pl.pallas_call(
        body, name=name, out_shape=jax.ShapeDtypeStruct((N_DEV, rows, cols), xs.dtype),
        in_specs=[_ANY], out_specs=_ANY,
        scratch_shapes=[pltpu.SemaphoreType.DMA((7,)), pltpu.SemaphoreType.DMA((7,)), pltpu.SemaphoreType.DMA],
    )(xs)


N_PEERS = N_DEV - 1
_HBM = pl.BlockSpec(memory_space=pltpu.HBM)
_SEM = pl.BlockSpec(memory_space=pltpu.SEMAPHORE)
_DATAFLOW = pltpu.SideEffectType.DATAFLOW_SIDE_EFFECTING
_TOKEN = jax.ShapeDtypeStruct((SUBLANES, LANES), F32)


def _peers():
    x, y, c = lax.axis_index("x"), lax.axis_index("y"), lax.axis_index("c")
    out = []
    for k in range(1, N_DEV):
        px = 1 - x if k & 4 else x
        py = 1 - y if k & 2 else y
        pc = 1 - c if k & 1 else c
        out.append(((px, py, pc), 4 * px + 2 * py + pc))
    return 4 * x + 2 * y + c, out


def _in_hbm(a):
    return pltpu.with_memory_space_constraint(a, pltpu.HBM)


def _landing(rows):
    return _in_hbm(lax.empty((N_DEV, rows, D_MODEL), BF16))


def _sem_pair():
    return pltpu.SemaphoreType.DMA((N_PEERS,)), pltpu.SemaphoreType.DMA((N_PEERS,))


def _gather_start(name, payloads):
    n = len(payloads)

    def body(*refs):
        src, land = refs[:n], refs[n:2 * n]
        sems = refs[2 * n:4 * n]
        token = refs[-1]
        me, peers = _peers()
        for g in range(n):
            for k, (pos, _) in enumerate(peers):
                pltpu.make_async_remote_copy(
                    src_ref=src[g], dst_ref=land[g].at[me], send_sem=sems[2 * g].at[k],
                    recv_sem=sems[2 * g + 1].at[k], device_id=pos, device_id_type=_MESH).start()
        token[...] = jnp.zeros_like(token)

    lands = [_landing(p.shape[0]) for p in payloads]
    sem_shapes = [s for _ in payloads for s in _sem_pair()]
    hbm_shapes = [pltpu.HBM(a.shape, a.dtype) for a in list(payloads) + lands]
    out = pl.pallas_call(
        body, name=name, out_shape=(*sem_shapes, *hbm_shapes, _TOKEN),
        in_specs=[_HBM] * (2 * n), out_specs=(*[_SEM] * (2 * n), *[_HBM] * (2 * n), pl.BlockSpec(memory_space=pltpu.VMEM)),
        input_output_aliases={i: 2 * n + i for i in range(2 * n)},
        compiler_params=pltpu.CompilerParams(has_side_effects=_DATAFLOW),
    )(*[_in_hbm(p) for p in payloads], *lands)
    sems, thru = out[:2 * n], out[2 * n:4 * n]
    return [(thru[g], thru[n + g], sems[2 * g], sems[2 * g + 1]) for g in range(n)], out[-1]


def _gather_wait(name, group, after):
    payload, land, send_sems, recv_sems = group

    def body(src_ref, land_ref, send_ref, recv_ref, after_ref, src_out, land_out):
        _, peers = _peers()
        for k, (pos, idx) in enumerate(peers):
            cp = pltpu.make_async_remote_copy(
                src_ref=src_ref, dst_ref=land_ref.at[idx], send_sem=send_ref.at[k], recv_sem=recv_ref.at[k],
                device_id=pos, device_id_type=_MESH)
            cp.wait_send()
            cp.wait_recv()

    _, land = pl.pallas_call(
        body, name=name, out_shape=(pltpu.HBM(payload.shape, payload.dtype), pltpu.HBM(land.shape, land.dtype)),
        in_specs=[_HBM, _HBM, _SEM, _SEM, _ANY], out_specs=(_HBM, _HBM), input_output_aliases={0: 0, 1: 1},
        compiler_params=pltpu.CompilerParams(has_side_effects=_DATAFLOW),
    )(payload, land, send_sems, recv_sems, after)
    me = 4 * lax.axis_index("x") + 2 * lax.axis_index("y") + lax.axis_index("c")
    return lax.dynamic_update_slice(land, payload[None], (me, 0, 0))


def _scatter_start(name, blocks):
    rows = blocks.shape[1]

    def body(blocks_ref, land_ref, send_sems, recv_sems, blocks_out, land_out, token):
        me, peers = _peers()
        for k, (pos, idx) in enumerate(peers):
            pltpu.make_async_remote_copy(
                src_ref=blocks_ref.at[idx], dst_ref=land_ref.at[me], send_sem=send_sems.at[k],
                recv_sem=recv_sems.at[k], device_id=pos, device_id_type=_MESH).start()
        token[...] = jnp.zeros_like(token)

    land = _landing(rows)
    send_sems, recv_sems, blocks_thru, land_thru, token = pl.pallas_call(
        body, name=name,
        out_shape=(*_sem_pair(), pltpu.HBM(blocks.shape, blocks.dtype), pltpu.HBM(land.shape, land.dtype), _TOKEN),
        in_specs=[_HBM, _HBM], out_specs=(_SEM, _SEM, _HBM, _HBM, pl.BlockSpec(memory_space=pltpu.VMEM)),
        input_output_aliases={0: 2, 1: 3},
        compiler_params=pltpu.CompilerParams(has_side_effects=_DATAFLOW),
    )(_in_hbm(blocks), land)
    return (blocks_thru, land_thru, send_sems, recv_sems), token


def _scatter_wait(groups, after):
    n = len(groups)

    def body(*refs):
        blocks, land = refs[:n], refs[n:2 * n]
        sems = refs[2 * n:4 * n]
        _, peers = _peers()
        for g in range(n):
            for k, (pos, idx) in enumerate(peers):
                cp = pltpu.make_async_remote_copy(
                    src_ref=blocks[g].at[idx], dst_ref=land[g].at[idx], send_sem=sems[2 * g].at[k],
                    recv_sem=sems[2 * g + 1].at[k], device_id=pos, device_id_type=_MESH)
                cp.wait_send()
                cp.wait_recv()

    hbm = [grp[0] for grp in groups] + [grp[1] for grp in groups]
    sems = [s for grp in groups for s in grp[2:]]
    out = pl.pallas_call(
        body, name="rs_wait", out_shape=tuple(pltpu.HBM(a.shape, a.dtype) for a in hbm),
        in_specs=[_HBM] * (2 * n) + [_SEM] * (2 * n) + [_ANY], out_specs=tuple([_HBM] * (2 * n)),
        input_output_aliases={i: i for i in range(2 * n)},
        compiler_params=pltpu.CompilerParams(has_side_effects=_DATAFLOW),
    )(*hbm, *sems, after)
    me = 4 * lax.axis_index("x") + 2 * lax.axis_index("y") + lax.axis_index("c")
    lands = []
    for g in range(n):
        own = lax.dynamic_index_in_dim(out[g], me, axis=0, keepdims=True)
        lands.append(lax.dynamic_update_slice(out[n + g], own, (me, 0, 0)))
    return lands


def _pad_rows(flat, rows):
    return jnp.pad(flat, (0, rows * D_MODEL - flat.shape[0])).reshape(rows, D_MODEL)


SMALL_NAMES = ("conv_b_pw1", "conv_w_dw", "conv_b_dw", "conv_ln_g", "conv_ln_b", "conv_b_pw2")


def _pack_small(p):
    flat = jnp.concatenate([p[n].reshape(-1) for n in SMALL_NAMES])
    return _pad_rows(flat, ROWS_SMALL).reshape(1, ROWS_SMALL, D_MODEL)


def _unpack_small(packed):
    flat = packed.reshape(-1)
    c = D_MODEL // N_DEV
    shapes = ((1, 2 * c), (1, CONV_WIDTH, c), (1, c), (1, c), (1, c), (1, c))
    out, o = {}, 0
    for n, shape in zip(SMALL_NAMES, shapes):
        size = shape[-1] * (shape[1] if len(shape) == 3 else 1)
        out[n] = flat[o:o + size].reshape(shape)
        o += size
    return out


def _gather_payloads(p):
    t = lambda a: jnp.swapaxes(a, -1, -2).astype(BF16)
    w1t, w3t, w2 = t(p["ffn_w1"]), t(p["ffn_w3"]), p["ffn_w2"].astype(BF16)
    small = _pack_small(p).reshape(-1)[:ROWS_SMALL * D_MODEL // 2]
    small = lax.bitcast_convert_type(small, BF16).reshape(ROWS_SMALL, D_MODEL)
    conv = jnp.concatenate([t(p["conv_w_pw1"][0]), p["conv_w_pw2"][0].astype(BF16), small], axis=0)
    ffn = [jnp.concatenate([w1t[l], w3t[l], w2[l]], axis=0) for l in range(2)]
    return [t(p["attn_w_qkv"][0]), p["attn_w_o"][0].astype(BF16), ffn[0], conv, ffn[1]]


def _device_rows(land, lo, n):
    return land[:, lo:lo + n].reshape(N_DEV * n, D_MODEL)


def _unpack_conv(land):
    small = lax.bitcast_convert_type(
        land[:, ROWS_PW1 + ROWS_PW2:].reshape(N_DEV, ROWS_SMALL * D_MODEL // 2, 2), F32)
    c = D_MODEL // N_DEV
    b_pw1 = small[:, :2 * c].reshape(1, 2 * D_MODEL)
    s = 2 * c
    w_dw = small[:, s:s + CONV_WIDTH * c].reshape(N_DEV, CONV_WIDTH, c).transpose(1, 0, 2).reshape(CONV_WIDTH, D_MODEL)
    s += CONV_WIDTH * c
    b_dw, ln_g, ln_b, b_pw2 = (small[:, s + i * c:s + (i + 1) * c].reshape(1, D_MODEL) for i in range(4))
    return dict(w_pw1_t=_device_rows(land, 0, ROWS_PW1), w_pw2=_device_rows(land, ROWS_PW1, ROWS_PW2),
                b_pw1=b_pw1, w_dw=w_dw, b_dw=b_dw, ln_g=ln_g, ln_b=ln_b, b_pw2=b_pw2)


def _dest_blocks(mats):
    return jnp.concatenate([a.reshape(N_DEV, -1, D_MODEL) for a in mats], axis=1)


def _small_grad_rows(g_bpw1, g_dw, g_bdw, g_lng, g_lnb, g_bpw2):
    c = D_MODEL // N_DEV
    small = jnp.concatenate(
        [g_bpw1.reshape(N_DEV, 2 * c), g_dw.reshape(CONV_WIDTH, N_DEV, c).transpose(1, 0, 2).reshape(N_DEV, -1),
         g_bdw.reshape(N_DEV, c), g_lng.reshape(N_DEV, c), g_lnb.reshape(N_DEV, c), g_bpw2.reshape(N_DEV, c)], axis=1)
    small = jnp.pad(small, ((0, 0), (0, ROWS_SMALL * D_MODEL - SMALL_USED)))
    return small.reshape(N_DEV * ROWS_SMALL, D_MODEL).astype(BF16)


LOSS_ROW = 9


def _pack_replicated(norm_mix, norm_ffn, b_qkv, sinks, b_o, norm_final, extra=None):
    rows = [norm_mix.reshape(2, D_MODEL), norm_ffn.reshape(2, D_MODEL), _pad_rows(b_qkv.reshape(-1), 2),
            _pad_rows(sinks.reshape(-1), 1), b_o.reshape(1, D_MODEL), norm_final.reshape(1, D_MODEL)]
    if extra is not None:
        rows.append(_pad_rows(extra.reshape(-1), 1))
    p = jnp.concatenate(rows, axis=0)
    return jnp.pad(p, ((0, REPL_ROWS - p.shape[0]), (0, 0)))


def _unpack_replicated(p):
    return dict(norm_mix=p[0:2], norm_ffn=p[2:4], attn_b_qkv=p[4:6].reshape(-1)[:QKV_DIM].reshape(1, QKV_DIM),
                attn_sinks=p[6, :N_Q_HEADS].reshape(1, N_Q_HEADS), attn_b_o=p[7:8], norm_final=p[8])


WEIGHT_ORDER = ['norm_mix', 'norm_ffn', 'attn_w_qkv', 'attn_b_qkv', 'attn_sinks', 'attn_w_o', 'attn_b_o',
                'conv_w_pw1', 'conv_b_pw1', 'conv_w_dw', 'conv_b_dw', 'conv_ln_g', 'conv_ln_b', 'conv_w_pw2',
                'conv_b_pw2', 'ffn_w1', 'ffn_w3', 'ffn_w2', 'norm_final']


def kernel(x, norm_mix, norm_ffn, attn_w_qkv, attn_b_qkv, attn_sinks, attn_w_o, attn_b_o, conv_w_pw1, conv_b_pw1, conv_w_dw, conv_b_dw, conv_ln_g, conv_ln_b, conv_w_pw2, conv_b_pw2, ffn_w1, ffn_w3, ffn_w2, norm_final, loss_target, m_norm_mix, m_norm_ffn, m_attn_w_qkv, m_attn_b_qkv, m_attn_sinks, m_attn_w_o, m_attn_b_o, m_conv_w_pw1, m_conv_b_pw1, m_conv_w_dw, m_conv_b_dw, m_conv_ln_g, m_conv_ln_b, m_conv_w_pw2, m_conv_b_pw2, m_ffn_w1, m_ffn_w3, m_ffn_w2, m_norm_final, v_norm_mix, v_norm_ffn, v_attn_w_qkv, v_attn_b_qkv, v_attn_sinks, v_attn_w_o, v_attn_b_o, v_conv_w_pw1, v_conv_b_pw1, v_conv_w_dw, v_conv_b_dw, v_conv_ln_g, v_conv_ln_b, v_conv_w_pw2, v_conv_b_pw2, v_ffn_w1, v_ffn_w3, v_ffn_w2, v_norm_final):
    xs = x[0]
    target = loss_target[0]
    seq = xs.shape[0]
    my_x, my_y, my_c = lax.axis_index("x"), lax.axis_index("y"), lax.axis_index("c")

    w = dict(attn_w_qkv=attn_w_qkv, attn_w_o=attn_w_o, conv_w_pw1=conv_w_pw1, conv_b_pw1=conv_b_pw1,
             conv_w_dw=conv_w_dw, conv_b_dw=conv_b_dw, conv_ln_g=conv_ln_g, conv_ln_b=conv_ln_b,
             conv_w_pw2=conv_w_pw2, conv_b_pw2=conv_b_pw2, ffn_w1=ffn_w1, ffn_w3=ffn_w3, ffn_w2=ffn_w2)
    m = dict(attn_w_qkv=m_attn_w_qkv, attn_w_o=m_attn_w_o, conv_w_pw1=m_conv_w_pw1, conv_b_pw1=m_conv_b_pw1,
             conv_w_dw=m_conv_w_dw, conv_b_dw=m_conv_b_dw, conv_ln_g=m_conv_ln_g, conv_ln_b=m_conv_ln_b,
             conv_w_pw2=m_conv_w_pw2, conv_b_pw2=m_conv_b_pw2, ffn_w1=m_ffn_w1, ffn_w3=m_ffn_w3, ffn_w2=m_ffn_w2)
    v = dict(attn_w_qkv=v_attn_w_qkv, attn_w_o=v_attn_w_o, conv_w_pw1=v_conv_w_pw1, conv_b_pw1=v_conv_b_pw1,
             conv_w_dw=v_conv_w_dw, conv_b_dw=v_conv_b_dw, conv_ln_g=v_conv_ln_g, conv_ln_b=v_conv_ln_b,
             conv_w_pw2=v_conv_w_pw2, conv_b_pw2=v_conv_b_pw2, ffn_w1=v_ffn_w1, ffn_w3=v_ffn_w3, ffn_w2=v_ffn_w2)

    payloads = _gather_payloads(w)
    (ag_qkv, ag_wo), tok = _gather_start("ag_start_attn", payloads[:2])
    (ag_ffn0, ag_conv, ag_ffn1), tok = _gather_start(
        "ag_start_rest", [payloads[2] + tok[0, 0].astype(BF16), payloads[3], payloads[4]])
    rc, rsa, rsb = _rope_tables(seq)
    sinks = attn_sinks.reshape(N_Q_HEADS)
    g_mix0, g_mix1 = norm_mix[0:1] + tok[0, 0], norm_mix[1:2]
    g_ffn0, g_ffn1 = norm_ffn[0:1], norm_ffn[1:2]
    g_fin = norm_final.reshape(1, D_MODEL)

    w_qkv_t = _gather_wait("ag_wait_qkv", ag_qkv, tok).reshape(QKV_DIM, D_MODEL)
    y0, q, k, vv = _qkv_fwd(xs, g_mix0, w_qkv_t, attn_b_qkv, rc, rsa, rsb)
    attn, probs, p_sinks = _attn_fwd(sinks, q, k, vv)
    w_o = _gather_wait("ag_wait_wo", ag_wo, attn).reshape(Q_DIM, D_MODEL)
    h1 = _mm_res("attn_out_proj", attn, w_o, attn_b_o, xs)
    w_ffn0 = _gather_wait("ag_wait_ffn0", ag_ffn0, h1)
    f0, u0, p0, s0 = _ffn_up("ffn0_up", h1, g_ffn0, w_ffn0)
    h2 = _ffn_down("ffn0_down", s0, w_ffn0, h1)
    wt = _unpack_conv(_gather_wait("ag_wait_conv", ag_conv, h2))
    wd = jnp.concatenate([wt["w_dw"][::-1], jnp.zeros((TAPS_PAD - CONV_WIDTH, D_MODEL), F32)], axis=0)
    y1, a, dwc, z = _conv_fwd(h2, g_mix1, wt["w_pw1_t"], wt["b_pw1"], wd, wt["b_dw"], wt["ln_g"], wt["ln_b"])
    h3 = _mm_res("conv_out_proj", z, wt["w_pw2"], wt["b_pw2"], h2)
    w_ffn1 = _gather_wait("ag_wait_ffn1", ag_ffn1, h3)
    f1, u1, p1, s1 = _ffn_up("ffn1_up", h3, g_ffn1, w_ffn1)
    dh4, dh4b, sq, dg_fin = _ffn_down_loss(s1, w_ffn1, h3, target, g_fin)

    du1, dp1 = _ffn_bwd_act("ffn1_bwd_act", dh4b, u1, p1, w_ffn1)
    dh3, dh3b, dg_ffn1 = _ffn_bwd_in("ffn1_bwd_in", du1, dp1, w_ffn1, h3, dh4, g_ffn1)
    gw2_1 = _mm_tn("ffn1_dw2", s1, dh4b, tk=FF_TILE)
    gw1t_1 = _mm_tn("ffn1_dw1", du1, f1, tk=FF_TILE)
    gw3t_1 = _mm_tn("ffn1_dw3", dp1, f1, tk=FF_TILE)
    rs_ffn1, tok = _scatter_start("rs_start_ffn1", _dest_blocks([gw1t_1, gw3t_1, gw2_1]))

    da, dlg, dlb, dbdw, dwd, dbpw1, dbpw2 = _conv_bwd(dh3, wt["w_pw2"], dwc, a, wt["ln_g"] + tok[0, 0],
                                                     wt["ln_b"], wd)
    gpw2 = _mm_tn("conv_dw_pw2", z, dh3b, tk=D_MODEL)
    dh2, dh2b, dg_mix1 = _mm_rms_bwd("conv_in_bwd", da, wt["w_pw1_t"], h2, dh3, g_mix1)
    gpw1t = _mm_tn("conv_dw_pw1", da, y1, tk=D_MODEL)
    small_rows = _small_grad_rows(dbpw1, dwd[:CONV_WIDTH][::-1], dbdw, dlg, dlb, dbpw2)
    rs_conv, tok = _scatter_start("rs_start_conv", _dest_blocks([gpw1t, gpw2, small_rows]))

    du0, dp0 = _ffn_bwd_act("ffn0_bwd_act", dh2b, u0, p0, w_ffn0)
    dh1, dh1b, dg_ffn0 = _ffn_bwd_in("ffn0_bwd_in", du0, dp0, w_ffn0, h1, dh2, g_ffn0 + tok[0, 0])
    gw2_0 = _mm_tn("ffn0_dw2", s0, dh2b, tk=FF_TILE)
    gw1t_0 = _mm_tn("ffn0_dw1", du0, f0, tk=FF_TILE)
    gw3t_0 = _mm_tn("ffn0_dw3", dp0, f0, tk=FF_TILE)
    rs_ffn0, tok = _scatter_start("rs_start_ffn0", _dest_blocks([gw1t_0, gw3t_0, gw2_0]))

    gwo = _mm_tn("attn_dw_o", attn, dh1b, tk=D_MODEL)
    rs_wo, tok2 = _scatter_start("rs_start_wo", _dest_blocks([gwo]))
    dattn, dbo = _nt_bias("attn_out_bwd", dh1, w_o)
    dq, dkc, dkp, dvc, dvp, dsink = _attn_bwd(probs, p_sinks + (tok[0, 0] + tok2[0, 0]), q, k, vv, dattn)
    dqkv, dbqkv = _rope_bwd(dq, dkc, dkp, dvc, dvp, rc, rsa, rsb)
    gqkvt = _mm_tn("attn_dw_qkv", dqkv, y0, tk=QKV_DIM)
    rs_qkv, tok = _scatter_start("rs_start_qkv", _dest_blocks([gqkvt]))
    dx, _, dg_mix0 = _mm_rms_bwd("qkv_in_bwd", dqkv, w_qkv_t, xs, dh1, g_mix0 + tok[0, 0])

    p_ffn1, p_conv, p_ffn0, p_wo, p_qkv = _scatter_wait([rs_ffn1, rs_conv, rs_ffn0, rs_wo, rs_qkv], dx)
    g_ffn = [_sum_slots("rs_sum_ffn0", p_ffn0), _sum_slots("rs_sum_ffn1", p_ffn1)]
    g_conv = _sum_slots("rs_sum_conv", p_conv)
    g_wo = _sum_slots("rs_sum_wo", p_wo)
    g_qkv = _sum_slots("rs_sum_qkv", p_qkv)

    def ff(slot, transpose):
        blk = jnp.stack([g[slot * ROWS_FF:(slot + 1) * ROWS_FF] for g in g_ffn])
        return jnp.swapaxes(blk, 1, 2) if transpose else blk

    grads = dict(
        ffn_w1=ff(W1T_SLOT, True), ffn_w3=ff(W3T_SLOT, True), ffn_w2=ff(W2_SLOT, False),
        attn_w_qkv=g_qkv.T[None], attn_w_o=g_wo[None],
        conv_w_pw1=g_conv[:ROWS_PW1].T[None], conv_w_pw2=g_conv[ROWS_PW1:ROWS_PW1 + ROWS_PW2][None])
    sharded = [dict(grads), {}, {}, {}]
    for n in grads:
        for dst, t in zip(sharded[1:], _adamw_native("adamw_" + n, grads[n], w[n], m[n], v[n])):
            dst[n] = t
    g_small = g_conv[ROWS_PW1 + ROWS_PW2:][None]
    small_out = _adamw_native("adamw_small", g_small, _pack_small(w), _pack_small(m), _pack_small(v))
    for dst, t in zip(sharded, (g_small,) + tuple(small_out)):
        dst.update(_unpack_small(t))

    part = _pack_replicated(jnp.concatenate([dg_mix0, dg_mix1]), jnp.concatenate([dg_ffn0, dg_ffn1]),
                            dbqkv, -dsink[:, 0], dbo, dg_fin, extra=sq[0, 0:1])
    parts = _all_gather("ag_replicated_grads", part)
    w_rep = _pack_replicated(norm_mix, norm_ffn, attn_b_qkv, attn_sinks, attn_b_o, norm_final)
    m_rep = _pack_replicated(m_norm_mix, m_norm_ffn, m_attn_b_qkv, m_attn_sinks, m_attn_b_o, m_norm_final)
    v_rep = _pack_replicated(v_norm_mix, v_norm_ffn, v_attn_b_qkv, v_attn_sinks, v_attn_b_o, v_norm_final)
    rep_out = _adamw_replicated(parts, w_rep, m_rep, v_rep)
    replicated = [_unpack_replicated(t) for t in rep_out]
    loss = rep_out[0][LOSS_ROW, 0] * (0.5 / D_MODEL)

    outs = [loss, dx.reshape(1, seq, D_MODEL)]
    for sh, rp in zip(sharded, replicated):
        merged = {**sh, **rp}
        outs += [merged[n] for n in WEIGHT_ORDER]
    return tuple(outs)
```

```python
import jax
import jax.numpy as jnp
from jax import lax
from jax.experimental import pallas as pl
from jax.experimental.pallas import tpu as pltpu

F32 = jnp.float32
BF16 = jnp.bfloat16

D_MODEL = 1024
HEAD_DIM = 64
N_Q_HEADS = 16
N_KV_HEADS = 2
Q_PER_KV = 8
Q_DIM = N_Q_HEADS * HEAD_DIM
KV_DIM = N_KV_HEADS * HEAD_DIM
QKV_DIM = Q_DIM + 2 * KV_DIM
BLOCK = 128
CONV_WIDTH = 31
D_FF = 2816
ROPE_THETA = 10000.0
RMS_EPS = 1e-5
LN_EPS = 1e-5
ADAM_LR = 0.001
ADAM_B1 = 0.9
ADAM_B2 = 0.999
ADAM_EPS = 1e-08
ADAM_WD = 0.01
ADAM_STEP = 10
N_DEV = 8

LANES = 128
SUBLANES = 8
TOKEN_TILE = 512
CONV_CHUNK = 64
CONV_HALO = 32
TAPS_PAD = 32
VMEM_LIMIT = 56 * 1024 * 1024
NEG_INF = float(jnp.finfo(jnp.float32).min)

ROWS_FF = D_FF // N_DEV
W1T_SLOT, W3T_SLOT, W2_SLOT = 0, 1, 2
ROWS_QKV = QKV_DIM // N_DEV
ROWS_WO = Q_DIM // N_DEV
ROWS_PW1 = 2 * D_MODEL // N_DEV
ROWS_PW2 = D_MODEL // N_DEV
ROWS_SMALL = 16
SMALL_USED = 2 * D_MODEL // N_DEV + CONV_WIDTH * (D_MODEL // N_DEV) + 4 * (D_MODEL // N_DEV)
FF_SPLIT = 2
FF_TILE_DEVS = N_DEV // FF_SPLIT
FF_TILE = FF_TILE_DEVS * ROWS_FF
REPL_ROWS = 16


def _call(body, *, name, grid, in_specs, out_specs, out_shape, scratch=(), sem=None):
    return pl.pallas_call(
        body, name=name, grid=grid, in_specs=in_specs, out_specs=out_specs, out_shape=out_shape,
        scratch_shapes=list(scratch),
        compiler_params=pltpu.CompilerParams(dimension_semantics=sem, vmem_limit_bytes=VMEM_LIMIT))


def _full(shape):
    return pl.BlockSpec(shape, lambda *_: (0,) * len(shape))


def _resident(shape):
    return pl.BlockSpec(shape, lambda *_: (0,) * len(shape), pipeline_mode=pl.Buffered(1))


def _rows(tm, n):
    return pl.BlockSpec((tm, n), lambda i, *_: (i, 0))


def _sig(x):
    return 1.0 / (1.0 + jnp.exp(-x))


def _sum_rows(x):
    return jnp.sum(x, axis=0, keepdims=True)


def _nt(a, b):
    return lax.dot_general(a, b, (((1,), (1,)), ((), ())), preferred_element_type=F32)


def _tn(a, b):
    return lax.dot_general(a, b, (((0,), (0,)), ((), ())), preferred_element_type=F32)


def _rms_stats(x):
    return lax.rsqrt(jnp.mean(x * x, axis=-1, keepdims=True) + RMS_EPS)


def _rms_bwd(dy, x, g, dres):
    r = _rms_stats(x)
    n = x * r
    dn = dy * g
    dx = dres + r * (dn - n * jnp.mean(dn * n, axis=-1, keepdims=True))
    return dx, _sum_rows(dy * n)


def _rope_tables(seq):
    pos = jnp.arange(seq, dtype=F32)
    inv_freq = ROPE_THETA ** (-jnp.arange(0, HEAD_DIM, 2, dtype=F32) / HEAD_DIM)
    ang = pos[:, None] * inv_freq[None, :]
    cos, sin = jnp.cos(ang), jnp.sin(ang)
    zero = jnp.zeros_like(sin)
    reps = LANES // HEAD_DIM
    c = jnp.tile(jnp.concatenate([cos, cos], axis=1), (1, reps))
    sa = jnp.tile(jnp.concatenate([-sin, zero], axis=1), (1, reps))
    sb = jnp.tile(jnp.concatenate([zero, sin], axis=1), (1, reps))
    return c, sa, sb


def _rope(t, c, sa, sb):
    half = HEAD_DIM // 2
    return t * c + pltpu.roll(t, LANES - half, 1) * sa + pltpu.roll(t, half, 1) * sb


def _rope_t(dt, c, sa, sb):
    half = HEAD_DIM // 2
    return dt * c + pltpu.roll(dt * sa, half, 1) + pltpu.roll(dt * sb, LANES - half, 1)


def _qkv_fwd(x, g, w, b, rc, rsa, rsb):
    seq = x.shape[0]
    tm = min(TOKEN_TILE, seq)

    def body(x_ref, g_ref, w_ref, b_ref, c_ref, sa_ref, sb_ref, y_ref, q_ref, k_ref, v_ref):
        xv = x_ref[...]
        y = (xv * _rms_stats(xv) * g_ref[...]).astype(BF16)
        y_ref[...] = y
        qkv = _nt(y, w_ref[...]) + b_ref[...]
        c, sa, sb = c_ref[...], sa_ref[...], sb_ref[...]
        for i in range(Q_DIM // LANES):
            blk = _rope(qkv[:, i * LANES:(i + 1) * LANES], c, sa, sb)
            q_ref[:, i * LANES:(i + 1) * LANES] = (blk * (HEAD_DIM ** -0.5)).astype(BF16)
        k_ref[...] = _rope(qkv[:, Q_DIM:Q_DIM + KV_DIM], c, sa, sb).astype(BF16)
        v_ref[...] = qkv[:, Q_DIM + KV_DIM:].astype(BF16)

    return _call(
        body, name="qkv_fwd", grid=(seq // tm,),
        in_specs=[_rows(tm, D_MODEL), _full((1, D_MODEL)), _full((QKV_DIM, D_MODEL)), _full((1, QKV_DIM)),
                  _rows(tm, LANES), _rows(tm, LANES), _rows(tm, LANES)],
        out_specs=[_rows(tm, D_MODEL), _rows(tm, Q_DIM), _rows(tm, KV_DIM), _rows(tm, KV_DIM)],
        out_shape=[jax.ShapeDtypeStruct((seq, D_MODEL), BF16), jax.ShapeDtypeStruct((seq, Q_DIM), BF16),
                   jax.ShapeDtypeStruct((seq, KV_DIM), BF16), jax.ShapeDtypeStruct((seq, KV_DIM), BF16)],
        sem=("parallel",))(x, g, w, b, rc, rsa, rsb)


GROUP_ROWS = Q_PER_KV * BLOCK


def _band_mask(n, rows=GROUP_ROWS):
    row = lax.broadcasted_iota(jnp.int32, (rows, 2 * BLOCK), 0) & (BLOCK - 1)
    col = lax.broadcasted_iota(jnp.int32, (rows, 2 * BLOCK), 1)
    rel = row + BLOCK - col
    return (rel >= 0) & (rel < BLOCK) & ((col >= BLOCK) | (n > 0))


def _softmax_with_sink(s, mask, sink):
    s = jnp.where(mask, s, NEG_INF)
    m = jnp.maximum(jnp.max(s, axis=-1, keepdims=True), sink)
    p = jnp.exp(s - m)
    e_sink = jnp.exp(sink - m)
    inv = 1.0 / (jnp.sum(p, axis=-1, keepdims=True) + e_sink)
    return p * inv, e_sink * inv


PAIRS_PER_KV = Q_PER_KV // 2


def _kv_specs():
    cur = pl.BlockSpec((BLOCK, KV_DIM), lambda n: (n, 0))
    prev = pl.BlockSpec((BLOCK, KV_DIM), lambda n: (jnp.maximum(n - 1, 0), 0))
    return cur, prev


def _low_lanes():
    return lax.broadcasted_iota(jnp.int32, (2 * BLOCK, KV_DIM), 1) < HEAD_DIM


def _kv_low_high(prev_ref, cur_ref, j, low):
    both = jnp.concatenate([prev_ref[...], cur_ref[...]], axis=0).astype(F32)
    swapped = pltpu.roll(both, HEAD_DIM, 1)
    at_low, at_high = (both, swapped) if j == 0 else (swapped, both)
    return jnp.where(low, at_low, 0.0).astype(BF16), jnp.where(low, 0.0, at_high).astype(BF16)


def _fold_pair_halves(acc, j, low):
    folded = acc + pltpu.roll(acc, HEAD_DIM, 1)
    return jnp.where(low, folded, 0.0) if j == 0 else jnp.where(low, 0.0, folded)


def _pair_lanes(j, i):
    g = j * PAIRS_PER_KV + i
    return slice(g * LANES, (g + 1) * LANES), 2 * g


def _attn_fwd(sinks, q, k, v):
    seq = q.shape[0]
    cur, prev = _kv_specs()

    def body(sink_ref, q_ref, kc_ref, kp_ref, vc_ref, vp_ref, o_ref, p_ref, ps_ref):
        mask = _band_mask(pl.program_id(0), BLOCK)
        lane = lax.broadcasted_iota(jnp.int32, (BLOCK, LANES), 1)
        p_sinks = jnp.zeros((BLOCK, LANES), F32)
        for j in range(N_KV_HEADS):
            cs = slice(j * HEAD_DIM, (j + 1) * HEAD_DIM)
            kk = jnp.concatenate([kp_ref[:, cs], kc_ref[:, cs]], axis=0)
            vv = jnp.concatenate([vp_ref[:, cs], vc_ref[:, cs]], axis=0)
            for gq in range(Q_PER_KV):
                h = j * Q_PER_KV + gq
                hs = slice(h * HEAD_DIM, (h + 1) * HEAD_DIM)
                probs, p_sink = _softmax_with_sink(_nt(q_ref[:, hs], kk), mask, sink_ref[h])
                pb = probs.astype(BF16)
                p_ref[h] = pb
                p_sinks = jnp.where(lane == h, p_sink, p_sinks)
                o_ref[:, hs] = jnp.dot(pb, vv, preferred_element_type=F32).astype(BF16)
        ps_ref[...] = p_sinks

    return _call(
        body, name="attn_fwd", grid=(seq // BLOCK,),
        in_specs=[pl.BlockSpec(memory_space=pltpu.SMEM), _rows(BLOCK, Q_DIM), cur, prev, cur, prev],
        out_specs=[_rows(BLOCK, Q_DIM), pl.BlockSpec((N_Q_HEADS, BLOCK, 2 * BLOCK), lambda n: (0, n, 0)),
                   _rows(BLOCK, LANES)],
        out_shape=[jax.ShapeDtypeStruct((seq, Q_DIM), BF16),
                   jax.ShapeDtypeStruct((N_Q_HEADS, seq, 2 * BLOCK), BF16),
                   jax.ShapeDtypeStruct((seq, LANES), F32)],
        sem=("parallel",))(sinks, q, k, k, v, v)


def _mm_res(name, a, w, b, res):
    seq, kdim = a.shape
    n = w.shape[1]
    tm = min(2 * TOKEN_TILE, seq)

    def body(a_ref, w_ref, b_ref, r_ref, o_ref):
        o_ref[...] = r_ref[...] + (jnp.dot(a_ref[...], w_ref[...], preferred_element_type=F32) + b_ref[...])

    return _call(
        body, name=name, grid=(seq // tm,),
        in_specs=[_rows(tm, kdim), _resident((kdim, n)), _full((1, n)), _rows(tm, n)],
        out_specs=_rows(tm, n), out_shape=jax.ShapeDtypeStruct((seq, n), F32),
        sem=("parallel",))(a, w, b, res)


def _ff_tile_spec(slot):
    return pl.BlockSpec((FF_TILE_DEVS, ROWS_FF, D_MODEL), lambda j, i: (j, slot, 0))


def _ff_whole_spec(slot):
    return pl.BlockSpec((N_DEV, ROWS_FF, D_MODEL), lambda i: (0, slot, 0), pipeline_mode=pl.Buffered(1))


def _ffn_up(name, h, g, gathered):
    seq = h.shape[0]
    tm = min(TOKEN_TILE, seq)

    def body(h_ref, g_ref, w1_ref, w3_ref, f_ref, u_ref, w_ref, s_ref):
        hv = h_ref[...]
        f = (hv * _rms_stats(hv) * g_ref[...]).astype(BF16)
        f_ref[...] = f
        u = _nt(f, w1_ref[...].reshape(FF_TILE, D_MODEL))
        w = _nt(f, w3_ref[...].reshape(FF_TILE, D_MODEL))
        u_ref[...] = u.astype(BF16)
        w_ref[...] = w.astype(BF16)
        s_ref[...] = (u * _sig(u) * w).astype(BF16)

    n_t = seq // tm
    tile_ff = pl.BlockSpec((tm, FF_TILE), lambda j, i: (i, j))
    f_spec = pl.BlockSpec((tm, D_MODEL), lambda j, i: (jnp.where(j == 0, i, n_t), 0))
    ff_shape = jax.ShapeDtypeStruct((seq, D_FF), BF16)
    return _call(
        body, name=name, grid=(FF_SPLIT, n_t),
        in_specs=[pl.BlockSpec((tm, D_MODEL), lambda j, i: (i, 0)), _full((1, D_MODEL)),
                  _ff_tile_spec(W1T_SLOT), _ff_tile_spec(W3T_SLOT)],
        out_specs=[f_spec, tile_ff, tile_ff, tile_ff],
        out_shape=[jax.ShapeDtypeStruct((seq + tm, D_MODEL), BF16), ff_shape, ff_shape, ff_shape],
        sem=("arbitrary", "arbitrary"))(h, g, gathered, gathered)


def _ffn_down(name, s, gathered, res):
    seq = s.shape[0]
    tm = min(2 * TOKEN_TILE, seq)

    def body(s_ref, w_ref, r_ref, o_ref):
        w2 = w_ref[...].reshape(D_FF, D_MODEL)
        o_ref[...] = r_ref[...] + jnp.dot(s_ref[...], w2, preferred_element_type=F32)

    return _call(
        body, name=name, grid=(seq // tm,),
        in_specs=[_rows(tm, D_FF), _ff_whole_spec(W2_SLOT), _rows(tm, D_MODEL)],
        out_specs=_rows(tm, D_MODEL), out_shape=jax.ShapeDtypeStruct((seq, D_MODEL), F32),
        sem=("parallel",))(s, gathered, res)


def _conv_fwd(h, g, wpw1, bpw1, wd, bdw, lng, lnb):
    seq = h.shape[0]
    tm = min(TOKEN_TILE, seq)
    n_chunks = tm // CONV_CHUNK
    win = CONV_CHUNK + CONV_HALO

    def body(h_ref, g_ref, w_ref, b_ref, wd_ref, bdw_ref, lng_ref, lnb_ref,
             y_ref, a_ref, dwc_ref, z_ref, gbuf):
        i = pl.program_id(0)

        @pl.when(i == 0)
        def _():
            gbuf[0:CONV_HALO, :] = jnp.zeros((CONV_HALO, D_MODEL), F32)

        @pl.when(i > 0)
        def _():
            gbuf[0:CONV_HALO, :] = gbuf[tm:tm + CONV_HALO, :]

        hv = h_ref[...]
        y = (hv * _rms_stats(hv) * g_ref[...]).astype(BF16)
        y_ref[...] = y
        a = _nt(y, w_ref[...]) + b_ref[...]
        a_ref[...] = a.astype(BF16)
        gbuf[CONV_HALO:CONV_HALO + tm, :] = a[:, :D_MODEL] * _sig(a[:, D_MODEL:])

        def chunk(c, carry):
            r0 = pl.multiple_of(c * CONV_CHUNK, CONV_CHUNK)
            for l in range(D_MODEL // LANES):
                ls = slice(l * LANES, (l + 1) * LANES)
                gw = gbuf[pl.ds(r0, win), ls]
                acc = jnp.zeros((CONV_CHUNK, LANES), F32) + bdw_ref[:, ls]
                for s in range(SUBLANES):
                    gs = gw if s == 0 else pltpu.roll(gw, s, 0)
                    for q in range(CONV_HALO // SUBLANES):
                        d = SUBLANES * q + s
                        if d < CONV_WIDTH:
                            lo = CONV_HALO - SUBLANES * q
                            acc = acc + wd_ref[d:d + 1, ls] * gs[lo:lo + CONV_CHUNK]
                dwc_ref[pl.ds(r0, CONV_CHUNK), ls] = acc
            return carry

        lax.fori_loop(0, n_chunks, chunk, 0)

        xv = dwc_ref[...]
        mu = jnp.mean(xv, axis=-1, keepdims=True)
        xc = xv - mu
        var = jnp.mean(xc * xc, axis=-1, keepdims=True)
        ln = xc * lax.rsqrt(var + LN_EPS) * lng_ref[...] + lnb_ref[...]
        z_ref[...] = (ln * _sig(ln)).astype(BF16)

    return _call(
        body, name="conv_fwd", grid=(seq // tm,),
        in_specs=[_rows(tm, D_MODEL), _full((1, D_MODEL)), _full((2 * D_MODEL, D_MODEL)), _full((1, 2 * D_MODEL)),
                  _full((TAPS_PAD, D_MODEL)), _full((1, D_MODEL)), _full((1, D_MODEL)), _full((1, D_MODEL))],
        out_specs=[_rows(tm, D_MODEL), _rows(tm, 2 * D_MODEL), _rows(tm, D_MODEL), _rows(tm, D_MODEL)],
        out_shape=[jax.ShapeDtypeStruct((seq, D_MODEL), BF16), jax.ShapeDtypeStruct((seq, 2 * D_MODEL), BF16),
                   jax.ShapeDtypeStruct((seq, D_MODEL), F32), jax.ShapeDtypeStruct((seq, D_MODEL), BF16)],
        scratch=[pltpu.VMEM((tm + CONV_HALO, D_MODEL), F32)],
        sem=("arbitrary",))(h, g, wpw1, bpw1, wd, bdw, lng, lnb)


def _ffn_down_loss(s, gathered, res, target, g):
    seq = s.shape[0]
    tm = min(TOKEN_TILE, seq)

    def body(s_ref, w_ref, r_ref, t_ref, g_ref, dh_ref, dhb_ref, loss_ref, dg_ref):
        @pl.when(pl.program_id(0) == 0)
        def _():
            loss_ref[...] = jnp.zeros_like(loss_ref)
            dg_ref[...] = jnp.zeros_like(dg_ref)

        hv = r_ref[...] + jnp.dot(s_ref[...], w_ref[...].reshape(D_FF, D_MODEL), preferred_element_type=F32)
        gv = g_ref[...]
        err = hv * _rms_stats(hv) * gv - t_ref[...]
        sq = jnp.sum(jnp.sum(err * err, axis=-1, keepdims=True), axis=0, keepdims=True)
        loss_ref[...] += jnp.broadcast_to(sq, loss_ref.shape)
        dx, dg = _rms_bwd(err * (1.0 / D_MODEL), hv, gv, 0.0)
        dh_ref[...] = dx
        dhb_ref[...] = dx.astype(BF16)
        dg_ref[...] += dg

    return _call(
        body, name="ffn1_down_loss", grid=(seq // tm,),
        in_specs=[_rows(tm, D_FF), _ff_whole_spec(W2_SLOT), _rows(tm, D_MODEL), _rows(tm, D_MODEL),
                  _full((1, D_MODEL))],
        out_specs=[_rows(tm, D_MODEL), _rows(tm, D_MODEL), _full((SUBLANES, LANES)), _full((1, D_MODEL))],
        out_shape=[jax.ShapeDtypeStruct((seq, D_MODEL), F32), jax.ShapeDtypeStruct((seq, D_MODEL), BF16),
                   jax.ShapeDtypeStruct((SUBLANES, LANES), F32), jax.ShapeDtypeStruct((1, D_MODEL), F32)],
        sem=("arbitrary",))(s, gathered, res, target, g)


def _ffn_bwd_act(name, dh, u, w, gathered):
    seq = dh.shape[0]
    tm = min(TOKEN_TILE, seq)

    def body(dh_ref, u_ref, w_ref, w2_ref, du_ref, dw_ref):
        ds = _nt(dh_ref[...], w2_ref[...].reshape(FF_TILE, D_MODEL))
        uv = u_ref[...].astype(F32)
        sg = _sig(uv)
        dw_ref[...] = (ds * (uv * sg)).astype(BF16)
        du_ref[...] = (ds * w_ref[...].astype(F32) * (sg * (1.0 + uv * (1.0 - sg)))).astype(BF16)

    tile_ff = pl.BlockSpec((tm, FF_TILE), lambda j, i: (i, j))
    ff_shape = jax.ShapeDtypeStruct((seq, D_FF), BF16)
    return _call(
        body, name=name, grid=(FF_SPLIT, seq // tm),
        in_specs=[pl.BlockSpec((tm, D_MODEL), lambda j, i: (i, 0)), tile_ff, tile_ff, _ff_tile_spec(W2_SLOT)],
        out_specs=[tile_ff, tile_ff], out_shape=[ff_shape, ff_shape],
        sem=("parallel", "parallel"))(dh, u, w, gathered)


def _ffn_bwd_in(name, du, dw, gathered, h_in, dres, g):
    seq = du.shape[0]
    tm = min(TOKEN_TILE, seq)

    def body(du_ref, dw_ref, w1_ref, w3_ref, h_ref, dr_ref, g_ref, dx_ref, dxb_ref, dg_ref):
        @pl.when(pl.program_id(0) == 0)
        def _():
            dg_ref[...] = jnp.zeros_like(dg_ref)

        df = jnp.dot(du_ref[...], w1_ref[...].reshape(D_FF, D_MODEL), preferred_element_type=F32)
        df = df + jnp.dot(dw_ref[...], w3_ref[...].reshape(D_FF, D_MODEL), preferred_element_type=F32)
        dx, dg = _rms_bwd(df, h_ref[...], g_ref[...], dr_ref[...])
        dx_ref[...] = dx
        dxb_ref[...] = dx.astype(BF16)
        dg_ref[...] += dg

    return _call(
        body, name=name, grid=(seq // tm,),
        in_specs=[_rows(tm, D_FF), _rows(tm, D_FF), _ff_whole_spec(W1T_SLOT), _ff_whole_spec(W3T_SLOT),
                  _rows(tm, D_MODEL), _rows(tm, D_MODEL), _full((1, D_MODEL))],
        out_specs=[_rows(tm, D_MODEL), _rows(tm, D_MODEL), _full((1, D_MODEL))],
        out_shape=[jax.ShapeDtypeStruct((seq, D_MODEL), F32), jax.ShapeDtypeStruct((seq, D_MODEL), BF16),
                   jax.ShapeDtypeStruct((1, D_MODEL), F32)],
        sem=("arbitrary",))(du, dw, gathered, gathered, h_in, dres, g)


def _mm_tn(name, a, b, *, tk):
    seq, kdim = a.shape
    n = b.shape[1]
    tt = min((4 if b.dtype == BF16 else 2) * TOKEN_TILE, seq)
    n_t = seq // tt

    def body(a_ref, b_ref, o_ref, acc):
        t = pl.program_id(1)

        @pl.when(t == 0)
        def _():
            acc[...] = jnp.zeros_like(acc)

        acc[...] += _tn(a_ref[...].astype(BF16), b_ref[...].astype(BF16))

        @pl.when(t == n_t - 1)
        def _():
            o_ref[...] = acc[...].astype(BF16)

    return _call(
        body, name=name, grid=(kdim // tk, n_t),
        in_specs=[pl.BlockSpec((tt, tk), lambda k, t: (t, k)), pl.BlockSpec((tt, n), lambda k, t: (t, 0))],
        out_specs=pl.BlockSpec((tk, n), lambda k, t: (k, 0)),
        out_shape=jax.ShapeDtypeStruct((kdim, n), BF16),
        scratch=[pltpu.VMEM((tk, n), F32)],
        sem=("parallel", "arbitrary"))(a, b)


def _conv_bwd(dh, wpw2, dwc, a, lng, lnb, wd):
    seq = dh.shape[0]
    tm = min(TOKEN_TILE, seq)
    nt = seq // tm
    n_chunks = tm // CONV_CHUNK
    win = CONV_CHUNK + CONV_HALO
    halo_per_tile = tm // CONV_HALO

    def body(dh_ref, w_ref, dwc_ref, a_ref, ah_ref, lng_ref, lnb_ref, wd_ref,
             da_ref, dlg_ref, dlb_ref, dbdw_ref, dwd_ref, dbpw1_ref, dbpw2_ref,
             gbuf, dbuf, dglu, dwd_part):
        i = pl.program_id(0)
        r = nt - 1 - i

        @pl.when(i == 0)
        def _():
            dlg_ref[...] = jnp.zeros_like(dlg_ref)
            dlb_ref[...] = jnp.zeros_like(dlb_ref)
            dbdw_ref[...] = jnp.zeros_like(dbdw_ref)
            dbpw1_ref[...] = jnp.zeros_like(dbpw1_ref)
            dbpw2_ref[...] = jnp.zeros_like(dbpw2_ref)
            dwd_part[...] = jnp.zeros_like(dwd_part)
            dbuf[tm:tm + CONV_HALO, :] = jnp.zeros((CONV_HALO, D_MODEL), F32)

        @pl.when(i > 0)
        def _():
            dbuf[tm:tm + CONV_HALO, :] = dbuf[0:CONV_HALO, :]

        dhv = dh_ref[...]
        dbpw2_ref[...] += _sum_rows(dhv)
        dz = _nt(dhv.astype(BF16), w_ref[...])
        xv = dwc_ref[...]
        lg = lng_ref[...]
        mu = jnp.mean(xv, axis=-1, keepdims=True)
        xc = xv - mu
        rstd = lax.rsqrt(jnp.mean(xc * xc, axis=-1, keepdims=True) + LN_EPS)
        xhat = xc * rstd
        ln = xhat * lg + lnb_ref[...]
        sg = _sig(ln)
        dln = dz * (sg * (1.0 + ln * (1.0 - sg)))
        dlg_ref[...] += _sum_rows(dln * xhat)
        dlb_ref[...] += _sum_rows(dln)
        dxh = dln * lg
        ddw = rstd * (dxh - jnp.mean(dxh, axis=-1, keepdims=True)
                      - xhat * jnp.mean(dxh * xhat, axis=-1, keepdims=True))
        dbdw_ref[...] += _sum_rows(ddw)
        dbuf[0:tm, :] = ddw

        av = a_ref[...].astype(F32)
        a1 = av[:, :D_MODEL]
        s2 = _sig(av[:, D_MODEL:])
        gbuf[CONV_HALO:CONV_HALO + tm, :] = a1 * s2
        ah = ah_ref[...].astype(F32)
        gh = ah[:, :D_MODEL] * _sig(ah[:, D_MODEL:])
        gbuf[0:CONV_HALO, :] = jnp.where(r > 0, gh, 0.0)

        def chunk(c, carry):
            r0 = pl.multiple_of(c * CONV_CHUNK, CONV_CHUNK)
            for l in range(D_MODEL // LANES):
                ls = slice(l * LANES, (l + 1) * LANES)
                dw_ = dbuf[pl.ds(r0, win), ls]
                gw = gbuf[pl.ds(r0, win), ls]
                dc = dw_[0:CONV_CHUNK]
                acc = jnp.zeros((CONV_CHUNK, LANES), F32)
                for s in range(SUBLANES):
                    ds_ = dw_ if s == 0 else pltpu.roll(dw_, win - s, 0)
                    gs = gw if s == 0 else pltpu.roll(gw, s, 0)
                    for q in range(CONV_HALO // SUBLANES):
                        d = SUBLANES * q + s
                        if d < CONV_WIDTH:
                            acc = acc + wd_ref[d:d + 1, ls] * ds_[SUBLANES * q:SUBLANES * q + CONV_CHUNK]
                            lo = CONV_HALO - SUBLANES * q
                            prod = dc * gs[lo:lo + CONV_CHUNK]
                            dwd_part[d, :, ls] += jnp.sum(
                                prod.reshape(CONV_CHUNK // SUBLANES, SUBLANES, LANES), axis=0)
                dglu[pl.ds(r0, CONV_CHUNK), ls] = acc
            return carry

        lax.fori_loop(0, n_chunks, chunk, 0)

        dg_ = dglu[...]
        da1 = dg_ * s2
        da2 = dg_ * a1 * s2 * (1.0 - s2)
        da_ref[:, :D_MODEL] = da1.astype(BF16)
        da_ref[:, D_MODEL:] = da2.astype(BF16)
        dbpw1_ref[:, :D_MODEL] += _sum_rows(da1)
        dbpw1_ref[:, D_MODEL:] += _sum_rows(da2)

        @pl.when(i == nt - 1)
        def _():
            dwd_ref[...] = jnp.sum(dwd_part[...], axis=1)

    rev = lambda n: pl.BlockSpec((tm, n), lambda i: (nt - 1 - i, 0))
    halo = pl.BlockSpec((CONV_HALO, 2 * D_MODEL),
                        lambda i: (jnp.maximum((nt - 1 - i) * halo_per_tile - 1, 0), 0))
    vec = lambda n: _full((1, n))
    return _call(
        body, name="conv_bwd", grid=(nt,),
        in_specs=[rev(D_MODEL), _full((D_MODEL, D_MODEL)), rev(D_MODEL), rev(2 * D_MODEL), halo,
                  vec(D_MODEL), vec(D_MODEL), _full((TAPS_PAD, D_MODEL))],
        out_specs=[rev(2 * D_MODEL), vec(D_MODEL), vec(D_MODEL), vec(D_MODEL), _full((TAPS_PAD, D_MODEL)),
                   vec(2 * D_MODEL), vec(D_MODEL)],
        out_shape=[jax.ShapeDtypeStruct((seq, 2 * D_MODEL), BF16), jax.ShapeDtypeStruct((1, D_MODEL), F32),
                   jax.ShapeDtypeStruct((1, D_MODEL), F32), jax.ShapeDtypeStruct((1, D_MODEL), F32),
                   jax.ShapeDtypeStruct((TAPS_PAD, D_MODEL), F32), jax.ShapeDtypeStruct((1, 2 * D_MODEL), F32),
                   jax.ShapeDtypeStruct((1, D_MODEL), F32)],
        scratch=[pltpu.VMEM((tm + CONV_HALO, D_MODEL), F32), pltpu.VMEM((tm + CONV_HALO, D_MODEL), F32),
                 pltpu.VMEM((tm, D_MODEL), F32), pltpu.VMEM((TAPS_PAD, SUBLANES, D_MODEL), F32)],
        sem=("arbitrary",))(dh, wpw2, dwc, a, a, lng, lnb, wd)


def _mm_rms_bwd(name, dact, wt, h_in, dres, g, bf16_copy):
    seq, n = dact.shape
    tm = min(TOKEN_TILE, seq)

    def body(da_ref, w_ref, h_ref, dr_ref, g_ref, dx_ref, *rest):
        dg_ref = rest[-1]

        @pl.when(pl.program_id(0) == 0)
        def _():
            dg_ref[...] = jnp.zeros_like(dg_ref)

        dy = jnp.dot(da_ref[...], w_ref[...], preferred_element_type=F32)
        dx, dg = _rms_bwd(dy, h_ref[...], g_ref[...], dr_ref[...])
        dx_ref[...] = dx
        if bf16_copy:
            rest[0][...] = dx.astype(BF16)
        dg_ref[...] += dg

    copy_spec = [_rows(tm, D_MODEL)] if bf16_copy else []
    copy_shape = [jax.ShapeDtypeStruct((seq, D_MODEL), BF16)] if bf16_copy else []
    return _call(
        body, name=name, grid=(seq // tm,),
        in_specs=[_rows(tm, n), _resident((n, D_MODEL)), _rows(tm, D_MODEL), _rows(tm, D_MODEL), _full((1, D_MODEL))],
        out_specs=[_rows(tm, D_MODEL), *copy_spec, _full((1, D_MODEL))],
        out_shape=[jax.ShapeDtypeStruct((seq, D_MODEL), F32), *copy_shape, jax.ShapeDtypeStruct((1, D_MODEL), F32)],
        sem=("arbitrary",))(dact, wt, h_in, dres, g)


def _nt_bias(name, dy, w):
    seq, n = dy.shape
    kdim = w.shape[0]
    tm = min(2 * TOKEN_TILE, seq)

    def body(dy_ref, w_ref, o_ref, db_ref):
        @pl.when(pl.program_id(0) == 0)
        def _():
            db_ref[...] = jnp.zeros_like(db_ref)

        dyv = dy_ref[...]
        db_ref[...] += _sum_rows(dyv)
        o_ref[...] = _nt(dyv.astype(BF16), w_ref[...]).astype(BF16)

    return _call(
        body, name=name, grid=(seq // tm,),
        in_specs=[_rows(tm, n), _resident((kdim, n))],
        out_specs=[_rows(tm, kdim), _full((1, n))],
        out_shape=[jax.ShapeDtypeStruct((seq, kdim), BF16), jax.ShapeDtypeStruct((1, n), F32)],
        sem=("arbitrary",))(dy, w)


def _attn_bwd(probs, p_sinks, q, k, v, do):
    seq = q.shape[0]
    cur, prev = _kv_specs()

    def body(p_ref, ps_ref, q_ref, kc_ref, kp_ref, vc_ref, vp_ref, do_ref,
             dq_ref, dkc_ref, dkp_ref, dvc_ref, dvp_ref, dsink_ref):
        n = pl.program_id(0)

        @pl.when(n == 0)
        def _():
            dsink_ref[...] = jnp.zeros_like(dsink_ref)

        low = _low_lanes()
        lane = lax.broadcasted_iota(jnp.int32, (BLOCK, LANES), 1)
        p_sinks_blk = ps_ref[...]
        dk_all = jnp.zeros((2 * BLOCK, KV_DIM), F32)
        dv_all = jnp.zeros((2 * BLOCK, KV_DIM), F32)
        for j in range(N_KV_HEADS):
            k_lo, k_hi = _kv_low_high(kp_ref, kc_ref, j, low)
            v_lo, v_hi = _kv_low_high(vp_ref, vc_ref, j, low)
            dk_acc = jnp.zeros((2 * BLOCK, KV_DIM), F32)
            dv_acc = jnp.zeros((2 * BLOCK, KV_DIM), F32)
            for i in range(PAIRS_PER_KV):
                ls, h = _pair_lanes(j, i)
                qp = q_ref[:, ls]
                dop = do_ref[:, ls]
                dsb, pb16 = [], []
                for t, v_sel in enumerate((v_lo, v_hi)):
                    pb = p_ref[h + t]
                    pf = pb.astype(F32)
                    p_sink = jnp.sum(jnp.where(lane == h + t, p_sinks_blk, 0.0), axis=-1, keepdims=True)
                    dp = _nt(dop, v_sel)
                    delta = jnp.sum(pf * dp, axis=-1, keepdims=True)
                    dsb.append((pf * (dp - delta)).astype(BF16))
                    pb16.append(pb)
                    dsink_ref[h + t:h + t + 1, :] += jnp.broadcast_to(_sum_rows(p_sink * delta), (1, LANES))
                dq = (jnp.dot(dsb[0], k_lo, preferred_element_type=F32)
                      + jnp.dot(dsb[1], k_hi, preferred_element_type=F32))
                dq_ref[:, ls] = dq * (HEAD_DIM ** -0.5)
                dk_acc = dk_acc + jnp.where(low, _tn(dsb[0], qp), _tn(dsb[1], qp))
                dv_acc = dv_acc + jnp.where(low, _tn(pb16[0], dop), _tn(pb16[1], dop))
            dk_all = dk_all + _fold_pair_halves(dk_acc, j, low)
            dv_all = dv_all + _fold_pair_halves(dv_acc, j, low)
        dkp_ref[...] = dk_all[:BLOCK]
        dkc_ref[...] = dk_all[BLOCK:]
        dvp_ref[...] = dv_all[:BLOCK]
        dvc_ref[...] = dv_all[BLOCK:]

    kv_out = _rows(BLOCK, KV_DIM)
    kv_shape = jax.ShapeDtypeStruct((seq, KV_DIM), F32)
    return _call(
        body, name="attn_bwd", grid=(seq // BLOCK,),
        in_specs=[pl.BlockSpec((N_Q_HEADS, BLOCK, 2 * BLOCK), lambda n: (0, n, 0)), _rows(BLOCK, LANES),
                  _rows(BLOCK, Q_DIM), cur, prev, cur, prev, _rows(BLOCK, Q_DIM)],
        out_specs=[_rows(BLOCK, Q_DIM), kv_out, kv_out, kv_out, kv_out, _full((N_Q_HEADS, LANES))],
        out_shape=[jax.ShapeDtypeStruct((seq, Q_DIM), F32), kv_shape, kv_shape, kv_shape, kv_shape,
                   jax.ShapeDtypeStruct((N_Q_HEADS, LANES), F32)],
        sem=("arbitrary",))(probs, p_sinks, q, k, k, v, v, do)


def _rope_bwd(dq, dkc, dkp, dvc, dvp, rc, rsa, rsb):
    seq = dq.shape[0]
    nb = seq // BLOCK
    tm = min(TOKEN_TILE, seq)
    nt = seq // tm
    per = tm // BLOCK
    nxt = pl.BlockSpec((BLOCK, KV_DIM), lambda i: (jnp.minimum((i + 1) * per, nb - 1), 0))

    def body(dq_ref, dkc_ref, dkp_ref, dkn_ref, dvc_ref, dvp_ref, dvn_ref, c_ref, sa_ref, sb_ref, o_ref, db_ref):
        i = pl.program_id(0)

        @pl.when(i == 0)
        def _():
            db_ref[...] = jnp.zeros_like(db_ref)

        c, sa, sb = c_ref[...], sa_ref[...], sb_ref[...]
        last = i == nt - 1

        def from_next_block(prev_ref, next_ref):
            tail = jnp.where(last, 0.0, next_ref[...])
            return tail if per == 1 else jnp.concatenate([prev_ref[BLOCK:, :], tail], axis=0)

        dk = dkc_ref[...] + from_next_block(dkp_ref, dkn_ref)
        dv = dvc_ref[...] + from_next_block(dvp_ref, dvn_ref)
        for l in range(Q_DIM // LANES):
            ls = slice(l * LANES, (l + 1) * LANES)
            blk = _rope_t(dq_ref[:, ls], c, sa, sb)
            o_ref[:, ls] = blk.astype(BF16)
            db_ref[:, ls] += _sum_rows(blk)
        dkr = _rope_t(dk, c, sa, sb)
        o_ref[:, Q_DIM:Q_DIM + KV_DIM] = dkr.astype(BF16)
        db_ref[:, Q_DIM:Q_DIM + KV_DIM] += _sum_rows(dkr)
        o_ref[:, Q_DIM + KV_DIM:] = dv.astype(BF16)
        db_ref[:, Q_DIM + KV_DIM:] += _sum_rows(dv)

    kv = _rows(tm, KV_DIM)
    tab = _rows(tm, LANES)
    return _call(
        body, name="rope_bwd", grid=(nt,),
        in_specs=[_rows(tm, Q_DIM), kv, kv, nxt, kv, kv, nxt, tab, tab, tab],
        out_specs=[_rows(tm, QKV_DIM), _full((1, QKV_DIM))],
        out_shape=[jax.ShapeDtypeStruct((seq, QKV_DIM), BF16), jax.ShapeDtypeStruct((1, QKV_DIM), F32)],
        sem=("arbitrary",))(dq, dkc, dkp, dkp, dvc, dvp, dvp, rc, rsa, rsb)


def _adamw(w, g, m, v):
    m = ADAM_B1 * m + (1.0 - ADAM_B1) * g
    v = ADAM_B2 * v + (1.0 - ADAM_B2) * (g * g)
    m_hat = m / (1.0 - ADAM_B1 ** ADAM_STEP)
    v_hat = v / (1.0 - ADAM_B2 ** ADAM_STEP)
    delta = -ADAM_LR * (m_hat / (jnp.sqrt(v_hat) + ADAM_EPS) + ADAM_WD * w)
    return delta, m, v


def _sum_slots(name, parts):
    _, rows, cols = parts.shape
    tr = rows if rows <= 512 else ROWS_FF

    def body(p_ref, g_ref):
        g = p_ref[0].astype(F32)
        for d in range(1, N_DEV):
            g = g + p_ref[d].astype(F32)
        g_ref[...] = g

    return _call(
        body, name=name, grid=(rows // tr,),
        in_specs=[pl.BlockSpec((N_DEV, tr, cols), lambda i: (0, i, 0))],
        out_specs=_rows(tr, cols), out_shape=jax.ShapeDtypeStruct((rows, cols), F32),
        sem=("parallel",))(parts)


def _adamw_native(name, g, w, m, v):
    layers, rows, cols = w.shape
    tr = rows if rows <= 512 else 256

    def body(g_ref, w_ref, m_ref, v_ref, d_ref, nm_ref, nv_ref):
        d_ref[...], nm_ref[...], nv_ref[...] = _adamw(w_ref[...], g_ref[...], m_ref[...], v_ref[...])

    spec = pl.BlockSpec((1, tr, cols), lambda l, i: (l, i, 0))
    shape = jax.ShapeDtypeStruct(w.shape, F32)
    return _call(
        body, name=name, grid=(layers, rows // tr), in_specs=[spec, spec, spec, spec],
        out_specs=[spec, spec, spec], out_shape=[shape, shape, shape],
        sem=("parallel", "parallel"))(g, w, m, v)


def _adamw_replicated(parts, w, m, v):
    def body(p_ref, w_ref, m_ref, v_ref, g_ref, d_ref, nm_ref, nv_ref):
        g = p_ref[0]
        for j in range(1, N_DEV):
            g = g + p_ref[j]
        g_ref[...] = g
        d_ref[...], nm_ref[...], nv_ref[...] = _adamw(w_ref[...], g, m_ref[...], v_ref[...])

    spec = _full((REPL_ROWS, D_MODEL))
    shape = jax.ShapeDtypeStruct((REPL_ROWS, D_MODEL), F32)
    return _call(
        body, name="adamw_replicated", grid=(1,),
        in_specs=[_full((N_DEV, REPL_ROWS, D_MODEL)), spec, spec, spec],
        out_specs=[spec, spec, spec, spec], out_shape=[shape, shape, shape, shape],
        sem=("arbitrary",))(parts, w, m, v)


_MESH = pl.DeviceIdType.MESH
_ANY = pl.BlockSpec(memory_space=pl.ANY)


def _all_gather(name, xs):
    rows, cols = xs.shape

    def body(x_ref, out_ref, send_sems, recv_sems, local_sem):
        x, y, c = lax.axis_index("x"), lax.axis_index("y"), lax.axis_index("c")
        me, sibling = (x, y, c), (x, y, 1 - c)
        chips = [(1 - x, y), (x, 1 - y), (1 - x, 1 - y)]

        def slot(px, py, pc):
            return out_ref.at[4 * px + 2 * py + pc]

        def copy(k, block, to, src=None):
            return pltpu.make_async_remote_copy(
                src_ref=slot(*block) if src is None else src, dst_ref=slot(*block),
                send_sem=send_sems.at[k], recv_sem=recv_sems.at[k], device_id=to, device_id_type=_MESH)

        mine = pltpu.make_async_copy(x_ref, slot(*me), local_sem)
        mine.start()
        first = [copy(0, me, sibling, src=x_ref)]
        first += [copy(1 + j, me, (*chip, c), src=x_ref) for j, chip in enumerate(chips)]
        for cp in first:
            cp.start()
        passed = [copy(4 + j, (*chip, c), sibling) for j, chip in enumerate(chips)]
        for j, chip in enumerate(chips):
            copy(1 + j, (*chip, c), me).wait_recv()
            passed[j].start()
        copy(0, sibling, me).wait_recv()
        for j, chip in enumerate(chips):
            copy(4 + j, (*chip, 1 - c), me).wait_recv()
        for cp in first + passed:
            cp.wait_send()
        mine.wait()

    return pl.pallas_call(
        body, name=name, out_shape=jax.ShapeDtypeStruct((N_DEV, rows, cols), xs.dtype),
        in_specs=[_ANY], out_specs=_ANY,
        scratch_shapes=[pltpu.SemaphoreType.DMA((7,)), pltpu.SemaphoreType.DMA((7,)), pltpu.SemaphoreType.DMA],
    )(xs)


N_PEERS = N_DEV - 1
_HBM = pl.BlockSpec(memory_space=pltpu.HBM)
_SEM = pl.BlockSpec(memory_space=pltpu.SEMAPHORE)
_DATAFLOW = pltpu.SideEffectType.DATAFLOW_SIDE_EFFECTING
_TOKEN = jax.ShapeDtypeStruct((SUBLANES, LANES), F32)


def _peers():
    x, y, c = lax.axis_index("x"), lax.axis_index("y"), lax.axis_index("c")
    out = []
    for k in range(1, N_DEV):
        px = 1 - x if k & 4 else x
        py = 1 - y if k & 2 else y
        pc = 1 - c if k & 1 else c
        out.append(((px, py, pc), 4 * px + 2 * py + pc))
    return 4 * x + 2 * y + c, out


def _in_hbm(a):
    return pltpu.with_memory_space_constraint(a, pltpu.HBM)


def _landing(rows):
    return _in_hbm(lax.empty((N_DEV, rows, D_MODEL), BF16))


def _sem_pair():
    return pltpu.SemaphoreType.DMA((N_PEERS,)), pltpu.SemaphoreType.DMA((N_PEERS,))


def _gather_start(name, payloads):
    n = len(payloads)

    def body(*refs):
        src, land = refs[:n], refs[n:2 * n]
        sems = refs[2 * n:4 * n]
        token = refs[-1]
        me, peers = _peers()
        for g in range(n):
            for k, (pos, _) in enumerate(peers):
                pltpu.make_async_remote_copy(
                    src_ref=src[g], dst_ref=land[g].at[me], send_sem=sems[2 * g].at[k],
                    recv_sem=sems[2 * g + 1].at[k], device_id=pos, device_id_type=_MESH).start()
        token[...] = jnp.zeros_like(token)

    lands = [_landing(p.shape[0]) for p in payloads]
    sem_shapes = [s for _ in payloads for s in _sem_pair()]
    hbm_shapes = [pltpu.HBM(a.shape, a.dtype) for a in list(payloads) + lands]
    out = pl.pallas_call(
        body, name=name, out_shape=(*sem_shapes, *hbm_shapes, _TOKEN),
        in_specs=[_HBM] * (2 * n), out_specs=(*[_SEM] * (2 * n), *[_HBM] * (2 * n), pl.BlockSpec(memory_space=pltpu.VMEM)),
        input_output_aliases={i: 2 * n + i for i in range(2 * n)},
        compiler_params=pltpu.CompilerParams(has_side_effects=_DATAFLOW),
    )(*[_in_hbm(p) for p in payloads], *lands)
    sems, thru = out[:2 * n], out[2 * n:4 * n]
    return [(thru[g], thru[n + g], sems[2 * g], sems[2 * g + 1]) for g in range(n)], out[-1]


def _gather_wait(name, group, after):
    payload, land, send_sems, recv_sems = group

    def body(src_ref, land_ref, send_ref, recv_ref, after_ref, src_out, land_out):
        _, peers = _peers()
        for k, (pos, idx) in enumerate(peers):
            cp = pltpu.make_async_remote_copy(
                src_ref=src_ref, dst_ref=land_ref.at[idx], send_sem=send_ref.at[k], recv_sem=recv_ref.at[k],
                device_id=pos, device_id_type=_MESH)
            cp.wait_send()
            cp.wait_recv()

    _, land = pl.pallas_call(
        body, name=name, out_shape=(pltpu.HBM(payload.shape, payload.dtype), pltpu.HBM(land.shape, land.dtype)),
        in_specs=[_HBM, _HBM, _SEM, _SEM, _ANY], out_specs=(_HBM, _HBM), input_output_aliases={0: 0, 1: 1},
        compiler_params=pltpu.CompilerParams(has_side_effects=_DATAFLOW),
    )(payload, land, send_sems, recv_sems, after)
    me = 4 * lax.axis_index("x") + 2 * lax.axis_index("y") + lax.axis_index("c")
    return lax.dynamic_update_slice(land, payload[None], (me, 0, 0))


def _scatter_start(name, blocks):
    rows = blocks.shape[1]

    def body(blocks_ref, land_ref, send_sems, recv_sems, blocks_out, land_out, token):
        me, peers = _peers()
        for k, (pos, idx) in enumerate(peers):
            pltpu.make_async_remote_copy(
                src_ref=blocks_ref.at[idx], dst_ref=land_ref.at[me], send_sem=send_sems.at[k],
                recv_sem=recv_sems.at[k], device_id=pos, device_id_type=_MESH).start()
        token[...] = jnp.zeros_like(token)

    land = _landing(rows)
    send_sems, recv_sems, blocks_thru, land_thru, token = pl.pallas_call(
        body, name=name,
        out_shape=(*_sem_pair(), pltpu.HBM(blocks.shape, blocks.dtype), pltpu.HBM(land.shape, land.dtype), _TOKEN),
        in_specs=[_HBM, _HBM], out_specs=(_SEM, _SEM, _HBM, _HBM, pl.BlockSpec(memory_space=pltpu.VMEM)),
        input_output_aliases={0: 2, 1: 3},
        compiler_params=pltpu.CompilerParams(has_side_effects=_DATAFLOW),
    )(_in_hbm(blocks), land)
    return (blocks_thru, land_thru, send_sems, recv_sems), token


def _scatter_wait(groups, after):
    n = len(groups)

    def body(*refs):
        blocks, land = refs[:n], refs[n:2 * n]
        sems = refs[2 * n:4 * n]
        _, peers = _peers()
        for g in range(n):
            for k, (pos, idx) in enumerate(peers):
                cp = pltpu.make_async_remote_copy(
                    src_ref=blocks[g].at[idx], dst_ref=land[g].at[idx], send_sem=sems[2 * g].at[k],
                    recv_sem=sems[2 * g + 1].at[k], device_id=pos, device_id_type=_MESH)
                cp.wait_send()
                cp.wait_recv()

    hbm = [grp[0] for grp in groups] + [grp[1] for grp in groups]
    sems = [s for grp in groups for s in grp[2:]]
    out = pl.pallas_call(
        body, name="rs_wait", out_shape=tuple(pltpu.HBM(a.shape, a.dtype) for a in hbm),
        in_specs=[_HBM] * (2 * n) + [_SEM] * (2 * n) + [_ANY], out_specs=tuple([_HBM] * (2 * n)),
        input_output_aliases={i: i for i in range(2 * n)},
        compiler_params=pltpu.CompilerParams(has_side_effects=_DATAFLOW),
    )(*hbm, *sems, after)
    me = 4 * lax.axis_index("x") + 2 * lax.axis_index("y") + lax.axis_index("c")
    lands = []
    for g in range(n):
        own = lax.dynamic_index_in_dim(out[g], me, axis=0, keepdims=True)
        lands.append(lax.dynamic_update_slice(out[n + g], own, (me, 0, 0)))
    return lands


def _pad_rows(flat, rows):
    return jnp.pad(flat, (0, rows * D_MODEL - flat.shape[0])).reshape(rows, D_MODEL)


SMALL_NAMES = ("conv_b_pw1", "conv_w_dw", "conv_b_dw", "conv_ln_g", "conv_ln_b", "conv_b_pw2")


def _pack_small(p):
    flat = jnp.concatenate([p[n].reshape(-1) for n in SMALL_NAMES])
    return _pad_rows(flat, ROWS_SMALL).reshape(1, ROWS_SMALL, D_MODEL)


def _unpack_small(packed):
    flat = packed.reshape(-1)
    c = D_MODEL // N_DEV
    shapes = ((1, 2 * c), (1, CONV_WIDTH, c), (1, c), (1, c), (1, c), (1, c))
    out, o = {}, 0
    for n, shape in zip(SMALL_NAMES, shapes):
        size = shape[-1] * (shape[1] if len(shape) == 3 else 1)
        out[n] = flat[o:o + size].reshape(shape)
        o += size
    return out


def _gather_payloads(p):
    t = lambda a: jnp.swapaxes(a, -1, -2).astype(BF16)
    w1t, w3t, w2 = t(p["ffn_w1"]), t(p["ffn_w3"]), p["ffn_w2"].astype(BF16)
    small = _pack_small(p).reshape(-1)[:ROWS_SMALL * D_MODEL // 2]
    small = lax.bitcast_convert_type(small, BF16).reshape(ROWS_SMALL, D_MODEL)
    conv = jnp.concatenate([t(p["conv_w_pw1"][0]), p["conv_w_pw2"][0].astype(BF16), small], axis=0)
    ffn = [jnp.concatenate([w1t[l], w3t[l], w2[l]], axis=0) for l in range(2)]
    return [t(p["attn_w_qkv"][0]), p["attn_w_o"][0].astype(BF16), ffn[0], conv, ffn[1]]


def _device_rows(land, lo, n):
    return land[:, lo:lo + n].reshape(N_DEV * n, D_MODEL)


def _unpack_conv(land):
    small = lax.bitcast_convert_type(
        land[:, ROWS_PW1 + ROWS_PW2:].reshape(N_DEV, ROWS_SMALL * D_MODEL // 2, 2), F32)
    c = D_MODEL // N_DEV
    b_pw1 = small[:, :2 * c].reshape(1, 2 * D_MODEL)
    s = 2 * c
    w_dw = small[:, s:s + CONV_WIDTH * c].reshape(N_DEV, CONV_WIDTH, c).transpose(1, 0, 2).reshape(CONV_WIDTH, D_MODEL)
    s += CONV_WIDTH * c
    b_dw, ln_g, ln_b, b_pw2 = (small[:, s + i * c:s + (i + 1) * c].reshape(1, D_MODEL) for i in range(4))
    return dict(w_pw1_t=_device_rows(land, 0, ROWS_PW1), w_pw2=_device_rows(land, ROWS_PW1, ROWS_PW2),
                b_pw1=b_pw1, w_dw=w_dw, b_dw=b_dw, ln_g=ln_g, ln_b=ln_b, b_pw2=b_pw2)


def _dest_blocks(mats):
    return jnp.concatenate([a.reshape(N_DEV, -1, D_MODEL) for a in mats], axis=1)


def _small_grad_rows(g_bpw1, g_dw, g_bdw, g_lng, g_lnb, g_bpw2):
    c = D_MODEL // N_DEV
    small = jnp.concatenate(
        [g_bpw1.reshape(N_DEV, 2 * c), g_dw.reshape(CONV_WIDTH, N_DEV, c).transpose(1, 0, 2).reshape(N_DEV, -1),
         g_bdw.reshape(N_DEV, c), g_lng.reshape(N_DEV, c), g_lnb.reshape(N_DEV, c), g_bpw2.reshape(N_DEV, c)], axis=1)
    small = jnp.pad(small, ((0, 0), (0, ROWS_SMALL * D_MODEL - SMALL_USED)))
    return small.reshape(N_DEV * ROWS_SMALL, D_MODEL).astype(BF16)


LOSS_ROW = 9


def _pack_replicated(norm_mix, norm_ffn, b_qkv, sinks, b_o, norm_final, extra=None):
    rows = [norm_mix.reshape(2, D_MODEL), norm_ffn.reshape(2, D_MODEL), _pad_rows(b_qkv.reshape(-1), 2),
            _pad_rows(sinks.reshape(-1), 1), b_o.reshape(1, D_MODEL), norm_final.reshape(1, D_MODEL)]
    if extra is not None:
        rows.append(_pad_rows(extra.reshape(-1), 1))
    p = jnp.concatenate(rows, axis=0)
    return jnp.pad(p, ((0, REPL_ROWS - p.shape[0]), (0, 0)))


def _unpack_replicated(p):
    return dict(norm_mix=p[0:2], norm_ffn=p[2:4], attn_b_qkv=p[4:6].reshape(-1)[:QKV_DIM].reshape(1, QKV_DIM),
                attn_sinks=p[6, :N_Q_HEADS].reshape(1, N_Q_HEADS), attn_b_o=p[7:8], norm_final=p[8])


WEIGHT_ORDER = ['norm_mix', 'norm_ffn', 'attn_w_qkv', 'attn_b_qkv', 'attn_sinks', 'attn_w_o', 'attn_b_o',
                'conv_w_pw1', 'conv_b_pw1', 'conv_w_dw', 'conv_b_dw', 'conv_ln_g', 'conv_ln_b', 'conv_w_pw2',
                'conv_b_pw2', 'ffn_w1', 'ffn_w3', 'ffn_w2', 'norm_final']


def kernel(x, norm_mix, norm_ffn, attn_w_qkv, attn_b_qkv, attn_sinks, attn_w_o, attn_b_o, conv_w_pw1, conv_b_pw1, conv_w_dw, conv_b_dw, conv_ln_g, conv_ln_b, conv_w_pw2, conv_b_pw2, ffn_w1, ffn_w3, ffn_w2, norm_final, loss_target, m_norm_mix, m_norm_ffn, m_attn_w_qkv, m_attn_b_qkv, m_attn_sinks, m_attn_w_o, m_attn_b_o, m_conv_w_pw1, m_conv_b_pw1, m_conv_w_dw, m_conv_b_dw, m_conv_ln_g, m_conv_ln_b, m_conv_w_pw2, m_conv_b_pw2, m_ffn_w1, m_ffn_w3, m_ffn_w2, m_norm_final, v_norm_mix, v_norm_ffn, v_attn_w_qkv, v_attn_b_qkv, v_attn_sinks, v_attn_w_o, v_attn_b_o, v_conv_w_pw1, v_conv_b_pw1, v_conv_w_dw, v_conv_b_dw, v_conv_ln_g, v_conv_ln_b, v_conv_w_pw2, v_conv_b_pw2, v_ffn_w1, v_ffn_w3, v_ffn_w2, v_norm_final):
    xs = x[0]
    target = loss_target[0]
    seq = xs.shape[0]
    my_x, my_y, my_c = lax.axis_index("x"), lax.axis_index("y"), lax.axis_index("c")

    w = dict(attn_w_qkv=attn_w_qkv, attn_w_o=attn_w_o, conv_w_pw1=conv_w_pw1, conv_b_pw1=conv_b_pw1,
             conv_w_dw=conv_w_dw, conv_b_dw=conv_b_dw, conv_ln_g=conv_ln_g, conv_ln_b=conv_ln_b,
             conv_w_pw2=conv_w_pw2, conv_b_pw2=conv_b_pw2, ffn_w1=ffn_w1, ffn_w3=ffn_w3, ffn_w2=ffn_w2)
    m = dict(attn_w_qkv=m_attn_w_qkv, attn_w_o=m_attn_w_o, conv_w_pw1=m_conv_w_pw1, conv_b_pw1=m_conv_b_pw1,
             conv_w_dw=m_conv_w_dw, conv_b_dw=m_conv_b_dw, conv_ln_g=m_conv_ln_g, conv_ln_b=m_conv_ln_b,
             conv_w_pw2=m_conv_w_pw2, conv_b_pw2=m_conv_b_pw2, ffn_w1=m_ffn_w1, ffn_w3=m_ffn_w3, ffn_w2=m_ffn_w2)
    v = dict(attn_w_qkv=v_attn_w_qkv, attn_w_o=v_attn_w_o, conv_w_pw1=v_conv_w_pw1, conv_b_pw1=v_conv_b_pw1,
             conv_w_dw=v_conv_w_dw, conv_b_dw=v_conv_b_dw, conv_ln_g=v_conv_ln_g, conv_ln_b=v_conv_ln_b,
             conv_w_pw2=v_conv_w_pw2, conv_b_pw2=v_conv_b_pw2, ffn_w1=v_ffn_w1, ffn_w3=v_ffn_w3, ffn_w2=v_ffn_w2)

    payloads = _gather_payloads(w)
    (ag_qkv, ag_wo), tok = _gather_start("ag_start_attn", payloads[:2])
    (ag_ffn0, ag_conv, ag_ffn1), tok = _gather_start(
        "ag_start_rest", [payloads[2] + tok[0, 0].astype(BF16), payloads[3], payloads[4]])
    rc, rsa, rsb = _rope_tables(seq)
    sinks = attn_sinks.reshape(N_Q_HEADS)
    g_mix0, g_mix1 = norm_mix[0:1] + tok[0, 0], norm_mix[1:2]
    g_ffn0, g_ffn1 = norm_ffn[0:1], norm_ffn[1:2]
    g_fin = norm_final.reshape(1, D_MODEL)

    w_qkv_t = _gather_wait("ag_wait_qkv", ag_qkv, tok).reshape(QKV_DIM, D_MODEL)
    y0, q, k, vv = _qkv_fwd(xs, g_mix0, w_qkv_t, attn_b_qkv, rc, rsa, rsb)
    attn, probs, p_sinks = _attn_fwd(sinks, q, k, vv)
    w_o = _gather_wait("ag_wait_wo", ag_wo, attn).reshape(Q_DIM, D_MODEL)
    h1 = _mm_res("attn_out_proj", attn, w_o, attn_b_o, xs)
    w_ffn0 = _gather_wait("ag_wait_ffn0", ag_ffn0, h1)
    f0, u0, p0, s0 = _ffn_up("ffn0_up", h1, g_ffn0, w_ffn0)
    h2 = _ffn_down("ffn0_down", s0, w_ffn0, h1)
    wt = _unpack_conv(_gather_wait("ag_wait_conv", ag_conv, h2))
    wd = jnp.concatenate([wt["w_dw"][::-1], jnp.zeros((TAPS_PAD - CONV_WIDTH, D_MODEL), F32)], axis=0)
    y1, a, dwc, z = _conv_fwd(h2, g_mix1, wt["w_pw1_t"], wt["b_pw1"], wd, wt["b_dw"], wt["ln_g"], wt["ln_b"])
    h3 = _mm_res("conv_out_proj", z, wt["w_pw2"], wt["b_pw2"], h2)
    w_ffn1 = _gather_wait("ag_wait_ffn1", ag_ffn1, h3)
    f1, u1, p1, s1 = _ffn_up("ffn1_up", h3, g_ffn1, w_ffn1)
    dh4, dh4b, sq, dg_fin = _ffn_down_loss(s1, w_ffn1, h3, target, g_fin)

    du1, dp1 = _ffn_bwd_act("ffn1_bwd_act", dh4b, u1, p1, w_ffn1)
    dh3, dh3b, dg_ffn1 = _ffn_bwd_in("ffn1_bwd_in", du1, dp1, w_ffn1, h3, dh4, g_ffn1)
    gw2_1 = _mm_tn("ffn1_dw2", s1, dh4b, tk=FF_TILE)
    gw1t_1 = _mm_tn("ffn1_dw1", du1, f1, tk=FF_TILE)
    gw3t_1 = _mm_tn("ffn1_dw3", dp1, f1, tk=FF_TILE)
    rs_ffn1, tok = _scatter_start("rs_start_ffn1", _dest_blocks([gw1t_1, gw3t_1, gw2_1]))

    da, dlg, dlb, dbdw, dwd, dbpw1, dbpw2 = _conv_bwd(dh3, wt["w_pw2"], dwc, a, wt["ln_g"] + tok[0, 0],
                                                     wt["ln_b"], wd)
    gpw2 = _mm_tn("conv_dw_pw2", z, dh3b, tk=D_MODEL)
    dh2, dh2b, dg_mix1 = _mm_rms_bwd("conv_in_bwd", da, wt["w_pw1_t"], h2, dh3, g_mix1, True)
    gpw1t = _mm_tn("conv_dw_pw1", da, y1, tk=D_MODEL)
    small_rows = _small_grad_rows(dbpw1, dwd[:CONV_WIDTH][::-1], dbdw, dlg, dlb, dbpw2)
    rs_conv, tok = _scatter_start("rs_start_conv", _dest_blocks([gpw1t, gpw2, small_rows]))

    du0, dp0 = _ffn_bwd_act("ffn0_bwd_act", dh2b, u0, p0, w_ffn0)
    dh1, dh1b, dg_ffn0 = _ffn_bwd_in("ffn0_bwd_in", du0, dp0, w_ffn0, h1, dh2, g_ffn0 + tok[0, 0])
    gw2_0 = _mm_tn("ffn0_dw2", s0, dh2b, tk=FF_TILE)
    gw1t_0 = _mm_tn("ffn0_dw1", du0, f0, tk=FF_TILE)
    gw3t_0 = _mm_tn("ffn0_dw3", dp0, f0, tk=FF_TILE)
    rs_ffn0, tok = _scatter_start("rs_start_ffn0", _dest_blocks([gw1t_0, gw3t_0, gw2_0]))

    gwo = _mm_tn("attn_dw_o", attn, dh1b, tk=D_MODEL)
    rs_wo, tok2 = _scatter_start("rs_start_wo", _dest_blocks([gwo]))
    dattn, dbo = _nt_bias("attn_out_bwd", dh1, w_o)
    dq, dkc, dkp, dvc, dvp, dsink = _attn_bwd(probs, p_sinks + (tok[0, 0] + tok2[0, 0]), q, k, vv, dattn)
    dqkv, dbqkv = _rope_bwd(dq, dkc, dkp, dvc, dvp, rc, rsa, rsb)
    gqkvt = _mm_tn("attn_dw_qkv", dqkv, y0, tk=QKV_DIM)
    rs_qkv, tok = _scatter_start("rs_start_qkv", _dest_blocks([gqkvt]))
    dx, dg_mix0 = _mm_rms_bwd("qkv_in_bwd", dqkv, w_qkv_t, xs, dh1, g_mix0 + tok[0, 0], False)

    p_ffn1, p_conv, p_ffn0, p_wo, p_qkv = _scatter_wait([rs_ffn1, rs_conv, rs_ffn0, rs_wo, rs_qkv], dx)
    g_ffn = [_sum_slots("rs_sum_ffn0", p_ffn0), _sum_slots("rs_sum_ffn1", p_ffn1)]
    g_conv = _sum_slots("rs_sum_conv", p_conv)
    g_wo = _sum_slots("rs_sum_wo", p_wo)
    g_qkv = _sum_slots("rs_sum_qkv", p_qkv)

    def ff(slot, transpose):
        blk = jnp.stack([g[slot * ROWS_FF:(slot + 1) * ROWS_FF] for g in g_ffn])
        return jnp.swapaxes(blk, 1, 2) if transpose else blk

    grads = dict(
        ffn_w1=ff(W1T_SLOT, True), ffn_w3=ff(W3T_SLOT, True), ffn_w2=ff(W2_SLOT, False),
        attn_w_qkv=g_qkv.T[None], attn_w_o=g_wo[None],
        conv_w_pw1=g_conv[:ROWS_PW1].T[None], conv_w_pw2=g_conv[ROWS_PW1:ROWS_PW1 + ROWS_PW2][None])
    sharded = [dict(grads), {}, {}, {}]
    for n in grads:
        for dst, t in zip(sharded[1:], _adamw_native("adamw_" + n, grads[n], w[n], m[n], v[n])):
            dst[n] = t
    g_small = g_conv[ROWS_PW1 + ROWS_PW2:][None]
    small_out = _adamw_native("adamw_small", g_small, _pack_small(w), _pack_small(m), _pack_small(v))
    for dst, t in zip(sharded, (g_small,) + tuple(small_out)):
        dst.update(_unpack_small(t))

    part = _pack_replicated(jnp.concatenate([dg_mix0, dg_mix1]), jnp.concatenate([dg_ffn0, dg_ffn1]),
                            dbqkv, -dsink[:, 0], dbo, dg_fin, extra=sq[0, 0:1])
    parts = _all_gather("ag_replicated_grads", part)
    w_rep = _pack_replicated(norm_mix, norm_ffn, attn_b_qkv, attn_sinks, attn_b_o, norm_final)
    m_rep = _pack_replicated(m_norm_mix, m_norm_ffn, m_attn_b_qkv, m_attn_sinks, m_attn_b_o, m_norm_final)
    v_rep = _pack_replicated(v_norm_mix, v_norm_ffn, v_attn_b_qkv, v_attn_sinks, v_attn_b_o, v_norm_final)
    rep_out = _adamw_replicated(parts, w_rep, m_rep, v_rep)
    replicated = [_unpack_replicated(t) for t in rep_out]
    loss = rep_out[0][LOSS_ROW, 0] * (0.5 / D_MODEL)

    outs = [loss, dx.reshape(1, seq, D_MODEL)]
    for sh, rp in zip(sharded, replicated):
        merged = {**sh, **rp}
        outs += [merged[n] for n in WEIGHT_ORDER]
    return tuple(outs)
```

```python
import jax
import jax.numpy as jnp
from jax import lax
from jax.experimental import pallas as pl
from jax.experimental.pallas import tpu as pltpu

F32 = jnp.float32
BF16 = jnp.bfloat16

D_MODEL = 1024
HEAD_DIM = 64
N_Q_HEADS = 16
N_KV_HEADS = 2
Q_PER_KV = 8
Q_DIM = N_Q_HEADS * HEAD_DIM
KV_DIM = N_KV_HEADS * HEAD_DIM
QKV_DIM = Q_DIM + 2 * KV_DIM
BLOCK = 128
CONV_WIDTH = 31
D_FF = 2816
ROPE_THETA = 10000.0
RMS_EPS = 1e-5
LN_EPS = 1e-5
ADAM_LR = 0.001
ADAM_B1 = 0.9
ADAM_B2 = 0.999
ADAM_EPS = 1e-08
ADAM_WD = 0.01
ADAM_STEP = 10
N_DEV = 8

LANES = 128
SUBLANES = 8
TOKEN_TILE = 512
CONV_CHUNK = 64
CONV_HALO = 32
TAPS_PAD = 32
VMEM_LIMIT = 56 * 1024 * 1024
NEG_INF = float(jnp.finfo(jnp.float32).min)

ROWS_FF = D_FF // N_DEV
W1T_SLOT, W3T_SLOT, W2_SLOT = 0, 1, 2
ROWS_QKV = QKV_DIM // N_DEV
ROWS_WO = Q_DIM // N_DEV
ROWS_PW1 = 2 * D_MODEL // N_DEV
ROWS_PW2 = D_MODEL // N_DEV
ROWS_SMALL = 16
SMALL_USED = 2 * D_MODEL // N_DEV + CONV_WIDTH * (D_MODEL // N_DEV) + 4 * (D_MODEL // N_DEV)
FF_SPLIT = 2
FF_TILE_DEVS = N_DEV // FF_SPLIT
FF_TILE = FF_TILE_DEVS * ROWS_FF
REPL_ROWS = 16


def _call(body, *, name, grid, in_specs, out_specs, out_shape, scratch=(), sem=None, aliases=None):
    return pl.pallas_call(
        body, name=name, grid=grid, in_specs=in_specs, out_specs=out_specs, out_shape=out_shape,
        scratch_shapes=list(scratch), input_output_aliases=aliases or {},
        compiler_params=pltpu.CompilerParams(dimension_semantics=sem, vmem_limit_bytes=VMEM_LIMIT))


def _full(shape):
    return pl.BlockSpec(shape, lambda *_: (0,) * len(shape))


def _resident(shape):
    return pl.BlockSpec(shape, lambda *_: (0,) * len(shape), pipeline_mode=pl.Buffered(1))


def _rows(tm, n):
    return pl.BlockSpec((tm, n), lambda i, *_: (i, 0))


def _sig(x):
    return 1.0 / (1.0 + jnp.exp(-x))


def _sum_rows(x):
    return jnp.sum(x, axis=0, keepdims=True)


def _nt(a, b):
    return lax.dot_general(a, b, (((1,), (1,)), ((), ())), preferred_element_type=F32)


def _tn(a, b):
    return lax.dot_general(a, b, (((0,), (0,)), ((), ())), preferred_element_type=F32)


def _rms_stats(x):
    return lax.rsqrt(jnp.mean(x * x, axis=-1, keepdims=True) + RMS_EPS)


def _rms_bwd(dy, x, g, dres):
    r = _rms_stats(x)
    n = x * r
    dn = dy * g
    dx = dres + r * (dn - n * jnp.mean(dn * n, axis=-1, keepdims=True))
    return dx, _sum_rows(dy * n)


def _rope_tables(seq):
    pos = jnp.arange(seq, dtype=F32)
    inv_freq = ROPE_THETA ** (-jnp.arange(0, HEAD_DIM, 2, dtype=F32) / HEAD_DIM)
    ang = pos[:, None] * inv_freq[None, :]
    cos, sin = jnp.cos(ang), jnp.sin(ang)
    zero = jnp.zeros_like(sin)
    reps = LANES // HEAD_DIM
    c = jnp.tile(jnp.concatenate([cos, cos], axis=1), (1, reps))
    sa = jnp.tile(jnp.concatenate([-sin, zero], axis=1), (1, reps))
    sb = jnp.tile(jnp.concatenate([zero, sin], axis=1), (1, reps))
    return c, sa, sb


def _rope(t, c, sa, sb):
    half = HEAD_DIM // 2
    return t * c + pltpu.roll(t, LANES - half, 1) * sa + pltpu.roll(t, half, 1) * sb


def _rope_t(dt, c, sa, sb):
    half = HEAD_DIM // 2
    return dt * c + pltpu.roll(dt * sa, half, 1) + pltpu.roll(dt * sb, LANES - half, 1)


def _qkv_fwd(x, g, w, b, rc, rsa, rsb):
    seq = x.shape[0]
    tm = min(TOKEN_TILE, seq)

    def body(x_ref, g_ref, w_ref, b_ref, c_ref, sa_ref, sb_ref, y_ref, q_ref, k_ref, v_ref):
        xv = x_ref[...]
        y = (xv * _rms_stats(xv) * g_ref[...]).astype(BF16)
        y_ref[...] = y
        qkv = _nt(y, w_ref[...]) + b_ref[...]
        c, sa, sb = c_ref[...], sa_ref[...], sb_ref[...]
        for i in range(Q_DIM // LANES):
            blk = _rope(qkv[:, i * LANES:(i + 1) * LANES], c, sa, sb)
            q_ref[:, i * LANES:(i + 1) * LANES] = (blk * (HEAD_DIM ** -0.5)).astype(BF16)
        k_ref[...] = _rope(qkv[:, Q_DIM:Q_DIM + KV_DIM], c, sa, sb).astype(BF16)
        v_ref[...] = qkv[:, Q_DIM + KV_DIM:].astype(BF16)

    return _call(
        body, name="qkv_fwd", grid=(seq // tm,),
        in_specs=[_rows(tm, D_MODEL), _full((1, D_MODEL)), _full((QKV_DIM, D_MODEL)), _full((1, QKV_DIM)),
                  _rows(tm, LANES), _rows(tm, LANES), _rows(tm, LANES)],
        out_specs=[_rows(tm, D_MODEL), _rows(tm, Q_DIM), _rows(tm, KV_DIM), _rows(tm, KV_DIM)],
        out_shape=[jax.ShapeDtypeStruct((seq, D_MODEL), BF16), jax.ShapeDtypeStruct((seq, Q_DIM), BF16),
                   jax.ShapeDtypeStruct((seq, KV_DIM), BF16), jax.ShapeDtypeStruct((seq, KV_DIM), BF16)],
        sem=("parallel",))(x, g, w, b, rc, rsa, rsb)


GROUP_ROWS = Q_PER_KV * BLOCK


def _band_mask(n, rows=GROUP_ROWS):
    row = lax.broadcasted_iota(jnp.int32, (rows, 2 * BLOCK), 0) & (BLOCK - 1)
    col = lax.broadcasted_iota(jnp.int32, (rows, 2 * BLOCK), 1)
    rel = row + BLOCK - col
    return (rel >= 0) & (rel < BLOCK) & ((col >= BLOCK) | (n > 0))


def _softmax_with_sink(s, mask, sink):
    s = jnp.where(mask, s, NEG_INF)
    m = jnp.maximum(jnp.max(s, axis=-1, keepdims=True), sink)
    p = jnp.exp(s - m)
    e_sink = jnp.exp(sink - m)
    inv = 1.0 / (jnp.sum(p, axis=-1, keepdims=True) + e_sink)
    return p * inv, e_sink * inv


PAIRS_PER_KV = Q_PER_KV // 2


def _kv_specs():
    cur = pl.BlockSpec((BLOCK, KV_DIM), lambda n: (n, 0))
    prev = pl.BlockSpec((BLOCK, KV_DIM), lambda n: (jnp.maximum(n - 1, 0), 0))
    return cur, prev


def _low_lanes():
    return lax.broadcasted_iota(jnp.int32, (2 * BLOCK, KV_DIM), 1) < HEAD_DIM


def _kv_low_high(prev_ref, cur_ref, j, low):
    both = jnp.concatenate([prev_ref[...], cur_ref[...]], axis=0).astype(F32)
    swapped = pltpu.roll(both, HEAD_DIM, 1)
    at_low, at_high = (both, swapped) if j == 0 else (swapped, both)
    return jnp.where(low, at_low, 0.0).astype(BF16), jnp.where(low, 0.0, at_high).astype(BF16)


def _fold_pair_halves(acc, j, low):
    folded = acc + pltpu.roll(acc, HEAD_DIM, 1)
    return jnp.where(low, folded, 0.0) if j == 0 else jnp.where(low, 0.0, folded)


def _pair_lanes(j, i):
    g = j * PAIRS_PER_KV + i
    return slice(g * LANES, (g + 1) * LANES), 2 * g


def _attn_fwd(sinks, q, k, v):
    seq = q.shape[0]
    cur, prev = _kv_specs()

    def body(sink_ref, q_ref, kc_ref, kp_ref, vc_ref, vp_ref, o_ref, p_ref, ps_ref):
        mask = _band_mask(pl.program_id(0), BLOCK)
        lane = lax.broadcasted_iota(jnp.int32, (BLOCK, LANES), 1)
        p_sinks = jnp.zeros((BLOCK, LANES), F32)
        for j in range(N_KV_HEADS):
            cs = slice(j * HEAD_DIM, (j + 1) * HEAD_DIM)
            kk = jnp.concatenate([kp_ref[:, cs], kc_ref[:, cs]], axis=0)
            vv = jnp.concatenate([vp_ref[:, cs], vc_ref[:, cs]], axis=0)
            for gq in range(Q_PER_KV):
                h = j * Q_PER_KV + gq
                hs = slice(h * HEAD_DIM, (h + 1) * HEAD_DIM)
                probs, p_sink = _softmax_with_sink(_nt(q_ref[:, hs], kk), mask, sink_ref[h])
                pb = probs.astype(BF16)
                p_ref[h] = pb
                p_sinks = jnp.where(lane == h, p_sink, p_sinks)
                o_ref[:, hs] = jnp.dot(pb, vv, preferred_element_type=F32).astype(BF16)
        ps_ref[...] = p_sinks

    return _call(
        body, name="attn_fwd", grid=(seq // BLOCK,),
        in_specs=[pl.BlockSpec(memory_space=pltpu.SMEM), _rows(BLOCK, Q_DIM), cur, prev, cur, prev],
        out_specs=[_rows(BLOCK, Q_DIM), pl.BlockSpec((N_Q_HEADS, BLOCK, 2 * BLOCK), lambda n: (0, n, 0)),
                   _rows(BLOCK, LANES)],
        out_shape=[jax.ShapeDtypeStruct((seq, Q_DIM), BF16),
                   jax.ShapeDtypeStruct((N_Q_HEADS, seq, 2 * BLOCK), BF16),
                   jax.ShapeDtypeStruct((seq, LANES), F32)],
        sem=("parallel",))(sinks, q, k, k, v, v)


def _mm_res(name, a, w, b, res):
    seq, kdim = a.shape
    n = w.shape[1]
    tm = min(2 * TOKEN_TILE, seq)

    def body(a_ref, w_ref, b_ref, r_ref, o_ref):
        o_ref[...] = r_ref[...] + (jnp.dot(a_ref[...], w_ref[...], preferred_element_type=F32) + b_ref[...])

    return _call(
        body, name=name, grid=(seq // tm,),
        in_specs=[_rows(tm, kdim), _resident((kdim, n)), _full((1, n)), _rows(tm, n)],
        out_specs=_rows(tm, n), out_shape=jax.ShapeDtypeStruct((seq, n), F32),
        sem=("parallel",))(a, w, b, res)


def _ff_tile_spec(slot):
    return pl.BlockSpec((FF_TILE_DEVS, ROWS_FF, D_MODEL), lambda j, i: (j, slot, 0))


def _ff_whole_spec(slot):
    return pl.BlockSpec((N_DEV, ROWS_FF, D_MODEL), lambda i: (0, slot, 0), pipeline_mode=pl.Buffered(1))


def _ffn_up(name, h, g, gathered):
    seq = h.shape[0]
    tm = min(TOKEN_TILE, seq)

    def body(h_ref, g_ref, w1_ref, w3_ref, f_ref, u_ref, w_ref, s_ref):
        hv = h_ref[...]
        f = (hv * _rms_stats(hv) * g_ref[...]).astype(BF16)
        f_ref[...] = f
        u = _nt(f, w1_ref[...].reshape(FF_TILE, D_MODEL))
        w = _nt(f, w3_ref[...].reshape(FF_TILE, D_MODEL))
        u_ref[...] = u.astype(BF16)
        w_ref[...] = w.astype(BF16)
        s_ref[...] = (u * _sig(u) * w).astype(BF16)

    n_t = seq // tm
    tile_ff = pl.BlockSpec((tm, FF_TILE), lambda j, i: (i, j))
    f_spec = pl.BlockSpec((tm, D_MODEL), lambda j, i: (jnp.where(j == 0, i, n_t), 0))
    ff_shape = jax.ShapeDtypeStruct((seq, D_FF), BF16)
    return _call(
        body, name=name, grid=(FF_SPLIT, n_t),
        in_specs=[pl.BlockSpec((tm, D_MODEL), lambda j, i: (i, 0)), _full((1, D_MODEL)),
                  _ff_tile_spec(W1T_SLOT), _ff_tile_spec(W3T_SLOT)],
        out_specs=[f_spec, tile_ff, tile_ff, tile_ff],
        out_shape=[jax.ShapeDtypeStruct((seq + tm, D_MODEL), BF16), ff_shape, ff_shape, ff_shape],
        sem=("arbitrary", "arbitrary"))(h, g, gathered, gathered)


def _ffn_down(name, s, gathered, res):
    seq = s.shape[0]
    tm = min(2 * TOKEN_TILE, seq)

    def body(s_ref, w_ref, r_ref, o_ref):
        w2 = w_ref[...].reshape(D_FF, D_MODEL)
        o_ref[...] = r_ref[...] + jnp.dot(s_ref[...], w2, preferred_element_type=F32)

    return _call(
        body, name=name, grid=(seq // tm,),
        in_specs=[_rows(tm, D_FF), _ff_whole_spec(W2_SLOT), _rows(tm, D_MODEL)],
        out_specs=_rows(tm, D_MODEL), out_shape=jax.ShapeDtypeStruct((seq, D_MODEL), F32),
        sem=("parallel",))(s, gathered, res)


def _conv_fwd(h, g, wpw1, bpw1, wd, bdw, lng, lnb):
    seq = h.shape[0]
    tm = min(TOKEN_TILE, seq)
    n_chunks = tm // CONV_CHUNK
    win = CONV_CHUNK + CONV_HALO

    def body(h_ref, g_ref, w_ref, b_ref, wd_ref, bdw_ref, lng_ref, lnb_ref,
             y_ref, a_ref, dwc_ref, z_ref, gbuf):
        i = pl.program_id(0)

        @pl.when(i == 0)
        def _():
            gbuf[0:CONV_HALO, :] = jnp.zeros((CONV_HALO, D_MODEL), F32)

        @pl.when(i > 0)
        def _():
            gbuf[0:CONV_HALO, :] = gbuf[tm:tm + CONV_HALO, :]

        hv = h_ref[...]
        y = (hv * _rms_stats(hv) * g_ref[...]).astype(BF16)
        y_ref[...] = y
        a = _nt(y, w_ref[...]) + b_ref[...]
        a_ref[...] = a.astype(BF16)
        gbuf[CONV_HALO:CONV_HALO + tm, :] = a[:, :D_MODEL] * _sig(a[:, D_MODEL:])

        def chunk(c, carry):
            r0 = pl.multiple_of(c * CONV_CHUNK, CONV_CHUNK)
            for l in range(D_MODEL // LANES):
                ls = slice(l * LANES, (l + 1) * LANES)
                gw = gbuf[pl.ds(r0, win), ls]
                acc = jnp.zeros((CONV_CHUNK, LANES), F32) + bdw_ref[:, ls]
                for s in range(SUBLANES):
                    gs = gw if s == 0 else pltpu.roll(gw, s, 0)
                    for q in range(CONV_HALO // SUBLANES):
                        d = SUBLANES * q + s
                        if d < CONV_WIDTH:
                            lo = CONV_HALO - SUBLANES * q
                            acc = acc + wd_ref[d:d + 1, ls] * gs[lo:lo + CONV_CHUNK]
                dwc_ref[pl.ds(r0, CONV_CHUNK), ls] = acc
            return carry

        lax.fori_loop(0, n_chunks, chunk, 0)

        xv = dwc_ref[...]
        mu = jnp.mean(xv, axis=-1, keepdims=True)
        xc = xv - mu
        var = jnp.mean(xc * xc, axis=-1, keepdims=True)
        ln = xc * lax.rsqrt(var + LN_EPS) * lng_ref[...] + lnb_ref[...]
        z_ref[...] = (ln * _sig(ln)).astype(BF16)

    return _call(
        body, name="conv_fwd", grid=(seq // tm,),
        in_specs=[_rows(tm, D_MODEL), _full((1, D_MODEL)), _full((2 * D_MODEL, D_MODEL)), _full((1, 2 * D_MODEL)),
                  _full((TAPS_PAD, D_MODEL)), _full((1, D_MODEL)), _full((1, D_MODEL)), _full((1, D_MODEL))],
        out_specs=[_rows(tm, D_MODEL), _rows(tm, 2 * D_MODEL), _rows(tm, D_MODEL), _rows(tm, D_MODEL)],
        out_shape=[jax.ShapeDtypeStruct((seq, D_MODEL), BF16), jax.ShapeDtypeStruct((seq, 2 * D_MODEL), BF16),
                   jax.ShapeDtypeStruct((seq, D_MODEL), F32), jax.ShapeDtypeStruct((seq, D_MODEL), BF16)],
        scratch=[pltpu.VMEM((tm + CONV_HALO, D_MODEL), F32)],
        sem=("arbitrary",))(h, g, wpw1, bpw1, wd, bdw, lng, lnb)


def _ffn_down_loss(s, gathered, res, target, g):
    seq = s.shape[0]
    tm = min(TOKEN_TILE, seq)

    def body(s_ref, w_ref, r_ref, t_ref, g_ref, dh_ref, dhb_ref, loss_ref, dg_ref):
        @pl.when(pl.program_id(0) == 0)
        def _():
            loss_ref[...] = jnp.zeros_like(loss_ref)
            dg_ref[...] = jnp.zeros_like(dg_ref)

        hv = r_ref[...] + jnp.dot(s_ref[...], w_ref[...].reshape(D_FF, D_MODEL), preferred_element_type=F32)
        gv = g_ref[...]
        err = hv * _rms_stats(hv) * gv - t_ref[...]
        sq = jnp.sum(jnp.sum(err * err, axis=-1, keepdims=True), axis=0, keepdims=True)
        loss_ref[...] += jnp.broadcast_to(sq, loss_ref.shape)
        dx, dg = _rms_bwd(err * (1.0 / D_MODEL), hv, gv, 0.0)
        dh_ref[...] = dx
        dhb_ref[...] = dx.astype(BF16)
        dg_ref[...] += dg

    return _call(
        body, name="ffn1_down_loss", grid=(seq // tm,),
        in_specs=[_rows(tm, D_FF), _ff_whole_spec(W2_SLOT), _rows(tm, D_MODEL), _rows(tm, D_MODEL),
                  _full((1, D_MODEL))],
        out_specs=[_rows(tm, D_MODEL), _rows(tm, D_MODEL), _full((SUBLANES, LANES)), _full((1, D_MODEL))],
        out_shape=[jax.ShapeDtypeStruct((seq, D_MODEL), F32), jax.ShapeDtypeStruct((seq, D_MODEL), BF16),
                   jax.ShapeDtypeStruct((SUBLANES, LANES), F32), jax.ShapeDtypeStruct((1, D_MODEL), F32)],
        sem=("arbitrary",))(s, gathered, res, target, g)


def _ffn_bwd_act(name, dh, u, w, gathered):
    seq = dh.shape[0]
    tm = min(TOKEN_TILE, seq)

    def body(dh_ref, u_ref, w_ref, w2_ref, du_ref, dw_ref):
        ds = _nt(dh_ref[...], w2_ref[...].reshape(FF_TILE, D_MODEL))
        uv = u_ref[...].astype(F32)
        sg = _sig(uv)
        dw_ref[...] = (ds * (uv * sg)).astype(BF16)
        du_ref[...] = (ds * w_ref[...].astype(F32) * (sg * (1.0 + uv * (1.0 - sg)))).astype(BF16)

    tile_ff = pl.BlockSpec((tm, FF_TILE), lambda j, i: (i, j))
    ff_shape = jax.ShapeDtypeStruct((seq, D_FF), BF16)
    return _call(
        body, name=name, grid=(FF_SPLIT, seq // tm),
        in_specs=[pl.BlockSpec((tm, D_MODEL), lambda j, i: (i, 0)), tile_ff, tile_ff, _ff_tile_spec(W2_SLOT)],
        out_specs=[tile_ff, tile_ff], out_shape=[ff_shape, ff_shape],
        sem=("parallel", "parallel"))(dh, u, w, gathered)


def _ffn_bwd_in(name, du, dw, gathered, h_in, dres, g):
    seq = du.shape[0]
    tm = min(TOKEN_TILE, seq)

    def body(du_ref, dw_ref, w1_ref, w3_ref, h_ref, dr_ref, g_ref, dx_ref, dxb_ref, dg_ref):
        @pl.when(pl.program_id(0) == 0)
        def _():
            dg_ref[...] = jnp.zeros_like(dg_ref)

        df = jnp.dot(du_ref[...], w1_ref[...].reshape(D_FF, D_MODEL), preferred_element_type=F32)
        df = df + jnp.dot(dw_ref[...], w3_ref[...].reshape(D_FF, D_MODEL), preferred_element_type=F32)
        dx, dg = _rms_bwd(df, h_ref[...], g_ref[...], dr_ref[...])
        dx_ref[...] = dx
        dxb_ref[...] = dx.astype(BF16)
        dg_ref[...] += dg

    return _call(
        body, name=name, grid=(seq // tm,),
        in_specs=[_rows(tm, D_FF), _rows(tm, D_FF), _ff_whole_spec(W1T_SLOT), _ff_whole_spec(W3T_SLOT),
                  _rows(tm, D_MODEL), _rows(tm, D_MODEL), _full((1, D_MODEL))],
        out_specs=[_rows(tm, D_MODEL), _rows(tm, D_MODEL), _full((1, D_MODEL))],
        out_shape=[jax.ShapeDtypeStruct((seq, D_MODEL), F32), jax.ShapeDtypeStruct((seq, D_MODEL), BF16),
                   jax.ShapeDtypeStruct((1, D_MODEL), F32)],
        sem=("arbitrary",))(du, dw, gathered, gathered, h_in, dres, g)


def _mm_tn(name, a, b, *, tk, into=None):
    seq, kdim = a.shape
    n = b.shape[1]
    tt = min((4 if b.dtype == BF16 else 2) * TOKEN_TILE, seq)
    n_t = seq // tt
    devs = tk // ROWS_FF

    def body(a_ref, b_ref, *rest):
        o_ref, acc = rest[-2:]
        t = pl.program_id(1)

        @pl.when(t == 0)
        def _():
            acc[...] = jnp.zeros_like(acc)

        acc[...] += _tn(a_ref[...].astype(BF16), b_ref[...].astype(BF16))

        @pl.when(t == n_t - 1)
        def _():
            out = acc[...].astype(BF16)
            o_ref[...] = out if into is None else out.reshape(devs, ROWS_FF, n)

    in_specs = [pl.BlockSpec((tt, tk), lambda k, t: (t, k)), pl.BlockSpec((tt, n), lambda k, t: (t, 0))]
    args = [a, b]
    aliases = None
    if into is None:
        out_spec = pl.BlockSpec((tk, n), lambda k, t: (k, 0))
        out_shape = jax.ShapeDtypeStruct((kdim, n), BF16)
    else:
        blocks, slot = into
        out_spec = pl.BlockSpec((devs, ROWS_FF, n), lambda k, t: (k, slot, 0))
        out_shape = jax.ShapeDtypeStruct((N_DEV, 3 * ROWS_FF, n), BF16)
        if blocks is not None:
            in_specs.append(_ANY)
            args.append(blocks)
            aliases = {2: 0}
    return _call(
        body, name=name, grid=(kdim // tk, n_t), in_specs=in_specs, out_specs=out_spec, out_shape=out_shape,
        scratch=[pltpu.VMEM((tk, n), F32)], sem=("parallel", "arbitrary"), aliases=aliases)(*args)


def _conv_bwd(dh, wpw2, dwc, a, lng, lnb, wd):
    seq = dh.shape[0]
    tm = min(TOKEN_TILE, seq)
    nt = seq // tm
    n_chunks = tm // CONV_CHUNK
    win = CONV_CHUNK + CONV_HALO
    halo_per_tile = tm // CONV_HALO

    def body(dh_ref, w_ref, dwc_ref, a_ref, ah_ref, lng_ref, lnb_ref, wd_ref,
             da_ref, dlg_ref, dlb_ref, dbdw_ref, dwd_ref, dbpw1_ref, dbpw2_ref,
             gbuf, dbuf, dglu, dwd_part):
        i = pl.program_id(0)
        r = nt - 1 - i

        @pl.when(i == 0)
        def _():
            dlg_ref[...] = jnp.zeros_like(dlg_ref)
            dlb_ref[...] = jnp.zeros_like(dlb_ref)
            dbdw_ref[...] = jnp.zeros_like(dbdw_ref)
            dbpw1_ref[...] = jnp.zeros_like(dbpw1_ref)
            dbpw2_ref[...] = jnp.zeros_like(dbpw2_ref)
            dwd_part[...] = jnp.zeros_like(dwd_part)
            dbuf[tm:tm + CONV_HALO, :] = jnp.zeros((CONV_HALO, D_MODEL), F32)

        @pl.when(i > 0)
        def _():
            dbuf[tm:tm + CONV_HALO, :] = dbuf[0:CONV_HALO, :]

        dhv = dh_ref[...]
        dbpw2_ref[...] += _sum_rows(dhv)
        dz = _nt(dhv.astype(BF16), w_ref[...])
        xv = dwc_ref[...]
        lg = lng_ref[...]
        mu = jnp.mean(xv, axis=-1, keepdims=True)
        xc = xv - mu
        rstd = lax.rsqrt(jnp.mean(xc * xc, axis=-1, keepdims=True) + LN_EPS)
        xhat = xc * rstd
        ln = xhat * lg + lnb_ref[...]
        sg = _sig(ln)
        dln = dz * (sg * (1.0 + ln * (1.0 - sg)))
        dlg_ref[...] += _sum_rows(dln * xhat)
        dlb_ref[...] += _sum_rows(dln)
        dxh = dln * lg
        ddw = rstd * (dxh - jnp.mean(dxh, axis=-1, keepdims=True)
                      - xhat * jnp.mean(dxh * xhat, axis=-1, keepdims=True))
        dbdw_ref[...] += _sum_rows(ddw)
        dbuf[0:tm, :] = ddw

        av = a_ref[...].astype(F32)
        a1 = av[:, :D_MODEL]
        s2 = _sig(av[:, D_MODEL:])
        gbuf[CONV_HALO:CONV_HALO + tm, :] = a1 * s2
        ah = ah_ref[...].astype(F32)
        gh = ah[:, :D_MODEL] * _sig(ah[:, D_MODEL:])
        gbuf[0:CONV_HALO, :] = jnp.where(r > 0, gh, 0.0)

        def chunk(c, carry):
            r0 = pl.multiple_of(c * CONV_CHUNK, CONV_CHUNK)
            for l in range(D_MODEL // LANES):
                ls = slice(l * LANES, (l + 1) * LANES)
                dw_ = dbuf[pl.ds(r0, win), ls]
                gw = gbuf[pl.ds(r0, win), ls]
                acc = jnp.zeros((CONV_CHUNK, LANES), F32)
                for s in range(SUBLANES):
                    ds_ = dw_ if s == 0 else pltpu.roll(dw_, win - s, 0)
                    for q in range(CONV_HALO // SUBLANES):
                        d = SUBLANES * q + s
                        if d < CONV_WIDTH:
                            acc = acc + wd_ref[d:d + 1, ls] * ds_[SUBLANES * q:SUBLANES * q + CONV_CHUNK]
                            lo = CONV_HALO - SUBLANES * q
                            prod = ds_[0:CONV_CHUNK] * gw[lo:lo + CONV_CHUNK]
                            dwd_part[d, :, ls] += jnp.sum(
                                prod.reshape(CONV_CHUNK // SUBLANES, SUBLANES, LANES), axis=0)
                dglu[pl.ds(r0, CONV_CHUNK), ls] = acc
            return carry

        lax.fori_loop(0, n_chunks, chunk, 0)

        dg_ = dglu[...]
        da1 = dg_ * s2
        da2 = dg_ * a1 * s2 * (1.0 - s2)
        da_ref[:, :D_MODEL] = da1.astype(BF16)
        da_ref[:, D_MODEL:] = da2.astype(BF16)
        dbpw1_ref[:, :D_MODEL] += _sum_rows(da1)
        dbpw1_ref[:, D_MODEL:] += _sum_rows(da2)

        @pl.when(i == nt - 1)
        def _():
            dwd_ref[...] = jnp.sum(dwd_part[...], axis=1)

    rev = lambda n: pl.BlockSpec((tm, n), lambda i: (nt - 1 - i, 0))
    halo = pl.BlockSpec((CONV_HALO, 2 * D_MODEL),
                        lambda i: (jnp.maximum((nt - 1 - i) * halo_per_tile - 1, 0), 0))
    vec = lambda n: _full((1, n))
    return _call(
        body, name="conv_bwd", grid=(nt,),
        in_specs=[rev(D_MODEL), _full((D_MODEL, D_MODEL)), rev(D_MODEL), rev(2 * D_MODEL), halo,
                  vec(D_MODEL), vec(D_MODEL), _full((TAPS_PAD, D_MODEL))],
        out_specs=[rev(2 * D_MODEL), vec(D_MODEL), vec(D_MODEL), vec(D_MODEL), _full((TAPS_PAD, D_MODEL)),
                   vec(2 * D_MODEL), vec(D_MODEL)],
        out_shape=[jax.ShapeDtypeStruct((seq, 2 * D_MODEL), BF16), jax.ShapeDtypeStruct((1, D_MODEL), F32),
                   jax.ShapeDtypeStruct((1, D_MODEL), F32), jax.ShapeDtypeStruct((1, D_MODEL), F32),
                   jax.ShapeDtypeStruct((TAPS_PAD, D_MODEL), F32), jax.ShapeDtypeStruct((1, 2 * D_MODEL), F32),
                   jax.ShapeDtypeStruct((1, D_MODEL), F32)],
        scratch=[pltpu.VMEM((tm + CONV_HALO, D_MODEL), F32), pltpu.VMEM((tm + CONV_HALO, D_MODEL), F32),
                 pltpu.VMEM((tm, D_MODEL), F32), pltpu.VMEM((TAPS_PAD, SUBLANES, D_MODEL), F32)],
        sem=("arbitrary",))(dh, wpw2, dwc, a, a, lng, lnb, wd)


def _mm_rms_bwd(name, dact, wt, h_in, dres, g, bf16_copy):
    seq, n = dact.shape
    tm = min(TOKEN_TILE, seq)

    def body(da_ref, w_ref, h_ref, dr_ref, g_ref, dx_ref, *rest):
        dg_ref = rest[-1]

        @pl.when(pl.program_id(0) == 0)
        def _():
            dg_ref[...] = jnp.zeros_like(dg_ref)

        dy = jnp.dot(da_ref[...], w_ref[...], preferred_element_type=F32)
        dx, dg = _rms_bwd(dy, h_ref[...], g_ref[...], dr_ref[...])
        dx_ref[...] = dx
        if bf16_copy:
            rest[0][...] = dx.astype(BF16)
        dg_ref[...] += dg

    copy_spec = [_rows(tm, D_MODEL)] if bf16_copy else []
    copy_shape = [jax.ShapeDtypeStruct((seq, D_MODEL), BF16)] if bf16_copy else []
    return _call(
        body, name=name, grid=(seq // tm,),
        in_specs=[_rows(tm, n), _resident((n, D_MODEL)), _rows(tm, D_MODEL), _rows(tm, D_MODEL), _full((1, D_MODEL))],
        out_specs=[_rows(tm, D_MODEL), *copy_spec, _full((1, D_MODEL))],
        out_shape=[jax.ShapeDtypeStruct((seq, D_MODEL), F32), *copy_shape, jax.ShapeDtypeStruct((1, D_MODEL), F32)],
        sem=("arbitrary",))(dact, wt, h_in, dres, g)


def _nt_bias(name, dy, w):
    seq, n = dy.shape
    kdim = w.shape[0]
    tm = min(2 * TOKEN_TILE, seq)

    def body(dy_ref, w_ref, o_ref, db_ref):
        @pl.when(pl.program_id(0) == 0)
        def _():
            db_ref[...] = jnp.zeros_like(db_ref)

        dyv = dy_ref[...]
        db_ref[...] += _sum_rows(dyv)
        o_ref[...] = _nt(dyv.astype(BF16), w_ref[...]).astype(BF16)

    return _call(
        body, name=name, grid=(seq // tm,),
        in_specs=[_rows(tm, n), _resident((kdim, n))],
        out_specs=[_rows(tm, kdim), _full((1, n))],
        out_shape=[jax.ShapeDtypeStruct((seq, kdim), BF16), jax.ShapeDtypeStruct((1, n), F32)],
        sem=("arbitrary",))(dy, w)


def _attn_bwd(probs, p_sinks, q, k, v, do):
    seq = q.shape[0]
    cur, prev = _kv_specs()

    def body(p_ref, ps_ref, q_ref, kc_ref, kp_ref, vc_ref, vp_ref, do_ref,
             dq_ref, dkc_ref, dkp_ref, dvc_ref, dvp_ref, dsink_ref):
        n = pl.program_id(0)

        @pl.when(n == 0)
        def _():
            dsink_ref[...] = jnp.zeros_like(dsink_ref)

        low = _low_lanes()
        lane = lax.broadcasted_iota(jnp.int32, (BLOCK, LANES), 1)
        p_sinks_blk = ps_ref[...]
        dk_all = jnp.zeros((2 * BLOCK, KV_DIM), F32)
        dv_all = jnp.zeros((2 * BLOCK, KV_DIM), F32)
        for j in range(N_KV_HEADS):
            k_lo, k_hi = _kv_low_high(kp_ref, kc_ref, j, low)
            v_lo, v_hi = _kv_low_high(vp_ref, vc_ref, j, low)
            dk_acc = jnp.zeros((2 * BLOCK, KV_DIM), F32)
            dv_acc = jnp.zeros((2 * BLOCK, KV_DIM), F32)
            for i in range(PAIRS_PER_KV):
                ls, h = _pair_lanes(j, i)
                qp = q_ref[:, ls]
                dop = do_ref[:, ls]
                dsb, pb16 = [], []
                for t, v_sel in enumerate((v_lo, v_hi)):
                    pb = p_ref[h + t]
                    pf = pb.astype(F32)
                    p_sink = jnp.sum(jnp.where(lane == h + t, p_sinks_blk, 0.0), axis=-1, keepdims=True)
                    dp = _nt(dop, v_sel)
                    delta = jnp.sum(pf * dp, axis=-1, keepdims=True)
                    dsb.append((pf * (dp - delta)).astype(BF16))
                    pb16.append(pb)
                    dsink_ref[h + t:h + t + 1, :] += jnp.broadcast_to(_sum_rows(p_sink * delta), (1, LANES))
                dq = (jnp.dot(dsb[0], k_lo, preferred_element_type=F32)
                      + jnp.dot(dsb[1], k_hi, preferred_element_type=F32))
                dq_ref[:, ls] = dq * (HEAD_DIM ** -0.5)
                dk_acc = dk_acc + jnp.where(low, _tn(dsb[0], qp), _tn(dsb[1], qp))
                dv_acc = dv_acc + jnp.where(low, _tn(pb16[0], dop), _tn(pb16[1], dop))
            dk_all = dk_all + _fold_pair_halves(dk_acc, j, low)
            dv_all = dv_all + _fold_pair_halves(dv_acc, j, low)
        dkp_ref[...] = dk_all[:BLOCK]
        dkc_ref[...] = dk_all[BLOCK:]
        dvp_ref[...] = dv_all[:BLOCK]
        dvc_ref[...] = dv_all[BLOCK:]

    kv_out = _rows(BLOCK, KV_DIM)
    kv_shape = jax.ShapeDtypeStruct((seq, KV_DIM), F32)
    return _call(
        body, name="attn_bwd", grid=(seq // BLOCK,),
        in_specs=[pl.BlockSpec((N_Q_HEADS, BLOCK, 2 * BLOCK), lambda n: (0, n, 0)), _rows(BLOCK, LANES),
                  _rows(BLOCK, Q_DIM), cur, prev, cur, prev, _rows(BLOCK, Q_DIM)],
        out_specs=[_rows(BLOCK, Q_DIM), kv_out, kv_out, kv_out, kv_out, _full((N_Q_HEADS, LANES))],
        out_shape=[jax.ShapeDtypeStruct((seq, Q_DIM), F32), kv_shape, kv_shape, kv_shape, kv_shape,
                   jax.ShapeDtypeStruct((N_Q_HEADS, LANES), F32)],
        sem=("arbitrary",))(probs, p_sinks, q, k, k, v, v, do)


def _rope_bwd(dq, dkc, dkp, dvc, dvp, rc, rsa, rsb):
    seq = dq.shape[0]
    nb = seq // BLOCK
    tm = min(TOKEN_TILE, seq)
    nt = seq // tm
    per = tm // BLOCK
    nxt = pl.BlockSpec((BLOCK, KV_DIM), lambda i: (jnp.minimum((i + 1) * per, nb - 1), 0))

    def body(dq_ref, dkc_ref, dkp_ref, dkn_ref, dvc_ref, dvp_ref, dvn_ref, c_ref, sa_ref, sb_ref, o_ref, db_ref):
        i = pl.program_id(0)

        @pl.when(i == 0)
        def _():
            db_ref[...] = jnp.zeros_like(db_ref)

        c, sa, sb = c_ref[...], sa_ref[...], sb_ref[...]
        last = i == nt - 1

        def from_next_block(prev_ref, next_ref):
            tail = jnp.where(last, 0.0, next_ref[...])
            return tail if per == 1 else jnp.concatenate([prev_ref[BLOCK:, :], tail], axis=0)

        dk = dkc_ref[...] + from_next_block(dkp_ref, dkn_ref)
        dv = dvc_ref[...] + from_next_block(dvp_ref, dvn_ref)
        for l in range(Q_DIM // LANES):
            ls = slice(l * LANES, (l + 1) * LANES)
            blk = _rope_t(dq_ref[:, ls], c, sa, sb)
            o_ref[:, ls] = blk.astype(BF16)
            db_ref[:, ls] += _sum_rows(blk)
        dkr = _rope_t(dk, c, sa, sb)
        o_ref[:, Q_DIM:Q_DIM + KV_DIM] = dkr.astype(BF16)
        db_ref[:, Q_DIM:Q_DIM + KV_DIM] += _sum_rows(dkr)
        o_ref[:, Q_DIM + KV_DIM:] = dv.astype(BF16)
        db_ref[:, Q_DIM + KV_DIM:] += _sum_rows(dv)

    kv = _rows(tm, KV_DIM)
    tab = _rows(tm, LANES)
    return _call(
        body, name="rope_bwd", grid=(nt,),
        in_specs=[_rows(tm, Q_DIM), kv, kv, nxt, kv, kv, nxt, tab, tab, tab],
        out_specs=[_rows(tm, QKV_DIM), _full((1, QKV_DIM))],
        out_shape=[jax.ShapeDtypeStruct((seq, QKV_DIM), BF16), jax.ShapeDtypeStruct((1, QKV_DIM), F32)],
        sem=("arbitrary",))(dq, dkc, dkp, dkp, dvc, dvp, dvp, rc, rsa, rsb)


def _adamw(w, g, m, v):
    m = ADAM_B1 * m + (1.0 - ADAM_B1) * g
    v = ADAM_B2 * v + (1.0 - ADAM_B2) * (g * g)
    m_hat = m / (1.0 - ADAM_B1 ** ADAM_STEP)
    v_hat = v / (1.0 - ADAM_B2 ** ADAM_STEP)
    delta = -ADAM_LR * (m_hat / (jnp.sqrt(v_hat) + ADAM_EPS) + ADAM_WD * w)
    return delta, m, v


def _sum_slots(name, parts):
    _, rows, cols = parts.shape
    tr = rows if rows <= 512 else ROWS_FF

    def body(p_ref, g_ref):
        g = p_ref[0].astype(F32)
        for d in range(1, N_DEV):
            g = g + p_ref[d].astype(F32)
        g_ref[...] = g

    return _call(
        body, name=name, grid=(rows // tr,),
        in_specs=[pl.BlockSpec((N_DEV, tr, cols), lambda i: (0, i, 0))],
        out_specs=_rows(tr, cols), out_shape=jax.ShapeDtypeStruct((rows, cols), F32),
        sem=("parallel",))(parts)


def _adamw_native(name, g, w, m, v):
    layers, rows, cols = w.shape
    tr = rows if rows <= 512 else 256

    def body(g_ref, w_ref, m_ref, v_ref, d_ref, nm_ref, nv_ref):
        d_ref[...], nm_ref[...], nv_ref[...] = _adamw(w_ref[...], g_ref[...], m_ref[...], v_ref[...])

    spec = pl.BlockSpec((1, tr, cols), lambda l, i: (l, i, 0))
    shape = jax.ShapeDtypeStruct(w.shape, F32)
    return _call(
        body, name=name, grid=(layers, rows // tr), in_specs=[spec, spec, spec, spec],
        out_specs=[spec, spec, spec], out_shape=[shape, shape, shape],
        sem=("parallel", "parallel"))(g, w, m, v)


def _adamw_replicated(parts, w, m, v):
    def body(p_ref, w_ref, m_ref, v_ref, g_ref, d_ref, nm_ref, nv_ref):
        g = p_ref[0]
        for j in range(1, N_DEV):
            g = g + p_ref[j]
        g_ref[...] = g
        d_ref[...], nm_ref[...], nv_ref[...] = _adamw(w_ref[...], g, m_ref[...], v_ref[...])

    spec = _full((REPL_ROWS, D_MODEL))
    shape = jax.ShapeDtypeStruct((REPL_ROWS, D_MODEL), F32)
    return _call(
        body, name="adamw_replicated", grid=(1,),
        in_specs=[_full((N_DEV, REPL_ROWS, D_MODEL)), spec, spec, spec],
        out_specs=[spec, spec, spec, spec], out_shape=[shape, shape, shape, shape],
        sem=("arbitrary",))(parts, w, m, v)


_MESH = pl.DeviceIdType.MESH
_ANY = pl.BlockSpec(memory_space=pl.ANY)


def _all_gather(name, xs):
    rows, cols = xs.shape

    def body(x_ref, out_ref, send_sems, recv_sems, local_sem):
        x, y, c = lax.axis_index("x"), lax.axis_index("y"), lax.axis_index("c")
        me, sibling = (x, y, c), (x, y, 1 - c)
        chips = [(1 - x, y), (x, 1 - y), (1 - x, 1 - y)]

        def slot(px, py, pc):
            return out_ref.at[4 * px + 2 * py + pc]

        def copy(k, block, to, src=None):
            return pltpu.make_async_remote_copy(
                src_ref=slot(*block) if src is None else src, dst_ref=slot(*block),
                send_sem=send_sems.at[k], recv_sem=recv_sems.at[k], device_id=to, device_id_type=_MESH)

        mine = pltpu.make_async_copy(x_ref, slot(*me), local_sem)
        mine.start()
        first = [copy(0, me, sibling, src=x_ref)]
        first += [copy(1 + j, me, (*chip, c), src=x_ref) for j, chip in enumerate(chips)]
        for cp in first:
            cp.start()
        passed = [copy(4 + j, (*chip, c), sibling) for j, chip in enumerate(chips)]
        for j, chip in enumerate(chips):
            copy(1 + j, (*chip, c), me).wait_recv()
            passed[j].start()
        copy(0, sibling, me).wait_recv()
        for j, chip in enumerate(chips):
            copy(4 + j, (*chip, 1 - c), me).wait_recv()
        for cp in first + passed:
            cp.wait_send()
        mine.wait()

    return pl.pallas_call(
        body, name=name, out_shape=jax.ShapeDtypeStruct((N_DEV, rows, cols), xs.dtype),
        in_specs=[_ANY], out_specs=_ANY,
        scratch_shapes=[pltpu.SemaphoreType.DMA((7,)), pltpu.SemaphoreType.DMA((7,)), pltpu.SemaphoreType.DMA],
    )(xs)


N_PEERS = N_DEV - 1
_HBM = pl.BlockSpec(memory_space=pltpu.HBM)
_SEM = pl.BlockSpec(memory_space=pltpu.SEMAPHORE)
_DATAFLOW = pltpu.SideEffectType.DATAFLOW_SIDE_EFFECTING
_TOKEN = jax.ShapeDtypeStruct((SUBLANES, LANES), F32)


def _peers():
    x, y, c = lax.axis_index("x"), lax.axis_index("y"), lax.axis_index("c")
    out = []
    for k in range(1, N_DEV):
        px = 1 - x if k & 4 else x
        py = 1 - y if k & 2 else y
        pc = 1 - c if k & 1 else c
        out.append(((px, py, pc), 4 * px + 2 * py + pc))
    return 4 * x + 2 * y + c, out


def _in_hbm(a):
    return pltpu.with_memory_space_constraint(a, pltpu.HBM)


def _landing(rows):
    return _in_hbm(lax.empty((N_DEV, rows, D_MODEL), BF16))


def _sem_pair():
    return pltpu.SemaphoreType.DMA((N_PEERS,)), pltpu.SemaphoreType.DMA((N_PEERS,))


def _gather_start(name, payloads):
    n = len(payloads)

    def body(*refs):
        src, land = refs[:n], refs[n:2 * n]
        sems = refs[2 * n:4 * n]
        token = refs[-1]
        me, peers = _peers()
        for g in range(n):
            for k, (pos, _) in enumerate(peers):
                pltpu.make_async_remote_copy(
                    src_ref=src[g], dst_ref=land[g].at[me], send_sem=sems[2 * g].at[k],
                    recv_sem=sems[2 * g + 1].at[k], device_id=pos, device_id_type=_MESH).start()
        token[...] = jnp.zeros_like(token)

    lands = [_landing(p.shape[0]) for p in payloads]
    sem_shapes = [s for _ in payloads for s in _sem_pair()]
    hbm_shapes = [pltpu.HBM(a.shape, a.dtype) for a in list(payloads) + lands]
    out = pl.pallas_call(
        body, name=name, out_shape=(*sem_shapes, *hbm_shapes, _TOKEN),
        in_specs=[_HBM] * (2 * n), out_specs=(*[_SEM] * (2 * n), *[_HBM] * (2 * n), pl.BlockSpec(memory_space=pltpu.VMEM)),
        input_output_aliases={i: 2 * n + i for i in range(2 * n)},
        compiler_params=pltpu.CompilerParams(has_side_effects=_DATAFLOW),
    )(*[_in_hbm(p) for p in payloads], *lands)
    sems, thru = out[:2 * n], out[2 * n:4 * n]
    return [(thru[g], thru[n + g], sems[2 * g], sems[2 * g + 1]) for g in range(n)], out[-1]


def _gather_wait(name, group, after):
    payload, land, send_sems, recv_sems = group

    def body(src_ref, land_ref, send_ref, recv_ref, after_ref, src_out, land_out):
        _, peers = _peers()
        for k, (pos, idx) in enumerate(peers):
            cp = pltpu.make_async_remote_copy(
                src_ref=src_ref, dst_ref=land_ref.at[idx], send_sem=send_ref.at[k], recv_sem=recv_ref.at[k],
                device_id=pos, device_id_type=_MESH)
            cp.wait_send()
            cp.wait_recv()

    _, land = pl.pallas_call(
        body, name=name, out_shape=(pltpu.HBM(payload.shape, payload.dtype), pltpu.HBM(land.shape, land.dtype)),
        in_specs=[_HBM, _HBM, _SEM, _SEM, _ANY], out_specs=(_HBM, _HBM), input_output_aliases={0: 0, 1: 1},
        compiler_params=pltpu.CompilerParams(has_side_effects=_DATAFLOW),
    )(payload, land, send_sems, recv_sems, after)
    me = 4 * lax.axis_index("x") + 2 * lax.axis_index("y") + lax.axis_index("c")
    return lax.dynamic_update_slice(land, payload[None], (me, 0, 0))


def _scatter_start(name, blocks):
    rows = blocks.shape[1]

    def body(blocks_ref, land_ref, send_sems, recv_sems, blocks_out, land_out, token):
        me, peers = _peers()
        for k, (pos, idx) in enumerate(peers):
            pltpu.make_async_remote_copy(
                src_ref=blocks_ref.at[idx], dst_ref=land_ref.at[me], send_sem=send_sems.at[k],
                recv_sem=recv_sems.at[k], device_id=pos, device_id_type=_MESH).start()
        token[...] = jnp.zeros_like(token)

    land = _landing(rows)
    send_sems, recv_sems, blocks_thru, land_thru, token = pl.pallas_call(
        body, name=name,
        out_shape=(*_sem_pair(), pltpu.HBM(blocks.shape, blocks.dtype), pltpu.HBM(land.shape, land.dtype), _TOKEN),
        in_specs=[_HBM, _HBM], out_specs=(_SEM, _SEM, _HBM, _HBM, pl.BlockSpec(memory_space=pltpu.VMEM)),
        input_output_aliases={0: 2, 1: 3},
        compiler_params=pltpu.CompilerParams(has_side_effects=_DATAFLOW),
    )(_in_hbm(blocks), land)
    return (blocks_thru, land_thru, send_sems, recv_sems), token


def _scatter_wait(groups, after):
    n = len(groups)

    def body(*refs):
        blocks, land = refs[:n], refs[n:2 * n]
        sems = refs[2 * n:4 * n]
        _, peers = _peers()
        for g in range(n):
            for k, (pos, idx) in enumerate(peers):
                cp = pltpu.make_async_remote_copy(
                    src_ref=blocks[g].at[idx], dst_ref=land[g].at[idx], send_sem=sems[2 * g].at[k],
                    recv_sem=sems[2 * g + 1].at[k], device_id=pos, device_id_type=_MESH)
                cp.wait_send()
                cp.wait_recv()

    hbm = [grp[0] for grp in groups] + [grp[1] for grp in groups]
    sems = [s for grp in groups for s in grp[2:]]
    out = pl.pallas_call(
        body, name="rs_wait", out_shape=tuple(pltpu.HBM(a.shape, a.dtype) for a in hbm),
        in_specs=[_HBM] * (2 * n) + [_SEM] * (2 * n) + [_ANY], out_specs=tuple([_HBM] * (2 * n)),
        input_output_aliases={i: i for i in range(2 * n)},
        compiler_params=pltpu.CompilerParams(has_side_effects=_DATAFLOW),
    )(*hbm, *sems, after)
    me = 4 * lax.axis_index("x") + 2 * lax.axis_index("y") + lax.axis_index("c")
    lands = []
    for g in range(n):
        own = lax.dynamic_index_in_dim(out[g], me, axis=0, keepdims=True)
        lands.append(lax.dynamic_update_slice(out[n + g], own, (me, 0, 0)))
    return lands


def _pad_rows(flat, rows):
    return jnp.pad(flat, (0, rows * D_MODEL - flat.shape[0])).reshape(rows, D_MODEL)


SMALL_NAMES = ("conv_b_pw1", "conv_w_dw", "conv_b_dw", "conv_ln_g", "conv_ln_b", "conv_b_pw2")


def _pack_small(p):
    flat = jnp.concatenate([p[n].reshape(-1) for n in SMALL_NAMES])
    return _pad_rows(flat, ROWS_SMALL).reshape(1, ROWS_SMALL, D_MODEL)


def _unpack_small(packed):
    flat = packed.reshape(-1)
    c = D_MODEL // N_DEV
    shapes = ((1, 2 * c), (1, CONV_WIDTH, c), (1, c), (1, c), (1, c), (1, c))
    out, o = {}, 0
    for n, shape in zip(SMALL_NAMES, shapes):
        size = shape[-1] * (shape[1] if len(shape) == 3 else 1)
        out[n] = flat[o:o + size].reshape(shape)
        o += size
    return out


def _gather_payloads(p):
    t = lambda a: jnp.swapaxes(a, -1, -2).astype(BF16)
    w1t, w3t, w2 = t(p["ffn_w1"]), t(p["ffn_w3"]), p["ffn_w2"].astype(BF16)
    small = _pack_small(p).reshape(-1)[:ROWS_SMALL * D_MODEL // 2]
    small = lax.bitcast_convert_type(small, BF16).reshape(ROWS_SMALL, D_MODEL)
    conv = jnp.concatenate([t(p["conv_w_pw1"][0]), p["conv_w_pw2"][0].astype(BF16), small], axis=0)
    ffn = [jnp.concatenate([w1t[l], w3t[l], w2[l]], axis=0) for l in range(2)]
    return [t(p["attn_w_qkv"][0]), p["attn_w_o"][0].astype(BF16), ffn[0], conv, ffn[1]]


def _device_rows(land, lo, n):
    return land[:, lo:lo + n].reshape(N_DEV * n, D_MODEL)


def _unpack_conv(land):
    small = lax.bitcast_convert_type(
        land[:, ROWS_PW1 + ROWS_PW2:].reshape(N_DEV, ROWS_SMALL * D_MODEL // 2, 2), F32)
    c = D_MODEL // N_DEV
    b_pw1 = small[:, :2 * c].reshape(1, 2 * D_MODEL)
    s = 2 * c
    w_dw = small[:, s:s + CONV_WIDTH * c].reshape(N_DEV, CONV_WIDTH, c).transpose(1, 0, 2).reshape(CONV_WIDTH, D_MODEL)
    s += CONV_WIDTH * c
    b_dw, ln_g, ln_b, b_pw2 = (small[:, s + i * c:s + (i + 1) * c].reshape(1, D_MODEL) for i in range(4))
    return dict(w_pw1_t=_device_rows(land, 0, ROWS_PW1), w_pw2=_device_rows(land, ROWS_PW1, ROWS_PW2),
                b_pw1=b_pw1, w_dw=w_dw, b_dw=b_dw, ln_g=ln_g, ln_b=ln_b, b_pw2=b_pw2)


def _dest_blocks(mats):
    return jnp.concatenate([a.reshape(N_DEV, -1, D_MODEL) for a in mats], axis=1)


def _small_grad_rows(g_bpw1, g_dw, g_bdw, g_lng, g_lnb, g_bpw2):
    c = D_MODEL // N_DEV
    small = jnp.concatenate(
        [g_bpw1.reshape(N_DEV, 2 * c), g_dw.reshape(CONV_WIDTH, N_DEV, c).transpose(1, 0, 2).reshape(N_DEV, -1),
         g_bdw.reshape(N_DEV, c), g_lng.reshape(N_DEV, c), g_lnb.reshape(N_DEV, c), g_bpw2.reshape(N_DEV, c)], axis=1)
    small = jnp.pad(small, ((0, 0), (0, ROWS_SMALL * D_MODEL - SMALL_USED)))
    return small.reshape(N_DEV * ROWS_SMALL, D_MODEL).astype(BF16)


LOSS_ROW = 9


def _pack_replicated(norm_mix, norm_ffn, b_qkv, sinks, b_o, norm_final, extra=None):
    rows = [norm_mix.reshape(2, D_MODEL), norm_ffn.reshape(2, D_MODEL), _pad_rows(b_qkv.reshape(-1), 2),
            _pad_rows(sinks.reshape(-1), 1), b_o.reshape(1, D_MODEL), norm_final.reshape(1, D_MODEL)]
    if extra is not None:
        rows.append(_pad_rows(extra.reshape(-1), 1))
    p = jnp.concatenate(rows, axis=0)
    return jnp.pad(p, ((0, REPL_ROWS - p.shape[0]), (0, 0)))


def _unpack_replicated(p):
    return dict(norm_mix=p[0:2], norm_ffn=p[2:4], attn_b_qkv=p[4:6].reshape(-1)[:QKV_DIM].reshape(1, QKV_DIM),
                attn_sinks=p[6, :N_Q_HEADS].reshape(1, N_Q_HEADS), attn_b_o=p[7:8], norm_final=p[8])


WEIGHT_ORDER = ['norm_mix', 'norm_ffn', 'attn_w_qkv', 'attn_b_qkv', 'attn_sinks', 'attn_w_o', 'attn_b_o',
                'conv_w_pw1', 'conv_b_pw1', 'conv_w_dw', 'conv_b_dw', 'conv_ln_g', 'conv_ln_b', 'conv_w_pw2',
                'conv_b_pw2', 'ffn_w1', 'ffn_w3', 'ffn_w2', 'norm_final']


def kernel(x, norm_mix, norm_ffn, attn_w_qkv, attn_b_qkv, attn_sinks, attn_w_o, attn_b_o, conv_w_pw1, conv_b_pw1, conv_w_dw, conv_b_dw, conv_ln_g, conv_ln_b, conv_w_pw2, conv_b_pw2, ffn_w1, ffn_w3, ffn_w2, norm_final, loss_target, m_norm_mix, m_norm_ffn, m_attn_w_qkv, m_attn_b_qkv, m_attn_sinks, m_attn_w_o, m_attn_b_o, m_conv_w_pw1, m_conv_b_pw1, m_conv_w_dw, m_conv_b_dw, m_conv_ln_g, m_conv_ln_b, m_conv_w_pw2, m_conv_b_pw2, m_ffn_w1, m_ffn_w3, m_ffn_w2, m_norm_final, v_norm_mix, v_norm_ffn, v_attn_w_qkv, v_attn_b_qkv, v_attn_sinks, v_attn_w_o, v_attn_b_o, v_conv_w_pw1, v_conv_b_pw1, v_conv_w_dw, v_conv_b_dw, v_conv_ln_g, v_conv_ln_b, v_conv_w_pw2, v_conv_b_pw2, v_ffn_w1, v_ffn_w3, v_ffn_w2, v_norm_final):
    xs = x[0]
    target = loss_target[0]
    seq = xs.shape[0]
    my_x, my_y, my_c = lax.axis_index("x"), lax.axis_index("y"), lax.axis_index("c")

    w = dict(attn_w_qkv=attn_w_qkv, attn_w_o=attn_w_o, conv_w_pw1=conv_w_pw1, conv_b_pw1=conv_b_pw1,
             conv_w_dw=conv_w_dw, conv_b_dw=conv_b_dw, conv_ln_g=conv_ln_g, conv_ln_b=conv_ln_b,
             conv_w_pw2=conv_w_pw2, conv_b_pw2=conv_b_pw2, ffn_w1=ffn_w1, ffn_w3=ffn_w3, ffn_w2=ffn_w2)
    m = dict(attn_w_qkv=m_attn_w_qkv, attn_w_o=m_attn_w_o, conv_w_pw1=m_conv_w_pw1, conv_b_pw1=m_conv_b_pw1,
             conv_w_dw=m_conv_w_dw, conv_b_dw=m_conv_b_dw, conv_ln_g=m_conv_ln_g, conv_ln_b=m_conv_ln_b,
             conv_w_pw2=m_conv_w_pw2, conv_b_pw2=m_conv_b_pw2, ffn_w1=m_ffn_w1, ffn_w3=m_ffn_w3, ffn_w2=m_ffn_w2)
    v = dict(attn_w_qkv=v_attn_w_qkv, attn_w_o=v_attn_w_o, conv_w_pw1=v_conv_w_pw1, conv_b_pw1=v_conv_b_pw1,
             conv_w_dw=v_conv_w_dw, conv_b_dw=v_conv_b_dw, conv_ln_g=v_conv_ln_g, conv_ln_b=v_conv_ln_b,
             conv_w_pw2=v_conv_w_pw2, conv_b_pw2=v_conv_b_pw2, ffn_w1=v_ffn_w1, ffn_w3=v_ffn_w3, ffn_w2=v_ffn_w2)

    payloads = _gather_payloads(w)
    (ag_qkv, ag_wo), tok = _gather_start("ag_start_attn", payloads[:2])
    (ag_ffn0, ag_conv, ag_ffn1), tok = _gather_start(
        "ag_start_rest", [payloads[2] + tok[0, 0].astype(BF16), payloads[3], payloads[4]])
    rc, rsa, rsb = _rope_tables(seq)
    sinks = attn_sinks.reshape(N_Q_HEADS)
    g_mix0, g_mix1 = norm_mix[0:1] + tok[0, 0], norm_mix[1:2]
    g_ffn0, g_ffn1 = norm_ffn[0:1], norm_ffn[1:2]
    g_fin = norm_final.reshape(1, D_MODEL)

    w_qkv_t = _gather_wait("ag_wait_qkv", ag_qkv, tok).reshape(QKV_DIM, D_MODEL)
    y0, q, k, vv = _qkv_fwd(xs, g_mix0, w_qkv_t, attn_b_qkv, rc, rsa, rsb)
    attn, probs, p_sinks = _attn_fwd(sinks, q, k, vv)
    w_o = _gather_wait("ag_wait_wo", ag_wo, attn).reshape(Q_DIM, D_MODEL)
    h1 = _mm_res("attn_out_proj", attn, w_o, attn_b_o, xs)
    w_ffn0 = _gather_wait("ag_wait_ffn0", ag_ffn0, h1)
    f0, u0, p0, s0 = _ffn_up("ffn0_up", h1, g_ffn0, w_ffn0)
    h2 = _ffn_down("ffn0_down", s0, w_ffn0, h1)
    wt = _unpack_conv(_gather_wait("ag_wait_conv", ag_conv, h2))
    wd = jnp.concatenate([wt["w_dw"][::-1], jnp.zeros((TAPS_PAD - CONV_WIDTH, D_MODEL), F32)], axis=0)
    y1, a, dwc, z = _conv_fwd(h2, g_mix1, wt["w_pw1_t"], wt["b_pw1"], wd, wt["b_dw"], wt["ln_g"], wt["ln_b"])
    h3 = _mm_res("conv_out_proj", z, wt["w_pw2"], wt["b_pw2"], h2)
    w_ffn1 = _gather_wait("ag_wait_ffn1", ag_ffn1, h3)
    f1, u1, p1, s1 = _ffn_up("ffn1_up", h3, g_ffn1, w_ffn1)
    dh4, dh4b, sq, dg_fin = _ffn_down_loss(s1, w_ffn1, h3, target, g_fin)

    du1, dp1 = _ffn_bwd_act("ffn1_bwd_act", dh4b, u1, p1, w_ffn1)
    dh3, dh3b, dg_ffn1 = _ffn_bwd_in("ffn1_bwd_in", du1, dp1, w_ffn1, h3, dh4, g_ffn1)
    blocks = _mm_tn("ffn1_dw2", s1, dh4b, tk=FF_TILE, into=(None, W2_SLOT))
    blocks = _mm_tn("ffn1_dw1", du1, f1, tk=FF_TILE, into=(blocks, W1T_SLOT))
    blocks = _mm_tn("ffn1_dw3", dp1, f1, tk=FF_TILE, into=(blocks, W3T_SLOT))
    rs_ffn1, tok = _scatter_start("rs_start_ffn1", blocks)

    da, dlg, dlb, dbdw, dwd, dbpw1, dbpw2 = _conv_bwd(dh3, wt["w_pw2"], dwc, a, wt["ln_g"] + tok[0, 0],
                                                     wt["ln_b"], wd)
    gpw2 = _mm_tn("conv_dw_pw2", z, dh3b, tk=D_MODEL)
    dh2, dh2b, dg_mix1 = _mm_rms_bwd("conv_in_bwd", da, wt["w_pw1_t"], h2, dh3, g_mix1, True)
    gpw1t = _mm_tn("conv_dw_pw1", da, y1, tk=D_MODEL)
    small_rows = _small_grad_rows(dbpw1, dwd[:CONV_WIDTH][::-1], dbdw, dlg, dlb, dbpw2)
    rs_conv, tok = _scatter_start("rs_start_conv", _dest_blocks([gpw1t, gpw2, small_rows]))

    du0, dp0 = _ffn_bwd_act("ffn0_bwd_act", dh2b, u0, p0, w_ffn0)
    dh1, dh1b, dg_ffn0 = _ffn_bwd_in("ffn0_bwd_in", du0, dp0, w_ffn0, h1, dh2, g_ffn0 + tok[0, 0])
    blocks = _mm_tn("ffn0_dw2", s0, dh2b, tk=FF_TILE, into=(None, W2_SLOT))
    blocks = _mm_tn("ffn0_dw1", du0, f0, tk=FF_TILE, into=(blocks, W1T_SLOT))
    blocks = _mm_tn("ffn0_dw3", dp0, f0, tk=FF_TILE, into=(blocks, W3T_SLOT))
    rs_ffn0, tok = _scatter_start("rs_start_ffn0", blocks)

    gwo = _mm_tn("attn_dw_o", attn, dh1b, tk=D_MODEL)
    rs_wo, tok2 = _scatter_start("rs_start_wo", _dest_blocks([gwo]))
    dattn, dbo = _nt_bias("attn_out_bwd", dh1, w_o)
    dq, dkc, dkp, dvc, dvp, dsink = _attn_bwd(probs, p_sinks + (tok[0, 0] + tok2[0, 0]), q, k, vv, dattn)
    dqkv, dbqkv = _rope_bwd(dq, dkc, dkp, dvc, dvp, rc, rsa, rsb)
    gqkvt = _mm_tn("attn_dw_qkv", dqkv, y0, tk=QKV_DIM)
    rs_qkv, tok = _scatter_start("rs_start_qkv", _dest_blocks([gqkvt]))
    dx, dg_mix0 = _mm_rms_bwd("qkv_in_bwd", dqkv, w_qkv_t, xs, dh1, g_mix0 + tok[0, 0], False)

    p_ffn1, p_conv, p_ffn0, p_wo, p_qkv = _scatter_wait([rs_ffn1, rs_conv, rs_ffn0, rs_wo, rs_qkv], dx)
    g_ffn = [_sum_slots("rs_sum_ffn0", p_ffn0), _sum_slots("rs_sum_ffn1", p_ffn1)]
    g_conv = _sum_slots("rs_sum_conv", p_conv)
    g_wo = _sum_slots("rs_sum_wo", p_wo)
    g_qkv = _sum_slots("rs_sum_qkv", p_qkv)

    def ff(slot, transpose):
        blk = jnp.stack([g[slot * ROWS_FF:(slot + 1) * ROWS_FF] for g in g_ffn])
        return jnp.swapaxes(blk, 1, 2) if transpose else blk

    grads = dict(
        ffn_w1=ff(W1T_SLOT, True), ffn_w3=ff(W3T_SLOT, True), ffn_w2=ff(W2_SLOT, False),
        attn_w_qkv=g_qkv.T[None], attn_w_o=g_wo[None],
        conv_w_pw1=g_conv[:ROWS_PW1].T[None], conv_w_pw2=g_conv[ROWS_PW1:ROWS_PW1 + ROWS_PW2][None])
    sharded = [dict(grads), {}, {}, {}]
    for n in grads:
        for dst, t in zip(sharded[1:], _adamw_native("adamw_" + n, grads[n], w[n], m[n], v[n])):
            dst[n] = t
    g_small = g_conv[ROWS_PW1 + ROWS_PW2:][None]
    small_out = _adamw_native("adamw_small", g_small, _pack_small(w), _pack_small(m), _pack_small(v))
    for dst, t in zip(sharded, (g_small,) + tuple(small_out)):
        dst.update(_unpack_small(t))

    part = _pack_replicated(jnp.concatenate([dg_mix0, dg_mix1]), jnp.concatenate([dg_ffn0, dg_ffn1]),
                            dbqkv, -dsink[:, 0], dbo, dg_fin, extra=sq[0, 0:1])
    parts = _all_gather("ag_replicated_grads", part)
    w_rep = _pack_replicated(norm_mix, norm_ffn, attn_b_qkv, attn_sinks, attn_b_o, norm_final)
    m_rep = _pack_replicated(m_norm_mix, m_norm_ffn, m_attn_b_qkv, m_attn_sinks, m_attn_b_o, m_norm_final)
    v_rep = _pack_replicated(v_norm_mix, v_norm_ffn, v_attn_b_qkv, v_attn_sinks, v_attn_b_o, v_norm_final)
    rep_out = _adamw_replicated(parts, w_rep, m_rep, v_rep)
    replicated = [_unpack_replicated(t) for t in rep_out]
    loss = rep_out[0][LOSS_ROW, 0] * (0.5 / D_MODEL)

    outs = [loss, dx.reshape(1, seq, D_MODEL)]
    for sh, rp in zip(sharded, replicated):
        merged = {**sh, **rp}
        outs += [merged[n] for n in WEIGHT_ORDER]
    return tuple(outs)
```

```python
import jax
import jax.numpy as jnp
from jax import lax
from jax.experimental import pallas as pl
from jax.experimental.pallas import tpu as pltpu

F32 = jnp.float32
BF16 = jnp.bfloat16

D_MODEL = 1024
HEAD_DIM = 64
N_Q_HEADS = 16
N_KV_HEADS = 2
Q_PER_KV = 8
Q_DIM = N_Q_HEADS * HEAD_DIM
KV_DIM = N_KV_HEADS * HEAD_DIM
QKV_DIM = Q_DIM + 2 * KV_DIM
BLOCK = 128
CONV_WIDTH = 31
D_FF = 2816
ROPE_THETA = 10000.0
RMS_EPS = 1e-5
LN_EPS = 1e-5
ADAM_LR = 0.001
ADAM_B1 = 0.9
ADAM_B2 = 0.999
ADAM_EPS = 1e-08
ADAM_WD = 0.01
ADAM_STEP = 10
N_DEV = 8

LANES = 128
SUBLANES = 8
TOKEN_TILE = 512
CONV_CHUNK = 64
CONV_HALO = 32
TAPS_PAD = 32
VMEM_LIMIT = 56 * 1024 * 1024
NEG_INF = float(jnp.finfo(jnp.float32).min)

ROWS_FF = D_FF // N_DEV
W1T_SLOT, W3T_SLOT, W2_SLOT = 0, 1, 2
ROWS_QKV = QKV_DIM // N_DEV
ROWS_WO = Q_DIM // N_DEV
ROWS_PW1 = 2 * D_MODEL // N_DEV
ROWS_PW2 = D_MODEL // N_DEV
ROWS_SMALL = 16
SMALL_USED = 2 * D_MODEL // N_DEV + CONV_WIDTH * (D_MODEL // N_DEV) + 4 * (D_MODEL // N_DEV)
FF_SPLIT = 2
FF_TILE_DEVS = N_DEV // FF_SPLIT
FF_TILE = FF_TILE_DEVS * ROWS_FF
REPL_ROWS = 16


def _call(body, *, name, grid, in_specs, out_specs, out_shape, scratch=(), sem=None, aliases=None):
    return pl.pallas_call(
        body, name=name, grid=grid, in_specs=in_specs, out_specs=out_specs, out_shape=out_shape,
        scratch_shapes=list(scratch), input_output_aliases=aliases or {},
        compiler_params=pltpu.CompilerParams(dimension_semantics=sem, vmem_limit_bytes=VMEM_LIMIT))


def _full(shape):
    return pl.BlockSpec(shape, lambda *_: (0,) * len(shape))


def _resident(shape):
    return pl.BlockSpec(shape, lambda *_: (0,) * len(shape), pipeline_mode=pl.Buffered(1))


def _rows(tm, n):
    return pl.BlockSpec((tm, n), lambda i, *_: (i, 0))


def _sig(x):
    return 1.0 / (1.0 + jnp.exp(-x))


def _sum_rows(x):
    return jnp.sum(x, axis=0, keepdims=True)


def _nt(a, b):
    return lax.dot_general(a, b, (((1,), (1,)), ((), ())), preferred_element_type=F32)


def _tn(a, b):
    return lax.dot_general(a, b, (((0,), (0,)), ((), ())), preferred_element_type=F32)


def _rms_stats(x):
    return lax.rsqrt(jnp.mean(x * x, axis=-1, keepdims=True) + RMS_EPS)


def _rms_bwd(dy, x, g, dres):
    r = _rms_stats(x)
    n = x * r
    dn = dy * g
    dx = dres + r * (dn - n * jnp.mean(dn * n, axis=-1, keepdims=True))
    return dx, _sum_rows(dy * n)


def _rope_tables(seq):
    pos = jnp.arange(seq, dtype=F32)
    inv_freq = ROPE_THETA ** (-jnp.arange(0, HEAD_DIM, 2, dtype=F32) / HEAD_DIM)
    ang = pos[:, None] * inv_freq[None, :]
    cos, sin = jnp.cos(ang), jnp.sin(ang)
    zero = jnp.zeros_like(sin)
    reps = LANES // HEAD_DIM
    c = jnp.tile(jnp.concatenate([cos, cos], axis=1), (1, reps))
    sa = jnp.tile(jnp.concatenate([-sin, zero], axis=1), (1, reps))
    sb = jnp.tile(jnp.concatenate([zero, sin], axis=1), (1, reps))
    return c, sa, sb


def _rope(t, c, sa, sb):
    half = HEAD_DIM // 2
    return t * c + pltpu.roll(t, LANES - half, 1) * sa + pltpu.roll(t, half, 1) * sb


def _rope_t(dt, c, sa, sb):
    half = HEAD_DIM // 2
    return dt * c + pltpu.roll(dt * sa, half, 1) + pltpu.roll(dt * sb, LANES - half, 1)


def _qkv_fwd(x, g, w, b, rc, rsa, rsb):
    seq = x.shape[0]
    tm = min(TOKEN_TILE, seq)

    def body(x_ref, g_ref, w_ref, b_ref, c_ref, sa_ref, sb_ref, y_ref, q_ref, k_ref, v_ref):
        xv = x_ref[...]
        y = (xv * _rms_stats(xv) * g_ref[...]).astype(BF16)
        y_ref[...] = y
        qkv = _nt(y, w_ref[...]) + b_ref[...]
        c, sa, sb = c_ref[...], sa_ref[...], sb_ref[...]
        for i in range(Q_DIM // LANES):
            blk = _rope(qkv[:, i * LANES:(i + 1) * LANES], c, sa, sb)
            q_ref[:, i * LANES:(i + 1) * LANES] = (blk * (HEAD_DIM ** -0.5)).astype(BF16)
        k_ref[...] = _rope(qkv[:, Q_DIM:Q_DIM + KV_DIM], c, sa, sb).astype(BF16)
        v_ref[...] = qkv[:, Q_DIM + KV_DIM:].astype(BF16)

    return _call(
        body, name="qkv_fwd", grid=(seq // tm,),
        in_specs=[_rows(tm, D_MODEL), _full((1, D_MODEL)), _full((QKV_DIM, D_MODEL)), _full((1, QKV_DIM)),
                  _rows(tm, LANES), _rows(tm, LANES), _rows(tm, LANES)],
        out_specs=[_rows(tm, D_MODEL), _rows(tm, Q_DIM), _rows(tm, KV_DIM), _rows(tm, KV_DIM)],
        out_shape=[jax.ShapeDtypeStruct((seq, D_MODEL), BF16), jax.ShapeDtypeStruct((seq, Q_DIM), BF16),
                   jax.ShapeDtypeStruct((seq, KV_DIM), BF16), jax.ShapeDtypeStruct((seq, KV_DIM), BF16)],
        sem=("parallel",))(x, g, w, b, rc, rsa, rsb)


GROUP_ROWS = Q_PER_KV * BLOCK


def _band_mask(n, rows=GROUP_ROWS):
    row = lax.broadcasted_iota(jnp.int32, (rows, 2 * BLOCK), 0) & (BLOCK - 1)
    col = lax.broadcasted_iota(jnp.int32, (rows, 2 * BLOCK), 1)
    rel = row + BLOCK - col
    return (rel >= 0) & (rel < BLOCK) & ((col >= BLOCK) | (n > 0))


def _softmax_with_sink(s, mask, sink):
    s = jnp.where(mask, s, NEG_INF)
    m = jnp.maximum(jnp.max(s, axis=-1, keepdims=True), sink)
    p = jnp.exp(s - m)
    e_sink = jnp.exp(sink - m)
    inv = 1.0 / (jnp.sum(p, axis=-1, keepdims=True) + e_sink)
    return p * inv, e_sink * inv


PAIRS_PER_KV = Q_PER_KV // 2


def _kv_specs():
    cur = pl.BlockSpec((BLOCK, KV_DIM), lambda n: (n, 0))
    prev = pl.BlockSpec((BLOCK, KV_DIM), lambda n: (jnp.maximum(n - 1, 0), 0))
    return cur, prev


def _low_lanes():
    return lax.broadcasted_iota(jnp.int32, (2 * BLOCK, KV_DIM), 1) < HEAD_DIM


def _kv_low_high(prev_ref, cur_ref, j, low):
    both = jnp.concatenate([prev_ref[...], cur_ref[...]], axis=0).astype(F32)
    swapped = pltpu.roll(both, HEAD_DIM, 1)
    at_low, at_high = (both, swapped) if j == 0 else (swapped, both)
    return jnp.where(low, at_low, 0.0).astype(BF16), jnp.where(low, 0.0, at_high).astype(BF16)


def _fold_pair_halves(acc, j, low):
    folded = acc + pltpu.roll(acc, HEAD_DIM, 1)
    return jnp.where(low, folded, 0.0) if j == 0 else jnp.where(low, 0.0, folded)


def _pair_lanes(j, i):
    g = j * PAIRS_PER_KV + i
    return slice(g * LANES, (g + 1) * LANES), 2 * g


def _attn_fwd(sinks, q, k, v):
    seq = q.shape[0]
    cur, prev = _kv_specs()

    def body(sink_ref, q_ref, kc_ref, kp_ref, vc_ref, vp_ref, o_ref, p_ref, ps_ref):
        mask = _band_mask(pl.program_id(0), BLOCK)
        lane = lax.broadcasted_iota(jnp.int32, (BLOCK, LANES), 1)
        p_sinks = jnp.zeros((BLOCK, LANES), F32)
        for j in range(N_KV_HEADS):
            cs = slice(j * HEAD_DIM, (j + 1) * HEAD_DIM)
            kk = jnp.concatenate([kp_ref[:, cs], kc_ref[:, cs]], axis=0)
            vv = jnp.concatenate([vp_ref[:, cs], vc_ref[:, cs]], axis=0)
            for gq in range(Q_PER_KV):
                h = j * Q_PER_KV + gq
                hs = slice(h * HEAD_DIM, (h + 1) * HEAD_DIM)
                probs, p_sink = _softmax_with_sink(_nt(q_ref[:, hs], kk), mask, sink_ref[h])
                pb = probs.astype(BF16)
                p_ref[h] = pb
                p_sinks = jnp.where(lane == h, p_sink, p_sinks)
                o_ref[:, hs] = jnp.dot(pb, vv, preferred_element_type=F32).astype(BF16)
        ps_ref[...] = p_sinks

    return _call(
        body, name="attn_fwd", grid=(seq // BLOCK,),
        in_specs=[pl.BlockSpec(memory_space=pltpu.SMEM), _rows(BLOCK, Q_DIM), cur, prev, cur, prev],
        out_specs=[_rows(BLOCK, Q_DIM), pl.BlockSpec((N_Q_HEADS, BLOCK, 2 * BLOCK), lambda n: (0, n, 0)),
                   _rows(BLOCK, LANES)],
        out_shape=[jax.ShapeDtypeStruct((seq, Q_DIM), BF16),
                   jax.ShapeDtypeStruct((N_Q_HEADS, seq, 2 * BLOCK), BF16),
                   jax.ShapeDtypeStruct((seq, LANES), F32)],
        sem=("parallel",))(sinks, q, k, k, v, v)


def _mm_res(name, a, w, b, res):
    seq, kdim = a.shape
    n = w.shape[1]
    tm = min(2 * TOKEN_TILE, seq)

    def body(a_ref, w_ref, b_ref, r_ref, o_ref):
        o_ref[...] = r_ref[...] + (jnp.dot(a_ref[...], w_ref[...], preferred_element_type=F32) + b_ref[...])

    return _call(
        body, name=name, grid=(seq // tm,),
        in_specs=[_rows(tm, kdim), _resident((kdim, n)), _full((1, n)), _rows(tm, n)],
        out_specs=_rows(tm, n), out_shape=jax.ShapeDtypeStruct((seq, n), F32),
        sem=("parallel",))(a, w, b, res)


def _ff_tile_spec(slot):
    return pl.BlockSpec((FF_TILE_DEVS, ROWS_FF, D_MODEL), lambda j, i: (j, slot, 0))


def _ff_whole_spec(slot):
    return pl.BlockSpec((N_DEV, ROWS_FF, D_MODEL), lambda i: (0, slot, 0), pipeline_mode=pl.Buffered(1))


def _ffn_up(name, h, g, gathered):
    seq = h.shape[0]
    tm = min(TOKEN_TILE, seq)

    def body(h_ref, g_ref, w1_ref, w3_ref, f_ref, u_ref, w_ref, s_ref):
        hv = h_ref[...]
        f = (hv * _rms_stats(hv) * g_ref[...]).astype(BF16)
        f_ref[...] = f
        u = _nt(f, w1_ref[...].reshape(FF_TILE, D_MODEL))
        w = _nt(f, w3_ref[...].reshape(FF_TILE, D_MODEL))
        u_ref[...] = u.astype(BF16)
        w_ref[...] = w.astype(BF16)
        s_ref[...] = (u * _sig(u) * w).astype(BF16)

    n_t = seq // tm
    tile_ff = pl.BlockSpec((tm, FF_TILE), lambda j, i: (i, j))
    f_spec = pl.BlockSpec((tm, D_MODEL), lambda j, i: (jnp.where(j == 0, i, n_t), 0))
    ff_shape = jax.ShapeDtypeStruct((seq, D_FF), BF16)
    return _call(
        body, name=name, grid=(FF_SPLIT, n_t),
        in_specs=[pl.BlockSpec((tm, D_MODEL), lambda j, i: (i, 0)), _full((1, D_MODEL)),
                  _ff_tile_spec(W1T_SLOT), _ff_tile_spec(W3T_SLOT)],
        out_specs=[f_spec, tile_ff, tile_ff, tile_ff],
        out_shape=[jax.ShapeDtypeStruct((seq + tm, D_MODEL), BF16), ff_shape, ff_shape, ff_shape],
        sem=("arbitrary", "arbitrary"))(h, g, gathered, gathered)


def _ffn_down(name, s, gathered, res):
    seq = s.shape[0]
    tm = min(2 * TOKEN_TILE, seq)

    def body(s_ref, w_ref, r_ref, o_ref):
        w2 = w_ref[...].reshape(D_FF, D_MODEL)
        o_ref[...] = r_ref[...] + jnp.dot(s_ref[...], w2, preferred_element_type=F32)

    return _call(
        body, name=name, grid=(seq // tm,),
        in_specs=[_rows(tm, D_FF), _ff_whole_spec(W2_SLOT), _rows(tm, D_MODEL)],
        out_specs=_rows(tm, D_MODEL), out_shape=jax.ShapeDtypeStruct((seq, D_MODEL), F32),
        sem=("parallel",))(s, gathered, res)


def _conv_fwd(h, g, wpw1, bpw1, wd, bdw, lng, lnb):
    seq = h.shape[0]
    tm = min(TOKEN_TILE, seq)
    n_chunks = tm // CONV_CHUNK
    win = CONV_CHUNK + CONV_HALO

    def body(h_ref, g_ref, w_ref, b_ref, wd_ref, bdw_ref, lng_ref, lnb_ref,
             y_ref, a_ref, dwc_ref, z_ref, gbuf):
        i = pl.program_id(0)

        @pl.when(i == 0)
        def _():
            gbuf[0:CONV_HALO, :] = jnp.zeros((CONV_HALO, D_MODEL), F32)

        @pl.when(i > 0)
        def _():
            gbuf[0:CONV_HALO, :] = gbuf[tm:tm + CONV_HALO, :]

        hv = h_ref[...]
        y = (hv * _rms_stats(hv) * g_ref[...]).astype(BF16)
        y_ref[...] = y
        a = _nt(y, w_ref[...]) + b_ref[...]
        a_ref[...] = a.astype(BF16)
        gbuf[CONV_HALO:CONV_HALO + tm, :] = a[:, :D_MODEL] * _sig(a[:, D_MODEL:])

        def chunk(c, carry):
            r0 = pl.multiple_of(c * CONV_CHUNK, CONV_CHUNK)
            for l in range(D_MODEL // LANES):
                ls = slice(l * LANES, (l + 1) * LANES)
                gw = gbuf[pl.ds(r0, win), ls]
                acc = jnp.zeros((CONV_CHUNK, LANES), F32) + bdw_ref[:, ls]
                for s in range(SUBLANES):
                    gs = gw if s == 0 else pltpu.roll(gw, s, 0)
                    for q in range(CONV_HALO // SUBLANES):
                        d = SUBLANES * q + s
                        if d < CONV_WIDTH:
                            lo = CONV_HALO - SUBLANES * q
                            acc = acc + wd_ref[d:d + 1, ls] * gs[lo:lo + CONV_CHUNK]
                dwc_ref[pl.ds(r0, CONV_CHUNK), ls] = acc
            return carry

        lax.fori_loop(0, n_chunks, chunk, 0)

        xv = dwc_ref[...]
        mu = jnp.mean(xv, axis=-1, keepdims=True)
        xc = xv - mu
        var = jnp.mean(xc * xc, axis=-1, keepdims=True)
        ln = xc * lax.rsqrt(var + LN_EPS) * lng_ref[...] + lnb_ref[...]
        z_ref[...] = (ln * _sig(ln)).astype(BF16)

    return _call(
        body, name="conv_fwd", grid=(seq // tm,),
        in_specs=[_rows(tm, D_MODEL), _full((1, D_MODEL)), _full((2 * D_MODEL, D_MODEL)), _full((1, 2 * D_MODEL)),
                  _full((TAPS_PAD, D_MODEL)), _full((1, D_MODEL)), _full((1, D_MODEL)), _full((1, D_MODEL))],
        out_specs=[_rows(tm, D_MODEL), _rows(tm, 2 * D_MODEL), _rows(tm, D_MODEL), _rows(tm, D_MODEL)],
        out_shape=[jax.ShapeDtypeStruct((seq, D_MODEL), BF16), jax.ShapeDtypeStruct((seq, 2 * D_MODEL), BF16),
                   jax.ShapeDtypeStruct((seq, D_MODEL), F32), jax.ShapeDtypeStruct((seq, D_MODEL), BF16)],
        scratch=[pltpu.VMEM((tm + CONV_HALO, D_MODEL), F32)],
        sem=("arbitrary",))(h, g, wpw1, bpw1, wd, bdw, lng, lnb)


def _ffn_down_loss(s, gathered, res, target, g):
    seq = s.shape[0]
    tm = min(TOKEN_TILE, seq)

    def body(s_ref, w_ref, r_ref, t_ref, g_ref, dh_ref, dhb_ref, loss_ref, dg_ref):
        @pl.when(pl.program_id(0) == 0)
        def _():
            loss_ref[...] = jnp.zeros_like(loss_ref)
            dg_ref[...] = jnp.zeros_like(dg_ref)

        hv = r_ref[...] + jnp.dot(s_ref[...], w_ref[...].reshape(D_FF, D_MODEL), preferred_element_type=F32)
        gv = g_ref[...]
        err = hv * _rms_stats(hv) * gv - t_ref[...]
        sq = jnp.sum(jnp.sum(err * err, axis=-1, keepdims=True), axis=0, keepdims=True)
        loss_ref[...] += jnp.broadcast_to(sq, loss_ref.shape)
        dx, dg = _rms_bwd(err * (1.0 / D_MODEL), hv, gv, 0.0)
        dh_ref[...] = dx
        dhb_ref[...] = dx.astype(BF16)
        dg_ref[...] += dg

    return _call(
        body, name="ffn1_down_loss", grid=(seq // tm,),
        in_specs=[_rows(tm, D_FF), _ff_whole_spec(W2_SLOT), _rows(tm, D_MODEL), _rows(tm, D_MODEL),
                  _full((1, D_MODEL))],
        out_specs=[_rows(tm, D_MODEL), _rows(tm, D_MODEL), _full((SUBLANES, LANES)), _full((1, D_MODEL))],
        out_shape=[jax.ShapeDtypeStruct((seq, D_MODEL), F32), jax.ShapeDtypeStruct((seq, D_MODEL), BF16),
                   jax.ShapeDtypeStruct((SUBLANES, LANES), F32), jax.ShapeDtypeStruct((1, D_MODEL), F32)],
        sem=("arbitrary",))(s, gathered, res, target, g)


def _ffn_bwd_act(name, dh, u, w, gathered):
    seq = dh.shape[0]
    tm = min(TOKEN_TILE, seq)

    def body(dh_ref, u_ref, w_ref, w2_ref, du_ref, dw_ref):
        ds = _nt(dh_ref[...], w2_ref[...].reshape(FF_TILE, D_MODEL))
        uv = u_ref[...].astype(F32)
        sg = _sig(uv)
        dw_ref[...] = (ds * (uv * sg)).astype(BF16)
        du_ref[...] = (ds * w_ref[...].astype(F32) * (sg * (1.0 + uv * (1.0 - sg)))).astype(BF16)

    tile_ff = pl.BlockSpec((tm, FF_TILE), lambda j, i: (i, j))
    ff_shape = jax.ShapeDtypeStruct((seq, D_FF), BF16)
    return _call(
        body, name=name, grid=(FF_SPLIT, seq // tm),
        in_specs=[pl.BlockSpec((tm, D_MODEL), lambda j, i: (i, 0)), tile_ff, tile_ff, _ff_tile_spec(W2_SLOT)],
        out_specs=[tile_ff, tile_ff], out_shape=[ff_shape, ff_shape],
        sem=("parallel", "parallel"))(dh, u, w, gathered)


def _ffn_bwd_in(name, du, dw, gathered, h_in, dres, g):
    seq = du.shape[0]
    tm = min(TOKEN_TILE, seq)

    def body(du_ref, dw_ref, w1_ref, w3_ref, h_ref, dr_ref, g_ref, dx_ref, dxb_ref, dg_ref):
        @pl.when(pl.program_id(0) == 0)
        def _():
            dg_ref[...] = jnp.zeros_like(dg_ref)

        df = jnp.dot(du_ref[...], w1_ref[...].reshape(D_FF, D_MODEL), preferred_element_type=F32)
        df = df + jnp.dot(dw_ref[...], w3_ref[...].reshape(D_FF, D_MODEL), preferred_element_type=F32)
        dx, dg = _rms_bwd(df, h_ref[...], g_ref[...], dr_ref[...])
        dx_ref[...] = dx
        dxb_ref[...] = dx.astype(BF16)
        dg_ref[...] += dg

    return _call(
        body, name=name, grid=(seq // tm,),
        in_specs=[_rows(tm, D_FF), _rows(tm, D_FF), _ff_whole_spec(W1T_SLOT), _ff_whole_spec(W3T_SLOT),
                  _rows(tm, D_MODEL), _rows(tm, D_MODEL), _full((1, D_MODEL))],
        out_specs=[_rows(tm, D_MODEL), _rows(tm, D_MODEL), _full((1, D_MODEL))],
        out_shape=[jax.ShapeDtypeStruct((seq, D_MODEL), F32), jax.ShapeDtypeStruct((seq, D_MODEL), BF16),
                   jax.ShapeDtypeStruct((1, D_MODEL), F32)],
        sem=("arbitrary",))(du, dw, gathered, gathered, h_in, dres, g)


def _mm_tn(name, a, b, *, tk, into=None):
    seq, kdim = a.shape
    n = b.shape[1]
    tt = min((4 if b.dtype == BF16 else 2) * TOKEN_TILE, seq)
    n_t = seq // tt
    devs = tk // ROWS_FF

    def body(a_ref, b_ref, *rest):
        o_ref, acc = rest[-2:]
        t = pl.program_id(1)

        @pl.when(t == 0)
        def _():
            acc[...] = jnp.zeros_like(acc)

        acc[...] += _tn(a_ref[...].astype(BF16), b_ref[...].astype(BF16))

        @pl.when(t == n_t - 1)
        def _():
            out = acc[...].astype(BF16)
            o_ref[...] = out if into is None else out.reshape(devs, ROWS_FF, n)

    in_specs = [pl.BlockSpec((tt, tk), lambda k, t: (t, k)), pl.BlockSpec((tt, n), lambda k, t: (t, 0))]
    args = [a, b]
    aliases = None
    if into is None:
        out_spec = pl.BlockSpec((tk, n), lambda k, t: (k, 0))
        out_shape = jax.ShapeDtypeStruct((kdim, n), BF16)
    else:
        blocks, slot = into
        out_spec = pl.BlockSpec((devs, ROWS_FF, n), lambda k, t: (k, slot, 0))
        out_shape = jax.ShapeDtypeStruct((N_DEV, 3 * ROWS_FF, n), BF16)
        if blocks is not None:
            in_specs.append(_ANY)
            args.append(blocks)
            aliases = {2: 0}
    return _call(
        body, name=name, grid=(kdim // tk, n_t), in_specs=in_specs, out_specs=out_spec, out_shape=out_shape,
        scratch=[pltpu.VMEM((tk, n), F32)], sem=("parallel", "arbitrary"), aliases=aliases)(*args)


def _conv_bwd(dh, wpw2, dwc, a, lng, lnb, wd):
    seq = dh.shape[0]
    tm = min(TOKEN_TILE, seq)
    nt = seq // tm
    n_chunks = tm // CONV_CHUNK
    win = CONV_CHUNK + CONV_HALO
    halo_per_tile = tm // CONV_HALO

    def body(dh_ref, w_ref, dwc_ref, a_ref, ah_ref, lng_ref, lnb_ref, wd_ref,
             da_ref, dlg_ref, dlb_ref, dbdw_ref, dwd_ref, dbpw1_ref, dbpw2_ref,
             gbuf, dbuf, dglu, dwd_part):
        i = pl.program_id(0)
        r = nt - 1 - i

        @pl.when(i == 0)
        def _():
            dlg_ref[...] = jnp.zeros_like(dlg_ref)
            dlb_ref[...] = jnp.zeros_like(dlb_ref)
            dbdw_ref[...] = jnp.zeros_like(dbdw_ref)
            dbpw1_ref[...] = jnp.zeros_like(dbpw1_ref)
            dbpw2_ref[...] = jnp.zeros_like(dbpw2_ref)
            dwd_part[...] = jnp.zeros_like(dwd_part)
            dbuf[tm:tm + CONV_HALO, :] = jnp.zeros((CONV_HALO, D_MODEL), F32)

        @pl.when(i > 0)
        def _():
            dbuf[tm:tm + CONV_HALO, :] = dbuf[0:CONV_HALO, :]

        dhv = dh_ref[...]
        dbpw2_ref[...] += _sum_rows(dhv)
        dz = _nt(dhv.astype(BF16), w_ref[...])
        xv = dwc_ref[...]
        lg = lng_ref[...]
        mu = jnp.mean(xv, axis=-1, keepdims=True)
        xc = xv - mu
        rstd = lax.rsqrt(jnp.mean(xc * xc, axis=-1, keepdims=True) + LN_EPS)
        xhat = xc * rstd
        ln = xhat * lg + lnb_ref[...]
        sg = _sig(ln)
        dln = dz * (sg * (1.0 + ln * (1.0 - sg)))
        dlg_ref[...] += _sum_rows(dln * xhat)
        dlb_ref[...] += _sum_rows(dln)
        dxh = dln * lg
        ddw = rstd * (dxh - jnp.mean(dxh, axis=-1, keepdims=True)
                      - xhat * jnp.mean(dxh * xhat, axis=-1, keepdims=True))
        dbdw_ref[...] += _sum_rows(ddw)
        dbuf[0:tm, :] = ddw

        av = a_ref[...].astype(F32)
        a1 = av[:, :D_MODEL]
        s2 = _sig(av[:, D_MODEL:])
        gbuf[CONV_HALO:CONV_HALO + tm, :] = a1 * s2
        ah = ah_ref[...].astype(F32)
        gh = ah[:, :D_MODEL] * _sig(ah[:, D_MODEL:])
        gbuf[0:CONV_HALO, :] = jnp.where(r > 0, gh, 0.0)

        def chunk(c, carry):
            r0 = pl.multiple_of(c * CONV_CHUNK, CONV_CHUNK)
            for l in range(D_MODEL // LANES):
                ls = slice(l * LANES, (l + 1) * LANES)
                dw_ = dbuf[pl.ds(r0, win), ls]
                gw = gbuf[pl.ds(r0, win), ls]
                acc = jnp.zeros((CONV_CHUNK, LANES), F32)
                for s in range(SUBLANES):
                    ds_ = dw_ if s == 0 else pltpu.roll(dw_, win - s, 0)
                    for q in range(CONV_HALO // SUBLANES):
                        d = SUBLANES * q + s
                        if d < CONV_WIDTH:
                            acc = acc + wd_ref[d:d + 1, ls] * ds_[SUBLANES * q:SUBLANES * q + CONV_CHUNK]
                            lo = CONV_HALO - SUBLANES * q
                            prod = ds_[0:CONV_CHUNK] * gw[lo:lo + CONV_CHUNK]
                            dwd_part[d, :, ls] += jnp.sum(
                                prod.reshape(CONV_CHUNK // SUBLANES, SUBLANES, LANES), axis=0)
                dglu[pl.ds(r0, CONV_CHUNK), ls] = acc
            return carry

        lax.fori_loop(0, n_chunks, chunk, 0)

        dg_ = dglu[...]
        da1 = dg_ * s2
        da2 = dg_ * a1 * s2 * (1.0 - s2)
        da_ref[:, :D_MODEL] = da1.astype(BF16)
        da_ref[:, D_MODEL:] = da2.astype(BF16)
        dbpw1_ref[:, :D_MODEL] += _sum_rows(da1)
        dbpw1_ref[:, D_MODEL:] += _sum_rows(da2)

        @pl.when(i == nt - 1)
        def _():
            dwd_ref[...] = jnp.sum(dwd_part[...], axis=1)

    rev = lambda n: pl.BlockSpec((tm, n), lambda i: (nt - 1 - i, 0))
    halo = pl.BlockSpec((CONV_HALO, 2 * D_MODEL),
                        lambda i: (jnp.maximum((nt - 1 - i) * halo_per_tile - 1, 0), 0))
    vec = lambda n: _full((1, n))
    return _call(
        body, name="conv_bwd", grid=(nt,),
        in_specs=[rev(D_MODEL), _full((D_MODEL, D_MODEL)), rev(D_MODEL), rev(2 * D_MODEL), halo,
                  vec(D_MODEL), vec(D_MODEL), _full((TAPS_PAD, D_MODEL))],
        out_specs=[rev(2 * D_MODEL), vec(D_MODEL), vec(D_MODEL), vec(D_MODEL), _full((TAPS_PAD, D_MODEL)),
                   vec(2 * D_MODEL), vec(D_MODEL)],
        out_shape=[jax.ShapeDtypeStruct((seq, 2 * D_MODEL), BF16), jax.ShapeDtypeStruct((1, D_MODEL), F32),
                   jax.ShapeDtypeStruct((1, D_MODEL), F32), jax.ShapeDtypeStruct((1, D_MODEL), F32),
                   jax.ShapeDtypeStruct((TAPS_PAD, D_MODEL), F32), jax.ShapeDtypeStruct((1, 2 * D_MODEL), F32),
                   jax.ShapeDtypeStruct((1, D_MODEL), F32)],
        scratch=[pltpu.VMEM((tm + CONV_HALO, D_MODEL), F32), pltpu.VMEM((tm + CONV_HALO, D_MODEL), F32),
                 pltpu.VMEM((tm, D_MODEL), F32), pltpu.VMEM((TAPS_PAD, SUBLANES, D_MODEL), F32)],
        sem=("arbitrary",))(dh, wpw2, dwc, a, a, lng, lnb, wd)


def _mm_rms_bwd(name, dact, wt, h_in, dres, g, bf16_copy):
    seq, n = dact.shape
    tm = min(TOKEN_TILE, seq)

    def body(da_ref, w_ref, h_ref, dr_ref, g_ref, dx_ref, *rest):
        dg_ref = rest[-1]

        @pl.when(pl.program_id(0) == 0)
        def _():
            dg_ref[...] = jnp.zeros_like(dg_ref)

        dy = jnp.dot(da_ref[...], w_ref[...], preferred_element_type=F32)
        dx, dg = _rms_bwd(dy, h_ref[...], g_ref[...], dr_ref[...])
        dx_ref[...] = dx
        if bf16_copy:
            rest[0][...] = dx.astype(BF16)
        dg_ref[...] += dg

    copy_spec = [_rows(tm, D_MODEL)] if bf16_copy else []
    copy_shape = [jax.ShapeDtypeStruct((seq, D_MODEL), BF16)] if bf16_copy else []
    return _call(
        body, name=name, grid=(seq // tm,),
        in_specs=[_rows(tm, n), _resident((n, D_MODEL)), _rows(tm, D_MODEL), _rows(tm, D_MODEL), _full((1, D_MODEL))],
        out_specs=[_rows(tm, D_MODEL), *copy_spec, _full((1, D_MODEL))],
        out_shape=[jax.ShapeDtypeStruct((seq, D_MODEL), F32), *copy_shape, jax.ShapeDtypeStruct((1, D_MODEL), F32)],
        sem=("arbitrary",))(dact, wt, h_in, dres, g)


def _nt_bias(name, dy, w):
    seq, n = dy.shape
    kdim = w.shape[0]
    tm = min(2 * TOKEN_TILE, seq)

    def body(dy_ref, w_ref, o_ref, db_ref):
        @pl.when(pl.program_id(0) == 0)
        def _():
            db_ref[...] = jnp.zeros_like(db_ref)

        dyv = dy_ref[...]
        db_ref[...] += _sum_rows(dyv)
        o_ref[...] = _nt(dyv.astype(BF16), w_ref[...]).astype(BF16)

    return _call(
        body, name=name, grid=(seq // tm,),
        in_specs=[_rows(tm, n), _resident((kdim, n))],
        out_specs=[_rows(tm, kdim), _full((1, n))],
        out_shape=[jax.ShapeDtypeStruct((seq, kdim), BF16), jax.ShapeDtypeStruct((1, n), F32)],
        sem=("arbitrary",))(dy, w)


def _attn_bwd(probs, p_sinks, q, k, v, do):
    seq = q.shape[0]
    cur, prev = _kv_specs()

    def body(p_ref, ps_ref, q_ref, kc_ref, kp_ref, vc_ref, vp_ref, do_ref,
             dq_ref, dkc_ref, dkp_ref, dvc_ref, dvp_ref, dsink_ref):
        n = pl.program_id(0)

        @pl.when(n == 0)
        def _():
            dsink_ref[...] = jnp.zeros_like(dsink_ref)

        low = _low_lanes()
        lane = lax.broadcasted_iota(jnp.int32, (BLOCK, LANES), 1)
        p_sinks_blk = ps_ref[...]
        dk_all = jnp.zeros((2 * BLOCK, KV_DIM), F32)
        dv_all = jnp.zeros((2 * BLOCK, KV_DIM), F32)
        for j in range(N_KV_HEADS):
            k_lo, k_hi = _kv_low_high(kp_ref, kc_ref, j, low)
            v_lo, v_hi = _kv_low_high(vp_ref, vc_ref, j, low)
            dk_acc = jnp.zeros((2 * BLOCK, KV_DIM), F32)
            dv_acc = jnp.zeros((2 * BLOCK, KV_DIM), F32)
            for i in range(PAIRS_PER_KV):
                ls, h = _pair_lanes(j, i)
                qp = q_ref[:, ls]
                dop = do_ref[:, ls]
                dsb, pb16 = [], []
                for t, v_sel in enumerate((v_lo, v_hi)):
                    pb = p_ref[h + t]
                    pf = pb.astype(F32)
                    p_sink = jnp.sum(jnp.where(lane == h + t, p_sinks_blk, 0.0), axis=-1, keepdims=True)
                    dp = _nt(dop, v_sel)
                    delta = jnp.sum(pf * dp, axis=-1, keepdims=True)
                    dsb.append((pf * (dp - delta)).astype(BF16))
                    pb16.append(pb)
                    dsink_ref[h + t:h + t + 1, :] += jnp.broadcast_to(_sum_rows(p_sink * delta), (1, LANES))
                dq = (jnp.dot(dsb[0], k_lo, preferred_element_type=F32)
                      + jnp.dot(dsb[1], k_hi, preferred_element_type=F32))
                dq_ref[:, ls] = dq * (HEAD_DIM ** -0.5)
                dk_acc = dk_acc + jnp.where(low, _tn(dsb[0], qp), _tn(dsb[1], qp))
                dv_acc = dv_acc + jnp.where(low, _tn(pb16[0], dop), _tn(pb16[1], dop))
            dk_all = dk_all + _fold_pair_halves(dk_acc, j, low)
            dv_all = dv_all + _fold_pair_halves(dv_acc, j, low)
        dkp_ref[...] = dk_all[:BLOCK]
        dkc_ref[...] = dk_all[BLOCK:]
        dvp_ref[...] = dv_all[:BLOCK]
        dvc_ref[...] = dv_all[BLOCK:]

    kv_out = _rows(BLOCK, KV_DIM)
    kv_shape = jax.ShapeDtypeStruct((seq, KV_DIM), F32)
    return _call(
        body, name="attn_bwd", grid=(seq // BLOCK,),
        in_specs=[pl.BlockSpec((N_Q_HEADS, BLOCK, 2 * BLOCK), lambda n: (0, n, 0)), _rows(BLOCK, LANES),
                  _rows(BLOCK, Q_DIM), cur, prev, cur, prev, _rows(BLOCK, Q_DIM)],
        out_specs=[_rows(BLOCK, Q_DIM), kv_out, kv_out, kv_out, kv_out, _full((N_Q_HEADS, LANES))],
        out_shape=[jax.ShapeDtypeStruct((seq, Q_DIM), F32), kv_shape, kv_shape, kv_shape, kv_shape,
                   jax.ShapeDtypeStruct((N_Q_HEADS, LANES), F32)],
        sem=("arbitrary",))(probs, p_sinks, q, k, k, v, v, do)


def _rope_bwd(dq, dkc, dkp, dvc, dvp, rc, rsa, rsb):
    seq = dq.shape[0]
    nb = seq // BLOCK
    tm = min(TOKEN_TILE, seq)
    nt = seq // tm
    per = tm // BLOCK
    nxt = pl.BlockSpec((BLOCK, KV_DIM), lambda i: (jnp.minimum((i + 1) * per, nb - 1), 0))

    def body(dq_ref, dkc_ref, dkp_ref, dkn_ref, dvc_ref, dvp_ref, dvn_ref, c_ref, sa_ref, sb_ref, o_ref, db_ref):
        i = pl.program_id(0)

        @pl.when(i == 0)
        def _():
            db_ref[...] = jnp.zeros_like(db_ref)

        c, sa, sb = c_ref[...], sa_ref[...], sb_ref[...]
        last = i == nt - 1

        def from_next_block(prev_ref, next_ref):
            tail = jnp.where(last, 0.0, next_ref[...])
            return tail if per == 1 else jnp.concatenate([prev_ref[BLOCK:, :], tail], axis=0)

        dk = dkc_ref[...] + from_next_block(dkp_ref, dkn_ref)
        dv = dvc_ref[...] + from_next_block(dvp_ref, dvn_ref)
        for l in range(Q_DIM // LANES):
            ls = slice(l * LANES, (l + 1) * LANES)
            blk = _rope_t(dq_ref[:, ls], c, sa, sb)
            o_ref[:, ls] = blk.astype(BF16)
            db_ref[:, ls] += _sum_rows(blk)
        dkr = _rope_t(dk, c, sa, sb)
        o_ref[:, Q_DIM:Q_DIM + KV_DIM] = dkr.astype(BF16)
        db_ref[:, Q_DIM:Q_DIM + KV_DIM] += _sum_rows(dkr)
        o_ref[:, Q_DIM + KV_DIM:] = dv.astype(BF16)
        db_ref[:, Q_DIM + KV_DIM:] += _sum_rows(dv)

    kv = _rows(tm, KV_DIM)
    tab = _rows(tm, LANES)
    return _call(
        body, name="rope_bwd", grid=(nt,),
        in_specs=[_rows(tm, Q_DIM), kv, kv, nxt, kv, kv, nxt, tab, tab, tab],
        out_specs=[_rows(tm, QKV_DIM), _full((1, QKV_DIM))],
        out_shape=[jax.ShapeDtypeStruct((seq, QKV_DIM), BF16), jax.ShapeDtypeStruct((1, QKV_DIM), F32)],
        sem=("arbitrary",))(dq, dkc, dkp, dkp, dvc, dvp, dvp, rc, rsa, rsb)


def _adamw(w, g, m, v):
    m = ADAM_B1 * m + (1.0 - ADAM_B1) * g
    v = ADAM_B2 * v + (1.0 - ADAM_B2) * (g * g)
    m_hat = m / (1.0 - ADAM_B1 ** ADAM_STEP)
    v_hat = v / (1.0 - ADAM_B2 ** ADAM_STEP)
    delta = -ADAM_LR * (m_hat / (jnp.sqrt(v_hat) + ADAM_EPS) + ADAM_WD * w)
    return delta, m, v


def _sum_slots(name, parts):
    _, rows, cols = parts.shape
    tr = rows if rows <= 512 else ROWS_FF

    def body(p_ref, g_ref):
        g = p_ref[0].astype(F32)
        for d in range(1, N_DEV):
            g = g + p_ref[d].astype(F32)
        g_ref[...] = g

    return _call(
        body, name=name, grid=(rows // tr,),
        in_specs=[pl.BlockSpec((N_DEV, tr, cols), lambda i: (0, i, 0))],
        out_specs=_rows(tr, cols), out_shape=jax.ShapeDtypeStruct((rows, cols), F32),
        sem=("parallel",))(parts)


def _adamw_native(name, g, w, m, v):
    layers, rows, cols = w.shape
    tr = rows if rows <= 512 else 256

    def body(g_ref, w_ref, m_ref, v_ref, d_ref, nm_ref, nv_ref):
        d_ref[...], nm_ref[...], nv_ref[...] = _adamw(w_ref[...], g_ref[...], m_ref[...], v_ref[...])

    spec = pl.BlockSpec((1, tr, cols), lambda l, i: (l, i, 0))
    shape = jax.ShapeDtypeStruct(w.shape, F32)
    return _call(
        body, name=name, grid=(layers, rows // tr), in_specs=[spec, spec, spec, spec],
        out_specs=[spec, spec, spec], out_shape=[shape, shape, shape],
        sem=("parallel", "parallel"))(g, w, m, v)


def _adamw_replicated(parts, w, m, v):
    def body(p_ref, w_ref, m_ref, v_ref, g_ref, d_ref, nm_ref, nv_ref):
        g = p_ref[0]
        for j in range(1, N_DEV):
            g = g + p_ref[j]
        g_ref[...] = g
        d_ref[...], nm_ref[...], nv_ref[...] = _adamw(w_ref[...], g, m_ref[...], v_ref[...])

    spec = _full((REPL_ROWS, D_MODEL))
    shape = jax.ShapeDtypeStruct((REPL_ROWS, D_MODEL), F32)
    return _call(
        body, name="adamw_replicated", grid=(1,),
        in_specs=[_full((N_DEV, REPL_ROWS, D_MODEL)), spec, spec, spec],
        out_specs=[spec, spec, spec, spec], out_shape=[shape, shape, shape, shape],
        sem=("arbitrary",))(parts, w, m, v)


_MESH = pl.DeviceIdType.MESH
_ANY = pl.BlockSpec(memory_space=pl.ANY)


def _all_gather(name, xs):
    rows, cols = xs.shape

    def body(x_ref, out_ref, send_sems, recv_sems, local_sem):
        x, y, c = lax.axis_index("x"), lax.axis_index("y"), lax.axis_index("c")
        me, sibling = (x, y, c), (x, y, 1 - c)
        chips = [(1 - x, y), (x, 1 - y), (1 - x, 1 - y)]

        def slot(px, py, pc):
            return out_ref.at[4 * px + 2 * py + pc]

        def copy(k, block, to, src=None):
            return pltpu.make_async_remote_copy(
                src_ref=slot(*block) if src is None else src, dst_ref=slot(*block),
                send_sem=send_sems.at[k], recv_sem=recv_sems.at[k], device_id=to, device_id_type=_MESH)

        mine = pltpu.make_async_copy(x_ref, slot(*me), local_sem)
        mine.start()
        first = [copy(0, me, sibling, src=x_ref)]
        first += [copy(1 + j, me, (*chip, c), src=x_ref) for j, chip in enumerate(chips)]
        for cp in first:
            cp.start()
        passed = [copy(4 + j, (*chip, c), sibling) for j, chip in enumerate(chips)]
        for j, chip in enumerate(chips):
            copy(1 + j, (*chip, c), me).wait_recv()
            passed[j].start()
        copy(0, sibling, me).wait_recv()
        for j, chip in enumerate(chips):
            copy(4 + j, (*chip, 1 - c), me).wait_recv()
        for cp in first + passed:
            cp.wait_send()
        mine.wait()

    return pl.pallas_call(
        body, name=name, out_shape=jax.ShapeDtypeStruct((N_DEV, rows, cols), xs.dtype),
        in_specs=[_ANY], out_specs=_ANY,
        scratch_shapes=[pltpu.SemaphoreType.DMA((7,)), pltpu.SemaphoreType.DMA((7,)), pltpu.SemaphoreType.DMA],
    )(xs)


N_PEERS = N_DEV - 1
_HBM = pl.BlockSpec(memory_space=pltpu.HBM)
_SEM = pl.BlockSpec(memory_space=pltpu.SEMAPHORE)
_DATAFLOW = pltpu.SideEffectType.DATAFLOW_SIDE_EFFECTING
_TOKEN = jax.ShapeDtypeStruct((SUBLANES, LANES), F32)


def _peers():
    x, y, c = lax.axis_index("x"), lax.axis_index("y"), lax.axis_index("c")
    out = []
    for k in range(1, N_DEV):
        px = 1 - x if k & 4 else x
        py = 1 - y if k & 2 else y
        pc = 1 - c if k & 1 else c
        out.append(((px, py, pc), 4 * px + 2 * py + pc))
    return 4 * x + 2 * y + c, out


def _in_hbm(a):
    return pltpu.with_memory_space_constraint(a, pltpu.HBM)


def _landing(rows):
    return _in_hbm(lax.empty((N_DEV, rows, D_MODEL), BF16))


def _sem_pair():
    return pltpu.SemaphoreType.DMA((N_PEERS,)), pltpu.SemaphoreType.DMA((N_PEERS,))


def _gather_start(name, payloads):
    n = len(payloads)

    def body(*refs):
        src, land = refs[:n], refs[n:2 * n]
        sems = refs[2 * n:4 * n]
        token = refs[-1]
        me, peers = _peers()
        for g in range(n):
            for k, (pos, _) in enumerate(peers):
                pltpu.make_async_remote_copy(
                    src_ref=src[g], dst_ref=land[g].at[me], send_sem=sems[2 * g].at[k],
                    recv_sem=sems[2 * g + 1].at[k], device_id=pos, device_id_type=_MESH).start()
        token[...] = jnp.zeros_like(token)

    lands = [_landing(p.shape[0]) for p in payloads]
    sem_shapes = [s for _ in payloads for s in _sem_pair()]
    hbm_shapes = [pltpu.HBM(a.shape, a.dtype) for a in list(payloads) + lands]
    out = pl.pallas_call(
        body, name=name, out_shape=(*sem_shapes, *hbm_shapes, _TOKEN),
        in_specs=[_HBM] * (2 * n), out_specs=(*[_SEM] * (2 * n), *[_HBM] * (2 * n), pl.BlockSpec(memory_space=pltpu.VMEM)),
        input_output_aliases={i: 2 * n + i for i in range(2 * n)},
        compiler_params=pltpu.CompilerParams(has_side_effects=_DATAFLOW),
    )(*[_in_hbm(p) for p in payloads], *lands)
    sems, thru = out[:2 * n], out[2 * n:4 * n]
    return [(thru[g], thru[n + g], sems[2 * g], sems[2 * g + 1]) for g in range(n)], out[-1]


def _gather_wait(name, group, after):
    payload, land, send_sems, recv_sems = group

    def body(src_ref, land_ref, send_ref, recv_ref, after_ref, src_out, land_out):
        _, peers = _peers()
        for k, (pos, idx) in enumerate(peers):
            cp = pltpu.make_async_remote_copy(
                src_ref=src_ref, dst_ref=land_ref.at[idx], send_sem=send_ref.at[k], recv_sem=recv_ref.at[k],
                device_id=pos, device_id_type=_MESH)
            cp.wait_send()
            cp.wait_recv()

    _, land = pl.pallas_call(
        body, name=name, out_shape=(pltpu.HBM(payload.shape, payload.dtype), pltpu.HBM(land.shape, land.dtype)),
        in_specs=[_HBM, _HBM, _SEM, _SEM, _ANY], out_specs=(_HBM, _HBM), input_output_aliases={0: 0, 1: 1},
        compiler_params=pltpu.CompilerParams(has_side_effects=_DATAFLOW),
    )(payload, land, send_sems, recv_sems, after)
    me = 4 * lax.axis_index("x") + 2 * lax.axis_index("y") + lax.axis_index("c")
    return lax.dynamic_update_slice(land, payload[None], (me, 0, 0))


def _scatter_start(name, blocks):
    rows = blocks.shape[1]

    def body(blocks_ref, land_ref, send_sems, recv_sems, blocks_out, land_out, token):
        me, peers = _peers()
        for k, (pos, idx) in enumerate(peers):
            pltpu.make_async_remote_copy(
                src_ref=blocks_ref.at[idx], dst_ref=land_ref.at[me], send_sem=send_sems.at[k],
                recv_sem=recv_sems.at[k], device_id=pos, device_id_type=_MESH).start()
        token[...] = jnp.zeros_like(token)

    land = _landing(rows)
    send_sems, recv_sems, blocks_thru, land_thru, token = pl.pallas_call(
        body, name=name,
        out_shape=(*_sem_pair(), pltpu.HBM(blocks.shape, blocks.dtype), pltpu.HBM(land.shape, land.dtype), _TOKEN),
        in_specs=[_HBM, _HBM], out_specs=(_SEM, _SEM, _HBM, _HBM, pl.BlockSpec(memory_space=pltpu.VMEM)),
        input_output_aliases={0: 2, 1: 3},
        compiler_params=pltpu.CompilerParams(has_side_effects=_DATAFLOW),
    )(_in_hbm(blocks), land)
    return (blocks_thru, land_thru, send_sems, recv_sems), token


def _scatter_wait(groups, after):
    n = len(groups)

    def body(*refs):
        blocks, land = refs[:n], refs[n:2 * n]
        sems = refs[2 * n:4 * n]
        _, peers = _peers()
        for g in range(n):
            for k, (pos, idx) in enumerate(peers):
                cp = pltpu.make_async_remote_copy(
                    src_ref=blocks[g].at[idx], dst_ref=land[g].at[idx], send_sem=sems[2 * g].at[k],
                    recv_sem=sems[2 * g + 1].at[k], device_id=pos, device_id_type=_MESH)
                cp.wait_send()
                cp.wait_recv()

    hbm = [grp[0] for grp in groups] + [grp[1] for grp in groups]
    sems = [s for grp in groups for s in grp[2:]]
    out = pl.pallas_call(
        body, name="rs_wait", out_shape=tuple(pltpu.HBM(a.shape, a.dtype) for a in hbm),
        in_specs=[_HBM] * (2 * n) + [_SEM] * (2 * n) + [_ANY], out_specs=tuple([_HBM] * (2 * n)),
        input_output_aliases={i: i for i in range(2 * n)},
        compiler_params=pltpu.CompilerParams(has_side_effects=_DATAFLOW),
    )(*hbm, *sems, after)
    me = 4 * lax.axis_index("x") + 2 * lax.axis_index("y") + lax.axis_index("c")
    lands = []
    for g in range(n):
        own = lax.dynamic_index_in_dim(out[g], me, axis=0, keepdims=True)
        lands.append(lax.dynamic_update_slice(out[n + g], own, (me, 0, 0)))
    return lands


def _pad_rows(flat, rows):
    return jnp.pad(flat, (0, rows * D_MODEL - flat.shape[0])).reshape(rows, D_MODEL)


SMALL_NAMES = ("conv_b_pw1", "conv_w_dw", "conv_b_dw", "conv_ln_g", "conv_ln_b", "conv_b_pw2")


def _pack_small(p):
    flat = jnp.concatenate([p[n].reshape(-1) for n in SMALL_NAMES])
    return _pad_rows(flat, ROWS_SMALL).reshape(1, ROWS_SMALL, D_MODEL)


def _unpack_small(packed):
    flat = packed.reshape(-1)
    c = D_MODEL // N_DEV
    shapes = ((1, 2 * c), (1, CONV_WIDTH, c), (1, c), (1, c), (1, c), (1, c))
    out, o = {}, 0
    for n, shape in zip(SMALL_NAMES, shapes):
        size = shape[-1] * (shape[1] if len(shape) == 3 else 1)
        out[n] = flat[o:o + size].reshape(shape)
        o += size
    return out


def _gather_payloads(p):
    t = lambda a: jnp.swapaxes(a, -1, -2).astype(BF16)
    w1t, w3t, w2 = t(p["ffn_w1"]), t(p["ffn_w3"]), p["ffn_w2"].astype(BF16)
    small = _pack_small(p).reshape(-1)[:ROWS_SMALL * D_MODEL // 2]
    small = lax.bitcast_convert_type(small, BF16).reshape(ROWS_SMALL, D_MODEL)
    conv = jnp.concatenate([t(p["conv_w_pw1"][0]), p["conv_w_pw2"][0].astype(BF16), small], axis=0)
    ffn = [jnp.concatenate([w1t[l], w3t[l], w2[l]], axis=0) for l in range(2)]
    return [t(p["attn_w_qkv"][0]), p["attn_w_o"][0].astype(BF16), ffn[0], conv, ffn[1]]


def _device_rows(land, lo, n):
    return land[:, lo:lo + n].reshape(N_DEV * n, D_MODEL)


def _unpack_conv(land):
    small = lax.bitcast_convert_type(
        land[:, ROWS_PW1 + ROWS_PW2:].reshape(N_DEV, ROWS_SMALL * D_MODEL // 2, 2), F32)
    c = D_MODEL // N_DEV
    b_pw1 = small[:, :2 * c].reshape(1, 2 * D_MODEL)
    s = 2 * c
    w_dw = small[:, s:s + CONV_WIDTH * c].reshape(N_DEV, CONV_WIDTH, c).transpose(1, 0, 2).reshape(CONV_WIDTH, D_MODEL)
    s += CONV_WIDTH * c
    b_dw, ln_g, ln_b, b_pw2 = (small[:, s + i * c:s + (i + 1) * c].reshape(1, D_MODEL) for i in range(4))
    return dict(w_pw1_t=_device_rows(land, 0, ROWS_PW1), w_pw2=_device_rows(land, ROWS_PW1, ROWS_PW2),
                b_pw1=b_pw1, w_dw=w_dw, b_dw=b_dw, ln_g=ln_g, ln_b=ln_b, b_pw2=b_pw2)


def _dest_blocks(mats):
    return jnp.concatenate([a.reshape(N_DEV, -1, D_MODEL) for a in mats], axis=1)


def _small_grad_rows(g_bpw1, g_dw, g_bdw, g_lng, g_lnb, g_bpw2):
    c = D_MODEL // N_DEV
    small = jnp.concatenate(
        [g_bpw1.reshape(N_DEV, 2 * c), g_dw.reshape(CONV_WIDTH, N_DEV, c).transpose(1, 0, 2).reshape(N_DEV, -1),
         g_bdw.reshape(N_DEV, c), g_lng.reshape(N_DEV, c), g_lnb.reshape(N_DEV, c), g_bpw2.reshape(N_DEV, c)], axis=1)
    small = jnp.pad(small, ((0, 0), (0, ROWS_SMALL * D_MODEL - SMALL_USED)))
    return small.reshape(N_DEV * ROWS_SMALL, D_MODEL).astype(BF16)


LOSS_ROW = 9


def _pack_replicated(norm_mix, norm_ffn, b_qkv, sinks, b_o, norm_final, extra=None):
    rows = [norm_mix.reshape(2, D_MODEL), norm_ffn.reshape(2, D_MODEL), _pad_rows(b_qkv.reshape(-1), 2),
            _pad_rows(sinks.reshape(-1), 1), b_o.reshape(1, D_MODEL), norm_final.reshape(1, D_MODEL)]
    if extra is not None:
        rows.append(_pad_rows(extra.reshape(-1), 1))
    p = jnp.concatenate(rows, axis=0)
    return jnp.pad(p, ((0, REPL_ROWS - p.shape[0]), (0, 0)))


def _unpack_replicated(p):
    return dict(norm_mix=p[0:2], norm_ffn=p[2:4], attn_b_qkv=p[4:6].reshape(-1)[:QKV_DIM].reshape(1, QKV_DIM),
                attn_sinks=p[6, :N_Q_HEADS].reshape(1, N_Q_HEADS), attn_b_o=p[7:8], norm_final=p[8])


WEIGHT_ORDER = ['norm_mix', 'norm_ffn', 'attn_w_qkv', 'attn_b_qkv', 'attn_sinks', 'attn_w_o', 'attn_b_o',
                'conv_w_pw1', 'conv_b_pw1', 'conv_w_dw', 'conv_b_dw', 'conv_ln_g', 'conv_ln_b', 'conv_w_pw2',
                'conv_b_pw2', 'ffn_w1', 'ffn_w3', 'ffn_w2', 'norm_final']


def kernel(x, norm_mix, norm_ffn, attn_w_qkv, attn_b_qkv, attn_sinks, attn_w_o, attn_b_o, conv_w_pw1, conv_b_pw1, conv_w_dw, conv_b_dw, conv_ln_g, conv_ln_b, conv_w_pw2, conv_b_pw2, ffn_w1, ffn_w3, ffn_w2, norm_final, loss_target, m_norm_mix, m_norm_ffn, m_attn_w_qkv, m_attn_b_qkv, m_attn_sinks, m_attn_w_o, m_attn_b_o, m_conv_w_pw1, m_conv_b_pw1, m_conv_w_dw, m_conv_b_dw, m_conv_ln_g, m_conv_ln_b, m_conv_w_pw2, m_conv_b_pw2, m_ffn_w1, m_ffn_w3, m_ffn_w2, m_norm_final, v_norm_mix, v_norm_ffn, v_attn_w_qkv, v_attn_b_qkv, v_attn_sinks, v_attn_w_o, v_attn_b_o, v_conv_w_pw1, v_conv_b_pw1, v_conv_w_dw, v_conv_b_dw, v_conv_ln_g, v_conv_ln_b, v_conv_w_pw2, v_conv_b_pw2, v_ffn_w1, v_ffn_w3, v_ffn_w2, v_norm_final):
    xs = x[0]
    target = loss_target[0]
    seq = xs.shape[0]
    my_x, my_y, my_c = lax.axis_index("x"), lax.axis_index("y"), lax.axis_index("c")

    w = dict(attn_w_qkv=attn_w_qkv, attn_w_o=attn_w_o, conv_w_pw1=conv_w_pw1, conv_b_pw1=conv_b_pw1,
             conv_w_dw=conv_w_dw, conv_b_dw=conv_b_dw, conv_ln_g=conv_ln_g, conv_ln_b=conv_ln_b,
             conv_w_pw2=conv_w_pw2, conv_b_pw2=conv_b_pw2, ffn_w1=ffn_w1, ffn_w3=ffn_w3, ffn_w2=ffn_w2)
    m = dict(attn_w_qkv=m_attn_w_qkv, attn_w_o=m_attn_w_o, conv_w_pw1=m_conv_w_pw1, conv_b_pw1=m_conv_b_pw1,
             conv_w_dw=m_conv_w_dw, conv_b_dw=m_conv_b_dw, conv_ln_g=m_conv_ln_g, conv_ln_b=m_conv_ln_b,
             conv_w_pw2=m_conv_w_pw2, conv_b_pw2=m_conv_b_pw2, ffn_w1=m_ffn_w1, ffn_w3=m_ffn_w3, ffn_w2=m_ffn_w2)
    v = dict(attn_w_qkv=v_attn_w_qkv, attn_w_o=v_attn_w_o, conv_w_pw1=v_conv_w_pw1, conv_b_pw1=v_conv_b_pw1,
             conv_w_dw=v_conv_w_dw, conv_b_dw=v_conv_b_dw, conv_ln_g=v_conv_ln_g, conv_ln_b=v_conv_ln_b,
             conv_w_pw2=v_conv_w_pw2, conv_b_pw2=v_conv_b_pw2, ffn_w1=v_ffn_w1, ffn_w3=v_ffn_w3, ffn_w2=v_ffn_w2)

    payloads = _gather_payloads(w)
    (ag_qkv, ag_wo), tok = _gather_start("ag_start_attn", payloads[:2])
    (ag_ffn0, ag_conv, ag_ffn1), tok = _gather_start(
        "ag_start_rest", [payloads[2] + tok[0, 0].astype(BF16), payloads[3], payloads[4]])
    rc, rsa, rsb = _rope_tables(seq)
    sinks = attn_sinks.reshape(N_Q_HEADS)
    g_mix0, g_mix1 = norm_mix[0:1] + tok[0, 0], norm_mix[1:2]
    g_ffn0, g_ffn1 = norm_ffn[0:1], norm_ffn[1:2]
    g_fin = norm_final.reshape(1, D_MODEL)

    w_qkv_t = _gather_wait("ag_wait_qkv", ag_qkv, tok).reshape(QKV_DIM, D_MODEL)
    y0, q, k, vv = _qkv_fwd(xs, g_mix0, w_qkv_t, attn_b_qkv, rc, rsa, rsb)
    attn, probs, p_sinks = _attn_fwd(sinks, q, k, vv)
    w_o = _gather_wait("ag_wait_wo", ag_wo, attn).reshape(Q_DIM, D_MODEL)
    h1 = _mm_res("attn_out_proj", attn, w_o, attn_b_o, xs)
    w_ffn0 = _gather_wait("ag_wait_ffn0", ag_ffn0, h1)
    f0, u0, p0, s0 = _ffn_up("ffn0_up", h1, g_ffn0, w_ffn0)
    h2 = _ffn_down("ffn0_down", s0, w_ffn0, h1)
    wt = _unpack_conv(_gather_wait("ag_wait_conv", ag_conv, h2))
    wd = jnp.concatenate([wt["w_dw"][::-1], jnp.zeros((TAPS_PAD - CONV_WIDTH, D_MODEL), F32)], axis=0)
    y1, a, dwc, z = _conv_fwd(h2, g_mix1, wt["w_pw1_t"], wt["b_pw1"], wd, wt["b_dw"], wt["ln_g"], wt["ln_b"])
    h3 = _mm_res("conv_out_proj", z, wt["w_pw2"], wt["b_pw2"], h2)
    w_ffn1 = _gather_wait("ag_wait_ffn1", ag_ffn1, h3)
    f1, u1, p1, s1 = _ffn_up("ffn1_up", h3, g_ffn1, w_ffn1)
    dh4, dh4b, sq, dg_fin = _ffn_down_loss(s1, w_ffn1, h3, target, g_fin)

    du1, dp1 = _ffn_bwd_act("ffn1_bwd_act", dh4b, u1, p1, w_ffn1)
    dh3, dh3b, dg_ffn1 = _ffn_bwd_in("ffn1_bwd_in", du1, dp1, w_ffn1, h3, dh4, g_ffn1)
    blocks = _mm_tn("ffn1_dw2", s1, dh4b, tk=FF_TILE, into=(None, W2_SLOT))
    blocks = _mm_tn("ffn1_dw1", du1, f1, tk=FF_TILE, into=(blocks, W1T_SLOT))
    blocks = _mm_tn("ffn1_dw3", dp1, f1, tk=FF_TILE, into=(blocks, W3T_SLOT))
    rs_ffn1, tok = _scatter_start("rs_start_ffn1", blocks)

    da, dlg, dlb, dbdw, dwd, dbpw1, dbpw2 = _conv_bwd(dh3, wt["w_pw2"], dwc, a, wt["ln_g"] + tok[0, 0],
                                                     wt["ln_b"], wd)
    gpw2 = _mm_tn("conv_dw_pw2", z, dh3b, tk=D_MODEL)
    dh2, dh2b, dg_mix1 = _mm_rms_bwd("conv_in_bwd", da, wt["w_pw1_t"], h2, dh3, g_mix1, True)
    gpw1t = _mm_tn("conv_dw_pw1", da, y1, tk=D_MODEL)
    small_rows = _small_grad_rows(dbpw1, dwd[:CONV_WIDTH][::-1], dbdw, dlg, dlb, dbpw2)
    rs_conv, tok = _scatter_start("rs_start_conv", _dest_blocks([gpw1t, gpw2, small_rows]))

    du0, dp0 = _ffn_bwd_act("ffn0_bwd_act", dh2b, u0, p0, w_ffn0)
    dh1, dh1b, dg_ffn0 = _ffn_bwd_in("ffn0_bwd_in", du0, dp0, w_ffn0, h1, dh2, g_ffn0 + tok[0, 0])
    blocks = _mm_tn("ffn0_dw2", s0, dh2b, tk=FF_TILE, into=(None, W2_SLOT))
    blocks = _mm_tn("ffn0_dw1", du0, f0, tk=FF_TILE, into=(blocks, W1T_SLOT))
    blocks = _mm_tn("ffn0_dw3", dp0, f0, tk=FF_TILE, into=(blocks, W3T_SLOT))
    rs_ffn0, tok = _scatter_start("rs_start_ffn0", blocks)

    gwo = _mm_tn("attn_dw_o", attn, dh1b, tk=D_MODEL)
    rs_wo, tok2 = _scatter_start("rs_start_wo", _dest_blocks([gwo]))
    dattn, dbo = _nt_bias("attn_out_bwd", dh1, w_o)
    dq, dkc, dkp, dvc, dvp, dsink = _attn_bwd(probs, p_sinks + (tok[0, 0] + tok2[0, 0]), q, k, vv, dattn)
    dqkv, dbqkv = _rope_bwd(dq, dkc, dkp, dvc, dvp, rc, rsa, rsb)
    gqkvt = _mm_tn("attn_dw_qkv", dqkv, y0, tk=QKV_DIM)
    rs_qkv, tok = _scatter_start("rs_start_qkv", _dest_blocks([gqkvt]))
    dx, dg_mix0 = _mm_rms_bwd("qkv_in_bwd", dqkv, w_qkv_t, xs, dh1, g_mix0 + tok[0, 0], False)

    p_ffn1, p_conv, p_ffn0, p_wo, p_qkv = _scatter_wait([rs_ffn1, rs_conv, rs_ffn0, rs_wo, rs_qkv], dx)
    g_ffn = [_sum_slots("rs_sum_ffn0", p_ffn0), _sum_slots("rs_sum_ffn1", p_ffn1)]
    g_conv = _sum_slots("rs_sum_conv", p_conv)
    g_wo = _sum_slots("rs_sum_wo", p_wo)
    g_qkv = _sum_slots("rs_sum_qkv", p_qkv)

    def ff(slot):
        return jnp.stack([g[slot * ROWS_FF:(slot + 1) * ROWS_FF] for g in g_ffn])

    tr = lambda a: jnp.swapaxes(a, -1, -2)
    grads = dict(
        ffn_w1=ff(W1T_SLOT), ffn_w3=ff(W3T_SLOT), ffn_w2=ff(W2_SLOT), attn_w_qkv=g_qkv[None], attn_w_o=g_wo[None],
        conv_w_pw1=g_conv[:ROWS_PW1].T[None], conv_w_pw2=g_conv[ROWS_PW1:ROWS_PW1 + ROWS_PW2][None])
    transposed = ("ffn_w1", "ffn_w3", "attn_w_qkv")
    sharded = [{}, {}, {}, {}]
    for n in grads:
        view = tr if n in transposed else (lambda a: a)
        outs_n = _adamw_native("adamw_" + n, grads[n], view(w[n]), view(m[n]), view(v[n]))
        for dst, t in zip(sharded, (grads[n],) + tuple(outs_n)):
            dst[n] = view(t)
    g_small = g_conv[ROWS_PW1 + ROWS_PW2:][None]
    small_out = _adamw_native("adamw_small", g_small, _pack_small(w), _pack_small(m), _pack_small(v))
    for dst, t in zip(sharded, (g_small,) + tuple(small_out)):
        dst.update(_unpack_small(t))

    part = _pack_replicated(jnp.concatenate([dg_mix0, dg_mix1]), jnp.concatenate([dg_ffn0, dg_ffn1]),
                            dbqkv, -dsink[:, 0], dbo, dg_fin, extra=sq[0, 0:1])
    parts = _all_gather("ag_replicated_grads", part)
    w_rep = _pack_replicated(norm_mix, norm_ffn, attn_b_qkv, attn_sinks, attn_b_o, norm_final)
    m_rep = _pack_replicated(m_norm_mix, m_norm_ffn, m_attn_b_qkv, m_attn_sinks, m_attn_b_o, m_norm_final)
    v_rep = _pack_replicated(v_norm_mix, v_norm_ffn, v_attn_b_qkv, v_attn_sinks, v_attn_b_o, v_norm_final)
    rep_out = _adamw_replicated(parts, w_rep, m_rep, v_rep)
    replicated = [_unpack_replicated(t) for t in rep_out]
    loss = rep_out[0][LOSS_ROW, 0] * (0.5 / D_MODEL)

    outs = [loss, dx.reshape(1, seq, D_MODEL)]
    for sh, rp in zip(sharded, replicated):
        merged = {**sh, **rp}
        outs += [merged[n] for n in WEIGHT_ORDER]
    return tuple(outs)
```

```python
import jax
import jax.numpy as jnp
from jax import lax
from jax.experimental import pallas as pl
from jax.experimental.pallas import tpu as pltpu

F32 = jnp.float32
BF16 = jnp.bfloat16

D_MODEL = 1024
HEAD_DIM = 64
N_Q_HEADS = 16
N_KV_HEADS = 2
Q_PER_KV = 8
Q_DIM = N_Q_HEADS * HEAD_DIM
KV_DIM = N_KV_HEADS * HEAD_DIM
QKV_DIM = Q_DIM + 2 * KV_DIM
BLOCK = 128
CONV_WIDTH = 31
D_FF = 2816
ROPE_THETA = 10000.0
RMS_EPS = 1e-5
LN_EPS = 1e-5
ADAM_LR = 0.001
ADAM_B1 = 0.9
ADAM_B2 = 0.999
ADAM_EPS = 1e-08
ADAM_WD = 0.01
ADAM_STEP = 10
N_DEV = 8

LANES = 128
SUBLANES = 8
TOKEN_TILE = 512
CONV_CHUNK = 64
CONV_HALO = 32
TAPS_PAD = 32
VMEM_LIMIT = 56 * 1024 * 1024
NEG_INF = float(jnp.finfo(jnp.float32).min)

ROWS_FF = D_FF // N_DEV
W1T_SLOT, W3T_SLOT, W2_SLOT = 0, 1, 2
ROWS_QKV = QKV_DIM // N_DEV
ROWS_WO = Q_DIM // N_DEV
ROWS_PW1 = 2 * D_MODEL // N_DEV
ROWS_PW2 = D_MODEL // N_DEV
ROWS_SMALL = 16
SMALL_USED = 2 * D_MODEL // N_DEV + CONV_WIDTH * (D_MODEL // N_DEV) + 4 * (D_MODEL // N_DEV)
FF_SPLIT = 2
FF_TILE_DEVS = N_DEV // FF_SPLIT
FF_TILE = FF_TILE_DEVS * ROWS_FF
REPL_ROWS = 16


def _call(body, *, name, grid, in_specs, out_specs, out_shape, scratch=(), sem=None, aliases=None):
    return pl.pallas_call(
        body, name=name, grid=grid, in_specs=in_specs, out_specs=out_specs, out_shape=out_shape,
        scratch_shapes=list(scratch), input_output_aliases=aliases or {},
        compiler_params=pltpu.CompilerParams(dimension_semantics=sem, vmem_limit_bytes=VMEM_LIMIT))


def _full(shape):
    return pl.BlockSpec(shape, lambda *_: (0,) * len(shape))


def _resident(shape):
    return pl.BlockSpec(shape, lambda *_: (0,) * len(shape), pipeline_mode=pl.Buffered(1))


def _rows(tm, n):
    return pl.BlockSpec((tm, n), lambda i, *_: (i, 0))


def _sig(x):
    return 1.0 / (1.0 + jnp.exp(-x))


def _sum_rows(x):
    return jnp.sum(x, axis=0, keepdims=True)


def _nt(a, b):
    return lax.dot_general(a, b, (((1,), (1,)), ((), ())), preferred_element_type=F32)


def _tn(a, b):
    return lax.dot_general(a, b, (((0,), (0,)), ((), ())), preferred_element_type=F32)


def _rms_stats(x):
    return lax.rsqrt(jnp.mean(x * x, axis=-1, keepdims=True) + RMS_EPS)


def _rms_bwd(dy, x, g, dres):
    r = _rms_stats(x)
    n = x * r
    dn = dy * g
    dx = dres + r * (dn - n * jnp.mean(dn * n, axis=-1, keepdims=True))
    return dx, _sum_rows(dy * n)


def _rope_tables(seq):
    half = HEAD_DIM // 2
    pos = jnp.arange(seq, dtype=F32)
    lane = jnp.arange(LANES)
    inv_freq = ROPE_THETA ** (-(2 * (lane % half)).astype(F32) / HEAD_DIM)
    ang = pos[:, None] * inv_freq[None, :]
    cos, sin = jnp.cos(ang), jnp.sin(ang)
    first_half = (lane % HEAD_DIM < half)[None, :]
    sa = jnp.where(first_half, -sin, 0.0)
    sb = jnp.where(first_half, 0.0, sin)
    return cos, sa, sb


def _rope(t, c, sa, sb):
    half = HEAD_DIM // 2
    return t * c + pltpu.roll(t, LANES - half, 1) * sa + pltpu.roll(t, half, 1) * sb


def _rope_t(dt, c, sa, sb):
    half = HEAD_DIM // 2
    return dt * c + pltpu.roll(dt * sa, half, 1) + pltpu.roll(dt * sb, LANES - half, 1)


def _qkv_fwd(x, g, w, b, rc, rsa, rsb):
    seq = x.shape[0]
    tm = min(TOKEN_TILE, seq)

    def body(x_ref, g_ref, w_ref, b_ref, c_ref, sa_ref, sb_ref, y_ref, q_ref, k_ref, v_ref):
        xv = x_ref[...]
        y = (xv * _rms_stats(xv) * g_ref[...]).astype(BF16)
        y_ref[...] = y
        qkv = _nt(y, w_ref[...]) + b_ref[...]
        c, sa, sb = c_ref[...], sa_ref[...], sb_ref[...]
        for i in range(Q_DIM // LANES):
            blk = _rope(qkv[:, i * LANES:(i + 1) * LANES], c, sa, sb)
            q_ref[:, i * LANES:(i + 1) * LANES] = (blk * (HEAD_DIM ** -0.5)).astype(BF16)
        k_ref[...] = _rope(qkv[:, Q_DIM:Q_DIM + KV_DIM], c, sa, sb).astype(BF16)
        v_ref[...] = qkv[:, Q_DIM + KV_DIM:].astype(BF16)

    return _call(
        body, name="qkv_fwd", grid=(seq // tm,),
        in_specs=[_rows(tm, D_MODEL), _full((1, D_MODEL)), _full((QKV_DIM, D_MODEL)), _full((1, QKV_DIM)),
                  _rows(tm, LANES), _rows(tm, LANES), _rows(tm, LANES)],
        out_specs=[_rows(tm, D_MODEL), _rows(tm, Q_DIM), _rows(tm, KV_DIM), _rows(tm, KV_DIM)],
        out_shape=[jax.ShapeDtypeStruct((seq, D_MODEL), BF16), jax.ShapeDtypeStruct((seq, Q_DIM), BF16),
                   jax.ShapeDtypeStruct((seq, KV_DIM), BF16), jax.ShapeDtypeStruct((seq, KV_DIM), BF16)],
        sem=("parallel",))(x, g, w, b, rc, rsa, rsb)


GROUP_ROWS = Q_PER_KV * BLOCK


def _band_mask(n, rows=GROUP_ROWS):
    row = lax.broadcasted_iota(jnp.int32, (rows, 2 * BLOCK), 0) & (BLOCK - 1)
    col = lax.broadcasted_iota(jnp.int32, (rows, 2 * BLOCK), 1)
    rel = row + BLOCK - col
    return (rel >= 0) & (rel < BLOCK) & ((col >= BLOCK) | (n > 0))


def _softmax_with_sink(s, mask, sink):
    s = jnp.where(mask, s, NEG_INF)
    m = jnp.maximum(jnp.max(s, axis=-1, keepdims=True), sink)
    p = jnp.exp(s - m)
    e_sink = jnp.exp(sink - m)
    inv = 1.0 / (jnp.sum(p, axis=-1, keepdims=True) + e_sink)
    return p * inv, e_sink * inv


PAIRS_PER_KV = Q_PER_KV // 2


def _kv_specs():
    cur = pl.BlockSpec((BLOCK, KV_DIM), lambda n: (n, 0))
    prev = pl.BlockSpec((BLOCK, KV_DIM), lambda n: (jnp.maximum(n - 1, 0), 0))
    return cur, prev


def _low_lanes():
    return lax.broadcasted_iota(jnp.int32, (2 * BLOCK, KV_DIM), 1) < HEAD_DIM


def _kv_low_high(prev_ref, cur_ref, j, low):
    both = jnp.concatenate([prev_ref[...], cur_ref[...]], axis=0).astype(F32)
    swapped = pltpu.roll(both, HEAD_DIM, 1)
    at_low, at_high = (both, swapped) if j == 0 else (swapped, both)
    return jnp.where(low, at_low, 0.0).astype(BF16), jnp.where(low, 0.0, at_high).astype(BF16)


def _fold_pair_halves(acc, j, low):
    folded = acc + pltpu.roll(acc, HEAD_DIM, 1)
    return jnp.where(low, folded, 0.0) if j == 0 else jnp.where(low, 0.0, folded)


def _pair_lanes(j, i):
    g = j * PAIRS_PER_KV + i
    return slice(g * LANES, (g + 1) * LANES), 2 * g


def _attn_fwd(sinks, q, k, v):
    seq = q.shape[0]
    cur, prev = _kv_specs()

    def body(sink_ref, q_ref, kc_ref, kp_ref, vc_ref, vp_ref, o_ref, p_ref, ps_ref):
        mask = _band_mask(pl.program_id(0), BLOCK)
        lane = lax.broadcasted_iota(jnp.int32, (BLOCK, LANES), 1)
        p_sinks = jnp.zeros((BLOCK, LANES), F32)
        for j in range(N_KV_HEADS):
            cs = slice(j * HEAD_DIM, (j + 1) * HEAD_DIM)
            kk = jnp.concatenate([kp_ref[:, cs], kc_ref[:, cs]], axis=0)
            vv = jnp.concatenate([vp_ref[:, cs], vc_ref[:, cs]], axis=0)
            for gq in range(Q_PER_KV):
                h = j * Q_PER_KV + gq
                hs = slice(h * HEAD_DIM, (h + 1) * HEAD_DIM)
                probs, p_sink = _softmax_with_sink(_nt(q_ref[:, hs], kk), mask, sink_ref[h])
                pb = probs.astype(BF16)
                p_ref[h] = pb
                p_sinks = jnp.where(lane == h, p_sink, p_sinks)
                o_ref[:, hs] = jnp.dot(pb, vv, preferred_element_type=F32).astype(BF16)
        ps_ref[...] = p_sinks

    return _call(
        body, name="attn_fwd", grid=(seq // BLOCK,),
        in_specs=[pl.BlockSpec(memory_space=pltpu.SMEM), _rows(BLOCK, Q_DIM), cur, prev, cur, prev],
        out_specs=[_rows(BLOCK, Q_DIM), pl.BlockSpec((N_Q_HEADS, BLOCK, 2 * BLOCK), lambda n: (0, n, 0)),
                   _rows(BLOCK, LANES)],
        out_shape=[jax.ShapeDtypeStruct((seq, Q_DIM), BF16),
                   jax.ShapeDtypeStruct((N_Q_HEADS, seq, 2 * BLOCK), BF16),
                   jax.ShapeDtypeStruct((seq, LANES), F32)],
        sem=("parallel",))(sinks, q, k, k, v, v)


def _mm_res(name, a, w, b, res):
    seq, kdim = a.shape
    n = w.shape[1]
    tm = min(2 * TOKEN_TILE, seq)

    def body(a_ref, w_ref, b_ref, r_ref, o_ref):
        o_ref[...] = r_ref[...] + (jnp.dot(a_ref[...], w_ref[...], preferred_element_type=F32) + b_ref[...])

    return _call(
        body, name=name, grid=(seq // tm,),
        in_specs=[_rows(tm, kdim), _resident((kdim, n)), _full((1, n)), _rows(tm, n)],
        out_specs=_rows(tm, n), out_shape=jax.ShapeDtypeStruct((seq, n), F32),
        sem=("parallel",))(a, w, b, res)


def _ff_tile_spec(slot):
    return pl.BlockSpec((FF_TILE_DEVS, ROWS_FF, D_MODEL), lambda j, i: (j, slot, 0))


def _ff_whole_spec(slot):
    return pl.BlockSpec((N_DEV, ROWS_FF, D_MODEL), lambda i: (0, slot, 0), pipeline_mode=pl.Buffered(1))


def _ffn_up(name, h, g, gathered):
    seq = h.shape[0]
    tm = min(TOKEN_TILE, seq)

    def body(h_ref, g_ref, w1_ref, w3_ref, f_ref, u_ref, w_ref, s_ref):
        hv = h_ref[...]
        f = (hv * _rms_stats(hv) * g_ref[...]).astype(BF16)
        f_ref[...] = f
        u = _nt(f, w1_ref[...].reshape(FF_TILE, D_MODEL))
        w = _nt(f, w3_ref[...].reshape(FF_TILE, D_MODEL))
        u_ref[...] = u.astype(BF16)
        w_ref[...] = w.astype(BF16)
        s_ref[...] = (u * _sig(u) * w).astype(BF16)

    n_t = seq // tm
    tile_ff = pl.BlockSpec((tm, FF_TILE), lambda j, i: (i, j))
    f_spec = pl.BlockSpec((tm, D_MODEL), lambda j, i: (jnp.where(j == 0, i, n_t), 0))
    ff_shape = jax.ShapeDtypeStruct((seq, D_FF), BF16)
    return _call(
        body, name=name, grid=(FF_SPLIT, n_t),
        in_specs=[pl.BlockSpec((tm, D_MODEL), lambda j, i: (i, 0)), _full((1, D_MODEL)),
                  _ff_tile_spec(W1T_SLOT), _ff_tile_spec(W3T_SLOT)],
        out_specs=[f_spec, tile_ff, tile_ff, tile_ff],
        out_shape=[jax.ShapeDtypeStruct((seq + tm, D_MODEL), BF16), ff_shape, ff_shape, ff_shape],
        sem=("arbitrary", "arbitrary"))(h, g, gathered, gathered)


def _ffn_down(name, s, gathered, res):
    seq = s.shape[0]
    tm = min(2 * TOKEN_TILE, seq)

    def body(s_ref, w_ref, r_ref, o_ref):
        w2 = w_ref[...].reshape(D_FF, D_MODEL)
        o_ref[...] = r_ref[...] + jnp.dot(s_ref[...], w2, preferred_element_type=F32)

    return _call(
        body, name=name, grid=(seq // tm,),
        in_specs=[_rows(tm, D_FF), _ff_whole_spec(W2_SLOT), _rows(tm, D_MODEL)],
        out_specs=_rows(tm, D_MODEL), out_shape=jax.ShapeDtypeStruct((seq, D_MODEL), F32),
        sem=("parallel",))(s, gathered, res)


def _conv_fwd(h, g, wpw1, bpw1, wd, bdw, lng, lnb):
    seq = h.shape[0]
    tm = min(TOKEN_TILE, seq)
    n_chunks = tm // CONV_CHUNK
    win = CONV_CHUNK + CONV_HALO

    def body(h_ref, g_ref, w_ref, b_ref, wd_ref, bdw_ref, lng_ref, lnb_ref,
             y_ref, a_ref, dwc_ref, z_ref, gbuf):
        i = pl.program_id(0)

        @pl.when(i == 0)
        def _():
            gbuf[0:CONV_HALO, :] = jnp.zeros((CONV_HALO, D_MODEL), F32)

        @pl.when(i > 0)
        def _():
            gbuf[0:CONV_HALO, :] = gbuf[tm:tm + CONV_HALO, :]

        hv = h_ref[...]
        y = (hv * _rms_stats(hv) * g_ref[...]).astype(BF16)
        y_ref[...] = y
        a = _nt(y, w_ref[...]) + b_ref[...]
        a_ref[...] = a.astype(BF16)
        gbuf[CONV_HALO:CONV_HALO + tm, :] = a[:, :D_MODEL] * _sig(a[:, D_MODEL:])

        def chunk(c, carry):
            r0 = pl.multiple_of(c * CONV_CHUNK, CONV_CHUNK)
            for l in range(D_MODEL // LANES):
                ls = slice(l * LANES, (l + 1) * LANES)
                gw = gbuf[pl.ds(r0, win), ls]
                acc = jnp.zeros((CONV_CHUNK, LANES), F32) + bdw_ref[:, ls]
                for s in range(SUBLANES):
                    gs = gw if s == 0 else pltpu.roll(gw, s, 0)
                    for q in range(CONV_HALO // SUBLANES):
                        d = SUBLANES * q + s
                        if d < CONV_WIDTH:
                            lo = CONV_HALO - SUBLANES * q
                            acc = acc + wd_ref[d:d + 1, ls] * gs[lo:lo + CONV_CHUNK]
                dwc_ref[pl.ds(r0, CONV_CHUNK), ls] = acc
            return carry

        lax.fori_loop(0, n_chunks, chunk, 0)

        xv = dwc_ref[...]
        mu = jnp.mean(xv, axis=-1, keepdims=True)
        xc = xv - mu
        var = jnp.mean(xc * xc, axis=-1, keepdims=True)
        ln = xc * lax.rsqrt(var + LN_EPS) * lng_ref[...] + lnb_ref[...]
        z_ref[...] = (ln * _sig(ln)).astype(BF16)

    return _call(
        body, name="conv_fwd", grid=(seq // tm,),
        in_specs=[_rows(tm, D_MODEL), _full((1, D_MODEL)), _full((2 * D_MODEL, D_MODEL)), _full((1, 2 * D_MODEL)),
                  _full((TAPS_PAD, D_MODEL)), _full((1, D_MODEL)), _full((1, D_MODEL)), _full((1, D_MODEL))],
        out_specs=[_rows(tm, D_MODEL), _rows(tm, 2 * D_MODEL), _rows(tm, D_MODEL), _rows(tm, D_MODEL)],
        out_shape=[jax.ShapeDtypeStruct((seq, D_MODEL), BF16), jax.ShapeDtypeStruct((seq, 2 * D_MODEL), BF16),
                   jax.ShapeDtypeStruct((seq, D_MODEL), F32), jax.ShapeDtypeStruct((seq, D_MODEL), BF16)],
        scratch=[pltpu.VMEM((tm + CONV_HALO, D_MODEL), F32)],
        sem=("arbitrary",))(h, g, wpw1, bpw1, wd, bdw, lng, lnb)


def _ffn_down_loss(s, gathered, res, target, g):
    seq = s.shape[0]
    tm = min(TOKEN_TILE, seq)

    def body(s_ref, w_ref, r_ref, t_ref, g_ref, dh_ref, dhb_ref, loss_ref, dg_ref):
        @pl.when(pl.program_id(0) == 0)
        def _():
            loss_ref[...] = jnp.zeros_like(loss_ref)
            dg_ref[...] = jnp.zeros_like(dg_ref)

        hv = r_ref[...] + jnp.dot(s_ref[...], w_ref[...].reshape(D_FF, D_MODEL), preferred_element_type=F32)
        gv = g_ref[...]
        err = hv * _rms_stats(hv) * gv - t_ref[...]
        sq = jnp.sum(jnp.sum(err * err, axis=-1, keepdims=True), axis=0, keepdims=True)
        loss_ref[...] += jnp.broadcast_to(sq, loss_ref.shape)
        dx, dg = _rms_bwd(err * (1.0 / D_MODEL), hv, gv, 0.0)
        dh_ref[...] = dx
        dhb_ref[...] = dx.astype(BF16)
        dg_ref[...] += dg

    return _call(
        body, name="ffn1_down_loss", grid=(seq // tm,),
        in_specs=[_rows(tm, D_FF), _ff_whole_spec(W2_SLOT), _rows(tm, D_MODEL), _rows(tm, D_MODEL),
                  _full((1, D_MODEL))],
        out_specs=[_rows(tm, D_MODEL), _rows(tm, D_MODEL), _full((SUBLANES, LANES)), _full((1, D_MODEL))],
        out_shape=[jax.ShapeDtypeStruct((seq, D_MODEL), F32), jax.ShapeDtypeStruct((seq, D_MODEL), BF16),
                   jax.ShapeDtypeStruct((SUBLANES, LANES), F32), jax.ShapeDtypeStruct((1, D_MODEL), F32)],
        sem=("arbitrary",))(s, gathered, res, target, g)


def _ffn_bwd_act(name, dh, u, w, gathered):
    seq = dh.shape[0]
    tm = min(TOKEN_TILE, seq)

    def body(dh_ref, u_ref, w_ref, w2_ref, du_ref, dw_ref):
        ds = _nt(dh_ref[...], w2_ref[...].reshape(FF_TILE, D_MODEL))
        uv = u_ref[...].astype(F32)
        sg = _sig(uv)
        dw_ref[...] = (ds * (uv * sg)).astype(BF16)
        du_ref[...] = (ds * w_ref[...].astype(F32) * (sg * (1.0 + uv * (1.0 - sg)))).astype(BF16)

    tile_ff = pl.BlockSpec((tm, FF_TILE), lambda j, i: (i, j))
    ff_shape = jax.ShapeDtypeStruct((seq, D_FF), BF16)
    return _call(
        body, name=name, grid=(FF_SPLIT, seq // tm),
        in_specs=[pl.BlockSpec((tm, D_MODEL), lambda j, i: (i, 0)), tile_ff, tile_ff, _ff_tile_spec(W2_SLOT)],
        out_specs=[tile_ff, tile_ff], out_shape=[ff_shape, ff_shape],
        sem=("parallel", "parallel"))(dh, u, w, gathered)


def _ffn_bwd_in(name, du, dw, gathered, h_in, dres, g):
    seq = du.shape[0]
    tm = min(TOKEN_TILE, seq)

    def body(du_ref, dw_ref, w1_ref, w3_ref, h_ref, dr_ref, g_ref, dx_ref, dxb_ref, dg_ref):
        @pl.when(pl.program_id(0) == 0)
        def _():
            dg_ref[...] = jnp.zeros_like(dg_ref)

        df = jnp.dot(du_ref[...], w1_ref[...].reshape(D_FF, D_MODEL), preferred_element_type=F32)
        df = df + jnp.dot(dw_ref[...], w3_ref[...].reshape(D_FF, D_MODEL), preferred_element_type=F32)
        dx, dg = _rms_bwd(df, h_ref[...], g_ref[...], dr_ref[...])
        dx_ref[...] = dx
        dxb_ref[...] = dx.astype(BF16)
        dg_ref[...] += dg

    return _call(
        body, name=name, grid=(seq // tm,),
        in_specs=[_rows(tm, D_FF), _rows(tm, D_FF), _ff_whole_spec(W1T_SLOT), _ff_whole_spec(W3T_SLOT),
                  _rows(tm, D_MODEL), _rows(tm, D_MODEL), _full((1, D_MODEL))],
        out_specs=[_rows(tm, D_MODEL), _rows(tm, D_MODEL), _full((1, D_MODEL))],
        out_shape=[jax.ShapeDtypeStruct((seq, D_MODEL), F32), jax.ShapeDtypeStruct((seq, D_MODEL), BF16),
                   jax.ShapeDtypeStruct((1, D_MODEL), F32)],
        sem=("arbitrary",))(du, dw, gathered, gathered, h_in, dres, g)


def _mm_tn(name, a, b, *, tk, into=None):
    seq, kdim = a.shape
    n = b.shape[1]
    tt = min((4 if b.dtype == BF16 else 2) * TOKEN_TILE, seq)
    n_t = seq // tt
    devs = tk // ROWS_FF

    def body(a_ref, b_ref, *rest):
        o_ref, acc = rest[-2:]
        t = pl.program_id(1)

        @pl.when(t == 0)
        def _():
            acc[...] = jnp.zeros_like(acc)

        acc[...] += _tn(a_ref[...].astype(BF16), b_ref[...].astype(BF16))

        @pl.when(t == n_t - 1)
        def _():
            out = acc[...].astype(BF16)
            o_ref[...] = out if into is None else out.reshape(devs, ROWS_FF, n)

    in_specs = [pl.BlockSpec((tt, tk), lambda k, t: (t, k)), pl.BlockSpec((tt, n), lambda k, t: (t, 0))]
    args = [a, b]
    aliases = None
    if into is None:
        out_spec = pl.BlockSpec((tk, n), lambda k, t: (k, 0))
        out_shape = jax.ShapeDtypeStruct((kdim, n), BF16)
    else:
        blocks, slot = into
        out_spec = pl.BlockSpec((devs, ROWS_FF, n), lambda k, t: (k, slot, 0))
        out_shape = jax.ShapeDtypeStruct((N_DEV, 3 * ROWS_FF, n), BF16)
        if blocks is not None:
            in_specs.append(_ANY)
            args.append(blocks)
            aliases = {2: 0}
    return _call(
        body, name=name, grid=(kdim // tk, n_t), in_specs=in_specs, out_specs=out_spec, out_shape=out_shape,
        scratch=[pltpu.VMEM((tk, n), F32)], sem=("parallel", "arbitrary"), aliases=aliases)(*args)


def _conv_bwd(dh, wpw2, dwc, a, lng, lnb, wd):
    seq = dh.shape[0]
    tm = min(TOKEN_TILE, seq)
    nt = seq // tm
    n_chunks = tm // CONV_CHUNK
    win = CONV_CHUNK + CONV_HALO
    halo_per_tile = tm // CONV_HALO

    def body(dh_ref, w_ref, dwc_ref, a_ref, ah_ref, lng_ref, lnb_ref, wd_ref,
             da_ref, dlg_ref, dlb_ref, dbdw_ref, dwd_ref, dbpw1_ref, dbpw2_ref,
             gbuf, dbuf, dglu, dwd_part):
        i = pl.program_id(0)
        r = nt - 1 - i

        @pl.when(i == 0)
        def _():
            dlg_ref[...] = jnp.zeros_like(dlg_ref)
            dlb_ref[...] = jnp.zeros_like(dlb_ref)
            dbdw_ref[...] = jnp.zeros_like(dbdw_ref)
            dbpw1_ref[...] = jnp.zeros_like(dbpw1_ref)
            dbpw2_ref[...] = jnp.zeros_like(dbpw2_ref)
            dwd_part[...] = jnp.zeros_like(dwd_part)
            dbuf[tm:tm + CONV_HALO, :] = jnp.zeros((CONV_HALO, D_MODEL), F32)

        @pl.when(i > 0)
        def _():
            dbuf[tm:tm + CONV_HALO, :] = dbuf[0:CONV_HALO, :]

        dhv = dh_ref[...]
        dbpw2_ref[...] += _sum_rows(dhv)
        dz = _nt(dhv.astype(BF16), w_ref[...])
        xv = dwc_ref[...]
        lg = lng_ref[...]
        mu = jnp.mean(xv, axis=-1, keepdims=True)
        xc = xv - mu
        rstd = lax.rsqrt(jnp.mean(xc * xc, axis=-1, keepdims=True) + LN_EPS)
        xhat = xc * rstd
        ln = xhat * lg + lnb_ref[...]
        sg = _sig(ln)
        dln = dz * (sg * (1.0 + ln * (1.0 - sg)))
        dlg_ref[...] += _sum_rows(dln * xhat)
        dlb_ref[...] += _sum_rows(dln)
        dxh = dln * lg
        ddw = rstd * (dxh - jnp.mean(dxh, axis=-1, keepdims=True)
                      - xhat * jnp.mean(dxh * xhat, axis=-1, keepdims=True))
        dbdw_ref[...] += _sum_rows(ddw)
        dbuf[0:tm, :] = ddw

        av = a_ref[...].astype(F32)
        a1 = av[:, :D_MODEL]
        s2 = _sig(av[:, D_MODEL:])
        gbuf[CONV_HALO:CONV_HALO + tm, :] = a1 * s2
        ah = ah_ref[...].astype(F32)
        gh = ah[:, :D_MODEL] * _sig(ah[:, D_MODEL:])
        gbuf[0:CONV_HALO, :] = jnp.where(r > 0, gh, 0.0)

        def chunk(c, carry):
            r0 = pl.multiple_of(c * CONV_CHUNK, CONV_CHUNK)
            for l in range(D_MODEL // LANES):
                ls = slice(l * LANES, (l + 1) * LANES)
                dw_ = dbuf[pl.ds(r0, win), ls]
                gw = gbuf[pl.ds(r0, win), ls]
                acc = jnp.zeros((CONV_CHUNK, LANES), F32)
                for s in range(SUBLANES):
                    ds_ = dw_ if s == 0 else pltpu.roll(dw_, win - s, 0)
                    for q in range(CONV_HALO // SUBLANES):
                        d = SUBLANES * q + s
                        if d < CONV_WIDTH:
                            acc = acc + wd_ref[d:d + 1, ls] * ds_[SUBLANES * q:SUBLANES * q + CONV_CHUNK]
                            lo = CONV_HALO - SUBLANES * q
                            prod = ds_[0:CONV_CHUNK] * gw[lo:lo + CONV_CHUNK]
                            dwd_part[d, :, ls] += jnp.sum(
                                prod.reshape(CONV_CHUNK // SUBLANES, SUBLANES, LANES), axis=0)
                dglu[pl.ds(r0, CONV_CHUNK), ls] = acc
            return carry

        lax.fori_loop(0, n_chunks, chunk, 0)

        dg_ = dglu[...]
        da1 = dg_ * s2
        da2 = dg_ * a1 * s2 * (1.0 - s2)
        da_ref[:, :D_MODEL] = da1.astype(BF16)
        da_ref[:, D_MODEL:] = da2.astype(BF16)
        dbpw1_ref[:, :D_MODEL] += _sum_rows(da1)
        dbpw1_ref[:, D_MODEL:] += _sum_rows(da2)

        @pl.when(i == nt - 1)
        def _():
            dwd_ref[...] = jnp.sum(dwd_part[...], axis=1)

    rev = lambda n: pl.BlockSpec((tm, n), lambda i: (nt - 1 - i, 0))
    halo = pl.BlockSpec((CONV_HALO, 2 * D_MODEL),
                        lambda i: (jnp.maximum((nt - 1 - i) * halo_per_tile - 1, 0), 0))
    vec = lambda n: _full((1, n))
    return _call(
        body, name="conv_bwd", grid=(nt,),
        in_specs=[rev(D_MODEL), _full((D_MODEL, D_MODEL)), rev(D_MODEL), rev(2 * D_MODEL), halo,
                  vec(D_MODEL), vec(D_MODEL), _full((TAPS_PAD, D_MODEL))],
        out_specs=[rev(2 * D_MODEL), vec(D_MODEL), vec(D_MODEL), vec(D_MODEL), _full((TAPS_PAD, D_MODEL)),
                   vec(2 * D_MODEL), vec(D_MODEL)],
        out_shape=[jax.ShapeDtypeStruct((seq, 2 * D_MODEL), BF16), jax.ShapeDtypeStruct((1, D_MODEL), F32),
                   jax.ShapeDtypeStruct((1, D_MODEL), F32), jax.ShapeDtypeStruct((1, D_MODEL), F32),
                   jax.ShapeDtypeStruct((TAPS_PAD, D_MODEL), F32), jax.ShapeDtypeStruct((1, 2 * D_MODEL), F32),
                   jax.ShapeDtypeStruct((1, D_MODEL), F32)],
        scratch=[pltpu.VMEM((tm + CONV_HALO, D_MODEL), F32), pltpu.VMEM((tm + CONV_HALO, D_MODEL), F32),
                 pltpu.VMEM((tm, D_MODEL), F32), pltpu.VMEM((TAPS_PAD, SUBLANES, D_MODEL), F32)],
        sem=("arbitrary",))(dh, wpw2, dwc, a, a, lng, lnb, wd)


def _mm_rms_bwd(name, dact, wt, h_in, dres, g, bf16_copy):
    seq, n = dact.shape
    tm = min(TOKEN_TILE, seq)

    def body(da_ref, w_ref, h_ref, dr_ref, g_ref, dx_ref, *rest):
        dg_ref = rest[-1]

        @pl.when(pl.program_id(0) == 0)
        def _():
            dg_ref[...] = jnp.zeros_like(dg_ref)

        dy = jnp.dot(da_ref[...], w_ref[...], preferred_element_type=F32)
        dx, dg = _rms_bwd(dy, h_ref[...], g_ref[...], dr_ref[...])
        dx_ref[...] = dx
        if bf16_copy:
            rest[0][...] = dx.astype(BF16)
        dg_ref[...] += dg

    copy_spec = [_rows(tm, D_MODEL)] if bf16_copy else []
    copy_shape = [jax.ShapeDtypeStruct((seq, D_MODEL), BF16)] if bf16_copy else []
    return _call(
        body, name=name, grid=(seq // tm,),
        in_specs=[_rows(tm, n), _resident((n, D_MODEL)), _rows(tm, D_MODEL), _rows(tm, D_MODEL), _full((1, D_MODEL))],
        out_specs=[_rows(tm, D_MODEL), *copy_spec, _full((1, D_MODEL))],
        out_shape=[jax.ShapeDtypeStruct((seq, D_MODEL), F32), *copy_shape, jax.ShapeDtypeStruct((1, D_MODEL), F32)],
        sem=("arbitrary",))(dact, wt, h_in, dres, g)


def _nt_bias(name, dy, w):
    seq, n = dy.shape
    kdim = w.shape[0]
    tm = min(2 * TOKEN_TILE, seq)

    def body(dy_ref, w_ref, o_ref, db_ref):
        @pl.when(pl.program_id(0) == 0)
        def _():
            db_ref[...] = jnp.zeros_like(db_ref)

        dyv = dy_ref[...]
        db_ref[...] += _sum_rows(dyv)
        o_ref[...] = _nt(dyv.astype(BF16), w_ref[...]).astype(BF16)

    return _call(
        body, name=name, grid=(seq // tm,),
        in_specs=[_rows(tm, n), _resident((kdim, n))],
        out_specs=[_rows(tm, kdim), _full((1, n))],
        out_shape=[jax.ShapeDtypeStruct((seq, kdim), BF16), jax.ShapeDtypeStruct((1, n), F32)],
        sem=("arbitrary",))(dy, w)


def _attn_bwd(probs, p_sinks, q, k, v, do):
    seq = q.shape[0]
    cur, prev = _kv_specs()

    def body(p_ref, ps_ref, q_ref, kc_ref, kp_ref, vc_ref, vp_ref, do_ref,
             dq_ref, dkc_ref, dkp_ref, dvc_ref, dvp_ref, dsink_ref):
        n = pl.program_id(0)

        @pl.when(n == 0)
        def _():
            dsink_ref[...] = jnp.zeros_like(dsink_ref)

        low = _low_lanes()
        lane = lax.broadcasted_iota(jnp.int32, (BLOCK, LANES), 1)
        p_sinks_blk = ps_ref[...]
        dk_all = jnp.zeros((2 * BLOCK, KV_DIM), F32)
        dv_all = jnp.zeros((2 * BLOCK, KV_DIM), F32)
        for j in range(N_KV_HEADS):
            k_lo, k_hi = _kv_low_high(kp_ref, kc_ref, j, low)
            v_lo, v_hi = _kv_low_high(vp_ref, vc_ref, j, low)
            dk_acc = jnp.zeros((2 * BLOCK, KV_DIM), F32)
            dv_acc = jnp.zeros((2 * BLOCK, KV_DIM), F32)
            for i in range(PAIRS_PER_KV):
                ls, h = _pair_lanes(j, i)
                qp = q_ref[:, ls]
                dop = do_ref[:, ls]
                dsb, pb16 = [], []
                for t, v_sel in enumerate((v_lo, v_hi)):
                    pb = p_ref[h + t]
                    pf = pb.astype(F32)
                    p_sink = jnp.sum(jnp.where(lane == h + t, p_sinks_blk, 0.0), axis=-1, keepdims=True)
                    dp = _nt(dop, v_sel)
                    delta = jnp.sum(pf * dp, axis=-1, keepdims=True)
                    dsb.append((pf * (dp - delta)).astype(BF16))
                    pb16.append(pb)
                    dsink_ref[h + t:h + t + 1, :] += jnp.broadcast_to(_sum_rows(p_sink * delta), (1, LANES))
                dq = (jnp.dot(dsb[0], k_lo, preferred_element_type=F32)
                      + jnp.dot(dsb[1], k_hi, preferred_element_type=F32))
                dq_ref[:, ls] = dq * (HEAD_DIM ** -0.5)
                dk_acc = dk_acc + jnp.where(low, _tn(dsb[0], qp), _tn(dsb[1], qp))
                dv_acc = dv_acc + jnp.where(low, _tn(pb16[0], dop), _tn(pb16[1], dop))
            dk_all = dk_all + _fold_pair_halves(dk_acc, j, low)
            dv_all = dv_all + _fold_pair_halves(dv_acc, j, low)
        dkp_ref[...] = dk_all[:BLOCK]
        dkc_ref[...] = dk_all[BLOCK:]
        dvp_ref[...] = dv_all[:BLOCK]
        dvc_ref[...] = dv_all[BLOCK:]

    kv_out = _rows(BLOCK, KV_DIM)
    kv_shape = jax.ShapeDtypeStruct((seq, KV_DIM), F32)
    return _call(
        body, name="attn_bwd", grid=(seq // BLOCK,),
        in_specs=[pl.BlockSpec((N_Q_HEADS, BLOCK, 2 * BLOCK), lambda n: (0, n, 0)), _rows(BLOCK, LANES),
                  _rows(BLOCK, Q_DIM), cur, prev, cur, prev, _rows(BLOCK, Q_DIM)],
        out_specs=[_rows(BLOCK, Q_DIM), kv_out, kv_out, kv_out, kv_out, _full((N_Q_HEADS, LANES))],
        out_shape=[jax.ShapeDtypeStruct((seq, Q_DIM), F32), kv_shape, kv_shape, kv_shape, kv_shape,
                   jax.ShapeDtypeStruct((N_Q_HEADS, LANES), F32)],
        sem=("arbitrary",))(probs, p_sinks, q, k, k, v, v, do)


def _rope_bwd(dq, dkc, dkp, dvc, dvp, rc, rsa, rsb):
    seq = dq.shape[0]
    nb = seq // BLOCK
    tm = min(TOKEN_TILE, seq)
    nt = seq // tm
    per = tm // BLOCK
    nxt = pl.BlockSpec((BLOCK, KV_DIM), lambda i: (jnp.minimum((i + 1) * per, nb - 1), 0))

    def body(dq_ref, dkc_ref, dkp_ref, dkn_ref, dvc_ref, dvp_ref, dvn_ref, c_ref, sa_ref, sb_ref, o_ref, db_ref):
        i = pl.program_id(0)

        @pl.when(i == 0)
        def _():
            db_ref[...] = jnp.zeros_like(db_ref)

        c, sa, sb = c_ref[...], sa_ref[...], sb_ref[...]
        last = i == nt - 1

        def from_next_block(prev_ref, next_ref):
            tail = jnp.where(last, 0.0, next_ref[...])
            return tail if per == 1 else jnp.concatenate([prev_ref[BLOCK:, :], tail], axis=0)

        dk = dkc_ref[...] + from_next_block(dkp_ref, dkn_ref)
        dv = dvc_ref[...] + from_next_block(dvp_ref, dvn_ref)
        for l in range(Q_DIM // LANES):
            ls = slice(l * LANES, (l + 1) * LANES)
            blk = _rope_t(dq_ref[:, ls], c, sa, sb)
            o_ref[:, ls] = blk.astype(BF16)
            db_ref[:, ls] += _sum_rows(blk)
        dkr = _rope_t(dk, c, sa, sb)
        o_ref[:, Q_DIM:Q_DIM + KV_DIM] = dkr.astype(BF16)
        db_ref[:, Q_DIM:Q_DIM + KV_DIM] += _sum_rows(dkr)
        o_ref[:, Q_DIM + KV_DIM:] = dv.astype(BF16)
        db_ref[:, Q_DIM + KV_DIM:] += _sum_rows(dv)

    kv = _rows(tm, KV_DIM)
    tab = _rows(tm, LANES)
    return _call(
        body, name="rope_bwd", grid=(nt,),
        in_specs=[_rows(tm, Q_DIM), kv, kv, nxt, kv, kv, nxt, tab, tab, tab],
        out_specs=[_rows(tm, QKV_DIM), _full((1, QKV_DIM))],
        out_shape=[jax.ShapeDtypeStruct((seq, QKV_DIM), BF16), jax.ShapeDtypeStruct((1, QKV_DIM), F32)],
        sem=("arbitrary",))(dq, dkc, dkp, dkp, dvc, dvp, dvp, rc, rsa, rsb)


def _adamw(w, g, m, v):
    m = ADAM_B1 * m + (1.0 - ADAM_B1) * g
    v = ADAM_B2 * v + (1.0 - ADAM_B2) * (g * g)
    m_hat = m / (1.0 - ADAM_B1 ** ADAM_STEP)
    v_hat = v / (1.0 - ADAM_B2 ** ADAM_STEP)
    delta = -ADAM_LR * (m_hat / (jnp.sqrt(v_hat) + ADAM_EPS) + ADAM_WD * w)
    return delta, m, v


def _sum_slots(name, parts):
    _, rows, cols = parts.shape
    tr = rows if rows <= 512 else ROWS_FF

    def body(p_ref, g_ref):
        g = p_ref[0].astype(F32)
        for d in range(1, N_DEV):
            g = g + p_ref[d].astype(F32)
        g_ref[...] = g

    return _call(
        body, name=name, grid=(rows // tr,),
        in_specs=[pl.BlockSpec((N_DEV, tr, cols), lambda i: (0, i, 0))],
        out_specs=_rows(tr, cols), out_shape=jax.ShapeDtypeStruct((rows, cols), F32),
        sem=("parallel",))(parts)


def _adamw_native(name, g, w, m, v):
    layers, rows, cols = w.shape
    tr = rows if rows <= 512 else 256

    def body(g_ref, w_ref, m_ref, v_ref, d_ref, nm_ref, nv_ref):
        d_ref[...], nm_ref[...], nv_ref[...] = _adamw(w_ref[...], g_ref[...], m_ref[...], v_ref[...])

    spec = pl.BlockSpec((1, tr, cols), lambda l, i: (l, i, 0))
    shape = jax.ShapeDtypeStruct(w.shape, F32)
    return _call(
        body, name=name, grid=(layers, rows // tr), in_specs=[spec, spec, spec, spec],
        out_specs=[spec, spec, spec], out_shape=[shape, shape, shape],
        sem=("parallel", "parallel"))(g, w, m, v)


def _adamw_replicated(parts, w, m, v):
    def body(p_ref, w_ref, m_ref, v_ref, g_ref, d_ref, nm_ref, nv_ref):
        g = p_ref[0]
        for j in range(1, N_DEV):
            g = g + p_ref[j]
        g_ref[...] = g
        d_ref[...], nm_ref[...], nv_ref[...] = _adamw(w_ref[...], g, m_ref[...], v_ref[...])

    spec = _full((REPL_ROWS, D_MODEL))
    shape = jax.ShapeDtypeStruct((REPL_ROWS, D_MODEL), F32)
    return _call(
        body, name="adamw_replicated", grid=(1,),
        in_specs=[_full((N_DEV, REPL_ROWS, D_MODEL)), spec, spec, spec],
        out_specs=[spec, spec, spec, spec], out_shape=[shape, shape, shape, shape],
        sem=("arbitrary",))(parts, w, m, v)


_MESH = pl.DeviceIdType.MESH
_ANY = pl.BlockSpec(memory_space=pl.ANY)


def _all_gather(name, xs):
    rows, cols = xs.shape

    def body(x_ref, out_ref, send_sems, recv_sems, local_sem):
        x, y, c = lax.axis_index("x"), lax.axis_index("y"), lax.axis_index("c")
        me, sibling = (x, y, c), (x, y, 1 - c)
        chips = [(1 - x, y), (x, 1 - y), (1 - x, 1 - y)]

        def slot(px, py, pc):
            return out_ref.at[4 * px + 2 * py + pc]

        def copy(k, block, to, src=None):
            return pltpu.make_async_remote_copy(
                src_ref=slot(*block) if src is None else src, dst_ref=slot(*block),
                send_sem=send_sems.at[k], recv_sem=recv_sems.at[k], device_id=to, device_id_type=_MESH)

        mine = pltpu.make_async_copy(x_ref, slot(*me), local_sem)
        mine.start()
        first = [copy(0, me, sibling, src=x_ref)]
        first += [copy(1 + j, me, (*chip, c), src=x_ref) for j, chip in enumerate(chips)]
        for cp in first:
            cp.start()
        passed = [copy(4 + j, (*chip, c), sibling) for j, chip in enumerate(chips)]
        for j, chip in enumerate(chips):
            copy(1 + j, (*chip, c), me).wait_recv()
            passed[j].start()
        copy(0, sibling, me).wait_recv()
        for j, chip in enumerate(chips):
            copy(4 + j, (*chip, 1 - c), me).wait_recv()
        for cp in first + passed:
            cp.wait_send()
        mine.wait()

    return pl.pallas_call(
        body, name=name, out_shape=jax.ShapeDtypeStruct((N_DEV, rows, cols), xs.dtype),
        in_specs=[_ANY], out_specs=_ANY,
        scratch_shapes=[pltpu.SemaphoreType.DMA((7,)), pltpu.SemaphoreType.DMA((7,)), pltpu.SemaphoreType.DMA],
    )(xs)


N_PEERS = N_DEV - 1
_HBM = pl.BlockSpec(memory_space=pltpu.HBM)
_SEM = pl.BlockSpec(memory_space=pltpu.SEMAPHORE)
_DATAFLOW = pltpu.SideEffectType.DATAFLOW_SIDE_EFFECTING
_TOKEN = jax.ShapeDtypeStruct((SUBLANES, LANES), F32)


def _peers():
    x, y, c = lax.axis_index("x"), lax.axis_index("y"), lax.axis_index("c")
    out = []
    for k in range(1, N_DEV):
        px = 1 - x if k & 4 else x
        py = 1 - y if k & 2 else y
        pc = 1 - c if k & 1 else c
        out.append(((px, py, pc), 4 * px + 2 * py + pc))
    return 4 * x + 2 * y + c, out


def _in_hbm(a):
    return pltpu.with_memory_space_constraint(a, pltpu.HBM)


def _landing(rows):
    return _in_hbm(lax.empty((N_DEV, rows, D_MODEL), BF16))


def _sem_pair():
    return pltpu.SemaphoreType.DMA((N_PEERS,)), pltpu.SemaphoreType.DMA((N_PEERS,))


def _gather_start(name, payloads):
    n = len(payloads)

    def body(*refs):
        src, land = refs[:n], refs[n:2 * n]
        sems = refs[2 * n:4 * n]
        token = refs[-1]
        me, peers = _peers()
        for g in range(n):
            for k, (pos, _) in enumerate(peers):
                pltpu.make_async_remote_copy(
                    src_ref=src[g], dst_ref=land[g].at[me], send_sem=sems[2 * g].at[k],
                    recv_sem=sems[2 * g + 1].at[k], device_id=pos, device_id_type=_MESH).start()
        token[...] = jnp.zeros_like(token)

    lands = [_landing(p.shape[0]) for p in payloads]
    sem_shapes = [s for _ in payloads for s in _sem_pair()]
    hbm_shapes = [pltpu.HBM(a.shape, a.dtype) for a in list(payloads) + lands]
    out = pl.pallas_call(
        body, name=name, out_shape=(*sem_shapes, *hbm_shapes, _TOKEN),
        in_specs=[_HBM] * (2 * n), out_specs=(*[_SEM] * (2 * n), *[_HBM] * (2 * n), pl.BlockSpec(memory_space=pltpu.VMEM)),
        input_output_aliases={i: 2 * n + i for i in range(2 * n)},
        compiler_params=pltpu.CompilerParams(has_side_effects=_DATAFLOW),
    )(*[_in_hbm(p) for p in payloads], *lands)
    sems, thru = out[:2 * n], out[2 * n:4 * n]
    return [(thru[g], thru[n + g], sems[2 * g], sems[2 * g + 1]) for g in range(n)], out[-1]


def _gather_wait(name, group, after):
    payload, land, send_sems, recv_sems = group

    def body(src_ref, land_ref, send_ref, recv_ref, after_ref, src_out, land_out):
        _, peers = _peers()
        for k, (pos, idx) in enumerate(peers):
            cp = pltpu.make_async_remote_copy(
                src_ref=src_ref, dst_ref=land_ref.at[idx], send_sem=send_ref.at[k], recv_sem=recv_ref.at[k],
                device_id=pos, device_id_type=_MESH)
            cp.wait_send()
            cp.wait_recv()

    _, land = pl.pallas_call(
        body, name=name, out_shape=(pltpu.HBM(payload.shape, payload.dtype), pltpu.HBM(land.shape, land.dtype)),
        in_specs=[_HBM, _HBM, _SEM, _SEM, _ANY], out_specs=(_HBM, _HBM), input_output_aliases={0: 0, 1: 1},
        compiler_params=pltpu.CompilerParams(has_side_effects=_DATAFLOW),
    )(payload, land, send_sems, recv_sems, after)
    me = 4 * lax.axis_index("x") + 2 * lax.axis_index("y") + lax.axis_index("c")
    return lax.dynamic_update_slice(land, payload[None], (me, 0, 0))


def _scatter_start(name, blocks):
    rows = blocks.shape[1]

    def body(blocks_ref, land_ref, send_sems, recv_sems, blocks_out, land_out, token):
        me, peers = _peers()
        for k, (pos, idx) in enumerate(peers):
            pltpu.make_async_remote_copy(
                src_ref=blocks_ref.at[idx], dst_ref=land_ref.at[me], send_sem=send_sems.at[k],
                recv_sem=recv_sems.at[k], device_id=pos, device_id_type=_MESH).start()
        token[...] = jnp.zeros_like(token)

    land = _landing(rows)
    send_sems, recv_sems, blocks_thru, land_thru, token = pl.pallas_call(
        body, name=name,
        out_shape=(*_sem_pair(), pltpu.HBM(blocks.shape, blocks.dtype), pltpu.HBM(land.shape, land.dtype), _TOKEN),
        in_specs=[_HBM, _HBM], out_specs=(_SEM, _SEM, _HBM, _HBM, pl.BlockSpec(memory_space=pltpu.VMEM)),
        input_output_aliases={0: 2, 1: 3},
        compiler_params=pltpu.CompilerParams(has_side_effects=_DATAFLOW),
    )(_in_hbm(blocks), land)
    return (blocks_thru, land_thru, send_sems, recv_sems), token


def _scatter_wait(groups, after):
    n = len(groups)

    def body(*refs):
        blocks, land = refs[:n], refs[n:2 * n]
        sems = refs[2 * n:4 * n]
        _, peers = _peers()
        for g in range(n):
            for k, (pos, idx) in enumerate(peers):
                cp = pltpu.make_async_remote_copy(
                    src_ref=blocks[g].at[idx], dst_ref=land[g].at[idx], send_sem=sems[2 * g].at[k],
                    recv_sem=sems[2 * g + 1].at[k], device_id=pos, device_id_type=_MESH)
                cp.wait_send()
                cp.wait_recv()

    hbm = [grp[0] for grp in groups] + [grp[1] for grp in groups]
    sems = [s for grp in groups for s in grp[2:]]
    out = pl.pallas_call(
        body, name="rs_wait", out_shape=tuple(pltpu.HBM(a.shape, a.dtype) for a in hbm),
        in_specs=[_HBM] * (2 * n) + [_SEM] * (2 * n) + [_ANY], out_specs=tuple([_HBM] * (2 * n)),
        input_output_aliases={i: i for i in range(2 * n)},
        compiler_params=pltpu.CompilerParams(has_side_effects=_DATAFLOW),
    )(*hbm, *sems, after)
    me = 4 * lax.axis_index("x") + 2 * lax.axis_index("y") + lax.axis_index("c")
    lands = []
    for g in range(n):
        own = lax.dynamic_index_in_dim(out[g], me, axis=0, keepdims=True)
        lands.append(lax.dynamic_update_slice(out[n + g], own, (me, 0, 0)))
    return lands


def _pad_rows(flat, rows):
    return jnp.pad(flat, (0, rows * D_MODEL - flat.shape[0])).reshape(rows, D_MODEL)


SMALL_NAMES = ("conv_b_pw1", "conv_w_dw", "conv_b_dw", "conv_ln_g", "conv_ln_b", "conv_b_pw2")


def _pack_small(p):
    flat = jnp.concatenate([p[n].reshape(-1) for n in SMALL_NAMES])
    return _pad_rows(flat, ROWS_SMALL).reshape(1, ROWS_SMALL, D_MODEL)


def _unpack_small(packed):
    flat = packed.reshape(-1)
    c = D_MODEL // N_DEV
    shapes = ((1, 2 * c), (1, CONV_WIDTH, c), (1, c), (1, c), (1, c), (1, c))
    out, o = {}, 0
    for n, shape in zip(SMALL_NAMES, shapes):
        size = shape[-1] * (shape[1] if len(shape) == 3 else 1)
        out[n] = flat[o:o + size].reshape(shape)
        o += size
    return out


def _gather_payloads(p):
    t = lambda a: jnp.swapaxes(a, -1, -2).astype(BF16)
    w1t, w3t, w2 = t(p["ffn_w1"]), t(p["ffn_w3"]), p["ffn_w2"].astype(BF16)
    bits = lax.bitcast_convert_type(_pack_small(p).reshape(-1)[:ROWS_SMALL * D_MODEL // 2], jnp.uint32)
    halves = [(bits >> 16).astype(jnp.uint16), (bits & 0xFFFF).astype(jnp.uint16)]
    small = lax.bitcast_convert_type(jnp.concatenate(halves), BF16).reshape(ROWS_SMALL, D_MODEL)
    conv = jnp.concatenate([t(p["conv_w_pw1"][0]), p["conv_w_pw2"][0].astype(BF16), small], axis=0)
    ffn = [jnp.concatenate([w1t[l], w3t[l], w2[l]], axis=0) for l in range(2)]
    return [t(p["attn_w_qkv"][0]), p["attn_w_o"][0].astype(BF16), ffn[0], conv, ffn[1]]


def _device_rows(land, lo, n):
    return land[:, lo:lo + n].reshape(N_DEV * n, D_MODEL)


def _unpack_conv(land):
    words = lax.bitcast_convert_type(land[:, ROWS_PW1 + ROWS_PW2:], jnp.uint16).astype(jnp.uint32)
    words = words.reshape(N_DEV, 2, ROWS_SMALL * D_MODEL // 2)
    small = lax.bitcast_convert_type((words[:, 0] << 16) | words[:, 1], F32)
    c = D_MODEL // N_DEV
    b_pw1 = small[:, :2 * c].reshape(1, 2 * D_MODEL)
    s = 2 * c
    w_dw = small[:, s:s + CONV_WIDTH * c].reshape(N_DEV, CONV_WIDTH, c).transpose(1, 0, 2).reshape(CONV_WIDTH, D_MODEL)
    s += CONV_WIDTH * c
    b_dw, ln_g, ln_b, b_pw2 = (small[:, s + i * c:s + (i + 1) * c].reshape(1, D_MODEL) for i in range(4))
    return dict(w_pw1_t=_device_rows(land, 0, ROWS_PW1), w_pw2=_device_rows(land, ROWS_PW1, ROWS_PW2),
                b_pw1=b_pw1, w_dw=w_dw, b_dw=b_dw, ln_g=ln_g, ln_b=ln_b, b_pw2=b_pw2)


def _dest_blocks(mats):
    return jnp.concatenate([a.reshape(N_DEV, -1, D_MODEL) for a in mats], axis=1)


def _small_grad_rows(g_bpw1, g_dw, g_bdw, g_lng, g_lnb, g_bpw2):
    c = D_MODEL // N_DEV
    small = jnp.concatenate(
        [g_bpw1.reshape(N_DEV, 2 * c), g_dw.reshape(CONV_WIDTH, N_DEV, c).transpose(1, 0, 2).reshape(N_DEV, -1),
         g_bdw.reshape(N_DEV, c), g_lng.reshape(N_DEV, c), g_lnb.reshape(N_DEV, c), g_bpw2.reshape(N_DEV, c)], axis=1)
    small = jnp.pad(small, ((0, 0), (0, ROWS_SMALL * D_MODEL - SMALL_USED)))
    return small.reshape(N_DEV * ROWS_SMALL, D_MODEL).astype(BF16)


LOSS_ROW = 9


def _pack_replicated(norm_mix, norm_ffn, b_qkv, sinks, b_o, norm_final, extra=None):
    rows = [norm_mix.reshape(2, D_MODEL), norm_ffn.reshape(2, D_MODEL), _pad_rows(b_qkv.reshape(-1), 2),
            _pad_rows(sinks.reshape(-1), 1), b_o.reshape(1, D_MODEL), norm_final.reshape(1, D_MODEL)]
    if extra is not None:
        rows.append(_pad_rows(extra.reshape(-1), 1))
    p = jnp.concatenate(rows, axis=0)
    return jnp.pad(p, ((0, REPL_ROWS - p.shape[0]), (0, 0)))


def _unpack_replicated(p):
    return dict(norm_mix=p[0:2], norm_ffn=p[2:4], attn_b_qkv=p[4:6].reshape(-1)[:QKV_DIM].reshape(1, QKV_DIM),
                attn_sinks=p[6, :N_Q_HEADS].reshape(1, N_Q_HEADS), attn_b_o=p[7:8], norm_final=p[8])


WEIGHT_ORDER = ['norm_mix', 'norm_ffn', 'attn_w_qkv', 'attn_b_qkv', 'attn_sinks', 'attn_w_o', 'attn_b_o',
                'conv_w_pw1', 'conv_b_pw1', 'conv_w_dw', 'conv_b_dw', 'conv_ln_g', 'conv_ln_b', 'conv_w_pw2',
                'conv_b_pw2', 'ffn_w1', 'ffn_w3', 'ffn_w2', 'norm_final']


def kernel(x, norm_mix, norm_ffn, attn_w_qkv, attn_b_qkv, attn_sinks, attn_w_o, attn_b_o, conv_w_pw1, conv_b_pw1, conv_w_dw, conv_b_dw, conv_ln_g, conv_ln_b, conv_w_pw2, conv_b_pw2, ffn_w1, ffn_w3, ffn_w2, norm_final, loss_target, m_norm_mix, m_norm_ffn, m_attn_w_qkv, m_attn_b_qkv, m_attn_sinks, m_attn_w_o, m_attn_b_o, m_conv_w_pw1, m_conv_b_pw1, m_conv_w_dw, m_conv_b_dw, m_conv_ln_g, m_conv_ln_b, m_conv_w_pw2, m_conv_b_pw2, m_ffn_w1, m_ffn_w3, m_ffn_w2, m_norm_final, v_norm_mix, v_norm_ffn, v_attn_w_qkv, v_attn_b_qkv, v_attn_sinks, v_attn_w_o, v_attn_b_o, v_conv_w_pw1, v_conv_b_pw1, v_conv_w_dw, v_conv_b_dw, v_conv_ln_g, v_conv_ln_b, v_conv_w_pw2, v_conv_b_pw2, v_ffn_w1, v_ffn_w3, v_ffn_w2, v_norm_final):
    xs = x[0]
    target = loss_target[0]
    seq = xs.shape[0]
    my_x, my_y, my_c = lax.axis_index("x"), lax.axis_index("y"), lax.axis_index("c")

    w = dict(attn_w_qkv=attn_w_qkv, attn_w_o=attn_w_o, conv_w_pw1=conv_w_pw1, conv_b_pw1=conv_b_pw1,
             conv_w_dw=conv_w_dw, conv_b_dw=conv_b_dw, conv_ln_g=conv_ln_g, conv_ln_b=conv_ln_b,
             conv_w_pw2=conv_w_pw2, conv_b_pw2=conv_b_pw2, ffn_w1=ffn_w1, ffn_w3=ffn_w3, ffn_w2=ffn_w2)
    m = dict(attn_w_qkv=m_attn_w_qkv, attn_w_o=m_attn_w_o, conv_w_pw1=m_conv_w_pw1, conv_b_pw1=m_conv_b_pw1,
             conv_w_dw=m_conv_w_dw, conv_b_dw=m_conv_b_dw, conv_ln_g=m_conv_ln_g, conv_ln_b=m_conv_ln_b,
             conv_w_pw2=m_conv_w_pw2, conv_b_pw2=m_conv_b_pw2, ffn_w1=m_ffn_w1, ffn_w3=m_ffn_w3, ffn_w2=m_ffn_w2)
    v = dict(attn_w_qkv=v_attn_w_qkv, attn_w_o=v_attn_w_o, conv_w_pw1=v_conv_w_pw1, conv_b_pw1=v_conv_b_pw1,
             conv_w_dw=v_conv_w_dw, conv_b_dw=v_conv_b_dw, conv_ln_g=v_conv_ln_g, conv_ln_b=v_conv_ln_b,
             conv_w_pw2=v_conv_w_pw2, conv_b_pw2=v_conv_b_pw2, ffn_w1=v_ffn_w1, ffn_w3=v_ffn_w3, ffn_w2=v_ffn_w2)

    payloads = _gather_payloads(w)
    (ag_qkv, ag_wo), tok = _gather_start("ag_start_attn", payloads[:2])
    (ag_ffn0, ag_conv, ag_ffn1), tok = _gather_start(
        "ag_start_rest", [payloads[2] + tok[0, 0].astype(BF16), payloads[3], payloads[4]])
    rc, rsa, rsb = _rope_tables(seq)
    sinks = attn_sinks.reshape(N_Q_HEADS)
    g_mix0, g_mix1 = norm_mix[0:1] + tok[0, 0], norm_mix[1:2]
    g_ffn0, g_ffn1 = norm_ffn[0:1], norm_ffn[1:2]
    g_fin = norm_final.reshape(1, D_MODEL)

    w_qkv_t = _gather_wait("ag_wait_qkv", ag_qkv, tok).reshape(QKV_DIM, D_MODEL)
    y0, q, k, vv = _qkv_fwd(xs, g_mix0, w_qkv_t, attn_b_qkv, rc, rsa, rsb)
    attn, probs, p_sinks = _attn_fwd(sinks, q, k, vv)
    w_o = _gather_wait("ag_wait_wo", ag_wo, attn).reshape(Q_DIM, D_MODEL)
    h1 = _mm_res("attn_out_proj", attn, w_o, attn_b_o, xs)
    w_ffn0 = _gather_wait("ag_wait_ffn0", ag_ffn0, h1)
    f0, u0, p0, s0 = _ffn_up("ffn0_up", h1, g_ffn0, w_ffn0)
    h2 = _ffn_down("ffn0_down", s0, w_ffn0, h1)
    wt = _unpack_conv(_gather_wait("ag_wait_conv", ag_conv, h2))
    wd = jnp.concatenate([wt["w_dw"][::-1], jnp.zeros((TAPS_PAD - CONV_WIDTH, D_MODEL), F32)], axis=0)
    y1, a, dwc, z = _conv_fwd(h2, g_mix1, wt["w_pw1_t"], wt["b_pw1"], wd, wt["b_dw"], wt["ln_g"], wt["ln_b"])
    h3 = _mm_res("conv_out_proj", z, wt["w_pw2"], wt["b_pw2"], h2)
    w_ffn1 = _gather_wait("ag_wait_ffn1", ag_ffn1, h3)
    f1, u1, p1, s1 = _ffn_up("ffn1_up", h3, g_ffn1, w_ffn1)
    dh4, dh4b, sq, dg_fin = _ffn_down_loss(s1, w_ffn1, h3, target, g_fin)

    du1, dp1 = _ffn_bwd_act("ffn1_bwd_act", dh4b, u1, p1, w_ffn1)
    dh3, dh3b, dg_ffn1 = _ffn_bwd_in("ffn1_bwd_in", du1, dp1, w_ffn1, h3, dh4, g_ffn1)
    blocks = _mm_tn("ffn1_dw2", s1, dh4b, tk=FF_TILE, into=(None, W2_SLOT))
    blocks = _mm_tn("ffn1_dw1", du1, f1, tk=FF_TILE, into=(blocks, W1T_SLOT))
    blocks = _mm_tn("ffn1_dw3", dp1, f1, tk=FF_TILE, into=(blocks, W3T_SLOT))
    rs_ffn1, tok = _scatter_start("rs_start_ffn1", blocks)

    da, dlg, dlb, dbdw, dwd, dbpw1, dbpw2 = _conv_bwd(dh3, wt["w_pw2"], dwc, a, wt["ln_g"] + tok[0, 0],
                                                     wt["ln_b"], wd)
    gpw2 = _mm_tn("conv_dw_pw2", z, dh3b, tk=D_MODEL)
    dh2, dh2b, dg_mix1 = _mm_rms_bwd("conv_in_bwd", da, wt["w_pw1_t"], h2, dh3, g_mix1, True)
    gpw1t = _mm_tn("conv_dw_pw1", da, y1, tk=D_MODEL)
    small_rows = _small_grad_rows(dbpw1, dwd[:CONV_WIDTH][::-1], dbdw, dlg, dlb, dbpw2)
    rs_conv, tok = _scatter_start("rs_start_conv", _dest_blocks([gpw1t, gpw2, small_rows]))

    du0, dp0 = _ffn_bwd_act("ffn0_bwd_act", dh2b, u0, p0, w_ffn0)
    dh1, dh1b, dg_ffn0 = _ffn_bwd_in("ffn0_bwd_in", du0, dp0, w_ffn0, h1, dh2, g_ffn0 + tok[0, 0])
    blocks = _mm_tn("ffn0_dw2", s0, dh2b, tk=FF_TILE, into=(None, W2_SLOT))
    blocks = _mm_tn("ffn0_dw1", du0, f0, tk=FF_TILE, into=(blocks, W1T_SLOT))
    blocks = _mm_tn("ffn0_dw3", dp0, f0, tk=FF_TILE, into=(blocks, W3T_SLOT))
    rs_ffn0, tok = _scatter_start("rs_start_ffn0", blocks)

    gwo = _mm_tn("attn_dw_o", attn, dh1b, tk=D_MODEL)
    rs_wo, tok2 = _scatter_start("rs_start_wo", _dest_blocks([gwo]))
    dattn, dbo = _nt_bias("attn_out_bwd", dh1, w_o)
    dq, dkc, dkp, dvc, dvp, dsink = _attn_bwd(probs, p_sinks + (tok[0, 0] + tok2[0, 0]), q, k, vv, dattn)
    dqkv, dbqkv = _rope_bwd(dq, dkc, dkp, dvc, dvp, rc, rsa, rsb)
    gqkvt = _mm_tn("attn_dw_qkv", dqkv, y0, tk=QKV_DIM)
    rs_qkv, tok = _scatter_start("rs_start_qkv", _dest_blocks([gqkvt]))
    dx, dg_mix0 = _mm_rms_bwd("qkv_in_bwd", dqkv, w_qkv_t, xs, dh1, g_mix0 + tok[0, 0], False)

    p_ffn1, p_conv, p_ffn0, p_wo, p_qkv = _scatter_wait([rs_ffn1, rs_conv, rs_ffn0, rs_wo, rs_qkv], dx)
    g_ffn = [_sum_slots("rs_sum_ffn0", p_ffn0), _sum_slots("rs_sum_ffn1", p_ffn1)]
    g_conv = _sum_slots("rs_sum_conv", p_conv)
    g_wo = _sum_slots("rs_sum_wo", p_wo)
    g_qkv = _sum_slots("rs_sum_qkv", p_qkv)

    def ff(slot):
        return jnp.stack([g[slot * ROWS_FF:(slot + 1) * ROWS_FF] for g in g_ffn])

    tr = lambda a: jnp.swapaxes(a, -1, -2)
    grads = dict(
        ffn_w1=ff(W1T_SLOT), ffn_w3=ff(W3T_SLOT), ffn_w2=ff(W2_SLOT), attn_w_qkv=g_qkv[None], attn_w_o=g_wo[None],
        conv_w_pw1=g_conv[:ROWS_PW1].T[None], conv_w_pw2=g_conv[ROWS_PW1:ROWS_PW1 + ROWS_PW2][None])
    transposed = ("ffn_w1", "ffn_w3", "attn_w_qkv")
    sharded = [{}, {}, {}, {}]
    for n in grads:
        view = tr if n in transposed else (lambda a: a)
        outs_n = _adamw_native("adamw_" + n, grads[n], view(w[n]), view(m[n]), view(v[n]))
        for dst, t in zip(sharded, (grads[n],) + tuple(outs_n)):
            dst[n] = view(t)
    g_small = g_conv[ROWS_PW1 + ROWS_PW2:][None]
    small_out = _adamw_native("adamw_small", g_small, _pack_small(w), _pack_small(m), _pack_small(v))
    for dst, t in zip(sharded, (g_small,) + tuple(small_out)):
        dst.update(_unpack_small(t))

    part = _pack_replicated(jnp.concatenate([dg_mix0, dg_mix1]), jnp.concatenate([dg_ffn0, dg_ffn1]),
                            dbqkv, -dsink[:, 0], dbo, dg_fin, extra=sq[0, 0:1])
    parts = _all_gather("ag_replicated_grads", part)
    w_rep = _pack_replicated(norm_mix, norm_ffn, attn_b_qkv, attn_sinks, attn_b_o, norm_final)
    m_rep = _pack_replicated(m_norm_mix, m_norm_ffn, m_attn_b_qkv, m_attn_sinks, m_attn_b_o, m_norm_final)
    v_rep = _pack_replicated(v_norm_mix, v_norm_ffn, v_attn_b_qkv, v_attn_sinks, v_attn_b_o, v_norm_final)
    rep_out = _adamw_replicated(parts, w_rep, m_rep, v_rep)
    replicated = [_unpack_replicated(t) for t in rep_out]
    loss = rep_out[0][LOSS_ROW, 0] * (0.5 / D_MODEL)

    outs = [loss, dx.reshape(1, seq, D_MODEL)]
    for sh, rp in zip(sharded, replicated):
        merged = {**sh, **rp}
        outs += [merged[n] for n in WEIGHT_ORDER]
    return tuple(outs)
```

```python
import jax
import jax.numpy as jnp
from jax import lax
from jax.experimental import pallas as pl
from jax.experimental.pallas import tpu as pltpu

F32 = jnp.float32
BF16 = jnp.bfloat16

D_MODEL = 1024
HEAD_DIM = 64
N_Q_HEADS = 16
N_KV_HEADS = 2
Q_PER_KV = 8
Q_DIM = N_Q_HEADS * HEAD_DIM
KV_DIM = N_KV_HEADS * HEAD_DIM
QKV_DIM = Q_DIM + 2 * KV_DIM
BLOCK = 128
CONV_WIDTH = 31
D_FF = 2816
ROPE_THETA = 10000.0
RMS_EPS = 1e-5
LN_EPS = 1e-5
ADAM_LR = 0.001
ADAM_B1 = 0.9
ADAM_B2 = 0.999
ADAM_EPS = 1e-08
ADAM_WD = 0.01
ADAM_STEP = 10
N_DEV = 8

LANES = 128
SUBLANES = 8
TOKEN_TILE = 512
CONV_CHUNK = 64
CONV_HALO = 32
TAPS_PAD = 32
VMEM_LIMIT = 56 * 1024 * 1024
NEG_INF = float(jnp.finfo(jnp.float32).min)

ROWS_FF = D_FF // N_DEV
W1T_SLOT, W3T_SLOT, W2_SLOT = 0, 1, 2
ROWS_QKV = QKV_DIM // N_DEV
ROWS_WO = Q_DIM // N_DEV
ROWS_PW1 = 2 * D_MODEL // N_DEV
ROWS_PW2 = D_MODEL // N_DEV
ROWS_SMALL = 16
SMALL_USED = 2 * D_MODEL // N_DEV + CONV_WIDTH * (D_MODEL // N_DEV) + 4 * (D_MODEL // N_DEV)
FF_SPLIT = 2
FF_TILE_DEVS = N_DEV // FF_SPLIT
FF_TILE = FF_TILE_DEVS * ROWS_FF
REPL_ROWS = 16


def _call(body, *, name, grid, in_specs, out_specs, out_shape, scratch=(), sem=None, aliases=None):
    return pl.pallas_call(
        body, name=name, grid=grid, in_specs=in_specs, out_specs=out_specs, out_shape=out_shape,
        scratch_shapes=list(scratch), input_output_aliases=aliases or {},
        compiler_params=pltpu.CompilerParams(dimension_semantics=sem, vmem_limit_bytes=VMEM_LIMIT))


def _full(shape):
    return pl.BlockSpec(shape, lambda *_: (0,) * len(shape))


def _resident(shape):
    return pl.BlockSpec(shape, lambda *_: (0,) * len(shape), pipeline_mode=pl.Buffered(1))


def _rows(tm, n):
    return pl.BlockSpec((tm, n), lambda i, *_: (i, 0))


def _sig(x):
    return 1.0 / (1.0 + jnp.exp(-x))


def _sum_rows(x):
    return jnp.sum(x, axis=0, keepdims=True)


def _nt(a, b):
    return lax.dot_general(a, b, (((1,), (1,)), ((), ())), preferred_element_type=F32)


def _tn(a, b):
    return lax.dot_general(a, b, (((0,), (0,)), ((), ())), preferred_element_type=F32)


def _rms_stats(x):
    return lax.rsqrt(jnp.mean(x * x, axis=-1, keepdims=True) + RMS_EPS)


def _rms_bwd(dy, x, g, dres):
    r = _rms_stats(x)
    n = x * r
    dn = dy * g
    dx = dres + r * (dn - n * jnp.mean(dn * n, axis=-1, keepdims=True))
    return dx, _sum_rows(dy * n)


def _rope_tables(seq):
    half = HEAD_DIM // 2
    pos = jnp.arange(seq, dtype=F32)
    lane = jnp.arange(LANES)
    inv_freq = ROPE_THETA ** (-(2 * (lane % half)).astype(F32) / HEAD_DIM)
    ang = pos[:, None] * inv_freq[None, :]
    cos, sin = jnp.cos(ang), jnp.sin(ang)
    first_half = (lane % HEAD_DIM < half)[None, :]
    sa = jnp.where(first_half, -sin, 0.0)
    sb = jnp.where(first_half, 0.0, sin)
    return cos, sa, sb


def _rope(t, c, sa, sb):
    half = HEAD_DIM // 2
    return t * c + pltpu.roll(t, LANES - half, 1) * sa + pltpu.roll(t, half, 1) * sb


def _rope_t(dt, c, sa, sb):
    half = HEAD_DIM // 2
    return dt * c + pltpu.roll(dt * sa, half, 1) + pltpu.roll(dt * sb, LANES - half, 1)


def _qkv_fwd(x, g, w, b, rc, rsa, rsb):
    seq = x.shape[0]
    tm = min(TOKEN_TILE, seq)

    def body(x_ref, g_ref, w_ref, b_ref, c_ref, sa_ref, sb_ref, y_ref, q_ref, k_ref, v_ref):
        xv = x_ref[...]
        y = (xv * _rms_stats(xv) * g_ref[...]).astype(BF16)
        y_ref[...] = y
        qkv = _nt(y, w_ref[...]) + b_ref[...]
        c, sa, sb = c_ref[...], sa_ref[...], sb_ref[...]
        for i in range(Q_DIM // LANES):
            blk = _rope(qkv[:, i * LANES:(i + 1) * LANES], c, sa, sb)
            q_ref[:, i * LANES:(i + 1) * LANES] = (blk * (HEAD_DIM ** -0.5)).astype(BF16)
        k_ref[...] = _rope(qkv[:, Q_DIM:Q_DIM + KV_DIM], c, sa, sb).astype(BF16)
        v_ref[...] = qkv[:, Q_DIM + KV_DIM:].astype(BF16)

    return _call(
        body, name="qkv_fwd", grid=(seq // tm,),
        in_specs=[_rows(tm, D_MODEL), _full((1, D_MODEL)), _full((QKV_DIM, D_MODEL)), _full((1, QKV_DIM)),
                  _rows(tm, LANES), _rows(tm, LANES), _rows(tm, LANES)],
        out_specs=[_rows(tm, D_MODEL), _rows(tm, Q_DIM), _rows(tm, KV_DIM), _rows(tm, KV_DIM)],
        out_shape=[jax.ShapeDtypeStruct((seq, D_MODEL), BF16), jax.ShapeDtypeStruct((seq, Q_DIM), BF16),
                   jax.ShapeDtypeStruct((seq, KV_DIM), BF16), jax.ShapeDtypeStruct((seq, KV_DIM), BF16)],
        sem=("parallel",))(x, g, w, b, rc, rsa, rsb)


def _band_mask(n):
    row = lax.broadcasted_iota(jnp.int32, (BLOCK, 2 * BLOCK), 0)
    col = lax.broadcasted_iota(jnp.int32, (BLOCK, 2 * BLOCK), 1)
    rel = row + BLOCK - col
    return (rel >= 0) & (rel < BLOCK) & ((col >= BLOCK) | (n > 0))


def _softmax_with_sink(s, mask, sink):
    s = jnp.where(mask, s, NEG_INF)
    m = jnp.maximum(jnp.max(s, axis=-1, keepdims=True), sink)
    p = jnp.exp(s - m)
    e_sink = jnp.exp(sink - m)
    inv = 1.0 / (jnp.sum(p, axis=-1, keepdims=True) + e_sink)
    return p * inv, e_sink * inv


PAIRS_PER_KV = Q_PER_KV // 2


def _kv_specs():
    cur = pl.BlockSpec((BLOCK, KV_DIM), lambda n: (n, 0))
    prev = pl.BlockSpec((BLOCK, KV_DIM), lambda n: (jnp.maximum(n - 1, 0), 0))
    return cur, prev


def _low_lanes():
    return lax.broadcasted_iota(jnp.int32, (2 * BLOCK, KV_DIM), 1) < HEAD_DIM


def _kv_low_high(prev_ref, cur_ref, j, low):
    both = jnp.concatenate([prev_ref[...], cur_ref[...]], axis=0).astype(F32)
    swapped = pltpu.roll(both, HEAD_DIM, 1)
    at_low, at_high = (both, swapped) if j == 0 else (swapped, both)
    return jnp.where(low, at_low, 0.0).astype(BF16), jnp.where(low, 0.0, at_high).astype(BF16)


def _fold_pair_halves(acc, j, low):
    folded = acc + pltpu.roll(acc, HEAD_DIM, 1)
    return jnp.where(low, folded, 0.0) if j == 0 else jnp.where(low, 0.0, folded)


def _pair_lanes(j, i):
    g = j * PAIRS_PER_KV + i
    return slice(g * LANES, (g + 1) * LANES), 2 * g


def _attn_fwd(sinks, q, k, v):
    seq = q.shape[0]
    cur, prev = _kv_specs()

    def body(sink_ref, q_ref, kc_ref, kp_ref, vc_ref, vp_ref, o_ref, p_ref, ps_ref):
        mask = _band_mask(pl.program_id(0))
        lane = lax.broadcasted_iota(jnp.int32, (BLOCK, LANES), 1)
        p_sinks = jnp.zeros((BLOCK, LANES), F32)
        for j in range(N_KV_HEADS):
            cs = slice(j * HEAD_DIM, (j + 1) * HEAD_DIM)
            kk = jnp.concatenate([kp_ref[:, cs], kc_ref[:, cs]], axis=0)
            vv = jnp.concatenate([vp_ref[:, cs], vc_ref[:, cs]], axis=0)
            for gq in range(Q_PER_KV):
                h = j * Q_PER_KV + gq
                hs = slice(h * HEAD_DIM, (h + 1) * HEAD_DIM)
                probs, p_sink = _softmax_with_sink(_nt(q_ref[:, hs], kk), mask, sink_ref[h])
                pb = probs.astype(BF16)
                p_ref[h] = pb
                p_sinks = jnp.where(lane == h, p_sink, p_sinks)
                o_ref[:, hs] = jnp.dot(pb, vv, preferred_element_type=F32).astype(BF16)
        ps_ref[...] = p_sinks

    return _call(
        body, name="attn_fwd", grid=(seq // BLOCK,),
        in_specs=[pl.BlockSpec(memory_space=pltpu.SMEM), _rows(BLOCK, Q_DIM), cur, prev, cur, prev],
        out_specs=[_rows(BLOCK, Q_DIM), pl.BlockSpec((N_Q_HEADS, BLOCK, 2 * BLOCK), lambda n: (0, n, 0)),
                   _rows(BLOCK, LANES)],
        out_shape=[jax.ShapeDtypeStruct((seq, Q_DIM), BF16),
                   jax.ShapeDtypeStruct((N_Q_HEADS, seq, 2 * BLOCK), BF16),
                   jax.ShapeDtypeStruct((seq, LANES), F32)],
        sem=("parallel",))(sinks, q, k, k, v, v)


def _mm_res(name, a, w, b, res):
    seq, kdim = a.shape
    n = w.shape[1]
    tm = min(2 * TOKEN_TILE, seq)

    def body(a_ref, w_ref, b_ref, r_ref, o_ref):
        o_ref[...] = r_ref[...] + (jnp.dot(a_ref[...], w_ref[...], preferred_element_type=F32) + b_ref[...])

    return _call(
        body, name=name, grid=(seq // tm,),
        in_specs=[_rows(tm, kdim), _resident((kdim, n)), _full((1, n)), _rows(tm, n)],
        out_specs=_rows(tm, n), out_shape=jax.ShapeDtypeStruct((seq, n), F32),
        sem=("parallel",))(a, w, b, res)


def _ff_tile_spec(slot):
    return pl.BlockSpec((FF_TILE_DEVS, ROWS_FF, D_MODEL), lambda j, i: (j, slot, 0))


def _ff_whole_spec(slot):
    return pl.BlockSpec((N_DEV, ROWS_FF, D_MODEL), lambda i: (0, slot, 0), pipeline_mode=pl.Buffered(1))


def _ffn_up(name, h, g, gathered):
    seq = h.shape[0]
    tm = min(TOKEN_TILE, seq)

    def body(h_ref, g_ref, w1_ref, w3_ref, f_ref, u_ref, w_ref, s_ref):
        hv = h_ref[...]
        f = (hv * _rms_stats(hv) * g_ref[...]).astype(BF16)
        f_ref[...] = f
        u = _nt(f, w1_ref[...].reshape(FF_TILE, D_MODEL))
        w = _nt(f, w3_ref[...].reshape(FF_TILE, D_MODEL))
        u_ref[...] = u.astype(BF16)
        w_ref[...] = w.astype(BF16)
        s_ref[...] = (u * _sig(u) * w).astype(BF16)

    n_t = seq // tm
    tile_ff = pl.BlockSpec((tm, FF_TILE), lambda j, i: (i, j))
    f_spec = pl.BlockSpec((tm, D_MODEL), lambda j, i: (jnp.where(j == 0, i, n_t), 0))
    ff_shape = jax.ShapeDtypeStruct((seq, D_FF), BF16)
    return _call(
        body, name=name, grid=(FF_SPLIT, n_t),
        in_specs=[pl.BlockSpec((tm, D_MODEL), lambda j, i: (i, 0)), _full((1, D_MODEL)),
                  _ff_tile_spec(W1T_SLOT), _ff_tile_spec(W3T_SLOT)],
        out_specs=[f_spec, tile_ff, tile_ff, tile_ff],
        out_shape=[jax.ShapeDtypeStruct((seq + tm, D_MODEL), BF16), ff_shape, ff_shape, ff_shape],
        sem=("arbitrary", "arbitrary"))(h, g, gathered, gathered)


def _ffn_down(name, s, gathered, res):
    seq = s.shape[0]
    tm = min(2 * TOKEN_TILE, seq)

    def body(s_ref, w_ref, r_ref, o_ref):
        w2 = w_ref[...].reshape(D_FF, D_MODEL)
        o_ref[...] = r_ref[...] + jnp.dot(s_ref[...], w2, preferred_element_type=F32)

    return _call(
        body, name=name, grid=(seq // tm,),
        in_specs=[_rows(tm, D_FF), _ff_whole_spec(W2_SLOT), _rows(tm, D_MODEL)],
        out_specs=_rows(tm, D_MODEL), out_shape=jax.ShapeDtypeStruct((seq, D_MODEL), F32),
        sem=("parallel",))(s, gathered, res)


def _conv_fwd(h, g, wpw1, bpw1, wd, bdw, lng, lnb):
    seq = h.shape[0]
    tm = min(TOKEN_TILE, seq)
    n_chunks = tm // CONV_CHUNK
    win = CONV_CHUNK + CONV_HALO

    def body(h_ref, g_ref, w_ref, b_ref, wd_ref, bdw_ref, lng_ref, lnb_ref,
             y_ref, a_ref, dwc_ref, z_ref, gbuf):
        i = pl.program_id(0)

        @pl.when(i == 0)
        def _():
            gbuf[0:CONV_HALO, :] = jnp.zeros((CONV_HALO, D_MODEL), F32)

        @pl.when(i > 0)
        def _():
            gbuf[0:CONV_HALO, :] = gbuf[tm:tm + CONV_HALO, :]

        hv = h_ref[...]
        y = (hv * _rms_stats(hv) * g_ref[...]).astype(BF16)
        y_ref[...] = y
        a = _nt(y, w_ref[...]) + b_ref[...]
        a_ref[...] = a.astype(BF16)
        gbuf[CONV_HALO:CONV_HALO + tm, :] = a[:, :D_MODEL] * _sig(a[:, D_MODEL:])

        def chunk(c, carry):
            r0 = pl.multiple_of(c * CONV_CHUNK, CONV_CHUNK)
            for l in range(D_MODEL // LANES):
                ls = slice(l * LANES, (l + 1) * LANES)
                gw = gbuf[pl.ds(r0, win), ls]
                acc = jnp.zeros((CONV_CHUNK, LANES), F32) + bdw_ref[:, ls]
                for s in range(SUBLANES):
                    gs = gw if s == 0 else pltpu.roll(gw, s, 0)
                    for q in range(CONV_HALO // SUBLANES):
                        d = SUBLANES * q + s
                        if d < CONV_WIDTH:
                            lo = CONV_HALO - SUBLANES * q
                            acc = acc + wd_ref[d:d + 1, ls] * gs[lo:lo + CONV_CHUNK]
                dwc_ref[pl.ds(r0, CONV_CHUNK), ls] = acc
            return carry

        lax.fori_loop(0, n_chunks, chunk, 0)

        xv = dwc_ref[...]
        mu = jnp.mean(xv, axis=-1, keepdims=True)
        xc = xv - mu
        var = jnp.mean(xc * xc, axis=-1, keepdims=True)
        ln = xc * lax.rsqrt(var + LN_EPS) * lng_ref[...] + lnb_ref[...]
        z_ref[...] = (ln * _sig(ln)).astype(BF16)

    return _call(
        body, name="conv_fwd", grid=(seq // tm,),
        in_specs=[_rows(tm, D_MODEL), _full((1, D_MODEL)), _full((2 * D_MODEL, D_MODEL)), _full((1, 2 * D_MODEL)),
                  _full((TAPS_PAD, D_MODEL)), _full((1, D_MODEL)), _full((1, D_MODEL)), _full((1, D_MODEL))],
        out_specs=[_rows(tm, D_MODEL), _rows(tm, 2 * D_MODEL), _rows(tm, D_MODEL), _rows(tm, D_MODEL)],
        out_shape=[jax.ShapeDtypeStruct((seq, D_MODEL), BF16), jax.ShapeDtypeStruct((seq, 2 * D_MODEL), BF16),
                   jax.ShapeDtypeStruct((seq, D_MODEL), F32), jax.ShapeDtypeStruct((seq, D_MODEL), BF16)],
        scratch=[pltpu.VMEM((tm + CONV_HALO, D_MODEL), F32)],
        sem=("arbitrary",))(h, g, wpw1, bpw1, wd, bdw, lng, lnb)


def _ffn_down_loss(s, gathered, res, target, g):
    seq = s.shape[0]
    tm = min(TOKEN_TILE, seq)

    def body(s_ref, w_ref, r_ref, t_ref, g_ref, dh_ref, dhb_ref, loss_ref, dg_ref):
        @pl.when(pl.program_id(0) == 0)
        def _():
            loss_ref[...] = jnp.zeros_like(loss_ref)
            dg_ref[...] = jnp.zeros_like(dg_ref)

        hv = r_ref[...] + jnp.dot(s_ref[...], w_ref[...].reshape(D_FF, D_MODEL), preferred_element_type=F32)
        gv = g_ref[...]
        err = hv * _rms_stats(hv) * gv - t_ref[...]
        sq = jnp.sum(jnp.sum(err * err, axis=-1, keepdims=True), axis=0, keepdims=True)
        loss_ref[...] += jnp.broadcast_to(sq, loss_ref.shape)
        dx, dg = _rms_bwd(err * (1.0 / D_MODEL), hv, gv, 0.0)
        dh_ref[...] = dx
        dhb_ref[...] = dx.astype(BF16)
        dg_ref[...] += dg

    return _call(
        body, name="ffn1_down_loss", grid=(seq // tm,),
        in_specs=[_rows(tm, D_FF), _ff_whole_spec(W2_SLOT), _rows(tm, D_MODEL), _rows(tm, D_MODEL),
                  _full((1, D_MODEL))],
        out_specs=[_rows(tm, D_MODEL), _rows(tm, D_MODEL), _full((SUBLANES, LANES)), _full((1, D_MODEL))],
        out_shape=[jax.ShapeDtypeStruct((seq, D_MODEL), F32), jax.ShapeDtypeStruct((seq, D_MODEL), BF16),
                   jax.ShapeDtypeStruct((SUBLANES, LANES), F32), jax.ShapeDtypeStruct((1, D_MODEL), F32)],
        sem=("arbitrary",))(s, gathered, res, target, g)


def _ffn_bwd_act(name, dh, u, w, gathered):
    seq = dh.shape[0]
    tm = min(TOKEN_TILE, seq)

    def body(dh_ref, u_ref, w_ref, w2_ref, du_ref, dw_ref):
        ds = _nt(dh_ref[...], w2_ref[...].reshape(FF_TILE, D_MODEL))
        uv = u_ref[...].astype(F32)
        sg = _sig(uv)
        dw_ref[...] = (ds * (uv * sg)).astype(BF16)
        du_ref[...] = (ds * w_ref[...].astype(F32) * (sg * (1.0 + uv * (1.0 - sg)))).astype(BF16)

    tile_ff = pl.BlockSpec((tm, FF_TILE), lambda j, i: (i, j))
    ff_shape = jax.ShapeDtypeStruct((seq, D_FF), BF16)
    return _call(
        body, name=name, grid=(FF_SPLIT, seq // tm),
        in_specs=[pl.BlockSpec((tm, D_MODEL), lambda j, i: (i, 0)), tile_ff, tile_ff, _ff_tile_spec(W2_SLOT)],
        out_specs=[tile_ff, tile_ff], out_shape=[ff_shape, ff_shape],
        sem=("parallel", "parallel"))(dh, u, w, gathered)


def _ffn_bwd_in(name, du, dw, gathered, h_in, dres, g):
    seq = du.shape[0]
    tm = min(TOKEN_TILE, seq)

    def body(du_ref, dw_ref, w1_ref, w3_ref, h_ref, dr_ref, g_ref, dx_ref, dxb_ref, dg_ref):
        @pl.when(pl.program_id(0) == 0)
        def _():
            dg_ref[...] = jnp.zeros_like(dg_ref)

        df = jnp.dot(du_ref[...], w1_ref[...].reshape(D_FF, D_MODEL), preferred_element_type=F32)
        df = df + jnp.dot(dw_ref[...], w3_ref[...].reshape(D_FF, D_MODEL), preferred_element_type=F32)
        dx, dg = _rms_bwd(df, h_ref[...], g_ref[...], dr_ref[...])
        dx_ref[...] = dx
        dxb_ref[...] = dx.astype(BF16)
        dg_ref[...] += dg

    return _call(
        body, name=name, grid=(seq // tm,),
        in_specs=[_rows(tm, D_FF), _rows(tm, D_FF), _ff_whole_spec(W1T_SLOT), _ff_whole_spec(W3T_SLOT),
                  _rows(tm, D_MODEL), _rows(tm, D_MODEL), _full((1, D_MODEL))],
        out_specs=[_rows(tm, D_MODEL), _rows(tm, D_MODEL), _full((1, D_MODEL))],
        out_shape=[jax.ShapeDtypeStruct((seq, D_MODEL), F32), jax.ShapeDtypeStruct((seq, D_MODEL), BF16),
                   jax.ShapeDtypeStruct((1, D_MODEL), F32)],
        sem=("arbitrary",))(du, dw, gathered, gathered, h_in, dres, g)


def _mm_tn(name, a, b, *, tk, into=None):
    seq, kdim = a.shape
    n = b.shape[1]
    tt = min((4 if b.dtype == BF16 else 2) * TOKEN_TILE, seq)
    n_t = seq // tt
    devs = tk // ROWS_FF

    def body(a_ref, b_ref, *rest):
        o_ref, acc = rest[-2:]
        t = pl.program_id(1)

        @pl.when(t == 0)
        def _():
            acc[...] = jnp.zeros_like(acc)

        acc[...] += _tn(a_ref[...].astype(BF16), b_ref[...].astype(BF16))

        @pl.when(t == n_t - 1)
        def _():
            out = acc[...].astype(BF16)
            o_ref[...] = out if into is None else out.reshape(devs, ROWS_FF, n)

    in_specs = [pl.BlockSpec((tt, tk), lambda k, t: (t, k)), pl.BlockSpec((tt, n), lambda k, t: (t, 0))]
    args = [a, b]
    aliases = None
    if into is None:
        out_spec = pl.BlockSpec((tk, n), lambda k, t: (k, 0))
        out_shape = jax.ShapeDtypeStruct((kdim, n), BF16)
    else:
        blocks, slot = into
        out_spec = pl.BlockSpec((devs, ROWS_FF, n), lambda k, t: (k, slot, 0))
        out_shape = jax.ShapeDtypeStruct((N_DEV, 3 * ROWS_FF, n), BF16)
        if blocks is not None:
            in_specs.append(_ANY)
            args.append(blocks)
            aliases = {2: 0}
    return _call(
        body, name=name, grid=(kdim // tk, n_t), in_specs=in_specs, out_specs=out_spec, out_shape=out_shape,
        scratch=[pltpu.VMEM((tk, n), F32)], sem=("parallel", "arbitrary"), aliases=aliases)(*args)


def _conv_bwd(dh, wpw2, dwc, a, lng, lnb, wd):
    seq = dh.shape[0]
    tm = min(TOKEN_TILE, seq)
    nt = seq // tm
    n_chunks = tm // CONV_CHUNK
    win = CONV_CHUNK + CONV_HALO
    halo_per_tile = tm // CONV_HALO

    def body(dh_ref, w_ref, dwc_ref, a_ref, ah_ref, lng_ref, lnb_ref, wd_ref,
             da_ref, dlg_ref, dlb_ref, dbdw_ref, dwd_ref, dbpw1_ref, dbpw2_ref,
             gbuf, dbuf, dglu, dwd_part):
        i = pl.program_id(0)
        r = nt - 1 - i

        @pl.when(i == 0)
        def _():
            dlg_ref[...] = jnp.zeros_like(dlg_ref)
            dlb_ref[...] = jnp.zeros_like(dlb_ref)
            dbdw_ref[...] = jnp.zeros_like(dbdw_ref)
            dbpw1_ref[...] = jnp.zeros_like(dbpw1_ref)
            dbpw2_ref[...] = jnp.zeros_like(dbpw2_ref)
            dwd_part[...] = jnp.zeros_like(dwd_part)
            dbuf[tm:tm + CONV_HALO, :] = jnp.zeros((CONV_HALO, D_MODEL), F32)

        @pl.when(i > 0)
        def _():
            dbuf[tm:tm + CONV_HALO, :] = dbuf[0:CONV_HALO, :]

        dhv = dh_ref[...]
        dbpw2_ref[...] += _sum_rows(dhv)
        dz = _nt(dhv.astype(BF16), w_ref[...])
        xv = dwc_ref[...]
        lg = lng_ref[...]
        mu = jnp.mean(xv, axis=-1, keepdims=True)
        xc = xv - mu
        rstd = lax.rsqrt(jnp.mean(xc * xc, axis=-1, keepdims=True) + LN_EPS)
        xhat = xc * rstd
        ln = xhat * lg + lnb_ref[...]
        sg = _sig(ln)
        dln = dz * (sg * (1.0 + ln * (1.0 - sg)))
        dlg_ref[...] += _sum_rows(dln * xhat)
        dlb_ref[...] += _sum_rows(dln)
        dxh = dln * lg
        ddw = rstd * (dxh - jnp.mean(dxh, axis=-1, keepdims=True)
                      - xhat * jnp.mean(dxh * xhat, axis=-1, keepdims=True))
        dbdw_ref[...] += _sum_rows(ddw)
        dbuf[0:tm, :] = ddw

        av = a_ref[...].astype(F32)
        a1 = av[:, :D_MODEL]
        s2 = _sig(av[:, D_MODEL:])
        gbuf[CONV_HALO:CONV_HALO + tm, :] = a1 * s2
        ah = ah_ref[...].astype(F32)
        gh = ah[:, :D_MODEL] * _sig(ah[:, D_MODEL:])
        gbuf[0:CONV_HALO, :] = jnp.where(r > 0, gh, 0.0)

        def chunk(c, carry):
            r0 = pl.multiple_of(c * CONV_CHUNK, CONV_CHUNK)
            for l in range(D_MODEL // LANES):
                ls = slice(l * LANES, (l + 1) * LANES)
                dw_ = dbuf[pl.ds(r0, win), ls]
                gw = gbuf[pl.ds(r0, win), ls]
                acc = jnp.zeros((CONV_CHUNK, LANES), F32)
                for s in range(SUBLANES):
                    ds_ = dw_ if s == 0 else pltpu.roll(dw_, win - s, 0)
                    for q in range(CONV_HALO // SUBLANES):
                        d = SUBLANES * q + s
                        if d < CONV_WIDTH:
                            acc = acc + wd_ref[d:d + 1, ls] * ds_[SUBLANES * q:SUBLANES * q + CONV_CHUNK]
                            lo = CONV_HALO - SUBLANES * q
                            prod = ds_[0:CONV_CHUNK] * gw[lo:lo + CONV_CHUNK]
                            dwd_part[d, :, ls] += jnp.sum(
                                prod.reshape(CONV_CHUNK // SUBLANES, SUBLANES, LANES), axis=0)
                dglu[pl.ds(r0, CONV_CHUNK), ls] = acc
            return carry

        lax.fori_loop(0, n_chunks, chunk, 0)

        dg_ = dglu[...]
        da1 = dg_ * s2
        da2 = dg_ * a1 * s2 * (1.0 - s2)
        da_ref[:, :D_MODEL] = da1.astype(BF16)
        da_ref[:, D_MODEL:] = da2.astype(BF16)
        dbpw1_ref[:, :D_MODEL] += _sum_rows(da1)
        dbpw1_ref[:, D_MODEL:] += _sum_rows(da2)

        @pl.when(i == nt - 1)
        def _():
            dwd_ref[...] = jnp.sum(dwd_part[...], axis=1)

    rev = lambda n: pl.BlockSpec((tm, n), lambda i: (nt - 1 - i, 0))
    halo = pl.BlockSpec((CONV_HALO, 2 * D_MODEL),
                        lambda i: (jnp.maximum((nt - 1 - i) * halo_per_tile - 1, 0), 0))
    vec = lambda n: _full((1, n))
    return _call(
        body, name="conv_bwd", grid=(nt,),
        in_specs=[rev(D_MODEL), _full((D_MODEL, D_MODEL)), rev(D_MODEL), rev(2 * D_MODEL), halo,
                  vec(D_MODEL), vec(D_MODEL), _full((TAPS_PAD, D_MODEL))],
        out_specs=[rev(2 * D_MODEL), vec(D_MODEL), vec(D_MODEL), vec(D_MODEL), _full((TAPS_PAD, D_MODEL)),
                   vec(2 * D_MODEL), vec(D_MODEL)],
        out_shape=[jax.ShapeDtypeStruct((seq, 2 * D_MODEL), BF16), jax.ShapeDtypeStruct((1, D_MODEL), F32),
                   jax.ShapeDtypeStruct((1, D_MODEL), F32), jax.ShapeDtypeStruct((1, D_MODEL), F32),
                   jax.ShapeDtypeStruct((TAPS_PAD, D_MODEL), F32), jax.ShapeDtypeStruct((1, 2 * D_MODEL), F32),
                   jax.ShapeDtypeStruct((1, D_MODEL), F32)],
        scratch=[pltpu.VMEM((tm + CONV_HALO, D_MODEL), F32), pltpu.VMEM((tm + CONV_HALO, D_MODEL), F32),
                 pltpu.VMEM((tm, D_MODEL), F32), pltpu.VMEM((TAPS_PAD, SUBLANES, D_MODEL), F32)],
        sem=("arbitrary",))(dh, wpw2, dwc, a, a, lng, lnb, wd)


def _mm_rms_bwd(name, dact, wt, h_in, dres, g, bf16_copy):
    seq, n = dact.shape
    tm = min(TOKEN_TILE, seq)

    def body(da_ref, w_ref, h_ref, dr_ref, g_ref, dx_ref, *rest):
        dg_ref = rest[-1]

        @pl.when(pl.program_id(0) == 0)
        def _():
            dg_ref[...] = jnp.zeros_like(dg_ref)

        dy = jnp.dot(da_ref[...], w_ref[...], preferred_element_type=F32)
        dx, dg = _rms_bwd(dy, h_ref[...], g_ref[...], dr_ref[...])
        dx_ref[...] = dx
        if bf16_copy:
            rest[0][...] = dx.astype(BF16)
        dg_ref[...] += dg

    copy_spec = [_rows(tm, D_MODEL)] if bf16_copy else []
    copy_shape = [jax.ShapeDtypeStruct((seq, D_MODEL), BF16)] if bf16_copy else []
    return _call(
        body, name=name, grid=(seq // tm,),
        in_specs=[_rows(tm, n), _resident((n, D_MODEL)), _rows(tm, D_MODEL), _rows(tm, D_MODEL), _full((1, D_MODEL))],
        out_specs=[_rows(tm, D_MODEL), *copy_spec, _full((1, D_MODEL))],
        out_shape=[jax.ShapeDtypeStruct((seq, D_MODEL), F32), *copy_shape, jax.ShapeDtypeStruct((1, D_MODEL), F32)],
        sem=("arbitrary",))(dact, wt, h_in, dres, g)


def _nt_bias(name, dy, w):
    seq, n = dy.shape
    kdim = w.shape[0]
    tm = min(2 * TOKEN_TILE, seq)

    def body(dy_ref, w_ref, o_ref, db_ref):
        @pl.when(pl.program_id(0) == 0)
        def _():
            db_ref[...] = jnp.zeros_like(db_ref)

        dyv = dy_ref[...]
        db_ref[...] += _sum_rows(dyv)
        o_ref[...] = _nt(dyv.astype(BF16), w_ref[...]).astype(BF16)

    return _call(
        body, name=name, grid=(seq // tm,),
        in_specs=[_rows(tm, n), _resident((kdim, n))],
        out_specs=[_rows(tm, kdim), _full((1, n))],
        out_shape=[jax.ShapeDtypeStruct((seq, kdim), BF16), jax.ShapeDtypeStruct((1, n), F32)],
        sem=("arbitrary",))(dy, w)


def _attn_bwd(probs, p_sinks, q, k, v, do):
    seq = q.shape[0]
    cur, prev = _kv_specs()

    def body(p_ref, ps_ref, q_ref, kc_ref, kp_ref, vc_ref, vp_ref, do_ref,
             dq_ref, dkc_ref, dkp_ref, dvc_ref, dvp_ref, dsink_ref):
        n = pl.program_id(0)

        @pl.when(n == 0)
        def _():
            dsink_ref[...] = jnp.zeros_like(dsink_ref)

        low = _low_lanes()
        lane = lax.broadcasted_iota(jnp.int32, (BLOCK, LANES), 1)
        p_sinks_blk = ps_ref[...]
        dk_all = jnp.zeros((2 * BLOCK, KV_DIM), F32)
        dv_all = jnp.zeros((2 * BLOCK, KV_DIM), F32)
        for j in range(N_KV_HEADS):
            k_lo, k_hi = _kv_low_high(kp_ref, kc_ref, j, low)
            v_lo, v_hi = _kv_low_high(vp_ref, vc_ref, j, low)
            dk_acc = jnp.zeros((2 * BLOCK, KV_DIM), F32)
            dv_acc = jnp.zeros((2 * BLOCK, KV_DIM), F32)
            for i in range(PAIRS_PER_KV):
                ls, h = _pair_lanes(j, i)
                qp = q_ref[:, ls]
                dop = do_ref[:, ls]
                dsb, pb16 = [], []
                for t, v_sel in enumerate((v_lo, v_hi)):
                    pb = p_ref[h + t]
                    pf = pb.astype(F32)
                    p_sink = jnp.sum(jnp.where(lane == h + t, p_sinks_blk, 0.0), axis=-1, keepdims=True)
                    dp = _nt(dop, v_sel)
                    delta = jnp.sum(pf * dp, axis=-1, keepdims=True)
                    dsb.append((pf * (dp - delta)).astype(BF16))
                    pb16.append(pb)
                    dsink_ref[h + t:h + t + 1, :] += jnp.broadcast_to(_sum_rows(p_sink * delta), (1, LANES))
                dq = (jnp.dot(dsb[0], k_lo, preferred_element_type=F32)
                      + jnp.dot(dsb[1], k_hi, preferred_element_type=F32))
                dq_ref[:, ls] = dq * (HEAD_DIM ** -0.5)
                dk_acc = dk_acc + jnp.where(low, _tn(dsb[0], qp), _tn(dsb[1], qp))
                dv_acc = dv_acc + jnp.where(low, _tn(pb16[0], dop), _tn(pb16[1], dop))
            dk_all = dk_all + _fold_pair_halves(dk_acc, j, low)
            dv_all = dv_all + _fold_pair_halves(dv_acc, j, low)
        dkp_ref[...] = dk_all[:BLOCK]
        dkc_ref[...] = dk_all[BLOCK:]
        dvp_ref[...] = dv_all[:BLOCK]
        dvc_ref[...] = dv_all[BLOCK:]

    kv_out = _rows(BLOCK, KV_DIM)
    kv_shape = jax.ShapeDtypeStruct((seq, KV_DIM), F32)
    return _call(
        body, name="attn_bwd", grid=(seq // BLOCK,),
        in_specs=[pl.BlockSpec((N_Q_HEADS, BLOCK, 2 * BLOCK), lambda n: (0, n, 0)), _rows(BLOCK, LANES),
                  _rows(BLOCK, Q_DIM), cur, prev, cur, prev, _rows(BLOCK, Q_DIM)],
        out_specs=[_rows(BLOCK, Q_DIM), kv_out, kv_out, kv_out, kv_out, _full((N_Q_HEADS, LANES))],
        out_shape=[jax.ShapeDtypeStruct((seq, Q_DIM), F32), kv_shape, kv_shape, kv_shape, kv_shape,
                   jax.ShapeDtypeStruct((N_Q_HEADS, LANES), F32)],
        sem=("arbitrary",))(probs, p_sinks, q, k, k, v, v, do)


def _rope_bwd(dq, dkc, dkp, dvc, dvp, rc, rsa, rsb):
    seq = dq.shape[0]
    nb = seq // BLOCK
    tm = min(TOKEN_TILE, seq)
    nt = seq // tm
    per = tm // BLOCK
    nxt = pl.BlockSpec((BLOCK, KV_DIM), lambda i: (jnp.minimum((i + 1) * per, nb - 1), 0))

    def body(dq_ref, dkc_ref, dkp_ref, dkn_ref, dvc_ref, dvp_ref, dvn_ref, c_ref, sa_ref, sb_ref, o_ref, db_ref):
        i = pl.program_id(0)

        @pl.when(i == 0)
        def _():
            db_ref[...] = jnp.zeros_like(db_ref)

        c, sa, sb = c_ref[...], sa_ref[...], sb_ref[...]
        last = i == nt - 1

        def from_next_block(prev_ref, next_ref):
            tail = jnp.where(last, 0.0, next_ref[...])
            return tail if per == 1 else jnp.concatenate([prev_ref[BLOCK:, :], tail], axis=0)

        dk = dkc_ref[...] + from_next_block(dkp_ref, dkn_ref)
        dv = dvc_ref[...] + from_next_block(dvp_ref, dvn_ref)
        for l in range(Q_DIM // LANES):
            ls = slice(l * LANES, (l + 1) * LANES)
            blk = _rope_t(dq_ref[:, ls], c, sa, sb)
            o_ref[:, ls] = blk.astype(BF16)
            db_ref[:, ls] += _sum_rows(blk)
        dkr = _rope_t(dk, c, sa, sb)
        o_ref[:, Q_DIM:Q_DIM + KV_DIM] = dkr.astype(BF16)
        db_ref[:, Q_DIM:Q_DIM + KV_DIM] += _sum_rows(dkr)
        o_ref[:, Q_DIM + KV_DIM:] = dv.astype(BF16)
        db_ref[:, Q_DIM + KV_DIM:] += _sum_rows(dv)

    kv = _rows(tm, KV_DIM)
    tab = _rows(tm, LANES)
    return _call(
        body, name="rope_bwd", grid=(nt,),
        in_specs=[_rows(tm, Q_DIM), kv, kv, nxt, kv, kv, nxt, tab, tab, tab],
        out_specs=[_rows(tm, QKV_DIM), _full((1, QKV_DIM))],
        out_shape=[jax.ShapeDtypeStruct((seq, QKV_DIM), BF16), jax.ShapeDtypeStruct((1, QKV_DIM), F32)],
        sem=("arbitrary",))(dq, dkc, dkp, dkp, dvc, dvp, dvp, rc, rsa, rsb)


def _adamw(w, g, m, v):
    m = ADAM_B1 * m + (1.0 - ADAM_B1) * g
    v = ADAM_B2 * v + (1.0 - ADAM_B2) * (g * g)
    m_hat = m / (1.0 - ADAM_B1 ** ADAM_STEP)
    v_hat = v / (1.0 - ADAM_B2 ** ADAM_STEP)
    delta = -ADAM_LR * (m_hat / (jnp.sqrt(v_hat) + ADAM_EPS) + ADAM_WD * w)
    return delta, m, v


def _sum_slots(name, parts):
    _, rows, cols = parts.shape
    tr = rows if rows <= 512 else ROWS_FF

    def body(p_ref, g_ref):
        g = p_ref[0].astype(F32)
        for d in range(1, N_DEV):
            g = g + p_ref[d].astype(F32)
        g_ref[...] = g

    return _call(
        body, name=name, grid=(rows // tr,),
        in_specs=[pl.BlockSpec((N_DEV, tr, cols), lambda i: (0, i, 0))],
        out_specs=_rows(tr, cols), out_shape=jax.ShapeDtypeStruct((rows, cols), F32),
        sem=("parallel",))(parts)


def _adamw_native(name, g, w, m, v):
    layers, rows, cols = w.shape
    tr = rows if rows <= 512 else 256

    def body(g_ref, w_ref, m_ref, v_ref, d_ref, nm_ref, nv_ref):
        d_ref[...], nm_ref[...], nv_ref[...] = _adamw(w_ref[...], g_ref[...], m_ref[...], v_ref[...])

    spec = pl.BlockSpec((1, tr, cols), lambda l, i: (l, i, 0))
    shape = jax.ShapeDtypeStruct(w.shape, F32)
    return _call(
        body, name=name, grid=(layers, rows // tr), in_specs=[spec, spec, spec, spec],
        out_specs=[spec, spec, spec], out_shape=[shape, shape, shape],
        sem=("parallel", "parallel"))(g, w, m, v)


def _adamw_from_slots(name, slots, row_block, w, m, v):
    layers, rows, cols = w.shape

    def body(*refs):
        slot_refs = refs[:layers]
        w_ref, m_ref, v_ref, g_ref, d_ref, nm_ref, nv_ref = refs[layers:]
        for l in range(layers):
            @pl.when(pl.program_id(0) == l)
            def _(p_ref=slot_refs[l]):
                g = p_ref[0].astype(F32)
                for d in range(1, N_DEV):
                    g = g + p_ref[d].astype(F32)
                g_ref[0] = g
        d, nm, nv = _adamw(w_ref[0], g_ref[0], m_ref[0], v_ref[0])
        d_ref[0], nm_ref[0], nv_ref[0] = d, nm, nv

    slot_spec = pl.BlockSpec((N_DEV, rows, cols), lambda l: (0, row_block, 0))
    spec = pl.BlockSpec((1, rows, cols), lambda l: (l, 0, 0))
    shape = jax.ShapeDtypeStruct(w.shape, F32)
    return _call(
        body, name=name, grid=(layers,), in_specs=[slot_spec] * layers + [spec, spec, spec],
        out_specs=[spec, spec, spec, spec], out_shape=[shape, shape, shape, shape],
        sem=("arbitrary",))(*slots, w, m, v)


def _adamw_replicated(parts, w, m, v):
    def body(p_ref, w_ref, m_ref, v_ref, g_ref, d_ref, nm_ref, nv_ref):
        g = p_ref[0]
        for j in range(1, N_DEV):
            g = g + p_ref[j]
        g_ref[...] = g
        d_ref[...], nm_ref[...], nv_ref[...] = _adamw(w_ref[...], g, m_ref[...], v_ref[...])

    spec = _full((REPL_ROWS, D_MODEL))
    shape = jax.ShapeDtypeStruct((REPL_ROWS, D_MODEL), F32)
    return _call(
        body, name="adamw_replicated", grid=(1,),
        in_specs=[_full((N_DEV, REPL_ROWS, D_MODEL)), spec, spec, spec],
        out_specs=[spec, spec, spec, spec], out_shape=[shape, shape, shape, shape],
        sem=("arbitrary",))(parts, w, m, v)


_MESH = pl.DeviceIdType.MESH
_ANY = pl.BlockSpec(memory_space=pl.ANY)


def _all_gather(name, xs):
    rows, cols = xs.shape

    def body(x_ref, out_ref, send_sems, recv_sems, local_sem):
        x, y, c = lax.axis_index("x"), lax.axis_index("y"), lax.axis_index("c")
        me, sibling = (x, y, c), (x, y, 1 - c)
        chips = [(1 - x, y), (x, 1 - y), (1 - x, 1 - y)]

        def slot(px, py, pc):
            return out_ref.at[4 * px + 2 * py + pc]

        def copy(k, block, to, src=None):
            return pltpu.make_async_remote_copy(
                src_ref=slot(*block) if src is None else src, dst_ref=slot(*block),
                send_sem=send_sems.at[k], recv_sem=recv_sems.at[k], device_id=to, device_id_type=_MESH)

        mine = pltpu.make_async_copy(x_ref, slot(*me), local_sem)
        mine.start()
        first = [copy(0, me, sibling, src=x_ref)]
        first += [copy(1 + j, me, (*chip, c), src=x_ref) for j, chip in enumerate(chips)]
        for cp in first:
            cp.start()
        passed = [copy(4 + j, (*chip, c), sibling) for j, chip in enumerate(chips)]
        for j, chip in enumerate(chips):
            copy(1 + j, (*chip, c), me).wait_recv()
            passed[j].start()
        copy(0, sibling, me).wait_recv()
        for j, chip in enumerate(chips):
            copy(4 + j, (*chip, 1 - c), me).wait_recv()
        for cp in first + passed:
            cp.wait_send()
        mine.wait()

    return pl.pallas_call(
        body, name=name, out_shape=jax.ShapeDtypeStruct((N_DEV, rows, cols), xs.dtype),
        in_specs=[_ANY], out_specs=_ANY,
        scratch_shapes=[pltpu.SemaphoreType.DMA((7,)), pltpu.SemaphoreType.DMA((7,)), pltpu.SemaphoreType.DMA],
    )(xs)


N_PEERS = N_DEV - 1
_HBM = pl.BlockSpec(memory_space=pltpu.HBM)
_SEM = pl.BlockSpec(memory_space=pltpu.SEMAPHORE)
_DATAFLOW = pltpu.SideEffectType.DATAFLOW_SIDE_EFFECTING
_TOKEN = jax.ShapeDtypeStruct((SUBLANES, LANES), F32)


def _peers():
    x, y, c = lax.axis_index("x"), lax.axis_index("y"), lax.axis_index("c")
    out = []
    for k in range(1, N_DEV):
        px = 1 - x if k & 4 else x
        py = 1 - y if k & 2 else y
        pc = 1 - c if k & 1 else c
        out.append(((px, py, pc), 4 * px + 2 * py + pc))
    return 4 * x + 2 * y + c, out


def _in_hbm(a):
    return pltpu.with_memory_space_constraint(a, pltpu.HBM)


def _landing(rows):
    return _in_hbm(lax.empty((N_DEV, rows, D_MODEL), BF16))


def _sem_pair():
    return pltpu.SemaphoreType.DMA((N_PEERS,)), pltpu.SemaphoreType.DMA((N_PEERS,))


def _gather_start(name, payloads):
    n = len(payloads)

    def body(*refs):
        src, land = refs[:n], refs[n:2 * n]
        sems = refs[2 * n:4 * n]
        token = refs[-1]
        me, peers = _peers()
        for g in range(n):
            for k, (pos, _) in enumerate(peers):
                pltpu.make_async_remote_copy(
                    src_ref=src[g], dst_ref=land[g].at[me], send_sem=sems[2 * g].at[k],
                    recv_sem=sems[2 * g + 1].at[k], device_id=pos, device_id_type=_MESH).start()
        token[...] = jnp.zeros_like(token)

    lands = [_landing(p.shape[0]) for p in payloads]
    sem_shapes = [s for _ in payloads for s in _sem_pair()]
    hbm_shapes = [pltpu.HBM(a.shape, a.dtype) for a in list(payloads) + lands]
    out = pl.pallas_call(
        body, name=name, out_shape=(*sem_shapes, *hbm_shapes, _TOKEN),
        in_specs=[_HBM] * (2 * n), out_specs=(*[_SEM] * (2 * n), *[_HBM] * (2 * n), pl.BlockSpec(memory_space=pltpu.VMEM)),
        input_output_aliases={i: 2 * n + i for i in range(2 * n)},
        compiler_params=pltpu.CompilerParams(has_side_effects=_DATAFLOW),
    )(*[_in_hbm(p) for p in payloads], *lands)
    sems, thru = out[:2 * n], out[2 * n:4 * n]
    return [(thru[g], thru[n + g], sems[2 * g], sems[2 * g + 1]) for g in range(n)], out[-1]


def _gather_wait(name, group, after):
    payload, land, send_sems, recv_sems = group

    def body(src_ref, land_ref, send_ref, recv_ref, after_ref, src_out, land_out):
        _, peers = _peers()
        for k, (pos, idx) in enumerate(peers):
            cp = pltpu.make_async_remote_copy(
                src_ref=src_ref, dst_ref=land_ref.at[idx], send_sem=send_ref.at[k], recv_sem=recv_ref.at[k],
                device_id=pos, device_id_type=_MESH)
            cp.wait_send()
            cp.wait_recv()

    _, land = pl.pallas_call(
        body, name=name, out_shape=(pltpu.HBM(payload.shape, payload.dtype), pltpu.HBM(land.shape, land.dtype)),
        in_specs=[_HBM, _HBM, _SEM, _SEM, _ANY], out_specs=(_HBM, _HBM), input_output_aliases={0: 0, 1: 1},
        compiler_params=pltpu.CompilerParams(has_side_effects=_DATAFLOW),
    )(payload, land, send_sems, recv_sems, after)
    me = 4 * lax.axis_index("x") + 2 * lax.axis_index("y") + lax.axis_index("c")
    return lax.dynamic_update_slice(land, payload[None], (me, 0, 0))


def _scatter_start(name, blocks):
    rows = blocks.shape[1]

    def body(blocks_ref, land_ref, send_sems, recv_sems, blocks_out, land_out, token):
        me, peers = _peers()
        for k, (pos, idx) in enumerate(peers):
            pltpu.make_async_remote_copy(
                src_ref=blocks_ref.at[idx], dst_ref=land_ref.at[me], send_sem=send_sems.at[k],
                recv_sem=recv_sems.at[k], device_id=pos, device_id_type=_MESH).start()
        token[...] = jnp.zeros_like(token)

    land = _landing(rows)
    send_sems, recv_sems, blocks_thru, land_thru, token = pl.pallas_call(
        body, name=name,
        out_shape=(*_sem_pair(), pltpu.HBM(blocks.shape, blocks.dtype), pltpu.HBM(land.shape, land.dtype), _TOKEN),
        in_specs=[_HBM, _HBM], out_specs=(_SEM, _SEM, _HBM, _HBM, pl.BlockSpec(memory_space=pltpu.VMEM)),
        input_output_aliases={0: 2, 1: 3},
        compiler_params=pltpu.CompilerParams(has_side_effects=_DATAFLOW),
    )(_in_hbm(blocks), land)
    return (blocks_thru, land_thru, send_sems, recv_sems), token


def _scatter_wait(groups, after):
    n = len(groups)

    def body(*refs):
        blocks, land = refs[:n], refs[n:2 * n]
        sems = refs[2 * n:4 * n]
        _, peers = _peers()
        for g in range(n):
            for k, (pos, idx) in enumerate(peers):
                cp = pltpu.make_async_remote_copy(
                    src_ref=blocks[g].at[idx], dst_ref=land[g].at[idx], send_sem=sems[2 * g].at[k],
                    recv_sem=sems[2 * g + 1].at[k], device_id=pos, device_id_type=_MESH)
                cp.wait_send()
                cp.wait_recv()

    hbm = [grp[0] for grp in groups] + [grp[1] for grp in groups]
    sems = [s for grp in groups for s in grp[2:]]
    out = pl.pallas_call(
        body, name="rs_wait", out_shape=tuple(pltpu.HBM(a.shape, a.dtype) for a in hbm),
        in_specs=[_HBM] * (2 * n) + [_SEM] * (2 * n) + [_ANY], out_specs=tuple([_HBM] * (2 * n)),
        input_output_aliases={i: i for i in range(2 * n)},
        compiler_params=pltpu.CompilerParams(has_side_effects=_DATAFLOW),
    )(*hbm, *sems, after)
    me = 4 * lax.axis_index("x") + 2 * lax.axis_index("y") + lax.axis_index("c")
    lands = []
    for g in range(n):
        own = lax.dynamic_index_in_dim(out[g], me, axis=0, keepdims=True)
        lands.append(lax.dynamic_update_slice(out[n + g], own, (me, 0, 0)))
    return lands


def _pad_rows(flat, rows):
    return jnp.pad(flat, (0, rows * D_MODEL - flat.shape[0])).reshape(rows, D_MODEL)


SMALL_NAMES = ("conv_b_pw1", "conv_w_dw", "conv_b_dw", "conv_ln_g", "conv_ln_b", "conv_b_pw2")


def _pack_small(p):
    flat = jnp.concatenate([p[n].reshape(-1) for n in SMALL_NAMES])
    return _pad_rows(flat, ROWS_SMALL).reshape(1, ROWS_SMALL, D_MODEL)


def _unpack_small(packed):
    flat = packed.reshape(-1)
    c = D_MODEL // N_DEV
    shapes = ((1, 2 * c), (1, CONV_WIDTH, c), (1, c), (1, c), (1, c), (1, c))
    out, o = {}, 0
    for n, shape in zip(SMALL_NAMES, shapes):
        size = shape[-1] * (shape[1] if len(shape) == 3 else 1)
        out[n] = flat[o:o + size].reshape(shape)
        o += size
    return out


def _gather_payloads(p):
    t = lambda a: jnp.swapaxes(a, -1, -2).astype(BF16)
    w1t, w3t, w2 = t(p["ffn_w1"]), t(p["ffn_w3"]), p["ffn_w2"].astype(BF16)
    bits = lax.bitcast_convert_type(_pack_small(p).reshape(-1)[:ROWS_SMALL * D_MODEL // 2], jnp.uint32)
    halves = [(bits >> 16).astype(jnp.uint16), (bits & 0xFFFF).astype(jnp.uint16)]
    small = lax.bitcast_convert_type(jnp.concatenate(halves), BF16).reshape(ROWS_SMALL, D_MODEL)
    conv = jnp.concatenate([t(p["conv_w_pw1"][0]), p["conv_w_pw2"][0].astype(BF16), small], axis=0)
    ffn = [jnp.concatenate([w1t[l], w3t[l], w2[l]], axis=0) for l in range(2)]
    return [t(p["attn_w_qkv"][0]), p["attn_w_o"][0].astype(BF16), ffn[0], conv, ffn[1]]


def _device_rows(land, lo, n):
    return land[:, lo:lo + n].reshape(N_DEV * n, D_MODEL)


def _unpack_conv(land):
    words = lax.bitcast_convert_type(land[:, ROWS_PW1 + ROWS_PW2:], jnp.uint16).astype(jnp.uint32)
    words = words.reshape(N_DEV, 2, ROWS_SMALL * D_MODEL // 2)
    small = lax.bitcast_convert_type((words[:, 0] << 16) | words[:, 1], F32)
    c = D_MODEL // N_DEV
    b_pw1 = small[:, :2 * c].reshape(1, 2 * D_MODEL)
    s = 2 * c
    w_dw = small[:, s:s + CONV_WIDTH * c].reshape(N_DEV, CONV_WIDTH, c).transpose(1, 0, 2).reshape(CONV_WIDTH, D_MODEL)
    s += CONV_WIDTH * c
    b_dw, ln_g, ln_b, b_pw2 = (small[:, s + i * c:s + (i + 1) * c].reshape(1, D_MODEL) for i in range(4))
    return dict(w_pw1_t=_device_rows(land, 0, ROWS_PW1), w_pw2=_device_rows(land, ROWS_PW1, ROWS_PW2),
                b_pw1=b_pw1, w_dw=w_dw, b_dw=b_dw, ln_g=ln_g, ln_b=ln_b, b_pw2=b_pw2)


def _dest_blocks(mats):
    return jnp.concatenate([a.reshape(N_DEV, -1, D_MODEL) for a in mats], axis=1)


def _small_grad_rows(g_bpw1, g_dw, g_bdw, g_lng, g_lnb, g_bpw2):
    c = D_MODEL // N_DEV
    small = jnp.concatenate(
        [g_bpw1.reshape(N_DEV, 2 * c), g_dw.reshape(CONV_WIDTH, N_DEV, c).transpose(1, 0, 2).reshape(N_DEV, -1),
         g_bdw.reshape(N_DEV, c), g_lng.reshape(N_DEV, c), g_lnb.reshape(N_DEV, c), g_bpw2.reshape(N_DEV, c)], axis=1)
    small = jnp.pad(small, ((0, 0), (0, ROWS_SMALL * D_MODEL - SMALL_USED)))
    return small.reshape(N_DEV * ROWS_SMALL, D_MODEL).astype(BF16)


LOSS_ROW = 9


def _pack_replicated(norm_mix, norm_ffn, b_qkv, sinks, b_o, norm_final, extra=None):
    rows = [norm_mix.reshape(2, D_MODEL), norm_ffn.reshape(2, D_MODEL), _pad_rows(b_qkv.reshape(-1), 2),
            _pad_rows(sinks.reshape(-1), 1), b_o.reshape(1, D_MODEL), norm_final.reshape(1, D_MODEL)]
    if extra is not None:
        rows.append(_pad_rows(extra.reshape(-1), 1))
    p = jnp.concatenate(rows, axis=0)
    return jnp.pad(p, ((0, REPL_ROWS - p.shape[0]), (0, 0)))


def _unpack_replicated(p):
    return dict(norm_mix=p[0:2], norm_ffn=p[2:4], attn_b_qkv=p[4:6].reshape(-1)[:QKV_DIM].reshape(1, QKV_DIM),
                attn_sinks=p[6, :N_Q_HEADS].reshape(1, N_Q_HEADS), attn_b_o=p[7:8], norm_final=p[8])


WEIGHT_ORDER = ['norm_mix', 'norm_ffn', 'attn_w_qkv', 'attn_b_qkv', 'attn_sinks', 'attn_w_o', 'attn_b_o',
                'conv_w_pw1', 'conv_b_pw1', 'conv_w_dw', 'conv_b_dw', 'conv_ln_g', 'conv_ln_b', 'conv_w_pw2',
                'conv_b_pw2', 'ffn_w1', 'ffn_w3', 'ffn_w2', 'norm_final']


def kernel(x, norm_mix, norm_ffn, attn_w_qkv, attn_b_qkv, attn_sinks, attn_w_o, attn_b_o, conv_w_pw1, conv_b_pw1, conv_w_dw, conv_b_dw, conv_ln_g, conv_ln_b, conv_w_pw2, conv_b_pw2, ffn_w1, ffn_w3, ffn_w2, norm_final, loss_target, m_norm_mix, m_norm_ffn, m_attn_w_qkv, m_attn_b_qkv, m_attn_sinks, m_attn_w_o, m_attn_b_o, m_conv_w_pw1, m_conv_b_pw1, m_conv_w_dw, m_conv_b_dw, m_conv_ln_g, m_conv_ln_b, m_conv_w_pw2, m_conv_b_pw2, m_ffn_w1, m_ffn_w3, m_ffn_w2, m_norm_final, v_norm_mix, v_norm_ffn, v_attn_w_qkv, v_attn_b_qkv, v_attn_sinks, v_attn_w_o, v_attn_b_o, v_conv_w_pw1, v_conv_b_pw1, v_conv_w_dw, v_conv_b_dw, v_conv_ln_g, v_conv_ln_b, v_conv_w_pw2, v_conv_b_pw2, v_ffn_w1, v_ffn_w3, v_ffn_w2, v_norm_final):
    xs = x[0]
    target = loss_target[0]
    seq = xs.shape[0]

    w = dict(attn_w_qkv=attn_w_qkv, attn_w_o=attn_w_o, conv_w_pw1=conv_w_pw1, conv_b_pw1=conv_b_pw1,
             conv_w_dw=conv_w_dw, conv_b_dw=conv_b_dw, conv_ln_g=conv_ln_g, conv_ln_b=conv_ln_b,
             conv_w_pw2=conv_w_pw2, conv_b_pw2=conv_b_pw2, ffn_w1=ffn_w1, ffn_w3=ffn_w3, ffn_w2=ffn_w2)
    m = dict(attn_w_qkv=m_attn_w_qkv, attn_w_o=m_attn_w_o, conv_w_pw1=m_conv_w_pw1, conv_b_pw1=m_conv_b_pw1,
             conv_w_dw=m_conv_w_dw, conv_b_dw=m_conv_b_dw, conv_ln_g=m_conv_ln_g, conv_ln_b=m_conv_ln_b,
             conv_w_pw2=m_conv_w_pw2, conv_b_pw2=m_conv_b_pw2, ffn_w1=m_ffn_w1, ffn_w3=m_ffn_w3, ffn_w2=m_ffn_w2)
    v = dict(attn_w_qkv=v_attn_w_qkv, attn_w_o=v_attn_w_o, conv_w_pw1=v_conv_w_pw1, conv_b_pw1=v_conv_b_pw1,
             conv_w_dw=v_conv_w_dw, conv_b_dw=v_conv_b_dw, conv_ln_g=v_conv_ln_g, conv_ln_b=v_conv_ln_b,
             conv_w_pw2=v_conv_w_pw2, conv_b_pw2=v_conv_b_pw2, ffn_w1=v_ffn_w1, ffn_w3=v_ffn_w3, ffn_w2=v_ffn_w2)

    payloads = _gather_payloads(w)
    (ag_qkv, ag_wo), tok = _gather_start("ag_start_attn", payloads[:2])
    (ag_ffn0, ag_conv, ag_ffn1), tok = _gather_start(
        "ag_start_rest", [payloads[2] + tok[0, 0].astype(BF16), payloads[3], payloads[4]])
    rc, rsa, rsb = _rope_tables(seq)
    sinks = attn_sinks.reshape(N_Q_HEADS)
    g_mix0, g_mix1 = norm_mix[0:1] + tok[0, 0], norm_mix[1:2]
    g_ffn0, g_ffn1 = norm_ffn[0:1], norm_ffn[1:2]
    g_fin = norm_final.reshape(1, D_MODEL)

    w_qkv_t = _gather_wait("ag_wait_qkv", ag_qkv, tok).reshape(QKV_DIM, D_MODEL)
    y0, q, k, vv = _qkv_fwd(xs, g_mix0, w_qkv_t, attn_b_qkv, rc, rsa, rsb)
    attn, probs, p_sinks = _attn_fwd(sinks, q, k, vv)
    w_o = _gather_wait("ag_wait_wo", ag_wo, attn).reshape(Q_DIM, D_MODEL)
    h1 = _mm_res("attn_out_proj", attn, w_o, attn_b_o, xs)
    w_ffn0 = _gather_wait("ag_wait_ffn0", ag_ffn0, h1)
    f0, u0, p0, s0 = _ffn_up("ffn0_up", h1, g_ffn0, w_ffn0)
    h2 = _ffn_down("ffn0_down", s0, w_ffn0, h1)
    wt = _unpack_conv(_gather_wait("ag_wait_conv", ag_conv, h2))
    wd = jnp.concatenate([wt["w_dw"][::-1], jnp.zeros((TAPS_PAD - CONV_WIDTH, D_MODEL), F32)], axis=0)
    y1, a, dwc, z = _conv_fwd(h2, g_mix1, wt["w_pw1_t"], wt["b_pw1"], wd, wt["b_dw"], wt["ln_g"], wt["ln_b"])
    h3 = _mm_res("conv_out_proj", z, wt["w_pw2"], wt["b_pw2"], h2)
    w_ffn1 = _gather_wait("ag_wait_ffn1", ag_ffn1, h3)
    f1, u1, p1, s1 = _ffn_up("ffn1_up", h3, g_ffn1, w_ffn1)
    dh4, dh4b, sq, dg_fin = _ffn_down_loss(s1, w_ffn1, h3, target, g_fin)

    du1, dp1 = _ffn_bwd_act("ffn1_bwd_act", dh4b, u1, p1, w_ffn1)
    dh3, dh3b, dg_ffn1 = _ffn_bwd_in("ffn1_bwd_in", du1, dp1, w_ffn1, h3, dh4, g_ffn1)
    blocks = _mm_tn("ffn1_dw2", s1, dh4b, tk=FF_TILE, into=(None, W2_SLOT))
    blocks = _mm_tn("ffn1_dw1", du1, f1, tk=FF_TILE, into=(blocks, W1T_SLOT))
    blocks = _mm_tn("ffn1_dw3", dp1, f1, tk=FF_TILE, into=(blocks, W3T_SLOT))
    rs_ffn1, tok = _scatter_start("rs_start_ffn1", blocks)

    da, dlg, dlb, dbdw, dwd, dbpw1, dbpw2 = _conv_bwd(dh3, wt["w_pw2"], dwc, a, wt["ln_g"] + tok[0, 0],
                                                     wt["ln_b"], wd)
    gpw2 = _mm_tn("conv_dw_pw2", z, dh3b, tk=D_MODEL)
    dh2, dh2b, dg_mix1 = _mm_rms_bwd("conv_in_bwd", da, wt["w_pw1_t"], h2, dh3, g_mix1, True)
    gpw1t = _mm_tn("conv_dw_pw1", da, y1, tk=D_MODEL)
    small_rows = _small_grad_rows(dbpw1, dwd[:CONV_WIDTH][::-1], dbdw, dlg, dlb, dbpw2)
    rs_conv, tok = _scatter_start("rs_start_conv", _dest_blocks([gpw1t, gpw2, small_rows]))

    du0, dp0 = _ffn_bwd_act("ffn0_bwd_act", dh2b, u0, p0, w_ffn0)
    dh1, dh1b, dg_ffn0 = _ffn_bwd_in("ffn0_bwd_in", du0, dp0, w_ffn0, h1, dh2, g_ffn0 + tok[0, 0])
    blocks = _mm_tn("ffn0_dw2", s0, dh2b, tk=FF_TILE, into=(None, W2_SLOT))
    blocks = _mm_tn("ffn0_dw1", du0, f0, tk=FF_TILE, into=(blocks, W1T_SLOT))
    blocks = _mm_tn("ffn0_dw3", dp0, f0, tk=FF_TILE, into=(blocks, W3T_SLOT))
    rs_ffn0, tok = _scatter_start("rs_start_ffn0", blocks)

    gwo = _mm_tn("attn_dw_o", attn, dh1b, tk=D_MODEL)
    rs_wo, tok2 = _scatter_start("rs_start_wo", _dest_blocks([gwo]))
    dattn, dbo = _nt_bias("attn_out_bwd", dh1, w_o)
    dq, dkc, dkp, dvc, dvp, dsink = _attn_bwd(probs, p_sinks + (tok[0, 0] + tok2[0, 0]), q, k, vv, dattn)
    dqkv, dbqkv = _rope_bwd(dq, dkc, dkp, dvc, dvp, rc, rsa, rsb)
    gqkvt = _mm_tn("attn_dw_qkv", dqkv, y0, tk=QKV_DIM)
    rs_qkv, tok = _scatter_start("rs_start_qkv", _dest_blocks([gqkvt]))
    dx, dg_mix0 = _mm_rms_bwd("qkv_in_bwd", dqkv, w_qkv_t, xs, dh1, g_mix0 + tok[0, 0], False)

    p_ffn1, p_conv, p_ffn0, p_wo, p_qkv = _scatter_wait([rs_ffn1, rs_conv, rs_ffn0, rs_wo, rs_qkv], dx)

    tr = lambda a: jnp.swapaxes(a, -1, -2)
    same = lambda a: a
    ffn_slots = [p_ffn0, p_ffn1]
    plan = dict(
        ffn_w1=(ffn_slots, W1T_SLOT, tr), ffn_w3=(ffn_slots, W3T_SLOT, tr), ffn_w2=(ffn_slots, W2_SLOT, same),
        attn_w_qkv=([p_qkv], 0, tr), attn_w_o=([p_wo], 0, same),
        conv_w_pw2=([p_conv], ROWS_PW1 // ROWS_PW2, same))
    sharded = [{}, {}, {}, {}]
    for n, (slots, row_block, view) in plan.items():
        outs_n = _adamw_from_slots("adamw_" + n, slots, row_block, view(w[n]), view(m[n]), view(v[n]))
        for dst, t in zip(sharded, outs_n):
            dst[n] = view(t)
    small_out = _adamw_from_slots("adamw_small", [p_conv], (ROWS_PW1 + ROWS_PW2) // ROWS_SMALL,
                                  _pack_small(w), _pack_small(m), _pack_small(v))
    for dst, t in zip(sharded, small_out):
        dst.update(_unpack_small(t))
    g_pw1 = _sum_slots("rs_sum_pw1", p_conv[:, :ROWS_PW1]).T[None]
    pw1_out = _adamw_native("adamw_conv_w_pw1", g_pw1, w["conv_w_pw1"], m["conv_w_pw1"], v["conv_w_pw1"])
    for dst, t in zip(sharded, (g_pw1,) + tuple(pw1_out)):
        dst["conv_w_pw1"] = t

    part = _pack_replicated(jnp.concatenate([dg_mix0, dg_mix1]), jnp.concatenate([dg_ffn0, dg_ffn1]),
                            dbqkv, -dsink[:, 0], dbo, dg_fin, extra=sq[0, 0:1])
    parts = _all_gather("ag_replicated_grads", part)
    w_rep = _pack_replicated(norm_mix, norm_ffn, attn_b_qkv, attn_sinks, attn_b_o, norm_final)
    m_rep = _pack_replicated(m_norm_mix, m_norm_ffn, m_attn_b_qkv, m_attn_sinks, m_attn_b_o, m_norm_final)
    v_rep = _pack_replicated(v_norm_mix, v_norm_ffn, v_attn_b_qkv, v_attn_sinks, v_attn_b_o, v_norm_final)
    rep_out = _adamw_replicated(parts, w_rep, m_rep, v_rep)
    replicated = [_unpack_replicated(t) for t in rep_out]
    loss = rep_out[0][LOSS_ROW, 0] * (0.5 / D_MODEL)

    outs = [loss, dx.reshape(1, seq, D_MODEL)]
    for sh, rp in zip(sharded, replicated):
        merged = {**sh, **rp}
        outs += [merged[n] for n in WEIGHT_ORDER]
    return tuple(outs)
```

```python
import jax
import jax.numpy as jnp
from jax import lax
from jax.experimental import pallas as pl
from jax.experimental.pallas import tpu as pltpu

F32 = jnp.float32
BF16 = jnp.bfloat16

D_MODEL = 1024
HEAD_DIM = 64
N_Q_HEADS = 16
N_KV_HEADS = 2
Q_PER_KV = 8
Q_DIM = N_Q_HEADS * HEAD_DIM
KV_DIM = N_KV_HEADS * HEAD_DIM
QKV_DIM = Q_DIM + 2 * KV_DIM
BLOCK = 128
CONV_WIDTH = 31
D_FF = 2816
ROPE_THETA = 10000.0
RMS_EPS = 1e-5
LN_EPS = 1e-5
ADAM_LR = 0.001
ADAM_B1 = 0.9
ADAM_B2 = 0.999
ADAM_EPS = 1e-08
ADAM_WD = 0.01
ADAM_STEP = 10
N_DEV = 8

LANES = 128
SUBLANES = 8
TOKEN_TILE = 512
CONV_CHUNK = 64
CONV_HALO = 32
TAPS_PAD = 32
VMEM_LIMIT = 56 * 1024 * 1024
NEG_INF = float(jnp.finfo(jnp.float32).min)

ROWS_FF = D_FF // N_DEV
W1T_SLOT, W3T_SLOT, W2_SLOT = 0, 1, 2
ROWS_QKV = QKV_DIM // N_DEV
ROWS_WO = Q_DIM // N_DEV
ROWS_PW1 = 2 * D_MODEL // N_DEV
ROWS_PW2 = D_MODEL // N_DEV
ROWS_SMALL = 16
SMALL_USED = 2 * D_MODEL // N_DEV + CONV_WIDTH * (D_MODEL // N_DEV) + 4 * (D_MODEL // N_DEV)
FF_SPLIT = 2
FF_TILE_DEVS = N_DEV // FF_SPLIT
FF_TILE = FF_TILE_DEVS * ROWS_FF
REPL_ROWS = 16


def _call(body, *, name, grid, in_specs, out_specs, out_shape, scratch=(), sem=None, aliases=None):
    return pl.pallas_call(
        body, name=name, grid=grid, in_specs=in_specs, out_specs=out_specs, out_shape=out_shape,
        scratch_shapes=list(scratch), input_output_aliases=aliases or {},
        compiler_params=pltpu.CompilerParams(dimension_semantics=sem, vmem_limit_bytes=VMEM_LIMIT))


def _full(shape):
    return pl.BlockSpec(shape, lambda *_: (0,) * len(shape))


def _resident(shape):
    return pl.BlockSpec(shape, lambda *_: (0,) * len(shape), pipeline_mode=pl.Buffered(1))


def _rows(tm, n):
    return pl.BlockSpec((tm, n), lambda i, *_: (i, 0))


def _sig(x):
    return 1.0 / (1.0 + jnp.exp(-x))


def _sum_rows(x):
    return jnp.sum(x, axis=0, keepdims=True)


def _nt(a, b):
    return lax.dot_general(a, b, (((1,), (1,)), ((), ())), preferred_element_type=F32)


def _tn(a, b):
    return lax.dot_general(a, b, (((0,), (0,)), ((), ())), preferred_element_type=F32)


def _rms_stats(x):
    return lax.rsqrt(jnp.mean(x * x, axis=-1, keepdims=True) + RMS_EPS)


def _rms_bwd(dy, x, g, dres):
    r = _rms_stats(x)
    n = x * r
    dn = dy * g
    dx = dres + r * (dn - n * jnp.mean(dn * n, axis=-1, keepdims=True))
    return dx, _sum_rows(dy * n)


def _rope_tables(seq):
    half = HEAD_DIM // 2
    pos = jnp.arange(seq, dtype=F32)
    inv_freq = ROPE_THETA ** (-jnp.arange(0, HEAD_DIM, 2, dtype=F32) / HEAD_DIM)
    ang = pos[:, None] * inv_freq[None, :]
    spread = lambda t: jnp.broadcast_to(t[:, None, :], (seq, LANES // half, half)).reshape(seq, LANES)
    cos, sin = spread(jnp.cos(ang)), spread(jnp.sin(ang))
    first_half = (jnp.arange(LANES) % HEAD_DIM < half)[None, :]
    sa = jnp.where(first_half, -sin, 0.0)
    sb = jnp.where(first_half, 0.0, sin)
    return cos, sa, sb


def _rope(t, c, sa, sb):
    half = HEAD_DIM // 2
    return t * c + pltpu.roll(t, LANES - half, 1) * sa + pltpu.roll(t, half, 1) * sb


def _rope_t(dt, c, sa, sb):
    half = HEAD_DIM // 2
    return dt * c + pltpu.roll(dt * sa, half, 1) + pltpu.roll(dt * sb, LANES - half, 1)


def _qkv_fwd(x, g, w, b, rc, rsa, rsb):
    seq = x.shape[0]
    tm = min(TOKEN_TILE, seq)

    def body(x_ref, g_ref, w_ref, b_ref, c_ref, sa_ref, sb_ref, y_ref, q_ref, k_ref, v_ref):
        xv = x_ref[...]
        y = (xv * _rms_stats(xv) * g_ref[...]).astype(BF16)
        y_ref[...] = y
        qkv = _nt(y, w_ref[...]) + b_ref[...]
        c, sa, sb = c_ref[...], sa_ref[...], sb_ref[...]
        for i in range(Q_DIM // LANES):
            blk = _rope(qkv[:, i * LANES:(i + 1) * LANES], c, sa, sb)
            q_ref[:, i * LANES:(i + 1) * LANES] = (blk * (HEAD_DIM ** -0.5)).astype(BF16)
        k_ref[...] = _rope(qkv[:, Q_DIM:Q_DIM + KV_DIM], c, sa, sb).astype(BF16)
        v_ref[...] = qkv[:, Q_DIM + KV_DIM:].astype(BF16)

    return _call(
        body, name="qkv_fwd", grid=(seq // tm,),
        in_specs=[_rows(tm, D_MODEL), _full((1, D_MODEL)), _full((QKV_DIM, D_MODEL)), _full((1, QKV_DIM)),
                  _rows(tm, LANES), _rows(tm, LANES), _rows(tm, LANES)],
        out_specs=[_rows(tm, D_MODEL), _rows(tm, Q_DIM), _rows(tm, KV_DIM), _rows(tm, KV_DIM)],
        out_shape=[jax.ShapeDtypeStruct((seq, D_MODEL), BF16), jax.ShapeDtypeStruct((seq, Q_DIM), BF16),
                   jax.ShapeDtypeStruct((seq, KV_DIM), BF16), jax.ShapeDtypeStruct((seq, KV_DIM), BF16)],
        sem=("parallel",))(x, g, w, b, rc, rsa, rsb)


def _band_mask(n):
    row = lax.broadcasted_iota(jnp.int32, (BLOCK, 2 * BLOCK), 0)
    col = lax.broadcasted_iota(jnp.int32, (BLOCK, 2 * BLOCK), 1)
    rel = row + BLOCK - col
    return (rel >= 0) & (rel < BLOCK) & ((col >= BLOCK) | (n > 0))


def _softmax_with_sink(s, mask, sink):
    s = jnp.where(mask, s, NEG_INF)
    m = jnp.maximum(jnp.max(s, axis=-1, keepdims=True), sink)
    p = jnp.exp(s - m)
    e_sink = jnp.exp(sink - m)
    inv = 1.0 / (jnp.sum(p, axis=-1, keepdims=True) + e_sink)
    return p * inv, e_sink * inv


PAIRS_PER_KV = Q_PER_KV // 2


def _kv_specs():
    cur = pl.BlockSpec((BLOCK, KV_DIM), lambda n: (n, 0))
    prev = pl.BlockSpec((BLOCK, KV_DIM), lambda n: (jnp.maximum(n - 1, 0), 0))
    return cur, prev


def _low_lanes():
    return lax.broadcasted_iota(jnp.int32, (2 * BLOCK, KV_DIM), 1) < HEAD_DIM


def _kv_low_high(prev_ref, cur_ref, j, low):
    both = jnp.concatenate([prev_ref[...], cur_ref[...]], axis=0).astype(F32)
    swapped = pltpu.roll(both, HEAD_DIM, 1)
    at_low, at_high = (both, swapped) if j == 0 else (swapped, both)
    return jnp.where(low, at_low, 0.0).astype(BF16), jnp.where(low, 0.0, at_high).astype(BF16)


def _fold_pair_halves(acc, j, low):
    folded = acc + pltpu.roll(acc, HEAD_DIM, 1)
    return jnp.where(low, folded, 0.0) if j == 0 else jnp.where(low, 0.0, folded)


def _pair_lanes(j, i):
    g = j * PAIRS_PER_KV + i
    return slice(g * LANES, (g + 1) * LANES), 2 * g


def _attn_fwd(sinks, q, k, v):
    seq = q.shape[0]
    cur, prev = _kv_specs()

    def body(sink_ref, q_ref, kc_ref, kp_ref, vc_ref, vp_ref, o_ref, p_ref, ps_ref):
        mask = _band_mask(pl.program_id(0))
        lane = lax.broadcasted_iota(jnp.int32, (BLOCK, LANES), 1)
        p_sinks = jnp.zeros((BLOCK, LANES), F32)
        for j in range(N_KV_HEADS):
            cs = slice(j * HEAD_DIM, (j + 1) * HEAD_DIM)
            kk = jnp.concatenate([kp_ref[:, cs], kc_ref[:, cs]], axis=0)
            vv = jnp.concatenate([vp_ref[:, cs], vc_ref[:, cs]], axis=0)
            for gq in range(Q_PER_KV):
                h = j * Q_PER_KV + gq
                hs = slice(h * HEAD_DIM, (h + 1) * HEAD_DIM)
                probs, p_sink = _softmax_with_sink(_nt(q_ref[:, hs], kk), mask, sink_ref[h])
                pb = probs.astype(BF16)
                p_ref[h] = pb
                p_sinks = jnp.where(lane == h, p_sink, p_sinks)
                o_ref[:, hs] = jnp.dot(pb, vv, preferred_element_type=F32).astype(BF16)
        ps_ref[...] = p_sinks

    return _call(
        body, name="attn_fwd", grid=(seq // BLOCK,),
        in_specs=[pl.BlockSpec(memory_space=pltpu.SMEM), _rows(BLOCK, Q_DIM), cur, prev, cur, prev],
        out_specs=[_rows(BLOCK, Q_DIM), pl.BlockSpec((N_Q_HEADS, BLOCK, 2 * BLOCK), lambda n: (0, n, 0)),
                   _rows(BLOCK, LANES)],
        out_shape=[jax.ShapeDtypeStruct((seq, Q_DIM), BF16),
                   jax.ShapeDtypeStruct((N_Q_HEADS, seq, 2 * BLOCK), BF16),
                   jax.ShapeDtypeStruct((seq, LANES), F32)],
        sem=("parallel",))(sinks, q, k, k, v, v)


def _mm_res(name, a, w, b, res):
    seq, kdim = a.shape
    n = w.shape[1]
    tm = min(2 * TOKEN_TILE, seq)

    def body(a_ref, w_ref, b_ref, r_ref, o_ref):
        o_ref[...] = r_ref[...] + (jnp.dot(a_ref[...], w_ref[...], preferred_element_type=F32) + b_ref[...])

    return _call(
        body, name=name, grid=(seq // tm,),
        in_specs=[_rows(tm, kdim), _resident((kdim, n)), _full((1, n)), _rows(tm, n)],
        out_specs=_rows(tm, n), out_shape=jax.ShapeDtypeStruct((seq, n), F32),
        sem=("parallel",))(a, w, b, res)


def _ff_tile_spec(slot):
    return pl.BlockSpec((FF_TILE_DEVS, ROWS_FF, D_MODEL), lambda j, i: (j, slot, 0))


def _ff_whole_spec(slot):
    return pl.BlockSpec((N_DEV, ROWS_FF, D_MODEL), lambda i: (0, slot, 0), pipeline_mode=pl.Buffered(1))


def _ffn_up(name, h, g, gathered):
    seq = h.shape[0]
    tm = min(TOKEN_TILE, seq)

    def body(h_ref, g_ref, w1_ref, w3_ref, f_ref, u_ref, w_ref, s_ref):
        hv = h_ref[...]
        f = (hv * _rms_stats(hv) * g_ref[...]).astype(BF16)
        f_ref[...] = f
        u = _nt(f, w1_ref[...].reshape(FF_TILE, D_MODEL))
        w = _nt(f, w3_ref[...].reshape(FF_TILE, D_MODEL))
        u_ref[...] = u.astype(BF16)
        w_ref[...] = w.astype(BF16)
        s_ref[...] = (u * _sig(u) * w).astype(BF16)

    n_t = seq // tm
    tile_ff = pl.BlockSpec((tm, FF_TILE), lambda j, i: (i, j))
    f_spec = pl.BlockSpec((tm, D_MODEL), lambda j, i: (jnp.where(j == 0, i, n_t), 0))
    ff_shape = jax.ShapeDtypeStruct((seq, D_FF), BF16)
    return _call(
        body, name=name, grid=(FF_SPLIT, n_t),
        in_specs=[pl.BlockSpec((tm, D_MODEL), lambda j, i: (i, 0)), _full((1, D_MODEL)),
                  _ff_tile_spec(W1T_SLOT), _ff_tile_spec(W3T_SLOT)],
        out_specs=[f_spec, tile_ff, tile_ff, tile_ff],
        out_shape=[jax.ShapeDtypeStruct((seq + tm, D_MODEL), BF16), ff_shape, ff_shape, ff_shape],
        sem=("arbitrary", "arbitrary"))(h, g, gathered, gathered)


def _ffn_down(name, s, gathered, res):
    seq = s.shape[0]
    tm = min(2 * TOKEN_TILE, seq)

    def body(s_ref, w_ref, r_ref, o_ref):
        w2 = w_ref[...].reshape(D_FF, D_MODEL)
        o_ref[...] = r_ref[...] + jnp.dot(s_ref[...], w2, preferred_element_type=F32)

    return _call(
        body, name=name, grid=(seq // tm,),
        in_specs=[_rows(tm, D_FF), _ff_whole_spec(W2_SLOT), _rows(tm, D_MODEL)],
        out_specs=_rows(tm, D_MODEL), out_shape=jax.ShapeDtypeStruct((seq, D_MODEL), F32),
        sem=("parallel",))(s, gathered, res)


def _conv_fwd(h, g, wpw1, bpw1, wd, bdw, lng, lnb):
    seq = h.shape[0]
    tm = min(TOKEN_TILE, seq)
    n_chunks = tm // CONV_CHUNK
    win = CONV_CHUNK + CONV_HALO

    def body(h_ref, g_ref, w_ref, b_ref, wd_ref, bdw_ref, lng_ref, lnb_ref,
             y_ref, a_ref, dwc_ref, z_ref, gbuf):
        i = pl.program_id(0)

        @pl.when(i == 0)
        def _():
            gbuf[0:CONV_HALO, :] = jnp.zeros((CONV_HALO, D_MODEL), F32)

        @pl.when(i > 0)
        def _():
            gbuf[0:CONV_HALO, :] = gbuf[tm:tm + CONV_HALO, :]

        hv = h_ref[...]
        y = (hv * _rms_stats(hv) * g_ref[...]).astype(BF16)
        y_ref[...] = y
        a = _nt(y, w_ref[...]) + b_ref[...]
        a_ref[...] = a.astype(BF16)
        gbuf[CONV_HALO:CONV_HALO + tm, :] = a[:, :D_MODEL] * _sig(a[:, D_MODEL:])

        def chunk(c, carry):
            r0 = pl.multiple_of(c * CONV_CHUNK, CONV_CHUNK)
            for l in range(D_MODEL // LANES):
                ls = slice(l * LANES, (l + 1) * LANES)
                gw = gbuf[pl.ds(r0, win), ls]
                acc = jnp.zeros((CONV_CHUNK, LANES), F32) + bdw_ref[:, ls]
                for s in range(SUBLANES):
                    gs = gw if s == 0 else pltpu.roll(gw, s, 0)
                    for q in range(CONV_HALO // SUBLANES):
                        d = SUBLANES * q + s
                        if d < CONV_WIDTH:
                            lo = CONV_HALO - SUBLANES * q
                            acc = acc + wd_ref[d:d + 1, ls] * gs[lo:lo + CONV_CHUNK]
                dwc_ref[pl.ds(r0, CONV_CHUNK), ls] = acc
            return carry

        lax.fori_loop(0, n_chunks, chunk, 0)

        xv = dwc_ref[...]
        mu = jnp.mean(xv, axis=-1, keepdims=True)
        xc = xv - mu
        var = jnp.mean(xc * xc, axis=-1, keepdims=True)
        ln = xc * lax.rsqrt(var + LN_EPS) * lng_ref[...] + lnb_ref[...]
        z_ref[...] = (ln * _sig(ln)).astype(BF16)

    return _call(
        body, name="conv_fwd", grid=(seq // tm,),
        in_specs=[_rows(tm, D_MODEL), _full((1, D_MODEL)), _full((2 * D_MODEL, D_MODEL)), _full((1, 2 * D_MODEL)),
                  _full((TAPS_PAD, D_MODEL)), _full((1, D_MODEL)), _full((1, D_MODEL)), _full((1, D_MODEL))],
        out_specs=[_rows(tm, D_MODEL), _rows(tm, 2 * D_MODEL), _rows(tm, D_MODEL), _rows(tm, D_MODEL)],
        out_shape=[jax.ShapeDtypeStruct((seq, D_MODEL), BF16), jax.ShapeDtypeStruct((seq, 2 * D_MODEL), BF16),
                   jax.ShapeDtypeStruct((seq, D_MODEL), F32), jax.ShapeDtypeStruct((seq, D_MODEL), BF16)],
        scratch=[pltpu.VMEM((tm + CONV_HALO, D_MODEL), F32)],
        sem=("arbitrary",))(h, g, wpw1, bpw1, wd, bdw, lng, lnb)


def _ffn_down_loss(s, gathered, res, target, g):
    seq = s.shape[0]
    tm = min(TOKEN_TILE, seq)

    def body(s_ref, w_ref, r_ref, t_ref, g_ref, dh_ref, dhb_ref, loss_ref, dg_ref):
        @pl.when(pl.program_id(0) == 0)
        def _():
            loss_ref[...] = jnp.zeros_like(loss_ref)
            dg_ref[...] = jnp.zeros_like(dg_ref)

        hv = r_ref[...] + jnp.dot(s_ref[...], w_ref[...].reshape(D_FF, D_MODEL), preferred_element_type=F32)
        gv = g_ref[...]
        err = hv * _rms_stats(hv) * gv - t_ref[...]
        sq = jnp.sum(jnp.sum(err * err, axis=-1, keepdims=True), axis=0, keepdims=True)
        loss_ref[...] += jnp.broadcast_to(sq, loss_ref.shape)
        dx, dg = _rms_bwd(err * (1.0 / D_MODEL), hv, gv, 0.0)
        dh_ref[...] = dx
        dhb_ref[...] = dx.astype(BF16)
        dg_ref[...] += dg

    return _call(
        body, name="ffn1_down_loss", grid=(seq // tm,),
        in_specs=[_rows(tm, D_FF), _ff_whole_spec(W2_SLOT), _rows(tm, D_MODEL), _rows(tm, D_MODEL),
                  _full((1, D_MODEL))],
        out_specs=[_rows(tm, D_MODEL), _rows(tm, D_MODEL), _full((SUBLANES, LANES)), _full((1, D_MODEL))],
        out_shape=[jax.ShapeDtypeStruct((seq, D_MODEL), F32), jax.ShapeDtypeStruct((seq, D_MODEL), BF16),
                   jax.ShapeDtypeStruct((SUBLANES, LANES), F32), jax.ShapeDtypeStruct((1, D_MODEL), F32)],
        sem=("arbitrary",))(s, gathered, res, target, g)


def _ffn_bwd_act(name, dh, u, w, gathered):
    seq = dh.shape[0]
    tm = min(TOKEN_TILE, seq)

    def body(dh_ref, u_ref, w_ref, w2_ref, du_ref, dw_ref):
        ds = _nt(dh_ref[...], w2_ref[...].reshape(FF_TILE, D_MODEL))
        uv = u_ref[...].astype(F32)
        sg = _sig(uv)
        dw_ref[...] = (ds * (uv * sg)).astype(BF16)
        du_ref[...] = (ds * w_ref[...].astype(F32) * (sg * (1.0 + uv * (1.0 - sg)))).astype(BF16)

    tile_ff = pl.BlockSpec((tm, FF_TILE), lambda j, i: (i, j))
    ff_shape = jax.ShapeDtypeStruct((seq, D_FF), BF16)
    return _call(
        body, name=name, grid=(FF_SPLIT, seq // tm),
        in_specs=[pl.BlockSpec((tm, D_MODEL), lambda j, i: (i, 0)), tile_ff, tile_ff, _ff_tile_spec(W2_SLOT)],
        out_specs=[tile_ff, tile_ff], out_shape=[ff_shape, ff_shape],
        sem=("parallel", "parallel"))(dh, u, w, gathered)


def _ffn_bwd_in(name, du, dw, gathered, h_in, dres, g):
    seq = du.shape[0]
    tm = min(TOKEN_TILE, seq)

    def body(du_ref, dw_ref, w1_ref, w3_ref, h_ref, dr_ref, g_ref, dx_ref, dxb_ref, dg_ref):
        @pl.when(pl.program_id(0) == 0)
        def _():
            dg_ref[...] = jnp.zeros_like(dg_ref)

        df = jnp.dot(du_ref[...], w1_ref[...].reshape(D_FF, D_MODEL), preferred_element_type=F32)
        df = df + jnp.dot(dw_ref[...], w3_ref[...].reshape(D_FF, D_MODEL), preferred_element_type=F32)
        dx, dg = _rms_bwd(df, h_ref[...], g_ref[...], dr_ref[...])
        dx_ref[...] = dx
        dxb_ref[...] = dx.astype(BF16)
        dg_ref[...] += dg

    return _call(
        body, name=name, grid=(seq // tm,),
        in_specs=[_rows(tm, D_FF), _rows(tm, D_FF), _ff_whole_spec(W1T_SLOT), _ff_whole_spec(W3T_SLOT),
                  _rows(tm, D_MODEL), _rows(tm, D_MODEL), _full((1, D_MODEL))],
        out_specs=[_rows(tm, D_MODEL), _rows(tm, D_MODEL), _full((1, D_MODEL))],
        out_shape=[jax.ShapeDtypeStruct((seq, D_MODEL), F32), jax.ShapeDtypeStruct((seq, D_MODEL), BF16),
                   jax.ShapeDtypeStruct((1, D_MODEL), F32)],
        sem=("arbitrary",))(du, dw, gathered, gathered, h_in, dres, g)


def _mm_tn(name, a, b, *, tk, into=None):
    seq, kdim = a.shape
    n = b.shape[1]
    tt = min((4 if b.dtype == BF16 else 2) * TOKEN_TILE, seq)
    n_t = seq // tt
    devs = tk // ROWS_FF

    def body(a_ref, b_ref, *rest):
        o_ref, acc = rest[-2:]
        t = pl.program_id(1)

        @pl.when(t == 0)
        def _():
            acc[...] = jnp.zeros_like(acc)

        acc[...] += _tn(a_ref[...].astype(BF16), b_ref[...].astype(BF16))

        @pl.when(t == n_t - 1)
        def _():
            out = acc[...].astype(BF16)
            o_ref[...] = out if into is None else out.reshape(devs, ROWS_FF, n)

    in_specs = [pl.BlockSpec((tt, tk), lambda k, t: (t, k)), pl.BlockSpec((tt, n), lambda k, t: (t, 0))]
    args = [a, b]
    aliases = None
    if into is None:
        out_spec = pl.BlockSpec((tk, n), lambda k, t: (k, 0))
        out_shape = jax.ShapeDtypeStruct((kdim, n), BF16)
    else:
        blocks, slot = into
        out_spec = pl.BlockSpec((devs, ROWS_FF, n), lambda k, t: (k, slot, 0))
        out_shape = jax.ShapeDtypeStruct((N_DEV, 3 * ROWS_FF, n), BF16)
        if blocks is not None:
            in_specs.append(_ANY)
            args.append(blocks)
            aliases = {2: 0}
    return _call(
        body, name=name, grid=(kdim // tk, n_t), in_specs=in_specs, out_specs=out_spec, out_shape=out_shape,
        scratch=[pltpu.VMEM((tk, n), F32)], sem=("parallel", "arbitrary"), aliases=aliases)(*args)


def _conv_bwd(dh, wpw2, dwc, a, lng, lnb, wd):
    seq = dh.shape[0]
    tm = min(TOKEN_TILE, seq)
    nt = seq // tm
    n_chunks = tm // CONV_CHUNK
    win = CONV_CHUNK + CONV_HALO
    halo_per_tile = tm // CONV_HALO

    def body(dh_ref, w_ref, dwc_ref, a_ref, ah_ref, lng_ref, lnb_ref, wd_ref,
             da_ref, dlg_ref, dlb_ref, dbdw_ref, dwd_ref, dbpw1_ref, dbpw2_ref,
             gbuf, dbuf, dglu, dwd_part):
        i = pl.program_id(0)
        r = nt - 1 - i

        @pl.when(i == 0)
        def _():
            dlg_ref[...] = jnp.zeros_like(dlg_ref)
            dlb_ref[...] = jnp.zeros_like(dlb_ref)
            dbdw_ref[...] = jnp.zeros_like(dbdw_ref)
            dbpw1_ref[...] = jnp.zeros_like(dbpw1_ref)
            dbpw2_ref[...] = jnp.zeros_like(dbpw2_ref)
            dwd_part[...] = jnp.zeros_like(dwd_part)
            dbuf[tm:tm + CONV_HALO, :] = jnp.zeros((CONV_HALO, D_MODEL), F32)

        @pl.when(i > 0)
        def _():
            dbuf[tm:tm + CONV_HALO, :] = dbuf[0:CONV_HALO, :]

        dhv = dh_ref[...]
        dbpw2_ref[...] += _sum_rows(dhv)
        dz = _nt(dhv.astype(BF16), w_ref[...])
        xv = dwc_ref[...]
        lg = lng_ref[...]
        mu = jnp.mean(xv, axis=-1, keepdims=True)
        xc = xv - mu
        rstd = lax.rsqrt(jnp.mean(xc * xc, axis=-1, keepdims=True) + LN_EPS)
        xhat = xc * rstd
        ln = xhat * lg + lnb_ref[...]
        sg = _sig(ln)
        dln = dz * (sg * (1.0 + ln * (1.0 - sg)))
        dlg_ref[...] += _sum_rows(dln * xhat)
        dlb_ref[...] += _sum_rows(dln)
        dxh = dln * lg
        ddw = rstd * (dxh - jnp.mean(dxh, axis=-1, keepdims=True)
                      - xhat * jnp.mean(dxh * xhat, axis=-1, keepdims=True))
        dbdw_ref[...] += _sum_rows(ddw)
        dbuf[0:tm, :] = ddw

        av = a_ref[...].astype(F32)
        a1 = av[:, :D_MODEL]
        s2 = _sig(av[:, D_MODEL:])
        gbuf[CONV_HALO:CONV_HALO + tm, :] = a1 * s2
        ah = ah_ref[...].astype(F32)
        gh = ah[:, :D_MODEL] * _sig(ah[:, D_MODEL:])
        gbuf[0:CONV_HALO, :] = jnp.where(r > 0, gh, 0.0)

        def chunk(c, carry):
            r0 = pl.multiple_of(c * CONV_CHUNK, CONV_CHUNK)
            for l in range(D_MODEL // LANES):
                ls = slice(l * LANES, (l + 1) * LANES)
                dw_ = dbuf[pl.ds(r0, win), ls]
                gw = gbuf[pl.ds(r0, win), ls]
                acc = jnp.zeros((CONV_CHUNK, LANES), F32)
                for s in range(SUBLANES):
                    ds_ = dw_ if s == 0 else pltpu.roll(dw_, win - s, 0)
                    for q in range(CONV_HALO // SUBLANES):
                        d = SUBLANES * q + s
                        if d < CONV_WIDTH:
                            acc = acc + wd_ref[d:d + 1, ls] * ds_[SUBLANES * q:SUBLANES * q + CONV_CHUNK]
                            lo = CONV_HALO - SUBLANES * q
                            prod = ds_[0:CONV_CHUNK] * gw[lo:lo + CONV_CHUNK]
                            dwd_part[d, :, ls] += jnp.sum(
                                prod.reshape(CONV_CHUNK // SUBLANES, SUBLANES, LANES), axis=0)
                dglu[pl.ds(r0, CONV_CHUNK), ls] = acc
            return carry

        lax.fori_loop(0, n_chunks, chunk, 0)

        dg_ = dglu[...]
        da1 = dg_ * s2
        da2 = dg_ * a1 * s2 * (1.0 - s2)
        da_ref[:, :D_MODEL] = da1.astype(BF16)
        da_ref[:, D_MODEL:] = da2.astype(BF16)
        dbpw1_ref[:, :D_MODEL] += _sum_rows(da1)
        dbpw1_ref[:, D_MODEL:] += _sum_rows(da2)

        @pl.when(i == nt - 1)
        def _():
            dwd_ref[...] = jnp.sum(dwd_part[...], axis=1)

    rev = lambda n: pl.BlockSpec((tm, n), lambda i: (nt - 1 - i, 0))
    halo = pl.BlockSpec((CONV_HALO, 2 * D_MODEL),
                        lambda i: (jnp.maximum((nt - 1 - i) * halo_per_tile - 1, 0), 0))
    vec = lambda n: _full((1, n))
    return _call(
        body, name="conv_bwd", grid=(nt,),
        in_specs=[rev(D_MODEL), _full((D_MODEL, D_MODEL)), rev(D_MODEL), rev(2 * D_MODEL), halo,
                  vec(D_MODEL), vec(D_MODEL), _full((TAPS_PAD, D_MODEL))],
        out_specs=[rev(2 * D_MODEL), vec(D_MODEL), vec(D_MODEL), vec(D_MODEL), _full((TAPS_PAD, D_MODEL)),
                   vec(2 * D_MODEL), vec(D_MODEL)],
        out_shape=[jax.ShapeDtypeStruct((seq, 2 * D_MODEL), BF16), jax.ShapeDtypeStruct((1, D_MODEL), F32),
                   jax.ShapeDtypeStruct((1, D_MODEL), F32), jax.ShapeDtypeStruct((1, D_MODEL), F32),
                   jax.ShapeDtypeStruct((TAPS_PAD, D_MODEL), F32), jax.ShapeDtypeStruct((1, 2 * D_MODEL), F32),
                   jax.ShapeDtypeStruct((1, D_MODEL), F32)],
        scratch=[pltpu.VMEM((tm + CONV_HALO, D_MODEL), F32), pltpu.VMEM((tm + CONV_HALO, D_MODEL), F32),
                 pltpu.VMEM((tm, D_MODEL), F32), pltpu.VMEM((TAPS_PAD, SUBLANES, D_MODEL), F32)],
        sem=("arbitrary",))(dh, wpw2, dwc, a, a, lng, lnb, wd)


def _mm_rms_bwd(name, dact, wt, h_in, dres, g, bf16_copy):
    seq, n = dact.shape
    tm = min(TOKEN_TILE, seq)

    def body(da_ref, w_ref, h_ref, dr_ref, g_ref, dx_ref, *rest):
        dg_ref = rest[-1]

        @pl.when(pl.program_id(0) == 0)
        def _():
            dg_ref[...] = jnp.zeros_like(dg_ref)

        dy = jnp.dot(da_ref[...], w_ref[...], preferred_element_type=F32)
        dx, dg = _rms_bwd(dy, h_ref[...], g_ref[...], dr_ref[...])
        dx_ref[...] = dx
        if bf16_copy:
            rest[0][...] = dx.astype(BF16)
        dg_ref[...] += dg

    copy_spec = [_rows(tm, D_MODEL)] if bf16_copy else []
    copy_shape = [jax.ShapeDtypeStruct((seq, D_MODEL), BF16)] if bf16_copy else []
    return _call(
        body, name=name, grid=(seq // tm,),
        in_specs=[_rows(tm, n), _resident((n, D_MODEL)), _rows(tm, D_MODEL), _rows(tm, D_MODEL), _full((1, D_MODEL))],
        out_specs=[_rows(tm, D_MODEL), *copy_spec, _full((1, D_MODEL))],
        out_shape=[jax.ShapeDtypeStruct((seq, D_MODEL), F32), *copy_shape, jax.ShapeDtypeStruct((1, D_MODEL), F32)],
        sem=("arbitrary",))(dact, wt, h_in, dres, g)


def _nt_bias(name, dy, w):
    seq, n = dy.shape
    kdim = w.shape[0]
    tm = min(2 * TOKEN_TILE, seq)

    def body(dy_ref, w_ref, o_ref, db_ref):
        @pl.when(pl.program_id(0) == 0)
        def _():
            db_ref[...] = jnp.zeros_like(db_ref)

        dyv = dy_ref[...]
        db_ref[...] += _sum_rows(dyv)
        o_ref[...] = _nt(dyv.astype(BF16), w_ref[...]).astype(BF16)

    return _call(
        body, name=name, grid=(seq // tm,),
        in_specs=[_rows(tm, n), _resident((kdim, n))],
        out_specs=[_rows(tm, kdim), _full((1, n))],
        out_shape=[jax.ShapeDtypeStruct((seq, kdim), BF16), jax.ShapeDtypeStruct((1, n), F32)],
        sem=("arbitrary",))(dy, w)


def _attn_bwd(probs, p_sinks, q, k, v, do):
    seq = q.shape[0]
    cur, prev = _kv_specs()

    def body(p_ref, ps_ref, q_ref, kc_ref, kp_ref, vc_ref, vp_ref, do_ref,
             dq_ref, dkc_ref, dkp_ref, dvc_ref, dvp_ref, dsink_ref):
        n = pl.program_id(0)

        @pl.when(n == 0)
        def _():
            dsink_ref[...] = jnp.zeros_like(dsink_ref)

        low = _low_lanes()
        lane = lax.broadcasted_iota(jnp.int32, (BLOCK, LANES), 1)
        p_sinks_blk = ps_ref[...]
        dk_all = jnp.zeros((2 * BLOCK, KV_DIM), F32)
        dv_all = jnp.zeros((2 * BLOCK, KV_DIM), F32)
        for j in range(N_KV_HEADS):
            k_lo, k_hi = _kv_low_high(kp_ref, kc_ref, j, low)
            v_lo, v_hi = _kv_low_high(vp_ref, vc_ref, j, low)
            dk_acc = jnp.zeros((2 * BLOCK, KV_DIM), F32)
            dv_acc = jnp.zeros((2 * BLOCK, KV_DIM), F32)
            for i in range(PAIRS_PER_KV):
                ls, h = _pair_lanes(j, i)
                qp = q_ref[:, ls]
                dop = do_ref[:, ls]
                dsb, pb16 = [], []
                for t, v_sel in enumerate((v_lo, v_hi)):
                    pb = p_ref[h + t]
                    pf = pb.astype(F32)
                    p_sink = jnp.sum(jnp.where(lane == h + t, p_sinks_blk, 0.0), axis=-1, keepdims=True)
                    dp = _nt(dop, v_sel)
                    delta = jnp.sum(pf * dp, axis=-1, keepdims=True)
                    dsb.append((pf * (dp - delta)).astype(BF16))
                    pb16.append(pb)
                    dsink_ref[h + t:h + t + 1, :] += jnp.broadcast_to(_sum_rows(p_sink * delta), (1, LANES))
                dq = (jnp.dot(dsb[0], k_lo, preferred_element_type=F32)
                      + jnp.dot(dsb[1], k_hi, preferred_element_type=F32))
                dq_ref[:, ls] = dq * (HEAD_DIM ** -0.5)
                dk_acc = dk_acc + jnp.where(low, _tn(dsb[0], qp), _tn(dsb[1], qp))
                dv_acc = dv_acc + jnp.where(low, _tn(pb16[0], dop), _tn(pb16[1], dop))
            dk_all = dk_all + _fold_pair_halves(dk_acc, j, low)
            dv_all = dv_all + _fold_pair_halves(dv_acc, j, low)
        dkp_ref[...] = dk_all[:BLOCK]
        dkc_ref[...] = dk_all[BLOCK:]
        dvp_ref[...] = dv_all[:BLOCK]
        dvc_ref[...] = dv_all[BLOCK:]

    kv_out = _rows(BLOCK, KV_DIM)
    kv_shape = jax.ShapeDtypeStruct((seq, KV_DIM), F32)
    return _call(
        body, name="attn_bwd", grid=(seq // BLOCK,),
        in_specs=[pl.BlockSpec((N_Q_HEADS, BLOCK, 2 * BLOCK), lambda n: (0, n, 0)), _rows(BLOCK, LANES),
                  _rows(BLOCK, Q_DIM), cur, prev, cur, prev, _rows(BLOCK, Q_DIM)],
        out_specs=[_rows(BLOCK, Q_DIM), kv_out, kv_out, kv_out, kv_out, _full((N_Q_HEADS, LANES))],
        out_shape=[jax.ShapeDtypeStruct((seq, Q_DIM), F32), kv_shape, kv_shape, kv_shape, kv_shape,
                   jax.ShapeDtypeStruct((N_Q_HEADS, LANES), F32)],
        sem=("arbitrary",))(probs, p_sinks, q, k, k, v, v, do)


def _rope_bwd(dq, dkc, dkp, dvc, dvp, rc, rsa, rsb):
    seq = dq.shape[0]
    nb = seq // BLOCK
    tm = min(TOKEN_TILE, seq)
    nt = seq // tm
    per = tm // BLOCK
    nxt = pl.BlockSpec((BLOCK, KV_DIM), lambda i: (jnp.minimum((i + 1) * per, nb - 1), 0))

    def body(dq_ref, dkc_ref, dkp_ref, dkn_ref, dvc_ref, dvp_ref, dvn_ref, c_ref, sa_ref, sb_ref, o_ref, db_ref):
        i = pl.program_id(0)

        @pl.when(i == 0)
        def _():
            db_ref[...] = jnp.zeros_like(db_ref)

        c, sa, sb = c_ref[...], sa_ref[...], sb_ref[...]
        last = i == nt - 1

        def from_next_block(prev_ref, next_ref):
            tail = jnp.where(last, 0.0, next_ref[...])
            return tail if per == 1 else jnp.concatenate([prev_ref[BLOCK:, :], tail], axis=0)

        dk = dkc_ref[...] + from_next_block(dkp_ref, dkn_ref)
        dv = dvc_ref[...] + from_next_block(dvp_ref, dvn_ref)
        for l in range(Q_DIM // LANES):
            ls = slice(l * LANES, (l + 1) * LANES)
            blk = _rope_t(dq_ref[:, ls], c, sa, sb)
            o_ref[:, ls] = blk.astype(BF16)
            db_ref[:, ls] += _sum_rows(blk)
        dkr = _rope_t(dk, c, sa, sb)
        o_ref[:, Q_DIM:Q_DIM + KV_DIM] = dkr.astype(BF16)
        db_ref[:, Q_DIM:Q_DIM + KV_DIM] += _sum_rows(dkr)
        o_ref[:, Q_DIM + KV_DIM:] = dv.astype(BF16)
        db_ref[:, Q_DIM + KV_DIM:] += _sum_rows(dv)

    kv = _rows(tm, KV_DIM)
    tab = _rows(tm, LANES)
    return _call(
        body, name="rope_bwd", grid=(nt,),
        in_specs=[_rows(tm, Q_DIM), kv, kv, nxt, kv, kv, nxt, tab, tab, tab],
        out_specs=[_rows(tm, QKV_DIM), _full((1, QKV_DIM))],
        out_shape=[jax.ShapeDtypeStruct((seq, QKV_DIM), BF16), jax.ShapeDtypeStruct((1, QKV_DIM), F32)],
        sem=("arbitrary",))(dq, dkc, dkp, dkp, dvc, dvp, dvp, rc, rsa, rsb)


def _adamw(w, g, m, v):
    m = ADAM_B1 * m + (1.0 - ADAM_B1) * g
    v = ADAM_B2 * v + (1.0 - ADAM_B2) * (g * g)
    m_hat = m / (1.0 - ADAM_B1 ** ADAM_STEP)
    v_hat = v / (1.0 - ADAM_B2 ** ADAM_STEP)
    delta = -ADAM_LR * (m_hat / (jnp.sqrt(v_hat) + ADAM_EPS) + ADAM_WD * w)
    return delta, m, v


def _sum_slots(name, parts):
    _, rows, cols = parts.shape
    tr = rows if rows <= 512 else ROWS_FF

    def body(p_ref, g_ref):
        g = p_ref[0].astype(F32)
        for d in range(1, N_DEV):
            g = g + p_ref[d].astype(F32)
        g_ref[...] = g

    return _call(
        body, name=name, grid=(rows // tr,),
        in_specs=[pl.BlockSpec((N_DEV, tr, cols), lambda i: (0, i, 0))],
        out_specs=_rows(tr, cols), out_shape=jax.ShapeDtypeStruct((rows, cols), F32),
        sem=("parallel",))(parts)


def _adamw_native(name, g, w, m, v):
    layers, rows, cols = w.shape
    tr = rows if rows <= 512 else 256

    def body(g_ref, w_ref, m_ref, v_ref, d_ref, nm_ref, nv_ref):
        d_ref[...], nm_ref[...], nv_ref[...] = _adamw(w_ref[...], g_ref[...], m_ref[...], v_ref[...])

    spec = pl.BlockSpec((1, tr, cols), lambda l, i: (l, i, 0))
    shape = jax.ShapeDtypeStruct(w.shape, F32)
    return _call(
        body, name=name, grid=(layers, rows // tr), in_specs=[spec, spec, spec, spec],
        out_specs=[spec, spec, spec], out_shape=[shape, shape, shape],
        sem=("parallel", "parallel"))(g, w, m, v)


def _adamw_from_slots(name, slots, row_block, w, m, v):
    layers, rows, cols = w.shape

    def body(*refs):
        slot_refs = refs[:layers]
        w_ref, m_ref, v_ref, g_ref, d_ref, nm_ref, nv_ref = refs[layers:]
        for l in range(layers):
            @pl.when(pl.program_id(0) == l)
            def _(p_ref=slot_refs[l]):
                g = p_ref[0].astype(F32)
                for d in range(1, N_DEV):
                    g = g + p_ref[d].astype(F32)
                g_ref[0] = g
        d, nm, nv = _adamw(w_ref[0], g_ref[0], m_ref[0], v_ref[0])
        d_ref[0], nm_ref[0], nv_ref[0] = d, nm, nv

    slot_spec = pl.BlockSpec((N_DEV, rows, cols), lambda l: (0, row_block, 0))
    spec = pl.BlockSpec((1, rows, cols), lambda l: (l, 0, 0))
    shape = jax.ShapeDtypeStruct(w.shape, F32)
    return _call(
        body, name=name, grid=(layers,), in_specs=[slot_spec] * layers + [spec, spec, spec],
        out_specs=[spec, spec, spec, spec], out_shape=[shape, shape, shape, shape],
        sem=("arbitrary",))(*slots, w, m, v)


def _adamw_replicated(parts, w, m, v):
    def body(p_ref, w_ref, m_ref, v_ref, g_ref, d_ref, nm_ref, nv_ref):
        g = p_ref[0]
        for j in range(1, N_DEV):
            g = g + p_ref[j]
        g_ref[...] = g
        d_ref[...], nm_ref[...], nv_ref[...] = _adamw(w_ref[...], g, m_ref[...], v_ref[...])

    spec = _full((REPL_ROWS, D_MODEL))
    shape = jax.ShapeDtypeStruct((REPL_ROWS, D_MODEL), F32)
    return _call(
        body, name="adamw_replicated", grid=(1,),
        in_specs=[_full((N_DEV, REPL_ROWS, D_MODEL)), spec, spec, spec],
        out_specs=[spec, spec, spec, spec], out_shape=[shape, shape, shape, shape],
        sem=("arbitrary",))(parts, w, m, v)


_MESH = pl.DeviceIdType.MESH
_ANY = pl.BlockSpec(memory_space=pl.ANY)


def _all_gather(name, xs):
    rows, cols = xs.shape

    def body(x_ref, out_ref, send_sems, recv_sems, local_sem):
        x, y, c = lax.axis_index("x"), lax.axis_index("y"), lax.axis_index("c")
        me, sibling = (x, y, c), (x, y, 1 - c)
        chips = [(1 - x, y), (x, 1 - y), (1 - x, 1 - y)]

        def slot(px, py, pc):
            return out_ref.at[4 * px + 2 * py + pc]

        def copy(k, block, to, src=None):
            return pltpu.make_async_remote_copy(
                src_ref=slot(*block) if src is None else src, dst_ref=slot(*block),
                send_sem=send_sems.at[k], recv_sem=recv_sems.at[k], device_id=to, device_id_type=_MESH)

        mine = pltpu.make_async_copy(x_ref, slot(*me), local_sem)
        mine.start()
        first = [copy(0, me, sibling, src=x_ref)]
        first += [copy(1 + j, me, (*chip, c), src=x_ref) for j, chip in enumerate(chips)]
        for cp in first:
            cp.start()
        passed = [copy(4 + j, (*chip, c), sibling) for j, chip in enumerate(chips)]
        for j, chip in enumerate(chips):
            copy(1 + j, (*chip, c), me).wait_recv()
            passed[j].start()
        copy(0, sibling, me).wait_recv()
        for j, chip in enumerate(chips):
            copy(4 + j, (*chip, 1 - c), me).wait_recv()
        for cp in first + passed:
            cp.wait_send()
        mine.wait()

    return pl.pallas_call(
        body, name=name, out_shape=jax.ShapeDtypeStruct((N_DEV, rows, cols), xs.dtype),
        in_specs=[_ANY], out_specs=_ANY,
        scratch_shapes=[pltpu.SemaphoreType.DMA((7,)), pltpu.SemaphoreType.DMA((7,)), pltpu.SemaphoreType.DMA],
    )(xs)


N_PEERS = N_DEV - 1
_HBM = pl.BlockSpec(memory_space=pltpu.HBM)
_SEM = pl.BlockSpec(memory_space=pltpu.SEMAPHORE)
_DATAFLOW = pltpu.SideEffectType.DATAFLOW_SIDE_EFFECTING
_TOKEN = jax.ShapeDtypeStruct((SUBLANES, LANES), F32)


def _peers():
    x, y, c = lax.axis_index("x"), lax.axis_index("y"), lax.axis_index("c")
    out = []
    for k in range(1, N_DEV):
        px = 1 - x if k & 4 else x
        py = 1 - y if k & 2 else y
        pc = 1 - c if k & 1 else c
        out.append(((px, py, pc), 4 * px + 2 * py + pc))
    return 4 * x + 2 * y + c, out


def _in_hbm(a):
    return pltpu.with_memory_space_constraint(a, pltpu.HBM)


def _landing(rows):
    return _in_hbm(lax.empty((N_DEV, rows, D_MODEL), BF16))


def _sem_pair():
    return pltpu.SemaphoreType.DMA((N_PEERS,)), pltpu.SemaphoreType.DMA((N_PEERS,))


def _gather_start(name, payloads):
    n = len(payloads)

    def body(*refs):
        src, land = refs[:n], refs[n:2 * n]
        sems = refs[2 * n:4 * n]
        token = refs[-1]
        me, peers = _peers()
        for g in range(n):
            for k, (pos, _) in enumerate(peers):
                pltpu.make_async_remote_copy(
                    src_ref=src[g], dst_ref=land[g].at[me], send_sem=sems[2 * g].at[k],
                    recv_sem=sems[2 * g + 1].at[k], device_id=pos, device_id_type=_MESH).start()
        token[...] = jnp.zeros_like(token)

    lands = [_landing(p.shape[0]) for p in payloads]
    sem_shapes = [s for _ in payloads for s in _sem_pair()]
    hbm_shapes = [pltpu.HBM(a.shape, a.dtype) for a in list(payloads) + lands]
    out = pl.pallas_call(
        body, name=name, out_shape=(*sem_shapes, *hbm_shapes, _TOKEN),
        in_specs=[_HBM] * (2 * n), out_specs=(*[_SEM] * (2 * n), *[_HBM] * (2 * n), pl.BlockSpec(memory_space=pltpu.VMEM)),
        input_output_aliases={i: 2 * n + i for i in range(2 * n)},
        compiler_params=pltpu.CompilerParams(has_side_effects=_DATAFLOW),
    )(*[_in_hbm(p) for p in payloads], *lands)
    sems, thru = out[:2 * n], out[2 * n:4 * n]
    return [(thru[g], thru[n + g], sems[2 * g], sems[2 * g + 1]) for g in range(n)], out[-1]


def _gather_wait(name, group, after):
    payload, land, send_sems, recv_sems = group

    def body(src_ref, land_ref, send_ref, recv_ref, after_ref, src_out, land_out):
        _, peers = _peers()
        for k, (pos, idx) in enumerate(peers):
            cp = pltpu.make_async_remote_copy(
                src_ref=src_ref, dst_ref=land_ref.at[idx], send_sem=send_ref.at[k], recv_sem=recv_ref.at[k],
                device_id=pos, device_id_type=_MESH)
            cp.wait_send()
            cp.wait_recv()

    _, land = pl.pallas_call(
        body, name=name, out_shape=(pltpu.HBM(payload.shape, payload.dtype), pltpu.HBM(land.shape, land.dtype)),
        in_specs=[_HBM, _HBM, _SEM, _SEM, _ANY], out_specs=(_HBM, _HBM), input_output_aliases={0: 0, 1: 1},
        compiler_params=pltpu.CompilerParams(has_side_effects=_DATAFLOW),
    )(payload, land, send_sems, recv_sems, after)
    me = 4 * lax.axis_index("x") + 2 * lax.axis_index("y") + lax.axis_index("c")
    return lax.dynamic_update_slice(land, payload[None], (me, 0, 0))


def _scatter_start(name, blocks):
    rows = blocks.shape[1]

    def body(blocks_ref, land_ref, send_sems, recv_sems, blocks_out, land_out, token):
        me, peers = _peers()
        for k, (pos, idx) in enumerate(peers):
            pltpu.make_async_remote_copy(
                src_ref=blocks_ref.at[idx], dst_ref=land_ref.at[me], send_sem=send_sems.at[k],
                recv_sem=recv_sems.at[k], device_id=pos, device_id_type=_MESH).start()
        token[...] = jnp.zeros_like(token)

    land = _landing(rows)
    send_sems, recv_sems, blocks_thru, land_thru, token = pl.pallas_call(
        body, name=name,
        out_shape=(*_sem_pair(), pltpu.HBM(blocks.shape, blocks.dtype), pltpu.HBM(land.shape, land.dtype), _TOKEN),
        in_specs=[_HBM, _HBM], out_specs=(_SEM, _SEM, _HBM, _HBM, pl.BlockSpec(memory_space=pltpu.VMEM)),
        input_output_aliases={0: 2, 1: 3},
        compiler_params=pltpu.CompilerParams(has_side_effects=_DATAFLOW),
    )(_in_hbm(blocks), land)
    return (blocks_thru, land_thru, send_sems, recv_sems), token


def _scatter_wait(groups, after):
    n = len(groups)

    def body(*refs):
        blocks, land = refs[:n], refs[n:2 * n]
        sems = refs[2 * n:4 * n]
        _, peers = _peers()
        for g in range(n):
            for k, (pos, idx) in enumerate(peers):
                cp = pltpu.make_async_remote_copy(
                    src_ref=blocks[g].at[idx], dst_ref=land[g].at[idx], send_sem=sems[2 * g].at[k],
                    recv_sem=sems[2 * g + 1].at[k], device_id=pos, device_id_type=_MESH)
                cp.wait_send()
                cp.wait_recv()

    hbm = [grp[0] for grp in groups] + [grp[1] for grp in groups]
    sems = [s for grp in groups for s in grp[2:]]
    out = pl.pallas_call(
        body, name="rs_wait", out_shape=tuple(pltpu.HBM(a.shape, a.dtype) for a in hbm),
        in_specs=[_HBM] * (2 * n) + [_SEM] * (2 * n) + [_ANY], out_specs=tuple([_HBM] * (2 * n)),
        input_output_aliases={i: i for i in range(2 * n)},
        compiler_params=pltpu.CompilerParams(has_side_effects=_DATAFLOW),
    )(*hbm, *sems, after)
    me = 4 * lax.axis_index("x") + 2 * lax.axis_index("y") + lax.axis_index("c")
    lands = []
    for g in range(n):
        own = lax.dynamic_index_in_dim(out[g], me, axis=0, keepdims=True)
        lands.append(lax.dynamic_update_slice(out[n + g], own, (me, 0, 0)))
    return lands


def _pad_rows(flat, rows):
    return jnp.pad(flat, (0, rows * D_MODEL - flat.shape[0])).reshape(rows, D_MODEL)


SMALL_NAMES = ("conv_b_pw1", "conv_w_dw", "conv_b_dw", "conv_ln_g", "conv_ln_b", "conv_b_pw2")


def _pack_small(p):
    flat = jnp.concatenate([p[n].reshape(-1) for n in SMALL_NAMES])
    return _pad_rows(flat, ROWS_SMALL).reshape(1, ROWS_SMALL, D_MODEL)


def _unpack_small(packed):
    flat = packed.reshape(-1)
    c = D_MODEL // N_DEV
    shapes = ((1, 2 * c), (1, CONV_WIDTH, c), (1, c), (1, c), (1, c), (1, c))
    out, o = {}, 0
    for n, shape in zip(SMALL_NAMES, shapes):
        size = shape[-1] * (shape[1] if len(shape) == 3 else 1)
        out[n] = flat[o:o + size].reshape(shape)
        o += size
    return out


def _gather_payloads(p):
    t = lambda a: jnp.swapaxes(a, -1, -2).astype(BF16)
    w1t, w3t, w2 = t(p["ffn_w1"]), t(p["ffn_w3"]), p["ffn_w2"].astype(BF16)
    bits = lax.bitcast_convert_type(_pack_small(p).reshape(-1)[:ROWS_SMALL * D_MODEL // 2], jnp.uint32)
    halves = [(bits >> 16).astype(jnp.uint16), (bits & 0xFFFF).astype(jnp.uint16)]
    small = lax.bitcast_convert_type(jnp.concatenate(halves), BF16).reshape(ROWS_SMALL, D_MODEL)
    conv = jnp.concatenate([t(p["conv_w_pw1"][0]), p["conv_w_pw2"][0].astype(BF16), small], axis=0)
    ffn = [jnp.concatenate([w1t[l], w3t[l], w2[l]], axis=0) for l in range(2)]
    return [t(p["attn_w_qkv"][0]), p["attn_w_o"][0].astype(BF16), ffn[0], conv, ffn[1]]


def _device_rows(land, lo, n):
    return land[:, lo:lo + n].reshape(N_DEV * n, D_MODEL)


def _unpack_conv(land):
    words = lax.bitcast_convert_type(land[:, ROWS_PW1 + ROWS_PW2:], jnp.uint16).astype(jnp.uint32)
    words = words.reshape(N_DEV, 2, ROWS_SMALL * D_MODEL // 2)
    small = lax.bitcast_convert_type((words[:, 0] << 16) | words[:, 1], F32)
    c = D_MODEL // N_DEV
    b_pw1 = small[:, :2 * c].reshape(1, 2 * D_MODEL)
    s = 2 * c
    w_dw = small[:, s:s + CONV_WIDTH * c].reshape(N_DEV, CONV_WIDTH, c).transpose(1, 0, 2).reshape(CONV_WIDTH, D_MODEL)
    s += CONV_WIDTH * c
    b_dw, ln_g, ln_b, b_pw2 = (small[:, s + i * c:s + (i + 1) * c].reshape(1, D_MODEL) for i in range(4))
    return dict(w_pw1_t=_device_rows(land, 0, ROWS_PW1), w_pw2=_device_rows(land, ROWS_PW1, ROWS_PW2),
                b_pw1=b_pw1, w_dw=w_dw, b_dw=b_dw, ln_g=ln_g, ln_b=ln_b, b_pw2=b_pw2)


def _dest_blocks(mats):
    return jnp.concatenate([a.reshape(N_DEV, -1, D_MODEL) for a in mats], axis=1)


def _small_grad_rows(g_bpw1, g_dw, g_bdw, g_lng, g_lnb, g_bpw2):
    c = D_MODEL // N_DEV
    small = jnp.concatenate(
        [g_bpw1.reshape(N_DEV, 2 * c), g_dw.reshape(CONV_WIDTH, N_DEV, c).transpose(1, 0, 2).reshape(N_DEV, -1),
         g_bdw.reshape(N_DEV, c), g_lng.reshape(N_DEV, c), g_lnb.reshape(N_DEV, c), g_bpw2.reshape(N_DEV, c)], axis=1)
    small = jnp.pad(small, ((0, 0), (0, ROWS_SMALL * D_MODEL - SMALL_USED)))
    return small.reshape(N_DEV * ROWS_SMALL, D_MODEL).astype(BF16)


LOSS_ROW = 9


def _pack_replicated(norm_mix, norm_ffn, b_qkv, sinks, b_o, norm_final, extra=None):
    rows = [norm_mix.reshape(2, D_MODEL), norm_ffn.reshape(2, D_MODEL), _pad_rows(b_qkv.reshape(-1), 2),
            _pad_rows(sinks.reshape(-1), 1), b_o.reshape(1, D_MODEL), norm_final.reshape(1, D_MODEL)]
    if extra is not None:
        rows.append(_pad_rows(extra.reshape(-1), 1))
    p = jnp.concatenate(rows, axis=0)
    return jnp.pad(p, ((0, REPL_ROWS - p.shape[0]), (0, 0)))


def _unpack_replicated(p):
    return dict(norm_mix=p[0:2], norm_ffn=p[2:4], attn_b_qkv=p[4:6].reshape(-1)[:QKV_DIM].reshape(1, QKV_DIM),
                attn_sinks=p[6, :N_Q_HEADS].reshape(1, N_Q_HEADS), attn_b_o=p[7:8], norm_final=p[8])


WEIGHT_ORDER = ['norm_mix', 'norm_ffn', 'attn_w_qkv', 'attn_b_qkv', 'attn_sinks', 'attn_w_o', 'attn_b_o',
                'conv_w_pw1', 'conv_b_pw1', 'conv_w_dw', 'conv_b_dw', 'conv_ln_g', 'conv_ln_b', 'conv_w_pw2',
                'conv_b_pw2', 'ffn_w1', 'ffn_w3', 'ffn_w2', 'norm_final']


def kernel(x, norm_mix, norm_ffn, attn_w_qkv, attn_b_qkv, attn_sinks, attn_w_o, attn_b_o, conv_w_pw1, conv_b_pw1, conv_w_dw, conv_b_dw, conv_ln_g, conv_ln_b, conv_w_pw2, conv_b_pw2, ffn_w1, ffn_w3, ffn_w2, norm_final, loss_target, m_norm_mix, m_norm_ffn, m_attn_w_qkv, m_attn_b_qkv, m_attn_sinks, m_attn_w_o, m_attn_b_o, m_conv_w_pw1, m_conv_b_pw1, m_conv_w_dw, m_conv_b_dw, m_conv_ln_g, m_conv_ln_b, m_conv_w_pw2, m_conv_b_pw2, m_ffn_w1, m_ffn_w3, m_ffn_w2, m_norm_final, v_norm_mix, v_norm_ffn, v_attn_w_qkv, v_attn_b_qkv, v_attn_sinks, v_attn_w_o, v_attn_b_o, v_conv_w_pw1, v_conv_b_pw1, v_conv_w_dw, v_conv_b_dw, v_conv_ln_g, v_conv_ln_b, v_conv_w_pw2, v_conv_b_pw2, v_ffn_w1, v_ffn_w3, v_ffn_w2, v_norm_final):
    xs = x[0]
    target = loss_target[0]
    seq = xs.shape[0]

    w = dict(attn_w_qkv=attn_w_qkv, attn_w_o=attn_w_o, conv_w_pw1=conv_w_pw1, conv_b_pw1=conv_b_pw1,
             conv_w_dw=conv_w_dw, conv_b_dw=conv_b_dw, conv_ln_g=conv_ln_g, conv_ln_b=conv_ln_b,
             conv_w_pw2=conv_w_pw2, conv_b_pw2=conv_b_pw2, ffn_w1=ffn_w1, ffn_w3=ffn_w3, ffn_w2=ffn_w2)
    m = dict(attn_w_qkv=m_attn_w_qkv, attn_w_o=m_attn_w_o, conv_w_pw1=m_conv_w_pw1, conv_b_pw1=m_conv_b_pw1,
             conv_w_dw=m_conv_w_dw, conv_b_dw=m_conv_b_dw, conv_ln_g=m_conv_ln_g, conv_ln_b=m_conv_ln_b,
             conv_w_pw2=m_conv_w_pw2, conv_b_pw2=m_conv_b_pw2, ffn_w1=m_ffn_w1, ffn_w3=m_ffn_w3, ffn_w2=m_ffn_w2)
    v = dict(attn_w_qkv=v_attn_w_qkv, attn_w_o=v_attn_w_o, conv_w_pw1=v_conv_w_pw1, conv_b_pw1=v_conv_b_pw1,
             conv_w_dw=v_conv_w_dw, conv_b_dw=v_conv_b_dw, conv_ln_g=v_conv_ln_g, conv_ln_b=v_conv_ln_b,
             conv_w_pw2=v_conv_w_pw2, conv_b_pw2=v_conv_b_pw2, ffn_w1=v_ffn_w1, ffn_w3=v_ffn_w3, ffn_w2=v_ffn_w2)

    payloads = _gather_payloads(w)
    (ag_qkv, ag_wo), tok = _gather_start("ag_start_attn", payloads[:2])
    (ag_ffn0, ag_conv, ag_ffn1), tok = _gather_start(
        "ag_start_rest", [payloads[2] + tok[0, 0].astype(BF16), payloads[3], payloads[4]])
    rc, rsa, rsb = _rope_tables(seq)
    sinks = attn_sinks.reshape(N_Q_HEADS)
    g_mix0, g_mix1 = norm_mix[0:1] + tok[0, 0], norm_mix[1:2]
    g_ffn0, g_ffn1 = norm_ffn[0:1], norm_ffn[1:2]
    g_fin = norm_final.reshape(1, D_MODEL)

    w_qkv_t = _gather_wait("ag_wait_qkv", ag_qkv, tok).reshape(QKV_DIM, D_MODEL)
    y0, q, k, vv = _qkv_fwd(xs, g_mix0, w_qkv_t, attn_b_qkv, rc, rsa, rsb)
    attn, probs, p_sinks = _attn_fwd(sinks, q, k, vv)
    w_o = _gather_wait("ag_wait_wo", ag_wo, attn).reshape(Q_DIM, D_MODEL)
    h1 = _mm_res("attn_out_proj", attn, w_o, attn_b_o, xs)
    w_ffn0 = _gather_wait("ag_wait_ffn0", ag_ffn0, h1)
    f0, u0, p0, s0 = _ffn_up("ffn0_up", h1, g_ffn0, w_ffn0)
    h2 = _ffn_down("ffn0_down", s0, w_ffn0, h1)
    wt = _unpack_conv(_gather_wait("ag_wait_conv", ag_conv, h2))
    wd = jnp.concatenate([wt["w_dw"][::-1], jnp.zeros((TAPS_PAD - CONV_WIDTH, D_MODEL), F32)], axis=0)
    y1, a, dwc, z = _conv_fwd(h2, g_mix1, wt["w_pw1_t"], wt["b_pw1"], wd, wt["b_dw"], wt["ln_g"], wt["ln_b"])
    h3 = _mm_res("conv_out_proj", z, wt["w_pw2"], wt["b_pw2"], h2)
    w_ffn1 = _gather_wait("ag_wait_ffn1", ag_ffn1, h3)
    f1, u1, p1, s1 = _ffn_up("ffn1_up", h3, g_ffn1, w_ffn1)
    dh4, dh4b, sq, dg_fin = _ffn_down_loss(s1, w_ffn1, h3, target, g_fin)

    du1, dp1 = _ffn_bwd_act("ffn1_bwd_act", dh4b, u1, p1, w_ffn1)
    dh3, dh3b, dg_ffn1 = _ffn_bwd_in("ffn1_bwd_in", du1, dp1, w_ffn1, h3, dh4, g_ffn1)
    blocks = _mm_tn("ffn1_dw2", s1, dh4b, tk=FF_TILE, into=(None, W2_SLOT))
    blocks = _mm_tn("ffn1_dw1", du1, f1, tk=FF_TILE, into=(blocks, W1T_SLOT))
    blocks = _mm_tn("ffn1_dw3", dp1, f1, tk=FF_TILE, into=(blocks, W3T_SLOT))
    rs_ffn1, tok = _scatter_start("rs_start_ffn1", blocks)

    da, dlg, dlb, dbdw, dwd, dbpw1, dbpw2 = _conv_bwd(dh3, wt["w_pw2"], dwc, a, wt["ln_g"] + tok[0, 0],
                                                     wt["ln_b"], wd)
    gpw2 = _mm_tn("conv_dw_pw2", z, dh3b, tk=D_MODEL)
    dh2, dh2b, dg_mix1 = _mm_rms_bwd("conv_in_bwd", da, wt["w_pw1_t"], h2, dh3, g_mix1, True)
    gpw1t = _mm_tn("conv_dw_pw1", da, y1, tk=D_MODEL)
    small_rows = _small_grad_rows(dbpw1, dwd[:CONV_WIDTH][::-1], dbdw, dlg, dlb, dbpw2)
    rs_conv, tok = _scatter_start("rs_start_conv", _dest_blocks([gpw1t, gpw2, small_rows]))

    du0, dp0 = _ffn_bwd_act("ffn0_bwd_act", dh2b, u0, p0, w_ffn0)
    dh1, dh1b, dg_ffn0 = _ffn_bwd_in("ffn0_bwd_in", du0, dp0, w_ffn0, h1, dh2, g_ffn0 + tok[0, 0])
    blocks = _mm_tn("ffn0_dw2", s0, dh2b, tk=FF_TILE, into=(None, W2_SLOT))
    blocks = _mm_tn("ffn0_dw1", du0, f0, tk=FF_TILE, into=(blocks, W1T_SLOT))
    blocks = _mm_tn("ffn0_dw3", dp0, f0, tk=FF_TILE, into=(blocks, W3T_SLOT))
    rs_ffn0, tok = _scatter_start("rs_start_ffn0", blocks)

    gwo = _mm_tn("attn_dw_o", attn, dh1b, tk=D_MODEL)
    rs_wo, tok2 = _scatter_start("rs_start_wo", _dest_blocks([gwo]))
    dattn, dbo = _nt_bias("attn_out_bwd", dh1, w_o)
    dq, dkc, dkp, dvc, dvp, dsink = _attn_bwd(probs, p_sinks + (tok[0, 0] + tok2[0, 0]), q, k, vv, dattn)
    dqkv, dbqkv = _rope_bwd(dq, dkc, dkp, dvc, dvp, rc, rsa, rsb)
    gqkvt = _mm_tn("attn_dw_qkv", dqkv, y0, tk=QKV_DIM)
    rs_qkv, tok = _scatter_start("rs_start_qkv", _dest_blocks([gqkvt]))
    dx, dg_mix0 = _mm_rms_bwd("qkv_in_bwd", dqkv, w_qkv_t, xs, dh1, g_mix0 + tok[0, 0], False)

    p_ffn1, p_conv, p_ffn0, p_wo, p_qkv = _scatter_wait([rs_ffn1, rs_conv, rs_ffn0, rs_wo, rs_qkv], dx)

    tr = lambda a: jnp.swapaxes(a, -1, -2)
    same = lambda a: a
    ffn_slots = [p_ffn0, p_ffn1]
    plan = dict(
        ffn_w1=(ffn_slots, W1T_SLOT, tr), ffn_w3=(ffn_slots, W3T_SLOT, tr), ffn_w2=(ffn_slots, W2_SLOT, same),
        attn_w_qkv=([p_qkv], 0, tr), attn_w_o=([p_wo], 0, same),
        conv_w_pw2=([p_conv], ROWS_PW1 // ROWS_PW2, same))
    sharded = [{}, {}, {}, {}]
    for n, (slots, row_block, view) in plan.items():
        outs_n = _adamw_from_slots("adamw_" + n, slots, row_block, view(w[n]), view(m[n]), view(v[n]))
        for dst, t in zip(sharded, outs_n):
            dst[n] = view(t)
    small_out = _adamw_from_slots("adamw_small", [p_conv], (ROWS_PW1 + ROWS_PW2) // ROWS_SMALL,
                                  _pack_small(w), _pack_small(m), _pack_small(v))
    for dst, t in zip(sharded, small_out):
        dst.update(_unpack_small(t))
    g_pw1 = _sum_slots("rs_sum_pw1", p_conv[:, :ROWS_PW1]).T[None]
    pw1_out = _adamw_native("adamw_conv_w_pw1", g_pw1, w["conv_w_pw1"], m["conv_w_pw1"], v["conv_w_pw1"])
    for dst, t in zip(sharded, (g_pw1,) + tuple(pw1_out)):
        dst["conv_w_pw1"] = t

    part = _pack_replicated(jnp.concatenate([dg_mix0, dg_mix1]), jnp.concatenate([dg_ffn0, dg_ffn1]),
                            dbqkv, -dsink[:, 0], dbo, dg_fin, extra=sq[0, 0:1])
    parts = _all_gather("ag_replicated_grads", part)
    w_rep = _pack_replicated(norm_mix, norm_ffn, attn_b_qkv, attn_sinks, attn_b_o, norm_final)
    m_rep = _pack_replicated(m_norm_mix, m_norm_ffn, m_attn_b_qkv, m_attn_sinks, m_attn_b_o, m_norm_final)
    v_rep = _pack_replicated(v_norm_mix, v_norm_ffn, v_attn_b_qkv, v_attn_sinks, v_attn_b_o, v_norm_final)
    rep_out = _adamw_replicated(parts, w_rep, m_rep, v_rep)
    replicated = [_unpack_replicated(t) for t in rep_out]
    loss = rep_out[0][LOSS_ROW, 0] * (0.5 / D_MODEL)

    outs = [loss, dx.reshape(1, seq, D_MODEL)]
    for sh, rp in zip(sharded, replicated):
        merged = {**sh, **rp}
        outs += [merged[n] for n in WEIGHT_ORDER]
    return tuple(outs)
```

```python
import jax
import jax.numpy as jnp
from jax import lax
from jax.experimental import pallas as pl
from jax.experimental.pallas import tpu as pltpu

F32 = jnp.float32
BF16 = jnp.bfloat16

D_MODEL = 1024
HEAD_DIM = 64
N_Q_HEADS = 16
N_KV_HEADS = 2
Q_PER_KV = 8
Q_DIM = N_Q_HEADS * HEAD_DIM
KV_DIM = N_KV_HEADS * HEAD_DIM
QKV_DIM = Q_DIM + 2 * KV_DIM
BLOCK = 128
CONV_WIDTH = 31
D_FF = 2816
ROPE_THETA = 10000.0
RMS_EPS = 1e-5
LN_EPS = 1e-5
ADAM_LR = 0.001
ADAM_B1 = 0.9
ADAM_B2 = 0.999
ADAM_EPS = 1e-08
ADAM_WD = 0.01
ADAM_STEP = 10
N_DEV = 8

LANES = 128
SUBLANES = 8
TOKEN_TILE = 512
CONV_CHUNK = 64
CONV_HALO = 32
TAPS_PAD = 32
VMEM_LIMIT = 56 * 1024 * 1024
NEG_INF = float(jnp.finfo(jnp.float32).min)

ROWS_FF = D_FF // N_DEV
W1T_SLOT, W3T_SLOT, W2_SLOT = 0, 1, 2
ROWS_QKV = QKV_DIM // N_DEV
ROWS_WO = Q_DIM // N_DEV
ROWS_PW1 = 2 * D_MODEL // N_DEV
ROWS_PW2 = D_MODEL // N_DEV
ROWS_SMALL = 16
SMALL_USED = 2 * D_MODEL // N_DEV + CONV_WIDTH * (D_MODEL // N_DEV) + 4 * (D_MODEL // N_DEV)
FF_SPLIT = 2
FF_TILE_DEVS = N_DEV // FF_SPLIT
FF_TILE = FF_TILE_DEVS * ROWS_FF
REPL_ROWS = 16


def _call(body, *, name, grid, in_specs, out_specs, out_shape, scratch=(), sem=None, aliases=None):
    return pl.pallas_call(
        body, name=name, grid=grid, in_specs=in_specs, out_specs=out_specs, out_shape=out_shape,
        scratch_shapes=list(scratch), input_output_aliases=aliases or {},
        compiler_params=pltpu.CompilerParams(dimension_semantics=sem, vmem_limit_bytes=VMEM_LIMIT))


def _full(shape):
    return pl.BlockSpec(shape, lambda *_: (0,) * len(shape))


def _resident(shape):
    return pl.BlockSpec(shape, lambda *_: (0,) * len(shape), pipeline_mode=pl.Buffered(1))


def _rows(tm, n):
    return pl.BlockSpec((tm, n), lambda i, *_: (i, 0))


def _sig(x):
    return 1.0 / (1.0 + jnp.exp(-x))


def _sum_rows(x):
    return jnp.sum(x, axis=0, keepdims=True)


def _nt(a, b):
    return lax.dot_general(a, b, (((1,), (1,)), ((), ())), preferred_element_type=F32)


def _tn(a, b):
    return lax.dot_general(a, b, (((0,), (0,)), ((), ())), preferred_element_type=F32)


def _rms_stats(x):
    return lax.rsqrt(jnp.mean(x * x, axis=-1, keepdims=True) + RMS_EPS)


def _rms_bwd(dy, x, g, dres):
    r = _rms_stats(x)
    n = x * r
    dn = dy * g
    dx = dres + r * (dn - n * jnp.mean(dn * n, axis=-1, keepdims=True))
    return dx, _sum_rows(dy * n)


def _rope_tables(seq):
    half = HEAD_DIM // 2
    pos = jnp.arange(seq, dtype=F32)
    inv_freq = ROPE_THETA ** (-jnp.arange(0, HEAD_DIM, 2, dtype=F32) / HEAD_DIM)
    ang = pos[:, None] * inv_freq[None, :]
    spread = lambda t: jnp.broadcast_to(t[:, None, :], (seq, LANES // half, half)).reshape(seq, LANES)
    cos, sin = spread(jnp.cos(ang)), spread(jnp.sin(ang))
    first_half = (jnp.arange(LANES) % HEAD_DIM < half)[None, :]
    sa = jnp.where(first_half, -sin, 0.0)
    sb = jnp.where(first_half, 0.0, sin)
    return cos, sa, sb


def _rope(t, c, sa, sb):
    half = HEAD_DIM // 2
    return t * c + pltpu.roll(t, LANES - half, 1) * sa + pltpu.roll(t, half, 1) * sb


def _rope_t(dt, c, sa, sb):
    half = HEAD_DIM // 2
    return dt * c + pltpu.roll(dt * sa, half, 1) + pltpu.roll(dt * sb, LANES - half, 1)


def _rms_fwd(x, g):
    seq = x.shape[0]
    tm = min(2 * TOKEN_TILE, seq)

    def body(x_ref, g_ref, y_ref):
        xv = x_ref[...]
        y_ref[...] = (xv * _rms_stats(xv) * g_ref[...]).astype(BF16)

    return _call(
        body, name="rms_fwd", grid=(seq // tm,), in_specs=[_rows(tm, D_MODEL), _full((1, D_MODEL))],
        out_specs=_rows(tm, D_MODEL), out_shape=jax.ShapeDtypeStruct((seq, D_MODEL), BF16),
        sem=("parallel",))(x, g)


def _qkv_fwd(y, w, b, rc, rsa, rsb):
    seq = y.shape[0]
    tm = min(TOKEN_TILE, seq)

    def body(y_ref, w_ref, b_ref, c_ref, sa_ref, sb_ref, q_ref, k_ref, v_ref):
        qkv = _nt(y_ref[...], w_ref[...]) + b_ref[...]
        c, sa, sb = c_ref[...], sa_ref[...], sb_ref[...]
        for i in range(Q_DIM // LANES):
            blk = _rope(qkv[:, i * LANES:(i + 1) * LANES], c, sa, sb)
            q_ref[:, i * LANES:(i + 1) * LANES] = (blk * (HEAD_DIM ** -0.5)).astype(BF16)
        k_ref[...] = _rope(qkv[:, Q_DIM:Q_DIM + KV_DIM], c, sa, sb).astype(BF16)
        v_ref[...] = qkv[:, Q_DIM + KV_DIM:].astype(BF16)

    return _call(
        body, name="qkv_fwd", grid=(seq // tm,),
        in_specs=[_rows(tm, D_MODEL), _resident((QKV_DIM, D_MODEL)), _full((1, QKV_DIM)),
                  _rows(tm, LANES), _rows(tm, LANES), _rows(tm, LANES)],
        out_specs=[_rows(tm, Q_DIM), _rows(tm, KV_DIM), _rows(tm, KV_DIM)],
        out_shape=[jax.ShapeDtypeStruct((seq, Q_DIM), BF16),
                   jax.ShapeDtypeStruct((seq, KV_DIM), BF16), jax.ShapeDtypeStruct((seq, KV_DIM), BF16)],
        sem=("parallel",))(y, w, b, rc, rsa, rsb)


def _band_mask(n):
    row = lax.broadcasted_iota(jnp.int32, (BLOCK, 2 * BLOCK), 0)
    col = lax.broadcasted_iota(jnp.int32, (BLOCK, 2 * BLOCK), 1)
    rel = row + BLOCK - col
    return (rel >= 0) & (rel < BLOCK) & ((col >= BLOCK) | (n > 0))


def _softmax_with_sink(s, mask, sink):
    s = jnp.where(mask, s, NEG_INF)
    m = jnp.maximum(jnp.max(s, axis=-1, keepdims=True), sink)
    p = jnp.exp(s - m)
    e_sink = jnp.exp(sink - m)
    inv = 1.0 / (jnp.sum(p, axis=-1, keepdims=True) + e_sink)
    return p * inv, e_sink * inv


PAIRS_PER_KV = Q_PER_KV // 2


def _kv_specs():
    cur = pl.BlockSpec((BLOCK, KV_DIM), lambda n: (n, 0))
    prev = pl.BlockSpec((BLOCK, KV_DIM), lambda n: (jnp.maximum(n - 1, 0), 0))
    return cur, prev


def _low_lanes():
    return lax.broadcasted_iota(jnp.int32, (2 * BLOCK, KV_DIM), 1) < HEAD_DIM


def _kv_low_high(prev, cur, j, low):
    both = jnp.concatenate([prev, cur], axis=0).astype(F32)
    swapped = pltpu.roll(both, HEAD_DIM, 1)
    at_low, at_high = (both, swapped) if j == 0 else (swapped, both)
    return jnp.where(low, at_low, 0.0).astype(BF16), jnp.where(low, 0.0, at_high).astype(BF16)


def _fold_pair_halves(acc, j, low):
    folded = acc + pltpu.roll(acc, HEAD_DIM, 1)
    return jnp.where(low, folded, 0.0) if j == 0 else jnp.where(low, 0.0, folded)


def _pair_lanes(j, i):
    g = j * PAIRS_PER_KV + i
    return slice(g * LANES, (g + 1) * LANES), 2 * g


def _attn_fwd(sinks, q, k, v):
    seq = q.shape[0]
    cur, prev = _kv_specs()

    def body(sink_ref, q_ref, kc_ref, kp_ref, vc_ref, vp_ref, o_ref, p_ref, ps_ref):
        mask = _band_mask(pl.program_id(0))
        lane = lax.broadcasted_iota(jnp.int32, (BLOCK, LANES), 1)
        p_sinks = jnp.zeros((BLOCK, LANES), F32)
        for j in range(N_KV_HEADS):
            cs = slice(j * HEAD_DIM, (j + 1) * HEAD_DIM)
            kk = jnp.concatenate([kp_ref[:, cs], kc_ref[:, cs]], axis=0)
            vv = jnp.concatenate([vp_ref[:, cs], vc_ref[:, cs]], axis=0)
            for gq in range(Q_PER_KV):
                h = j * Q_PER_KV + gq
                hs = slice(h * HEAD_DIM, (h + 1) * HEAD_DIM)
                probs, p_sink = _softmax_with_sink(_nt(q_ref[:, hs], kk), mask, sink_ref[h])
                pb = probs.astype(BF16)
                p_ref[h] = pb
                p_sinks = jnp.where(lane == h, p_sink, p_sinks)
                o_ref[:, hs] = jnp.dot(pb, vv, preferred_element_type=F32).astype(BF16)
        ps_ref[...] = p_sinks

    return _call(
        body, name="attn_fwd", grid=(seq // BLOCK,),
        in_specs=[pl.BlockSpec(memory_space=pltpu.SMEM), _rows(BLOCK, Q_DIM), cur, prev, cur, prev],
        out_specs=[_rows(BLOCK, Q_DIM), pl.BlockSpec((N_Q_HEADS, BLOCK, 2 * BLOCK), lambda n: (0, n, 0)),
                   _rows(BLOCK, LANES)],
        out_shape=[jax.ShapeDtypeStruct((seq, Q_DIM), BF16),
                   jax.ShapeDtypeStruct((N_Q_HEADS, seq, 2 * BLOCK), BF16),
                   jax.ShapeDtypeStruct((seq, LANES), F32)],
        sem=("parallel",))(sinks, q, k, k, v, v)


def _mm_res(name, a, w, b, res):
    seq, kdim = a.shape
    n = w.shape[1]
    tm = min(2 * TOKEN_TILE, seq)

    def body(a_ref, w_ref, b_ref, r_ref, o_ref):
        o_ref[...] = r_ref[...] + (jnp.dot(a_ref[...], w_ref[...], preferred_element_type=F32) + b_ref[...])

    return _call(
        body, name=name, grid=(seq // tm,),
        in_specs=[_rows(tm, kdim), _resident((kdim, n)), _full((1, n)), _rows(tm, n)],
        out_specs=_rows(tm, n), out_shape=jax.ShapeDtypeStruct((seq, n), F32),
        sem=("parallel",))(a, w, b, res)


def _ff_tile_spec(slot):
    return pl.BlockSpec((FF_TILE_DEVS, ROWS_FF, D_MODEL), lambda j, i: (j, slot, 0))


def _ff_whole_spec(slot):
    return pl.BlockSpec((N_DEV, ROWS_FF, D_MODEL), lambda i: (0, slot, 0), pipeline_mode=pl.Buffered(1))


def _ffn_up(name, h, g, gathered):
    seq = h.shape[0]
    tm = min(TOKEN_TILE, seq)

    def body(h_ref, g_ref, w1_ref, w3_ref, f_ref, u_ref, w_ref, s_ref):
        hv = h_ref[...]
        f = (hv * _rms_stats(hv) * g_ref[...]).astype(BF16)
        f_ref[...] = f
        u = _nt(f, w1_ref[...].reshape(FF_TILE, D_MODEL))
        w = _nt(f, w3_ref[...].reshape(FF_TILE, D_MODEL))
        u_ref[...] = u.astype(BF16)
        w_ref[...] = w.astype(BF16)
        s_ref[...] = (u * _sig(u) * w).astype(BF16)

    n_t = seq // tm
    tile_ff = pl.BlockSpec((tm, FF_TILE), lambda j, i: (i, j))
    f_spec = pl.BlockSpec((tm, D_MODEL), lambda j, i: (jnp.where(j == 0, i, n_t), 0))
    ff_shape = jax.ShapeDtypeStruct((seq, D_FF), BF16)
    return _call(
        body, name=name, grid=(FF_SPLIT, n_t),
        in_specs=[pl.BlockSpec((tm, D_MODEL), lambda j, i: (i, 0)), _full((1, D_MODEL)),
                  _ff_tile_spec(W1T_SLOT), _ff_tile_spec(W3T_SLOT)],
        out_specs=[f_spec, tile_ff, tile_ff, tile_ff],
        out_shape=[jax.ShapeDtypeStruct((seq + tm, D_MODEL), BF16), ff_shape, ff_shape, ff_shape],
        sem=("arbitrary", "arbitrary"))(h, g, gathered, gathered)


def _ffn_down(name, s, gathered, res):
    seq = s.shape[0]
    tm = min(2 * TOKEN_TILE, seq)

    def body(s_ref, w_ref, r_ref, o_ref):
        w2 = w_ref[...].reshape(D_FF, D_MODEL)
        o_ref[...] = r_ref[...] + jnp.dot(s_ref[...], w2, preferred_element_type=F32)

    return _call(
        body, name=name, grid=(seq // tm,),
        in_specs=[_rows(tm, D_FF), _ff_whole_spec(W2_SLOT), _rows(tm, D_MODEL)],
        out_specs=_rows(tm, D_MODEL), out_shape=jax.ShapeDtypeStruct((seq, D_MODEL), F32),
        sem=("parallel",))(s, gathered, res)


def _conv_fwd(h, g, wpw1, bpw1, wd, bdw, lng, lnb):
    seq = h.shape[0]
    tm = min(TOKEN_TILE, seq)
    n_chunks = tm // CONV_CHUNK
    win = CONV_CHUNK + CONV_HALO

    def body(h_ref, g_ref, w_ref, b_ref, wd_ref, bdw_ref, lng_ref, lnb_ref,
             y_ref, a_ref, dwc_ref, z_ref, gbuf):
        i = pl.program_id(0)

        @pl.when(i == 0)
        def _():
            gbuf[0:CONV_HALO, :] = jnp.zeros((CONV_HALO, D_MODEL), F32)

        @pl.when(i > 0)
        def _():
            gbuf[0:CONV_HALO, :] = gbuf[tm:tm + CONV_HALO, :]

        hv = h_ref[...]
        y = (hv * _rms_stats(hv) * g_ref[...]).astype(BF16)
        y_ref[...] = y
        a = _nt(y, w_ref[...]) + b_ref[...]
        a_ref[...] = a.astype(BF16)
        gbuf[CONV_HALO:CONV_HALO + tm, :] = a[:, :D_MODEL] * _sig(a[:, D_MODEL:])

        def chunk(c, carry):
            r0 = pl.multiple_of(c * CONV_CHUNK, CONV_CHUNK)
            for l in range(D_MODEL // LANES):
                ls = slice(l * LANES, (l + 1) * LANES)
                gw = gbuf[pl.ds(r0, win), ls]
                acc = jnp.zeros((CONV_CHUNK, LANES), F32) + bdw_ref[:, ls]
                for s in range(SUBLANES):
                    gs = gw if s == 0 else pltpu.roll(gw, s, 0)
                    for q in range(CONV_HALO // SUBLANES):
                        d = SUBLANES * q + s
                        if d < CONV_WIDTH:
                            lo = CONV_HALO - SUBLANES * q
                            acc = acc + wd_ref[d:d + 1, ls] * gs[lo:lo + CONV_CHUNK]
                dwc_ref[pl.ds(r0, CONV_CHUNK), ls] = acc
            return carry

        lax.fori_loop(0, n_chunks, chunk, 0)

        xv = dwc_ref[...]
        mu = jnp.mean(xv, axis=-1, keepdims=True)
        xc = xv - mu
        var = jnp.mean(xc * xc, axis=-1, keepdims=True)
        ln = xc * lax.rsqrt(var + LN_EPS) * lng_ref[...] + lnb_ref[...]
        z_ref[...] = (ln * _sig(ln)).astype(BF16)

    return _call(
        body, name="conv_fwd", grid=(seq // tm,),
        in_specs=[_rows(tm, D_MODEL), _full((1, D_MODEL)), _full((2 * D_MODEL, D_MODEL)), _full((1, 2 * D_MODEL)),
                  _full((TAPS_PAD, D_MODEL)), _full((1, D_MODEL)), _full((1, D_MODEL)), _full((1, D_MODEL))],
        out_specs=[_rows(tm, D_MODEL), _rows(tm, 2 * D_MODEL), _rows(tm, D_MODEL), _rows(tm, D_MODEL)],
        out_shape=[jax.ShapeDtypeStruct((seq, D_MODEL), BF16), jax.ShapeDtypeStruct((seq, 2 * D_MODEL), BF16),
                   jax.ShapeDtypeStruct((seq, D_MODEL), F32), jax.ShapeDtypeStruct((seq, D_MODEL), BF16)],
        scratch=[pltpu.VMEM((tm + CONV_HALO, D_MODEL), F32)],
        sem=("arbitrary",))(h, g, wpw1, bpw1, wd, bdw, lng, lnb)


def _ffn_down_loss(s, gathered, res, target, g):
    seq = s.shape[0]
    tm = min(TOKEN_TILE, seq)

    def body(s_ref, w_ref, r_ref, t_ref, g_ref, dh_ref, dhb_ref, loss_ref, dg_ref):
        @pl.when(pl.program_id(0) == 0)
        def _():
            loss_ref[...] = jnp.zeros_like(loss_ref)
            dg_ref[...] = jnp.zeros_like(dg_ref)

        hv = r_ref[...] + jnp.dot(s_ref[...], w_ref[...].reshape(D_FF, D_MODEL), preferred_element_type=F32)
        gv = g_ref[...]
        err = hv * _rms_stats(hv) * gv - t_ref[...]
        sq = jnp.sum(jnp.sum(err * err, axis=-1, keepdims=True), axis=0, keepdims=True)
        loss_ref[...] += jnp.broadcast_to(sq, loss_ref.shape)
        dx, dg = _rms_bwd(err * (1.0 / D_MODEL), hv, gv, 0.0)
        dh_ref[...] = dx
        dhb_ref[...] = dx.astype(BF16)
        dg_ref[...] += dg

    return _call(
        body, name="ffn1_down_loss", grid=(seq // tm,),
        in_specs=[_rows(tm, D_FF), _ff_whole_spec(W2_SLOT), _rows(tm, D_MODEL), _rows(tm, D_MODEL),
                  _full((1, D_MODEL))],
        out_specs=[_rows(tm, D_MODEL), _rows(tm, D_MODEL), _full((SUBLANES, LANES)), _full((1, D_MODEL))],
        out_shape=[jax.ShapeDtypeStruct((seq, D_MODEL), F32), jax.ShapeDtypeStruct((seq, D_MODEL), BF16),
                   jax.ShapeDtypeStruct((SUBLANES, LANES), F32), jax.ShapeDtypeStruct((1, D_MODEL), F32)],
        sem=("arbitrary",))(s, gathered, res, target, g)


def _ffn_bwd_act(name, dh, u, w, gathered):
    seq = dh.shape[0]
    tm = min(TOKEN_TILE, seq)

    def body(dh_ref, u_ref, w_ref, w2_ref, du_ref, dw_ref):
        ds = _nt(dh_ref[...], w2_ref[...].reshape(FF_TILE, D_MODEL))
        uv = u_ref[...].astype(F32)
        sg = _sig(uv)
        dw_ref[...] = (ds * (uv * sg)).astype(BF16)
        du_ref[...] = (ds * w_ref[...].astype(F32) * (sg * (1.0 + uv * (1.0 - sg)))).astype(BF16)

    tile_ff = pl.BlockSpec((tm, FF_TILE), lambda j, i: (i, j))
    ff_shape = jax.ShapeDtypeStruct((seq, D_FF), BF16)
    return _call(
        body, name=name, grid=(FF_SPLIT, seq // tm),
        in_specs=[pl.BlockSpec((tm, D_MODEL), lambda j, i: (i, 0)), tile_ff, tile_ff, _ff_tile_spec(W2_SLOT)],
        out_specs=[tile_ff, tile_ff], out_shape=[ff_shape, ff_shape],
        sem=("parallel", "parallel"))(dh, u, w, gathered)


def _ffn_bwd_in(name, du, dw, gathered, h_in, dres, g):
    seq = du.shape[0]
    tm = min(TOKEN_TILE, seq)

    def body(du_ref, dw_ref, w1_ref, w3_ref, h_ref, dr_ref, g_ref, dx_ref, dxb_ref, dg_ref):
        @pl.when(pl.program_id(0) == 0)
        def _():
            dg_ref[...] = jnp.zeros_like(dg_ref)

        df = jnp.dot(du_ref[...], w1_ref[...].reshape(D_FF, D_MODEL), preferred_element_type=F32)
        df = df + jnp.dot(dw_ref[...], w3_ref[...].reshape(D_FF, D_MODEL), preferred_element_type=F32)
        dx, dg = _rms_bwd(df, h_ref[...], g_ref[...], dr_ref[...])
        dx_ref[...] = dx
        dxb_ref[...] = dx.astype(BF16)
        dg_ref[...] += dg

    return _call(
        body, name=name, grid=(seq // tm,),
        in_specs=[_rows(tm, D_FF), _rows(tm, D_FF), _ff_whole_spec(W1T_SLOT), _ff_whole_spec(W3T_SLOT),
                  _rows(tm, D_MODEL), _rows(tm, D_MODEL), _full((1, D_MODEL))],
        out_specs=[_rows(tm, D_MODEL), _rows(tm, D_MODEL), _full((1, D_MODEL))],
        out_shape=[jax.ShapeDtypeStruct((seq, D_MODEL), F32), jax.ShapeDtypeStruct((seq, D_MODEL), BF16),
                   jax.ShapeDtypeStruct((1, D_MODEL), F32)],
        sem=("arbitrary",))(du, dw, gathered, gathered, h_in, dres, g)


def _mm_tn(name, a, b, *, tk, into=None):
    seq, kdim = a.shape
    n = b.shape[1]
    tt = min((4 if b.dtype == BF16 else 2) * TOKEN_TILE, seq)
    n_t = seq // tt
    devs = tk // ROWS_FF

    def body(a_ref, b_ref, *rest):
        o_ref, acc = rest[-2:]
        t = pl.program_id(1)

        @pl.when(t == 0)
        def _():
            acc[...] = jnp.zeros_like(acc)

        acc[...] += _tn(a_ref[...].astype(BF16), b_ref[...].astype(BF16))

        @pl.when(t == n_t - 1)
        def _():
            out = acc[...].astype(BF16)
            o_ref[...] = out if into is None else out.reshape(devs, ROWS_FF, n)

    in_specs = [pl.BlockSpec((tt, tk), lambda k, t: (t, k)), pl.BlockSpec((tt, n), lambda k, t: (t, 0))]
    args = [a, b]
    aliases = None
    if into is None:
        out_spec = pl.BlockSpec((tk, n), lambda k, t: (k, 0))
        out_shape = jax.ShapeDtypeStruct((kdim, n), BF16)
    else:
        blocks, slot = into
        out_spec = pl.BlockSpec((devs, ROWS_FF, n), lambda k, t: (k, slot, 0))
        out_shape = jax.ShapeDtypeStruct((N_DEV, 3 * ROWS_FF, n), BF16)
        if blocks is not None:
            in_specs.append(_ANY)
            args.append(blocks)
            aliases = {2: 0}
    return _call(
        body, name=name, grid=(kdim // tk, n_t), in_specs=in_specs, out_specs=out_spec, out_shape=out_shape,
        scratch=[pltpu.VMEM((tk, n), F32)], sem=("parallel", "arbitrary"), aliases=aliases)(*args)


def _conv_bwd(dh, wpw2, dwc, a, lng, lnb, wd):
    seq = dh.shape[0]
    tm = min(TOKEN_TILE, seq)
    nt = seq // tm
    n_chunks = tm // CONV_CHUNK
    win = CONV_CHUNK + CONV_HALO
    halo_per_tile = tm // CONV_HALO

    def body(dh_ref, w_ref, dwc_ref, a_ref, ah_ref, lng_ref, lnb_ref, wd_ref,
             da_ref, dlg_ref, dlb_ref, dbdw_ref, dwd_ref, dbpw1_ref, dbpw2_ref,
             gbuf, dbuf, dglu, dwd_part):
        i = pl.program_id(0)
        r = nt - 1 - i

        @pl.when(i == 0)
        def _():
            dlg_ref[...] = jnp.zeros_like(dlg_ref)
            dlb_ref[...] = jnp.zeros_like(dlb_ref)
            dbdw_ref[...] = jnp.zeros_like(dbdw_ref)
            dbpw1_ref[...] = jnp.zeros_like(dbpw1_ref)
            dbpw2_ref[...] = jnp.zeros_like(dbpw2_ref)
            dwd_part[...] = jnp.zeros_like(dwd_part)
            dbuf[tm:tm + CONV_HALO, :] = jnp.zeros((CONV_HALO, D_MODEL), F32)

        @pl.when(i > 0)
        def _():
            dbuf[tm:tm + CONV_HALO, :] = dbuf[0:CONV_HALO, :]

        dhv = dh_ref[...]
        dbpw2_ref[...] += _sum_rows(dhv)
        dz = _nt(dhv.astype(BF16), w_ref[...])
        xv = dwc_ref[...]
        lg = lng_ref[...]
        mu = jnp.mean(xv, axis=-1, keepdims=True)
        xc = xv - mu
        rstd = lax.rsqrt(jnp.mean(xc * xc, axis=-1, keepdims=True) + LN_EPS)
        xhat = xc * rstd
        ln = xhat * lg + lnb_ref[...]
        sg = _sig(ln)
        dln = dz * (sg * (1.0 + ln * (1.0 - sg)))
        dlg_ref[...] += _sum_rows(dln * xhat)
        dlb_ref[...] += _sum_rows(dln)
        dxh = dln * lg
        ddw = rstd * (dxh - jnp.mean(dxh, axis=-1, keepdims=True)
                      - xhat * jnp.mean(dxh * xhat, axis=-1, keepdims=True))
        dbdw_ref[...] += _sum_rows(ddw)
        dbuf[0:tm, :] = ddw

        av = a_ref[...].astype(F32)
        a1 = av[:, :D_MODEL]
        s2 = _sig(av[:, D_MODEL:])
        gbuf[CONV_HALO:CONV_HALO + tm, :] = a1 * s2
        ah = ah_ref[...].astype(F32)
        gh = ah[:, :D_MODEL] * _sig(ah[:, D_MODEL:])
        gbuf[0:CONV_HALO, :] = jnp.where(r > 0, gh, 0.0)

        def chunk(c, carry):
            r0 = pl.multiple_of(c * CONV_CHUNK, CONV_CHUNK)
            for l in range(D_MODEL // LANES):
                ls = slice(l * LANES, (l + 1) * LANES)
                dw_ = dbuf[pl.ds(r0, win), ls]
                gw = gbuf[pl.ds(r0, win), ls]
                acc = jnp.zeros((CONV_CHUNK, LANES), F32)
                for s in range(SUBLANES):
                    ds_ = dw_ if s == 0 else pltpu.roll(dw_, win - s, 0)
                    for q in range(CONV_HALO // SUBLANES):
                        d = SUBLANES * q + s
                        if d < CONV_WIDTH:
                            acc = acc + wd_ref[d:d + 1, ls] * ds_[SUBLANES * q:SUBLANES * q + CONV_CHUNK]
                            lo = CONV_HALO - SUBLANES * q
                            prod = ds_[0:CONV_CHUNK] * gw[lo:lo + CONV_CHUNK]
                            dwd_part[d, :, ls] += jnp.sum(
                                prod.reshape(CONV_CHUNK // SUBLANES, SUBLANES, LANES), axis=0)
                dglu[pl.ds(r0, CONV_CHUNK), ls] = acc
            return carry

        lax.fori_loop(0, n_chunks, chunk, 0)

        dg_ = dglu[...]
        da1 = dg_ * s2
        da2 = dg_ * a1 * s2 * (1.0 - s2)
        da_ref[:, :D_MODEL] = da1.astype(BF16)
        da_ref[:, D_MODEL:] = da2.astype(BF16)
        dbpw1_ref[:, :D_MODEL] += _sum_rows(da1)
        dbpw1_ref[:, D_MODEL:] += _sum_rows(da2)

        @pl.when(i == nt - 1)
        def _():
            dwd_ref[...] = jnp.sum(dwd_part[...], axis=1)

    rev = lambda n: pl.BlockSpec((tm, n), lambda i: (nt - 1 - i, 0))
    halo = pl.BlockSpec((CONV_HALO, 2 * D_MODEL),
                        lambda i: (jnp.maximum((nt - 1 - i) * halo_per_tile - 1, 0), 0))
    vec = lambda n: _full((1, n))
    return _call(
        body, name="conv_bwd", grid=(nt,),
        in_specs=[rev(D_MODEL), _full((D_MODEL, D_MODEL)), rev(D_MODEL), rev(2 * D_MODEL), halo,
                  vec(D_MODEL), vec(D_MODEL), _full((TAPS_PAD, D_MODEL))],
        out_specs=[rev(2 * D_MODEL), vec(D_MODEL), vec(D_MODEL), vec(D_MODEL), _full((TAPS_PAD, D_MODEL)),
                   vec(2 * D_MODEL), vec(D_MODEL)],
        out_shape=[jax.ShapeDtypeStruct((seq, 2 * D_MODEL), BF16), jax.ShapeDtypeStruct((1, D_MODEL), F32),
                   jax.ShapeDtypeStruct((1, D_MODEL), F32), jax.ShapeDtypeStruct((1, D_MODEL), F32),
                   jax.ShapeDtypeStruct((TAPS_PAD, D_MODEL), F32), jax.ShapeDtypeStruct((1, 2 * D_MODEL), F32),
                   jax.ShapeDtypeStruct((1, D_MODEL), F32)],
        scratch=[pltpu.VMEM((tm + CONV_HALO, D_MODEL), F32), pltpu.VMEM((tm + CONV_HALO, D_MODEL), F32),
                 pltpu.VMEM((tm, D_MODEL), F32), pltpu.VMEM((TAPS_PAD, SUBLANES, D_MODEL), F32)],
        sem=("arbitrary",))(dh, wpw2, dwc, a, a, lng, lnb, wd)


def _mm_rms_bwd(name, dact, wt, h_in, dres, g, bf16_copy):
    seq, n = dact.shape
    tm = min(TOKEN_TILE, seq)

    def body(da_ref, w_ref, h_ref, dr_ref, g_ref, dx_ref, *rest):
        dg_ref = rest[-1]

        @pl.when(pl.program_id(0) == 0)
        def _():
            dg_ref[...] = jnp.zeros_like(dg_ref)

        dy = jnp.dot(da_ref[...], w_ref[...], preferred_element_type=F32)
        dx, dg = _rms_bwd(dy, h_ref[...], g_ref[...], dr_ref[...])
        dx_ref[...] = dx
        if bf16_copy:
            rest[0][...] = dx.astype(BF16)
        dg_ref[...] += dg

    copy_spec = [_rows(tm, D_MODEL)] if bf16_copy else []
    copy_shape = [jax.ShapeDtypeStruct((seq, D_MODEL), BF16)] if bf16_copy else []
    return _call(
        body, name=name, grid=(seq // tm,),
        in_specs=[_rows(tm, n), _resident((n, D_MODEL)), _rows(tm, D_MODEL), _rows(tm, D_MODEL), _full((1, D_MODEL))],
        out_specs=[_rows(tm, D_MODEL), *copy_spec, _full((1, D_MODEL))],
        out_shape=[jax.ShapeDtypeStruct((seq, D_MODEL), F32), *copy_shape, jax.ShapeDtypeStruct((1, D_MODEL), F32)],
        sem=("arbitrary",))(dact, wt, h_in, dres, g)


def _nt_bias(name, dy, w):
    seq, n = dy.shape
    kdim = w.shape[0]
    tm = min(2 * TOKEN_TILE, seq)

    def body(dy_ref, w_ref, o_ref, db_ref):
        @pl.when(pl.program_id(0) == 0)
        def _():
            db_ref[...] = jnp.zeros_like(db_ref)

        dyv = dy_ref[...]
        db_ref[...] += _sum_rows(dyv)
        o_ref[...] = _nt(dyv.astype(BF16), w_ref[...]).astype(BF16)

    return _call(
        body, name=name, grid=(seq // tm,),
        in_specs=[_rows(tm, n), _resident((kdim, n))],
        out_specs=[_rows(tm, kdim), _full((1, n))],
        out_shape=[jax.ShapeDtypeStruct((seq, kdim), BF16), jax.ShapeDtypeStruct((1, n), F32)],
        sem=("arbitrary",))(dy, w)


def _attn_bwd(probs, p_sinks, q, k, v, do):
    seq = q.shape[0]
    per = 2 if seq % (2 * BLOCK) == 0 else 1
    tile = per * BLOCK
    cur = pl.BlockSpec((tile, KV_DIM), lambda n: (n, 0))
    prev = pl.BlockSpec((BLOCK, KV_DIM), lambda n: (jnp.maximum(n * per - 1, 0), 0))

    def body(p_ref, ps_ref, q_ref, kc_ref, kp_ref, vc_ref, vp_ref, do_ref,
             dq_ref, dkc_ref, dkp_ref, dvc_ref, dvp_ref, dsink_ref):
        @pl.when(pl.program_id(0) == 0)
        def _():
            dsink_ref[...] = jnp.zeros_like(dsink_ref)

        low = _low_lanes()
        lane = lax.broadcasted_iota(jnp.int32, (BLOCK, LANES), 1)
        for b in range(per):
            rows = slice(b * BLOCK, (b + 1) * BLOCK)
            before = slice((b - 1) * BLOCK, b * BLOCK)
            k_prev, v_prev = (kp_ref[...], vp_ref[...]) if b == 0 else (kc_ref[before, :], vc_ref[before, :])
            k_cur, v_cur = kc_ref[rows, :], vc_ref[rows, :]
            p_sinks_blk = ps_ref[rows, :]
            dk_all = jnp.zeros((2 * BLOCK, KV_DIM), F32)
            dv_all = jnp.zeros((2 * BLOCK, KV_DIM), F32)
            for j in range(N_KV_HEADS):
                k_lo, k_hi = _kv_low_high(k_prev, k_cur, j, low)
                v_lo, v_hi = _kv_low_high(v_prev, v_cur, j, low)
                dk_acc = jnp.zeros((2 * BLOCK, KV_DIM), F32)
                dv_acc = jnp.zeros((2 * BLOCK, KV_DIM), F32)
                for i in range(PAIRS_PER_KV):
                    ls, h = _pair_lanes(j, i)
                    qp = q_ref[rows, ls]
                    dop = do_ref[rows, ls]
                    dsb, pb16 = [], []
                    for t, v_sel in enumerate((v_lo, v_hi)):
                        pb = p_ref[h + t, rows, :]
                        pf = pb.astype(F32)
                        p_sink = jnp.sum(jnp.where(lane == h + t, p_sinks_blk, 0.0), axis=-1, keepdims=True)
                        dp = _nt(dop, v_sel)
                        delta = jnp.sum(pf * dp, axis=-1, keepdims=True)
                        dsb.append((pf * (dp - delta)).astype(BF16))
                        pb16.append(pb)
                        dsink_ref[h + t:h + t + 1, :] += jnp.broadcast_to(_sum_rows(p_sink * delta), (1, LANES))
                    dq = (jnp.dot(dsb[0], k_lo, preferred_element_type=F32)
                          + jnp.dot(dsb[1], k_hi, preferred_element_type=F32))
                    dq_ref[rows, ls] = dq * (HEAD_DIM ** -0.5)
                    dk_acc = dk_acc + jnp.where(low, _tn(dsb[0], qp), _tn(dsb[1], qp))
                    dv_acc = dv_acc + jnp.where(low, _tn(pb16[0], dop), _tn(pb16[1], dop))
                dk_all = dk_all + _fold_pair_halves(dk_acc, j, low)
                dv_all = dv_all + _fold_pair_halves(dv_acc, j, low)
            dkp_ref[rows, :] = dk_all[:BLOCK]
            dkc_ref[rows, :] = dk_all[BLOCK:]
            dvp_ref[rows, :] = dv_all[:BLOCK]
            dvc_ref[rows, :] = dv_all[BLOCK:]

    kv_out = _rows(tile, KV_DIM)
    kv_shape = jax.ShapeDtypeStruct((seq, KV_DIM), F32)
    return _call(
        body, name="attn_bwd", grid=(seq // tile,),
        in_specs=[pl.BlockSpec((N_Q_HEADS, tile, 2 * BLOCK), lambda n: (0, n, 0)), _rows(tile, LANES),
                  _rows(tile, Q_DIM), cur, prev, cur, prev, _rows(tile, Q_DIM)],
        out_specs=[_rows(tile, Q_DIM), kv_out, kv_out, kv_out, kv_out, _full((N_Q_HEADS, LANES))],
        out_shape=[jax.ShapeDtypeStruct((seq, Q_DIM), F32), kv_shape, kv_shape, kv_shape, kv_shape,
                   jax.ShapeDtypeStruct((N_Q_HEADS, LANES), F32)],
        sem=("arbitrary",))(probs, p_sinks, q, k, k, v, v, do)


def _rope_bwd(dq, dkc, dkp, dvc, dvp, rc, rsa, rsb):
    seq = dq.shape[0]
    nb = seq // BLOCK
    tm = min(TOKEN_TILE, seq)
    nt = seq // tm
    per = tm // BLOCK
    nxt = pl.BlockSpec((BLOCK, KV_DIM), lambda i: (jnp.minimum((i + 1) * per, nb - 1), 0))

    def body(dq_ref, dkc_ref, dkp_ref, dkn_ref, dvc_ref, dvp_ref, dvn_ref, c_ref, sa_ref, sb_ref, o_ref, db_ref):
        i = pl.program_id(0)

        @pl.when(i == 0)
        def _():
            db_ref[...] = jnp.zeros_like(db_ref)

        c, sa, sb = c_ref[...], sa_ref[...], sb_ref[...]
        last = i == nt - 1

        def from_next_block(prev_ref, next_ref):
            tail = jnp.where(last, 0.0, next_ref[...])
            return tail if per == 1 else jnp.concatenate([prev_ref[BLOCK:, :], tail], axis=0)

        dk = dkc_ref[...] + from_next_block(dkp_ref, dkn_ref)
        dv = dvc_ref[...] + from_next_block(dvp_ref, dvn_ref)
        for l in range(Q_DIM // LANES):
            ls = slice(l * LANES, (l + 1) * LANES)
            blk = _rope_t(dq_ref[:, ls], c, sa, sb)
            o_ref[:, ls] = blk.astype(BF16)
            db_ref[:, ls] += _sum_rows(blk)
        dkr = _rope_t(dk, c, sa, sb)
        o_ref[:, Q_DIM:Q_DIM + KV_DIM] = dkr.astype(BF16)
        db_ref[:, Q_DIM:Q_DIM + KV_DIM] += _sum_rows(dkr)
        o_ref[:, Q_DIM + KV_DIM:] = dv.astype(BF16)
        db_ref[:, Q_DIM + KV_DIM:] += _sum_rows(dv)

    kv = _rows(tm, KV_DIM)
    tab = _rows(tm, LANES)
    return _call(
        body, name="rope_bwd", grid=(nt,),
        in_specs=[_rows(tm, Q_DIM), kv, kv, nxt, kv, kv, nxt, tab, tab, tab],
        out_specs=[_rows(tm, QKV_DIM), _full((1, QKV_DIM))],
        out_shape=[jax.ShapeDtypeStruct((seq, QKV_DIM), BF16), jax.ShapeDtypeStruct((1, QKV_DIM), F32)],
        sem=("arbitrary",))(dq, dkc, dkp, dkp, dvc, dvp, dvp, rc, rsa, rsb)


def _adamw(w, g, m, v):
    m = ADAM_B1 * m + (1.0 - ADAM_B1) * g
    v = ADAM_B2 * v + (1.0 - ADAM_B2) * (g * g)
    m_hat = m / (1.0 - ADAM_B1 ** ADAM_STEP)
    v_hat = v / (1.0 - ADAM_B2 ** ADAM_STEP)
    delta = -ADAM_LR * (m_hat / (jnp.sqrt(v_hat) + ADAM_EPS) + ADAM_WD * w)
    return delta, m, v


def _sum_slots(name, parts):
    _, rows, cols = parts.shape
    tr = rows if rows <= 512 else ROWS_FF

    def body(p_ref, g_ref):
        g = p_ref[0].astype(F32)
        for d in range(1, N_DEV):
            g = g + p_ref[d].astype(F32)
        g_ref[...] = g

    return _call(
        body, name=name, grid=(rows // tr,),
        in_specs=[pl.BlockSpec((N_DEV, tr, cols), lambda i: (0, i, 0))],
        out_specs=_rows(tr, cols), out_shape=jax.ShapeDtypeStruct((rows, cols), F32),
        sem=("parallel",))(parts)


def _adamw_native(name, g, w, m, v):
    layers, rows, cols = w.shape
    tr = rows if rows <= 512 else 256

    def body(g_ref, w_ref, m_ref, v_ref, d_ref, nm_ref, nv_ref):
        d_ref[...], nm_ref[...], nv_ref[...] = _adamw(w_ref[...], g_ref[...], m_ref[...], v_ref[...])

    spec = pl.BlockSpec((1, tr, cols), lambda l, i: (l, i, 0))
    shape = jax.ShapeDtypeStruct(w.shape, F32)
    return _call(
        body, name=name, grid=(layers, rows // tr), in_specs=[spec, spec, spec, spec],
        out_specs=[spec, spec, spec], out_shape=[shape, shape, shape],
        sem=("parallel", "parallel"))(g, w, m, v)


def _adamw_from_slots(name, slots, row_block, w, m, v):
    layers, rows, cols = w.shape

    def body(*refs):
        slot_refs = refs[:layers]
        w_ref, m_ref, v_ref, g_ref, d_ref, nm_ref, nv_ref = refs[layers:]
        for l in range(layers):
            @pl.when(pl.program_id(0) == l)
            def _(p_ref=slot_refs[l]):
                g = p_ref[0].astype(F32)
                for d in range(1, N_DEV):
                    g = g + p_ref[d].astype(F32)
                g_ref[0] = g
        d, nm, nv = _adamw(w_ref[0], g_ref[0], m_ref[0], v_ref[0])
        d_ref[0], nm_ref[0], nv_ref[0] = d, nm, nv

    slot_spec = pl.BlockSpec((N_DEV, rows, cols), lambda l: (0, row_block, 0))
    spec = pl.BlockSpec((1, rows, cols), lambda l: (l, 0, 0))
    shape = jax.ShapeDtypeStruct(w.shape, F32)
    return _call(
        body, name=name, grid=(layers,), in_specs=[slot_spec] * layers + [spec, spec, spec],
        out_specs=[spec, spec, spec, spec], out_shape=[shape, shape, shape, shape],
        sem=("arbitrary",))(*slots, w, m, v)


def _adamw_replicated(parts, w, m, v):
    def body(p_ref, w_ref, m_ref, v_ref, g_ref, d_ref, nm_ref, nv_ref):
        g = p_ref[0]
        for j in range(1, N_DEV):
            g = g + p_ref[j]
        g_ref[...] = g
        d_ref[...], nm_ref[...], nv_ref[...] = _adamw(w_ref[...], g, m_ref[...], v_ref[...])

    spec = _full((REPL_ROWS, D_MODEL))
    shape = jax.ShapeDtypeStruct((REPL_ROWS, D_MODEL), F32)
    return _call(
        body, name="adamw_replicated", grid=(1,),
        in_specs=[_full((N_DEV, REPL_ROWS, D_MODEL)), spec, spec, spec],
        out_specs=[spec, spec, spec, spec], out_shape=[shape, shape, shape, shape],
        sem=("arbitrary",))(parts, w, m, v)


_MESH = pl.DeviceIdType.MESH
_ANY = pl.BlockSpec(memory_space=pl.ANY)


def _all_gather(name, xs):
    rows, cols = xs.shape

    def body(x_ref, out_ref, send_sems, recv_sems, local_sem):
        x, y, c = lax.axis_index("x"), lax.axis_index("y"), lax.axis_index("c")
        me, sibling = (x, y, c), (x, y, 1 - c)
        chips = [(1 - x, y), (x, 1 - y), (1 - x, 1 - y)]

        def slot(px, py, pc):
            return out_ref.at[4 * px + 2 * py + pc]

        def copy(k, block, to, src=None):
            return pltpu.make_async_remote_copy(
                src_ref=slot(*block) if src is None else src, dst_ref=slot(*block),
                send_sem=send_sems.at[k], recv_sem=recv_sems.at[k], device_id=to, device_id_type=_MESH)

        mine = pltpu.make_async_copy(x_ref, slot(*me), local_sem)
        mine.start()
        first = [copy(0, me, sibling, src=x_ref)]
        first += [copy(1 + j, me, (*chip, c), src=x_ref) for j, chip in enumerate(chips)]
        for cp in first:
            cp.start()
        passed = [copy(4 + j, (*chip, c), sibling) for j, chip in enumerate(chips)]
        for j, chip in enumerate(chips):
            copy(1 + j, (*chip, c), me).wait_recv()
            passed[j].start()
        copy(0, sibling, me).wait_recv()
        for j, chip in enumerate(chips):
            copy(4 + j, (*chip, 1 - c), me).wait_recv()
        for cp in first + passed:
            cp.wait_send()
        mine.wait()

    return pl.pallas_call(
        body, name=name, out_shape=jax.ShapeDtypeStruct((N_DEV, rows, cols), xs.dtype),
        in_specs=[_ANY], out_specs=_ANY,
        scratch_shapes=[pltpu.SemaphoreType.DMA((7,)), pltpu.SemaphoreType.DMA((7,)), pltpu.SemaphoreType.DMA],
    )(xs)


N_PEERS = N_DEV - 1
_HBM = pl.BlockSpec(memory_space=pltpu.HBM)
_SEM = pl.BlockSpec(memory_space=pltpu.SEMAPHORE)
_DATAFLOW = pltpu.SideEffectType.DATAFLOW_SIDE_EFFECTING
_TOKEN = jax.ShapeDtypeStruct((SUBLANES, LANES), F32)


def _peers():
    x, y, c = lax.axis_index("x"), lax.axis_index("y"), lax.axis_index("c")
    out = []
    for k in range(1, N_DEV):
        px = 1 - x if k & 4 else x
        py = 1 - y if k & 2 else y
        pc = 1 - c if k & 1 else c
        out.append(((px, py, pc), 4 * px + 2 * py + pc))
    return 4 * x + 2 * y + c, out


def _in_hbm(a):
    return pltpu.with_memory_space_constraint(a, pltpu.HBM)


def _landing(rows):
    return _in_hbm(lax.empty((N_DEV, rows, D_MODEL), BF16))


def _sem_pair():
    return pltpu.SemaphoreType.DMA((N_PEERS,)), pltpu.SemaphoreType.DMA((N_PEERS,))


def _gather_start(name, payloads):
    n = len(payloads)

    def body(*refs):
        src, land = refs[:n], refs[n:2 * n]
        sems = refs[2 * n:4 * n]
        token = refs[-1]
        me, peers = _peers()
        for g in range(n):
            for k, (pos, _) in enumerate(peers):
                pltpu.make_async_remote_copy(
                    src_ref=src[g], dst_ref=land[g].at[me], send_sem=sems[2 * g].at[k],
                    recv_sem=sems[2 * g + 1].at[k], device_id=pos, device_id_type=_MESH).start()
        token[...] = jnp.zeros_like(token)

    lands = [_landing(p.shape[0]) for p in payloads]
    sem_shapes = [s for _ in payloads for s in _sem_pair()]
    hbm_shapes = [pltpu.HBM(a.shape, a.dtype) for a in list(payloads) + lands]
    out = pl.pallas_call(
        body, name=name, out_shape=(*sem_shapes, *hbm_shapes, _TOKEN),
        in_specs=[_HBM] * (2 * n), out_specs=(*[_SEM] * (2 * n), *[_HBM] * (2 * n), pl.BlockSpec(memory_space=pltpu.VMEM)),
        input_output_aliases={i: 2 * n + i for i in range(2 * n)},
        compiler_params=pltpu.CompilerParams(has_side_effects=_DATAFLOW),
    )(*[_in_hbm(p) for p in payloads], *lands)
    sems, thru = out[:2 * n], out[2 * n:4 * n]
    return [(thru[g], thru[n + g], sems[2 * g], sems[2 * g + 1]) for g in range(n)], out[-1]


def _gather_wait(name, group, after):
    payload, land, send_sems, recv_sems = group

    def body(src_ref, land_ref, send_ref, recv_ref, after_ref, src_out, land_out):
        _, peers = _peers()
        for k, (pos, idx) in enumerate(peers):
            cp = pltpu.make_async_remote_copy(
                src_ref=src_ref, dst_ref=land_ref.at[idx], send_sem=send_ref.at[k], recv_sem=recv_ref.at[k],
                device_id=pos, device_id_type=_MESH)
            cp.wait_send()
            cp.wait_recv()

    _, land = pl.pallas_call(
        body, name=name, out_shape=(pltpu.HBM(payload.shape, payload.dtype), pltpu.HBM(land.shape, land.dtype)),
        in_specs=[_HBM, _HBM, _SEM, _SEM, _ANY], out_specs=(_HBM, _HBM), input_output_aliases={0: 0, 1: 1},
        compiler_params=pltpu.CompilerParams(has_side_effects=_DATAFLOW),
    )(payload, land, send_sems, recv_sems, after)
    me = 4 * lax.axis_index("x") + 2 * lax.axis_index("y") + lax.axis_index("c")
    return lax.dynamic_update_slice(land, payload[None], (me, 0, 0))


def _scatter_start(name, blocks):
    rows = blocks.shape[1]

    def body(blocks_ref, land_ref, send_sems, recv_sems, blocks_out, land_out, token):
        me, peers = _peers()
        for k, (pos, idx) in enumerate(peers):
            pltpu.make_async_remote_copy(
                src_ref=blocks_ref.at[idx], dst_ref=land_ref.at[me], send_sem=send_sems.at[k],
                recv_sem=recv_sems.at[k], device_id=pos, device_id_type=_MESH).start()
        token[...] = jnp.zeros_like(token)

    land = _landing(rows)
    send_sems, recv_sems, blocks_thru, land_thru, token = pl.pallas_call(
        body, name=name,
        out_shape=(*_sem_pair(), pltpu.HBM(blocks.shape, blocks.dtype), pltpu.HBM(land.shape, land.dtype), _TOKEN),
        in_specs=[_HBM, _HBM], out_specs=(_SEM, _SEM, _HBM, _HBM, pl.BlockSpec(memory_space=pltpu.VMEM)),
        input_output_aliases={0: 2, 1: 3},
        compiler_params=pltpu.CompilerParams(has_side_effects=_DATAFLOW),
    )(_in_hbm(blocks), land)
    return (blocks_thru, land_thru, send_sems, recv_sems), token


def _scatter_wait(groups, after):
    n = len(groups)

    def body(*refs):
        blocks, land = refs[:n], refs[n:2 * n]
        sems = refs[2 * n:4 * n]
        _, peers = _peers()
        for g in range(n):
            for k, (pos, idx) in enumerate(peers):
                cp = pltpu.make_async_remote_copy(
                    src_ref=blocks[g].at[idx], dst_ref=land[g].at[idx], send_sem=sems[2 * g].at[k],
                    recv_sem=sems[2 * g + 1].at[k], device_id=pos, device_id_type=_MESH)
                cp.wait_send()
                cp.wait_recv()

    hbm = [grp[0] for grp in groups] + [grp[1] for grp in groups]
    sems = [s for grp in groups for s in grp[2:]]
    out = pl.pallas_call(
        body, name="rs_wait", out_shape=tuple(pltpu.HBM(a.shape, a.dtype) for a in hbm),
        in_specs=[_HBM] * (2 * n) + [_SEM] * (2 * n) + [_ANY], out_specs=tuple([_HBM] * (2 * n)),
        input_output_aliases={i: i for i in range(2 * n)},
        compiler_params=pltpu.CompilerParams(has_side_effects=_DATAFLOW),
    )(*hbm, *sems, after)
    me = 4 * lax.axis_index("x") + 2 * lax.axis_index("y") + lax.axis_index("c")
    lands = []
    for g in range(n):
        own = lax.dynamic_index_in_dim(out[g], me, axis=0, keepdims=True)
        lands.append(lax.dynamic_update_slice(out[n + g], own, (me, 0, 0)))
    return lands


def _pad_rows(flat, rows):
    return jnp.pad(flat, (0, rows * D_MODEL - flat.shape[0])).reshape(rows, D_MODEL)


SMALL_NAMES = ("conv_b_pw1", "conv_w_dw", "conv_b_dw", "conv_ln_g", "conv_ln_b", "conv_b_pw2")


def _pack_small(p):
    flat = jnp.concatenate([p[n].reshape(-1) for n in SMALL_NAMES])
    return _pad_rows(flat, ROWS_SMALL).reshape(1, ROWS_SMALL, D_MODEL)


def _unpack_small(packed):
    flat = packed.reshape(-1)
    c = D_MODEL // N_DEV
    shapes = ((1, 2 * c), (1, CONV_WIDTH, c), (1, c), (1, c), (1, c), (1, c))
    out, o = {}, 0
    for n, shape in zip(SMALL_NAMES, shapes):
        size = shape[-1] * (shape[1] if len(shape) == 3 else 1)
        out[n] = flat[o:o + size].reshape(shape)
        o += size
    return out


def _gather_payloads(p):
    t = lambda a: jnp.swapaxes(a, -1, -2).astype(BF16)
    w1t, w3t, w2 = t(p["ffn_w1"]), t(p["ffn_w3"]), p["ffn_w2"].astype(BF16)
    bits = lax.bitcast_convert_type(_pack_small(p).reshape(-1)[:ROWS_SMALL * D_MODEL // 2], jnp.uint32)
    halves = [(bits >> 16).astype(jnp.uint16), (bits & 0xFFFF).astype(jnp.uint16)]
    small = lax.bitcast_convert_type(jnp.concatenate(halves), BF16).reshape(ROWS_SMALL, D_MODEL)
    conv = jnp.concatenate([t(p["conv_w_pw1"][0]), p["conv_w_pw2"][0].astype(BF16), small], axis=0)
    ffn = [jnp.concatenate([w1t[l], w3t[l], w2[l]], axis=0) for l in range(2)]
    return [t(p["attn_w_qkv"][0]), p["attn_w_o"][0].astype(BF16), ffn[0], conv, ffn[1]]


def _device_rows(land, lo, n):
    return land[:, lo:lo + n].reshape(N_DEV * n, D_MODEL)


def _unpack_conv(land):
    words = lax.bitcast_convert_type(land[:, ROWS_PW1 + ROWS_PW2:], jnp.uint16).astype(jnp.uint32)
    words = words.reshape(N_DEV, 2, ROWS_SMALL * D_MODEL // 2)
    small = lax.bitcast_convert_type((words[:, 0] << 16) | words[:, 1], F32)
    c = D_MODEL // N_DEV
    b_pw1 = small[:, :2 * c].reshape(1, 2 * D_MODEL)
    s = 2 * c
    w_dw = small[:, s:s + CONV_WIDTH * c].reshape(N_DEV, CONV_WIDTH, c).transpose(1, 0, 2).reshape(CONV_WIDTH, D_MODEL)
    s += CONV_WIDTH * c
    b_dw, ln_g, ln_b, b_pw2 = (small[:, s + i * c:s + (i + 1) * c].reshape(1, D_MODEL) for i in range(4))
    return dict(w_pw1_t=_device_rows(land, 0, ROWS_PW1), w_pw2=_device_rows(land, ROWS_PW1, ROWS_PW2),
                b_pw1=b_pw1, w_dw=w_dw, b_dw=b_dw, ln_g=ln_g, ln_b=ln_b, b_pw2=b_pw2)


def _dest_blocks(mats):
    return jnp.concatenate([a.reshape(N_DEV, -1, D_MODEL) for a in mats], axis=1)


def _small_grad_rows(g_bpw1, g_dw, g_bdw, g_lng, g_lnb, g_bpw2):
    c = D_MODEL // N_DEV
    small = jnp.concatenate(
        [g_bpw1.reshape(N_DEV, 2 * c), g_dw.reshape(CONV_WIDTH, N_DEV, c).transpose(1, 0, 2).reshape(N_DEV, -1),
         g_bdw.reshape(N_DEV, c), g_lng.reshape(N_DEV, c), g_lnb.reshape(N_DEV, c), g_bpw2.reshape(N_DEV, c)], axis=1)
    small = jnp.pad(small, ((0, 0), (0, ROWS_SMALL * D_MODEL - SMALL_USED)))
    return small.reshape(N_DEV * ROWS_SMALL, D_MODEL).astype(BF16)


LOSS_ROW = 9


def _pack_replicated(norm_mix, norm_ffn, b_qkv, sinks, b_o, norm_final, extra=None):
    rows = [norm_mix.reshape(2, D_MODEL), norm_ffn.reshape(2, D_MODEL), _pad_rows(b_qkv.reshape(-1), 2),
            _pad_rows(sinks.reshape(-1), 1), b_o.reshape(1, D_MODEL), norm_final.reshape(1, D_MODEL)]
    if extra is not None:
        rows.append(_pad_rows(extra.reshape(-1), 1))
    p = jnp.concatenate(rows, axis=0)
    return jnp.pad(p, ((0, REPL_ROWS - p.shape[0]), (0, 0)))


def _unpack_replicated(p):
    return dict(norm_mix=p[0:2], norm_ffn=p[2:4], attn_b_qkv=p[4:6].reshape(-1)[:QKV_DIM].reshape(1, QKV_DIM),
                attn_sinks=p[6, :N_Q_HEADS].reshape(1, N_Q_HEADS), attn_b_o=p[7:8], norm_final=p[8])


WEIGHT_ORDER = ['norm_mix', 'norm_ffn', 'attn_w_qkv', 'attn_b_qkv', 'attn_sinks', 'attn_w_o', 'attn_b_o',
                'conv_w_pw1', 'conv_b_pw1', 'conv_w_dw', 'conv_b_dw', 'conv_ln_g', 'conv_ln_b', 'conv_w_pw2',
                'conv_b_pw2', 'ffn_w1', 'ffn_w3', 'ffn_w2', 'norm_final']


def kernel(x, norm_mix, norm_ffn, attn_w_qkv, attn_b_qkv, attn_sinks, attn_w_o, attn_b_o, conv_w_pw1, conv_b_pw1, conv_w_dw, conv_b_dw, conv_ln_g, conv_ln_b, conv_w_pw2, conv_b_pw2, ffn_w1, ffn_w3, ffn_w2, norm_final, loss_target, m_norm_mix, m_norm_ffn, m_attn_w_qkv, m_attn_b_qkv, m_attn_sinks, m_attn_w_o, m_attn_b_o, m_conv_w_pw1, m_conv_b_pw1, m_conv_w_dw, m_conv_b_dw, m_conv_ln_g, m_conv_ln_b, m_conv_w_pw2, m_conv_b_pw2, m_ffn_w1, m_ffn_w3, m_ffn_w2, m_norm_final, v_norm_mix, v_norm_ffn, v_attn_w_qkv, v_attn_b_qkv, v_attn_sinks, v_attn_w_o, v_attn_b_o, v_conv_w_pw1, v_conv_b_pw1, v_conv_w_dw, v_conv_b_dw, v_conv_ln_g, v_conv_ln_b, v_conv_w_pw2, v_conv_b_pw2, v_ffn_w1, v_ffn_w3, v_ffn_w2, v_norm_final):
    xs = x[0]
    target = loss_target[0]
    seq = xs.shape[0]

    w = dict(attn_w_qkv=attn_w_qkv, attn_w_o=attn_w_o, conv_w_pw1=conv_w_pw1, conv_b_pw1=conv_b_pw1,
             conv_w_dw=conv_w_dw, conv_b_dw=conv_b_dw, conv_ln_g=conv_ln_g, conv_ln_b=conv_ln_b,
             conv_w_pw2=conv_w_pw2, conv_b_pw2=conv_b_pw2, ffn_w1=ffn_w1, ffn_w3=ffn_w3, ffn_w2=ffn_w2)
    m = dict(attn_w_qkv=m_attn_w_qkv, attn_w_o=m_attn_w_o, conv_w_pw1=m_conv_w_pw1, conv_b_pw1=m_conv_b_pw1,
             conv_w_dw=m_conv_w_dw, conv_b_dw=m_conv_b_dw, conv_ln_g=m_conv_ln_g, conv_ln_b=m_conv_ln_b,
             conv_w_pw2=m_conv_w_pw2, conv_b_pw2=m_conv_b_pw2, ffn_w1=m_ffn_w1, ffn_w3=m_ffn_w3, ffn_w2=m_ffn_w2)
    v = dict(attn_w_qkv=v_attn_w_qkv, attn_w_o=v_attn_w_o, conv_w_pw1=v_conv_w_pw1, conv_b_pw1=v_conv_b_pw1,
             conv_w_dw=v_conv_w_dw, conv_b_dw=v_conv_b_dw, conv_ln_g=v_conv_ln_g, conv_ln_b=v_conv_ln_b,
             conv_w_pw2=v_conv_w_pw2, conv_b_pw2=v_conv_b_pw2, ffn_w1=v_ffn_w1, ffn_w3=v_ffn_w3, ffn_w2=v_ffn_w2)

    payloads = _gather_payloads(w)
    (ag_qkv, ag_wo), tok = _gather_start("ag_start_attn", payloads[:2])
    (ag_ffn0, ag_conv, ag_ffn1), tok = _gather_start(
        "ag_start_rest", [payloads[2] + tok[0, 0].astype(BF16), payloads[3], payloads[4]])
    rc, rsa, rsb = _rope_tables(seq)
    sinks = attn_sinks.reshape(N_Q_HEADS)
    g_mix0, g_mix1 = norm_mix[0:1] + tok[0, 0], norm_mix[1:2]
    g_ffn0, g_ffn1 = norm_ffn[0:1], norm_ffn[1:2]
    g_fin = norm_final.reshape(1, D_MODEL)

    y0 = _rms_fwd(xs, g_mix0)
    w_qkv_t = _gather_wait("ag_wait_qkv", ag_qkv, y0).reshape(QKV_DIM, D_MODEL)
    q, k, vv = _qkv_fwd(y0, w_qkv_t, attn_b_qkv, rc, rsa, rsb)
    attn, probs, p_sinks = _attn_fwd(sinks, q, k, vv)
    w_o = _gather_wait("ag_wait_wo", ag_wo, attn).reshape(Q_DIM, D_MODEL)
    h1 = _mm_res("attn_out_proj", attn, w_o, attn_b_o, xs)
    w_ffn0 = _gather_wait("ag_wait_ffn0", ag_ffn0, h1)
    f0, u0, p0, s0 = _ffn_up("ffn0_up", h1, g_ffn0, w_ffn0)
    h2 = _ffn_down("ffn0_down", s0, w_ffn0, h1)
    wt = _unpack_conv(_gather_wait("ag_wait_conv", ag_conv, h2))
    wd = jnp.concatenate([wt["w_dw"][::-1], jnp.zeros((TAPS_PAD - CONV_WIDTH, D_MODEL), F32)], axis=0)
    y1, a, dwc, z = _conv_fwd(h2, g_mix1, wt["w_pw1_t"], wt["b_pw1"], wd, wt["b_dw"], wt["ln_g"], wt["ln_b"])
    h3 = _mm_res("conv_out_proj", z, wt["w_pw2"], wt["b_pw2"], h2)
    w_ffn1 = _gather_wait("ag_wait_ffn1", ag_ffn1, h3)
    f1, u1, p1, s1 = _ffn_up("ffn1_up", h3, g_ffn1, w_ffn1)
    dh4, dh4b, sq, dg_fin = _ffn_down_loss(s1, w_ffn1, h3, target, g_fin)

    du1, dp1 = _ffn_bwd_act("ffn1_bwd_act", dh4b, u1, p1, w_ffn1)
    dh3, dh3b, dg_ffn1 = _ffn_bwd_in("ffn1_bwd_in", du1, dp1, w_ffn1, h3, dh4, g_ffn1)
    blocks = _mm_tn("ffn1_dw2", s1, dh4b, tk=FF_TILE, into=(None, W2_SLOT))
    blocks = _mm_tn("ffn1_dw1", du1, f1, tk=FF_TILE, into=(blocks, W1T_SLOT))
    blocks = _mm_tn("ffn1_dw3", dp1, f1, tk=FF_TILE, into=(blocks, W3T_SLOT))
    rs_ffn1, tok = _scatter_start("rs_start_ffn1", blocks)

    da, dlg, dlb, dbdw, dwd, dbpw1, dbpw2 = _conv_bwd(dh3, wt["w_pw2"], dwc, a, wt["ln_g"] + tok[0, 0],
                                                     wt["ln_b"], wd)
    gpw2 = _mm_tn("conv_dw_pw2", z, dh3b, tk=D_MODEL)
    dh2, dh2b, dg_mix1 = _mm_rms_bwd("conv_in_bwd", da, wt["w_pw1_t"], h2, dh3, g_mix1, True)
    gpw1t = _mm_tn("conv_dw_pw1", da, y1, tk=D_MODEL)
    small_rows = _small_grad_rows(dbpw1, dwd[:CONV_WIDTH][::-1], dbdw, dlg, dlb, dbpw2)
    rs_conv, tok = _scatter_start("rs_start_conv", _dest_blocks([gpw1t, gpw2, small_rows]))

    du0, dp0 = _ffn_bwd_act("ffn0_bwd_act", dh2b, u0, p0, w_ffn0)
    dh1, dh1b, dg_ffn0 = _ffn_bwd_in("ffn0_bwd_in", du0, dp0, w_ffn0, h1, dh2, g_ffn0 + tok[0, 0])
    blocks = _mm_tn("ffn0_dw2", s0, dh2b, tk=FF_TILE, into=(None, W2_SLOT))
    blocks = _mm_tn("ffn0_dw1", du0, f0, tk=FF_TILE, into=(blocks, W1T_SLOT))
    blocks = _mm_tn("ffn0_dw3", dp0, f0, tk=FF_TILE, into=(blocks, W3T_SLOT))
    rs_ffn0, tok = _scatter_start("rs_start_ffn0", blocks)

    gwo = _mm_tn("attn_dw_o", attn, dh1b, tk=D_MODEL)
    rs_wo, tok2 = _scatter_start("rs_start_wo", _dest_blocks([gwo]))
    dattn, dbo = _nt_bias("attn_out_bwd", dh1, w_o)
    dq, dkc, dkp, dvc, dvp, dsink = _attn_bwd(probs, p_sinks + (tok[0, 0] + tok2[0, 0]), q, k, vv, dattn)
    dqkv, dbqkv = _rope_bwd(dq, dkc, dkp, dvc, dvp, rc, rsa, rsb)
    gqkvt = _mm_tn("attn_dw_qkv", dqkv, y0, tk=QKV_DIM)
    rs_qkv, tok = _scatter_start("rs_start_qkv", _dest_blocks([gqkvt]))
    dx, dg_mix0 = _mm_rms_bwd("qkv_in_bwd", dqkv, w_qkv_t, xs, dh1, g_mix0 + tok[0, 0], False)

    p_ffn1, p_conv, p_ffn0, p_wo, p_qkv = _scatter_wait([rs_ffn1, rs_conv, rs_ffn0, rs_wo, rs_qkv], dx)

    tr = lambda a: jnp.swapaxes(a, -1, -2)
    same = lambda a: a
    ffn_slots = [p_ffn0, p_ffn1]
    plan = dict(
        ffn_w1=(ffn_slots, W1T_SLOT, tr), ffn_w3=(ffn_slots, W3T_SLOT, tr), ffn_w2=(ffn_slots, W2_SLOT, same),
        attn_w_qkv=([p_qkv], 0, tr), attn_w_o=([p_wo], 0, same),
        conv_w_pw2=([p_conv], ROWS_PW1 // ROWS_PW2, same))
    sharded = [{}, {}, {}, {}]
    for n, (slots, row_block, view) in plan.items():
        outs_n = _adamw_from_slots("adamw_" + n, slots, row_block, view(w[n]), view(m[n]), view(v[n]))
        for dst, t in zip(sharded, outs_n):
            dst[n] = view(t)
    small_out = _adamw_from_slots("adamw_small", [p_conv], (ROWS_PW1 + ROWS_PW2) // ROWS_SMALL,
                                  _pack_small(w), _pack_small(m), _pack_small(v))
    for dst, t in zip(sharded, small_out):
        dst.update(_unpack_small(t))
    g_pw1 = _sum_slots("rs_sum_pw1", p_conv[:, :ROWS_PW1]).T[None]
    pw1_out = _adamw_native("adamw_conv_w_pw1", g_pw1, w["conv_w_pw1"], m["conv_w_pw1"], v["conv_w_pw1"])
    for dst, t in zip(sharded, (g_pw1,) + tuple(pw1_out)):
        dst["conv_w_pw1"] = t

    part = _pack_replicated(jnp.concatenate([dg_mix0, dg_mix1]), jnp.concatenate([dg_ffn0, dg_ffn1]),
                            dbqkv, -dsink[:, 0], dbo, dg_fin, extra=sq[0, 0:1])
    parts = _all_gather("ag_replicated_grads", part)
    w_rep = _pack_replicated(norm_mix, norm_ffn, attn_b_qkv, attn_sinks, attn_b_o, norm_final)
    m_rep = _pack_replicated(m_norm_mix, m_norm_ffn, m_attn_b_qkv, m_attn_sinks, m_attn_b_o, m_norm_final)
    v_rep = _pack_replicated(v_norm_mix, v_norm_ffn, v_attn_b_qkv, v_attn_sinks, v_attn_b_o, v_norm_final)
    rep_out = _adamw_replicated(parts, w_rep, m_rep, v_rep)
    replicated = [_unpack_replicated(t) for t in rep_out]
    loss = rep_out[0][LOSS_ROW, 0] * (0.5 / D_MODEL)

    outs = [loss, dx.reshape(1, seq, D_MODEL)]
    for sh, rp in zip(sharded, replicated):
        merged = {**sh, **rp}
        outs += [merged[n] for n in WEIGHT_ORDER]
    return tuple(outs)
```

```python
import jax
import jax.numpy as jnp
from jax import lax
from jax.experimental import pallas as pl
from jax.experimental.pallas import tpu as pltpu

F32 = jnp.float32
BF16 = jnp.bfloat16

D_MODEL = 1024
HEAD_DIM = 64
N_Q_HEADS = 16
N_KV_HEADS = 2
Q_PER_KV = 8
Q_DIM = N_Q_HEADS * HEAD_DIM
KV_DIM = N_KV_HEADS * HEAD_DIM
QKV_DIM = Q_DIM + 2 * KV_DIM
BLOCK = 128
CONV_WIDTH = 31
D_FF = 2816
ROPE_THETA = 10000.0
RMS_EPS = 1e-5
LN_EPS = 1e-5
ADAM_LR = 0.001
ADAM_B1 = 0.9
ADAM_B2 = 0.999
ADAM_EPS = 1e-08
ADAM_WD = 0.01
ADAM_STEP = 10
N_DEV = 8

LANES = 128
SUBLANES = 8
TOKEN_TILE = 512
CONV_CHUNK = 64
CONV_HALO = 32
TAPS_PAD = 32
VMEM_LIMIT = 56 * 1024 * 1024
NEG_INF = float(jnp.finfo(jnp.float32).min)

ROWS_FF = D_FF // N_DEV
W1T_SLOT, W3T_SLOT, W2_SLOT = 0, 1, 2
ROWS_PW1 = 2 * D_MODEL // N_DEV
ROWS_PW2 = D_MODEL // N_DEV
ROWS_SMALL = 16
SMALL_USED = 2 * D_MODEL // N_DEV + CONV_WIDTH * (D_MODEL // N_DEV) + 4 * (D_MODEL // N_DEV)
FF_SPLIT = 2
FF_TILE_DEVS = N_DEV // FF_SPLIT
FF_TILE = FF_TILE_DEVS * ROWS_FF
REPL_ROWS = 16


def _call(body, *, name, grid, in_specs, out_specs, out_shape, scratch=(), sem=None, aliases=None):
    return pl.pallas_call(
        body, name=name, grid=grid, in_specs=in_specs, out_specs=out_specs, out_shape=out_shape,
        scratch_shapes=list(scratch), input_output_aliases=aliases or {},
        compiler_params=pltpu.CompilerParams(dimension_semantics=sem, vmem_limit_bytes=VMEM_LIMIT))


def _full(shape):
    return pl.BlockSpec(shape, lambda *_: (0,) * len(shape))


def _resident(shape):
    return pl.BlockSpec(shape, lambda *_: (0,) * len(shape), pipeline_mode=pl.Buffered(1))


def _rows(tm, n):
    return pl.BlockSpec((tm, n), lambda i, *_: (i, 0))


def _sig(x):
    return 1.0 / (1.0 + jnp.exp(-x))


def _sum_rows(x):
    return jnp.sum(x, axis=0, keepdims=True)


def _nt(a, b):
    return lax.dot_general(a, b, (((1,), (1,)), ((), ())), preferred_element_type=F32)


def _tn(a, b):
    return lax.dot_general(a, b, (((0,), (0,)), ((), ())), preferred_element_type=F32)


def _rms_stats(x):
    return lax.rsqrt(jnp.mean(x * x, axis=-1, keepdims=True) + RMS_EPS)


def _rms_bwd(dy, x, g, dres):
    r = _rms_stats(x)
    n = x * r
    dn = dy * g
    dx = dres + r * (dn - n * jnp.mean(dn * n, axis=-1, keepdims=True))
    return dx, _sum_rows(dy * n)


def _rope_tables(seq):
    half = HEAD_DIM // 2
    pos = jnp.arange(seq, dtype=F32)
    inv_freq = ROPE_THETA ** (-jnp.arange(0, HEAD_DIM, 2, dtype=F32) / HEAD_DIM)
    ang = pos[:, None] * inv_freq[None, :]
    spread = lambda t: jnp.broadcast_to(t[:, None, :], (seq, LANES // half, half)).reshape(seq, LANES)
    cos, sin = spread(jnp.cos(ang)), spread(jnp.sin(ang))
    first_half = (jnp.arange(LANES) % HEAD_DIM < half)[None, :]
    sa = jnp.where(first_half, -sin, 0.0)
    sb = jnp.where(first_half, 0.0, sin)
    return cos, sa, sb


def _rope(t, c, sa, sb):
    half = HEAD_DIM // 2
    return t * c + pltpu.roll(t, LANES - half, 1) * sa + pltpu.roll(t, half, 1) * sb


def _rope_t(dt, c, sa, sb):
    half = HEAD_DIM // 2
    return dt * c + pltpu.roll(dt * sa, half, 1) + pltpu.roll(dt * sb, LANES - half, 1)


def _rms_fwd(x, g):
    seq = x.shape[0]
    tm = min(2 * TOKEN_TILE, seq)

    def body(x_ref, g_ref, y_ref):
        xv = x_ref[...]
        y_ref[...] = (xv * _rms_stats(xv) * g_ref[...]).astype(BF16)

    return _call(
        body, name="rms_fwd", grid=(seq // tm,), in_specs=[_rows(tm, D_MODEL), _full((1, D_MODEL))],
        out_specs=_rows(tm, D_MODEL), out_shape=jax.ShapeDtypeStruct((seq, D_MODEL), BF16),
        sem=("parallel",))(x, g)


def _qkv_fwd(y, w, b, rc, rsa, rsb):
    seq = y.shape[0]
    tm = min(TOKEN_TILE, seq)

    def body(y_ref, w_ref, b_ref, c_ref, sa_ref, sb_ref, q_ref, k_ref, v_ref):
        qkv = _nt(y_ref[...], w_ref[...]) + b_ref[...]
        c, sa, sb = c_ref[...], sa_ref[...], sb_ref[...]
        for i in range(Q_DIM // LANES):
            blk = _rope(qkv[:, i * LANES:(i + 1) * LANES], c, sa, sb)
            q_ref[:, i * LANES:(i + 1) * LANES] = (blk * (HEAD_DIM ** -0.5)).astype(BF16)
        k_ref[...] = _rope(qkv[:, Q_DIM:Q_DIM + KV_DIM], c, sa, sb).astype(BF16)
        v_ref[...] = qkv[:, Q_DIM + KV_DIM:].astype(BF16)

    return _call(
        body, name="qkv_fwd", grid=(seq // tm,),
        in_specs=[_rows(tm, D_MODEL), _resident((QKV_DIM, D_MODEL)), _full((1, QKV_DIM)),
                  _rows(tm, LANES), _rows(tm, LANES), _rows(tm, LANES)],
        out_specs=[_rows(tm, Q_DIM), _rows(tm, KV_DIM), _rows(tm, KV_DIM)],
        out_shape=[jax.ShapeDtypeStruct((seq, Q_DIM), BF16),
                   jax.ShapeDtypeStruct((seq, KV_DIM), BF16), jax.ShapeDtypeStruct((seq, KV_DIM), BF16)],
        sem=("parallel",))(y, w, b, rc, rsa, rsb)


def _band_mask(n):
    row = lax.broadcasted_iota(jnp.int32, (BLOCK, 2 * BLOCK), 0)
    col = lax.broadcasted_iota(jnp.int32, (BLOCK, 2 * BLOCK), 1)
    rel = row + BLOCK - col
    return (rel >= 0) & (rel < BLOCK) & ((col >= BLOCK) | (n > 0))


def _softmax_with_sink(s, mask, sink):
    s = jnp.where(mask, s, NEG_INF)
    m = jnp.maximum(jnp.max(s, axis=-1, keepdims=True), sink)
    p = jnp.exp(s - m)
    e_sink = jnp.exp(sink - m)
    inv = 1.0 / (jnp.sum(p, axis=-1, keepdims=True) + e_sink)
    return p * inv, e_sink * inv


PAIRS_PER_KV = Q_PER_KV // 2


def _kv_specs():
    cur = pl.BlockSpec((BLOCK, KV_DIM), lambda n: (n, 0))
    prev = pl.BlockSpec((BLOCK, KV_DIM), lambda n: (jnp.maximum(n - 1, 0), 0))
    return cur, prev


def _low_lanes():
    return lax.broadcasted_iota(jnp.int32, (2 * BLOCK, KV_DIM), 1) < HEAD_DIM


def _kv_low_high(prev, cur, j, low):
    both = jnp.concatenate([prev, cur], axis=0).astype(F32)
    swapped = pltpu.roll(both, HEAD_DIM, 1)
    at_low, at_high = (both, swapped) if j == 0 else (swapped, both)
    return jnp.where(low, at_low, 0.0).astype(BF16), jnp.where(low, 0.0, at_high).astype(BF16)


def _fold_pair_halves(acc, j, low):
    folded = acc + pltpu.roll(acc, HEAD_DIM, 1)
    return jnp.where(low, folded, 0.0) if j == 0 else jnp.where(low, 0.0, folded)


def _pair_lanes(j, i):
    g = j * PAIRS_PER_KV + i
    return slice(g * LANES, (g + 1) * LANES), 2 * g


def _attn_fwd(sinks, q, k, v):
    seq = q.shape[0]
    cur, prev = _kv_specs()

    def body(sink_ref, q_ref, kc_ref, kp_ref, vc_ref, vp_ref, o_ref, p_ref, ps_ref):
        mask = _band_mask(pl.program_id(0))
        lane = lax.broadcasted_iota(jnp.int32, (BLOCK, LANES), 1)
        p_sinks = jnp.zeros((BLOCK, LANES), F32)
        for j in range(N_KV_HEADS):
            cs = slice(j * HEAD_DIM, (j + 1) * HEAD_DIM)
            kk = jnp.concatenate([kp_ref[:, cs], kc_ref[:, cs]], axis=0)
            vv = jnp.concatenate([vp_ref[:, cs], vc_ref[:, cs]], axis=0)
            for gq in range(Q_PER_KV):
                h = j * Q_PER_KV + gq
                hs = slice(h * HEAD_DIM, (h + 1) * HEAD_DIM)
                probs, p_sink = _softmax_with_sink(_nt(q_ref[:, hs], kk), mask, sink_ref[h])
                pb = probs.astype(BF16)
                p_ref[h] = pb
                p_sinks = jnp.where(lane == h, p_sink, p_sinks)
                o_ref[:, hs] = jnp.dot(pb, vv, preferred_element_type=F32).astype(BF16)
        ps_ref[...] = p_sinks

    return _call(
        body, name="attn_fwd", grid=(seq // BLOCK,),
        in_specs=[pl.BlockSpec(memory_space=pltpu.SMEM), _rows(BLOCK, Q_DIM), cur, prev, cur, prev],
        out_specs=[_rows(BLOCK, Q_DIM), pl.BlockSpec((N_Q_HEADS, BLOCK, 2 * BLOCK), lambda n: (0, n, 0)),
                   _rows(BLOCK, LANES)],
        out_shape=[jax.ShapeDtypeStruct((seq, Q_DIM), BF16),
                   jax.ShapeDtypeStruct((N_Q_HEADS, seq, 2 * BLOCK), BF16),
                   jax.ShapeDtypeStruct((seq, LANES), F32)],
        sem=("parallel",))(sinks, q, k, k, v, v)


def _mm_res(name, a, w, b, res):
    seq, kdim = a.shape
    n = w.shape[1]
    tm = min(2 * TOKEN_TILE, seq)

    def body(a_ref, w_ref, b_ref, r_ref, o_ref):
        o_ref[...] = r_ref[...] + (jnp.dot(a_ref[...], w_ref[...], preferred_element_type=F32) + b_ref[...])

    return _call(
        body, name=name, grid=(seq // tm,),
        in_specs=[_rows(tm, kdim), _resident((kdim, n)), _full((1, n)), _rows(tm, n)],
        out_specs=_rows(tm, n), out_shape=jax.ShapeDtypeStruct((seq, n), F32),
        sem=("parallel",))(a, w, b, res)


def _ff_tile_spec(slot):
    return pl.BlockSpec((FF_TILE_DEVS, ROWS_FF, D_MODEL), lambda j, i: (j, slot, 0))


def _ff_whole_spec(slot):
    return pl.BlockSpec((N_DEV, ROWS_FF, D_MODEL), lambda i: (0, slot, 0), pipeline_mode=pl.Buffered(1))


def _ffn_up(name, h, g, gathered):
    seq = h.shape[0]
    tm = min(TOKEN_TILE, seq)

    def body(h_ref, g_ref, w1_ref, w3_ref, f_ref, u_ref, w_ref, s_ref):
        hv = h_ref[...]
        f = (hv * _rms_stats(hv) * g_ref[...]).astype(BF16)
        f_ref[...] = f
        u = _nt(f, w1_ref[...].reshape(FF_TILE, D_MODEL))
        w = _nt(f, w3_ref[...].reshape(FF_TILE, D_MODEL))
        u_ref[...] = u.astype(BF16)
        w_ref[...] = w.astype(BF16)
        s_ref[...] = (u * _sig(u) * w).astype(BF16)

    n_t = seq // tm
    tile_ff = pl.BlockSpec((tm, FF_TILE), lambda j, i: (i, j))
    f_spec = pl.BlockSpec((tm, D_MODEL), lambda j, i: (jnp.where(j == 0, i, n_t), 0))
    ff_shape = jax.ShapeDtypeStruct((seq, D_FF), BF16)
    return _call(
        body, name=name, grid=(FF_SPLIT, n_t),
        in_specs=[pl.BlockSpec((tm, D_MODEL), lambda j, i: (i, 0)), _full((1, D_MODEL)),
                  _ff_tile_spec(W1T_SLOT), _ff_tile_spec(W3T_SLOT)],
        out_specs=[f_spec, tile_ff, tile_ff, tile_ff],
        out_shape=[jax.ShapeDtypeStruct((seq + tm, D_MODEL), BF16), ff_shape, ff_shape, ff_shape],
        sem=("arbitrary", "arbitrary"))(h, g, gathered, gathered)


def _ffn_down(name, s, gathered, res):
    seq = s.shape[0]
    tm = min(2 * TOKEN_TILE, seq)

    def body(s_ref, w_ref, r_ref, o_ref):
        w2 = w_ref[...].reshape(D_FF, D_MODEL)
        o_ref[...] = r_ref[...] + jnp.dot(s_ref[...], w2, preferred_element_type=F32)

    return _call(
        body, name=name, grid=(seq // tm,),
        in_specs=[_rows(tm, D_FF), _ff_whole_spec(W2_SLOT), _rows(tm, D_MODEL)],
        out_specs=_rows(tm, D_MODEL), out_shape=jax.ShapeDtypeStruct((seq, D_MODEL), F32),
        sem=("parallel",))(s, gathered, res)


def _conv_fwd(h, g, wpw1, bpw1, wd, bdw, lng, lnb):
    seq = h.shape[0]
    tm = min(TOKEN_TILE, seq)
    n_chunks = tm // CONV_CHUNK
    win = CONV_CHUNK + CONV_HALO

    def body(h_ref, g_ref, w_ref, b_ref, wd_ref, bdw_ref, lng_ref, lnb_ref,
             y_ref, a_ref, dwc_ref, z_ref, gbuf):
        i = pl.program_id(0)

        @pl.when(i == 0)
        def _():
            gbuf[0:CONV_HALO, :] = jnp.zeros((CONV_HALO, D_MODEL), F32)

        @pl.when(i > 0)
        def _():
            gbuf[0:CONV_HALO, :] = gbuf[tm:tm + CONV_HALO, :]

        hv = h_ref[...]
        y = (hv * _rms_stats(hv) * g_ref[...]).astype(BF16)
        y_ref[...] = y
        a = _nt(y, w_ref[...]) + b_ref[...]
        a_ref[...] = a.astype(BF16)
        gbuf[CONV_HALO:CONV_HALO + tm, :] = a[:, :D_MODEL] * _sig(a[:, D_MODEL:])

        def chunk(c, carry):
            r0 = pl.multiple_of(c * CONV_CHUNK, CONV_CHUNK)
            for l in range(D_MODEL // LANES):
                ls = slice(l * LANES, (l + 1) * LANES)
                gw = gbuf[pl.ds(r0, win), ls]
                acc = jnp.zeros((CONV_CHUNK, LANES), F32) + bdw_ref[:, ls]
                for s in range(SUBLANES):
                    gs = gw if s == 0 else pltpu.roll(gw, s, 0)
                    for q in range(CONV_HALO // SUBLANES):
                        d = SUBLANES * q + s
                        if d < CONV_WIDTH:
                            lo = CONV_HALO - SUBLANES * q
                            acc = acc + wd_ref[d:d + 1, ls] * gs[lo:lo + CONV_CHUNK]
                dwc_ref[pl.ds(r0, CONV_CHUNK), ls] = acc
            return carry

        lax.fori_loop(0, n_chunks, chunk, 0)

        xv = dwc_ref[...]
        mu = jnp.mean(xv, axis=-1, keepdims=True)
        xc = xv - mu
        var = jnp.mean(xc * xc, axis=-1, keepdims=True)
        ln = xc * lax.rsqrt(var + LN_EPS) * lng_ref[...] + lnb_ref[...]
        z_ref[...] = (ln * _sig(ln)).astype(BF16)

    return _call(
        body, name="conv_fwd", grid=(seq // tm,),
        in_specs=[_rows(tm, D_MODEL), _full((1, D_MODEL)), _full((2 * D_MODEL, D_MODEL)), _full((1, 2 * D_MODEL)),
                  _full((TAPS_PAD, D_MODEL)), _full((1, D_MODEL)), _full((1, D_MODEL)), _full((1, D_MODEL))],
        out_specs=[_rows(tm, D_MODEL), _rows(tm, 2 * D_MODEL), _rows(tm, D_MODEL), _rows(tm, D_MODEL)],
        out_shape=[jax.ShapeDtypeStruct((seq, D_MODEL), BF16), jax.ShapeDtypeStruct((seq, 2 * D_MODEL), BF16),
                   jax.ShapeDtypeStruct((seq, D_MODEL), F32), jax.ShapeDtypeStruct((seq, D_MODEL), BF16)],
        scratch=[pltpu.VMEM((tm + CONV_HALO, D_MODEL), F32)],
        sem=("arbitrary",))(h, g, wpw1, bpw1, wd, bdw, lng, lnb)


def _ffn_down_loss(s, gathered, res, target, g):
    seq = s.shape[0]
    tm = min(TOKEN_TILE, seq)

    def body(s_ref, w_ref, r_ref, t_ref, g_ref, dh_ref, dhb_ref, loss_ref, dg_ref):
        @pl.when(pl.program_id(0) == 0)
        def _():
            loss_ref[...] = jnp.zeros_like(loss_ref)
            dg_ref[...] = jnp.zeros_like(dg_ref)

        hv = r_ref[...] + jnp.dot(s_ref[...], w_ref[...].reshape(D_FF, D_MODEL), preferred_element_type=F32)
        gv = g_ref[...]
        err = hv * _rms_stats(hv) * gv - t_ref[...]
        sq = jnp.sum(jnp.sum(err * err, axis=-1, keepdims=True), axis=0, keepdims=True)
        loss_ref[...] += jnp.broadcast_to(sq, loss_ref.shape)
        dx, dg = _rms_bwd(err * (1.0 / D_MODEL), hv, gv, 0.0)
        dh_ref[...] = dx
        dhb_ref[...] = dx.astype(BF16)
        dg_ref[...] += dg

    return _call(
        body, name="ffn1_down_loss", grid=(seq // tm,),
        in_specs=[_rows(tm, D_FF), _ff_whole_spec(W2_SLOT), _rows(tm, D_MODEL), _rows(tm, D_MODEL),
                  _full((1, D_MODEL))],
        out_specs=[_rows(tm, D_MODEL), _rows(tm, D_MODEL), _full((SUBLANES, LANES)), _full((1, D_MODEL))],
        out_shape=[jax.ShapeDtypeStruct((seq, D_MODEL), F32), jax.ShapeDtypeStruct((seq, D_MODEL), BF16),
                   jax.ShapeDtypeStruct((SUBLANES, LANES), F32), jax.ShapeDtypeStruct((1, D_MODEL), F32)],
        sem=("arbitrary",))(s, gathered, res, target, g)


def _ffn_bwd_act(name, dh, u, w, gathered):
    seq = dh.shape[0]
    tm = min(TOKEN_TILE, seq)

    def body(dh_ref, u_ref, w_ref, w2_ref, du_ref, dw_ref):
        ds = _nt(dh_ref[...], w2_ref[...].reshape(FF_TILE, D_MODEL))
        uv = u_ref[...].astype(F32)
        sg = _sig(uv)
        dw_ref[...] = (ds * (uv * sg)).astype(BF16)
        du_ref[...] = (ds * w_ref[...].astype(F32) * (sg * (1.0 + uv * (1.0 - sg)))).astype(BF16)

    tile_ff = pl.BlockSpec((tm, FF_TILE), lambda j, i: (i, j))
    ff_shape = jax.ShapeDtypeStruct((seq, D_FF), BF16)
    return _call(
        body, name=name, grid=(FF_SPLIT, seq // tm),
        in_specs=[pl.BlockSpec((tm, D_MODEL), lambda j, i: (i, 0)), tile_ff, tile_ff, _ff_tile_spec(W2_SLOT)],
        out_specs=[tile_ff, tile_ff], out_shape=[ff_shape, ff_shape],
        sem=("parallel", "parallel"))(dh, u, w, gathered)


def _ffn_bwd_in(name, du, dw, gathered, h_in, dres, g):
    seq = du.shape[0]
    tm = min(TOKEN_TILE, seq)

    def body(du_ref, dw_ref, w1_ref, w3_ref, h_ref, dr_ref, g_ref, dx_ref, dxb_ref, dg_ref):
        @pl.when(pl.program_id(0) == 0)
        def _():
            dg_ref[...] = jnp.zeros_like(dg_ref)

        df = jnp.dot(du_ref[...], w1_ref[...].reshape(D_FF, D_MODEL), preferred_element_type=F32)
        df = df + jnp.dot(dw_ref[...], w3_ref[...].reshape(D_FF, D_MODEL), preferred_element_type=F32)
        dx, dg = _rms_bwd(df, h_ref[...], g_ref[...], dr_ref[...])
        dx_ref[...] = dx
        dxb_ref[...] = dx.astype(BF16)
        dg_ref[...] += dg

    return _call(
        body, name=name, grid=(seq // tm,),
        in_specs=[_rows(tm, D_FF), _rows(tm, D_FF), _ff_whole_spec(W1T_SLOT), _ff_whole_spec(W3T_SLOT),
                  _rows(tm, D_MODEL), _rows(tm, D_MODEL), _full((1, D_MODEL))],
        out_specs=[_rows(tm, D_MODEL), _rows(tm, D_MODEL), _full((1, D_MODEL))],
        out_shape=[jax.ShapeDtypeStruct((seq, D_MODEL), F32), jax.ShapeDtypeStruct((seq, D_MODEL), BF16),
                   jax.ShapeDtypeStruct((1, D_MODEL), F32)],
        sem=("arbitrary",))(du, dw, gathered, gathered, h_in, dres, g)


def _mm_tn(name, a, b, *, tk, into=None):
    seq, kdim = a.shape
    n = b.shape[1]
    tt = min((4 if b.dtype == BF16 else 2) * TOKEN_TILE, seq)
    n_t = seq // tt
    devs = tk // ROWS_FF

    def body(a_ref, b_ref, *rest):
        o_ref, acc = rest[-2:]
        t = pl.program_id(1)

        @pl.when(t == 0)
        def _():
            acc[...] = jnp.zeros_like(acc)

        acc[...] += _tn(a_ref[...].astype(BF16), b_ref[...].astype(BF16))

        @pl.when(t == n_t - 1)
        def _():
            out = acc[...].astype(BF16)
            o_ref[...] = out if into is None else out.reshape(devs, ROWS_FF, n)

    in_specs = [pl.BlockSpec((tt, tk), lambda k, t: (t, k)), pl.BlockSpec((tt, n), lambda k, t: (t, 0))]
    args = [a, b]
    aliases = None
    if into is None:
        out_spec = pl.BlockSpec((tk, n), lambda k, t: (k, 0))
        out_shape = jax.ShapeDtypeStruct((kdim, n), BF16)
    else:
        blocks, slot = into
        out_spec = pl.BlockSpec((devs, ROWS_FF, n), lambda k, t: (k, slot, 0))
        out_shape = jax.ShapeDtypeStruct((N_DEV, 3 * ROWS_FF, n), BF16)
        if blocks is not None:
            in_specs.append(_ANY)
            args.append(blocks)
            aliases = {2: 0}
    return _call(
        body, name=name, grid=(kdim // tk, n_t), in_specs=in_specs, out_specs=out_spec, out_shape=out_shape,
        scratch=[pltpu.VMEM((tk, n), F32)], sem=("parallel", "arbitrary"), aliases=aliases)(*args)


def _conv_bwd(dh, wpw2, dwc, a, lng, lnb, wd):
    seq = dh.shape[0]
    tm = min(TOKEN_TILE, seq)
    nt = seq // tm
    n_chunks = tm // CONV_CHUNK
    win = CONV_CHUNK + CONV_HALO
    halo_per_tile = tm // CONV_HALO

    def body(dh_ref, w_ref, dwc_ref, a_ref, ah_ref, lng_ref, lnb_ref, wd_ref,
             da_ref, dlg_ref, dlb_ref, dbdw_ref, dwd_ref, dbpw1_ref, dbpw2_ref,
             gbuf, dbuf, dglu, dwd_part):
        i = pl.program_id(0)
        r = nt - 1 - i

        @pl.when(i == 0)
        def _():
            dlg_ref[...] = jnp.zeros_like(dlg_ref)
            dlb_ref[...] = jnp.zeros_like(dlb_ref)
            dbdw_ref[...] = jnp.zeros_like(dbdw_ref)
            dbpw1_ref[...] = jnp.zeros_like(dbpw1_ref)
            dbpw2_ref[...] = jnp.zeros_like(dbpw2_ref)
            dwd_part[...] = jnp.zeros_like(dwd_part)
            dbuf[tm:tm + CONV_HALO, :] = jnp.zeros((CONV_HALO, D_MODEL), F32)

        @pl.when(i > 0)
        def _():
            dbuf[tm:tm + CONV_HALO, :] = dbuf[0:CONV_HALO, :]

        dhv = dh_ref[...]
        dbpw2_ref[...] += _sum_rows(dhv)
        dz = _nt(dhv.astype(BF16), w_ref[...])
        xv = dwc_ref[...]
        lg = lng_ref[...]
        mu = jnp.mean(xv, axis=-1, keepdims=True)
        xc = xv - mu
        rstd = lax.rsqrt(jnp.mean(xc * xc, axis=-1, keepdims=True) + LN_EPS)
        xhat = xc * rstd
        ln = xhat * lg + lnb_ref[...]
        sg = _sig(ln)
        dln = dz * (sg * (1.0 + ln * (1.0 - sg)))
        dlg_ref[...] += _sum_rows(dln * xhat)
        dlb_ref[...] += _sum_rows(dln)
        dxh = dln * lg
        ddw = rstd * (dxh - jnp.mean(dxh, axis=-1, keepdims=True)
                      - xhat * jnp.mean(dxh * xhat, axis=-1, keepdims=True))
        dbdw_ref[...] += _sum_rows(ddw)
        dbuf[0:tm, :] = ddw

        av = a_ref[...].astype(F32)
        a1 = av[:, :D_MODEL]
        s2 = _sig(av[:, D_MODEL:])
        gbuf[CONV_HALO:CONV_HALO + tm, :] = a1 * s2
        ah = ah_ref[...].astype(F32)
        gh = ah[:, :D_MODEL] * _sig(ah[:, D_MODEL:])
        gbuf[0:CONV_HALO, :] = jnp.where(r > 0, gh, 0.0)

        def chunk(c, carry):
            r0 = pl.multiple_of(c * CONV_CHUNK, CONV_CHUNK)
            for l in range(D_MODEL // LANES):
                ls = slice(l * LANES, (l + 1) * LANES)
                dw_ = dbuf[pl.ds(r0, win), ls]
                gw = gbuf[pl.ds(r0, win), ls]
                acc = jnp.zeros((CONV_CHUNK, LANES), F32)
                for s in range(SUBLANES):
                    ds_ = dw_ if s == 0 else pltpu.roll(dw_, win - s, 0)
                    for q in range(CONV_HALO // SUBLANES):
                        d = SUBLANES * q + s
                        if d < CONV_WIDTH:
                            acc = acc + wd_ref[d:d + 1, ls] * ds_[SUBLANES * q:SUBLANES * q + CONV_CHUNK]
                            lo = CONV_HALO - SUBLANES * q
                            prod = ds_[0:CONV_CHUNK] * gw[lo:lo + CONV_CHUNK]
                            dwd_part[d, :, ls] += jnp.sum(
                                prod.reshape(CONV_CHUNK // SUBLANES, SUBLANES, LANES), axis=0)
                dglu[pl.ds(r0, CONV_CHUNK), ls] = acc
            return carry

        lax.fori_loop(0, n_chunks, chunk, 0)

        dg_ = dglu[...]
        da1 = dg_ * s2
        da2 = dg_ * a1 * s2 * (1.0 - s2)
        da_ref[:, :D_MODEL] = da1.astype(BF16)
        da_ref[:, D_MODEL:] = da2.astype(BF16)
        dbpw1_ref[:, :D_MODEL] += _sum_rows(da1)
        dbpw1_ref[:, D_MODEL:] += _sum_rows(da2)

        @pl.when(i == nt - 1)
        def _():
            dwd_ref[...] = jnp.sum(dwd_part[...], axis=1)

    rev = lambda n: pl.BlockSpec((tm, n), lambda i: (nt - 1 - i, 0))
    halo = pl.BlockSpec((CONV_HALO, 2 * D_MODEL),
                        lambda i: (jnp.maximum((nt - 1 - i) * halo_per_tile - 1, 0), 0))
    vec = lambda n: _full((1, n))
    return _call(
        body, name="conv_bwd", grid=(nt,),
        in_specs=[rev(D_MODEL), _full((D_MODEL, D_MODEL)), rev(D_MODEL), rev(2 * D_MODEL), halo,
                  vec(D_MODEL), vec(D_MODEL), _full((TAPS_PAD, D_MODEL))],
        out_specs=[rev(2 * D_MODEL), vec(D_MODEL), vec(D_MODEL), vec(D_MODEL), _full((TAPS_PAD, D_MODEL)),
                   vec(2 * D_MODEL), vec(D_MODEL)],
        out_shape=[jax.ShapeDtypeStruct((seq, 2 * D_MODEL), BF16), jax.ShapeDtypeStruct((1, D_MODEL), F32),
                   jax.ShapeDtypeStruct((1, D_MODEL), F32), jax.ShapeDtypeStruct((1, D_MODEL), F32),
                   jax.ShapeDtypeStruct((TAPS_PAD, D_MODEL), F32), jax.ShapeDtypeStruct((1, 2 * D_MODEL), F32),
                   jax.ShapeDtypeStruct((1, D_MODEL), F32)],
        scratch=[pltpu.VMEM((tm + CONV_HALO, D_MODEL), F32), pltpu.VMEM((tm + CONV_HALO, D_MODEL), F32),
                 pltpu.VMEM((tm, D_MODEL), F32), pltpu.VMEM((TAPS_PAD, SUBLANES, D_MODEL), F32)],
        sem=("arbitrary",))(dh, wpw2, dwc, a, a, lng, lnb, wd)


def _mm_rms_bwd(name, dact, wt, h_in, dres, g, bf16_copy):
    seq, n = dact.shape
    tm = min(TOKEN_TILE, seq)

    def body(da_ref, w_ref, h_ref, dr_ref, g_ref, dx_ref, *rest):
        dg_ref = rest[-1]

        @pl.when(pl.program_id(0) == 0)
        def _():
            dg_ref[...] = jnp.zeros_like(dg_ref)

        dy = jnp.dot(da_ref[...], w_ref[...], preferred_element_type=F32)
        dx, dg = _rms_bwd(dy, h_ref[...], g_ref[...], dr_ref[...])
        dx_ref[...] = dx
        if bf16_copy:
            rest[0][...] = dx.astype(BF16)
        dg_ref[...] += dg

    copy_spec = [_rows(tm, D_MODEL)] if bf16_copy else []
    copy_shape = [jax.ShapeDtypeStruct((seq, D_MODEL), BF16)] if bf16_copy else []
    return _call(
        body, name=name, grid=(seq // tm,),
        in_specs=[_rows(tm, n), _resident((n, D_MODEL)), _rows(tm, D_MODEL), _rows(tm, D_MODEL), _full((1, D_MODEL))],
        out_specs=[_rows(tm, D_MODEL), *copy_spec, _full((1, D_MODEL))],
        out_shape=[jax.ShapeDtypeStruct((seq, D_MODEL), F32), *copy_shape, jax.ShapeDtypeStruct((1, D_MODEL), F32)],
        sem=("arbitrary",))(dact, wt, h_in, dres, g)


def _nt_bias(name, dy, w):
    seq, n = dy.shape
    kdim = w.shape[0]
    tm = min(2 * TOKEN_TILE, seq)

    def body(dy_ref, w_ref, o_ref, db_ref):
        @pl.when(pl.program_id(0) == 0)
        def _():
            db_ref[...] = jnp.zeros_like(db_ref)

        dyv = dy_ref[...]
        db_ref[...] += _sum_rows(dyv)
        o_ref[...] = _nt(dyv.astype(BF16), w_ref[...]).astype(BF16)

    return _call(
        body, name=name, grid=(seq // tm,),
        in_specs=[_rows(tm, n), _resident((kdim, n))],
        out_specs=[_rows(tm, kdim), _full((1, n))],
        out_shape=[jax.ShapeDtypeStruct((seq, kdim), BF16), jax.ShapeDtypeStruct((1, n), F32)],
        sem=("arbitrary",))(dy, w)


def _attn_bwd(probs, p_sinks, q, k, v, do):
    seq = q.shape[0]
    per = 2 if seq % (2 * BLOCK) == 0 else 1
    tile = per * BLOCK
    cur = pl.BlockSpec((tile, KV_DIM), lambda n: (n, 0))
    prev = pl.BlockSpec((BLOCK, KV_DIM), lambda n: (jnp.maximum(n * per - 1, 0), 0))

    def body(p_ref, ps_ref, q_ref, kc_ref, kp_ref, vc_ref, vp_ref, do_ref,
             dq_ref, dkc_ref, dkp_ref, dvc_ref, dvp_ref, dsink_ref):
        @pl.when(pl.program_id(0) == 0)
        def _():
            dsink_ref[...] = jnp.zeros_like(dsink_ref)

        low = _low_lanes()
        lane = lax.broadcasted_iota(jnp.int32, (BLOCK, LANES), 1)
        for b in range(per):
            rows = slice(b * BLOCK, (b + 1) * BLOCK)
            before = slice((b - 1) * BLOCK, b * BLOCK)
            k_prev, v_prev = (kp_ref[...], vp_ref[...]) if b == 0 else (kc_ref[before, :], vc_ref[before, :])
            k_cur, v_cur = kc_ref[rows, :], vc_ref[rows, :]
            p_sinks_blk = ps_ref[rows, :]
            dk_all = jnp.zeros((2 * BLOCK, KV_DIM), F32)
            dv_all = jnp.zeros((2 * BLOCK, KV_DIM), F32)
            for j in range(N_KV_HEADS):
                k_lo, k_hi = _kv_low_high(k_prev, k_cur, j, low)
                v_lo, v_hi = _kv_low_high(v_prev, v_cur, j, low)
                dk_acc = jnp.zeros((2 * BLOCK, KV_DIM), F32)
                dv_acc = jnp.zeros((2 * BLOCK, KV_DIM), F32)
                for i in range(PAIRS_PER_KV):
                    ls, h = _pair_lanes(j, i)
                    qp = q_ref[rows, ls]
                    dop = do_ref[rows, ls]
                    dsb, pb16 = [], []
                    for t, v_sel in enumerate((v_lo, v_hi)):
                        pb = p_ref[h + t, rows, :]
                        pf = pb.astype(F32)
                        p_sink = jnp.sum(jnp.where(lane == h + t, p_sinks_blk, 0.0), axis=-1, keepdims=True)
                        dp = _nt(dop, v_sel)
                        delta = jnp.sum(pf * dp, axis=-1, keepdims=True)
                        dsb.append((pf * (dp - delta)).astype(BF16))
                        pb16.append(pb)
                        dsink_ref[h + t:h + t + 1, :] += jnp.broadcast_to(_sum_rows(p_sink * delta), (1, LANES))
                    dq = (jnp.dot(dsb[0], k_lo, preferred_element_type=F32)
                          + jnp.dot(dsb[1], k_hi, preferred_element_type=F32))
                    dq_ref[rows, ls] = dq * (HEAD_DIM ** -0.5)
                    dk_acc = dk_acc + jnp.where(low, _tn(dsb[0], qp), _tn(dsb[1], qp))
                    dv_acc = dv_acc + jnp.where(low, _tn(pb16[0], dop), _tn(pb16[1], dop))
                dk_all = dk_all + _fold_pair_halves(dk_acc, j, low)
                dv_all = dv_all + _fold_pair_halves(dv_acc, j, low)
            dkp_ref[rows, :] = dk_all[:BLOCK]
            dkc_ref[rows, :] = dk_all[BLOCK:]
            dvp_ref[rows, :] = dv_all[:BLOCK]
            dvc_ref[rows, :] = dv_all[BLOCK:]

    kv_out = _rows(tile, KV_DIM)
    kv_shape = jax.ShapeDtypeStruct((seq, KV_DIM), F32)
    return _call(
        body, name="attn_bwd", grid=(seq // tile,),
        in_specs=[pl.BlockSpec((N_Q_HEADS, tile, 2 * BLOCK), lambda n: (0, n, 0)), _rows(tile, LANES),
                  _rows(tile, Q_DIM), cur, prev, cur, prev, _rows(tile, Q_DIM)],
        out_specs=[_rows(tile, Q_DIM), kv_out, kv_out, kv_out, kv_out, _full((N_Q_HEADS, LANES))],
        out_shape=[jax.ShapeDtypeStruct((seq, Q_DIM), F32), kv_shape, kv_shape, kv_shape, kv_shape,
                   jax.ShapeDtypeStruct((N_Q_HEADS, LANES), F32)],
        sem=("arbitrary",))(probs, p_sinks, q, k, k, v, v, do)


def _rope_bwd(dq, dkc, dkp, dvc, dvp, rc, rsa, rsb):
    seq = dq.shape[0]
    nb = seq // BLOCK
    tm = min(TOKEN_TILE, seq)
    nt = seq // tm
    per = tm // BLOCK
    nxt = pl.BlockSpec((BLOCK, KV_DIM), lambda i: (jnp.minimum((i + 1) * per, nb - 1), 0))

    def body(dq_ref, dkc_ref, dkp_ref, dkn_ref, dvc_ref, dvp_ref, dvn_ref, c_ref, sa_ref, sb_ref, o_ref, db_ref):
        i = pl.program_id(0)

        @pl.when(i == 0)
        def _():
            db_ref[...] = jnp.zeros_like(db_ref)

        c, sa, sb = c_ref[...], sa_ref[...], sb_ref[...]
        last = i == nt - 1

        def from_next_block(prev_ref, next_ref):
            tail = jnp.where(last, 0.0, next_ref[...])
            return tail if per == 1 else jnp.concatenate([prev_ref[BLOCK:, :], tail], axis=0)

        dk = dkc_ref[...] + from_next_block(dkp_ref, dkn_ref)
        dv = dvc_ref[...] + from_next_block(dvp_ref, dvn_ref)
        for l in range(Q_DIM // LANES):
            ls = slice(l * LANES, (l + 1) * LANES)
            blk = _rope_t(dq_ref[:, ls], c, sa, sb)
            o_ref[:, ls] = blk.astype(BF16)
            db_ref[:, ls] += _sum_rows(blk)
        dkr = _rope_t(dk, c, sa, sb)
        o_ref[:, Q_DIM:Q_DIM + KV_DIM] = dkr.astype(BF16)
        db_ref[:, Q_DIM:Q_DIM + KV_DIM] += _sum_rows(dkr)
        o_ref[:, Q_DIM + KV_DIM:] = dv.astype(BF16)
        db_ref[:, Q_DIM + KV_DIM:] += _sum_rows(dv)

    kv = _rows(tm, KV_DIM)
    tab = _rows(tm, LANES)
    return _call(
        body, name="rope_bwd", grid=(nt,),
        in_specs=[_rows(tm, Q_DIM), kv, kv, nxt, kv, kv, nxt, tab, tab, tab],
        out_specs=[_rows(tm, QKV_DIM), _full((1, QKV_DIM))],
        out_shape=[jax.ShapeDtypeStruct((seq, QKV_DIM), BF16), jax.ShapeDtypeStruct((1, QKV_DIM), F32)],
        sem=("arbitrary",))(dq, dkc, dkp, dkp, dvc, dvp, dvp, rc, rsa, rsb)


def _adamw(w, g, m, v):
    m = ADAM_B1 * m + (1.0 - ADAM_B1) * g
    v = ADAM_B2 * v + (1.0 - ADAM_B2) * (g * g)
    m_hat = m / (1.0 - ADAM_B1 ** ADAM_STEP)
    v_hat = v / (1.0 - ADAM_B2 ** ADAM_STEP)
    delta = -ADAM_LR * (m_hat / (jnp.sqrt(v_hat) + ADAM_EPS) + ADAM_WD * w)
    return delta, m, v


def _sum_slots(name, parts):
    _, rows, cols = parts.shape
    tr = rows if rows <= 512 else ROWS_FF

    def body(p_ref, g_ref):
        g = p_ref[0].astype(F32)
        for d in range(1, N_DEV):
            g = g + p_ref[d].astype(F32)
        g_ref[...] = g

    return _call(
        body, name=name, grid=(rows // tr,),
        in_specs=[pl.BlockSpec((N_DEV, tr, cols), lambda i: (0, i, 0))],
        out_specs=_rows(tr, cols), out_shape=jax.ShapeDtypeStruct((rows, cols), F32),
        sem=("parallel",))(parts)


def _adamw_native(name, g, w, m, v):
    layers, rows, cols = w.shape
    tr = rows if rows <= 512 else 256

    def body(g_ref, w_ref, m_ref, v_ref, d_ref, nm_ref, nv_ref):
        d_ref[...], nm_ref[...], nv_ref[...] = _adamw(w_ref[...], g_ref[...], m_ref[...], v_ref[...])

    spec = pl.BlockSpec((1, tr, cols), lambda l, i: (l, i, 0))
    shape = jax.ShapeDtypeStruct(w.shape, F32)
    return _call(
        body, name=name, grid=(layers, rows // tr), in_specs=[spec, spec, spec, spec],
        out_specs=[spec, spec, spec], out_shape=[shape, shape, shape],
        sem=("parallel", "parallel"))(g, w, m, v)


def _adamw_from_slots(name, slots, row_block, w, m, v):
    layers, rows, cols = w.shape

    def body(*refs):
        slot_refs = refs[:layers]
        w_ref, m_ref, v_ref, g_ref, d_ref, nm_ref, nv_ref = refs[layers:]
        for l in range(layers):
            @pl.when(pl.program_id(0) == l)
            def _(p_ref=slot_refs[l]):
                g = p_ref[0].astype(F32)
                for d in range(1, N_DEV):
                    g = g + p_ref[d].astype(F32)
                g_ref[0] = g
        d, nm, nv = _adamw(w_ref[0], g_ref[0], m_ref[0], v_ref[0])
        d_ref[0], nm_ref[0], nv_ref[0] = d, nm, nv

    slot_spec = pl.BlockSpec((N_DEV, rows, cols), lambda l: (0, row_block, 0))
    spec = pl.BlockSpec((1, rows, cols), lambda l: (l, 0, 0))
    shape = jax.ShapeDtypeStruct(w.shape, F32)
    return _call(
        body, name=name, grid=(layers,), in_specs=[slot_spec] * layers + [spec, spec, spec],
        out_specs=[spec, spec, spec, spec], out_shape=[shape, shape, shape, shape],
        sem=("arbitrary",))(*slots, w, m, v)


def _adamw_replicated(parts, w, m, v):
    def body(p_ref, w_ref, m_ref, v_ref, g_ref, d_ref, nm_ref, nv_ref):
        g = p_ref[0]
        for j in range(1, N_DEV):
            g = g + p_ref[j]
        g_ref[...] = g
        d_ref[...], nm_ref[...], nv_ref[...] = _adamw(w_ref[...], g, m_ref[...], v_ref[...])

    spec = _full((REPL_ROWS, D_MODEL))
    shape = jax.ShapeDtypeStruct((REPL_ROWS, D_MODEL), F32)
    return _call(
        body, name="adamw_replicated", grid=(1,),
        in_specs=[_full((N_DEV, REPL_ROWS, D_MODEL)), spec, spec, spec],
        out_specs=[spec, spec, spec, spec], out_shape=[shape, shape, shape, shape],
        sem=("arbitrary",))(parts, w, m, v)


_MESH = pl.DeviceIdType.MESH
_ANY = pl.BlockSpec(memory_space=pl.ANY)


def _all_gather(name, xs):
    rows, cols = xs.shape

    def body(x_ref, out_ref, send_sems, recv_sems, local_sem):
        x, y, c = lax.axis_index("x"), lax.axis_index("y"), lax.axis_index("c")
        me, sibling = (x, y, c), (x, y, 1 - c)
        chips = [(1 - x, y), (x, 1 - y), (1 - x, 1 - y)]

        def slot(px, py, pc):
            return out_ref.at[4 * px + 2 * py + pc]

        def copy(k, block, to, src=None):
            return pltpu.make_async_remote_copy(
                src_ref=slot(*block) if src is None else src, dst_ref=slot(*block),
                send_sem=send_sems.at[k], recv_sem=recv_sems.at[k], device_id=to, device_id_type=_MESH)

        mine = pltpu.make_async_copy(x_ref, slot(*me), local_sem)
        mine.start()
        first = [copy(0, me, sibling, src=x_ref)]
        first += [copy(1 + j, me, (*chip, c), src=x_ref) for j, chip in enumerate(chips)]
        for cp in first:
            cp.start()
        passed = [copy(4 + j, (*chip, c), sibling) for j, chip in enumerate(chips)]
        for j, chip in enumerate(chips):
            copy(1 + j, (*chip, c), me).wait_recv()
            passed[j].start()
        copy(0, sibling, me).wait_recv()
        for j, chip in enumerate(chips):
            copy(4 + j, (*chip, 1 - c), me).wait_recv()
        for cp in first + passed:
            cp.wait_send()
        mine.wait()

    return pl.pallas_call(
        body, name=name, out_shape=jax.ShapeDtypeStruct((N_DEV, rows, cols), xs.dtype),
        in_specs=[_ANY], out_specs=_ANY,
        scratch_shapes=[pltpu.SemaphoreType.DMA((7,)), pltpu.SemaphoreType.DMA((7,)), pltpu.SemaphoreType.DMA],
    )(xs)


N_PEERS = N_DEV - 1
_HBM = pl.BlockSpec(memory_space=pltpu.HBM)
_SEM = pl.BlockSpec(memory_space=pltpu.SEMAPHORE)
_DATAFLOW = pltpu.SideEffectType.DATAFLOW_SIDE_EFFECTING
_TOKEN = jax.ShapeDtypeStruct((SUBLANES, LANES), F32)


def _peers():
    x, y, c = lax.axis_index("x"), lax.axis_index("y"), lax.axis_index("c")
    out = []
    for k in range(1, N_DEV):
        px = 1 - x if k & 4 else x
        py = 1 - y if k & 2 else y
        pc = 1 - c if k & 1 else c
        out.append(((px, py, pc), 4 * px + 2 * py + pc))
    return 4 * x + 2 * y + c, out


def _in_hbm(a):
    return pltpu.with_memory_space_constraint(a, pltpu.HBM)


def _landing(rows):
    return _in_hbm(lax.empty((N_DEV, rows, D_MODEL), BF16))


SEMS_PER_GROUP = 3


def _group_sems():
    return pltpu.SemaphoreType.DMA((N_PEERS,)), pltpu.SemaphoreType.DMA((N_PEERS,)), pltpu.SemaphoreType.DMA(())


def _gather_start(name, payloads):
    n = len(payloads)
    ns = SEMS_PER_GROUP

    def body(*refs):
        src, land = refs[:n], refs[n:2 * n]
        sems = refs[2 * n:(2 + ns) * n]
        token = refs[-1]
        me, peers = _peers()
        for g in range(n):
            send_sems, recv_sems, own_sem = sems[ns * g:ns * (g + 1)]
            for k, (pos, _) in enumerate(peers):
                pltpu.make_async_remote_copy(
                    src_ref=src[g], dst_ref=land[g].at[me], send_sem=send_sems.at[k],
                    recv_sem=recv_sems.at[k], device_id=pos, device_id_type=_MESH).start()
            pltpu.make_async_copy(src[g], land[g].at[me], own_sem).start()
        token[...] = jnp.zeros_like(token)

    lands = [_landing(p.shape[0]) for p in payloads]
    sem_shapes = [s for _ in payloads for s in _group_sems()]
    hbm_shapes = [pltpu.HBM(a.shape, a.dtype) for a in list(payloads) + lands]
    out = pl.pallas_call(
        body, name=name, out_shape=(*sem_shapes, *hbm_shapes, _TOKEN),
        in_specs=[_HBM] * (2 * n),
        out_specs=(*[_SEM] * (ns * n), *[_HBM] * (2 * n), pl.BlockSpec(memory_space=pltpu.VMEM)),
        input_output_aliases={i: ns * n + i for i in range(2 * n)},
        compiler_params=pltpu.CompilerParams(has_side_effects=_DATAFLOW),
    )(*[_in_hbm(p) for p in payloads], *lands)
    sems, thru = out[:ns * n], out[ns * n:(ns + 2) * n]
    return [(thru[g], thru[n + g], *sems[ns * g:ns * (g + 1)]) for g in range(n)], out[-1]


def _gather_wait(name, group, after):
    payload, land, send_sems, recv_sems, own_sem = group

    def body(src_ref, land_ref, send_ref, recv_ref, own_ref, after_ref, src_out, land_out):
        me, peers = _peers()
        for k, (pos, idx) in enumerate(peers):
            cp = pltpu.make_async_remote_copy(
                src_ref=src_ref, dst_ref=land_ref.at[idx], send_sem=send_ref.at[k], recv_sem=recv_ref.at[k],
                device_id=pos, device_id_type=_MESH)
            cp.wait_send()
            cp.wait_recv()
        pltpu.make_async_copy(src_ref, land_ref.at[me], own_ref).wait()

    _, land = pl.pallas_call(
        body, name=name, out_shape=(pltpu.HBM(payload.shape, payload.dtype), pltpu.HBM(land.shape, land.dtype)),
        in_specs=[_HBM, _HBM, _SEM, _SEM, _SEM, _ANY], out_specs=(_HBM, _HBM), input_output_aliases={0: 0, 1: 1},
        compiler_params=pltpu.CompilerParams(has_side_effects=_DATAFLOW),
    )(payload, land, send_sems, recv_sems, own_sem, after)
    return land


def _scatter_start(name, blocks):
    rows = blocks.shape[1]

    def body(blocks_ref, land_ref, send_sems, recv_sems, own_sem, blocks_out, land_out, token):
        me, peers = _peers()
        for k, (pos, idx) in enumerate(peers):
            pltpu.make_async_remote_copy(
                src_ref=blocks_ref.at[idx], dst_ref=land_ref.at[me], send_sem=send_sems.at[k],
                recv_sem=recv_sems.at[k], device_id=pos, device_id_type=_MESH).start()
        pltpu.make_async_copy(blocks_ref.at[me], land_ref.at[me], own_sem).start()
        token[...] = jnp.zeros_like(token)

    land = _landing(rows)
    send_sems, recv_sems, own_sem, blocks_thru, land_thru, token = pl.pallas_call(
        body, name=name,
        out_shape=(*_group_sems(), pltpu.HBM(blocks.shape, blocks.dtype), pltpu.HBM(land.shape, land.dtype), _TOKEN),
        in_specs=[_HBM, _HBM], out_specs=(_SEM, _SEM, _SEM, _HBM, _HBM, pl.BlockSpec(memory_space=pltpu.VMEM)),
        input_output_aliases={0: 3, 1: 4},
        compiler_params=pltpu.CompilerParams(has_side_effects=_DATAFLOW),
    )(_in_hbm(blocks), land)
    return (blocks_thru, land_thru, send_sems, recv_sems, own_sem), token


def _scatter_wait(groups, after):
    n = len(groups)
    ns = SEMS_PER_GROUP

    def body(*refs):
        blocks, land = refs[:n], refs[n:2 * n]
        sems = refs[2 * n:(2 + ns) * n]
        me, peers = _peers()
        for g in range(n):
            send_sems, recv_sems, own_sem = sems[ns * g:ns * (g + 1)]
            for k, (pos, idx) in enumerate(peers):
                cp = pltpu.make_async_remote_copy(
                    src_ref=blocks[g].at[idx], dst_ref=land[g].at[idx], send_sem=send_sems.at[k],
                    recv_sem=recv_sems.at[k], device_id=pos, device_id_type=_MESH)
                cp.wait_send()
                cp.wait_recv()
            pltpu.make_async_copy(blocks[g].at[me], land[g].at[me], own_sem).wait()

    hbm = [grp[0] for grp in groups] + [grp[1] for grp in groups]
    sems = [s for grp in groups for s in grp[2:]]
    out = pl.pallas_call(
        body, name="rs_wait", out_shape=tuple(pltpu.HBM(a.shape, a.dtype) for a in hbm),
        in_specs=[_HBM] * (2 * n) + [_SEM] * (ns * n) + [_ANY], out_specs=tuple([_HBM] * (2 * n)),
        input_output_aliases={i: i for i in range(2 * n)},
        compiler_params=pltpu.CompilerParams(has_side_effects=_DATAFLOW),
    )(*hbm, *sems, after)
    return list(out[n:])


def _pad_rows(flat, rows):
    return jnp.pad(flat, (0, rows * D_MODEL - flat.shape[0])).reshape(rows, D_MODEL)


SMALL_NAMES = ("conv_b_pw1", "conv_w_dw", "conv_b_dw", "conv_ln_g", "conv_ln_b", "conv_b_pw2")


def _pack_small(p):
    flat = jnp.concatenate([p[n].reshape(-1) for n in SMALL_NAMES])
    return _pad_rows(flat, ROWS_SMALL).reshape(1, ROWS_SMALL, D_MODEL)


def _unpack_small(packed):
    flat = packed.reshape(-1)
    c = D_MODEL // N_DEV
    shapes = ((1, 2 * c), (1, CONV_WIDTH, c), (1, c), (1, c), (1, c), (1, c))
    out, o = {}, 0
    for n, shape in zip(SMALL_NAMES, shapes):
        size = shape[-1] * (shape[1] if len(shape) == 3 else 1)
        out[n] = flat[o:o + size].reshape(shape)
        o += size
    return out


def _gather_payloads(p):
    t = lambda a: jnp.swapaxes(a, -1, -2).astype(BF16)
    w1t, w3t, w2 = t(p["ffn_w1"]), t(p["ffn_w3"]), p["ffn_w2"].astype(BF16)
    bits = lax.bitcast_convert_type(_pack_small(p).reshape(-1)[:ROWS_SMALL * D_MODEL // 2], jnp.uint32)
    halves = [(bits >> 16).astype(jnp.uint16), (bits & 0xFFFF).astype(jnp.uint16)]
    small = lax.bitcast_convert_type(jnp.concatenate(halves), BF16).reshape(ROWS_SMALL, D_MODEL)
    conv = jnp.concatenate([t(p["conv_w_pw1"][0]), p["conv_w_pw2"][0].astype(BF16), small], axis=0)
    ffn = [jnp.concatenate([w1t[l], w3t[l], w2[l]], axis=0) for l in range(2)]
    return [t(p["attn_w_qkv"][0]), p["attn_w_o"][0].astype(BF16), ffn[0], conv, ffn[1]]


def _device_rows(land, lo, n):
    return land[:, lo:lo + n].reshape(N_DEV * n, D_MODEL)


def _unpack_conv(land):
    words = lax.bitcast_convert_type(land[:, ROWS_PW1 + ROWS_PW2:], jnp.uint16).astype(jnp.uint32)
    words = words.reshape(N_DEV, 2, ROWS_SMALL * D_MODEL // 2)
    small = lax.bitcast_convert_type((words[:, 0] << 16) | words[:, 1], F32)
    c = D_MODEL // N_DEV
    b_pw1 = small[:, :2 * c].reshape(1, 2 * D_MODEL)
    s = 2 * c
    w_dw = small[:, s:s + CONV_WIDTH * c].reshape(N_DEV, CONV_WIDTH, c).transpose(1, 0, 2).reshape(CONV_WIDTH, D_MODEL)
    s += CONV_WIDTH * c
    b_dw, ln_g, ln_b, b_pw2 = (small[:, s + i * c:s + (i + 1) * c].reshape(1, D_MODEL) for i in range(4))
    return dict(w_pw1_t=_device_rows(land, 0, ROWS_PW1), w_pw2=_device_rows(land, ROWS_PW1, ROWS_PW2),
                b_pw1=b_pw1, w_dw=w_dw, b_dw=b_dw, ln_g=ln_g, ln_b=ln_b, b_pw2=b_pw2)


def _dest_blocks(mats):
    return jnp.concatenate([a.reshape(N_DEV, -1, D_MODEL) for a in mats], axis=1)


def _small_grad_rows(g_bpw1, g_dw, g_bdw, g_lng, g_lnb, g_bpw2):
    c = D_MODEL // N_DEV
    small = jnp.concatenate(
        [g_bpw1.reshape(N_DEV, 2 * c), g_dw.reshape(CONV_WIDTH, N_DEV, c).transpose(1, 0, 2).reshape(N_DEV, -1),
         g_bdw.reshape(N_DEV, c), g_lng.reshape(N_DEV, c), g_lnb.reshape(N_DEV, c), g_bpw2.reshape(N_DEV, c)], axis=1)
    small = jnp.pad(small, ((0, 0), (0, ROWS_SMALL * D_MODEL - SMALL_USED)))
    return small.reshape(N_DEV * ROWS_SMALL, D_MODEL).astype(BF16)


LOSS_ROW = 9


def _pack_replicated(norm_mix, norm_ffn, b_qkv, sinks, b_o, norm_final, extra=None):
    rows = [norm_mix.reshape(2, D_MODEL), norm_ffn.reshape(2, D_MODEL), _pad_rows(b_qkv.reshape(-1), 2),
            _pad_rows(sinks.reshape(-1), 1), b_o.reshape(1, D_MODEL), norm_final.reshape(1, D_MODEL)]
    if extra is not None:
        rows.append(_pad_rows(extra.reshape(-1), 1))
    p = jnp.concatenate(rows, axis=0)
    return jnp.pad(p, ((0, REPL_ROWS - p.shape[0]), (0, 0)))


def _unpack_replicated(p):
    return dict(norm_mix=p[0:2], norm_ffn=p[2:4], attn_b_qkv=p[4:6].reshape(-1)[:QKV_DIM].reshape(1, QKV_DIM),
                attn_sinks=p[6, :N_Q_HEADS].reshape(1, N_Q_HEADS), attn_b_o=p[7:8], norm_final=p[8])


WEIGHT_ORDER = ['norm_mix', 'norm_ffn', 'attn_w_qkv', 'attn_b_qkv', 'attn_sinks', 'attn_w_o', 'attn_b_o',
                'conv_w_pw1', 'conv_b_pw1', 'conv_w_dw', 'conv_b_dw', 'conv_ln_g', 'conv_ln_b', 'conv_w_pw2',
                'conv_b_pw2', 'ffn_w1', 'ffn_w3', 'ffn_w2', 'norm_final']


def kernel(x, norm_mix, norm_ffn, attn_w_qkv, attn_b_qkv, attn_sinks, attn_w_o, attn_b_o, conv_w_pw1, conv_b_pw1, conv_w_dw, conv_b_dw, conv_ln_g, conv_ln_b, conv_w_pw2, conv_b_pw2, ffn_w1, ffn_w3, ffn_w2, norm_final, loss_target, m_norm_mix, m_norm_ffn, m_attn_w_qkv, m_attn_b_qkv, m_attn_sinks, m_attn_w_o, m_attn_b_o, m_conv_w_pw1, m_conv_b_pw1, m_conv_w_dw, m_conv_b_dw, m_conv_ln_g, m_conv_ln_b, m_conv_w_pw2, m_conv_b_pw2, m_ffn_w1, m_ffn_w3, m_ffn_w2, m_norm_final, v_norm_mix, v_norm_ffn, v_attn_w_qkv, v_attn_b_qkv, v_attn_sinks, v_attn_w_o, v_attn_b_o, v_conv_w_pw1, v_conv_b_pw1, v_conv_w_dw, v_conv_b_dw, v_conv_ln_g, v_conv_ln_b, v_conv_w_pw2, v_conv_b_pw2, v_ffn_w1, v_ffn_w3, v_ffn_w2, v_norm_final):
    xs = x[0]
    target = loss_target[0]
    seq = xs.shape[0]

    w = dict(attn_w_qkv=attn_w_qkv, attn_w_o=attn_w_o, conv_w_pw1=conv_w_pw1, conv_b_pw1=conv_b_pw1,
             conv_w_dw=conv_w_dw, conv_b_dw=conv_b_dw, conv_ln_g=conv_ln_g, conv_ln_b=conv_ln_b,
             conv_w_pw2=conv_w_pw2, conv_b_pw2=conv_b_pw2, ffn_w1=ffn_w1, ffn_w3=ffn_w3, ffn_w2=ffn_w2)
    m = dict(attn_w_qkv=m_attn_w_qkv, attn_w_o=m_attn_w_o, conv_w_pw1=m_conv_w_pw1, conv_b_pw1=m_conv_b_pw1,
             conv_w_dw=m_conv_w_dw, conv_b_dw=m_conv_b_dw, conv_ln_g=m_conv_ln_g, conv_ln_b=m_conv_ln_b,
             conv_w_pw2=m_conv_w_pw2, conv_b_pw2=m_conv_b_pw2, ffn_w1=m_ffn_w1, ffn_w3=m_ffn_w3, ffn_w2=m_ffn_w2)
    v = dict(attn_w_qkv=v_attn_w_qkv, attn_w_o=v_attn_w_o, conv_w_pw1=v_conv_w_pw1, conv_b_pw1=v_conv_b_pw1,
             conv_w_dw=v_conv_w_dw, conv_b_dw=v_conv_b_dw, conv_ln_g=v_conv_ln_g, conv_ln_b=v_conv_ln_b,
             conv_w_pw2=v_conv_w_pw2, conv_b_pw2=v_conv_b_pw2, ffn_w1=v_ffn_w1, ffn_w3=v_ffn_w3, ffn_w2=v_ffn_w2)

    payloads = _gather_payloads(w)
    (ag_qkv, ag_wo), tok = _gather_start("ag_start_attn", payloads[:2])
    (ag_ffn0, ag_conv, ag_ffn1), tok = _gather_start(
        "ag_start_rest", [payloads[2] + tok[0, 0].astype(BF16), payloads[3], payloads[4]])
    rc, rsa, rsb = _rope_tables(seq)
    sinks = attn_sinks.reshape(N_Q_HEADS)
    g_mix0, g_mix1 = norm_mix[0:1] + tok[0, 0], norm_mix[1:2]
    g_ffn0, g_ffn1 = norm_ffn[0:1], norm_ffn[1:2]
    g_fin = norm_final.reshape(1, D_MODEL)

    y0 = _rms_fwd(xs, g_mix0)
    w_qkv_t = _gather_wait("ag_wait_qkv", ag_qkv, y0).reshape(QKV_DIM, D_MODEL)
    q, k, vv = _qkv_fwd(y0, w_qkv_t, attn_b_qkv, rc, rsa, rsb)
    attn, probs, p_sinks = _attn_fwd(sinks, q, k, vv)
    w_o = _gather_wait("ag_wait_wo", ag_wo, attn).reshape(Q_DIM, D_MODEL)
    h1 = _mm_res("attn_out_proj", attn, w_o, attn_b_o, xs)
    w_ffn0 = _gather_wait("ag_wait_ffn0", ag_ffn0, h1)
    f0, u0, p0, s0 = _ffn_up("ffn0_up", h1, g_ffn0, w_ffn0)
    h2 = _ffn_down("ffn0_down", s0, w_ffn0, h1)
    wt = _unpack_conv(_gather_wait("ag_wait_conv", ag_conv, h2))
    wd = jnp.concatenate([wt["w_dw"][::-1], jnp.zeros((TAPS_PAD - CONV_WIDTH, D_MODEL), F32)], axis=0)
    y1, a, dwc, z = _conv_fwd(h2, g_mix1, wt["w_pw1_t"], wt["b_pw1"], wd, wt["b_dw"], wt["ln_g"], wt["ln_b"])
    h3 = _mm_res("conv_out_proj", z, wt["w_pw2"], wt["b_pw2"], h2)
    w_ffn1 = _gather_wait("ag_wait_ffn1", ag_ffn1, h3)
    f1, u1, p1, s1 = _ffn_up("ffn1_up", h3, g_ffn1, w_ffn1)
    dh4, dh4b, sq, dg_fin = _ffn_down_loss(s1, w_ffn1, h3, target, g_fin)

    du1, dp1 = _ffn_bwd_act("ffn1_bwd_act", dh4b, u1, p1, w_ffn1)
    dh3, dh3b, dg_ffn1 = _ffn_bwd_in("ffn1_bwd_in", du1, dp1, w_ffn1, h3, dh4, g_ffn1)
    blocks = _mm_tn("ffn1_dw2", s1, dh4b, tk=FF_TILE, into=(None, W2_SLOT))
    blocks = _mm_tn("ffn1_dw1", du1, f1, tk=FF_TILE, into=(blocks, W1T_SLOT))
    blocks = _mm_tn("ffn1_dw3", dp1, f1, tk=FF_TILE, into=(blocks, W3T_SLOT))
    rs_ffn1, tok = _scatter_start("rs_start_ffn1", blocks)

    da, dlg, dlb, dbdw, dwd, dbpw1, dbpw2 = _conv_bwd(dh3, wt["w_pw2"], dwc, a, wt["ln_g"] + tok[0, 0],
                                                     wt["ln_b"], wd)
    gpw2 = _mm_tn("conv_dw_pw2", z, dh3b, tk=D_MODEL)
    dh2, dh2b, dg_mix1 = _mm_rms_bwd("conv_in_bwd", da, wt["w_pw1_t"], h2, dh3, g_mix1, True)
    gpw1t = _mm_tn("conv_dw_pw1", da, y1, tk=D_MODEL)
    small_rows = _small_grad_rows(dbpw1, dwd[:CONV_WIDTH][::-1], dbdw, dlg, dlb, dbpw2)
    rs_conv, tok = _scatter_start("rs_start_conv", _dest_blocks([gpw1t, gpw2, small_rows]))

    du0, dp0 = _ffn_bwd_act("ffn0_bwd_act", dh2b, u0, p0, w_ffn0)
    dh1, dh1b, dg_ffn0 = _ffn_bwd_in("ffn0_bwd_in", du0, dp0, w_ffn0, h1, dh2, g_ffn0 + tok[0, 0])
    blocks = _mm_tn("ffn0_dw2", s0, dh2b, tk=FF_TILE, into=(None, W2_SLOT))
    blocks = _mm_tn("ffn0_dw1", du0, f0, tk=FF_TILE, into=(blocks, W1T_SLOT))
    blocks = _mm_tn("ffn0_dw3", dp0, f0, tk=FF_TILE, into=(blocks, W3T_SLOT))
    rs_ffn0, tok = _scatter_start("rs_start_ffn0", blocks)

    gwo = _mm_tn("attn_dw_o", attn, dh1b, tk=D_MODEL)
    rs_wo, tok2 = _scatter_start("rs_start_wo", _dest_blocks([gwo]))
    dattn, dbo = _nt_bias("attn_out_bwd", dh1, w_o)
    dq, dkc, dkp, dvc, dvp, dsink = _attn_bwd(probs, p_sinks + (tok[0, 0] + tok2[0, 0]), q, k, vv, dattn)
    dqkv, dbqkv = _rope_bwd(dq, dkc, dkp, dvc, dvp, rc, rsa, rsb)
    gqkvt = _mm_tn("attn_dw_qkv", dqkv, y0, tk=QKV_DIM)
    rs_qkv, tok = _scatter_start("rs_start_qkv", _dest_blocks([gqkvt]))
    dx, dg_mix0 = _mm_rms_bwd("qkv_in_bwd", dqkv, w_qkv_t, xs, dh1, g_mix0 + tok[0, 0], False)

    p_ffn1, p_conv, p_ffn0, p_wo, p_qkv = _scatter_wait([rs_ffn1, rs_conv, rs_ffn0, rs_wo, rs_qkv], dx)

    tr = lambda a: jnp.swapaxes(a, -1, -2)
    same = lambda a: a
    ffn_slots = [p_ffn0, p_ffn1]
    plan = dict(
        ffn_w1=(ffn_slots, W1T_SLOT, tr), ffn_w3=(ffn_slots, W3T_SLOT, tr), ffn_w2=(ffn_slots, W2_SLOT, same),
        attn_w_qkv=([p_qkv], 0, tr), attn_w_o=([p_wo], 0, same),
        conv_w_pw2=([p_conv], ROWS_PW1 // ROWS_PW2, same))
    sharded = [{}, {}, {}, {}]
    for n, (slots, row_block, view) in plan.items():
        outs_n = _adamw_from_slots("adamw_" + n, slots, row_block, view(w[n]), view(m[n]), view(v[n]))
        for dst, t in zip(sharded, outs_n):
            dst[n] = view(t)
    small_out = _adamw_from_slots("adamw_small", [p_conv], (ROWS_PW1 + ROWS_PW2) // ROWS_SMALL,
                                  _pack_small(w), _pack_small(m), _pack_small(v))
    for dst, t in zip(sharded, small_out):
        dst.update(_unpack_small(t))
    g_pw1 = _sum_slots("rs_sum_pw1", p_conv[:, :ROWS_PW1]).T[None]
    pw1_out = _adamw_native("adamw_conv_w_pw1", g_pw1, w["conv_w_pw1"], m["conv_w_pw1"], v["conv_w_pw1"])
    for dst, t in zip(sharded, (g_pw1,) + tuple(pw1_out)):
        dst["conv_w_pw1"] = t

    part = _pack_replicated(jnp.concatenate([dg_mix0, dg_mix1]), jnp.concatenate([dg_ffn0, dg_ffn1]),
                            dbqkv, -dsink[:, 0], dbo, dg_fin, extra=sq[0, 0:1])
    parts = _all_gather("ag_replicated_grads", part)
    w_rep = _pack_replicated(norm_mix, norm_ffn, attn_b_qkv, attn_sinks, attn_b_o, norm_final)
    m_rep = _pack_replicated(m_norm_mix, m_norm_ffn, m_attn_b_qkv, m_attn_sinks, m_attn_b_o, m_norm_final)
    v_rep = _pack_replicated(v_norm_mix, v_norm_ffn, v_attn_b_qkv, v_attn_sinks, v_attn_b_o, v_norm_final)
    rep_out = _adamw_replicated(parts, w_rep, m_rep, v_rep)
    replicated = [_unpack_replicated(t) for t in rep_out]
    loss = rep_out[0][LOSS_ROW, 0] * (0.5 / D_MODEL)

    outs = [loss, dx.reshape(1, seq, D_MODEL)]
    for sh, rp in zip(sharded, replicated):
        merged = {**sh, **rp}
        outs += [merged[n] for n in WEIGHT_ORDER]
    return tuple(outs)
```

```python
import jax
import jax.numpy as jnp
from jax import lax
from jax.experimental import pallas as pl
from jax.experimental.pallas import tpu as pltpu

F32 = jnp.float32
BF16 = jnp.bfloat16

D_MODEL = 1024
HEAD_DIM = 64
N_Q_HEADS = 16
N_KV_HEADS = 2
Q_PER_KV = 8
Q_DIM = N_Q_HEADS * HEAD_DIM
KV_DIM = N_KV_HEADS * HEAD_DIM
QKV_DIM = Q_DIM + 2 * KV_DIM
BLOCK = 128
CONV_WIDTH = 31
D_FF = 2816
ROPE_THETA = 10000.0
RMS_EPS = 1e-5
LN_EPS = 1e-5
ADAM_LR = 0.001
ADAM_B1 = 0.9
ADAM_B2 = 0.999
ADAM_EPS = 1e-08
ADAM_WD = 0.01
ADAM_STEP = 10
N_DEV = 8

LANES = 128
SUBLANES = 8
TOKEN_TILE = 512
CONV_CHUNK = 64
CONV_HALO = 32
TAPS_PAD = 32
VMEM_LIMIT = 56 * 1024 * 1024
NEG_INF = float(jnp.finfo(jnp.float32).min)

ROWS_FF = D_FF // N_DEV
W1T_SLOT, W3T_SLOT, W2_SLOT = 0, 1, 2
ROWS_PW1 = 2 * D_MODEL // N_DEV
ROWS_PW2 = D_MODEL // N_DEV
ROWS_SMALL = 16
SMALL_USED = 2 * D_MODEL // N_DEV + CONV_WIDTH * (D_MODEL // N_DEV) + 4 * (D_MODEL // N_DEV)
FF_SPLIT = 2
FF_TILE_DEVS = N_DEV // FF_SPLIT
FF_TILE = FF_TILE_DEVS * ROWS_FF
REPL_ROWS = 16


def _call(body, *, name, grid, in_specs, out_specs, out_shape, scratch=(), sem=None, aliases=None):
    return pl.pallas_call(
        body, name=name, grid=grid, in_specs=in_specs, out_specs=out_specs, out_shape=out_shape,
        scratch_shapes=list(scratch), input_output_aliases=aliases or {},
        compiler_params=pltpu.CompilerParams(dimension_semantics=sem, vmem_limit_bytes=VMEM_LIMIT))


def _full(shape):
    return pl.BlockSpec(shape, lambda *_: (0,) * len(shape))


def _resident(shape):
    return pl.BlockSpec(shape, lambda *_: (0,) * len(shape), pipeline_mode=pl.Buffered(1))


def _rows(tm, n):
    return pl.BlockSpec((tm, n), lambda i, *_: (i, 0))


def _sig(x):
    return 1.0 / (1.0 + jnp.exp(-x))


def _sum_rows(x):
    return jnp.sum(x, axis=0, keepdims=True)


def _nt(a, b):
    return lax.dot_general(a, b, (((1,), (1,)), ((), ())), preferred_element_type=F32)


def _tn(a, b):
    return lax.dot_general(a, b, (((0,), (0,)), ((), ())), preferred_element_type=F32)


def _rms_stats(x):
    return lax.rsqrt(jnp.mean(x * x, axis=-1, keepdims=True) + RMS_EPS)


def _rms_bwd(dy, x, g, dres):
    r = _rms_stats(x)
    n = x * r
    dn = dy * g
    dx = dres + r * (dn - n * jnp.mean(dn * n, axis=-1, keepdims=True))
    return dx, _sum_rows(dy * n)


def _rope_tables(seq):
    half = HEAD_DIM // 2
    pos = jnp.arange(seq, dtype=F32)
    inv_freq = ROPE_THETA ** (-jnp.arange(0, HEAD_DIM, 2, dtype=F32) / HEAD_DIM)
    ang = pos[:, None] * inv_freq[None, :]
    spread = lambda t: jnp.broadcast_to(t[:, None, :], (seq, LANES // half, half)).reshape(seq, LANES)
    return spread(jnp.cos(ang)), spread(jnp.sin(ang))


def _signed_sin(sin):
    lane = lax.broadcasted_iota(jnp.int32, sin.shape, 1)
    first_half = (lane & (HEAD_DIM - 1)) < HEAD_DIM // 2
    return jnp.where(first_half, -sin, 0.0), jnp.where(first_half, 0.0, sin)


def _rope(t, c, sa, sb):
    half = HEAD_DIM // 2
    return t * c + pltpu.roll(t, LANES - half, 1) * sa + pltpu.roll(t, half, 1) * sb


def _rope_t(dt, c, sa, sb):
    half = HEAD_DIM // 2
    return dt * c + pltpu.roll(dt * sa, half, 1) + pltpu.roll(dt * sb, LANES - half, 1)


def _rms_fwd(x, g):
    seq = x.shape[0]
    tm = min(2 * TOKEN_TILE, seq)

    def body(x_ref, g_ref, y_ref):
        xv = x_ref[...]
        y_ref[...] = (xv * _rms_stats(xv) * g_ref[...]).astype(BF16)

    return _call(
        body, name="rms_fwd", grid=(seq // tm,), in_specs=[_rows(tm, D_MODEL), _full((1, D_MODEL))],
        out_specs=_rows(tm, D_MODEL), out_shape=jax.ShapeDtypeStruct((seq, D_MODEL), BF16),
        sem=("parallel",))(x, g)


def _qkv_fwd(y, w, b, rc, rs):
    seq = y.shape[0]
    tm = min(TOKEN_TILE, seq)

    def body(y_ref, w_ref, b_ref, c_ref, s_ref, q_ref, k_ref, v_ref):
        qkv = _nt(y_ref[...], w_ref[...]) + b_ref[...]
        c = c_ref[...]
        sa, sb = _signed_sin(s_ref[...])
        for i in range(Q_DIM // LANES):
            blk = _rope(qkv[:, i * LANES:(i + 1) * LANES], c, sa, sb)
            q_ref[:, i * LANES:(i + 1) * LANES] = (blk * (HEAD_DIM ** -0.5)).astype(BF16)
        k_ref[...] = _rope(qkv[:, Q_DIM:Q_DIM + KV_DIM], c, sa, sb).astype(BF16)
        v_ref[...] = qkv[:, Q_DIM + KV_DIM:].astype(BF16)

    return _call(
        body, name="qkv_fwd", grid=(seq // tm,),
        in_specs=[_rows(tm, D_MODEL), _resident((QKV_DIM, D_MODEL)), _full((1, QKV_DIM)),
                  _rows(tm, LANES), _rows(tm, LANES)],
        out_specs=[_rows(tm, Q_DIM), _rows(tm, KV_DIM), _rows(tm, KV_DIM)],
        out_shape=[jax.ShapeDtypeStruct((seq, Q_DIM), BF16),
                   jax.ShapeDtypeStruct((seq, KV_DIM), BF16), jax.ShapeDtypeStruct((seq, KV_DIM), BF16)],
        sem=("parallel",))(y, w, b, rc, rs)


def _band_mask(n):
    row = lax.broadcasted_iota(jnp.int32, (BLOCK, 2 * BLOCK), 0)
    col = lax.broadcasted_iota(jnp.int32, (BLOCK, 2 * BLOCK), 1)
    rel = row + BLOCK - col
    return (rel >= 0) & (rel < BLOCK) & ((col >= BLOCK) | (n > 0))


def _softmax_with_sink(s, mask, sink):
    s = jnp.where(mask, s, NEG_INF)
    m = jnp.maximum(jnp.max(s, axis=-1, keepdims=True), sink)
    p = jnp.exp(s - m)
    e_sink = jnp.exp(sink - m)
    inv = 1.0 / (jnp.sum(p, axis=-1, keepdims=True) + e_sink)
    return p * inv, e_sink * inv


PAIRS_PER_KV = Q_PER_KV // 2


def _kv_specs():
    cur = pl.BlockSpec((BLOCK, KV_DIM), lambda n: (n, 0))
    prev = pl.BlockSpec((BLOCK, KV_DIM), lambda n: (jnp.maximum(n - 1, 0), 0))
    return cur, prev


def _low_lanes():
    return lax.broadcasted_iota(jnp.int32, (2 * BLOCK, KV_DIM), 1) < HEAD_DIM


def _kv_low_high(prev, cur, j, low):
    both = jnp.concatenate([prev, cur], axis=0).astype(F32)
    swapped = pltpu.roll(both, HEAD_DIM, 1)
    at_low, at_high = (both, swapped) if j == 0 else (swapped, both)
    return jnp.where(low, at_low, 0.0).astype(BF16), jnp.where(low, 0.0, at_high).astype(BF16)


def _fold_pair_halves(acc, j, low):
    folded = acc + pltpu.roll(acc, HEAD_DIM, 1)
    return jnp.where(low, folded, 0.0) if j == 0 else jnp.where(low, 0.0, folded)


def _pair_lanes(j, i):
    g = j * PAIRS_PER_KV + i
    return slice(g * LANES, (g + 1) * LANES), 2 * g


def _attn_fwd(sinks, q, k, v):
    seq = q.shape[0]
    cur, prev = _kv_specs()

    def body(sink_ref, q_ref, kc_ref, kp_ref, vc_ref, vp_ref, o_ref, p_ref, ps_ref):
        mask = _band_mask(pl.program_id(0))
        lane = lax.broadcasted_iota(jnp.int32, (BLOCK, LANES), 1)
        p_sinks = jnp.zeros((BLOCK, LANES), F32)
        for j in range(N_KV_HEADS):
            cs = slice(j * HEAD_DIM, (j + 1) * HEAD_DIM)
            kk = jnp.concatenate([kp_ref[:, cs], kc_ref[:, cs]], axis=0)
            vv = jnp.concatenate([vp_ref[:, cs], vc_ref[:, cs]], axis=0)
            for gq in range(Q_PER_KV):
                h = j * Q_PER_KV + gq
                hs = slice(h * HEAD_DIM, (h + 1) * HEAD_DIM)
                probs, p_sink = _softmax_with_sink(_nt(q_ref[:, hs], kk), mask, sink_ref[h])
                pb = probs.astype(BF16)
                p_ref[h] = pb
                p_sinks = jnp.where(lane == h, p_sink, p_sinks)
                o_ref[:, hs] = jnp.dot(pb, vv, preferred_element_type=F32).astype(BF16)
        ps_ref[...] = p_sinks

    return _call(
        body, name="attn_fwd", grid=(seq // BLOCK,),
        in_specs=[pl.BlockSpec(memory_space=pltpu.SMEM), _rows(BLOCK, Q_DIM), cur, prev, cur, prev],
        out_specs=[_rows(BLOCK, Q_DIM), pl.BlockSpec((N_Q_HEADS, BLOCK, 2 * BLOCK), lambda n: (0, n, 0)),
                   _rows(BLOCK, LANES)],
        out_shape=[jax.ShapeDtypeStruct((seq, Q_DIM), BF16),
                   jax.ShapeDtypeStruct((N_Q_HEADS, seq, 2 * BLOCK), BF16),
                   jax.ShapeDtypeStruct((seq, LANES), F32)],
        sem=("parallel",))(sinks, q, k, k, v, v)


def _mm_res(name, a, w, b, res):
    seq, kdim = a.shape
    n = w.shape[1]
    tm = min(2 * TOKEN_TILE, seq)

    def body(a_ref, w_ref, b_ref, r_ref, o_ref):
        o_ref[...] = r_ref[...] + (jnp.dot(a_ref[...], w_ref[...], preferred_element_type=F32) + b_ref[...])

    return _call(
        body, name=name, grid=(seq // tm,),
        in_specs=[_rows(tm, kdim), _resident((kdim, n)), _full((1, n)), _rows(tm, n)],
        out_specs=_rows(tm, n), out_shape=jax.ShapeDtypeStruct((seq, n), F32),
        sem=("parallel",))(a, w, b, res)


def _ff_tile_spec(slot):
    return pl.BlockSpec((FF_TILE_DEVS, ROWS_FF, D_MODEL), lambda j, i: (j, slot, 0))


def _ff_whole_spec(slot):
    return pl.BlockSpec((N_DEV, ROWS_FF, D_MODEL), lambda i: (0, slot, 0), pipeline_mode=pl.Buffered(1))


def _ffn_up(name, h, g, gathered):
    seq = h.shape[0]
    tm = min(TOKEN_TILE, seq)

    def body(h_ref, g_ref, w1_ref, w3_ref, f_ref, u_ref, w_ref, s_ref):
        hv = h_ref[...]
        f = (hv * _rms_stats(hv) * g_ref[...]).astype(BF16)
        f_ref[...] = f
        u = _nt(f, w1_ref[...].reshape(FF_TILE, D_MODEL))
        w = _nt(f, w3_ref[...].reshape(FF_TILE, D_MODEL))
        u_ref[...] = u.astype(BF16)
        w_ref[...] = w.astype(BF16)
        s_ref[...] = (u * _sig(u) * w).astype(BF16)

    n_t = seq // tm
    tile_ff = pl.BlockSpec((tm, FF_TILE), lambda j, i: (i, j))
    f_spec = pl.BlockSpec((tm, D_MODEL), lambda j, i: (jnp.where(j == 0, i, n_t), 0))
    ff_shape = jax.ShapeDtypeStruct((seq, D_FF), BF16)
    return _call(
        body, name=name, grid=(FF_SPLIT, n_t),
        in_specs=[pl.BlockSpec((tm, D_MODEL), lambda j, i: (i, 0)), _full((1, D_MODEL)),
                  _ff_tile_spec(W1T_SLOT), _ff_tile_spec(W3T_SLOT)],
        out_specs=[f_spec, tile_ff, tile_ff, tile_ff],
        out_shape=[jax.ShapeDtypeStruct((seq + tm, D_MODEL), BF16), ff_shape, ff_shape, ff_shape],
        sem=("arbitrary", "arbitrary"))(h, g, gathered, gathered)


def _ffn_down(name, s, gathered, res):
    seq = s.shape[0]
    tm = min(2 * TOKEN_TILE, seq)

    def body(s_ref, w_ref, r_ref, o_ref):
        w2 = w_ref[...].reshape(D_FF, D_MODEL)
        o_ref[...] = r_ref[...] + jnp.dot(s_ref[...], w2, preferred_element_type=F32)

    return _call(
        body, name=name, grid=(seq // tm,),
        in_specs=[_rows(tm, D_FF), _ff_whole_spec(W2_SLOT), _rows(tm, D_MODEL)],
        out_specs=_rows(tm, D_MODEL), out_shape=jax.ShapeDtypeStruct((seq, D_MODEL), F32),
        sem=("parallel",))(s, gathered, res)


def _conv_fwd(h, g, wpw1, bpw1, wd, bdw, lng, lnb):
    seq = h.shape[0]
    tm = min(TOKEN_TILE, seq)
    n_chunks = tm // CONV_CHUNK
    win = CONV_CHUNK + CONV_HALO

    def body(h_ref, g_ref, w_ref, b_ref, wd_ref, bdw_ref, lng_ref, lnb_ref,
             y_ref, a_ref, dwc_ref, z_ref, gbuf):
        i = pl.program_id(0)

        @pl.when(i == 0)
        def _():
            gbuf[0:CONV_HALO, :] = jnp.zeros((CONV_HALO, D_MODEL), F32)

        @pl.when(i > 0)
        def _():
            gbuf[0:CONV_HALO, :] = gbuf[tm:tm + CONV_HALO, :]

        hv = h_ref[...]
        y = (hv * _rms_stats(hv) * g_ref[...]).astype(BF16)
        y_ref[...] = y
        a = _nt(y, w_ref[...]) + b_ref[...]
        a_ref[...] = a.astype(BF16)
        gbuf[CONV_HALO:CONV_HALO + tm, :] = a[:, :D_MODEL] * _sig(a[:, D_MODEL:])

        def chunk(c, carry):
            r0 = pl.multiple_of(c * CONV_CHUNK, CONV_CHUNK)
            for l in range(D_MODEL // LANES):
                ls = slice(l * LANES, (l + 1) * LANES)
                gw = gbuf[pl.ds(r0, win), ls]
                acc = jnp.zeros((CONV_CHUNK, LANES), F32) + bdw_ref[:, ls]
                for s in range(SUBLANES):
                    gs = gw if s == 0 else pltpu.roll(gw, s, 0)
                    for q in range(CONV_HALO // SUBLANES):
                        d = SUBLANES * q + s
                        if d < CONV_WIDTH:
                            lo = CONV_HALO - SUBLANES * q
                            acc = acc + wd_ref[d:d + 1, ls] * gs[lo:lo + CONV_CHUNK]
                dwc_ref[pl.ds(r0, CONV_CHUNK), ls] = acc
            return carry

        lax.fori_loop(0, n_chunks, chunk, 0)

        xv = dwc_ref[...]
        mu = jnp.mean(xv, axis=-1, keepdims=True)
        xc = xv - mu
        var = jnp.mean(xc * xc, axis=-1, keepdims=True)
        ln = xc * lax.rsqrt(var + LN_EPS) * lng_ref[...] + lnb_ref[...]
        z_ref[...] = (ln * _sig(ln)).astype(BF16)

    return _call(
        body, name="conv_fwd", grid=(seq // tm,),
        in_specs=[_rows(tm, D_MODEL), _full((1, D_MODEL)), _full((2 * D_MODEL, D_MODEL)), _full((1, 2 * D_MODEL)),
                  _full((TAPS_PAD, D_MODEL)), _full((1, D_MODEL)), _full((1, D_MODEL)), _full((1, D_MODEL))],
        out_specs=[_rows(tm, D_MODEL), _rows(tm, 2 * D_MODEL), _rows(tm, D_MODEL), _rows(tm, D_MODEL)],
        out_shape=[jax.ShapeDtypeStruct((seq, D_MODEL), BF16), jax.ShapeDtypeStruct((seq, 2 * D_MODEL), BF16),
                   jax.ShapeDtypeStruct((seq, D_MODEL), F32), jax.ShapeDtypeStruct((seq, D_MODEL), BF16)],
        scratch=[pltpu.VMEM((tm + CONV_HALO, D_MODEL), F32)],
        sem=("arbitrary",))(h, g, wpw1, bpw1, wd, bdw, lng, lnb)


def _ffn_down_loss(s, gathered, res, target, g):
    seq = s.shape[0]
    tm = min(TOKEN_TILE, seq)

    def body(s_ref, w_ref, r_ref, t_ref, g_ref, dh_ref, dhb_ref, loss_ref, dg_ref):
        @pl.when(pl.program_id(0) == 0)
        def _():
            loss_ref[...] = jnp.zeros_like(loss_ref)
            dg_ref[...] = jnp.zeros_like(dg_ref)

        hv = r_ref[...] + jnp.dot(s_ref[...], w_ref[...].reshape(D_FF, D_MODEL), preferred_element_type=F32)
        gv = g_ref[...]
        err = hv * _rms_stats(hv) * gv - t_ref[...]
        sq = jnp.sum(jnp.sum(err * err, axis=-1, keepdims=True), axis=0, keepdims=True)
        loss_ref[...] += jnp.broadcast_to(sq, loss_ref.shape)
        dx, dg = _rms_bwd(err * (1.0 / D_MODEL), hv, gv, 0.0)
        dh_ref[...] = dx
        dhb_ref[...] = dx.astype(BF16)
        dg_ref[...] += dg

    return _call(
        body, name="ffn1_down_loss", grid=(seq // tm,),
        in_specs=[_rows(tm, D_FF), _ff_whole_spec(W2_SLOT), _rows(tm, D_MODEL), _rows(tm, D_MODEL),
                  _full((1, D_MODEL))],
        out_specs=[_rows(tm, D_MODEL), _rows(tm, D_MODEL), _full((SUBLANES, LANES)), _full((1, D_MODEL))],
        out_shape=[jax.ShapeDtypeStruct((seq, D_MODEL), F32), jax.ShapeDtypeStruct((seq, D_MODEL), BF16),
                   jax.ShapeDtypeStruct((SUBLANES, LANES), F32), jax.ShapeDtypeStruct((1, D_MODEL), F32)],
        sem=("arbitrary",))(s, gathered, res, target, g)


def _ffn_bwd_act(name, dh, u, w, gathered):
    seq = dh.shape[0]
    tm = min(TOKEN_TILE, seq)

    def body(dh_ref, u_ref, w_ref, w2_ref, du_ref, dw_ref):
        ds = _nt(dh_ref[...], w2_ref[...].reshape(FF_TILE, D_MODEL))
        uv = u_ref[...].astype(F32)
        sg = _sig(uv)
        dw_ref[...] = (ds * (uv * sg)).astype(BF16)
        du_ref[...] = (ds * w_ref[...].astype(F32) * (sg * (1.0 + uv * (1.0 - sg)))).astype(BF16)

    tile_ff = pl.BlockSpec((tm, FF_TILE), lambda j, i: (i, j))
    ff_shape = jax.ShapeDtypeStruct((seq, D_FF), BF16)
    return _call(
        body, name=name, grid=(FF_SPLIT, seq // tm),
        in_specs=[pl.BlockSpec((tm, D_MODEL), lambda j, i: (i, 0)), tile_ff, tile_ff, _ff_tile_spec(W2_SLOT)],
        out_specs=[tile_ff, tile_ff], out_shape=[ff_shape, ff_shape],
        sem=("parallel", "parallel"))(dh, u, w, gathered)


def _ffn_bwd_in(name, du, dw, gathered, h_in, dres, g):
    seq = du.shape[0]
    tm = min(TOKEN_TILE, seq)

    def body(du_ref, dw_ref, w1_ref, w3_ref, h_ref, dr_ref, g_ref, dx_ref, dxb_ref, dg_ref):
        @pl.when(pl.program_id(0) == 0)
        def _():
            dg_ref[...] = jnp.zeros_like(dg_ref)

        df = jnp.dot(du_ref[...], w1_ref[...].reshape(D_FF, D_MODEL), preferred_element_type=F32)
        df = df + jnp.dot(dw_ref[...], w3_ref[...].reshape(D_FF, D_MODEL), preferred_element_type=F32)
        dx, dg = _rms_bwd(df, h_ref[...], g_ref[...], dr_ref[...])
        dx_ref[...] = dx
        dxb_ref[...] = dx.astype(BF16)
        dg_ref[...] += dg

    return _call(
        body, name=name, grid=(seq // tm,),
        in_specs=[_rows(tm, D_FF), _rows(tm, D_FF), _ff_whole_spec(W1T_SLOT), _ff_whole_spec(W3T_SLOT),
                  _rows(tm, D_MODEL), _rows(tm, D_MODEL), _full((1, D_MODEL))],
        out_specs=[_rows(tm, D_MODEL), _rows(tm, D_MODEL), _full((1, D_MODEL))],
        out_shape=[jax.ShapeDtypeStruct((seq, D_MODEL), F32), jax.ShapeDtypeStruct((seq, D_MODEL), BF16),
                   jax.ShapeDtypeStruct((1, D_MODEL), F32)],
        sem=("arbitrary",))(du, dw, gathered, gathered, h_in, dres, g)


def _mm_tn(name, a, b, *, tk, into=None):
    seq, kdim = a.shape
    n = b.shape[1]
    tt = min((4 if b.dtype == BF16 else 2) * TOKEN_TILE, seq)
    n_t = seq // tt
    devs = tk // ROWS_FF

    def body(a_ref, b_ref, *rest):
        o_ref, acc = rest[-2:]
        t = pl.program_id(1)

        @pl.when(t == 0)
        def _():
            acc[...] = jnp.zeros_like(acc)

        acc[...] += _tn(a_ref[...].astype(BF16), b_ref[...].astype(BF16))

        @pl.when(t == n_t - 1)
        def _():
            out = acc[...].astype(BF16)
            o_ref[...] = out if into is None else out.reshape(devs, ROWS_FF, n)

    in_specs = [pl.BlockSpec((tt, tk), lambda k, t: (t, k)), pl.BlockSpec((tt, n), lambda k, t: (t, 0))]
    args = [a, b]
    aliases = None
    if into is None:
        out_spec = pl.BlockSpec((tk, n), lambda k, t: (k, 0))
        out_shape = jax.ShapeDtypeStruct((kdim, n), BF16)
    else:
        blocks, slot = into
        out_spec = pl.BlockSpec((devs, ROWS_FF, n), lambda k, t: (k, slot, 0))
        out_shape = jax.ShapeDtypeStruct((N_DEV, 3 * ROWS_FF, n), BF16)
        if blocks is not None:
            in_specs.append(_ANY)
            args.append(blocks)
            aliases = {2: 0}
    return _call(
        body, name=name, grid=(kdim // tk, n_t), in_specs=in_specs, out_specs=out_spec, out_shape=out_shape,
        scratch=[pltpu.VMEM((tk, n), F32)], sem=("parallel", "arbitrary"), aliases=aliases)(*args)


def _conv_bwd(dh, wpw2, dwc, a, lng, lnb, wd):
    seq = dh.shape[0]
    tm = min(TOKEN_TILE, seq)
    nt = seq // tm
    n_chunks = tm // CONV_CHUNK
    win = CONV_CHUNK + CONV_HALO
    halo_per_tile = tm // CONV_HALO

    def body(dh_ref, w_ref, dwc_ref, a_ref, ah_ref, lng_ref, lnb_ref, wd_ref,
             da_ref, dlg_ref, dlb_ref, dbdw_ref, dwd_ref, dbpw1_ref, dbpw2_ref,
             gbuf, dbuf, dglu, dwd_part):
        i = pl.program_id(0)
        r = nt - 1 - i

        @pl.when(i == 0)
        def _():
            dlg_ref[...] = jnp.zeros_like(dlg_ref)
            dlb_ref[...] = jnp.zeros_like(dlb_ref)
            dbdw_ref[...] = jnp.zeros_like(dbdw_ref)
            dbpw1_ref[...] = jnp.zeros_like(dbpw1_ref)
            dbpw2_ref[...] = jnp.zeros_like(dbpw2_ref)
            dwd_part[...] = jnp.zeros_like(dwd_part)
            dbuf[tm:tm + CONV_HALO, :] = jnp.zeros((CONV_HALO, D_MODEL), F32)

        @pl.when(i > 0)
        def _():
            dbuf[tm:tm + CONV_HALO, :] = dbuf[0:CONV_HALO, :]

        dhv = dh_ref[...]
        dbpw2_ref[...] += _sum_rows(dhv)
        dz = _nt(dhv.astype(BF16), w_ref[...])
        xv = dwc_ref[...]
        lg = lng_ref[...]
        mu = jnp.mean(xv, axis=-1, keepdims=True)
        xc = xv - mu
        rstd = lax.rsqrt(jnp.mean(xc * xc, axis=-1, keepdims=True) + LN_EPS)
        xhat = xc * rstd
        ln = xhat * lg + lnb_ref[...]
        sg = _sig(ln)
        dln = dz * (sg * (1.0 + ln * (1.0 - sg)))
        dlg_ref[...] += _sum_rows(dln * xhat)
        dlb_ref[...] += _sum_rows(dln)
        dxh = dln * lg
        ddw = rstd * (dxh - jnp.mean(dxh, axis=-1, keepdims=True)
                      - xhat * jnp.mean(dxh * xhat, axis=-1, keepdims=True))
        dbdw_ref[...] += _sum_rows(ddw)
        dbuf[0:tm, :] = ddw

        av = a_ref[...].astype(F32)
        a1 = av[:, :D_MODEL]
        s2 = _sig(av[:, D_MODEL:])
        gbuf[CONV_HALO:CONV_HALO + tm, :] = a1 * s2
        ah = ah_ref[...].astype(F32)
        gh = ah[:, :D_MODEL] * _sig(ah[:, D_MODEL:])
        gbuf[0:CONV_HALO, :] = jnp.where(r > 0, gh, 0.0)

        def chunk(c, carry):
            r0 = pl.multiple_of(c * CONV_CHUNK, CONV_CHUNK)
            for l in range(D_MODEL // LANES):
                ls = slice(l * LANES, (l + 1) * LANES)
                dw_ = dbuf[pl.ds(r0, win), ls]
                gw = gbuf[pl.ds(r0, win), ls]
                acc = jnp.zeros((CONV_CHUNK, LANES), F32)
                for s in range(SUBLANES):
                    ds_ = dw_ if s == 0 else pltpu.roll(dw_, win - s, 0)
                    for q in range(CONV_HALO // SUBLANES):
                        d = SUBLANES * q + s
                        if d < CONV_WIDTH:
                            acc = acc + wd_ref[d:d + 1, ls] * ds_[SUBLANES * q:SUBLANES * q + CONV_CHUNK]
                            lo = CONV_HALO - SUBLANES * q
                            prod = ds_[0:CONV_CHUNK] * gw[lo:lo + CONV_CHUNK]
                            dwd_part[d, :, ls] += jnp.sum(
                                prod.reshape(CONV_CHUNK // SUBLANES, SUBLANES, LANES), axis=0)
                dglu[pl.ds(r0, CONV_CHUNK), ls] = acc
            return carry

        lax.fori_loop(0, n_chunks, chunk, 0)

        dg_ = dglu[...]
        da1 = dg_ * s2
        da2 = dg_ * a1 * s2 * (1.0 - s2)
        da_ref[:, :D_MODEL] = da1.astype(BF16)
        da_ref[:, D_MODEL:] = da2.astype(BF16)
        dbpw1_ref[:, :D_MODEL] += _sum_rows(da1)
        dbpw1_ref[:, D_MODEL:] += _sum_rows(da2)

        @pl.when(i == nt - 1)
        def _():
            dwd_ref[...] = jnp.sum(dwd_part[...], axis=1)

    rev = lambda n: pl.BlockSpec((tm, n), lambda i: (nt - 1 - i, 0))
    halo = pl.BlockSpec((CONV_HALO, 2 * D_MODEL),
                        lambda i: (jnp.maximum((nt - 1 - i) * halo_per_tile - 1, 0), 0))
    vec = lambda n: _full((1, n))
    return _call(
        body, name="conv_bwd", grid=(nt,),
        in_specs=[rev(D_MODEL), _full((D_MODEL, D_MODEL)), rev(D_MODEL), rev(2 * D_MODEL), halo,
                  vec(D_MODEL), vec(D_MODEL), _full((TAPS_PAD, D_MODEL))],
        out_specs=[rev(2 * D_MODEL), vec(D_MODEL), vec(D_MODEL), vec(D_MODEL), _full((TAPS_PAD, D_MODEL)),
                   vec(2 * D_MODEL), vec(D_MODEL)],
        out_shape=[jax.ShapeDtypeStruct((seq, 2 * D_MODEL), BF16), jax.ShapeDtypeStruct((1, D_MODEL), F32),
                   jax.ShapeDtypeStruct((1, D_MODEL), F32), jax.ShapeDtypeStruct((1, D_MODEL), F32),
                   jax.ShapeDtypeStruct((TAPS_PAD, D_MODEL), F32), jax.ShapeDtypeStruct((1, 2 * D_MODEL), F32),
                   jax.ShapeDtypeStruct((1, D_MODEL), F32)],
        scratch=[pltpu.VMEM((tm + CONV_HALO, D_MODEL), F32), pltpu.VMEM((tm + CONV_HALO, D_MODEL), F32),
                 pltpu.VMEM((tm, D_MODEL), F32), pltpu.VMEM((TAPS_PAD, SUBLANES, D_MODEL), F32)],
        sem=("arbitrary",))(dh, wpw2, dwc, a, a, lng, lnb, wd)


def _mm_rms_bwd(name, dact, wt, h_in, dres, g, bf16_copy):
    seq, n = dact.shape
    tm = min(TOKEN_TILE, seq)

    def body(da_ref, w_ref, h_ref, dr_ref, g_ref, dx_ref, *rest):
        dg_ref = rest[-1]

        @pl.when(pl.program_id(0) == 0)
        def _():
            dg_ref[...] = jnp.zeros_like(dg_ref)

        dy = jnp.dot(da_ref[...], w_ref[...], preferred_element_type=F32)
        dx, dg = _rms_bwd(dy, h_ref[...], g_ref[...], dr_ref[...])
        dx_ref[...] = dx
        if bf16_copy:
            rest[0][...] = dx.astype(BF16)
        dg_ref[...] += dg

    copy_spec = [_rows(tm, D_MODEL)] if bf16_copy else []
    copy_shape = [jax.ShapeDtypeStruct((seq, D_MODEL), BF16)] if bf16_copy else []
    return _call(
        body, name=name, grid=(seq // tm,),
        in_specs=[_rows(tm, n), _resident((n, D_MODEL)), _rows(tm, D_MODEL), _rows(tm, D_MODEL), _full((1, D_MODEL))],
        out_specs=[_rows(tm, D_MODEL), *copy_spec, _full((1, D_MODEL))],
        out_shape=[jax.ShapeDtypeStruct((seq, D_MODEL), F32), *copy_shape, jax.ShapeDtypeStruct((1, D_MODEL), F32)],
        sem=("arbitrary",))(dact, wt, h_in, dres, g)


def _nt_bias(name, dy, w):
    seq, n = dy.shape
    kdim = w.shape[0]
    tm = min(2 * TOKEN_TILE, seq)

    def body(dy_ref, w_ref, o_ref, db_ref):
        @pl.when(pl.program_id(0) == 0)
        def _():
            db_ref[...] = jnp.zeros_like(db_ref)

        dyv = dy_ref[...]
        db_ref[...] += _sum_rows(dyv)
        o_ref[...] = _nt(dyv.astype(BF16), w_ref[...]).astype(BF16)

    return _call(
        body, name=name, grid=(seq // tm,),
        in_specs=[_rows(tm, n), _resident((kdim, n))],
        out_specs=[_rows(tm, kdim), _full((1, n))],
        out_shape=[jax.ShapeDtypeStruct((seq, kdim), BF16), jax.ShapeDtypeStruct((1, n), F32)],
        sem=("arbitrary",))(dy, w)


def _attn_bwd(probs, p_sinks, q, k, v, do):
    seq = q.shape[0]
    per = 2 if seq % (2 * BLOCK) == 0 else 1
    tile = per * BLOCK
    cur = pl.BlockSpec((tile, KV_DIM), lambda n: (n, 0))
    prev = pl.BlockSpec((BLOCK, KV_DIM), lambda n: (jnp.maximum(n * per - 1, 0), 0))

    def body(p_ref, ps_ref, q_ref, kc_ref, kp_ref, vc_ref, vp_ref, do_ref,
             dq_ref, dkc_ref, dkp_ref, dvc_ref, dvp_ref, dsink_ref):
        @pl.when(pl.program_id(0) == 0)
        def _():
            dsink_ref[...] = jnp.zeros_like(dsink_ref)

        low = _low_lanes()
        lane = lax.broadcasted_iota(jnp.int32, (BLOCK, LANES), 1)
        for b in range(per):
            rows = slice(b * BLOCK, (b + 1) * BLOCK)
            before = slice((b - 1) * BLOCK, b * BLOCK)
            k_prev, v_prev = (kp_ref[...], vp_ref[...]) if b == 0 else (kc_ref[before, :], vc_ref[before, :])
            k_cur, v_cur = kc_ref[rows, :], vc_ref[rows, :]
            p_sinks_blk = ps_ref[rows, :]
            dk_all = jnp.zeros((2 * BLOCK, KV_DIM), F32)
            dv_all = jnp.zeros((2 * BLOCK, KV_DIM), F32)
            for j in range(N_KV_HEADS):
                k_lo, k_hi = _kv_low_high(k_prev, k_cur, j, low)
                v_lo, v_hi = _kv_low_high(v_prev, v_cur, j, low)
                dk_acc = jnp.zeros((2 * BLOCK, KV_DIM), F32)
                dv_acc = jnp.zeros((2 * BLOCK, KV_DIM), F32)
                for i in range(PAIRS_PER_KV):
                    ls, h = _pair_lanes(j, i)
                    qp = q_ref[rows, ls]
                    dop = do_ref[rows, ls]
                    dsb, pb16 = [], []
                    for t, v_sel in enumerate((v_lo, v_hi)):
                        pb = p_ref[h + t, rows, :]
                        pf = pb.astype(F32)
                        p_sink = jnp.sum(jnp.where(lane == h + t, p_sinks_blk, 0.0), axis=-1, keepdims=True)
                        dp = _nt(dop, v_sel)
                        delta = jnp.sum(pf * dp, axis=-1, keepdims=True)
                        dsb.append((pf * (dp - delta)).astype(BF16))
                        pb16.append(pb)
                        dsink_ref[h + t:h + t + 1, :] += jnp.broadcast_to(_sum_rows(p_sink * delta), (1, LANES))
                    dq = (jnp.dot(dsb[0], k_lo, preferred_element_type=F32)
                          + jnp.dot(dsb[1], k_hi, preferred_element_type=F32))
                    dq_ref[rows, ls] = dq * (HEAD_DIM ** -0.5)
                    dk_acc = dk_acc + jnp.where(low, _tn(dsb[0], qp), _tn(dsb[1], qp))
                    dv_acc = dv_acc + jnp.where(low, _tn(pb16[0], dop), _tn(pb16[1], dop))
                dk_all = dk_all + _fold_pair_halves(dk_acc, j, low)
                dv_all = dv_all + _fold_pair_halves(dv_acc, j, low)
            dkp_ref[rows, :] = dk_all[:BLOCK]
            dkc_ref[rows, :] = dk_all[BLOCK:]
            dvp_ref[rows, :] = dv_all[:BLOCK]
            dvc_ref[rows, :] = dv_all[BLOCK:]

    kv_out = _rows(tile, KV_DIM)
    kv_shape = jax.ShapeDtypeStruct((seq, KV_DIM), F32)
    return _call(
        body, name="attn_bwd", grid=(seq // tile,),
        in_specs=[pl.BlockSpec((N_Q_HEADS, tile, 2 * BLOCK), lambda n: (0, n, 0)), _rows(tile, LANES),
                  _rows(tile, Q_DIM), cur, prev, cur, prev, _rows(tile, Q_DIM)],
        out_specs=[_rows(tile, Q_DIM), kv_out, kv_out, kv_out, kv_out, _full((N_Q_HEADS, LANES))],
        out_shape=[jax.ShapeDtypeStruct((seq, Q_DIM), F32), kv_shape, kv_shape, kv_shape, kv_shape,
                   jax.ShapeDtypeStruct((N_Q_HEADS, LANES), F32)],
        sem=("arbitrary",))(probs, p_sinks, q, k, k, v, v, do)


def _rope_bwd(dq, dkc, dkp, dvc, dvp, rc, rs):
    seq = dq.shape[0]
    nb = seq // BLOCK
    tm = min(TOKEN_TILE, seq)
    nt = seq // tm
    per = tm // BLOCK
    nxt = pl.BlockSpec((BLOCK, KV_DIM), lambda i: (jnp.minimum((i + 1) * per, nb - 1), 0))

    def body(dq_ref, dkc_ref, dkp_ref, dkn_ref, dvc_ref, dvp_ref, dvn_ref, c_ref, s_ref, o_ref, db_ref):
        i = pl.program_id(0)

        @pl.when(i == 0)
        def _():
            db_ref[...] = jnp.zeros_like(db_ref)

        c = c_ref[...]
        sa, sb = _signed_sin(s_ref[...])
        last = i == nt - 1

        def from_next_block(prev_ref, next_ref):
            tail = jnp.where(last, 0.0, next_ref[...])
            return tail if per == 1 else jnp.concatenate([prev_ref[BLOCK:, :], tail], axis=0)

        dk = dkc_ref[...] + from_next_block(dkp_ref, dkn_ref)
        dv = dvc_ref[...] + from_next_block(dvp_ref, dvn_ref)
        for l in range(Q_DIM // LANES):
            ls = slice(l * LANES, (l + 1) * LANES)
            blk = _rope_t(dq_ref[:, ls], c, sa, sb)
            o_ref[:, ls] = blk.astype(BF16)
            db_ref[:, ls] += _sum_rows(blk)
        dkr = _rope_t(dk, c, sa, sb)
        o_ref[:, Q_DIM:Q_DIM + KV_DIM] = dkr.astype(BF16)
        db_ref[:, Q_DIM:Q_DIM + KV_DIM] += _sum_rows(dkr)
        o_ref[:, Q_DIM + KV_DIM:] = dv.astype(BF16)
        db_ref[:, Q_DIM + KV_DIM:] += _sum_rows(dv)

    kv = _rows(tm, KV_DIM)
    tab = _rows(tm, LANES)
    return _call(
        body, name="rope_bwd", grid=(nt,),
        in_specs=[_rows(tm, Q_DIM), kv, kv, nxt, kv, kv, nxt, tab, tab],
        out_specs=[_rows(tm, QKV_DIM), _full((1, QKV_DIM))],
        out_shape=[jax.ShapeDtypeStruct((seq, QKV_DIM), BF16), jax.ShapeDtypeStruct((1, QKV_DIM), F32)],
        sem=("arbitrary",))(dq, dkc, dkp, dkp, dvc, dvp, dvp, rc, rs)


def _adamw(w, g, m, v):
    m = ADAM_B1 * m + (1.0 - ADAM_B1) * g
    v = ADAM_B2 * v + (1.0 - ADAM_B2) * (g * g)
    m_hat = m / (1.0 - ADAM_B1 ** ADAM_STEP)
    v_hat = v / (1.0 - ADAM_B2 ** ADAM_STEP)
    delta = -ADAM_LR * (m_hat / (jnp.sqrt(v_hat) + ADAM_EPS) + ADAM_WD * w)
    return delta, m, v


def _sum_slots(name, parts):
    _, rows, cols = parts.shape
    tr = rows if rows <= 512 else ROWS_FF

    def body(p_ref, g_ref):
        g = p_ref[0].astype(F32)
        for d in range(1, N_DEV):
            g = g + p_ref[d].astype(F32)
        g_ref[...] = g

    return _call(
        body, name=name, grid=(rows // tr,),
        in_specs=[pl.BlockSpec((N_DEV, tr, cols), lambda i: (0, i, 0))],
        out_specs=_rows(tr, cols), out_shape=jax.ShapeDtypeStruct((rows, cols), F32),
        sem=("parallel",))(parts)


def _adamw_native(name, g, w, m, v):
    layers, rows, cols = w.shape
    tr = rows if rows <= 512 else 256

    def body(g_ref, w_ref, m_ref, v_ref, d_ref, nm_ref, nv_ref):
        d_ref[...], nm_ref[...], nv_ref[...] = _adamw(w_ref[...], g_ref[...], m_ref[...], v_ref[...])

    spec = pl.BlockSpec((1, tr, cols), lambda l, i: (l, i, 0))
    shape = jax.ShapeDtypeStruct(w.shape, F32)
    return _call(
        body, name=name, grid=(layers, rows // tr), in_specs=[spec, spec, spec, spec],
        out_specs=[spec, spec, spec], out_shape=[shape, shape, shape],
        sem=("parallel", "parallel"))(g, w, m, v)


def _adamw_from_slots(name, slots, row_block, w, m, v):
    layers, rows, cols = w.shape

    def body(*refs):
        slot_refs = refs[:layers]
        w_ref, m_ref, v_ref, g_ref, d_ref, nm_ref, nv_ref = refs[layers:]
        for l in range(layers):
            @pl.when(pl.program_id(0) == l)
            def _(p_ref=slot_refs[l]):
                g = p_ref[0].astype(F32)
                for d in range(1, N_DEV):
                    g = g + p_ref[d].astype(F32)
                g_ref[0] = g
        d, nm, nv = _adamw(w_ref[0], g_ref[0], m_ref[0], v_ref[0])
        d_ref[0], nm_ref[0], nv_ref[0] = d, nm, nv

    slot_spec = pl.BlockSpec((N_DEV, rows, cols), lambda l: (0, row_block, 0))
    spec = pl.BlockSpec((1, rows, cols), lambda l: (l, 0, 0))
    shape = jax.ShapeDtypeStruct(w.shape, F32)
    return _call(
        body, name=name, grid=(layers,), in_specs=[slot_spec] * layers + [spec, spec, spec],
        out_specs=[spec, spec, spec, spec], out_shape=[shape, shape, shape, shape],
        sem=("arbitrary",))(*slots, w, m, v)


def _adamw_replicated(parts, w, m, v):
    def body(p_ref, w_ref, m_ref, v_ref, g_ref, d_ref, nm_ref, nv_ref):
        g = p_ref[0]
        for j in range(1, N_DEV):
            g = g + p_ref[j]
        g_ref[...] = g
        d_ref[...], nm_ref[...], nv_ref[...] = _adamw(w_ref[...], g, m_ref[...], v_ref[...])

    spec = _full((REPL_ROWS, D_MODEL))
    shape = jax.ShapeDtypeStruct((REPL_ROWS, D_MODEL), F32)
    return _call(
        body, name="adamw_replicated", grid=(1,),
        in_specs=[_full((N_DEV, REPL_ROWS, D_MODEL)), spec, spec, spec],
        out_specs=[spec, spec, spec, spec], out_shape=[shape, shape, shape, shape],
        sem=("arbitrary",))(parts, w, m, v)


_MESH = pl.DeviceIdType.MESH
_ANY = pl.BlockSpec(memory_space=pl.ANY)


def _all_gather(name, xs):
    rows, cols = xs.shape

    def body(x_ref, out_ref, send_sems, recv_sems, local_sem):
        x, y, c = lax.axis_index("x"), lax.axis_index("y"), lax.axis_index("c")
        me, sibling = (x, y, c), (x, y, 1 - c)
        chips = [(1 - x, y), (x, 1 - y), (1 - x, 1 - y)]

        def slot(px, py, pc):
            return out_ref.at[4 * px + 2 * py + pc]

        def copy(k, block, to, src=None):
            return pltpu.make_async_remote_copy(
                src_ref=slot(*block) if src is None else src, dst_ref=slot(*block),
                send_sem=send_sems.at[k], recv_sem=recv_sems.at[k], device_id=to, device_id_type=_MESH)

        mine = pltpu.make_async_copy(x_ref, slot(*me), local_sem)
        mine.start()
        first = [copy(0, me, sibling, src=x_ref)]
        first += [copy(1 + j, me, (*chip, c), src=x_ref) for j, chip in enumerate(chips)]
        for cp in first:
            cp.start()
        passed = [copy(4 + j, (*chip, c), sibling) for j, chip in enumerate(chips)]
        for j, chip in enumerate(chips):
            copy(1 + j, (*chip, c), me).wait_recv()
            passed[j].start()
        copy(0, sibling, me).wait_recv()
        for j, chip in enumerate(chips):
            copy(4 + j, (*chip, 1 - c), me).wait_recv()
        for cp in first + passed:
            cp.wait_send()
        mine.wait()

    return pl.pallas_call(
        body, name=name, out_shape=jax.ShapeDtypeStruct((N_DEV, rows, cols), xs.dtype),
        in_specs=[_ANY], out_specs=_ANY,
        scratch_shapes=[pltpu.SemaphoreType.DMA((7,)), pltpu.SemaphoreType.DMA((7,)), pltpu.SemaphoreType.DMA],
    )(xs)


N_PEERS = N_DEV - 1
_HBM = pl.BlockSpec(memory_space=pltpu.HBM)
_SEM = pl.BlockSpec(memory_space=pltpu.SEMAPHORE)
_DATAFLOW = pltpu.SideEffectType.DATAFLOW_SIDE_EFFECTING
_TOKEN = jax.ShapeDtypeStruct((SUBLANES, LANES), F32)


def _peers():
    x, y, c = lax.axis_index("x"), lax.axis_index("y"), lax.axis_index("c")
    out = []
    for k in range(1, N_DEV):
        px = 1 - x if k & 4 else x
        py = 1 - y if k & 2 else y
        pc = 1 - c if k & 1 else c
        out.append(((px, py, pc), 4 * px + 2 * py + pc))
    return 4 * x + 2 * y + c, out


def _in_hbm(a):
    return pltpu.with_memory_space_constraint(a, pltpu.HBM)


def _landing(rows):
    return _in_hbm(lax.empty((N_DEV, rows, D_MODEL), BF16))


SEMS_PER_GROUP = 3


def _group_sems():
    return pltpu.SemaphoreType.DMA((N_PEERS,)), pltpu.SemaphoreType.DMA((N_PEERS,)), pltpu.SemaphoreType.DMA(())


def _gather_start(name, payloads):
    n = len(payloads)
    ns = SEMS_PER_GROUP

    def body(*refs):
        src, land = refs[:n], refs[n:2 * n]
        sems = refs[2 * n:(2 + ns) * n]
        token = refs[-1]
        me, peers = _peers()
        for g in range(n):
            send_sems, recv_sems, own_sem = sems[ns * g:ns * (g + 1)]
            for k, (pos, _) in enumerate(peers):
                pltpu.make_async_remote_copy(
                    src_ref=src[g], dst_ref=land[g].at[me], send_sem=send_sems.at[k],
                    recv_sem=recv_sems.at[k], device_id=pos, device_id_type=_MESH).start()
            pltpu.make_async_copy(src[g], land[g].at[me], own_sem).start()
        token[...] = jnp.zeros_like(token)

    lands = [_landing(p.shape[0]) for p in payloads]
    sem_shapes = [s for _ in payloads for s in _group_sems()]
    hbm_shapes = [pltpu.HBM(a.shape, a.dtype) for a in list(payloads) + lands]
    out = pl.pallas_call(
        body, name=name, out_shape=(*sem_shapes, *hbm_shapes, _TOKEN),
        in_specs=[_HBM] * (2 * n),
        out_specs=(*[_SEM] * (ns * n), *[_HBM] * (2 * n), pl.BlockSpec(memory_space=pltpu.VMEM)),
        input_output_aliases={i: ns * n + i for i in range(2 * n)},
        compiler_params=pltpu.CompilerParams(has_side_effects=_DATAFLOW),
    )(*[_in_hbm(p) for p in payloads], *lands)
    sems, thru = out[:ns * n], out[ns * n:(ns + 2) * n]
    return [(thru[g], thru[n + g], *sems[ns * g:ns * (g + 1)]) for g in range(n)], out[-1]


def _gather_wait(name, group, after):
    payload, land, send_sems, recv_sems, own_sem = group

    def body(src_ref, land_ref, send_ref, recv_ref, own_ref, after_ref, src_out, land_out):
        me, peers = _peers()
        for k, (pos, idx) in enumerate(peers):
            cp = pltpu.make_async_remote_copy(
                src_ref=src_ref, dst_ref=land_ref.at[idx], send_sem=send_ref.at[k], recv_sem=recv_ref.at[k],
                device_id=pos, device_id_type=_MESH)
            cp.wait_send()
            cp.wait_recv()
        pltpu.make_async_copy(src_ref, land_ref.at[me], own_ref).wait()

    _, land = pl.pallas_call(
        body, name=name, out_shape=(pltpu.HBM(payload.shape, payload.dtype), pltpu.HBM(land.shape, land.dtype)),
        in_specs=[_HBM, _HBM, _SEM, _SEM, _SEM, _ANY], out_specs=(_HBM, _HBM), input_output_aliases={0: 0, 1: 1},
        compiler_params=pltpu.CompilerParams(has_side_effects=_DATAFLOW),
    )(payload, land, send_sems, recv_sems, own_sem, after)
    return land


def _scatter_start(name, blocks):
    rows = blocks.shape[1]

    def body(blocks_ref, land_ref, send_sems, recv_sems, own_sem, blocks_out, land_out, token):
        me, peers = _peers()
        for k, (pos, idx) in enumerate(peers):
            pltpu.make_async_remote_copy(
                src_ref=blocks_ref.at[idx], dst_ref=land_ref.at[me], send_sem=send_sems.at[k],
                recv_sem=recv_sems.at[k], device_id=pos, device_id_type=_MESH).start()
        pltpu.make_async_copy(blocks_ref.at[me], land_ref.at[me], own_sem).start()
        token[...] = jnp.zeros_like(token)

    land = _landing(rows)
    send_sems, recv_sems, own_sem, blocks_thru, land_thru, token = pl.pallas_call(
        body, name=name,
        out_shape=(*_group_sems(), pltpu.HBM(blocks.shape, blocks.dtype), pltpu.HBM(land.shape, land.dtype), _TOKEN),
        in_specs=[_HBM, _HBM], out_specs=(_SEM, _SEM, _SEM, _HBM, _HBM, pl.BlockSpec(memory_space=pltpu.VMEM)),
        input_output_aliases={0: 3, 1: 4},
        compiler_params=pltpu.CompilerParams(has_side_effects=_DATAFLOW),
    )(_in_hbm(blocks), land)
    return (blocks_thru, land_thru, send_sems, recv_sems, own_sem), token


def _scatter_wait(groups, after):
    n = len(groups)
    ns = SEMS_PER_GROUP

    def body(*refs):
        blocks, land = refs[:n], refs[n:2 * n]
        sems = refs[2 * n:(2 + ns) * n]
        me, peers = _peers()
        for g in range(n):
            send_sems, recv_sems, own_sem = sems[ns * g:ns * (g + 1)]
            for k, (pos, idx) in enumerate(peers):
                cp = pltpu.make_async_remote_copy(
                    src_ref=blocks[g].at[idx], dst_ref=land[g].at[idx], send_sem=send_sems.at[k],
                    recv_sem=recv_sems.at[k], device_id=pos, device_id_type=_MESH)
                cp.wait_send()
                cp.wait_recv()
            pltpu.make_async_copy(blocks[g].at[me], land[g].at[me], own_sem).wait()

    hbm = [grp[0] for grp in groups] + [grp[1] for grp in groups]
    sems = [s for grp in groups for s in grp[2:]]
    out = pl.pallas_call(
        body, name="rs_wait", out_shape=tuple(pltpu.HBM(a.shape, a.dtype) for a in hbm),
        in_specs=[_HBM] * (2 * n) + [_SEM] * (ns * n) + [_ANY], out_specs=tuple([_HBM] * (2 * n)),
        input_output_aliases={i: i for i in range(2 * n)},
        compiler_params=pltpu.CompilerParams(has_side_effects=_DATAFLOW),
    )(*hbm, *sems, after)
    return list(out[n:])


def _pad_rows(flat, rows):
    return jnp.pad(flat, (0, rows * D_MODEL - flat.shape[0])).reshape(rows, D_MODEL)


SMALL_NAMES = ("conv_b_pw1", "conv_w_dw", "conv_b_dw", "conv_ln_g", "conv_ln_b", "conv_b_pw2")


def _pack_small(p):
    flat = jnp.concatenate([p[n].reshape(-1) for n in SMALL_NAMES])
    return _pad_rows(flat, ROWS_SMALL).reshape(1, ROWS_SMALL, D_MODEL)


def _unpack_small(packed):
    flat = packed.reshape(-1)
    c = D_MODEL // N_DEV
    shapes = ((1, 2 * c), (1, CONV_WIDTH, c), (1, c), (1, c), (1, c), (1, c))
    out, o = {}, 0
    for n, shape in zip(SMALL_NAMES, shapes):
        size = shape[-1] * (shape[1] if len(shape) == 3 else 1)
        out[n] = flat[o:o + size].reshape(shape)
        o += size
    return out


def _gather_payloads(p):
    t = lambda a: jnp.swapaxes(a, -1, -2).astype(BF16)
    w1t, w3t, w2 = t(p["ffn_w1"]), t(p["ffn_w3"]), p["ffn_w2"].astype(BF16)
    bits = lax.bitcast_convert_type(_pack_small(p).reshape(-1)[:ROWS_SMALL * D_MODEL // 2], jnp.uint32)
    halves = [(bits >> 16).astype(jnp.uint16), (bits & 0xFFFF).astype(jnp.uint16)]
    small = lax.bitcast_convert_type(jnp.concatenate(halves), BF16).reshape(ROWS_SMALL, D_MODEL)
    conv = jnp.concatenate([t(p["conv_w_pw1"][0]), p["conv_w_pw2"][0].astype(BF16), small], axis=0)
    ffn = [jnp.concatenate([w1t[l], w3t[l], w2[l]], axis=0) for l in range(2)]
    return [t(p["attn_w_qkv"][0]), p["attn_w_o"][0].astype(BF16), ffn[0], conv, ffn[1]]


def _device_rows(land, lo, n):
    return land[:, lo:lo + n].reshape(N_DEV * n, D_MODEL)


def _unpack_conv(land):
    words = lax.bitcast_convert_type(land[:, ROWS_PW1 + ROWS_PW2:], jnp.uint16).astype(jnp.uint32)
    words = words.reshape(N_DEV, 2, ROWS_SMALL * D_MODEL // 2)
    small = lax.bitcast_convert_type((words[:, 0] << 16) | words[:, 1], F32)
    c = D_MODEL // N_DEV
    b_pw1 = small[:, :2 * c].reshape(1, 2 * D_MODEL)
    s = 2 * c
    w_dw = small[:, s:s + CONV_WIDTH * c].reshape(N_DEV, CONV_WIDTH, c).transpose(1, 0, 2).reshape(CONV_WIDTH, D_MODEL)
    s += CONV_WIDTH * c
    b_dw, ln_g, ln_b, b_pw2 = (small[:, s + i * c:s + (i + 1) * c].reshape(1, D_MODEL) for i in range(4))
    return dict(w_pw1_t=_device_rows(land, 0, ROWS_PW1), w_pw2=_device_rows(land, ROWS_PW1, ROWS_PW2),
                b_pw1=b_pw1, w_dw=w_dw, b_dw=b_dw, ln_g=ln_g, ln_b=ln_b, b_pw2=b_pw2)


def _dest_blocks(mats):
    return jnp.concatenate([a.reshape(N_DEV, -1, D_MODEL) for a in mats], axis=1)


def _small_grad_rows(g_bpw1, g_dw, g_bdw, g_lng, g_lnb, g_bpw2):
    c = D_MODEL // N_DEV
    small = jnp.concatenate(
        [g_bpw1.reshape(N_DEV, 2 * c), g_dw.reshape(CONV_WIDTH, N_DEV, c).transpose(1, 0, 2).reshape(N_DEV, -1),
         g_bdw.reshape(N_DEV, c), g_lng.reshape(N_DEV, c), g_lnb.reshape(N_DEV, c), g_bpw2.reshape(N_DEV, c)], axis=1)
    small = jnp.pad(small, ((0, 0), (0, ROWS_SMALL * D_MODEL - SMALL_USED)))
    return small.reshape(N_DEV * ROWS_SMALL, D_MODEL).astype(BF16)


LOSS_ROW = 9


def _pack_replicated(norm_mix, norm_ffn, b_qkv, sinks, b_o, norm_final, extra=None):
    rows = [norm_mix.reshape(2, D_MODEL), norm_ffn.reshape(2, D_MODEL), _pad_rows(b_qkv.reshape(-1), 2),
            _pad_rows(sinks.reshape(-1), 1), b_o.reshape(1, D_MODEL), norm_final.reshape(1, D_MODEL)]
    if extra is not None:
        rows.append(_pad_rows(extra.reshape(-1), 1))
    p = jnp.concatenate(rows, axis=0)
    return jnp.pad(p, ((0, REPL_ROWS - p.shape[0]), (0, 0)))


def _unpack_replicated(p):
    return dict(norm_mix=p[0:2], norm_ffn=p[2:4], attn_b_qkv=p[4:6].reshape(-1)[:QKV_DIM].reshape(1, QKV_DIM),
                attn_sinks=p[6, :N_Q_HEADS].reshape(1, N_Q_HEADS), attn_b_o=p[7:8], norm_final=p[8])


WEIGHT_ORDER = ['norm_mix', 'norm_ffn', 'attn_w_qkv', 'attn_b_qkv', 'attn_sinks', 'attn_w_o', 'attn_b_o',
                'conv_w_pw1', 'conv_b_pw1', 'conv_w_dw', 'conv_b_dw', 'conv_ln_g', 'conv_ln_b', 'conv_w_pw2',
                'conv_b_pw2', 'ffn_w1', 'ffn_w3', 'ffn_w2', 'norm_final']


def kernel(x, norm_mix, norm_ffn, attn_w_qkv, attn_b_qkv, attn_sinks, attn_w_o, attn_b_o, conv_w_pw1, conv_b_pw1, conv_w_dw, conv_b_dw, conv_ln_g, conv_ln_b, conv_w_pw2, conv_b_pw2, ffn_w1, ffn_w3, ffn_w2, norm_final, loss_target, m_norm_mix, m_norm_ffn, m_attn_w_qkv, m_attn_b_qkv, m_attn_sinks, m_attn_w_o, m_attn_b_o, m_conv_w_pw1, m_conv_b_pw1, m_conv_w_dw, m_conv_b_dw, m_conv_ln_g, m_conv_ln_b, m_conv_w_pw2, m_conv_b_pw2, m_ffn_w1, m_ffn_w3, m_ffn_w2, m_norm_final, v_norm_mix, v_norm_ffn, v_attn_w_qkv, v_attn_b_qkv, v_attn_sinks, v_attn_w_o, v_attn_b_o, v_conv_w_pw1, v_conv_b_pw1, v_conv_w_dw, v_conv_b_dw, v_conv_ln_g, v_conv_ln_b, v_conv_w_pw2, v_conv_b_pw2, v_ffn_w1, v_ffn_w3, v_ffn_w2, v_norm_final):
    xs = x[0]
    target = loss_target[0]
    seq = xs.shape[0]

    w = dict(attn_w_qkv=attn_w_qkv, attn_w_o=attn_w_o, conv_w_pw1=conv_w_pw1, conv_b_pw1=conv_b_pw1,
             conv_w_dw=conv_w_dw, conv_b_dw=conv_b_dw, conv_ln_g=conv_ln_g, conv_ln_b=conv_ln_b,
             conv_w_pw2=conv_w_pw2, conv_b_pw2=conv_b_pw2, ffn_w1=ffn_w1, ffn_w3=ffn_w3, ffn_w2=ffn_w2)
    m = dict(attn_w_qkv=m_attn_w_qkv, attn_w_o=m_attn_w_o, conv_w_pw1=m_conv_w_pw1, conv_b_pw1=m_conv_b_pw1,
             conv_w_dw=m_conv_w_dw, conv_b_dw=m_conv_b_dw, conv_ln_g=m_conv_ln_g, conv_ln_b=m_conv_ln_b,
             conv_w_pw2=m_conv_w_pw2, conv_b_pw2=m_conv_b_pw2, ffn_w1=m_ffn_w1, ffn_w3=m_ffn_w3, ffn_w2=m_ffn_w2)
    v = dict(attn_w_qkv=v_attn_w_qkv, attn_w_o=v_attn_w_o, conv_w_pw1=v_conv_w_pw1, conv_b_pw1=v_conv_b_pw1,
             conv_w_dw=v_conv_w_dw, conv_b_dw=v_conv_b_dw, conv_ln_g=v_conv_ln_g, conv_ln_b=v_conv_ln_b,
             conv_w_pw2=v_conv_w_pw2, conv_b_pw2=v_conv_b_pw2, ffn_w1=v_ffn_w1, ffn_w3=v_ffn_w3, ffn_w2=v_ffn_w2)

    payloads = _gather_payloads(w)
    (ag_qkv, ag_wo), tok = _gather_start("ag_start_attn", payloads[:2])
    (ag_ffn0, ag_conv, ag_ffn1), tok = _gather_start(
        "ag_start_rest", [payloads[2] + tok[0, 0].astype(BF16), payloads[3], payloads[4]])
    rc, rs = _rope_tables(seq)
    sinks = attn_sinks.reshape(N_Q_HEADS)
    g_mix0, g_mix1 = norm_mix[0:1] + tok[0, 0], norm_mix[1:2]
    g_ffn0, g_ffn1 = norm_ffn[0:1], norm_ffn[1:2]
    g_fin = norm_final.reshape(1, D_MODEL)

    y0 = _rms_fwd(xs, g_mix0)
    w_qkv_t = _gather_wait("ag_wait_qkv", ag_qkv, y0).reshape(QKV_DIM, D_MODEL)
    q, k, vv = _qkv_fwd(y0, w_qkv_t, attn_b_qkv, rc, rs)
    attn, probs, p_sinks = _attn_fwd(sinks, q, k, vv)
    w_o = _gather_wait("ag_wait_wo", ag_wo, attn).reshape(Q_DIM, D_MODEL)
    h1 = _mm_res("attn_out_proj", attn, w_o, attn_b_o, xs)
    w_ffn0 = _gather_wait("ag_wait_ffn0", ag_ffn0, h1)
    f0, u0, p0, s0 = _ffn_up("ffn0_up", h1, g_ffn0, w_ffn0)
    h2 = _ffn_down("ffn0_down", s0, w_ffn0, h1)
    wt = _unpack_conv(_gather_wait("ag_wait_conv", ag_conv, h2))
    wd = jnp.concatenate([wt["w_dw"][::-1], jnp.zeros((TAPS_PAD - CONV_WIDTH, D_MODEL), F32)], axis=0)
    y1, a, dwc, z = _conv_fwd(h2, g_mix1, wt["w_pw1_t"], wt["b_pw1"], wd, wt["b_dw"], wt["ln_g"], wt["ln_b"])
    h3 = _mm_res("conv_out_proj", z, wt["w_pw2"], wt["b_pw2"], h2)
    w_ffn1 = _gather_wait("ag_wait_ffn1", ag_ffn1, h3)
    f1, u1, p1, s1 = _ffn_up("ffn1_up", h3, g_ffn1, w_ffn1)
    dh4, dh4b, sq, dg_fin = _ffn_down_loss(s1, w_ffn1, h3, target, g_fin)

    du1, dp1 = _ffn_bwd_act("ffn1_bwd_act", dh4b, u1, p1, w_ffn1)
    dh3, dh3b, dg_ffn1 = _ffn_bwd_in("ffn1_bwd_in", du1, dp1, w_ffn1, h3, dh4, g_ffn1)
    blocks = _mm_tn("ffn1_dw2", s1, dh4b, tk=FF_TILE, into=(None, W2_SLOT))
    blocks = _mm_tn("ffn1_dw1", du1, f1, tk=FF_TILE, into=(blocks, W1T_SLOT))
    blocks = _mm_tn("ffn1_dw3", dp1, f1, tk=FF_TILE, into=(blocks, W3T_SLOT))
    rs_ffn1, tok = _scatter_start("rs_start_ffn1", blocks)

    da, dlg, dlb, dbdw, dwd, dbpw1, dbpw2 = _conv_bwd(dh3, wt["w_pw2"], dwc, a, wt["ln_g"] + tok[0, 0],
                                                     wt["ln_b"], wd)
    gpw2 = _mm_tn("conv_dw_pw2", z, dh3b, tk=D_MODEL)
    dh2, dh2b, dg_mix1 = _mm_rms_bwd("conv_in_bwd", da, wt["w_pw1_t"], h2, dh3, g_mix1, True)
    gpw1t = _mm_tn("conv_dw_pw1", da, y1, tk=D_MODEL)
    small_rows = _small_grad_rows(dbpw1, dwd[:CONV_WIDTH][::-1], dbdw, dlg, dlb, dbpw2)
    rs_conv, tok = _scatter_start("rs_start_conv", _dest_blocks([gpw1t, gpw2, small_rows]))

    du0, dp0 = _ffn_bwd_act("ffn0_bwd_act", dh2b, u0, p0, w_ffn0)
    dh1, dh1b, dg_ffn0 = _ffn_bwd_in("ffn0_bwd_in", du0, dp0, w_ffn0, h1, dh2, g_ffn0 + tok[0, 0])
    blocks = _mm_tn("ffn0_dw2", s0, dh2b, tk=FF_TILE, into=(None, W2_SLOT))
    blocks = _mm_tn("ffn0_dw1", du0, f0, tk=FF_TILE, into=(blocks, W1T_SLOT))
    blocks = _mm_tn("ffn0_dw3", dp0, f0, tk=FF_TILE, into=(blocks, W3T_SLOT))
    rs_ffn0, tok = _scatter_start("rs_start_ffn0", blocks)

    gwo = _mm_tn("attn_dw_o", attn, dh1b, tk=D_MODEL)
    rs_wo, tok2 = _scatter_start("rs_start_wo", _dest_blocks([gwo]))
    dattn, dbo = _nt_bias("attn_out_bwd", dh1, w_o)
    dq, dkc, dkp, dvc, dvp, dsink = _attn_bwd(probs, p_sinks + (tok[0, 0] + tok2[0, 0]), q, k, vv, dattn)
    dqkv, dbqkv = _rope_bwd(dq, dkc, dkp, dvc, dvp, rc, rs)
    gqkvt = _mm_tn("attn_dw_qkv", dqkv, y0, tk=QKV_DIM)
    rs_qkv, tok = _scatter_start("rs_start_qkv", _dest_blocks([gqkvt]))
    dx, dg_mix0 = _mm_rms_bwd("qkv_in_bwd", dqkv, w_qkv_t, xs, dh1, g_mix0 + tok[0, 0], False)

    p_ffn1, p_conv, p_ffn0, p_wo, p_qkv = _scatter_wait([rs_ffn1, rs_conv, rs_ffn0, rs_wo, rs_qkv], dx)

    tr = lambda a: jnp.swapaxes(a, -1, -2)
    same = lambda a: a
    ffn_slots = [p_ffn0, p_ffn1]
    plan = dict(
        ffn_w1=(ffn_slots, W1T_SLOT, tr), ffn_w3=(ffn_slots, W3T_SLOT, tr), ffn_w2=(ffn_slots, W2_SLOT, same),
        attn_w_qkv=([p_qkv], 0, tr), attn_w_o=([p_wo], 0, same),
        conv_w_pw2=([p_conv], ROWS_PW1 // ROWS_PW2, same))
    sharded = [{}, {}, {}, {}]
    for n, (slots, row_block, view) in plan.items():
        outs_n = _adamw_from_slots("adamw_" + n, slots, row_block, view(w[n]), view(m[n]), view(v[n]))
        for dst, t in zip(sharded, outs_n):
            dst[n] = view(t)
    small_out = _adamw_from_slots("adamw_small", [p_conv], (ROWS_PW1 + ROWS_PW2) // ROWS_SMALL,
                                  _pack_small(w), _pack_small(m), _pack_small(v))
    for dst, t in zip(sharded, small_out):
        dst.update(_unpack_small(t))
    g_pw1 = _sum_slots("rs_sum_pw1", p_conv[:, :ROWS_PW1]).T[None]
    pw1_out = _adamw_native("adamw_conv_w_pw1", g_pw1, w["conv_w_pw1"], m["conv_w_pw1"], v["conv_w_pw1"])
    for dst, t in zip(sharded, (g_pw1,) + tuple(pw1_out)):
        dst["conv_w_pw1"] = t

    part = _pack_replicated(jnp.concatenate([dg_mix0, dg_mix1]), jnp.concatenate([dg_ffn0, dg_ffn1]),
                            dbqkv, -dsink[:, 0], dbo, dg_fin, extra=sq[0, 0:1])
    parts = _all_gather("ag_replicated_grads", part)
    w_rep = _pack_replicated(norm_mix, norm_ffn, attn_b_qkv, attn_sinks, attn_b_o, norm_final)
    m_rep = _pack_replicated(m_norm_mix, m_norm_ffn, m_attn_b_qkv, m_attn_sinks, m_attn_b_o, m_norm_final)
    v_rep = _pack_replicated(v_norm_mix, v_norm_ffn, v_attn_b_qkv, v_attn_sinks, v_attn_b_o, v_norm_final)
    rep_out = _adamw_replicated(parts, w_rep, m_rep, v_rep)
    replicated = [_unpack_replicated(t) for t in rep_out]
    loss = rep_out[0][LOSS_ROW, 0] * (0.5 / D_MODEL)

    outs = [loss, dx.reshape(1, seq, D_MODEL)]
    for sh, rp in zip(sharded, replicated):
        merged = {**sh, **rp}
        outs += [merged[n] for n in WEIGHT_ORDER]
    return tuple(outs)
```

```python
import jax
import jax.numpy as jnp
from jax import lax
from jax.experimental import pallas as pl
from jax.experimental.pallas import tpu as pltpu

F32 = jnp.float32
BF16 = jnp.bfloat16

D_MODEL = 1024
HEAD_DIM = 64
N_Q_HEADS = 16
N_KV_HEADS = 2
Q_PER_KV = 8
Q_DIM = N_Q_HEADS * HEAD_DIM
KV_DIM = N_KV_HEADS * HEAD_DIM
QKV_DIM = Q_DIM + 2 * KV_DIM
BLOCK = 128
CONV_WIDTH = 31
D_FF = 2816
ROPE_THETA = 10000.0
RMS_EPS = 1e-5
LN_EPS = 1e-5
ADAM_LR = 0.001
ADAM_B1 = 0.9
ADAM_B2 = 0.999
ADAM_EPS = 1e-08
ADAM_WD = 0.01
ADAM_STEP = 10
N_DEV = 8

LANES = 128
SUBLANES = 8
TOKEN_TILE = 512
CONV_CHUNK = 64
CONV_HALO = 32
TAPS_PAD = 32
VMEM_LIMIT = 56 * 1024 * 1024
NEG_INF = float(jnp.finfo(jnp.float32).min)

ROWS_FF = D_FF // N_DEV
W1T_SLOT, W3T_SLOT, W2_SLOT = 0, 1, 2
ROWS_PW1 = 2 * D_MODEL // N_DEV
ROWS_PW2 = D_MODEL // N_DEV
ROWS_SMALL = 16
SMALL_USED = 2 * D_MODEL // N_DEV + CONV_WIDTH * (D_MODEL // N_DEV) + 4 * (D_MODEL // N_DEV)
FF_SPLIT = 2
FF_TILE_DEVS = N_DEV // FF_SPLIT
FF_TILE = FF_TILE_DEVS * ROWS_FF
REPL_ROWS = 16


def _call(body, *, name, grid, in_specs, out_specs, out_shape, scratch=(), sem=None, aliases=None):
    return pl.pallas_call(
        body, name=name, grid=grid, in_specs=in_specs, out_specs=out_specs, out_shape=out_shape,
        scratch_shapes=list(scratch), input_output_aliases=aliases or {},
        compiler_params=pltpu.CompilerParams(dimension_semantics=sem, vmem_limit_bytes=VMEM_LIMIT))


def _full(shape):
    return pl.BlockSpec(shape, lambda *_: (0,) * len(shape))


def _resident(shape):
    return pl.BlockSpec(shape, lambda *_: (0,) * len(shape), pipeline_mode=pl.Buffered(1))


def _rows(tm, n):
    return pl.BlockSpec((tm, n), lambda i, *_: (i, 0))


def _sig(x):
    return 1.0 / (1.0 + jnp.exp(-x))


def _sum_rows(x):
    return jnp.sum(x, axis=0, keepdims=True)


def _nt(a, b):
    return lax.dot_general(a, b, (((1,), (1,)), ((), ())), preferred_element_type=F32)


def _tn(a, b):
    return lax.dot_general(a, b, (((0,), (0,)), ((), ())), preferred_element_type=F32)


def _rms_stats(x):
    return lax.rsqrt(jnp.mean(x * x, axis=-1, keepdims=True) + RMS_EPS)


def _rms_bwd(dy, x, g, dres):
    r = _rms_stats(x)
    n = x * r
    dn = dy * g
    dx = dres + r * (dn - n * jnp.mean(dn * n, axis=-1, keepdims=True))
    return dx, _sum_rows(dy * n)


def _rope_tables(seq):
    half = HEAD_DIM // 2
    pos = jnp.arange(seq, dtype=F32)
    inv_freq = ROPE_THETA ** (-jnp.arange(0, HEAD_DIM, 2, dtype=F32) / HEAD_DIM)
    ang = pos[:, None] * inv_freq[None, :]
    spread = lambda t: jnp.broadcast_to(t[:, None, :], (seq, LANES // half, half)).reshape(seq, LANES)
    return spread(jnp.cos(ang)), spread(jnp.sin(ang))


def _signed_sin(sin):
    lane = lax.broadcasted_iota(jnp.int32, sin.shape, 1)
    first_half = (lane & (HEAD_DIM - 1)) < HEAD_DIM // 2
    return jnp.where(first_half, -sin, 0.0), jnp.where(first_half, 0.0, sin)


def _rope(t, c, sa, sb):
    half = HEAD_DIM // 2
    return t * c + pltpu.roll(t, LANES - half, 1) * sa + pltpu.roll(t, half, 1) * sb


def _rope_t(dt, c, sa, sb):
    half = HEAD_DIM // 2
    return dt * c + pltpu.roll(dt * sa, half, 1) + pltpu.roll(dt * sb, LANES - half, 1)


def _rms_fwd(x, g):
    seq = x.shape[0]
    tm = min(2 * TOKEN_TILE, seq)

    def body(x_ref, g_ref, y_ref):
        xv = x_ref[...]
        y_ref[...] = (xv * _rms_stats(xv) * g_ref[...]).astype(BF16)

    return _call(
        body, name="rms_fwd", grid=(seq // tm,), in_specs=[_rows(tm, D_MODEL), _full((1, D_MODEL))],
        out_specs=_rows(tm, D_MODEL), out_shape=jax.ShapeDtypeStruct((seq, D_MODEL), BF16),
        sem=("parallel",))(x, g)


def _qkv_fwd(y, w, b, rc, rs):
    seq = y.shape[0]
    tm = min(TOKEN_TILE, seq)

    def body(y_ref, w_ref, b_ref, c_ref, s_ref, q_ref, k_ref, v_ref):
        qkv = _nt(y_ref[...], w_ref[...]) + b_ref[...]
        c = c_ref[...]
        sa, sb = _signed_sin(s_ref[...])
        for i in range(Q_DIM // LANES):
            blk = _rope(qkv[:, i * LANES:(i + 1) * LANES], c, sa, sb)
            q_ref[:, i * LANES:(i + 1) * LANES] = (blk * (HEAD_DIM ** -0.5)).astype(BF16)
        k_ref[...] = _rope(qkv[:, Q_DIM:Q_DIM + KV_DIM], c, sa, sb).astype(BF16)
        v_ref[...] = qkv[:, Q_DIM + KV_DIM:].astype(BF16)

    return _call(
        body, name="qkv_fwd", grid=(seq // tm,),
        in_specs=[_rows(tm, D_MODEL), _resident((QKV_DIM, D_MODEL)), _full((1, QKV_DIM)),
                  _rows(tm, LANES), _rows(tm, LANES)],
        out_specs=[_rows(tm, Q_DIM), _rows(tm, KV_DIM), _rows(tm, KV_DIM)],
        out_shape=[jax.ShapeDtypeStruct((seq, Q_DIM), BF16),
                   jax.ShapeDtypeStruct((seq, KV_DIM), BF16), jax.ShapeDtypeStruct((seq, KV_DIM), BF16)],
        sem=("parallel",))(y, w, b, rc, rs)


def _band_mask(n):
    row = lax.broadcasted_iota(jnp.int32, (BLOCK, 2 * BLOCK), 0)
    col = lax.broadcasted_iota(jnp.int32, (BLOCK, 2 * BLOCK), 1)
    rel = row + BLOCK - col
    return (rel >= 0) & (rel < BLOCK) & ((col >= BLOCK) | (n > 0))


def _softmax_with_sink(s, mask, sink):
    s = jnp.where(mask, s, NEG_INF)
    m = jnp.maximum(jnp.max(s, axis=-1, keepdims=True), sink)
    p = jnp.exp(s - m)
    e_sink = jnp.exp(sink - m)
    inv = 1.0 / (jnp.sum(p, axis=-1, keepdims=True) + e_sink)
    return p * inv, e_sink * inv


PAIRS_PER_KV = Q_PER_KV // 2


def _kv_specs():
    cur = pl.BlockSpec((BLOCK, KV_DIM), lambda n: (n, 0))
    prev = pl.BlockSpec((BLOCK, KV_DIM), lambda n: (jnp.maximum(n - 1, 0), 0))
    return cur, prev


def _low_lanes():
    return lax.broadcasted_iota(jnp.int32, (2 * BLOCK, KV_DIM), 1) < HEAD_DIM


def _kv_low_high(prev, cur, j, low):
    both = jnp.concatenate([prev, cur], axis=0).astype(F32)
    swapped = pltpu.roll(both, HEAD_DIM, 1)
    at_low, at_high = (both, swapped) if j == 0 else (swapped, both)
    return jnp.where(low, at_low, 0.0).astype(BF16), jnp.where(low, 0.0, at_high).astype(BF16)


def _fold_pair_halves(acc, j, low):
    folded = acc + pltpu.roll(acc, HEAD_DIM, 1)
    return jnp.where(low, folded, 0.0) if j == 0 else jnp.where(low, 0.0, folded)


def _pair_lanes(j, i):
    g = j * PAIRS_PER_KV + i
    return slice(g * LANES, (g + 1) * LANES), 2 * g


def _attn_fwd(sinks, q, k, v):
    seq = q.shape[0]
    cur, prev = _kv_specs()

    def body(sink_ref, q_ref, kc_ref, kp_ref, vc_ref, vp_ref, o_ref, p_ref, ps_ref):
        mask = _band_mask(pl.program_id(0))
        lane = lax.broadcasted_iota(jnp.int32, (BLOCK, LANES), 1)
        p_sinks = jnp.zeros((BLOCK, LANES), F32)
        for j in range(N_KV_HEADS):
            cs = slice(j * HEAD_DIM, (j + 1) * HEAD_DIM)
            kk = jnp.concatenate([kp_ref[:, cs], kc_ref[:, cs]], axis=0)
            vv = jnp.concatenate([vp_ref[:, cs], vc_ref[:, cs]], axis=0)
            for gq in range(Q_PER_KV):
                h = j * Q_PER_KV + gq
                hs = slice(h * HEAD_DIM, (h + 1) * HEAD_DIM)
                probs, p_sink = _softmax_with_sink(_nt(q_ref[:, hs], kk), mask, sink_ref[h])
                pb = probs.astype(BF16)
                p_ref[h] = pb
                p_sinks = jnp.where(lane == h, p_sink, p_sinks)
                o_ref[:, hs] = jnp.dot(pb, vv, preferred_element_type=F32).astype(BF16)
        ps_ref[...] = p_sinks

    return _call(
        body, name="attn_fwd", grid=(seq // BLOCK,),
        in_specs=[pl.BlockSpec(memory_space=pltpu.SMEM), _rows(BLOCK, Q_DIM), cur, prev, cur, prev],
        out_specs=[_rows(BLOCK, Q_DIM), pl.BlockSpec((N_Q_HEADS, BLOCK, 2 * BLOCK), lambda n: (0, n, 0)),
                   _rows(BLOCK, LANES)],
        out_shape=[jax.ShapeDtypeStruct((seq, Q_DIM), BF16),
                   jax.ShapeDtypeStruct((N_Q_HEADS, seq, 2 * BLOCK), BF16),
                   jax.ShapeDtypeStruct((seq, LANES), F32)],
        sem=("parallel",))(sinks, q, k, k, v, v)


def _mm_res(name, a, w, b, res):
    seq, kdim = a.shape
    n = w.shape[1]
    tm = min(2 * TOKEN_TILE, seq)

    def body(a_ref, w_ref, b_ref, r_ref, o_ref):
        o_ref[...] = r_ref[...] + (jnp.dot(a_ref[...], w_ref[...], preferred_element_type=F32) + b_ref[...])

    return _call(
        body, name=name, grid=(seq // tm,),
        in_specs=[_rows(tm, kdim), _resident((kdim, n)), _full((1, n)), _rows(tm, n)],
        out_specs=_rows(tm, n), out_shape=jax.ShapeDtypeStruct((seq, n), F32),
        sem=("parallel",))(a, w, b, res)


def _ff_tile_spec(slot):
    return pl.BlockSpec((FF_TILE_DEVS, ROWS_FF, D_MODEL), lambda j, i: (j, slot, 0))


def _ff_whole_spec(slot):
    return pl.BlockSpec((N_DEV, ROWS_FF, D_MODEL), lambda i: (0, slot, 0), pipeline_mode=pl.Buffered(1))


def _ffn_up(name, h, g, gathered):
    seq = h.shape[0]
    tm = min(TOKEN_TILE, seq)

    def body(h_ref, g_ref, w1_ref, w3_ref, f_ref, u_ref, w_ref, s_ref):
        hv = h_ref[...]
        f = (hv * _rms_stats(hv) * g_ref[...]).astype(BF16)
        f_ref[...] = f
        u = _nt(f, w1_ref[...].reshape(FF_TILE, D_MODEL))
        w = _nt(f, w3_ref[...].reshape(FF_TILE, D_MODEL))
        u_ref[...] = u.astype(BF16)
        w_ref[...] = w.astype(BF16)
        s_ref[...] = (u * _sig(u) * w).astype(BF16)

    n_t = seq // tm
    tile_ff = pl.BlockSpec((tm, FF_TILE), lambda j, i: (i, j))
    f_spec = pl.BlockSpec((tm, D_MODEL), lambda j, i: (jnp.where(j == 0, i, n_t), 0))
    ff_shape = jax.ShapeDtypeStruct((seq, D_FF), BF16)
    return _call(
        body, name=name, grid=(FF_SPLIT, n_t),
        in_specs=[pl.BlockSpec((tm, D_MODEL), lambda j, i: (i, 0)), _full((1, D_MODEL)),
                  _ff_tile_spec(W1T_SLOT), _ff_tile_spec(W3T_SLOT)],
        out_specs=[f_spec, tile_ff, tile_ff, tile_ff],
        out_shape=[jax.ShapeDtypeStruct((seq + tm, D_MODEL), BF16), ff_shape, ff_shape, ff_shape],
        sem=("arbitrary", "arbitrary"))(h, g, gathered, gathered)


def _ffn_down(name, s, gathered, res):
    seq = s.shape[0]
    tm = min(2 * TOKEN_TILE, seq)

    def body(s_ref, w_ref, r_ref, o_ref):
        w2 = w_ref[...].reshape(D_FF, D_MODEL)
        o_ref[...] = r_ref[...] + jnp.dot(s_ref[...], w2, preferred_element_type=F32)

    return _call(
        body, name=name, grid=(seq // tm,),
        in_specs=[_rows(tm, D_FF), _ff_whole_spec(W2_SLOT), _rows(tm, D_MODEL)],
        out_specs=_rows(tm, D_MODEL), out_shape=jax.ShapeDtypeStruct((seq, D_MODEL), F32),
        sem=("parallel",))(s, gathered, res)


def _conv_fwd(h, g, wpw1, bpw1, wd, bdw, lng, lnb):
    seq = h.shape[0]
    tm = min(TOKEN_TILE, seq)
    n_chunks = tm // CONV_CHUNK
    win = CONV_CHUNK + CONV_HALO

    def body(h_ref, g_ref, w_ref, b_ref, wd_ref, bdw_ref, lng_ref, lnb_ref,
             y_ref, a_ref, dwc_ref, z_ref, gbuf):
        i = pl.program_id(0)

        @pl.when(i == 0)
        def _():
            gbuf[0:CONV_HALO, :] = jnp.zeros((CONV_HALO, D_MODEL), F32)

        @pl.when(i > 0)
        def _():
            gbuf[0:CONV_HALO, :] = gbuf[tm:tm + CONV_HALO, :]

        hv = h_ref[...]
        y = (hv * _rms_stats(hv) * g_ref[...]).astype(BF16)
        y_ref[...] = y
        a = _nt(y, w_ref[...]) + b_ref[...]
        a_ref[...] = a.astype(BF16)
        gbuf[CONV_HALO:CONV_HALO + tm, :] = a[:, :D_MODEL] * _sig(a[:, D_MODEL:])

        def chunk(c, carry):
            r0 = pl.multiple_of(c * CONV_CHUNK, CONV_CHUNK)
            for l in range(D_MODEL // LANES):
                ls = slice(l * LANES, (l + 1) * LANES)
                gw = gbuf[pl.ds(r0, win), ls]
                acc = jnp.zeros((CONV_CHUNK, LANES), F32) + bdw_ref[:, ls]
                for s in range(SUBLANES):
                    gs = gw if s == 0 else pltpu.roll(gw, s, 0)
                    for q in range(CONV_HALO // SUBLANES):
                        d = SUBLANES * q + s
                        if d < CONV_WIDTH:
                            lo = CONV_HALO - SUBLANES * q
                            acc = acc + wd_ref[d:d + 1, ls] * gs[lo:lo + CONV_CHUNK]
                dwc_ref[pl.ds(r0, CONV_CHUNK), ls] = acc
            return carry

        lax.fori_loop(0, n_chunks, chunk, 0)

        xv = dwc_ref[...]
        mu = jnp.mean(xv, axis=-1, keepdims=True)
        xc = xv - mu
        var = jnp.mean(xc * xc, axis=-1, keepdims=True)
        ln = xc * lax.rsqrt(var + LN_EPS) * lng_ref[...] + lnb_ref[...]
        z_ref[...] = (ln * _sig(ln)).astype(BF16)

    return _call(
        body, name="conv_fwd", grid=(seq // tm,),
        in_specs=[_rows(tm, D_MODEL), _full((1, D_MODEL)), _full((2 * D_MODEL, D_MODEL)), _full((1, 2 * D_MODEL)),
                  _full((TAPS_PAD, D_MODEL)), _full((1, D_MODEL)), _full((1, D_MODEL)), _full((1, D_MODEL))],
        out_specs=[_rows(tm, D_MODEL), _rows(tm, 2 * D_MODEL), _rows(tm, D_MODEL), _rows(tm, D_MODEL)],
        out_shape=[jax.ShapeDtypeStruct((seq, D_MODEL), BF16), jax.ShapeDtypeStruct((seq, 2 * D_MODEL), BF16),
                   jax.ShapeDtypeStruct((seq, D_MODEL), F32), jax.ShapeDtypeStruct((seq, D_MODEL), BF16)],
        scratch=[pltpu.VMEM((tm + CONV_HALO, D_MODEL), F32)],
        sem=("arbitrary",))(h, g, wpw1, bpw1, wd, bdw, lng, lnb)


def _ffn_down_loss(s, gathered, res, target, g):
    seq = s.shape[0]
    tm = min(TOKEN_TILE, seq)

    def body(s_ref, w_ref, r_ref, t_ref, g_ref, dh_ref, dhb_ref, loss_ref, dg_ref):
        @pl.when(pl.program_id(0) == 0)
        def _():
            loss_ref[...] = jnp.zeros_like(loss_ref)
            dg_ref[...] = jnp.zeros_like(dg_ref)

        hv = r_ref[...] + jnp.dot(s_ref[...], w_ref[...].reshape(D_FF, D_MODEL), preferred_element_type=F32)
        gv = g_ref[...]
        err = hv * _rms_stats(hv) * gv - t_ref[...]
        sq = jnp.sum(jnp.sum(err * err, axis=-1, keepdims=True), axis=0, keepdims=True)
        loss_ref[...] += jnp.broadcast_to(sq, loss_ref.shape)
        dx, dg = _rms_bwd(err * (1.0 / D_MODEL), hv, gv, 0.0)
        dh_ref[...] = dx
        dhb_ref[...] = dx.astype(BF16)
        dg_ref[...] += dg

    return _call(
        body, name="ffn1_down_loss", grid=(seq // tm,),
        in_specs=[_rows(tm, D_FF), _ff_whole_spec(W2_SLOT), _rows(tm, D_MODEL), _rows(tm, D_MODEL),
                  _full((1, D_MODEL))],
        out_specs=[_rows(tm, D_MODEL), _rows(tm, D_MODEL), _full((SUBLANES, LANES)), _full((1, D_MODEL))],
        out_shape=[jax.ShapeDtypeStruct((seq, D_MODEL), F32), jax.ShapeDtypeStruct((seq, D_MODEL), BF16),
                   jax.ShapeDtypeStruct((SUBLANES, LANES), F32), jax.ShapeDtypeStruct((1, D_MODEL), F32)],
        sem=("arbitrary",))(s, gathered, res, target, g)


def _ffn_bwd_act(name, dh, u, w, gathered):
    seq = dh.shape[0]
    tm = min(TOKEN_TILE, seq)

    def body(dh_ref, u_ref, w_ref, w2_ref, du_ref, dw_ref):
        ds = _nt(dh_ref[...], w2_ref[...].reshape(FF_TILE, D_MODEL))
        uv = u_ref[...].astype(F32)
        sg = _sig(uv)
        dw_ref[...] = (ds * (uv * sg)).astype(BF16)
        du_ref[...] = (ds * w_ref[...].astype(F32) * (sg * (1.0 + uv * (1.0 - sg)))).astype(BF16)

    tile_ff = pl.BlockSpec((tm, FF_TILE), lambda j, i: (i, j))
    ff_shape = jax.ShapeDtypeStruct((seq, D_FF), BF16)
    return _call(
        body, name=name, grid=(FF_SPLIT, seq // tm),
        in_specs=[pl.BlockSpec((tm, D_MODEL), lambda j, i: (i, 0)), tile_ff, tile_ff, _ff_tile_spec(W2_SLOT)],
        out_specs=[tile_ff, tile_ff], out_shape=[ff_shape, ff_shape],
        sem=("parallel", "parallel"))(dh, u, w, gathered)


def _ffn_bwd_in(name, du, dw, gathered, h_in, dres, g):
    seq = du.shape[0]
    tm = min(TOKEN_TILE, seq)

    def body(du_ref, dw_ref, w1_ref, w3_ref, h_ref, dr_ref, g_ref, dx_ref, dxb_ref, dg_ref):
        @pl.when(pl.program_id(0) == 0)
        def _():
            dg_ref[...] = jnp.zeros_like(dg_ref)

        df = jnp.dot(du_ref[...], w1_ref[...].reshape(D_FF, D_MODEL), preferred_element_type=F32)
        df = df + jnp.dot(dw_ref[...], w3_ref[...].reshape(D_FF, D_MODEL), preferred_element_type=F32)
        dx, dg = _rms_bwd(df, h_ref[...], g_ref[...], dr_ref[...])
        dx_ref[...] = dx
        dxb_ref[...] = dx.astype(BF16)
        dg_ref[...] += dg

    return _call(
        body, name=name, grid=(seq // tm,),
        in_specs=[_rows(tm, D_FF), _rows(tm, D_FF), _ff_whole_spec(W1T_SLOT), _ff_whole_spec(W3T_SLOT),
                  _rows(tm, D_MODEL), _rows(tm, D_MODEL), _full((1, D_MODEL))],
        out_specs=[_rows(tm, D_MODEL), _rows(tm, D_MODEL), _full((1, D_MODEL))],
        out_shape=[jax.ShapeDtypeStruct((seq, D_MODEL), F32), jax.ShapeDtypeStruct((seq, D_MODEL), BF16),
                   jax.ShapeDtypeStruct((1, D_MODEL), F32)],
        sem=("arbitrary",))(du, dw, gathered, gathered, h_in, dres, g)


def _mm_tn(name, a, b, *, tk, into=None):
    seq, kdim = a.shape
    n = b.shape[1]
    tt = min((4 if b.dtype == BF16 else 2) * TOKEN_TILE, seq)
    n_t = seq // tt
    devs = tk // ROWS_FF

    def body(a_ref, b_ref, *rest):
        o_ref, acc = rest[-2:]
        t = pl.program_id(1)

        @pl.when(t == 0)
        def _():
            acc[...] = jnp.zeros_like(acc)

        acc[...] += _tn(a_ref[...].astype(BF16), b_ref[...].astype(BF16))

        @pl.when(t == n_t - 1)
        def _():
            out = acc[...].astype(BF16)
            o_ref[...] = out if into is None else out.reshape(devs, ROWS_FF, n)

    in_specs = [pl.BlockSpec((tt, tk), lambda k, t: (t, k)), pl.BlockSpec((tt, n), lambda k, t: (t, 0))]
    args = [a, b]
    aliases = None
    if into is None:
        out_spec = pl.BlockSpec((tk, n), lambda k, t: (k, 0))
        out_shape = jax.ShapeDtypeStruct((kdim, n), BF16)
    else:
        blocks, slot = into
        out_spec = pl.BlockSpec((devs, ROWS_FF, n), lambda k, t: (k, slot, 0))
        out_shape = jax.ShapeDtypeStruct((N_DEV, 3 * ROWS_FF, n), BF16)
        if blocks is not None:
            in_specs.append(_ANY)
            args.append(blocks)
            aliases = {2: 0}
    return _call(
        body, name=name, grid=(kdim // tk, n_t), in_specs=in_specs, out_specs=out_spec, out_shape=out_shape,
        scratch=[pltpu.VMEM((tk, n), F32)], sem=("parallel", "arbitrary"), aliases=aliases)(*args)


def _conv_bwd(dh, wpw2, dwc, a, lng, lnb, wd):
    seq = dh.shape[0]
    tm = min(TOKEN_TILE, seq)
    nt = seq // tm
    n_chunks = tm // CONV_CHUNK
    win = CONV_CHUNK + CONV_HALO
    halo_per_tile = tm // CONV_HALO

    def body(dh_ref, w_ref, dwc_ref, a_ref, ah_ref, lng_ref, lnb_ref, wd_ref,
             da_ref, dlg_ref, dlb_ref, dbdw_ref, dwd_ref, dbpw1_ref, dbpw2_ref,
             gbuf, dbuf, dglu, dwd_part):
        i = pl.program_id(0)
        r = nt - 1 - i

        @pl.when(i == 0)
        def _():
            dlg_ref[...] = jnp.zeros_like(dlg_ref)
            dlb_ref[...] = jnp.zeros_like(dlb_ref)
            dbdw_ref[...] = jnp.zeros_like(dbdw_ref)
            dbpw1_ref[...] = jnp.zeros_like(dbpw1_ref)
            dbpw2_ref[...] = jnp.zeros_like(dbpw2_ref)
            dwd_part[...] = jnp.zeros_like(dwd_part)
            dbuf[tm:tm + CONV_HALO, :] = jnp.zeros((CONV_HALO, D_MODEL), F32)

        @pl.when(i > 0)
        def _():
            dbuf[tm:tm + CONV_HALO, :] = dbuf[0:CONV_HALO, :]

        dhv = dh_ref[...]
        dbpw2_ref[...] += _sum_rows(dhv)
        dz = _nt(dhv.astype(BF16), w_ref[...])
        xv = dwc_ref[...]
        lg = lng_ref[...]
        mu = jnp.mean(xv, axis=-1, keepdims=True)
        xc = xv - mu
        rstd = lax.rsqrt(jnp.mean(xc * xc, axis=-1, keepdims=True) + LN_EPS)
        xhat = xc * rstd
        ln = xhat * lg + lnb_ref[...]
        sg = _sig(ln)
        dln = dz * (sg * (1.0 + ln * (1.0 - sg)))
        dlg_ref[...] += _sum_rows(dln * xhat)
        dlb_ref[...] += _sum_rows(dln)
        dxh = dln * lg
        ddw = rstd * (dxh - jnp.mean(dxh, axis=-1, keepdims=True)
                      - xhat * jnp.mean(dxh * xhat, axis=-1, keepdims=True))
        dbdw_ref[...] += _sum_rows(ddw)
        dbuf[0:tm, :] = ddw

        av = a_ref[...].astype(F32)
        a1 = av[:, :D_MODEL]
        s2 = _sig(av[:, D_MODEL:])
        gbuf[CONV_HALO:CONV_HALO + tm, :] = a1 * s2
        ah = ah_ref[...].astype(F32)
        gh = ah[:, :D_MODEL] * _sig(ah[:, D_MODEL:])
        gbuf[0:CONV_HALO, :] = jnp.where(r > 0, gh, 0.0)

        def chunk(c, carry):
            r0 = pl.multiple_of(c * CONV_CHUNK, CONV_CHUNK)
            for l in range(D_MODEL // LANES):
                ls = slice(l * LANES, (l + 1) * LANES)
                dw_ = dbuf[pl.ds(r0, win), ls]
                gw = gbuf[pl.ds(r0, win), ls]
                acc = jnp.zeros((CONV_CHUNK, LANES), F32)
                for s in range(SUBLANES):
                    ds_ = dw_ if s == 0 else pltpu.roll(dw_, win - s, 0)
                    for q in range(CONV_HALO // SUBLANES):
                        d = SUBLANES * q + s
                        if d < CONV_WIDTH:
                            acc = acc + wd_ref[d:d + 1, ls] * ds_[SUBLANES * q:SUBLANES * q + CONV_CHUNK]
                            lo = CONV_HALO - SUBLANES * q
                            prod = ds_[0:CONV_CHUNK] * gw[lo:lo + CONV_CHUNK]
                            dwd_part[d, :, ls] += jnp.sum(
                                prod.reshape(CONV_CHUNK // SUBLANES, SUBLANES, LANES), axis=0)
                dglu[pl.ds(r0, CONV_CHUNK), ls] = acc
            return carry

        lax.fori_loop(0, n_chunks, chunk, 0)

        dg_ = dglu[...]
        da1 = dg_ * s2
        da2 = dg_ * a1 * s2 * (1.0 - s2)
        da_ref[:, :D_MODEL] = da1.astype(BF16)
        da_ref[:, D_MODEL:] = da2.astype(BF16)
        dbpw1_ref[:, :D_MODEL] += _sum_rows(da1)
        dbpw1_ref[:, D_MODEL:] += _sum_rows(da2)

        @pl.when(i == nt - 1)
        def _():
            dwd_ref[...] = jnp.sum(dwd_part[...], axis=1)

    rev = lambda n: pl.BlockSpec((tm, n), lambda i: (nt - 1 - i, 0))
    halo = pl.BlockSpec((CONV_HALO, 2 * D_MODEL),
                        lambda i: (jnp.maximum((nt - 1 - i) * halo_per_tile - 1, 0), 0))
    vec = lambda n: _full((1, n))
    return _call(
        body, name="conv_bwd", grid=(nt,),
        in_specs=[rev(D_MODEL), _full((D_MODEL, D_MODEL)), rev(D_MODEL), rev(2 * D_MODEL), halo,
                  vec(D_MODEL), vec(D_MODEL), _full((TAPS_PAD, D_MODEL))],
        out_specs=[rev(2 * D_MODEL), vec(D_MODEL), vec(D_MODEL), vec(D_MODEL), _full((TAPS_PAD, D_MODEL)),
                   vec(2 * D_MODEL), vec(D_MODEL)],
        out_shape=[jax.ShapeDtypeStruct((seq, 2 * D_MODEL), BF16), jax.ShapeDtypeStruct((1, D_MODEL), F32),
                   jax.ShapeDtypeStruct((1, D_MODEL), F32), jax.ShapeDtypeStruct((1, D_MODEL), F32),
                   jax.ShapeDtypeStruct((TAPS_PAD, D_MODEL), F32), jax.ShapeDtypeStruct((1, 2 * D_MODEL), F32),
                   jax.ShapeDtypeStruct((1, D_MODEL), F32)],
        scratch=[pltpu.VMEM((tm + CONV_HALO, D_MODEL), F32), pltpu.VMEM((tm + CONV_HALO, D_MODEL), F32),
                 pltpu.VMEM((tm, D_MODEL), F32), pltpu.VMEM((TAPS_PAD, SUBLANES, D_MODEL), F32)],
        sem=("arbitrary",))(dh, wpw2, dwc, a, a, lng, lnb, wd)


def _mm_rms_bwd(name, dact, wt, h_in, dres, g, bf16_copy):
    seq, n = dact.shape
    tm = min(TOKEN_TILE, seq)

    def body(da_ref, w_ref, h_ref, dr_ref, g_ref, dx_ref, *rest):
        dg_ref = rest[-1]

        @pl.when(pl.program_id(0) == 0)
        def _():
            dg_ref[...] = jnp.zeros_like(dg_ref)

        dy = jnp.dot(da_ref[...], w_ref[...], preferred_element_type=F32)
        dx, dg = _rms_bwd(dy, h_ref[...], g_ref[...], dr_ref[...])
        dx_ref[...] = dx
        if bf16_copy:
            rest[0][...] = dx.astype(BF16)
        dg_ref[...] += dg

    copy_spec = [_rows(tm, D_MODEL)] if bf16_copy else []
    copy_shape = [jax.ShapeDtypeStruct((seq, D_MODEL), BF16)] if bf16_copy else []
    return _call(
        body, name=name, grid=(seq // tm,),
        in_specs=[_rows(tm, n), _resident((n, D_MODEL)), _rows(tm, D_MODEL), _rows(tm, D_MODEL), _full((1, D_MODEL))],
        out_specs=[_rows(tm, D_MODEL), *copy_spec, _full((1, D_MODEL))],
        out_shape=[jax.ShapeDtypeStruct((seq, D_MODEL), F32), *copy_shape, jax.ShapeDtypeStruct((1, D_MODEL), F32)],
        sem=("arbitrary",))(dact, wt, h_in, dres, g)


def _nt_bias(name, dy, w):
    seq, n = dy.shape
    kdim = w.shape[0]
    tm = min(2 * TOKEN_TILE, seq)

    def body(dy_ref, w_ref, o_ref, db_ref):
        @pl.when(pl.program_id(0) == 0)
        def _():
            db_ref[...] = jnp.zeros_like(db_ref)

        dyv = dy_ref[...]
        db_ref[...] += _sum_rows(dyv)
        o_ref[...] = _nt(dyv.astype(BF16), w_ref[...]).astype(BF16)

    return _call(
        body, name=name, grid=(seq // tm,),
        in_specs=[_rows(tm, n), _resident((kdim, n))],
        out_specs=[_rows(tm, kdim), _full((1, n))],
        out_shape=[jax.ShapeDtypeStruct((seq, kdim), BF16), jax.ShapeDtypeStruct((1, n), F32)],
        sem=("arbitrary",))(dy, w)


def _attn_bwd(probs, p_sinks, q, k, v, do):
    seq = q.shape[0]
    per = 2 if seq % (2 * BLOCK) == 0 else 1
    tile = per * BLOCK
    cur = pl.BlockSpec((tile, KV_DIM), lambda n: (n, 0))
    prev = pl.BlockSpec((BLOCK, KV_DIM), lambda n: (jnp.maximum(n * per - 1, 0), 0))

    def body(p_ref, ps_ref, q_ref, kc_ref, kp_ref, vc_ref, vp_ref, do_ref,
             dq_ref, dkc_ref, dkp_ref, dvc_ref, dvp_ref, dsink_ref):
        @pl.when(pl.program_id(0) == 0)
        def _():
            dsink_ref[...] = jnp.zeros_like(dsink_ref)

        low = _low_lanes()
        lane = lax.broadcasted_iota(jnp.int32, (BLOCK, LANES), 1)
        for b in range(per):
            rows = slice(b * BLOCK, (b + 1) * BLOCK)
            before = slice((b - 1) * BLOCK, b * BLOCK)
            k_prev, v_prev = (kp_ref[...], vp_ref[...]) if b == 0 else (kc_ref[before, :], vc_ref[before, :])
            k_cur, v_cur = kc_ref[rows, :], vc_ref[rows, :]
            p_sinks_blk = ps_ref[rows, :]
            dk_all = jnp.zeros((2 * BLOCK, KV_DIM), F32)
            dv_all = jnp.zeros((2 * BLOCK, KV_DIM), F32)
            for j in range(N_KV_HEADS):
                k_lo, k_hi = _kv_low_high(k_prev, k_cur, j, low)
                v_lo, v_hi = _kv_low_high(v_prev, v_cur, j, low)
                dk_acc = jnp.zeros((2 * BLOCK, KV_DIM), F32)
                dv_acc = jnp.zeros((2 * BLOCK, KV_DIM), F32)
                for i in range(PAIRS_PER_KV):
                    ls, h = _pair_lanes(j, i)
                    qp = q_ref[rows, ls]
                    dop = do_ref[rows, ls]
                    dsb, pb16 = [], []
                    for t, v_sel in enumerate((v_lo, v_hi)):
                        pb = p_ref[h + t, rows, :]
                        pf = pb.astype(F32)
                        p_sink = jnp.sum(jnp.where(lane == h + t, p_sinks_blk, 0.0), axis=-1, keepdims=True)
                        dp = _nt(dop, v_sel)
                        delta = jnp.sum(pf * dp, axis=-1, keepdims=True)
                        dsb.append((pf * (dp - delta)).astype(BF16))
                        pb16.append(pb)
                        dsink_ref[h + t:h + t + 1, :] += jnp.broadcast_to(_sum_rows(p_sink * delta), (1, LANES))
                    dq = (jnp.dot(dsb[0], k_lo, preferred_element_type=F32)
                          + jnp.dot(dsb[1], k_hi, preferred_element_type=F32))
                    dq_ref[rows, ls] = dq * (HEAD_DIM ** -0.5)
                    dk_acc = dk_acc + jnp.where(low, _tn(dsb[0], qp), _tn(dsb[1], qp))
                    dv_acc = dv_acc + jnp.where(low, _tn(pb16[0], dop), _tn(pb16[1], dop))
                dk_all = dk_all + _fold_pair_halves(dk_acc, j, low)
                dv_all = dv_all + _fold_pair_halves(dv_acc, j, low)
            dkp_ref[rows, :] = dk_all[:BLOCK]
            dkc_ref[rows, :] = dk_all[BLOCK:]
            dvp_ref[rows, :] = dv_all[:BLOCK]
            dvc_ref[rows, :] = dv_all[BLOCK:]

    kv_out = _rows(tile, KV_DIM)
    kv_shape = jax.ShapeDtypeStruct((seq, KV_DIM), F32)
    return _call(
        body, name="attn_bwd", grid=(seq // tile,),
        in_specs=[pl.BlockSpec((N_Q_HEADS, tile, 2 * BLOCK), lambda n: (0, n, 0)), _rows(tile, LANES),
                  _rows(tile, Q_DIM), cur, prev, cur, prev, _rows(tile, Q_DIM)],
        out_specs=[_rows(tile, Q_DIM), kv_out, kv_out, kv_out, kv_out, _full((N_Q_HEADS, LANES))],
        out_shape=[jax.ShapeDtypeStruct((seq, Q_DIM), F32), kv_shape, kv_shape, kv_shape, kv_shape,
                   jax.ShapeDtypeStruct((N_Q_HEADS, LANES), F32)],
        sem=("arbitrary",))(probs, p_sinks, q, k, k, v, v, do)


def _rope_bwd(dq, dkc, dkp, dvc, dvp, rc, rs):
    seq = dq.shape[0]
    nb = seq // BLOCK
    tm = min(TOKEN_TILE, seq)
    nt = seq // tm
    per = tm // BLOCK
    nxt = pl.BlockSpec((BLOCK, KV_DIM), lambda i: (jnp.minimum((i + 1) * per, nb - 1), 0))

    def body(dq_ref, dkc_ref, dkp_ref, dkn_ref, dvc_ref, dvp_ref, dvn_ref, c_ref, s_ref, o_ref, db_ref):
        i = pl.program_id(0)

        @pl.when(i == 0)
        def _():
            db_ref[...] = jnp.zeros_like(db_ref)

        c = c_ref[...]
        sa, sb = _signed_sin(s_ref[...])
        last = i == nt - 1

        def from_next_block(prev_ref, next_ref):
            tail = jnp.where(last, 0.0, next_ref[...])
            return tail if per == 1 else jnp.concatenate([prev_ref[BLOCK:, :], tail], axis=0)

        dk = dkc_ref[...] + from_next_block(dkp_ref, dkn_ref)
        dv = dvc_ref[...] + from_next_block(dvp_ref, dvn_ref)
        for l in range(Q_DIM // LANES):
            ls = slice(l * LANES, (l + 1) * LANES)
            blk = _rope_t(dq_ref[:, ls], c, sa, sb)
            o_ref[:, ls] = blk.astype(BF16)
            db_ref[:, ls] += _sum_rows(blk)
        dkr = _rope_t(dk, c, sa, sb)
        o_ref[:, Q_DIM:Q_DIM + KV_DIM] = dkr.astype(BF16)
        db_ref[:, Q_DIM:Q_DIM + KV_DIM] += _sum_rows(dkr)
        o_ref[:, Q_DIM + KV_DIM:] = dv.astype(BF16)
        db_ref[:, Q_DIM + KV_DIM:] += _sum_rows(dv)

    kv = _rows(tm, KV_DIM)
    tab = _rows(tm, LANES)
    return _call(
        body, name="rope_bwd", grid=(nt,),
        in_specs=[_rows(tm, Q_DIM), kv, kv, nxt, kv, kv, nxt, tab, tab],
        out_specs=[_rows(tm, QKV_DIM), _full((1, QKV_DIM))],
        out_shape=[jax.ShapeDtypeStruct((seq, QKV_DIM), BF16), jax.ShapeDtypeStruct((1, QKV_DIM), F32)],
        sem=("arbitrary",))(dq, dkc, dkp, dkp, dvc, dvp, dvp, rc, rs)


def _adamw(w, g, m, v):
    m = ADAM_B1 * m + (1.0 - ADAM_B1) * g
    v = ADAM_B2 * v + (1.0 - ADAM_B2) * (g * g)
    m_hat = m / (1.0 - ADAM_B1 ** ADAM_STEP)
    v_hat = v / (1.0 - ADAM_B2 ** ADAM_STEP)
    delta = -ADAM_LR * (m_hat / (jnp.sqrt(v_hat) + ADAM_EPS) + ADAM_WD * w)
    return delta, m, v


def _sum_slots(name, parts, rows):
    cols = parts.shape[2]

    def body(p_ref, g_ref):
        g = p_ref[0].astype(F32)
        for d in range(1, N_DEV):
            g = g + p_ref[d].astype(F32)
        g_ref[...] = g

    return _call(
        body, name=name, grid=(1,),
        in_specs=[pl.BlockSpec((N_DEV, rows, cols), lambda i: (0, 0, 0))],
        out_specs=_full((rows, cols)), out_shape=jax.ShapeDtypeStruct((rows, cols), F32),
        sem=("arbitrary",))(parts)


def _adamw_native(name, g, w, m, v):
    layers, rows, cols = w.shape
    tr = rows if rows <= 512 else 256

    def body(g_ref, w_ref, m_ref, v_ref, d_ref, nm_ref, nv_ref):
        d_ref[...], nm_ref[...], nv_ref[...] = _adamw(w_ref[...], g_ref[...], m_ref[...], v_ref[...])

    spec = pl.BlockSpec((1, tr, cols), lambda l, i: (l, i, 0))
    shape = jax.ShapeDtypeStruct(w.shape, F32)
    return _call(
        body, name=name, grid=(layers, rows // tr), in_specs=[spec, spec, spec, spec],
        out_specs=[spec, spec, spec], out_shape=[shape, shape, shape],
        sem=("parallel", "parallel"))(g, w, m, v)


def _adamw_from_slots(name, slots, row_block, w, m, v):
    layers, rows, cols = w.shape

    def body(*refs):
        slot_refs = refs[:layers]
        w_ref, m_ref, v_ref, g_ref, d_ref, nm_ref, nv_ref = refs[layers:]
        for l in range(layers):
            @pl.when(pl.program_id(0) == l)
            def _(p_ref=slot_refs[l]):
                g = p_ref[0].astype(F32)
                for d in range(1, N_DEV):
                    g = g + p_ref[d].astype(F32)
                g_ref[0] = g
        d, nm, nv = _adamw(w_ref[0], g_ref[0], m_ref[0], v_ref[0])
        d_ref[0], nm_ref[0], nv_ref[0] = d, nm, nv

    slot_spec = pl.BlockSpec((N_DEV, rows, cols), lambda l: (0, row_block, 0))
    spec = pl.BlockSpec((1, rows, cols), lambda l: (l, 0, 0))
    shape = jax.ShapeDtypeStruct(w.shape, F32)
    return _call(
        body, name=name, grid=(layers,), in_specs=[slot_spec] * layers + [spec, spec, spec],
        out_specs=[spec, spec, spec, spec], out_shape=[shape, shape, shape, shape],
        sem=("arbitrary",))(*slots, w, m, v)


def _adamw_replicated(parts, w, m, v):
    def body(p_ref, w_ref, m_ref, v_ref, g_ref, d_ref, nm_ref, nv_ref):
        g = p_ref[0]
        for j in range(1, N_DEV):
            g = g + p_ref[j]
        g_ref[...] = g
        d_ref[...], nm_ref[...], nv_ref[...] = _adamw(w_ref[...], g, m_ref[...], v_ref[...])

    spec = _full((REPL_ROWS, D_MODEL))
    shape = jax.ShapeDtypeStruct((REPL_ROWS, D_MODEL), F32)
    return _call(
        body, name="adamw_replicated", grid=(1,),
        in_specs=[_full((N_DEV, REPL_ROWS, D_MODEL)), spec, spec, spec],
        out_specs=[spec, spec, spec, spec], out_shape=[shape, shape, shape, shape],
        sem=("arbitrary",))(parts, w, m, v)


_MESH = pl.DeviceIdType.MESH
_ANY = pl.BlockSpec(memory_space=pl.ANY)


def _all_gather(name, xs):
    rows, cols = xs.shape

    def body(x_ref, out_ref, send_sems, recv_sems, local_sem):
        x, y, c = lax.axis_index("x"), lax.axis_index("y"), lax.axis_index("c")
        me, sibling = (x, y, c), (x, y, 1 - c)
        chips = [(1 - x, y), (x, 1 - y), (1 - x, 1 - y)]

        def slot(px, py, pc):
            return out_ref.at[4 * px + 2 * py + pc]

        def copy(k, block, to, src=None):
            return pltpu.make_async_remote_copy(
                src_ref=slot(*block) if src is None else src, dst_ref=slot(*block),
                send_sem=send_sems.at[k], recv_sem=recv_sems.at[k], device_id=to, device_id_type=_MESH)

        mine = pltpu.make_async_copy(x_ref, slot(*me), local_sem)
        mine.start()
        first = [copy(0, me, sibling, src=x_ref)]
        first += [copy(1 + j, me, (*chip, c), src=x_ref) for j, chip in enumerate(chips)]
        for cp in first:
            cp.start()
        passed = [copy(4 + j, (*chip, c), sibling) for j, chip in enumerate(chips)]
        for j, chip in enumerate(chips):
            copy(1 + j, (*chip, c), me).wait_recv()
            passed[j].start()
        copy(0, sibling, me).wait_recv()
        for j, chip in enumerate(chips):
            copy(4 + j, (*chip, 1 - c), me).wait_recv()
        for cp in first + passed:
            cp.wait_send()
        mine.wait()

    return pl.pallas_call(
        body, name=name, out_shape=jax.ShapeDtypeStruct((N_DEV, rows, cols), xs.dtype),
        in_specs=[_ANY], out_specs=_ANY,
        scratch_shapes=[pltpu.SemaphoreType.DMA((7,)), pltpu.SemaphoreType.DMA((7,)), pltpu.SemaphoreType.DMA],
    )(xs)


N_PEERS = N_DEV - 1
_HBM = pl.BlockSpec(memory_space=pltpu.HBM)
_SEM = pl.BlockSpec(memory_space=pltpu.SEMAPHORE)
_DATAFLOW = pltpu.SideEffectType.DATAFLOW_SIDE_EFFECTING
_TOKEN = jax.ShapeDtypeStruct((SUBLANES, LANES), F32)


def _peers():
    x, y, c = lax.axis_index("x"), lax.axis_index("y"), lax.axis_index("c")
    out = []
    for k in range(1, N_DEV):
        px = 1 - x if k & 4 else x
        py = 1 - y if k & 2 else y
        pc = 1 - c if k & 1 else c
        out.append(((px, py, pc), 4 * px + 2 * py + pc))
    return 4 * x + 2 * y + c, out


def _in_hbm(a):
    return pltpu.with_memory_space_constraint(a, pltpu.HBM)


def _landing(rows):
    return _in_hbm(lax.empty((N_DEV, rows, D_MODEL), BF16))


SEMS_PER_GROUP = 3


def _group_sems():
    return pltpu.SemaphoreType.DMA((N_PEERS,)), pltpu.SemaphoreType.DMA((N_PEERS,)), pltpu.SemaphoreType.DMA(())


def _gather_start(name, payloads):
    n = len(payloads)
    ns = SEMS_PER_GROUP

    def body(*refs):
        src, land = refs[:n], refs[n:2 * n]
        sems = refs[2 * n:(2 + ns) * n]
        token = refs[-1]
        me, peers = _peers()
        for g in range(n):
            send_sems, recv_sems, own_sem = sems[ns * g:ns * (g + 1)]
            for k, (pos, _) in enumerate(peers):
                pltpu.make_async_remote_copy(
                    src_ref=src[g], dst_ref=land[g].at[me], send_sem=send_sems.at[k],
                    recv_sem=recv_sems.at[k], device_id=pos, device_id_type=_MESH).start()
            pltpu.make_async_copy(src[g], land[g].at[me], own_sem).start()
        token[...] = jnp.zeros_like(token)

    lands = [_landing(p.shape[0]) for p in payloads]
    sem_shapes = [s for _ in payloads for s in _group_sems()]
    hbm_shapes = [pltpu.HBM(a.shape, a.dtype) for a in list(payloads) + lands]
    out = pl.pallas_call(
        body, name=name, out_shape=(*sem_shapes, *hbm_shapes, _TOKEN),
        in_specs=[_HBM] * (2 * n),
        out_specs=(*[_SEM] * (ns * n), *[_HBM] * (2 * n), pl.BlockSpec(memory_space=pltpu.VMEM)),
        input_output_aliases={i: ns * n + i for i in range(2 * n)},
        compiler_params=pltpu.CompilerParams(has_side_effects=_DATAFLOW),
    )(*[_in_hbm(p) for p in payloads], *lands)
    sems, thru = out[:ns * n], out[ns * n:(ns + 2) * n]
    return [(thru[g], thru[n + g], *sems[ns * g:ns * (g + 1)]) for g in range(n)], out[-1]


def _gather_wait(name, group, after):
    payload, land, send_sems, recv_sems, own_sem = group

    def body(src_ref, land_ref, send_ref, recv_ref, own_ref, after_ref, src_out, land_out):
        me, peers = _peers()
        for k, (pos, idx) in enumerate(peers):
            cp = pltpu.make_async_remote_copy(
                src_ref=src_ref, dst_ref=land_ref.at[idx], send_sem=send_ref.at[k], recv_sem=recv_ref.at[k],
                device_id=pos, device_id_type=_MESH)
            cp.wait_send()
            cp.wait_recv()
        pltpu.make_async_copy(src_ref, land_ref.at[me], own_ref).wait()

    _, land = pl.pallas_call(
        body, name=name, out_shape=(pltpu.HBM(payload.shape, payload.dtype), pltpu.HBM(land.shape, land.dtype)),
        in_specs=[_HBM, _HBM, _SEM, _SEM, _SEM, _ANY], out_specs=(_HBM, _HBM), input_output_aliases={0: 0, 1: 1},
        compiler_params=pltpu.CompilerParams(has_side_effects=_DATAFLOW),
    )(payload, land, send_sems, recv_sems, own_sem, after)
    return land


def _scatter_start(name, blocks):
    rows = blocks.shape[1]

    def body(blocks_ref, land_ref, send_sems, recv_sems, own_sem, blocks_out, land_out, token):
        me, peers = _peers()
        for k, (pos, idx) in enumerate(peers):
            pltpu.make_async_remote_copy(
                src_ref=blocks_ref.at[idx], dst_ref=land_ref.at[me], send_sem=send_sems.at[k],
                recv_sem=recv_sems.at[k], device_id=pos, device_id_type=_MESH).start()
        pltpu.make_async_copy(blocks_ref.at[me], land_ref.at[me], own_sem).start()
        token[...] = jnp.zeros_like(token)

    land = _landing(rows)
    send_sems, recv_sems, own_sem, blocks_thru, land_thru, token = pl.pallas_call(
        body, name=name,
        out_shape=(*_group_sems(), pltpu.HBM(blocks.shape, blocks.dtype), pltpu.HBM(land.shape, land.dtype), _TOKEN),
        in_specs=[_HBM, _HBM], out_specs=(_SEM, _SEM, _SEM, _HBM, _HBM, pl.BlockSpec(memory_space=pltpu.VMEM)),
        input_output_aliases={0: 3, 1: 4},
        compiler_params=pltpu.CompilerParams(has_side_effects=_DATAFLOW),
    )(_in_hbm(blocks), land)
    return (blocks_thru, land_thru, send_sems, recv_sems, own_sem), token


def _scatter_wait(groups, after):
    n = len(groups)
    ns = SEMS_PER_GROUP

    def body(*refs):
        blocks, land = refs[:n], refs[n:2 * n]
        sems = refs[2 * n:(2 + ns) * n]
        me, peers = _peers()
        for g in range(n):
            send_sems, recv_sems, own_sem = sems[ns * g:ns * (g + 1)]
            for k, (pos, idx) in enumerate(peers):
                cp = pltpu.make_async_remote_copy(
                    src_ref=blocks[g].at[idx], dst_ref=land[g].at[idx], send_sem=send_sems.at[k],
                    recv_sem=recv_sems.at[k], device_id=pos, device_id_type=_MESH)
                cp.wait_send()
                cp.wait_recv()
            pltpu.make_async_copy(blocks[g].at[me], land[g].at[me], own_sem).wait()

    hbm = [grp[0] for grp in groups] + [grp[1] for grp in groups]
    sems = [s for grp in groups for s in grp[2:]]
    out = pl.pallas_call(
        body, name="rs_wait", out_shape=tuple(pltpu.HBM(a.shape, a.dtype) for a in hbm),
        in_specs=[_HBM] * (2 * n) + [_SEM] * (ns * n) + [_ANY], out_specs=tuple([_HBM] * (2 * n)),
        input_output_aliases={i: i for i in range(2 * n)},
        compiler_params=pltpu.CompilerParams(has_side_effects=_DATAFLOW),
    )(*hbm, *sems, after)
    return list(out[n:])


def _pad_rows(flat, rows):
    return jnp.pad(flat, (0, rows * D_MODEL - flat.shape[0])).reshape(rows, D_MODEL)


SMALL_NAMES = ("conv_b_pw1", "conv_w_dw", "conv_b_dw", "conv_ln_g", "conv_ln_b", "conv_b_pw2")


def _pack_small(p):
    flat = jnp.concatenate([p[n].reshape(-1) for n in SMALL_NAMES])
    return _pad_rows(flat, ROWS_SMALL).reshape(1, ROWS_SMALL, D_MODEL)


def _unpack_small(packed):
    flat = packed.reshape(-1)
    c = D_MODEL // N_DEV
    shapes = ((1, 2 * c), (1, CONV_WIDTH, c), (1, c), (1, c), (1, c), (1, c))
    out, o = {}, 0
    for n, shape in zip(SMALL_NAMES, shapes):
        size = shape[-1] * (shape[1] if len(shape) == 3 else 1)
        out[n] = flat[o:o + size].reshape(shape)
        o += size
    return out


def _gather_payloads(p):
    t = lambda a: jnp.swapaxes(a, -1, -2).astype(BF16)
    w1t, w3t, w2 = t(p["ffn_w1"]), t(p["ffn_w3"]), p["ffn_w2"].astype(BF16)
    bits = lax.bitcast_convert_type(_pack_small(p).reshape(-1)[:ROWS_SMALL * D_MODEL // 2], jnp.uint32)
    halves = [(bits >> 16).astype(jnp.uint16), (bits & 0xFFFF).astype(jnp.uint16)]
    small = lax.bitcast_convert_type(jnp.concatenate(halves), BF16).reshape(ROWS_SMALL, D_MODEL)
    conv = jnp.concatenate([t(p["conv_w_pw1"][0]), p["conv_w_pw2"][0].astype(BF16), small], axis=0)
    ffn = [jnp.concatenate([w1t[l], w3t[l], w2[l]], axis=0) for l in range(2)]
    return [t(p["attn_w_qkv"][0]), p["attn_w_o"][0].astype(BF16), ffn[0], conv, ffn[1]]


def _device_rows(land, lo, n):
    return land[:, lo:lo + n].reshape(N_DEV * n, D_MODEL)


def _unpack_conv(land):
    words = lax.bitcast_convert_type(land[:, ROWS_PW1 + ROWS_PW2:], jnp.uint16).astype(jnp.uint32)
    words = words.reshape(N_DEV, 2, ROWS_SMALL * D_MODEL // 2)
    small = lax.bitcast_convert_type((words[:, 0] << 16) | words[:, 1], F32)
    c = D_MODEL // N_DEV
    b_pw1 = small[:, :2 * c].reshape(1, 2 * D_MODEL)
    s = 2 * c
    w_dw = small[:, s:s + CONV_WIDTH * c].reshape(N_DEV, CONV_WIDTH, c).transpose(1, 0, 2).reshape(CONV_WIDTH, D_MODEL)
    s += CONV_WIDTH * c
    b_dw, ln_g, ln_b, b_pw2 = (small[:, s + i * c:s + (i + 1) * c].reshape(1, D_MODEL) for i in range(4))
    return dict(w_pw1_t=_device_rows(land, 0, ROWS_PW1), w_pw2=_device_rows(land, ROWS_PW1, ROWS_PW2),
                b_pw1=b_pw1, w_dw=w_dw, b_dw=b_dw, ln_g=ln_g, ln_b=ln_b, b_pw2=b_pw2)


def _dest_blocks(mats):
    return jnp.concatenate([a.reshape(N_DEV, -1, D_MODEL) for a in mats], axis=1)


def _small_grad_rows(g_bpw1, g_dw, g_bdw, g_lng, g_lnb, g_bpw2):
    c = D_MODEL // N_DEV
    small = jnp.concatenate(
        [g_bpw1.reshape(N_DEV, 2 * c), g_dw.reshape(CONV_WIDTH, N_DEV, c).transpose(1, 0, 2).reshape(N_DEV, -1),
         g_bdw.reshape(N_DEV, c), g_lng.reshape(N_DEV, c), g_lnb.reshape(N_DEV, c), g_bpw2.reshape(N_DEV, c)], axis=1)
    small = jnp.pad(small, ((0, 0), (0, ROWS_SMALL * D_MODEL - SMALL_USED)))
    return small.reshape(N_DEV * ROWS_SMALL, D_MODEL).astype(BF16)


LOSS_ROW = 9


def _pack_replicated(norm_mix, norm_ffn, b_qkv, sinks, b_o, norm_final, extra=None):
    rows = [norm_mix.reshape(2, D_MODEL), norm_ffn.reshape(2, D_MODEL), _pad_rows(b_qkv.reshape(-1), 2),
            _pad_rows(sinks.reshape(-1), 1), b_o.reshape(1, D_MODEL), norm_final.reshape(1, D_MODEL)]
    if extra is not None:
        rows.append(_pad_rows(extra.reshape(-1), 1))
    p = jnp.concatenate(rows, axis=0)
    return jnp.pad(p, ((0, REPL_ROWS - p.shape[0]), (0, 0)))


def _unpack_replicated(p):
    return dict(norm_mix=p[0:2], norm_ffn=p[2:4], attn_b_qkv=p[4:6].reshape(-1)[:QKV_DIM].reshape(1, QKV_DIM),
                attn_sinks=p[6, :N_Q_HEADS].reshape(1, N_Q_HEADS), attn_b_o=p[7:8], norm_final=p[8])


WEIGHT_ORDER = ['norm_mix', 'norm_ffn', 'attn_w_qkv', 'attn_b_qkv', 'attn_sinks', 'attn_w_o', 'attn_b_o',
                'conv_w_pw1', 'conv_b_pw1', 'conv_w_dw', 'conv_b_dw', 'conv_ln_g', 'conv_ln_b', 'conv_w_pw2',
                'conv_b_pw2', 'ffn_w1', 'ffn_w3', 'ffn_w2', 'norm_final']


def kernel(x, norm_mix, norm_ffn, attn_w_qkv, attn_b_qkv, attn_sinks, attn_w_o, attn_b_o, conv_w_pw1, conv_b_pw1, conv_w_dw, conv_b_dw, conv_ln_g, conv_ln_b, conv_w_pw2, conv_b_pw2, ffn_w1, ffn_w3, ffn_w2, norm_final, loss_target, m_norm_mix, m_norm_ffn, m_attn_w_qkv, m_attn_b_qkv, m_attn_sinks, m_attn_w_o, m_attn_b_o, m_conv_w_pw1, m_conv_b_pw1, m_conv_w_dw, m_conv_b_dw, m_conv_ln_g, m_conv_ln_b, m_conv_w_pw2, m_conv_b_pw2, m_ffn_w1, m_ffn_w3, m_ffn_w2, m_norm_final, v_norm_mix, v_norm_ffn, v_attn_w_qkv, v_attn_b_qkv, v_attn_sinks, v_attn_w_o, v_attn_b_o, v_conv_w_pw1, v_conv_b_pw1, v_conv_w_dw, v_conv_b_dw, v_conv_ln_g, v_conv_ln_b, v_conv_w_pw2, v_conv_b_pw2, v_ffn_w1, v_ffn_w3, v_ffn_w2, v_norm_final):
    xs = x[0]
    target = loss_target[0]
    seq = xs.shape[0]

    w = dict(attn_w_qkv=attn_w_qkv, attn_w_o=attn_w_o, conv_w_pw1=conv_w_pw1, conv_b_pw1=conv_b_pw1,
             conv_w_dw=conv_w_dw, conv_b_dw=conv_b_dw, conv_ln_g=conv_ln_g, conv_ln_b=conv_ln_b,
             conv_w_pw2=conv_w_pw2, conv_b_pw2=conv_b_pw2, ffn_w1=ffn_w1, ffn_w3=ffn_w3, ffn_w2=ffn_w2)
    m = dict(attn_w_qkv=m_attn_w_qkv, attn_w_o=m_attn_w_o, conv_w_pw1=m_conv_w_pw1, conv_b_pw1=m_conv_b_pw1,
             conv_w_dw=m_conv_w_dw, conv_b_dw=m_conv_b_dw, conv_ln_g=m_conv_ln_g, conv_ln_b=m_conv_ln_b,
             conv_w_pw2=m_conv_w_pw2, conv_b_pw2=m_conv_b_pw2, ffn_w1=m_ffn_w1, ffn_w3=m_ffn_w3, ffn_w2=m_ffn_w2)
    v = dict(attn_w_qkv=v_attn_w_qkv, attn_w_o=v_attn_w_o, conv_w_pw1=v_conv_w_pw1, conv_b_pw1=v_conv_b_pw1,
             conv_w_dw=v_conv_w_dw, conv_b_dw=v_conv_b_dw, conv_ln_g=v_conv_ln_g, conv_ln_b=v_conv_ln_b,
             conv_w_pw2=v_conv_w_pw2, conv_b_pw2=v_conv_b_pw2, ffn_w1=v_ffn_w1, ffn_w3=v_ffn_w3, ffn_w2=v_ffn_w2)

    payloads = _gather_payloads(w)
    (ag_qkv, ag_wo), tok = _gather_start("ag_start_attn", payloads[:2])
    (ag_ffn0, ag_conv, ag_ffn1), tok = _gather_start(
        "ag_start_rest", [payloads[2] + tok[0, 0].astype(BF16), payloads[3], payloads[4]])
    rc, rs = _rope_tables(seq)
    sinks = attn_sinks.reshape(N_Q_HEADS)
    g_mix0, g_mix1 = norm_mix[0:1] + tok[0, 0], norm_mix[1:2]
    g_ffn0, g_ffn1 = norm_ffn[0:1], norm_ffn[1:2]
    g_fin = norm_final.reshape(1, D_MODEL)

    y0 = _rms_fwd(xs, g_mix0)
    w_qkv_t = _gather_wait("ag_wait_qkv", ag_qkv, y0).reshape(QKV_DIM, D_MODEL)
    q, k, vv = _qkv_fwd(y0, w_qkv_t, attn_b_qkv, rc, rs)
    attn, probs, p_sinks = _attn_fwd(sinks, q, k, vv)
    w_o = _gather_wait("ag_wait_wo", ag_wo, attn).reshape(Q_DIM, D_MODEL)
    h1 = _mm_res("attn_out_proj", attn, w_o, attn_b_o, xs)
    w_ffn0 = _gather_wait("ag_wait_ffn0", ag_ffn0, h1)
    f0, u0, p0, s0 = _ffn_up("ffn0_up", h1, g_ffn0, w_ffn0)
    h2 = _ffn_down("ffn0_down", s0, w_ffn0, h1)
    wt = _unpack_conv(_gather_wait("ag_wait_conv", ag_conv, h2))
    wd = jnp.concatenate([wt["w_dw"][::-1], jnp.zeros((TAPS_PAD - CONV_WIDTH, D_MODEL), F32)], axis=0)
    y1, a, dwc, z = _conv_fwd(h2, g_mix1, wt["w_pw1_t"], wt["b_pw1"], wd, wt["b_dw"], wt["ln_g"], wt["ln_b"])
    h3 = _mm_res("conv_out_proj", z, wt["w_pw2"], wt["b_pw2"], h2)
    w_ffn1 = _gather_wait("ag_wait_ffn1", ag_ffn1, h3)
    f1, u1, p1, s1 = _ffn_up("ffn1_up", h3, g_ffn1, w_ffn1)
    dh4, dh4b, sq, dg_fin = _ffn_down_loss(s1, w_ffn1, h3, target, g_fin)

    du1, dp1 = _ffn_bwd_act("ffn1_bwd_act", dh4b, u1, p1, w_ffn1)
    dh3, dh3b, dg_ffn1 = _ffn_bwd_in("ffn1_bwd_in", du1, dp1, w_ffn1, h3, dh4, g_ffn1)
    blocks = _mm_tn("ffn1_dw2", s1, dh4b, tk=FF_TILE, into=(None, W2_SLOT))
    blocks = _mm_tn("ffn1_dw1", du1, f1, tk=FF_TILE, into=(blocks, W1T_SLOT))
    blocks = _mm_tn("ffn1_dw3", dp1, f1, tk=FF_TILE, into=(blocks, W3T_SLOT))
    rs_ffn1, tok = _scatter_start("rs_start_ffn1", blocks)

    da, dlg, dlb, dbdw, dwd, dbpw1, dbpw2 = _conv_bwd(dh3, wt["w_pw2"], dwc, a, wt["ln_g"] + tok[0, 0],
                                                     wt["ln_b"], wd)
    gpw2 = _mm_tn("conv_dw_pw2", z, dh3b, tk=D_MODEL)
    dh2, dh2b, dg_mix1 = _mm_rms_bwd("conv_in_bwd", da, wt["w_pw1_t"], h2, dh3, g_mix1, True)
    gpw1t = _mm_tn("conv_dw_pw1", da, y1, tk=D_MODEL)
    small_rows = _small_grad_rows(dbpw1, dwd[:CONV_WIDTH][::-1], dbdw, dlg, dlb, dbpw2)
    rs_conv, tok = _scatter_start("rs_start_conv", _dest_blocks([gpw1t, gpw2, small_rows]))

    du0, dp0 = _ffn_bwd_act("ffn0_bwd_act", dh2b, u0, p0, w_ffn0)
    dh1, dh1b, dg_ffn0 = _ffn_bwd_in("ffn0_bwd_in", du0, dp0, w_ffn0, h1, dh2, g_ffn0 + tok[0, 0])
    blocks = _mm_tn("ffn0_dw2", s0, dh2b, tk=FF_TILE, into=(None, W2_SLOT))
    blocks = _mm_tn("ffn0_dw1", du0, f0, tk=FF_TILE, into=(blocks, W1T_SLOT))
    blocks = _mm_tn("ffn0_dw3", dp0, f0, tk=FF_TILE, into=(blocks, W3T_SLOT))
    rs_ffn0, tok = _scatter_start("rs_start_ffn0", blocks)

    gwo = _mm_tn("attn_dw_o", attn, dh1b, tk=D_MODEL)
    rs_wo, tok2 = _scatter_start("rs_start_wo", _dest_blocks([gwo]))
    dattn, dbo = _nt_bias("attn_out_bwd", dh1, w_o)
    dq, dkc, dkp, dvc, dvp, dsink = _attn_bwd(probs, p_sinks + (tok[0, 0] + tok2[0, 0]), q, k, vv, dattn)
    dqkv, dbqkv = _rope_bwd(dq, dkc, dkp, dvc, dvp, rc, rs)
    gqkvt = _mm_tn("attn_dw_qkv", dqkv, y0, tk=QKV_DIM)
    rs_qkv, tok = _scatter_start("rs_start_qkv", _dest_blocks([gqkvt]))
    dx, dg_mix0 = _mm_rms_bwd("qkv_in_bwd", dqkv, w_qkv_t, xs, dh1, g_mix0 + tok[0, 0], False)

    p_ffn1, p_conv, p_ffn0, p_wo, p_qkv = _scatter_wait([rs_ffn1, rs_conv, rs_ffn0, rs_wo, rs_qkv], dx)

    tr = lambda a: jnp.swapaxes(a, -1, -2)
    same = lambda a: a
    ffn_slots = [p_ffn0, p_ffn1]
    plan = dict(
        ffn_w1=(ffn_slots, W1T_SLOT, tr), ffn_w3=(ffn_slots, W3T_SLOT, tr), ffn_w2=(ffn_slots, W2_SLOT, same),
        attn_w_qkv=([p_qkv], 0, tr), attn_w_o=([p_wo], 0, same),
        conv_w_pw2=([p_conv], ROWS_PW1 // ROWS_PW2, same))
    sharded = [{}, {}, {}, {}]
    for n, (slots, row_block, view) in plan.items():
        outs_n = _adamw_from_slots("adamw_" + n, slots, row_block, view(w[n]), view(m[n]), view(v[n]))
        for dst, t in zip(sharded, outs_n):
            dst[n] = view(t)
    small_out = _adamw_from_slots("adamw_small", [p_conv], (ROWS_PW1 + ROWS_PW2) // ROWS_SMALL,
                                  _pack_small(w), _pack_small(m), _pack_small(v))
    for dst, t in zip(sharded, small_out):
        dst.update(_unpack_small(t))
    g_pw1 = _sum_slots("rs_sum_pw1", p_conv, ROWS_PW1).T[None]
    pw1_out = _adamw_native("adamw_conv_w_pw1", g_pw1, w["conv_w_pw1"], m["conv_w_pw1"], v["conv_w_pw1"])
    for dst, t in zip(sharded, (g_pw1,) + tuple(pw1_out)):
        dst["conv_w_pw1"] = t

    part = _pack_replicated(jnp.concatenate([dg_mix0, dg_mix1]), jnp.concatenate([dg_ffn0, dg_ffn1]),
                            dbqkv, -dsink[:, 0], dbo, dg_fin, extra=sq[0, 0:1])
    parts = _all_gather("ag_replicated_grads", part)
    w_rep = _pack_replicated(norm_mix, norm_ffn, attn_b_qkv, attn_sinks, attn_b_o, norm_final)
    m_rep = _pack_replicated(m_norm_mix, m_norm_ffn, m_attn_b_qkv, m_attn_sinks, m_attn_b_o, m_norm_final)
    v_rep = _pack_replicated(v_norm_mix, v_norm_ffn, v_attn_b_qkv, v_attn_sinks, v_attn_b_o, v_norm_final)
    rep_out = _adamw_replicated(parts, w_rep, m_rep, v_rep)
    replicated = [_unpack_replicated(t) for t in rep_out]
    loss = rep_out[0][LOSS_ROW, 0] * (0.5 / D_MODEL)

    outs = [loss, dx.reshape(1, seq, D_MODEL)]
    for sh, rp in zip(sharded, replicated):
        merged = {**sh, **rp}
        outs += [merged[n] for n in WEIGHT_ORDER]
    return tuple(outs)
```

```python
import jax
import jax.numpy as jnp
from jax import lax
from jax.experimental import pallas as pl
from jax.experimental.pallas import tpu as pltpu

F32 = jnp.float32
BF16 = jnp.bfloat16

D_MODEL = 1024
HEAD_DIM = 64
N_Q_HEADS = 16
N_KV_HEADS = 2
Q_PER_KV = 8
Q_DIM = N_Q_HEADS * HEAD_DIM
KV_DIM = N_KV_HEADS * HEAD_DIM
QKV_DIM = Q_DIM + 2 * KV_DIM
BLOCK = 128
CONV_WIDTH = 31
D_FF = 2816
ROPE_THETA = 10000.0
RMS_EPS = 1e-5
LN_EPS = 1e-5
ADAM_LR = 0.001
ADAM_B1 = 0.9
ADAM_B2 = 0.999
ADAM_EPS = 1e-08
ADAM_WD = 0.01
ADAM_STEP = 10
N_DEV = 8

LANES = 128
SUBLANES = 8
TOKEN_TILE = 512
CONV_CHUNK = 64
CONV_HALO = 32
TAPS_PAD = 32
VMEM_LIMIT = 56 * 1024 * 1024
NEG_INF = float(jnp.finfo(jnp.float32).min)

ROWS_FF = D_FF // N_DEV
W1T_SLOT, W3T_SLOT, W2_SLOT = 0, 1, 2
ROWS_PW1 = 2 * D_MODEL // N_DEV
ROWS_PW2 = D_MODEL // N_DEV
ROWS_SMALL = 16
SMALL_USED = 2 * D_MODEL // N_DEV + CONV_WIDTH * (D_MODEL // N_DEV) + 4 * (D_MODEL // N_DEV)
FF_SPLIT = 2
FF_TILE_DEVS = N_DEV // FF_SPLIT
FF_TILE = FF_TILE_DEVS * ROWS_FF
REPL_ROWS = 16


def _call(body, *, name, grid, in_specs, out_specs, out_shape, scratch=(), sem=None, aliases=None):
    return pl.pallas_call(
        body, name=name, grid=grid, in_specs=in_specs, out_specs=out_specs, out_shape=out_shape,
        scratch_shapes=list(scratch), input_output_aliases=aliases or {},
        compiler_params=pltpu.CompilerParams(dimension_semantics=sem, vmem_limit_bytes=VMEM_LIMIT))


def _full(shape):
    return pl.BlockSpec(shape, lambda *_: (0,) * len(shape))


def _resident(shape):
    return pl.BlockSpec(shape, lambda *_: (0,) * len(shape), pipeline_mode=pl.Buffered(1))


def _rows(tm, n):
    return pl.BlockSpec((tm, n), lambda i, *_: (i, 0))


def _sig(x):
    return 1.0 / (1.0 + jnp.exp(-x))


def _sum_rows(x):
    return jnp.sum(x, axis=0, keepdims=True)


def _nt(a, b):
    return lax.dot_general(a, b, (((1,), (1,)), ((), ())), preferred_element_type=F32)


def _tn(a, b):
    return lax.dot_general(a, b, (((0,), (0,)), ((), ())), preferred_element_type=F32)


def _rms_stats(x):
    return lax.rsqrt(jnp.mean(x * x, axis=-1, keepdims=True) + RMS_EPS)


def _rms_bwd(dy, x, g, dres):
    r = _rms_stats(x)
    n = x * r
    dn = dy * g
    dx = dres + r * (dn - n * jnp.mean(dn * n, axis=-1, keepdims=True))
    return dx, _sum_rows(dy * n)


def _rope_tables(seq):
    half = HEAD_DIM // 2
    pos = jnp.arange(seq, dtype=F32)
    inv_freq = ROPE_THETA ** (-jnp.arange(0, HEAD_DIM, 2, dtype=F32) / HEAD_DIM)
    ang = pos[:, None] * inv_freq[None, :]
    spread = lambda t: jnp.broadcast_to(t[:, None, :], (seq, LANES // half, half)).reshape(seq, LANES)
    return spread(jnp.cos(ang)), spread(jnp.sin(ang))


def _signed_sin(sin):
    lane = lax.broadcasted_iota(jnp.int32, sin.shape, 1)
    first_half = (lane & (HEAD_DIM - 1)) < HEAD_DIM // 2
    return jnp.where(first_half, -sin, 0.0), jnp.where(first_half, 0.0, sin)


def _rope(t, c, sa, sb):
    half = HEAD_DIM // 2
    return t * c + pltpu.roll(t, LANES - half, 1) * sa + pltpu.roll(t, half, 1) * sb


def _rope_t(dt, c, sa, sb):
    half = HEAD_DIM // 2
    return dt * c + pltpu.roll(dt * sa, half, 1) + pltpu.roll(dt * sb, LANES - half, 1)


def _rms_fwd(x, g):
    seq = x.shape[0]
    tm = min(2 * TOKEN_TILE, seq)

    def body(x_ref, g_ref, y_ref):
        xv = x_ref[...]
        y_ref[...] = (xv * _rms_stats(xv) * g_ref[...]).astype(BF16)

    return _call(
        body, name="rms_fwd", grid=(seq // tm,), in_specs=[_rows(tm, D_MODEL), _full((1, D_MODEL))],
        out_specs=_rows(tm, D_MODEL), out_shape=jax.ShapeDtypeStruct((seq, D_MODEL), BF16),
        sem=("parallel",))(x, g)


def _qkv_fwd(y, w, b, rc, rs):
    seq = y.shape[0]
    tm = min(TOKEN_TILE, seq)

    def body(y_ref, w_ref, b_ref, c_ref, s_ref, q_ref, k_ref, v_ref):
        qkv = _nt(y_ref[...], w_ref[...]) + b_ref[...]
        c = c_ref[...]
        sa, sb = _signed_sin(s_ref[...])
        for i in range(Q_DIM // LANES):
            blk = _rope(qkv[:, i * LANES:(i + 1) * LANES], c, sa, sb)
            q_ref[:, i * LANES:(i + 1) * LANES] = (blk * (HEAD_DIM ** -0.5)).astype(BF16)
        k_ref[...] = _rope(qkv[:, Q_DIM:Q_DIM + KV_DIM], c, sa, sb).astype(BF16)
        v_ref[...] = qkv[:, Q_DIM + KV_DIM:].astype(BF16)

    return _call(
        body, name="qkv_fwd", grid=(seq // tm,),
        in_specs=[_rows(tm, D_MODEL), _resident((QKV_DIM, D_MODEL)), _full((1, QKV_DIM)),
                  _rows(tm, LANES), _rows(tm, LANES)],
        out_specs=[_rows(tm, Q_DIM), _rows(tm, KV_DIM), _rows(tm, KV_DIM)],
        out_shape=[jax.ShapeDtypeStruct((seq, Q_DIM), BF16),
                   jax.ShapeDtypeStruct((seq, KV_DIM), BF16), jax.ShapeDtypeStruct((seq, KV_DIM), BF16)],
        sem=("parallel",))(y, w, b, rc, rs)


def _band_mask(n):
    row = lax.broadcasted_iota(jnp.int32, (BLOCK, 2 * BLOCK), 0)
    col = lax.broadcasted_iota(jnp.int32, (BLOCK, 2 * BLOCK), 1)
    rel = row + BLOCK - col
    return (rel >= 0) & (rel < BLOCK) & ((col >= BLOCK) | (n > 0))


def _softmax_with_sink(s, mask, sink):
    s = jnp.where(mask, s, NEG_INF)
    m = jnp.maximum(jnp.max(s, axis=-1, keepdims=True), sink)
    p = jnp.exp(s - m)
    e_sink = jnp.exp(sink - m)
    inv = 1.0 / (jnp.sum(p, axis=-1, keepdims=True) + e_sink)
    return p * inv, e_sink * inv


PAIRS_PER_KV = Q_PER_KV // 2


def _kv_specs():
    cur = pl.BlockSpec((BLOCK, KV_DIM), lambda n: (n, 0))
    prev = pl.BlockSpec((BLOCK, KV_DIM), lambda n: (jnp.maximum(n - 1, 0), 0))
    return cur, prev


def _low_lanes():
    return lax.broadcasted_iota(jnp.int32, (2 * BLOCK, KV_DIM), 1) < HEAD_DIM


def _kv_low_high(prev, cur, j, low):
    both = jnp.concatenate([prev, cur], axis=0).astype(F32)
    swapped = pltpu.roll(both, HEAD_DIM, 1)
    at_low, at_high = (both, swapped) if j == 0 else (swapped, both)
    return jnp.where(low, at_low, 0.0).astype(BF16), jnp.where(low, 0.0, at_high).astype(BF16)


def _fold_pair_halves(acc, j, low):
    folded = acc + pltpu.roll(acc, HEAD_DIM, 1)
    return jnp.where(low, folded, 0.0) if j == 0 else jnp.where(low, 0.0, folded)


def _pair_lanes(j, i):
    g = j * PAIRS_PER_KV + i
    return slice(g * LANES, (g + 1) * LANES), 2 * g


def _attn_fwd(sinks, q, k, v):
    seq = q.shape[0]
    cur, prev = _kv_specs()

    def body(sink_ref, q_ref, kc_ref, kp_ref, vc_ref, vp_ref, o_ref, p_ref, ps_ref):
        mask = _band_mask(pl.program_id(0))
        lane = lax.broadcasted_iota(jnp.int32, (BLOCK, LANES), 1)
        p_sinks = jnp.zeros((BLOCK, LANES), F32)
        for j in range(N_KV_HEADS):
            cs = slice(j * HEAD_DIM, (j + 1) * HEAD_DIM)
            kk = jnp.concatenate([kp_ref[:, cs], kc_ref[:, cs]], axis=0)
            vv = jnp.concatenate([vp_ref[:, cs], vc_ref[:, cs]], axis=0)
            for gq in range(Q_PER_KV):
                h = j * Q_PER_KV + gq
                hs = slice(h * HEAD_DIM, (h + 1) * HEAD_DIM)
                probs, p_sink = _softmax_with_sink(_nt(q_ref[:, hs], kk), mask, sink_ref[h])
                pb = probs.astype(BF16)
                p_ref[h] = pb
                p_sinks = jnp.where(lane == h, p_sink, p_sinks)
                o_ref[:, hs] = jnp.dot(pb, vv, preferred_element_type=F32).astype(BF16)
        ps_ref[...] = p_sinks

    return _call(
        body, name="attn_fwd", grid=(seq // BLOCK,),
        in_specs=[pl.BlockSpec(memory_space=pltpu.SMEM), _rows(BLOCK, Q_DIM), cur, prev, cur, prev],
        out_specs=[_rows(BLOCK, Q_DIM), pl.BlockSpec((N_Q_HEADS, BLOCK, 2 * BLOCK), lambda n: (0, n, 0)),
                   _rows(BLOCK, LANES)],
        out_shape=[jax.ShapeDtypeStruct((seq, Q_DIM), BF16),
                   jax.ShapeDtypeStruct((N_Q_HEADS, seq, 2 * BLOCK), BF16),
                   jax.ShapeDtypeStruct((seq, LANES), F32)],
        sem=("parallel",))(sinks, q, k, k, v, v)


def _mm_res(name, a, w, b, res, g):
    seq, kdim = a.shape
    n = w.shape[1]
    tm = min(2 * TOKEN_TILE, seq)

    def body(a_ref, w_ref, b_ref, r_ref, g_ref, o_ref, f_ref):
        hv = r_ref[...] + (jnp.dot(a_ref[...], w_ref[...], preferred_element_type=F32) + b_ref[...])
        o_ref[...] = hv
        f_ref[...] = (hv * _rms_stats(hv) * g_ref[...]).astype(BF16)

    return _call(
        body, name=name, grid=(seq // tm,),
        in_specs=[_rows(tm, kdim), _resident((kdim, n)), _full((1, n)), _rows(tm, n), _full((1, n))],
        out_specs=[_rows(tm, n), _rows(tm, n)],
        out_shape=[jax.ShapeDtypeStruct((seq, n), F32), jax.ShapeDtypeStruct((seq, n), BF16)],
        sem=("parallel",))(a, w, b, res, g)


def _ff_tile_spec(slot):
    return pl.BlockSpec((FF_TILE_DEVS, ROWS_FF, D_MODEL), lambda j, i: (j, slot, 0))


def _ff_whole_spec(slot):
    return pl.BlockSpec((N_DEV, ROWS_FF, D_MODEL), lambda i: (0, slot, 0), pipeline_mode=pl.Buffered(1))


def _ffn_up(name, f, gathered):
    seq = f.shape[0]
    tm = min(TOKEN_TILE, seq)

    def body(f_ref, w1_ref, w3_ref, u_ref, w_ref, s_ref):
        fv = f_ref[...]
        u = _nt(fv, w1_ref[...].reshape(FF_TILE, D_MODEL))
        w = _nt(fv, w3_ref[...].reshape(FF_TILE, D_MODEL))
        u_ref[...] = u.astype(BF16)
        w_ref[...] = w.astype(BF16)
        s_ref[...] = (u * _sig(u) * w).astype(BF16)

    tile_ff = pl.BlockSpec((tm, FF_TILE), lambda j, i: (i, j))
    ff_shape = jax.ShapeDtypeStruct((seq, D_FF), BF16)
    return _call(
        body, name=name, grid=(FF_SPLIT, seq // tm),
        in_specs=[pl.BlockSpec((tm, D_MODEL), lambda j, i: (i, 0)), _ff_tile_spec(W1T_SLOT), _ff_tile_spec(W3T_SLOT)],
        out_specs=[tile_ff, tile_ff, tile_ff], out_shape=[ff_shape, ff_shape, ff_shape],
        sem=("parallel", "parallel"))(f, gathered, gathered)


def _ffn_down(name, s, gathered, res):
    seq = s.shape[0]
    tm = min(2 * TOKEN_TILE, seq)

    def body(s_ref, w_ref, r_ref, o_ref):
        w2 = w_ref[...].reshape(D_FF, D_MODEL)
        o_ref[...] = r_ref[...] + jnp.dot(s_ref[...], w2, preferred_element_type=F32)

    return _call(
        body, name=name, grid=(seq // tm,),
        in_specs=[_rows(tm, D_FF), _ff_whole_spec(W2_SLOT), _rows(tm, D_MODEL)],
        out_specs=_rows(tm, D_MODEL), out_shape=jax.ShapeDtypeStruct((seq, D_MODEL), F32),
        sem=("parallel",))(s, gathered, res)


def _conv_fwd(h, g, wpw1, bpw1, wd, bdw, lng, lnb):
    seq = h.shape[0]
    tm = min(TOKEN_TILE, seq)
    n_chunks = tm // CONV_CHUNK
    win = CONV_CHUNK + CONV_HALO

    def body(h_ref, g_ref, w_ref, b_ref, wd_ref, bdw_ref, lng_ref, lnb_ref,
             y_ref, a_ref, dwc_ref, z_ref, gbuf):
        i = pl.program_id(0)

        @pl.when(i == 0)
        def _():
            gbuf[0:CONV_HALO, :] = jnp.zeros((CONV_HALO, D_MODEL), F32)

        @pl.when(i > 0)
        def _():
            gbuf[0:CONV_HALO, :] = gbuf[tm:tm + CONV_HALO, :]

        hv = h_ref[...]
        y = (hv * _rms_stats(hv) * g_ref[...]).astype(BF16)
        y_ref[...] = y
        a = _nt(y, w_ref[...]) + b_ref[...]
        a_ref[...] = a.astype(BF16)
        gbuf[CONV_HALO:CONV_HALO + tm, :] = a[:, :D_MODEL] * _sig(a[:, D_MODEL:])

        def chunk(c, carry):
            r0 = pl.multiple_of(c * CONV_CHUNK, CONV_CHUNK)
            for l in range(D_MODEL // LANES):
                ls = slice(l * LANES, (l + 1) * LANES)
                gw = gbuf[pl.ds(r0, win), ls]
                acc = jnp.zeros((CONV_CHUNK, LANES), F32) + bdw_ref[:, ls]
                for s in range(SUBLANES):
                    gs = gw if s == 0 else pltpu.roll(gw, s, 0)
                    for q in range(CONV_HALO // SUBLANES):
                        d = SUBLANES * q + s
                        if d < CONV_WIDTH:
                            lo = CONV_HALO - SUBLANES * q
                            acc = acc + wd_ref[d:d + 1, ls] * gs[lo:lo + CONV_CHUNK]
                dwc_ref[pl.ds(r0, CONV_CHUNK), ls] = acc
            return carry

        lax.fori_loop(0, n_chunks, chunk, 0)

        xv = dwc_ref[...]
        mu = jnp.mean(xv, axis=-1, keepdims=True)
        xc = xv - mu
        var = jnp.mean(xc * xc, axis=-1, keepdims=True)
        ln = xc * lax.rsqrt(var + LN_EPS) * lng_ref[...] + lnb_ref[...]
        z_ref[...] = (ln * _sig(ln)).astype(BF16)

    return _call(
        body, name="conv_fwd", grid=(seq // tm,),
        in_specs=[_rows(tm, D_MODEL), _full((1, D_MODEL)), _full((2 * D_MODEL, D_MODEL)), _full((1, 2 * D_MODEL)),
                  _full((TAPS_PAD, D_MODEL)), _full((1, D_MODEL)), _full((1, D_MODEL)), _full((1, D_MODEL))],
        out_specs=[_rows(tm, D_MODEL), _rows(tm, 2 * D_MODEL), _rows(tm, D_MODEL), _rows(tm, D_MODEL)],
        out_shape=[jax.ShapeDtypeStruct((seq, D_MODEL), BF16), jax.ShapeDtypeStruct((seq, 2 * D_MODEL), BF16),
                   jax.ShapeDtypeStruct((seq, D_MODEL), F32), jax.ShapeDtypeStruct((seq, D_MODEL), BF16)],
        scratch=[pltpu.VMEM((tm + CONV_HALO, D_MODEL), F32)],
        sem=("arbitrary",))(h, g, wpw1, bpw1, wd, bdw, lng, lnb)


def _ffn_down_loss(s, gathered, res, target, g):
    seq = s.shape[0]
    tm = min(TOKEN_TILE, seq)

    def body(s_ref, w_ref, r_ref, t_ref, g_ref, dh_ref, dhb_ref, loss_ref, dg_ref):
        @pl.when(pl.program_id(0) == 0)
        def _():
            loss_ref[...] = jnp.zeros_like(loss_ref)
            dg_ref[...] = jnp.zeros_like(dg_ref)

        hv = r_ref[...] + jnp.dot(s_ref[...], w_ref[...].reshape(D_FF, D_MODEL), preferred_element_type=F32)
        gv = g_ref[...]
        err = hv * _rms_stats(hv) * gv - t_ref[...]
        sq = jnp.sum(jnp.sum(err * err, axis=-1, keepdims=True), axis=0, keepdims=True)
        loss_ref[...] += jnp.broadcast_to(sq, loss_ref.shape)
        dx, dg = _rms_bwd(err * (1.0 / D_MODEL), hv, gv, 0.0)
        dh_ref[...] = dx
        dhb_ref[...] = dx.astype(BF16)
        dg_ref[...] += dg

    return _call(
        body, name="ffn1_down_loss", grid=(seq // tm,),
        in_specs=[_rows(tm, D_FF), _ff_whole_spec(W2_SLOT), _rows(tm, D_MODEL), _rows(tm, D_MODEL),
                  _full((1, D_MODEL))],
        out_specs=[_rows(tm, D_MODEL), _rows(tm, D_MODEL), _full((SUBLANES, LANES)), _full((1, D_MODEL))],
        out_shape=[jax.ShapeDtypeStruct((seq, D_MODEL), F32), jax.ShapeDtypeStruct((seq, D_MODEL), BF16),
                   jax.ShapeDtypeStruct((SUBLANES, LANES), F32), jax.ShapeDtypeStruct((1, D_MODEL), F32)],
        sem=("arbitrary",))(s, gathered, res, target, g)


def _ffn_bwd_act(name, dh, u, w, gathered):
    seq = dh.shape[0]
    tm = min(TOKEN_TILE, seq)

    def body(dh_ref, u_ref, w_ref, w2_ref, du_ref, dw_ref):
        ds = _nt(dh_ref[...], w2_ref[...].reshape(FF_TILE, D_MODEL))
        uv = u_ref[...].astype(F32)
        sg = _sig(uv)
        dw_ref[...] = (ds * (uv * sg)).astype(BF16)
        du_ref[...] = (ds * w_ref[...].astype(F32) * (sg * (1.0 + uv * (1.0 - sg)))).astype(BF16)

    tile_ff = pl.BlockSpec((tm, FF_TILE), lambda j, i: (i, j))
    ff_shape = jax.ShapeDtypeStruct((seq, D_FF), BF16)
    return _call(
        body, name=name, grid=(FF_SPLIT, seq // tm),
        in_specs=[pl.BlockSpec((tm, D_MODEL), lambda j, i: (i, 0)), tile_ff, tile_ff, _ff_tile_spec(W2_SLOT)],
        out_specs=[tile_ff, tile_ff], out_shape=[ff_shape, ff_shape],
        sem=("parallel", "parallel"))(dh, u, w, gathered)


def _ffn_bwd_in(name, du, dw, gathered, h_in, dres, g):
    seq = du.shape[0]
    tm = min(TOKEN_TILE, seq)

    def body(du_ref, dw_ref, w1_ref, w3_ref, h_ref, dr_ref, g_ref, dx_ref, dxb_ref, dg_ref):
        @pl.when(pl.program_id(0) == 0)
        def _():
            dg_ref[...] = jnp.zeros_like(dg_ref)

        df = jnp.dot(du_ref[...], w1_ref[...].reshape(D_FF, D_MODEL), preferred_element_type=F32)
        df = df + jnp.dot(dw_ref[...], w3_ref[...].reshape(D_FF, D_MODEL), preferred_element_type=F32)
        dx, dg = _rms_bwd(df, h_ref[...], g_ref[...], dr_ref[...])
        dx_ref[...] = dx
        dxb_ref[...] = dx.astype(BF16)
        dg_ref[...] += dg

    return _call(
        body, name=name, grid=(seq // tm,),
        in_specs=[_rows(tm, D_FF), _rows(tm, D_FF), _ff_whole_spec(W1T_SLOT), _ff_whole_spec(W3T_SLOT),
                  _rows(tm, D_MODEL), _rows(tm, D_MODEL), _full((1, D_MODEL))],
        out_specs=[_rows(tm, D_MODEL), _rows(tm, D_MODEL), _full((1, D_MODEL))],
        out_shape=[jax.ShapeDtypeStruct((seq, D_MODEL), F32), jax.ShapeDtypeStruct((seq, D_MODEL), BF16),
                   jax.ShapeDtypeStruct((1, D_MODEL), F32)],
        sem=("arbitrary",))(du, dw, gathered, gathered, h_in, dres, g)


def _mm_tn(name, a, b, *, tk, into=None):
    seq, kdim = a.shape
    n = b.shape[1]
    tt = min((4 if b.dtype == BF16 else 2) * TOKEN_TILE, seq)
    n_t = seq // tt
    devs = tk // ROWS_FF

    def body(a_ref, b_ref, *rest):
        o_ref, acc = rest[-2:]
        t = pl.program_id(1)

        @pl.when(t == 0)
        def _():
            acc[...] = jnp.zeros_like(acc)

        acc[...] += _tn(a_ref[...].astype(BF16), b_ref[...].astype(BF16))

        @pl.when(t == n_t - 1)
        def _():
            out = acc[...].astype(BF16)
            o_ref[...] = out if into is None else out.reshape(devs, ROWS_FF, n)

    in_specs = [pl.BlockSpec((tt, tk), lambda k, t: (t, k)), pl.BlockSpec((tt, n), lambda k, t: (t, 0))]
    args = [a, b]
    aliases = None
    if into is None:
        out_spec = pl.BlockSpec((tk, n), lambda k, t: (k, 0))
        out_shape = jax.ShapeDtypeStruct((kdim, n), BF16)
    else:
        blocks, slot = into
        out_spec = pl.BlockSpec((devs, ROWS_FF, n), lambda k, t: (k, slot, 0))
        out_shape = jax.ShapeDtypeStruct((N_DEV, 3 * ROWS_FF, n), BF16)
        if blocks is not None:
            in_specs.append(_ANY)
            args.append(blocks)
            aliases = {2: 0}
    return _call(
        body, name=name, grid=(kdim // tk, n_t), in_specs=in_specs, out_specs=out_spec, out_shape=out_shape,
        scratch=[pltpu.VMEM((tk, n), F32)], sem=("parallel", "arbitrary"), aliases=aliases)(*args)


def _conv_bwd(dh, wpw2, dwc, a, lng, lnb, wd):
    seq = dh.shape[0]
    tm = min(TOKEN_TILE, seq)
    nt = seq // tm
    n_chunks = tm // CONV_CHUNK
    win = CONV_CHUNK + CONV_HALO
    halo_per_tile = tm // CONV_HALO

    def body(dh_ref, w_ref, dwc_ref, a_ref, ah_ref, lng_ref, lnb_ref, wd_ref,
             da_ref, dlg_ref, dlb_ref, dbdw_ref, dwd_ref, dbpw1_ref, dbpw2_ref,
             gbuf, dbuf, dglu, dwd_part):
        i = pl.program_id(0)
        r = nt - 1 - i

        @pl.when(i == 0)
        def _():
            dlg_ref[...] = jnp.zeros_like(dlg_ref)
            dlb_ref[...] = jnp.zeros_like(dlb_ref)
            dbdw_ref[...] = jnp.zeros_like(dbdw_ref)
            dbpw1_ref[...] = jnp.zeros_like(dbpw1_ref)
            dbpw2_ref[...] = jnp.zeros_like(dbpw2_ref)
            dwd_part[...] = jnp.zeros_like(dwd_part)
            dbuf[tm:tm + CONV_HALO, :] = jnp.zeros((CONV_HALO, D_MODEL), F32)

        @pl.when(i > 0)
        def _():
            dbuf[tm:tm + CONV_HALO, :] = dbuf[0:CONV_HALO, :]

        dhv = dh_ref[...]
        dbpw2_ref[...] += _sum_rows(dhv)
        dz = _nt(dhv.astype(BF16), w_ref[...])
        xv = dwc_ref[...]
        lg = lng_ref[...]
        mu = jnp.mean(xv, axis=-1, keepdims=True)
        xc = xv - mu
        rstd = lax.rsqrt(jnp.mean(xc * xc, axis=-1, keepdims=True) + LN_EPS)
        xhat = xc * rstd
        ln = xhat * lg + lnb_ref[...]
        sg = _sig(ln)
        dln = dz * (sg * (1.0 + ln * (1.0 - sg)))
        dlg_ref[...] += _sum_rows(dln * xhat)
        dlb_ref[...] += _sum_rows(dln)
        dxh = dln * lg
        ddw = rstd * (dxh - jnp.mean(dxh, axis=-1, keepdims=True)
                      - xhat * jnp.mean(dxh * xhat, axis=-1, keepdims=True))
        dbdw_ref[...] += _sum_rows(ddw)
        dbuf[0:tm, :] = ddw

        av = a_ref[...].astype(F32)
        a1 = av[:, :D_MODEL]
        s2 = _sig(av[:, D_MODEL:])
        gbuf[CONV_HALO:CONV_HALO + tm, :] = a1 * s2
        ah = ah_ref[...].astype(F32)
        gh = ah[:, :D_MODEL] * _sig(ah[:, D_MODEL:])
        gbuf[0:CONV_HALO, :] = jnp.where(r > 0, gh, 0.0)

        def chunk(c, carry):
            r0 = pl.multiple_of(c * CONV_CHUNK, CONV_CHUNK)
            for l in range(D_MODEL // LANES):
                ls = slice(l * LANES, (l + 1) * LANES)
                dw_ = dbuf[pl.ds(r0, win), ls]
                gw = gbuf[pl.ds(r0, win), ls]
                acc = jnp.zeros((CONV_CHUNK, LANES), F32)
                for s in range(SUBLANES):
                    ds_ = dw_ if s == 0 else pltpu.roll(dw_, win - s, 0)
                    for q in range(CONV_HALO // SUBLANES):
                        d = SUBLANES * q + s
                        if d < CONV_WIDTH:
                            acc = acc + wd_ref[d:d + 1, ls] * ds_[SUBLANES * q:SUBLANES * q + CONV_CHUNK]
                            lo = CONV_HALO - SUBLANES * q
                            prod = ds_[0:CONV_CHUNK] * gw[lo:lo + CONV_CHUNK]
                            dwd_part[d, :, ls] += jnp.sum(
                                prod.reshape(CONV_CHUNK // SUBLANES, SUBLANES, LANES), axis=0)
                dglu[pl.ds(r0, CONV_CHUNK), ls] = acc
            return carry

        lax.fori_loop(0, n_chunks, chunk, 0)

        dg_ = dglu[...]
        da1 = dg_ * s2
        da2 = dg_ * a1 * s2 * (1.0 - s2)
        da_ref[:, :D_MODEL] = da1.astype(BF16)
        da_ref[:, D_MODEL:] = da2.astype(BF16)
        dbpw1_ref[:, :D_MODEL] += _sum_rows(da1)
        dbpw1_ref[:, D_MODEL:] += _sum_rows(da2)

        @pl.when(i == nt - 1)
        def _():
            dwd_ref[...] = jnp.sum(dwd_part[...], axis=1)

    rev = lambda n: pl.BlockSpec((tm, n), lambda i: (nt - 1 - i, 0))
    halo = pl.BlockSpec((CONV_HALO, 2 * D_MODEL),
                        lambda i: (jnp.maximum((nt - 1 - i) * halo_per_tile - 1, 0), 0))
    vec = lambda n: _full((1, n))
    return _call(
        body, name="conv_bwd", grid=(nt,),
        in_specs=[rev(D_MODEL), _full((D_MODEL, D_MODEL)), rev(D_MODEL), rev(2 * D_MODEL), halo,
                  vec(D_MODEL), vec(D_MODEL), _full((TAPS_PAD, D_MODEL))],
        out_specs=[rev(2 * D_MODEL), vec(D_MODEL), vec(D_MODEL), vec(D_MODEL), _full((TAPS_PAD, D_MODEL)),
                   vec(2 * D_MODEL), vec(D_MODEL)],
        out_shape=[jax.ShapeDtypeStruct((seq, 2 * D_MODEL), BF16), jax.ShapeDtypeStruct((1, D_MODEL), F32),
                   jax.ShapeDtypeStruct((1, D_MODEL), F32), jax.ShapeDtypeStruct((1, D_MODEL), F32),
                   jax.ShapeDtypeStruct((TAPS_PAD, D_MODEL), F32), jax.ShapeDtypeStruct((1, 2 * D_MODEL), F32),
                   jax.ShapeDtypeStruct((1, D_MODEL), F32)],
        scratch=[pltpu.VMEM((tm + CONV_HALO, D_MODEL), F32), pltpu.VMEM((tm + CONV_HALO, D_MODEL), F32),
                 pltpu.VMEM((tm, D_MODEL), F32), pltpu.VMEM((TAPS_PAD, SUBLANES, D_MODEL), F32)],
        sem=("arbitrary",))(dh, wpw2, dwc, a, a, lng, lnb, wd)


def _mm_rms_bwd(name, dact, wt, h_in, dres, g, bf16_copy):
    seq, n = dact.shape
    tm = min(TOKEN_TILE, seq)

    def body(da_ref, w_ref, h_ref, dr_ref, g_ref, dx_ref, *rest):
        dg_ref = rest[-1]

        @pl.when(pl.program_id(0) == 0)
        def _():
            dg_ref[...] = jnp.zeros_like(dg_ref)

        dy = jnp.dot(da_ref[...], w_ref[...], preferred_element_type=F32)
        dx, dg = _rms_bwd(dy, h_ref[...], g_ref[...], dr_ref[...])
        dx_ref[...] = dx
        if bf16_copy:
            rest[0][...] = dx.astype(BF16)
        dg_ref[...] += dg

    copy_spec = [_rows(tm, D_MODEL)] if bf16_copy else []
    copy_shape = [jax.ShapeDtypeStruct((seq, D_MODEL), BF16)] if bf16_copy else []
    return _call(
        body, name=name, grid=(seq // tm,),
        in_specs=[_rows(tm, n), _resident((n, D_MODEL)), _rows(tm, D_MODEL), _rows(tm, D_MODEL), _full((1, D_MODEL))],
        out_specs=[_rows(tm, D_MODEL), *copy_spec, _full((1, D_MODEL))],
        out_shape=[jax.ShapeDtypeStruct((seq, D_MODEL), F32), *copy_shape, jax.ShapeDtypeStruct((1, D_MODEL), F32)],
        sem=("arbitrary",))(dact, wt, h_in, dres, g)


def _nt_bias(name, dy, w):
    seq, n = dy.shape
    kdim = w.shape[0]
    tm = min(2 * TOKEN_TILE, seq)

    def body(dy_ref, w_ref, o_ref, db_ref):
        @pl.when(pl.program_id(0) == 0)
        def _():
            db_ref[...] = jnp.zeros_like(db_ref)

        dyv = dy_ref[...]
        db_ref[...] += _sum_rows(dyv)
        o_ref[...] = _nt(dyv.astype(BF16), w_ref[...]).astype(BF16)

    return _call(
        body, name=name, grid=(seq // tm,),
        in_specs=[_rows(tm, n), _resident((kdim, n))],
        out_specs=[_rows(tm, kdim), _full((1, n))],
        out_shape=[jax.ShapeDtypeStruct((seq, kdim), BF16), jax.ShapeDtypeStruct((1, n), F32)],
        sem=("arbitrary",))(dy, w)


def _attn_bwd(probs, p_sinks, q, k, v, do):
    seq = q.shape[0]
    per = 2 if seq % (2 * BLOCK) == 0 else 1
    tile = per * BLOCK
    cur = pl.BlockSpec((tile, KV_DIM), lambda n: (n, 0))
    prev = pl.BlockSpec((BLOCK, KV_DIM), lambda n: (jnp.maximum(n * per - 1, 0), 0))

    def body(p_ref, ps_ref, q_ref, kc_ref, kp_ref, vc_ref, vp_ref, do_ref,
             dq_ref, dkc_ref, dkp_ref, dvc_ref, dvp_ref, dsink_ref):
        @pl.when(pl.program_id(0) == 0)
        def _():
            dsink_ref[...] = jnp.zeros_like(dsink_ref)

        low = _low_lanes()
        lane = lax.broadcasted_iota(jnp.int32, (BLOCK, LANES), 1)
        for b in range(per):
            rows = slice(b * BLOCK, (b + 1) * BLOCK)
            before = slice((b - 1) * BLOCK, b * BLOCK)
            k_prev, v_prev = (kp_ref[...], vp_ref[...]) if b == 0 else (kc_ref[before, :], vc_ref[before, :])
            k_cur, v_cur = kc_ref[rows, :], vc_ref[rows, :]
            p_sinks_blk = ps_ref[rows, :]
            dk_all = jnp.zeros((2 * BLOCK, KV_DIM), F32)
            dv_all = jnp.zeros((2 * BLOCK, KV_DIM), F32)
            for j in range(N_KV_HEADS):
                k_lo, k_hi = _kv_low_high(k_prev, k_cur, j, low)
                v_lo, v_hi = _kv_low_high(v_prev, v_cur, j, low)
                dk_acc = jnp.zeros((2 * BLOCK, KV_DIM), F32)
                dv_acc = jnp.zeros((2 * BLOCK, KV_DIM), F32)
                for i in range(PAIRS_PER_KV):
                    ls, h = _pair_lanes(j, i)
                    qp = q_ref[rows, ls]
                    dop = do_ref[rows, ls]
                    dsb, pb16 = [], []
                    for t, v_sel in enumerate((v_lo, v_hi)):
                        pb = p_ref[h + t, rows, :]
                        pf = pb.astype(F32)
                        p_sink = jnp.sum(jnp.where(lane == h + t, p_sinks_blk, 0.0), axis=-1, keepdims=True)
                        dp = _nt(dop, v_sel)
                        delta = jnp.sum(pf * dp, axis=-1, keepdims=True)
                        dsb.append((pf * (dp - delta)).astype(BF16))
                        pb16.append(pb)
                        dsink_ref[h + t:h + t + 1, :] += jnp.broadcast_to(_sum_rows(p_sink * delta), (1, LANES))
                    dq = (jnp.dot(dsb[0], k_lo, preferred_element_type=F32)
                          + jnp.dot(dsb[1], k_hi, preferred_element_type=F32))
                    dq_ref[rows, ls] = dq * (HEAD_DIM ** -0.5)
                    dk_acc = dk_acc + jnp.where(low, _tn(dsb[0], qp), _tn(dsb[1], qp))
                    dv_acc = dv_acc + jnp.where(low, _tn(pb16[0], dop), _tn(pb16[1], dop))
                dk_all = dk_all + _fold_pair_halves(dk_acc, j, low)
                dv_all = dv_all + _fold_pair_halves(dv_acc, j, low)
            dkp_ref[rows, :] = dk_all[:BLOCK]
            dkc_ref[rows, :] = dk_all[BLOCK:]
            dvp_ref[rows, :] = dv_all[:BLOCK]
            dvc_ref[rows, :] = dv_all[BLOCK:]

    kv_out = _rows(tile, KV_DIM)
    kv_shape = jax.ShapeDtypeStruct((seq, KV_DIM), F32)
    return _call(
        body, name="attn_bwd", grid=(seq // tile,),
        in_specs=[pl.BlockSpec((N_Q_HEADS, tile, 2 * BLOCK), lambda n: (0, n, 0)), _rows(tile, LANES),
                  _rows(tile, Q_DIM), cur, prev, cur, prev, _rows(tile, Q_DIM)],
        out_specs=[_rows(tile, Q_DIM), kv_out, kv_out, kv_out, kv_out, _full((N_Q_HEADS, LANES))],
        out_shape=[jax.ShapeDtypeStruct((seq, Q_DIM), F32), kv_shape, kv_shape, kv_shape, kv_shape,
                   jax.ShapeDtypeStruct((N_Q_HEADS, LANES), F32)],
        sem=("arbitrary",))(probs, p_sinks, q, k, k, v, v, do)


def _rope_bwd(dq, dkc, dkp, dvc, dvp, rc, rs):
    seq = dq.shape[0]
    nb = seq // BLOCK
    tm = min(TOKEN_TILE, seq)
    nt = seq // tm
    per = tm // BLOCK
    nxt = pl.BlockSpec((BLOCK, KV_DIM), lambda i: (jnp.minimum((i + 1) * per, nb - 1), 0))

    def body(dq_ref, dkc_ref, dkp_ref, dkn_ref, dvc_ref, dvp_ref, dvn_ref, c_ref, s_ref, o_ref, db_ref):
        i = pl.program_id(0)

        @pl.when(i == 0)
        def _():
            db_ref[...] = jnp.zeros_like(db_ref)

        c = c_ref[...]
        sa, sb = _signed_sin(s_ref[...])
        last = i == nt - 1

        def from_next_block(prev_ref, next_ref):
            tail = jnp.where(last, 0.0, next_ref[...])
            return tail if per == 1 else jnp.concatenate([prev_ref[BLOCK:, :], tail], axis=0)

        dk = dkc_ref[...] + from_next_block(dkp_ref, dkn_ref)
        dv = dvc_ref[...] + from_next_block(dvp_ref, dvn_ref)
        for l in range(Q_DIM // LANES):
            ls = slice(l * LANES, (l + 1) * LANES)
            blk = _rope_t(dq_ref[:, ls], c, sa, sb)
            o_ref[:, ls] = blk.astype(BF16)
            db_ref[:, ls] += _sum_rows(blk)
        dkr = _rope_t(dk, c, sa, sb)
        o_ref[:, Q_DIM:Q_DIM + KV_DIM] = dkr.astype(BF16)
        db_ref[:, Q_DIM:Q_DIM + KV_DIM] += _sum_rows(dkr)
        o_ref[:, Q_DIM + KV_DIM:] = dv.astype(BF16)
        db_ref[:, Q_DIM + KV_DIM:] += _sum_rows(dv)

    kv = _rows(tm, KV_DIM)
    tab = _rows(tm, LANES)
    return _call(
        body, name="rope_bwd", grid=(nt,),
        in_specs=[_rows(tm, Q_DIM), kv, kv, nxt, kv, kv, nxt, tab, tab],
        out_specs=[_rows(tm, QKV_DIM), _full((1, QKV_DIM))],
        out_shape=[jax.ShapeDtypeStruct((seq, QKV_DIM), BF16), jax.ShapeDtypeStruct((1, QKV_DIM), F32)],
        sem=("arbitrary",))(dq, dkc, dkp, dkp, dvc, dvp, dvp, rc, rs)


def _adamw(w, g, m, v):
    m = ADAM_B1 * m + (1.0 - ADAM_B1) * g
    v = ADAM_B2 * v + (1.0 - ADAM_B2) * (g * g)
    m_hat = m / (1.0 - ADAM_B1 ** ADAM_STEP)
    v_hat = v / (1.0 - ADAM_B2 ** ADAM_STEP)
    delta = -ADAM_LR * (m_hat / (jnp.sqrt(v_hat) + ADAM_EPS) + ADAM_WD * w)
    return delta, m, v


def _sum_slots(name, parts):
    _, rows, cols = parts.shape
    tr = rows if rows <= 512 else ROWS_FF

    def body(p_ref, g_ref):
        g = p_ref[0].astype(F32)
        for d in range(1, N_DEV):
            g = g + p_ref[d].astype(F32)
        g_ref[...] = g

    return _call(
        body, name=name, grid=(rows // tr,),
        in_specs=[pl.BlockSpec((N_DEV, tr, cols), lambda i: (0, i, 0))],
        out_specs=_rows(tr, cols), out_shape=jax.ShapeDtypeStruct((rows, cols), F32),
        sem=("parallel",))(parts)


def _adamw_native(name, g, w, m, v):
    layers, rows, cols = w.shape
    tr = rows if rows <= 512 else 256

    def body(g_ref, w_ref, m_ref, v_ref, d_ref, nm_ref, nv_ref):
        d_ref[...], nm_ref[...], nv_ref[...] = _adamw(w_ref[...], g_ref[...], m_ref[...], v_ref[...])

    spec = pl.BlockSpec((1, tr, cols), lambda l, i: (l, i, 0))
    shape = jax.ShapeDtypeStruct(w.shape, F32)
    return _call(
        body, name=name, grid=(layers, rows // tr), in_specs=[spec, spec, spec, spec],
        out_specs=[spec, spec, spec], out_shape=[shape, shape, shape],
        sem=("parallel", "parallel"))(g, w, m, v)


def _adamw_from_slots(name, slots, row_block, w, m, v):
    layers, rows, cols = w.shape

    def body(*refs):
        slot_refs = refs[:layers]
        w_ref, m_ref, v_ref, g_ref, d_ref, nm_ref, nv_ref = refs[layers:]
        for l in range(layers):
            @pl.when(pl.program_id(0) == l)
            def _(p_ref=slot_refs[l]):
                g = p_ref[0].astype(F32)
                for d in range(1, N_DEV):
                    g = g + p_ref[d].astype(F32)
                g_ref[0] = g
        d, nm, nv = _adamw(w_ref[0], g_ref[0], m_ref[0], v_ref[0])
        d_ref[0], nm_ref[0], nv_ref[0] = d, nm, nv

    slot_spec = pl.BlockSpec((N_DEV, rows, cols), lambda l: (0, row_block, 0))
    spec = pl.BlockSpec((1, rows, cols), lambda l: (l, 0, 0))
    shape = jax.ShapeDtypeStruct(w.shape, F32)
    return _call(
        body, name=name, grid=(layers,), in_specs=[slot_spec] * layers + [spec, spec, spec],
        out_specs=[spec, spec, spec, spec], out_shape=[shape, shape, shape, shape],
        sem=("arbitrary",))(*slots, w, m, v)


def _adamw_replicated(parts, w, m, v):
    def body(p_ref, w_ref, m_ref, v_ref, g_ref, d_ref, nm_ref, nv_ref):
        g = p_ref[0]
        for j in range(1, N_DEV):
            g = g + p_ref[j]
        g_ref[...] = g
        d_ref[...], nm_ref[...], nv_ref[...] = _adamw(w_ref[...], g, m_ref[...], v_ref[...])

    spec = _full((REPL_ROWS, D_MODEL))
    shape = jax.ShapeDtypeStruct((REPL_ROWS, D_MODEL), F32)
    return _call(
        body, name="adamw_replicated", grid=(1,),
        in_specs=[_full((N_DEV, REPL_ROWS, D_MODEL)), spec, spec, spec],
        out_specs=[spec, spec, spec, spec], out_shape=[shape, shape, shape, shape],
        sem=("arbitrary",))(parts, w, m, v)


_MESH = pl.DeviceIdType.MESH
_ANY = pl.BlockSpec(memory_space=pl.ANY)


def _all_gather(name, xs):
    rows, cols = xs.shape

    def body(x_ref, out_ref, send_sems, recv_sems, local_sem):
        x, y, c = lax.axis_index("x"), lax.axis_index("y"), lax.axis_index("c")
        me, sibling = (x, y, c), (x, y, 1 - c)
        chips = [(1 - x, y), (x, 1 - y), (1 - x, 1 - y)]

        def slot(px, py, pc):
            return out_ref.at[4 * px + 2 * py + pc]

        def copy(k, block, to, src=None):
            return pltpu.make_async_remote_copy(
                src_ref=slot(*block) if src is None else src, dst_ref=slot(*block),
                send_sem=send_sems.at[k], recv_sem=recv_sems.at[k], device_id=to, device_id_type=_MESH)

        mine = pltpu.make_async_copy(x_ref, slot(*me), local_sem)
        mine.start()
        first = [copy(0, me, sibling, src=x_ref)]
        first += [copy(1 + j, me, (*chip, c), src=x_ref) for j, chip in enumerate(chips)]
        for cp in first:
            cp.start()
        passed = [copy(4 + j, (*chip, c), sibling) for j, chip in enumerate(chips)]
        for j, chip in enumerate(chips):
            copy(1 + j, (*chip, c), me).wait_recv()
            passed[j].start()
        copy(0, sibling, me).wait_recv()
        for j, chip in enumerate(chips):
            copy(4 + j, (*chip, 1 - c), me).wait_recv()
        for cp in first + passed:
            cp.wait_send()
        mine.wait()

    return pl.pallas_call(
        body, name=name, out_shape=jax.ShapeDtypeStruct((N_DEV, rows, cols), xs.dtype),
        in_specs=[_ANY], out_specs=_ANY,
        scratch_shapes=[pltpu.SemaphoreType.DMA((7,)), pltpu.SemaphoreType.DMA((7,)), pltpu.SemaphoreType.DMA],
    )(xs)


N_PEERS = N_DEV - 1
_HBM = pl.BlockSpec(memory_space=pltpu.HBM)
_SEM = pl.BlockSpec(memory_space=pltpu.SEMAPHORE)
_DATAFLOW = pltpu.SideEffectType.DATAFLOW_SIDE_EFFECTING
_TOKEN = jax.ShapeDtypeStruct((SUBLANES, LANES), F32)


def _peers():
    x, y, c = lax.axis_index("x"), lax.axis_index("y"), lax.axis_index("c")
    out = []
    for k in range(1, N_DEV):
        px = 1 - x if k & 4 else x
        py = 1 - y if k & 2 else y
        pc = 1 - c if k & 1 else c
        out.append(((px, py, pc), 4 * px + 2 * py + pc))
    return 4 * x + 2 * y + c, out


def _in_hbm(a):
    return pltpu.with_memory_space_constraint(a, pltpu.HBM)


def _landing(rows):
    return _in_hbm(lax.empty((N_DEV, rows, D_MODEL), BF16))


SEMS_PER_GROUP = 3


def _group_sems():
    return pltpu.SemaphoreType.DMA((N_PEERS,)), pltpu.SemaphoreType.DMA((N_PEERS,)), pltpu.SemaphoreType.DMA(())


def _gather_start(name, payloads):
    n = len(payloads)
    ns = SEMS_PER_GROUP

    def body(*refs):
        src, land = refs[:n], refs[n:2 * n]
        sems = refs[2 * n:(2 + ns) * n]
        token = refs[-1]
        me, peers = _peers()
        for g in range(n):
            send_sems, recv_sems, own_sem = sems[ns * g:ns * (g + 1)]
            for k, (pos, _) in enumerate(peers):
                pltpu.make_async_remote_copy(
                    src_ref=src[g], dst_ref=land[g].at[me], send_sem=send_sems.at[k],
                    recv_sem=recv_sems.at[k], device_id=pos, device_id_type=_MESH).start()
            pltpu.make_async_copy(src[g], land[g].at[me], own_sem).start()
        token[...] = jnp.zeros_like(token)

    lands = [_landing(p.shape[0]) for p in payloads]
    sem_shapes = [s for _ in payloads for s in _group_sems()]
    hbm_shapes = [pltpu.HBM(a.shape, a.dtype) for a in list(payloads) + lands]
    out = pl.pallas_call(
        body, name=name, out_shape=(*sem_shapes, *hbm_shapes, _TOKEN),
        in_specs=[_HBM] * (2 * n),
        out_specs=(*[_SEM] * (ns * n), *[_HBM] * (2 * n), pl.BlockSpec(memory_space=pltpu.VMEM)),
        input_output_aliases={i: ns * n + i for i in range(2 * n)},
        compiler_params=pltpu.CompilerParams(has_side_effects=_DATAFLOW),
    )(*[_in_hbm(p) for p in payloads], *lands)
    sems, thru = out[:ns * n], out[ns * n:(ns + 2) * n]
    return [(thru[g], thru[n + g], *sems[ns * g:ns * (g + 1)]) for g in range(n)], out[-1]


def _gather_wait(name, group, after):
    payload, land, send_sems, recv_sems, own_sem = group

    def body(src_ref, land_ref, send_ref, recv_ref, own_ref, after_ref, src_out, land_out):
        me, peers = _peers()
        for k, (pos, idx) in enumerate(peers):
            cp = pltpu.make_async_remote_copy(
                src_ref=src_ref, dst_ref=land_ref.at[idx], send_sem=send_ref.at[k], recv_sem=recv_ref.at[k],
                device_id=pos, device_id_type=_MESH)
            cp.wait_send()
            cp.wait_recv()
        pltpu.make_async_copy(src_ref, land_ref.at[me], own_ref).wait()

    _, land = pl.pallas_call(
        body, name=name, out_shape=(pltpu.HBM(payload.shape, payload.dtype), pltpu.HBM(land.shape, land.dtype)),
        in_specs=[_HBM, _HBM, _SEM, _SEM, _SEM, _ANY], out_specs=(_HBM, _HBM), input_output_aliases={0: 0, 1: 1},
        compiler_params=pltpu.CompilerParams(has_side_effects=_DATAFLOW),
    )(payload, land, send_sems, recv_sems, own_sem, after)
    return land


def _scatter_start(name, blocks):
    rows = blocks.shape[1]

    def body(blocks_ref, land_ref, send_sems, recv_sems, own_sem, blocks_out, land_out, token):
        me, peers = _peers()
        for k, (pos, idx) in enumerate(peers):
            pltpu.make_async_remote_copy(
                src_ref=blocks_ref.at[idx], dst_ref=land_ref.at[me], send_sem=send_sems.at[k],
                recv_sem=recv_sems.at[k], device_id=pos, device_id_type=_MESH).start()
        pltpu.make_async_copy(blocks_ref.at[me], land_ref.at[me], own_sem).start()
        token[...] = jnp.zeros_like(token)

    land = _landing(rows)
    send_sems, recv_sems, own_sem, blocks_thru, land_thru, token = pl.pallas_call(
        body, name=name,
        out_shape=(*_group_sems(), pltpu.HBM(blocks.shape, blocks.dtype), pltpu.HBM(land.shape, land.dtype), _TOKEN),
        in_specs=[_HBM, _HBM], out_specs=(_SEM, _SEM, _SEM, _HBM, _HBM, pl.BlockSpec(memory_space=pltpu.VMEM)),
        input_output_aliases={0: 3, 1: 4},
        compiler_params=pltpu.CompilerParams(has_side_effects=_DATAFLOW),
    )(_in_hbm(blocks), land)
    return (blocks_thru, land_thru, send_sems, recv_sems, own_sem), token


def _scatter_wait(groups, after):
    n = len(groups)
    ns = SEMS_PER_GROUP

    def body(*refs):
        blocks, land = refs[:n], refs[n:2 * n]
        sems = refs[2 * n:(2 + ns) * n]
        me, peers = _peers()
        for g in range(n):
            send_sems, recv_sems, own_sem = sems[ns * g:ns * (g + 1)]
            for k, (pos, idx) in enumerate(peers):
                cp = pltpu.make_async_remote_copy(
                    src_ref=blocks[g].at[idx], dst_ref=land[g].at[idx], send_sem=send_sems.at[k],
                    recv_sem=recv_sems.at[k], device_id=pos, device_id_type=_MESH)
                cp.wait_send()
                cp.wait_recv()
            pltpu.make_async_copy(blocks[g].at[me], land[g].at[me], own_sem).wait()

    hbm = [grp[0] for grp in groups] + [grp[1] for grp in groups]
    sems = [s for grp in groups for s in grp[2:]]
    out = pl.pallas_call(
        body, name="rs_wait", out_shape=tuple(pltpu.HBM(a.shape, a.dtype) for a in hbm),
        in_specs=[_HBM] * (2 * n) + [_SEM] * (ns * n) + [_ANY], out_specs=tuple([_HBM] * (2 * n)),
        input_output_aliases={i: i for i in range(2 * n)},
        compiler_params=pltpu.CompilerParams(has_side_effects=_DATAFLOW),
    )(*hbm, *sems, after)
    return list(out[n:])


def _pad_rows(flat, rows):
    return jnp.pad(flat, (0, rows * D_MODEL - flat.shape[0])).reshape(rows, D_MODEL)


SMALL_NAMES = ("conv_b_pw1", "conv_w_dw", "conv_b_dw", "conv_ln_g", "conv_ln_b", "conv_b_pw2")


def _pack_small(p):
    flat = jnp.concatenate([p[n].reshape(-1) for n in SMALL_NAMES])
    return _pad_rows(flat, ROWS_SMALL).reshape(1, ROWS_SMALL, D_MODEL)


def _unpack_small(packed):
    flat = packed.reshape(-1)
    c = D_MODEL // N_DEV
    shapes = ((1, 2 * c), (1, CONV_WIDTH, c), (1, c), (1, c), (1, c), (1, c))
    out, o = {}, 0
    for n, shape in zip(SMALL_NAMES, shapes):
        size = shape[-1] * (shape[1] if len(shape) == 3 else 1)
        out[n] = flat[o:o + size].reshape(shape)
        o += size
    return out


def _gather_payloads(p):
    t = lambda a: jnp.swapaxes(a, -1, -2).astype(BF16)
    w1t, w3t, w2 = t(p["ffn_w1"]), t(p["ffn_w3"]), p["ffn_w2"].astype(BF16)
    bits = lax.bitcast_convert_type(_pack_small(p).reshape(-1)[:ROWS_SMALL * D_MODEL // 2], jnp.uint32)
    halves = [(bits >> 16).astype(jnp.uint16), (bits & 0xFFFF).astype(jnp.uint16)]
    small = lax.bitcast_convert_type(jnp.concatenate(halves), BF16).reshape(ROWS_SMALL, D_MODEL)
    conv = jnp.concatenate([t(p["conv_w_pw1"][0]), p["conv_w_pw2"][0].astype(BF16), small], axis=0)
    ffn = [jnp.concatenate([w1t[l], w3t[l], w2[l]], axis=0) for l in range(2)]
    return [t(p["attn_w_qkv"][0]), p["attn_w_o"][0].astype(BF16), ffn[0], conv, ffn[1]]


def _device_rows(land, lo, n):
    return land[:, lo:lo + n].reshape(N_DEV * n, D_MODEL)


def _unpack_conv(land):
    words = lax.bitcast_convert_type(land[:, ROWS_PW1 + ROWS_PW2:], jnp.uint16).astype(jnp.uint32)
    words = words.reshape(N_DEV, 2, ROWS_SMALL * D_MODEL // 2)
    small = lax.bitcast_convert_type((words[:, 0] << 16) | words[:, 1], F32)
    c = D_MODEL // N_DEV
    b_pw1 = small[:, :2 * c].reshape(1, 2 * D_MODEL)
    s = 2 * c
    w_dw = small[:, s:s + CONV_WIDTH * c].reshape(N_DEV, CONV_WIDTH, c).transpose(1, 0, 2).reshape(CONV_WIDTH, D_MODEL)
    s += CONV_WIDTH * c
    b_dw, ln_g, ln_b, b_pw2 = (small[:, s + i * c:s + (i + 1) * c].reshape(1, D_MODEL) for i in range(4))
    return dict(w_pw1_t=_device_rows(land, 0, ROWS_PW1), w_pw2=_device_rows(land, ROWS_PW1, ROWS_PW2),
                b_pw1=b_pw1, w_dw=w_dw, b_dw=b_dw, ln_g=ln_g, ln_b=ln_b, b_pw2=b_pw2)


def _dest_blocks(mats):
    return jnp.concatenate([a.reshape(N_DEV, -1, D_MODEL) for a in mats], axis=1)


def _small_grad_rows(g_bpw1, g_dw, g_bdw, g_lng, g_lnb, g_bpw2):
    c = D_MODEL // N_DEV
    small = jnp.concatenate(
        [g_bpw1.reshape(N_DEV, 2 * c), g_dw.reshape(CONV_WIDTH, N_DEV, c).transpose(1, 0, 2).reshape(N_DEV, -1),
         g_bdw.reshape(N_DEV, c), g_lng.reshape(N_DEV, c), g_lnb.reshape(N_DEV, c), g_bpw2.reshape(N_DEV, c)], axis=1)
    small = jnp.pad(small, ((0, 0), (0, ROWS_SMALL * D_MODEL - SMALL_USED)))
    return small.reshape(N_DEV * ROWS_SMALL, D_MODEL).astype(BF16)


LOSS_ROW = 9


def _pack_replicated(norm_mix, norm_ffn, b_qkv, sinks, b_o, norm_final, extra=None):
    rows = [norm_mix.reshape(2, D_MODEL), norm_ffn.reshape(2, D_MODEL), _pad_rows(b_qkv.reshape(-1), 2),
            _pad_rows(sinks.reshape(-1), 1), b_o.reshape(1, D_MODEL), norm_final.reshape(1, D_MODEL)]
    if extra is not None:
        rows.append(_pad_rows(extra.reshape(-1), 1))
    p = jnp.concatenate(rows, axis=0)
    return jnp.pad(p, ((0, REPL_ROWS - p.shape[0]), (0, 0)))


def _unpack_replicated(p):
    return dict(norm_mix=p[0:2], norm_ffn=p[2:4], attn_b_qkv=p[4:6].reshape(-1)[:QKV_DIM].reshape(1, QKV_DIM),
                attn_sinks=p[6, :N_Q_HEADS].reshape(1, N_Q_HEADS), attn_b_o=p[7:8], norm_final=p[8])


WEIGHT_ORDER = ['norm_mix', 'norm_ffn', 'attn_w_qkv', 'attn_b_qkv', 'attn_sinks', 'attn_w_o', 'attn_b_o',
                'conv_w_pw1', 'conv_b_pw1', 'conv_w_dw', 'conv_b_dw', 'conv_ln_g', 'conv_ln_b', 'conv_w_pw2',
                'conv_b_pw2', 'ffn_w1', 'ffn_w3', 'ffn_w2', 'norm_final']


def kernel(x, norm_mix, norm_ffn, attn_w_qkv, attn_b_qkv, attn_sinks, attn_w_o, attn_b_o, conv_w_pw1, conv_b_pw1, conv_w_dw, conv_b_dw, conv_ln_g, conv_ln_b, conv_w_pw2, conv_b_pw2, ffn_w1, ffn_w3, ffn_w2, norm_final, loss_target, m_norm_mix, m_norm_ffn, m_attn_w_qkv, m_attn_b_qkv, m_attn_sinks, m_attn_w_o, m_attn_b_o, m_conv_w_pw1, m_conv_b_pw1, m_conv_w_dw, m_conv_b_dw, m_conv_ln_g, m_conv_ln_b, m_conv_w_pw2, m_conv_b_pw2, m_ffn_w1, m_ffn_w3, m_ffn_w2, m_norm_final, v_norm_mix, v_norm_ffn, v_attn_w_qkv, v_attn_b_qkv, v_attn_sinks, v_attn_w_o, v_attn_b_o, v_conv_w_pw1, v_conv_b_pw1, v_conv_w_dw, v_conv_b_dw, v_conv_ln_g, v_conv_ln_b, v_conv_w_pw2, v_conv_b_pw2, v_ffn_w1, v_ffn_w3, v_ffn_w2, v_norm_final):
    xs = x[0]
    target = loss_target[0]
    seq = xs.shape[0]

    w = dict(attn_w_qkv=attn_w_qkv, attn_w_o=attn_w_o, conv_w_pw1=conv_w_pw1, conv_b_pw1=conv_b_pw1,
             conv_w_dw=conv_w_dw, conv_b_dw=conv_b_dw, conv_ln_g=conv_ln_g, conv_ln_b=conv_ln_b,
             conv_w_pw2=conv_w_pw2, conv_b_pw2=conv_b_pw2, ffn_w1=ffn_w1, ffn_w3=ffn_w3, ffn_w2=ffn_w2)
    m = dict(attn_w_qkv=m_attn_w_qkv, attn_w_o=m_attn_w_o, conv_w_pw1=m_conv_w_pw1, conv_b_pw1=m_conv_b_pw1,
             conv_w_dw=m_conv_w_dw, conv_b_dw=m_conv_b_dw, conv_ln_g=m_conv_ln_g, conv_ln_b=m_conv_ln_b,
             conv_w_pw2=m_conv_w_pw2, conv_b_pw2=m_conv_b_pw2, ffn_w1=m_ffn_w1, ffn_w3=m_ffn_w3, ffn_w2=m_ffn_w2)
    v = dict(attn_w_qkv=v_attn_w_qkv, attn_w_o=v_attn_w_o, conv_w_pw1=v_conv_w_pw1, conv_b_pw1=v_conv_b_pw1,
             conv_w_dw=v_conv_w_dw, conv_b_dw=v_conv_b_dw, conv_ln_g=v_conv_ln_g, conv_ln_b=v_conv_ln_b,
             conv_w_pw2=v_conv_w_pw2, conv_b_pw2=v_conv_b_pw2, ffn_w1=v_ffn_w1, ffn_w3=v_ffn_w3, ffn_w2=v_ffn_w2)

    payloads = _gather_payloads(w)
    (ag_qkv, ag_wo), tok = _gather_start("ag_start_attn", payloads[:2])
    (ag_ffn0, ag_conv, ag_ffn1), tok = _gather_start(
        "ag_start_rest", [payloads[2] + tok[0, 0].astype(BF16), payloads[3], payloads[4]])
    rc, rs = _rope_tables(seq)
    sinks = attn_sinks.reshape(N_Q_HEADS)
    g_mix0, g_mix1 = norm_mix[0:1] + tok[0, 0], norm_mix[1:2]
    g_ffn0, g_ffn1 = norm_ffn[0:1], norm_ffn[1:2]
    g_fin = norm_final.reshape(1, D_MODEL)

    y0 = _rms_fwd(xs, g_mix0)
    w_qkv_t = _gather_wait("ag_wait_qkv", ag_qkv, y0).reshape(QKV_DIM, D_MODEL)
    q, k, vv = _qkv_fwd(y0, w_qkv_t, attn_b_qkv, rc, rs)
    attn, probs, p_sinks = _attn_fwd(sinks, q, k, vv)
    w_o = _gather_wait("ag_wait_wo", ag_wo, attn).reshape(Q_DIM, D_MODEL)
    h1, f0 = _mm_res("attn_out_proj", attn, w_o, attn_b_o, xs, g_ffn0)
    w_ffn0 = _gather_wait("ag_wait_ffn0", ag_ffn0, h1)
    u0, p0, s0 = _ffn_up("ffn0_up", f0, w_ffn0)
    h2 = _ffn_down("ffn0_down", s0, w_ffn0, h1)
    wt = _unpack_conv(_gather_wait("ag_wait_conv", ag_conv, h2))
    wd = jnp.concatenate([wt["w_dw"][::-1], jnp.zeros((TAPS_PAD - CONV_WIDTH, D_MODEL), F32)], axis=0)
    y1, a, dwc, z = _conv_fwd(h2, g_mix1, wt["w_pw1_t"], wt["b_pw1"], wd, wt["b_dw"], wt["ln_g"], wt["ln_b"])
    h3, f1 = _mm_res("conv_out_proj", z, wt["w_pw2"], wt["b_pw2"], h2, g_ffn1)
    w_ffn1 = _gather_wait("ag_wait_ffn1", ag_ffn1, h3)
    u1, p1, s1 = _ffn_up("ffn1_up", f1, w_ffn1)
    dh4, dh4b, sq, dg_fin = _ffn_down_loss(s1, w_ffn1, h3, target, g_fin)

    du1, dp1 = _ffn_bwd_act("ffn1_bwd_act", dh4b, u1, p1, w_ffn1)
    dh3, dh3b, dg_ffn1 = _ffn_bwd_in("ffn1_bwd_in", du1, dp1, w_ffn1, h3, dh4, g_ffn1)
    blocks = _mm_tn("ffn1_dw2", s1, dh4b, tk=FF_TILE, into=(None, W2_SLOT))
    blocks = _mm_tn("ffn1_dw1", du1, f1, tk=FF_TILE, into=(blocks, W1T_SLOT))
    blocks = _mm_tn("ffn1_dw3", dp1, f1, tk=FF_TILE, into=(blocks, W3T_SLOT))
    rs_ffn1, tok = _scatter_start("rs_start_ffn1", blocks)

    da, dlg, dlb, dbdw, dwd, dbpw1, dbpw2 = _conv_bwd(dh3, wt["w_pw2"], dwc, a, wt["ln_g"] + tok[0, 0],
                                                     wt["ln_b"], wd)
    gpw2 = _mm_tn("conv_dw_pw2", z, dh3b, tk=D_MODEL)
    dh2, dh2b, dg_mix1 = _mm_rms_bwd("conv_in_bwd", da, wt["w_pw1_t"], h2, dh3, g_mix1, True)
    gpw1t = _mm_tn("conv_dw_pw1", da, y1, tk=D_MODEL)
    small_rows = _small_grad_rows(dbpw1, dwd[:CONV_WIDTH][::-1], dbdw, dlg, dlb, dbpw2)
    rs_conv, tok = _scatter_start("rs_start_conv", _dest_blocks([gpw1t, gpw2, small_rows]))

    du0, dp0 = _ffn_bwd_act("ffn0_bwd_act", dh2b, u0, p0, w_ffn0)
    dh1, dh1b, dg_ffn0 = _ffn_bwd_in("ffn0_bwd_in", du0, dp0, w_ffn0, h1, dh2, g_ffn0 + tok[0, 0])
    blocks = _mm_tn("ffn0_dw2", s0, dh2b, tk=FF_TILE, into=(None, W2_SLOT))
    blocks = _mm_tn("ffn0_dw1", du0, f0, tk=FF_TILE, into=(blocks, W1T_SLOT))
    blocks = _mm_tn("ffn0_dw3", dp0, f0, tk=FF_TILE, into=(blocks, W3T_SLOT))
    rs_ffn0, tok = _scatter_start("rs_start_ffn0", blocks)

    gwo = _mm_tn("attn_dw_o", attn, dh1b, tk=D_MODEL)
    rs_wo, tok2 = _scatter_start("rs_start_wo", _dest_blocks([gwo]))
    dattn, dbo = _nt_bias("attn_out_bwd", dh1, w_o)
    dq, dkc, dkp, dvc, dvp, dsink = _attn_bwd(probs, p_sinks + (tok[0, 0] + tok2[0, 0]), q, k, vv, dattn)
    dqkv, dbqkv = _rope_bwd(dq, dkc, dkp, dvc, dvp, rc, rs)
    gqkvt = _mm_tn("attn_dw_qkv", dqkv, y0, tk=QKV_DIM)
    rs_qkv, tok = _scatter_start("rs_start_qkv", _dest_blocks([gqkvt]))
    dx, dg_mix0 = _mm_rms_bwd("qkv_in_bwd", dqkv, w_qkv_t, xs, dh1, g_mix0 + tok[0, 0], False)

    p_ffn1, p_conv, p_ffn0, p_wo, p_qkv = _scatter_wait([rs_ffn1, rs_conv, rs_ffn0, rs_wo, rs_qkv], dx)

    tr = lambda a: jnp.swapaxes(a, -1, -2)
    same = lambda a: a
    ffn_slots = [p_ffn0, p_ffn1]
    plan = dict(
        ffn_w1=(ffn_slots, W1T_SLOT, tr), ffn_w3=(ffn_slots, W3T_SLOT, tr), ffn_w2=(ffn_slots, W2_SLOT, same),
        attn_w_qkv=([p_qkv], 0, tr), attn_w_o=([p_wo], 0, same),
        conv_w_pw2=([p_conv], ROWS_PW1 // ROWS_PW2, same))
    sharded = [{}, {}, {}, {}]
    for n, (slots, row_block, view) in plan.items():
        outs_n = _adamw_from_slots("adamw_" + n, slots, row_block, view(w[n]), view(m[n]), view(v[n]))
        for dst, t in zip(sharded, outs_n):
            dst[n] = view(t)
    small_out = _adamw_from_slots("adamw_small", [p_conv], (ROWS_PW1 + ROWS_PW2) // ROWS_SMALL,
                                  _pack_small(w), _pack_small(m), _pack_small(v))
    for dst, t in zip(sharded, small_out):
        dst.update(_unpack_small(t))
    g_pw1 = _sum_slots("rs_sum_pw1", p_conv[:, :ROWS_PW1]).T[None]
    pw1_out = _adamw_native("adamw_conv_w_pw1", g_pw1, w["conv_w_pw1"], m["conv_w_pw1"], v["conv_w_pw1"])
    for dst, t in zip(sharded, (g_pw1,) + tuple(pw1_out)):
        dst["conv_w_pw1"] = t

    part = _pack_replicated(jnp.concatenate([dg_mix0, dg_mix1]), jnp.concatenate([dg_ffn0, dg_ffn1]),
                            dbqkv, -dsink[:, 0], dbo, dg_fin, extra=sq[0, 0:1])
    parts = _all_gather("ag_replicated_grads", part)
    w_rep = _pack_replicated(norm_mix, norm_ffn, attn_b_qkv, attn_sinks, attn_b_o, norm_final)
    m_rep = _pack_replicated(m_norm_mix, m_norm_ffn, m_attn_b_qkv, m_attn_sinks, m_attn_b_o, m_norm_final)
    v_rep = _pack_replicated(v_norm_mix, v_norm_ffn, v_attn_b_qkv, v_attn_sinks, v_attn_b_o, v_norm_final)
    rep_out = _adamw_replicated(parts, w_rep, m_rep, v_rep)
    replicated = [_unpack_replicated(t) for t in rep_out]
    loss = rep_out[0][LOSS_ROW, 0] * (0.5 / D_MODEL)

    outs = [loss, dx.reshape(1, seq, D_MODEL)]
    for sh, rp in zip(sharded, replicated):
        merged = {**sh, **rp}
        outs += [merged[n] for n in WEIGHT_ORDER]
    return tuple(outs)
```

```python
import jax
import jax.numpy as jnp
from jax import lax
from jax.experimental import pallas as pl
from jax.experimental.pallas import tpu as pltpu

F32 = jnp.float32
BF16 = jnp.bfloat16

D_MODEL = 1024
HEAD_DIM = 64
N_Q_HEADS = 16
N_KV_HEADS = 2
Q_PER_KV = 8
Q_DIM = N_Q_HEADS * HEAD_DIM
KV_DIM = N_KV_HEADS * HEAD_DIM
QKV_DIM = Q_DIM + 2 * KV_DIM
BLOCK = 128
CONV_WIDTH = 31
D_FF = 2816
ROPE_THETA = 10000.0
RMS_EPS = 1e-5
LN_EPS = 1e-5
ADAM_LR = 0.001
ADAM_B1 = 0.9
ADAM_B2 = 0.999
ADAM_EPS = 1e-08
ADAM_WD = 0.01
ADAM_STEP = 10
N_DEV = 8

LANES = 128
SUBLANES = 8
TOKEN_TILE = 512
CONV_CHUNK = 64
CONV_HALO = 32
TAPS_PAD = 32
VMEM_LIMIT = 56 * 1024 * 1024
NEG_INF = float(jnp.finfo(jnp.float32).min)

ROWS_FF = D_FF // N_DEV
W1T_SLOT, W3T_SLOT, W2_SLOT = 0, 1, 2
ROWS_PW1 = 2 * D_MODEL // N_DEV
ROWS_PW2 = D_MODEL // N_DEV
ROWS_SMALL = 16
SMALL_USED = 2 * D_MODEL // N_DEV + CONV_WIDTH * (D_MODEL // N_DEV) + 4 * (D_MODEL // N_DEV)
FF_SPLIT = 2
FF_TILE_DEVS = N_DEV // FF_SPLIT
FF_TILE = FF_TILE_DEVS * ROWS_FF
REPL_ROWS = 16


def _call(body, *, name, grid, in_specs, out_specs, out_shape, scratch=(), sem=None, aliases=None):
    return pl.pallas_call(
        body, name=name, grid=grid, in_specs=in_specs, out_specs=out_specs, out_shape=out_shape,
        scratch_shapes=list(scratch), input_output_aliases=aliases or {},
        compiler_params=pltpu.CompilerParams(dimension_semantics=sem, vmem_limit_bytes=VMEM_LIMIT))


def _full(shape):
    return pl.BlockSpec(shape, lambda *_: (0,) * len(shape))


def _resident(shape):
    return pl.BlockSpec(shape, lambda *_: (0,) * len(shape), pipeline_mode=pl.Buffered(1))


def _rows(tm, n):
    return pl.BlockSpec((tm, n), lambda i, *_: (i, 0))


def _sig(x):
    return 1.0 / (1.0 + jnp.exp(-x))


def _sum_rows(x):
    return jnp.sum(x, axis=0, keepdims=True)


def _nt(a, b):
    return lax.dot_general(a, b, (((1,), (1,)), ((), ())), preferred_element_type=F32)


def _tn(a, b):
    return lax.dot_general(a, b, (((0,), (0,)), ((), ())), preferred_element_type=F32)


def _rms_stats(x):
    return lax.rsqrt(jnp.mean(x * x, axis=-1, keepdims=True) + RMS_EPS)


def _rms_bwd(dy, x, g, dres):
    r = _rms_stats(x)
    n = x * r
    dn = dy * g
    dx = dres + r * (dn - n * jnp.mean(dn * n, axis=-1, keepdims=True))
    return dx, _sum_rows(dy * n)


def _rope_tables(seq):
    half = HEAD_DIM // 2
    pos = jnp.arange(seq, dtype=F32)
    inv_freq = ROPE_THETA ** (-jnp.arange(0, HEAD_DIM, 2, dtype=F32) / HEAD_DIM)
    ang = pos[:, None] * inv_freq[None, :]
    spread = lambda t: jnp.broadcast_to(t[:, None, :], (seq, LANES // half, half)).reshape(seq, LANES)
    return spread(jnp.cos(ang)), spread(jnp.sin(ang))


def _signed_sin(sin):
    lane = lax.broadcasted_iota(jnp.int32, sin.shape, 1)
    first_half = (lane & (HEAD_DIM - 1)) < HEAD_DIM // 2
    return jnp.where(first_half, -sin, 0.0), jnp.where(first_half, 0.0, sin)


def _rope(t, c, sa, sb):
    half = HEAD_DIM // 2
    return t * c + pltpu.roll(t, LANES - half, 1) * sa + pltpu.roll(t, half, 1) * sb


def _rope_t(dt, c, sa, sb):
    half = HEAD_DIM // 2
    return dt * c + pltpu.roll(dt * sa, half, 1) + pltpu.roll(dt * sb, LANES - half, 1)


def _rms_fwd(x, g):
    seq = x.shape[0]
    tm = min(2 * TOKEN_TILE, seq)

    def body(x_ref, g_ref, y_ref):
        xv = x_ref[...]
        y_ref[...] = (xv * _rms_stats(xv) * g_ref[...]).astype(BF16)

    return _call(
        body, name="rms_fwd", grid=(seq // tm,), in_specs=[_rows(tm, D_MODEL), _full((1, D_MODEL))],
        out_specs=_rows(tm, D_MODEL), out_shape=jax.ShapeDtypeStruct((seq, D_MODEL), BF16),
        sem=("parallel",))(x, g)


def _qkv_fwd(y, w, b, rc, rs):
    seq = y.shape[0]
    tm = min(TOKEN_TILE, seq)

    def body(y_ref, w_ref, b_ref, c_ref, s_ref, q_ref, k_ref, v_ref):
        qkv = _nt(y_ref[...], w_ref[...]) + b_ref[...]
        c = c_ref[...]
        sa, sb = _signed_sin(s_ref[...])
        for i in range(Q_DIM // LANES):
            blk = _rope(qkv[:, i * LANES:(i + 1) * LANES], c, sa, sb)
            q_ref[:, i * LANES:(i + 1) * LANES] = (blk * (HEAD_DIM ** -0.5)).astype(BF16)
        k_ref[...] = _rope(qkv[:, Q_DIM:Q_DIM + KV_DIM], c, sa, sb).astype(BF16)
        v_ref[...] = qkv[:, Q_DIM + KV_DIM:].astype(BF16)

    return _call(
        body, name="qkv_fwd", grid=(seq // tm,),
        in_specs=[_rows(tm, D_MODEL), _resident((QKV_DIM, D_MODEL)), _full((1, QKV_DIM)),
                  _rows(tm, LANES), _rows(tm, LANES)],
        out_specs=[_rows(tm, Q_DIM), _rows(tm, KV_DIM), _rows(tm, KV_DIM)],
        out_shape=[jax.ShapeDtypeStruct((seq, Q_DIM), BF16),
                   jax.ShapeDtypeStruct((seq, KV_DIM), BF16), jax.ShapeDtypeStruct((seq, KV_DIM), BF16)],
        sem=("parallel",))(y, w, b, rc, rs)


def _band_mask(n):
    row = lax.broadcasted_iota(jnp.int32, (BLOCK, 2 * BLOCK), 0)
    col = lax.broadcasted_iota(jnp.int32, (BLOCK, 2 * BLOCK), 1)
    rel = row + BLOCK - col
    return (rel >= 0) & (rel < BLOCK) & ((col >= BLOCK) | (n > 0))


def _softmax_with_sink(s, mask, sink):
    s = jnp.where(mask, s, NEG_INF)
    m = jnp.maximum(jnp.max(s, axis=-1, keepdims=True), sink)
    p = jnp.exp(s - m)
    e_sink = jnp.exp(sink - m)
    inv = 1.0 / (jnp.sum(p, axis=-1, keepdims=True) + e_sink)
    return p * inv, e_sink * inv


PAIRS_PER_KV = Q_PER_KV // 2


def _kv_specs():
    cur = pl.BlockSpec((BLOCK, KV_DIM), lambda n: (n, 0))
    prev = pl.BlockSpec((BLOCK, KV_DIM), lambda n: (jnp.maximum(n - 1, 0), 0))
    return cur, prev


def _low_lanes():
    return lax.broadcasted_iota(jnp.int32, (2 * BLOCK, KV_DIM), 1) < HEAD_DIM


def _kv_low_high(prev, cur, j, low):
    both = jnp.concatenate([prev, cur], axis=0).astype(F32)
    swapped = pltpu.roll(both, HEAD_DIM, 1)
    at_low, at_high = (both, swapped) if j == 0 else (swapped, both)
    return jnp.where(low, at_low, 0.0).astype(BF16), jnp.where(low, 0.0, at_high).astype(BF16)


def _fold_pair_halves(acc, j, low):
    folded = acc + pltpu.roll(acc, HEAD_DIM, 1)
    return jnp.where(low, folded, 0.0) if j == 0 else jnp.where(low, 0.0, folded)


def _pair_lanes(j, i):
    g = j * PAIRS_PER_KV + i
    return slice(g * LANES, (g + 1) * LANES), 2 * g


def _attn_fwd(sinks, q, k, v):
    seq = q.shape[0]
    cur, prev = _kv_specs()

    def body(sink_ref, q_ref, kc_ref, kp_ref, vc_ref, vp_ref, o_ref, p_ref, ps_ref):
        mask = _band_mask(pl.program_id(0))
        lane = lax.broadcasted_iota(jnp.int32, (BLOCK, LANES), 1)
        p_sinks = jnp.zeros((BLOCK, LANES), F32)
        for j in range(N_KV_HEADS):
            cs = slice(j * HEAD_DIM, (j + 1) * HEAD_DIM)
            kk = jnp.concatenate([kp_ref[:, cs], kc_ref[:, cs]], axis=0)
            vv = jnp.concatenate([vp_ref[:, cs], vc_ref[:, cs]], axis=0)
            for gq in range(Q_PER_KV):
                h = j * Q_PER_KV + gq
                hs = slice(h * HEAD_DIM, (h + 1) * HEAD_DIM)
                probs, p_sink = _softmax_with_sink(_nt(q_ref[:, hs], kk), mask, sink_ref[h])
                pb = probs.astype(BF16)
                p_ref[h] = pb
                p_sinks = jnp.where(lane == h, p_sink, p_sinks)
                o_ref[:, hs] = jnp.dot(pb, vv, preferred_element_type=F32).astype(BF16)
        ps_ref[...] = p_sinks

    return _call(
        body, name="attn_fwd", grid=(seq // BLOCK,),
        in_specs=[pl.BlockSpec(memory_space=pltpu.SMEM), _rows(BLOCK, Q_DIM), cur, prev, cur, prev],
        out_specs=[_rows(BLOCK, Q_DIM), pl.BlockSpec((N_Q_HEADS, BLOCK, 2 * BLOCK), lambda n: (0, n, 0)),
                   _rows(BLOCK, LANES)],
        out_shape=[jax.ShapeDtypeStruct((seq, Q_DIM), BF16),
                   jax.ShapeDtypeStruct((N_Q_HEADS, seq, 2 * BLOCK), BF16),
                   jax.ShapeDtypeStruct((seq, LANES), F32)],
        sem=("parallel",))(sinks, q, k, k, v, v)


def _mm_res(name, a, w, b, res):
    seq, kdim = a.shape
    n = w.shape[1]
    tm = min(2 * TOKEN_TILE, seq)

    def body(a_ref, w_ref, b_ref, r_ref, o_ref):
        o_ref[...] = r_ref[...] + (jnp.dot(a_ref[...], w_ref[...], preferred_element_type=F32) + b_ref[...])

    return _call(
        body, name=name, grid=(seq // tm,),
        in_specs=[_rows(tm, kdim), _resident((kdim, n)), _full((1, n)), _rows(tm, n)],
        out_specs=_rows(tm, n), out_shape=jax.ShapeDtypeStruct((seq, n), F32),
        sem=("parallel",))(a, w, b, res)


def _ff_tile_spec(slot):
    return pl.BlockSpec((FF_TILE_DEVS, ROWS_FF, D_MODEL), lambda j, i: (j, slot, 0))


def _ff_whole_spec(slot):
    return pl.BlockSpec((N_DEV, ROWS_FF, D_MODEL), lambda i: (0, slot, 0), pipeline_mode=pl.Buffered(1))


def _ffn_up(name, h, g, gathered):
    seq = h.shape[0]
    tm = min(TOKEN_TILE, seq)

    def body(h_ref, g_ref, w1_ref, w3_ref, f_ref, u_ref, w_ref, s_ref):
        hv = h_ref[...]
        f = (hv * _rms_stats(hv) * g_ref[...]).astype(BF16)
        f_ref[...] = f
        for j in range(FF_SPLIT):
            devs = pl.ds(j * FF_TILE_DEVS, FF_TILE_DEVS)
            cols = pl.ds(j * FF_TILE, FF_TILE)
            u = _nt(f, w1_ref[devs].reshape(FF_TILE, D_MODEL))
            w = _nt(f, w3_ref[devs].reshape(FF_TILE, D_MODEL))
            u_ref[:, cols] = u.astype(BF16)
            w_ref[:, cols] = w.astype(BF16)
            s_ref[:, cols] = (u * _sig(u) * w).astype(BF16)

    ff_shape = jax.ShapeDtypeStruct((seq, D_FF), BF16)
    return _call(
        body, name=name, grid=(seq // tm,),
        in_specs=[_rows(tm, D_MODEL), _full((1, D_MODEL)), _ff_whole_spec(W1T_SLOT), _ff_whole_spec(W3T_SLOT)],
        out_specs=[_rows(tm, D_MODEL), _rows(tm, D_FF), _rows(tm, D_FF), _rows(tm, D_FF)],
        out_shape=[jax.ShapeDtypeStruct((seq, D_MODEL), BF16), ff_shape, ff_shape, ff_shape],
        sem=("parallel",))(h, g, gathered, gathered)


def _ffn_down(name, s, gathered, res):
    seq = s.shape[0]
    tm = min(2 * TOKEN_TILE, seq)

    def body(s_ref, w_ref, r_ref, o_ref):
        w2 = w_ref[...].reshape(D_FF, D_MODEL)
        o_ref[...] = r_ref[...] + jnp.dot(s_ref[...], w2, preferred_element_type=F32)

    return _call(
        body, name=name, grid=(seq // tm,),
        in_specs=[_rows(tm, D_FF), _ff_whole_spec(W2_SLOT), _rows(tm, D_MODEL)],
        out_specs=_rows(tm, D_MODEL), out_shape=jax.ShapeDtypeStruct((seq, D_MODEL), F32),
        sem=("parallel",))(s, gathered, res)


def _conv_fwd(h, g, wpw1, bpw1, wd, bdw, lng, lnb):
    seq = h.shape[0]
    tm = min(TOKEN_TILE, seq)
    n_chunks = tm // CONV_CHUNK
    win = CONV_CHUNK + CONV_HALO

    def body(h_ref, g_ref, w_ref, b_ref, wd_ref, bdw_ref, lng_ref, lnb_ref,
             y_ref, a_ref, dwc_ref, z_ref, gbuf):
        i = pl.program_id(0)

        @pl.when(i == 0)
        def _():
            gbuf[0:CONV_HALO, :] = jnp.zeros((CONV_HALO, D_MODEL), F32)

        @pl.when(i > 0)
        def _():
            gbuf[0:CONV_HALO, :] = gbuf[tm:tm + CONV_HALO, :]

        hv = h_ref[...]
        y = (hv * _rms_stats(hv) * g_ref[...]).astype(BF16)
        y_ref[...] = y
        a = _nt(y, w_ref[...]) + b_ref[...]
        a_ref[...] = a.astype(BF16)
        gbuf[CONV_HALO:CONV_HALO + tm, :] = a[:, :D_MODEL] * _sig(a[:, D_MODEL:])

        def chunk(c, carry):
            r0 = pl.multiple_of(c * CONV_CHUNK, CONV_CHUNK)
            for l in range(D_MODEL // LANES):
                ls = slice(l * LANES, (l + 1) * LANES)
                gw = gbuf[pl.ds(r0, win), ls]
                acc = jnp.zeros((CONV_CHUNK, LANES), F32) + bdw_ref[:, ls]
                for s in range(SUBLANES):
                    gs = gw if s == 0 else pltpu.roll(gw, s, 0)
                    for q in range(CONV_HALO // SUBLANES):
                        d = SUBLANES * q + s
                        if d < CONV_WIDTH:
                            lo = CONV_HALO - SUBLANES * q
                            acc = acc + wd_ref[d:d + 1, ls] * gs[lo:lo + CONV_CHUNK]
                dwc_ref[pl.ds(r0, CONV_CHUNK), ls] = acc
            return carry

        lax.fori_loop(0, n_chunks, chunk, 0)

        xv = dwc_ref[...]
        mu = jnp.mean(xv, axis=-1, keepdims=True)
        xc = xv - mu
        var = jnp.mean(xc * xc, axis=-1, keepdims=True)
        ln = xc * lax.rsqrt(var + LN_EPS) * lng_ref[...] + lnb_ref[...]
        z_ref[...] = (ln * _sig(ln)).astype(BF16)

    return _call(
        body, name="conv_fwd", grid=(seq // tm,),
        in_specs=[_rows(tm, D_MODEL), _full((1, D_MODEL)), _full((2 * D_MODEL, D_MODEL)), _full((1, 2 * D_MODEL)),
                  _full((TAPS_PAD, D_MODEL)), _full((1, D_MODEL)), _full((1, D_MODEL)), _full((1, D_MODEL))],
        out_specs=[_rows(tm, D_MODEL), _rows(tm, 2 * D_MODEL), _rows(tm, D_MODEL), _rows(tm, D_MODEL)],
        out_shape=[jax.ShapeDtypeStruct((seq, D_MODEL), BF16), jax.ShapeDtypeStruct((seq, 2 * D_MODEL), BF16),
                   jax.ShapeDtypeStruct((seq, D_MODEL), F32), jax.ShapeDtypeStruct((seq, D_MODEL), BF16)],
        scratch=[pltpu.VMEM((tm + CONV_HALO, D_MODEL), F32)],
        sem=("arbitrary",))(h, g, wpw1, bpw1, wd, bdw, lng, lnb)


def _ffn_down_loss(s, gathered, res, target, g):
    seq = s.shape[0]
    tm = min(TOKEN_TILE, seq)

    def body(s_ref, w_ref, r_ref, t_ref, g_ref, dh_ref, dhb_ref, loss_ref, dg_ref):
        @pl.when(pl.program_id(0) == 0)
        def _():
            loss_ref[...] = jnp.zeros_like(loss_ref)
            dg_ref[...] = jnp.zeros_like(dg_ref)

        hv = r_ref[...] + jnp.dot(s_ref[...], w_ref[...].reshape(D_FF, D_MODEL), preferred_element_type=F32)
        gv = g_ref[...]
        err = hv * _rms_stats(hv) * gv - t_ref[...]
        sq = jnp.sum(jnp.sum(err * err, axis=-1, keepdims=True), axis=0, keepdims=True)
        loss_ref[...] += jnp.broadcast_to(sq, loss_ref.shape)
        dx, dg = _rms_bwd(err * (1.0 / D_MODEL), hv, gv, 0.0)
        dh_ref[...] = dx
        dhb_ref[...] = dx.astype(BF16)
        dg_ref[...] += dg

    return _call(
        body, name="ffn1_down_loss", grid=(seq // tm,),
        in_specs=[_rows(tm, D_FF), _ff_whole_spec(W2_SLOT), _rows(tm, D_MODEL), _rows(tm, D_MODEL),
                  _full((1, D_MODEL))],
        out_specs=[_rows(tm, D_MODEL), _rows(tm, D_MODEL), _full((SUBLANES, LANES)), _full((1, D_MODEL))],
        out_shape=[jax.ShapeDtypeStruct((seq, D_MODEL), F32), jax.ShapeDtypeStruct((seq, D_MODEL), BF16),
                   jax.ShapeDtypeStruct((SUBLANES, LANES), F32), jax.ShapeDtypeStruct((1, D_MODEL), F32)],
        sem=("arbitrary",))(s, gathered, res, target, g)


def _ffn_bwd_act(name, dh, u, w, gathered):
    seq = dh.shape[0]
    tm = min(TOKEN_TILE, seq)

    def body(dh_ref, u_ref, w_ref, w2_ref, du_ref, dw_ref):
        ds = _nt(dh_ref[...], w2_ref[...].reshape(FF_TILE, D_MODEL))
        uv = u_ref[...].astype(F32)
        sg = _sig(uv)
        dw_ref[...] = (ds * (uv * sg)).astype(BF16)
        du_ref[...] = (ds * w_ref[...].astype(F32) * (sg * (1.0 + uv * (1.0 - sg)))).astype(BF16)

    tile_ff = pl.BlockSpec((tm, FF_TILE), lambda j, i: (i, j))
    ff_shape = jax.ShapeDtypeStruct((seq, D_FF), BF16)
    return _call(
        body, name=name, grid=(FF_SPLIT, seq // tm),
        in_specs=[pl.BlockSpec((tm, D_MODEL), lambda j, i: (i, 0)), tile_ff, tile_ff, _ff_tile_spec(W2_SLOT)],
        out_specs=[tile_ff, tile_ff], out_shape=[ff_shape, ff_shape],
        sem=("parallel", "parallel"))(dh, u, w, gathered)


def _ffn_bwd_in(name, du, dw, gathered, h_in, dres, g):
    seq = du.shape[0]
    tm = min(TOKEN_TILE, seq)

    def body(du_ref, dw_ref, w1_ref, w3_ref, h_ref, dr_ref, g_ref, dx_ref, dxb_ref, dg_ref):
        @pl.when(pl.program_id(0) == 0)
        def _():
            dg_ref[...] = jnp.zeros_like(dg_ref)

        df = jnp.dot(du_ref[...], w1_ref[...].reshape(D_FF, D_MODEL), preferred_element_type=F32)
        df = df + jnp.dot(dw_ref[...], w3_ref[...].reshape(D_FF, D_MODEL), preferred_element_type=F32)
        dx, dg = _rms_bwd(df, h_ref[...], g_ref[...], dr_ref[...])
        dx_ref[...] = dx
        dxb_ref[...] = dx.astype(BF16)
        dg_ref[...] += dg

    return _call(
        body, name=name, grid=(seq // tm,),
        in_specs=[_rows(tm, D_FF), _rows(tm, D_FF), _ff_whole_spec(W1T_SLOT), _ff_whole_spec(W3T_SLOT),
                  _rows(tm, D_MODEL), _rows(tm, D_MODEL), _full((1, D_MODEL))],
        out_specs=[_rows(tm, D_MODEL), _rows(tm, D_MODEL), _full((1, D_MODEL))],
        out_shape=[jax.ShapeDtypeStruct((seq, D_MODEL), F32), jax.ShapeDtypeStruct((seq, D_MODEL), BF16),
                   jax.ShapeDtypeStruct((1, D_MODEL), F32)],
        sem=("arbitrary",))(du, dw, gathered, gathered, h_in, dres, g)


def _mm_tn(name, a, b, *, tk, into=None):
    seq, kdim = a.shape
    n = b.shape[1]
    tt = min((4 if b.dtype == BF16 else 2) * TOKEN_TILE, seq)
    n_t = seq // tt
    devs = tk // ROWS_FF

    def body(a_ref, b_ref, *rest):
        o_ref, acc = rest[-2:]
        t = pl.program_id(1)

        @pl.when(t == 0)
        def _():
            acc[...] = jnp.zeros_like(acc)

        acc[...] += _tn(a_ref[...].astype(BF16), b_ref[...].astype(BF16))

        @pl.when(t == n_t - 1)
        def _():
            out = acc[...].astype(BF16)
            o_ref[...] = out if into is None else out.reshape(devs, ROWS_FF, n)

    in_specs = [pl.BlockSpec((tt, tk), lambda k, t: (t, k)), pl.BlockSpec((tt, n), lambda k, t: (t, 0))]
    args = [a, b]
    aliases = None
    if into is None:
        out_spec = pl.BlockSpec((tk, n), lambda k, t: (k, 0))
        out_shape = jax.ShapeDtypeStruct((kdim, n), BF16)
    else:
        blocks, slot = into
        out_spec = pl.BlockSpec((devs, ROWS_FF, n), lambda k, t: (k, slot, 0))
        out_shape = jax.ShapeDtypeStruct((N_DEV, 3 * ROWS_FF, n), BF16)
        if blocks is not None:
            in_specs.append(_ANY)
            args.append(blocks)
            aliases = {2: 0}
    return _call(
        body, name=name, grid=(kdim // tk, n_t), in_specs=in_specs, out_specs=out_spec, out_shape=out_shape,
        scratch=[pltpu.VMEM((tk, n), F32)], sem=("parallel", "arbitrary"), aliases=aliases)(*args)


def _conv_bwd(dh, wpw2, dwc, a, lng, lnb, wd):
    seq = dh.shape[0]
    tm = min(TOKEN_TILE, seq)
    nt = seq // tm
    n_chunks = tm // CONV_CHUNK
    win = CONV_CHUNK + CONV_HALO
    halo_per_tile = tm // CONV_HALO

    def body(dh_ref, w_ref, dwc_ref, a_ref, ah_ref, lng_ref, lnb_ref, wd_ref,
             da_ref, dlg_ref, dlb_ref, dbdw_ref, dwd_ref, dbpw1_ref, dbpw2_ref,
             gbuf, dbuf, dglu, dwd_part):
        i = pl.program_id(0)
        r = nt - 1 - i

        @pl.when(i == 0)
        def _():
            dlg_ref[...] = jnp.zeros_like(dlg_ref)
            dlb_ref[...] = jnp.zeros_like(dlb_ref)
            dbdw_ref[...] = jnp.zeros_like(dbdw_ref)
            dbpw1_ref[...] = jnp.zeros_like(dbpw1_ref)
            dbpw2_ref[...] = jnp.zeros_like(dbpw2_ref)
            dwd_part[...] = jnp.zeros_like(dwd_part)
            dbuf[tm:tm + CONV_HALO, :] = jnp.zeros((CONV_HALO, D_MODEL), F32)

        @pl.when(i > 0)
        def _():
            dbuf[tm:tm + CONV_HALO, :] = dbuf[0:CONV_HALO, :]

        dhv = dh_ref[...]
        dbpw2_ref[...] += _sum_rows(dhv)
        dz = _nt(dhv.astype(BF16), w_ref[...])
        xv = dwc_ref[...]
        lg = lng_ref[...]
        mu = jnp.mean(xv, axis=-1, keepdims=True)
        xc = xv - mu
        rstd = lax.rsqrt(jnp.mean(xc * xc, axis=-1, keepdims=True) + LN_EPS)
        xhat = xc * rstd
        ln = xhat * lg + lnb_ref[...]
        sg = _sig(ln)
        dln = dz * (sg * (1.0 + ln * (1.0 - sg)))
        dlg_ref[...] += _sum_rows(dln * xhat)
        dlb_ref[...] += _sum_rows(dln)
        dxh = dln * lg
        ddw = rstd * (dxh - jnp.mean(dxh, axis=-1, keepdims=True)
                      - xhat * jnp.mean(dxh * xhat, axis=-1, keepdims=True))
        dbdw_ref[...] += _sum_rows(ddw)
        dbuf[0:tm, :] = ddw

        av = a_ref[...].astype(F32)
        a1 = av[:, :D_MODEL]
        s2 = _sig(av[:, D_MODEL:])
        gbuf[CONV_HALO:CONV_HALO + tm, :] = a1 * s2
        ah = ah_ref[...].astype(F32)
        gh = ah[:, :D_MODEL] * _sig(ah[:, D_MODEL:])
        gbuf[0:CONV_HALO, :] = jnp.where(r > 0, gh, 0.0)

        def chunk(c, carry):
            r0 = pl.multiple_of(c * CONV_CHUNK, CONV_CHUNK)
            for l in range(D_MODEL // LANES):
                ls = slice(l * LANES, (l + 1) * LANES)
                dw_ = dbuf[pl.ds(r0, win), ls]
                gw = gbuf[pl.ds(r0, win), ls]
                acc = jnp.zeros((CONV_CHUNK, LANES), F32)
                for s in range(SUBLANES):
                    ds_ = dw_ if s == 0 else pltpu.roll(dw_, win - s, 0)
                    for q in range(CONV_HALO // SUBLANES):
                        d = SUBLANES * q + s
                        if d < CONV_WIDTH:
                            acc = acc + wd_ref[d:d + 1, ls] * ds_[SUBLANES * q:SUBLANES * q + CONV_CHUNK]
                            lo = CONV_HALO - SUBLANES * q
                            prod = ds_[0:CONV_CHUNK] * gw[lo:lo + CONV_CHUNK]
                            dwd_part[d, :, ls] += jnp.sum(
                                prod.reshape(CONV_CHUNK // SUBLANES, SUBLANES, LANES), axis=0)
                dglu[pl.ds(r0, CONV_CHUNK), ls] = acc
            return carry

        lax.fori_loop(0, n_chunks, chunk, 0)

        dg_ = dglu[...]
        da1 = dg_ * s2
        da2 = dg_ * a1 * s2 * (1.0 - s2)
        da_ref[:, :D_MODEL] = da1.astype(BF16)
        da_ref[:, D_MODEL:] = da2.astype(BF16)
        dbpw1_ref[:, :D_MODEL] += _sum_rows(da1)
        dbpw1_ref[:, D_MODEL:] += _sum_rows(da2)

        @pl.when(i == nt - 1)
        def _():
            dwd_ref[...] = jnp.sum(dwd_part[...], axis=1)

    rev = lambda n: pl.BlockSpec((tm, n), lambda i: (nt - 1 - i, 0))
    halo = pl.BlockSpec((CONV_HALO, 2 * D_MODEL),
                        lambda i: (jnp.maximum((nt - 1 - i) * halo_per_tile - 1, 0), 0))
    vec = lambda n: _full((1, n))
    return _call(
        body, name="conv_bwd", grid=(nt,),
        in_specs=[rev(D_MODEL), _full((D_MODEL, D_MODEL)), rev(D_MODEL), rev(2 * D_MODEL), halo,
                  vec(D_MODEL), vec(D_MODEL), _full((TAPS_PAD, D_MODEL))],
        out_specs=[rev(2 * D_MODEL), vec(D_MODEL), vec(D_MODEL), vec(D_MODEL), _full((TAPS_PAD, D_MODEL)),
                   vec(2 * D_MODEL), vec(D_MODEL)],
        out_shape=[jax.ShapeDtypeStruct((seq, 2 * D_MODEL), BF16), jax.ShapeDtypeStruct((1, D_MODEL), F32),
                   jax.ShapeDtypeStruct((1, D_MODEL), F32), jax.ShapeDtypeStruct((1, D_MODEL), F32),
                   jax.ShapeDtypeStruct((TAPS_PAD, D_MODEL), F32), jax.ShapeDtypeStruct((1, 2 * D_MODEL), F32),
                   jax.ShapeDtypeStruct((1, D_MODEL), F32)],
        scratch=[pltpu.VMEM((tm + CONV_HALO, D_MODEL), F32), pltpu.VMEM((tm + CONV_HALO, D_MODEL), F32),
                 pltpu.VMEM((tm, D_MODEL), F32), pltpu.VMEM((TAPS_PAD, SUBLANES, D_MODEL), F32)],
        sem=("arbitrary",))(dh, wpw2, dwc, a, a, lng, lnb, wd)


def _mm_rms_bwd(name, dact, wt, h_in, dres, g, bf16_copy):
    seq, n = dact.shape
    tm = min(TOKEN_TILE, seq)

    def body(da_ref, w_ref, h_ref, dr_ref, g_ref, dx_ref, *rest):
        dg_ref = rest[-1]

        @pl.when(pl.program_id(0) == 0)
        def _():
            dg_ref[...] = jnp.zeros_like(dg_ref)

        dy = jnp.dot(da_ref[...], w_ref[...], preferred_element_type=F32)
        dx, dg = _rms_bwd(dy, h_ref[...], g_ref[...], dr_ref[...])
        dx_ref[...] = dx
        if bf16_copy:
            rest[0][...] = dx.astype(BF16)
        dg_ref[...] += dg

    copy_spec = [_rows(tm, D_MODEL)] if bf16_copy else []
    copy_shape = [jax.ShapeDtypeStruct((seq, D_MODEL), BF16)] if bf16_copy else []
    return _call(
        body, name=name, grid=(seq // tm,),
        in_specs=[_rows(tm, n), _resident((n, D_MODEL)), _rows(tm, D_MODEL), _rows(tm, D_MODEL), _full((1, D_MODEL))],
        out_specs=[_rows(tm, D_MODEL), *copy_spec, _full((1, D_MODEL))],
        out_shape=[jax.ShapeDtypeStruct((seq, D_MODEL), F32), *copy_shape, jax.ShapeDtypeStruct((1, D_MODEL), F32)],
        sem=("arbitrary",))(dact, wt, h_in, dres, g)


def _nt_bias(name, dy, w):
    seq, n = dy.shape
    kdim = w.shape[0]
    tm = min(2 * TOKEN_TILE, seq)

    def body(dy_ref, w_ref, o_ref, db_ref):
        @pl.when(pl.program_id(0) == 0)
        def _():
            db_ref[...] = jnp.zeros_like(db_ref)

        dyv = dy_ref[...]
        db_ref[...] += _sum_rows(dyv)
        o_ref[...] = _nt(dyv.astype(BF16), w_ref[...]).astype(BF16)

    return _call(
        body, name=name, grid=(seq // tm,),
        in_specs=[_rows(tm, n), _resident((kdim, n))],
        out_specs=[_rows(tm, kdim), _full((1, n))],
        out_shape=[jax.ShapeDtypeStruct((seq, kdim), BF16), jax.ShapeDtypeStruct((1, n), F32)],
        sem=("arbitrary",))(dy, w)


def _attn_bwd(probs, p_sinks, q, k, v, do):
    seq = q.shape[0]
    per = 2 if seq % (2 * BLOCK) == 0 else 1
    tile = per * BLOCK
    cur = pl.BlockSpec((tile, KV_DIM), lambda n: (n, 0))
    prev = pl.BlockSpec((BLOCK, KV_DIM), lambda n: (jnp.maximum(n * per - 1, 0), 0))

    def body(p_ref, ps_ref, q_ref, kc_ref, kp_ref, vc_ref, vp_ref, do_ref,
             dq_ref, dkc_ref, dkp_ref, dvc_ref, dvp_ref, dsink_ref):
        @pl.when(pl.program_id(0) == 0)
        def _():
            dsink_ref[...] = jnp.zeros_like(dsink_ref)

        low = _low_lanes()
        lane = lax.broadcasted_iota(jnp.int32, (BLOCK, LANES), 1)
        for b in range(per):
            rows = slice(b * BLOCK, (b + 1) * BLOCK)
            before = slice((b - 1) * BLOCK, b * BLOCK)
            k_prev, v_prev = (kp_ref[...], vp_ref[...]) if b == 0 else (kc_ref[before, :], vc_ref[before, :])
            k_cur, v_cur = kc_ref[rows, :], vc_ref[rows, :]
            p_sinks_blk = ps_ref[rows, :]
            dk_all = jnp.zeros((2 * BLOCK, KV_DIM), F32)
            dv_all = jnp.zeros((2 * BLOCK, KV_DIM), F32)
            for j in range(N_KV_HEADS):
                k_lo, k_hi = _kv_low_high(k_prev, k_cur, j, low)
                v_lo, v_hi = _kv_low_high(v_prev, v_cur, j, low)
                dk_acc = jnp.zeros((2 * BLOCK, KV_DIM), F32)
                dv_acc = jnp.zeros((2 * BLOCK, KV_DIM), F32)
                for i in range(PAIRS_PER_KV):
                    ls, h = _pair_lanes(j, i)
                    qp = q_ref[rows, ls]
                    dop = do_ref[rows, ls]
                    dsb, pb16 = [], []
                    for t, v_sel in enumerate((v_lo, v_hi)):
                        pb = p_ref[h + t, rows, :]
                        pf = pb.astype(F32)
                        p_sink = jnp.sum(jnp.where(lane == h + t, p_sinks_blk, 0.0), axis=-1, keepdims=True)
                        dp = _nt(dop, v_sel)
                        delta = jnp.sum(pf * dp, axis=-1, keepdims=True)
                        dsb.append((pf * (dp - delta)).astype(BF16))
                        pb16.append(pb)
                        dsink_ref[h + t:h + t + 1, :] += jnp.broadcast_to(_sum_rows(p_sink * delta), (1, LANES))
                    dq = (jnp.dot(dsb[0], k_lo, preferred_element_type=F32)
                          + jnp.dot(dsb[1], k_hi, preferred_element_type=F32))
                    dq_ref[rows, ls] = dq * (HEAD_DIM ** -0.5)
                    dk_acc = dk_acc + jnp.where(low, _tn(dsb[0], qp), _tn(dsb[1], qp))
                    dv_acc = dv_acc + jnp.where(low, _tn(pb16[0], dop), _tn(pb16[1], dop))
                dk_all = dk_all + _fold_pair_halves(dk_acc, j, low)
                dv_all = dv_all + _fold_pair_halves(dv_acc, j, low)
            dkp_ref[rows, :] = dk_all[:BLOCK]
            dkc_ref[rows, :] = dk_all[BLOCK:]
            dvp_ref[rows, :] = dv_all[:BLOCK]
            dvc_ref[rows, :] = dv_all[BLOCK:]

    kv_out = _rows(tile, KV_DIM)
    kv_shape = jax.ShapeDtypeStruct((seq, KV_DIM), F32)
    return _call(
        body, name="attn_bwd", grid=(seq // tile,),
        in_specs=[pl.BlockSpec((N_Q_HEADS, tile, 2 * BLOCK), lambda n: (0, n, 0)), _rows(tile, LANES),
                  _rows(tile, Q_DIM), cur, prev, cur, prev, _rows(tile, Q_DIM)],
        out_specs=[_rows(tile, Q_DIM), kv_out, kv_out, kv_out, kv_out, _full((N_Q_HEADS, LANES))],
        out_shape=[jax.ShapeDtypeStruct((seq, Q_DIM), F32), kv_shape, kv_shape, kv_shape, kv_shape,
                   jax.ShapeDtypeStruct((N_Q_HEADS, LANES), F32)],
        sem=("arbitrary",))(probs, p_sinks, q, k, k, v, v, do)


def _rope_bwd(dq, dkc, dkp, dvc, dvp, rc, rs):
    seq = dq.shape[0]
    nb = seq // BLOCK
    tm = min(TOKEN_TILE, seq)
    nt = seq // tm
    per = tm // BLOCK
    nxt = pl.BlockSpec((BLOCK, KV_DIM), lambda i: (jnp.minimum((i + 1) * per, nb - 1), 0))

    def body(dq_ref, dkc_ref, dkp_ref, dkn_ref, dvc_ref, dvp_ref, dvn_ref, c_ref, s_ref, o_ref, db_ref):
        i = pl.program_id(0)

        @pl.when(i == 0)
        def _():
            db_ref[...] = jnp.zeros_like(db_ref)

        c = c_ref[...]
        sa, sb = _signed_sin(s_ref[...])
        last = i == nt - 1

        def from_next_block(prev_ref, next_ref):
            tail = jnp.where(last, 0.0, next_ref[...])
            return tail if per == 1 else jnp.concatenate([prev_ref[BLOCK:, :], tail], axis=0)

        dk = dkc_ref[...] + from_next_block(dkp_ref, dkn_ref)
        dv = dvc_ref[...] + from_next_block(dvp_ref, dvn_ref)
        for l in range(Q_DIM // LANES):
            ls = slice(l * LANES, (l + 1) * LANES)
            blk = _rope_t(dq_ref[:, ls], c, sa, sb)
            o_ref[:, ls] = blk.astype(BF16)
            db_ref[:, ls] += _sum_rows(blk)
        dkr = _rope_t(dk, c, sa, sb)
        o_ref[:, Q_DIM:Q_DIM + KV_DIM] = dkr.astype(BF16)
        db_ref[:, Q_DIM:Q_DIM + KV_DIM] += _sum_rows(dkr)
        o_ref[:, Q_DIM + KV_DIM:] = dv.astype(BF16)
        db_ref[:, Q_DIM + KV_DIM:] += _sum_rows(dv)

    kv = _rows(tm, KV_DIM)
    tab = _rows(tm, LANES)
    return _call(
        body, name="rope_bwd", grid=(nt,),
        in_specs=[_rows(tm, Q_DIM), kv, kv, nxt, kv, kv, nxt, tab, tab],
        out_specs=[_rows(tm, QKV_DIM), _full((1, QKV_DIM))],
        out_shape=[jax.ShapeDtypeStruct((seq, QKV_DIM), BF16), jax.ShapeDtypeStruct((1, QKV_DIM), F32)],
        sem=("arbitrary",))(dq, dkc, dkp, dkp, dvc, dvp, dvp, rc, rs)


def _adamw(w, g, m, v):
    m = ADAM_B1 * m + (1.0 - ADAM_B1) * g
    v = ADAM_B2 * v + (1.0 - ADAM_B2) * (g * g)
    m_hat = m / (1.0 - ADAM_B1 ** ADAM_STEP)
    v_hat = v / (1.0 - ADAM_B2 ** ADAM_STEP)
    delta = -ADAM_LR * (m_hat / (jnp.sqrt(v_hat) + ADAM_EPS) + ADAM_WD * w)
    return delta, m, v


def _sum_slots(name, parts):
    _, rows, cols = parts.shape
    tr = rows if rows <= 512 else ROWS_FF

    def body(p_ref, g_ref):
        g = p_ref[0].astype(F32)
        for d in range(1, N_DEV):
            g = g + p_ref[d].astype(F32)
        g_ref[...] = g

    return _call(
        body, name=name, grid=(rows // tr,),
        in_specs=[pl.BlockSpec((N_DEV, tr, cols), lambda i: (0, i, 0))],
        out_specs=_rows(tr, cols), out_shape=jax.ShapeDtypeStruct((rows, cols), F32),
        sem=("parallel",))(parts)


def _adamw_native(name, g, w, m, v):
    layers, rows, cols = w.shape
    tr = rows if rows <= 512 else 256

    def body(g_ref, w_ref, m_ref, v_ref, d_ref, nm_ref, nv_ref):
        d_ref[...], nm_ref[...], nv_ref[...] = _adamw(w_ref[...], g_ref[...], m_ref[...], v_ref[...])

    spec = pl.BlockSpec((1, tr, cols), lambda l, i: (l, i, 0))
    shape = jax.ShapeDtypeStruct(w.shape, F32)
    return _call(
        body, name=name, grid=(layers, rows // tr), in_specs=[spec, spec, spec, spec],
        out_specs=[spec, spec, spec], out_shape=[shape, shape, shape],
        sem=("parallel", "parallel"))(g, w, m, v)


def _adamw_from_slots(name, slots, row_block, w, m, v):
    layers, rows, cols = w.shape

    def body(*refs):
        slot_refs = refs[:layers]
        w_ref, m_ref, v_ref, g_ref, d_ref, nm_ref, nv_ref = refs[layers:]
        for l in range(layers):
            @pl.when(pl.program_id(0) == l)
            def _(p_ref=slot_refs[l]):
                g = p_ref[0].astype(F32)
                for d in range(1, N_DEV):
                    g = g + p_ref[d].astype(F32)
                g_ref[0] = g
        d, nm, nv = _adamw(w_ref[0], g_ref[0], m_ref[0], v_ref[0])
        d_ref[0], nm_ref[0], nv_ref[0] = d, nm, nv

    slot_spec = pl.BlockSpec((N_DEV, rows, cols), lambda l: (0, row_block, 0))
    spec = pl.BlockSpec((1, rows, cols), lambda l: (l, 0, 0))
    shape = jax.ShapeDtypeStruct(w.shape, F32)
    return _call(
        body, name=name, grid=(layers,), in_specs=[slot_spec] * layers + [spec, spec, spec],
        out_specs=[spec, spec, spec, spec], out_shape=[shape, shape, shape, shape],
        sem=("arbitrary",))(*slots, w, m, v)


def _adamw_replicated(parts, w, m, v):
    def body(p_ref, w_ref, m_ref, v_ref, g_ref, d_ref, nm_ref, nv_ref):
        g = p_ref[0]
        for j in range(1, N_DEV):
            g = g + p_ref[j]
        g_ref[...] = g
        d_ref[...], nm_ref[...], nv_ref[...] = _adamw(w_ref[...], g, m_ref[...], v_ref[...])

    spec = _full((REPL_ROWS, D_MODEL))
    shape = jax.ShapeDtypeStruct((REPL_ROWS, D_MODEL), F32)
    return _call(
        body, name="adamw_replicated", grid=(1,),
        in_specs=[_full((N_DEV, REPL_ROWS, D_MODEL)), spec, spec, spec],
        out_specs=[spec, spec, spec, spec], out_shape=[shape, shape, shape, shape],
        sem=("arbitrary",))(parts, w, m, v)


_MESH = pl.DeviceIdType.MESH
_ANY = pl.BlockSpec(memory_space=pl.ANY)


def _all_gather(name, xs):
    rows, cols = xs.shape

    def body(x_ref, out_ref, send_sems, recv_sems, local_sem):
        x, y, c = lax.axis_index("x"), lax.axis_index("y"), lax.axis_index("c")
        me, sibling = (x, y, c), (x, y, 1 - c)
        chips = [(1 - x, y), (x, 1 - y), (1 - x, 1 - y)]

        def slot(px, py, pc):
            return out_ref.at[4 * px + 2 * py + pc]

        def copy(k, block, to, src=None):
            return pltpu.make_async_remote_copy(
                src_ref=slot(*block) if src is None else src, dst_ref=slot(*block),
                send_sem=send_sems.at[k], recv_sem=recv_sems.at[k], device_id=to, device_id_type=_MESH)

        mine = pltpu.make_async_copy(x_ref, slot(*me), local_sem)
        mine.start()
        first = [copy(0, me, sibling, src=x_ref)]
        first += [copy(1 + j, me, (*chip, c), src=x_ref) for j, chip in enumerate(chips)]
        for cp in first:
            cp.start()
        passed = [copy(4 + j, (*chip, c), sibling) for j, chip in enumerate(chips)]
        for j, chip in enumerate(chips):
            copy(1 + j, (*chip, c), me).wait_recv()
            passed[j].start()
        copy(0, sibling, me).wait_recv()
        for j, chip in enumerate(chips):
            copy(4 + j, (*chip, 1 - c), me).wait_recv()
        for cp in first + passed:
            cp.wait_send()
        mine.wait()

    return pl.pallas_call(
        body, name=name, out_shape=jax.ShapeDtypeStruct((N_DEV, rows, cols), xs.dtype),
        in_specs=[_ANY], out_specs=_ANY,
        scratch_shapes=[pltpu.SemaphoreType.DMA((7,)), pltpu.SemaphoreType.DMA((7,)), pltpu.SemaphoreType.DMA],
    )(xs)


N_PEERS = N_DEV - 1
_HBM = pl.BlockSpec(memory_space=pltpu.HBM)
_SEM = pl.BlockSpec(memory_space=pltpu.SEMAPHORE)
_DATAFLOW = pltpu.SideEffectType.DATAFLOW_SIDE_EFFECTING
_TOKEN = jax.ShapeDtypeStruct((SUBLANES, LANES), F32)


def _peers():
    x, y, c = lax.axis_index("x"), lax.axis_index("y"), lax.axis_index("c")
    out = []
    for k in range(1, N_DEV):
        px = 1 - x if k & 4 else x
        py = 1 - y if k & 2 else y
        pc = 1 - c if k & 1 else c
        out.append(((px, py, pc), 4 * px + 2 * py + pc))
    return 4 * x + 2 * y + c, out


def _in_hbm(a):
    return pltpu.with_memory_space_constraint(a, pltpu.HBM)


def _landing(rows):
    return _in_hbm(lax.empty((N_DEV, rows, D_MODEL), BF16))


SEMS_PER_GROUP = 3


def _group_sems():
    return pltpu.SemaphoreType.DMA((N_PEERS,)), pltpu.SemaphoreType.DMA((N_PEERS,)), pltpu.SemaphoreType.DMA(())


def _gather_start(name, payloads):
    n = len(payloads)
    ns = SEMS_PER_GROUP

    def body(*refs):
        src, land = refs[:n], refs[n:2 * n]
        sems = refs[2 * n:(2 + ns) * n]
        token = refs[-1]
        me, peers = _peers()
        for g in range(n):
            send_sems, recv_sems, own_sem = sems[ns * g:ns * (g + 1)]
            for k, (pos, _) in enumerate(peers):
                pltpu.make_async_remote_copy(
                    src_ref=src[g], dst_ref=land[g].at[me], send_sem=send_sems.at[k],
                    recv_sem=recv_sems.at[k], device_id=pos, device_id_type=_MESH).start()
            pltpu.make_async_copy(src[g], land[g].at[me], own_sem).start()
        token[...] = jnp.zeros_like(token)

    lands = [_landing(p.shape[0]) for p in payloads]
    sem_shapes = [s for _ in payloads for s in _group_sems()]
    hbm_shapes = [pltpu.HBM(a.shape, a.dtype) for a in list(payloads) + lands]
    out = pl.pallas_call(
        body, name=name, out_shape=(*sem_shapes, *hbm_shapes, _TOKEN),
        in_specs=[_HBM] * (2 * n),
        out_specs=(*[_SEM] * (ns * n), *[_HBM] * (2 * n), pl.BlockSpec(memory_space=pltpu.VMEM)),
        input_output_aliases={i: ns * n + i for i in range(2 * n)},
        compiler_params=pltpu.CompilerParams(has_side_effects=_DATAFLOW),
    )(*[_in_hbm(p) for p in payloads], *lands)
    sems, thru = out[:ns * n], out[ns * n:(ns + 2) * n]
    return [(thru[g], thru[n + g], *sems[ns * g:ns * (g + 1)]) for g in range(n)], out[-1]


def _gather_wait(name, group, after):
    payload, land, send_sems, recv_sems, own_sem = group

    def body(src_ref, land_ref, send_ref, recv_ref, own_ref, after_ref, src_out, land_out):
        me, peers = _peers()
        for k, (pos, idx) in enumerate(peers):
            cp = pltpu.make_async_remote_copy(
                src_ref=src_ref, dst_ref=land_ref.at[idx], send_sem=send_ref.at[k], recv_sem=recv_ref.at[k],
                device_id=pos, device_id_type=_MESH)
            cp.wait_send()
            cp.wait_recv()
        pltpu.make_async_copy(src_ref, land_ref.at[me], own_ref).wait()

    _, land = pl.pallas_call(
        body, name=name, out_shape=(pltpu.HBM(payload.shape, payload.dtype), pltpu.HBM(land.shape, land.dtype)),
        in_specs=[_HBM, _HBM, _SEM, _SEM, _SEM, _ANY], out_specs=(_HBM, _HBM), input_output_aliases={0: 0, 1: 1},
        compiler_params=pltpu.CompilerParams(has_side_effects=_DATAFLOW),
    )(payload, land, send_sems, recv_sems, own_sem, after)
    return land


def _scatter_start(name, blocks):
    rows = blocks.shape[1]

    def body(blocks_ref, land_ref, send_sems, recv_sems, own_sem, blocks_out, land_out, token):
        me, peers = _peers()
        for k, (pos, idx) in enumerate(peers):
            pltpu.make_async_remote_copy(
                src_ref=blocks_ref.at[idx], dst_ref=land_ref.at[me], send_sem=send_sems.at[k],
                recv_sem=recv_sems.at[k], device_id=pos, device_id_type=_MESH).start()
        pltpu.make_async_copy(blocks_ref.at[me], land_ref.at[me], own_sem).start()
        token[...] = jnp.zeros_like(token)

    land = _landing(rows)
    send_sems, recv_sems, own_sem, blocks_thru, land_thru, token = pl.pallas_call(
        body, name=name,
        out_shape=(*_group_sems(), pltpu.HBM(blocks.shape, blocks.dtype), pltpu.HBM(land.shape, land.dtype), _TOKEN),
        in_specs=[_HBM, _HBM], out_specs=(_SEM, _SEM, _SEM, _HBM, _HBM, pl.BlockSpec(memory_space=pltpu.VMEM)),
        input_output_aliases={0: 3, 1: 4},
        compiler_params=pltpu.CompilerParams(has_side_effects=_DATAFLOW),
    )(_in_hbm(blocks), land)
    return (blocks_thru, land_thru, send_sems, recv_sems, own_sem), token


def _scatter_wait(groups, after):
    n = len(groups)
    ns = SEMS_PER_GROUP

    def body(*refs):
        blocks, land = refs[:n], refs[n:2 * n]
        sems = refs[2 * n:(2 + ns) * n]
        me, peers = _peers()
        for g in range(n):
            send_sems, recv_sems, own_sem = sems[ns * g:ns * (g + 1)]
            for k, (pos, idx) in enumerate(peers):
                cp = pltpu.make_async_remote_copy(
                    src_ref=blocks[g].at[idx], dst_ref=land[g].at[idx], send_sem=send_sems.at[k],
                    recv_sem=recv_sems.at[k], device_id=pos, device_id_type=_MESH)
                cp.wait_send()
                cp.wait_recv()
            pltpu.make_async_copy(blocks[g].at[me], land[g].at[me], own_sem).wait()

    hbm = [grp[0] for grp in groups] + [grp[1] for grp in groups]
    sems = [s for grp in groups for s in grp[2:]]
    out = pl.pallas_call(
        body, name="rs_wait", out_shape=tuple(pltpu.HBM(a.shape, a.dtype) for a in hbm),
        in_specs=[_HBM] * (2 * n) + [_SEM] * (ns * n) + [_ANY], out_specs=tuple([_HBM] * (2 * n)),
        input_output_aliases={i: i for i in range(2 * n)},
        compiler_params=pltpu.CompilerParams(has_side_effects=_DATAFLOW),
    )(*hbm, *sems, after)
    return list(out[n:])


def _pad_rows(flat, rows):
    return jnp.pad(flat, (0, rows * D_MODEL - flat.shape[0])).reshape(rows, D_MODEL)


SMALL_NAMES = ("conv_b_pw1", "conv_w_dw", "conv_b_dw", "conv_ln_g", "conv_ln_b", "conv_b_pw2")


def _pack_small(p):
    flat = jnp.concatenate([p[n].reshape(-1) for n in SMALL_NAMES])
    return _pad_rows(flat, ROWS_SMALL).reshape(1, ROWS_SMALL, D_MODEL)


def _unpack_small(packed):
    flat = packed.reshape(-1)
    c = D_MODEL // N_DEV
    shapes = ((1, 2 * c), (1, CONV_WIDTH, c), (1, c), (1, c), (1, c), (1, c))
    out, o = {}, 0
    for n, shape in zip(SMALL_NAMES, shapes):
        size = shape[-1] * (shape[1] if len(shape) == 3 else 1)
        out[n] = flat[o:o + size].reshape(shape)
        o += size
    return out


def _gather_payloads(p):
    t = lambda a: jnp.swapaxes(a, -1, -2).astype(BF16)
    w1t, w3t, w2 = t(p["ffn_w1"]), t(p["ffn_w3"]), p["ffn_w2"].astype(BF16)
    bits = lax.bitcast_convert_type(_pack_small(p).reshape(-1)[:ROWS_SMALL * D_MODEL // 2], jnp.uint32)
    halves = [(bits >> 16).astype(jnp.uint16), (bits & 0xFFFF).astype(jnp.uint16)]
    small = lax.bitcast_convert_type(jnp.concatenate(halves), BF16).reshape(ROWS_SMALL, D_MODEL)
    conv = jnp.concatenate([t(p["conv_w_pw1"][0]), p["conv_w_pw2"][0].astype(BF16), small], axis=0)
    ffn = [jnp.concatenate([w1t[l], w3t[l], w2[l]], axis=0) for l in range(2)]
    return [t(p["attn_w_qkv"][0]), p["attn_w_o"][0].astype(BF16), ffn[0], conv, ffn[1]]


def _device_rows(land, lo, n):
    return land[:, lo:lo + n].reshape(N_DEV * n, D_MODEL)


def _unpack_conv(land):
    words = lax.bitcast_convert_type(land[:, ROWS_PW1 + ROWS_PW2:], jnp.uint16).astype(jnp.uint32)
    words = words.reshape(N_DEV, 2, ROWS_SMALL * D_MODEL // 2)
    small = lax.bitcast_convert_type((words[:, 0] << 16) | words[:, 1], F32)
    c = D_MODEL // N_DEV
    b_pw1 = small[:, :2 * c].reshape(1, 2 * D_MODEL)
    s = 2 * c
    w_dw = small[:, s:s + CONV_WIDTH * c].reshape(N_DEV, CONV_WIDTH, c).transpose(1, 0, 2).reshape(CONV_WIDTH, D_MODEL)
    s += CONV_WIDTH * c
    b_dw, ln_g, ln_b, b_pw2 = (small[:, s + i * c:s + (i + 1) * c].reshape(1, D_MODEL) for i in range(4))
    return dict(w_pw1_t=_device_rows(land, 0, ROWS_PW1), w_pw2=_device_rows(land, ROWS_PW1, ROWS_PW2),
                b_pw1=b_pw1, w_dw=w_dw, b_dw=b_dw, ln_g=ln_g, ln_b=ln_b, b_pw2=b_pw2)


def _dest_blocks(mats):
    return jnp.concatenate([a.reshape(N_DEV, -1, D_MODEL) for a in mats], axis=1)


def _small_grad_rows(g_bpw1, g_dw, g_bdw, g_lng, g_lnb, g_bpw2):
    c = D_MODEL // N_DEV
    small = jnp.concatenate(
        [g_bpw1.reshape(N_DEV, 2 * c), g_dw.reshape(CONV_WIDTH, N_DEV, c).transpose(1, 0, 2).reshape(N_DEV, -1),
         g_bdw.reshape(N_DEV, c), g_lng.reshape(N_DEV, c), g_lnb.reshape(N_DEV, c), g_bpw2.reshape(N_DEV, c)], axis=1)
    small = jnp.pad(small, ((0, 0), (0, ROWS_SMALL * D_MODEL - SMALL_USED)))
    return small.reshape(N_DEV * ROWS_SMALL, D_MODEL).astype(BF16)


LOSS_ROW = 9


def _pack_replicated(norm_mix, norm_ffn, b_qkv, sinks, b_o, norm_final, extra=None):
    rows = [norm_mix.reshape(2, D_MODEL), norm_ffn.reshape(2, D_MODEL), _pad_rows(b_qkv.reshape(-1), 2),
            _pad_rows(sinks.reshape(-1), 1), b_o.reshape(1, D_MODEL), norm_final.reshape(1, D_MODEL)]
    if extra is not None:
        rows.append(_pad_rows(extra.reshape(-1), 1))
    p = jnp.concatenate(rows, axis=0)
    return jnp.pad(p, ((0, REPL_ROWS - p.shape[0]), (0, 0)))


def _unpack_replicated(p):
    return dict(norm_mix=p[0:2], norm_ffn=p[2:4], attn_b_qkv=p[4:6].reshape(-1)[:QKV_DIM].reshape(1, QKV_DIM),
                attn_sinks=p[6, :N_Q_HEADS].reshape(1, N_Q_HEADS), attn_b_o=p[7:8], norm_final=p[8])


WEIGHT_ORDER = ['norm_mix', 'norm_ffn', 'attn_w_qkv', 'attn_b_qkv', 'attn_sinks', 'attn_w_o', 'attn_b_o',
                'conv_w_pw1', 'conv_b_pw1', 'conv_w_dw', 'conv_b_dw', 'conv_ln_g', 'conv_ln_b', 'conv_w_pw2',
                'conv_b_pw2', 'ffn_w1', 'ffn_w3', 'ffn_w2', 'norm_final']


def kernel(x, norm_mix, norm_ffn, attn_w_qkv, attn_b_qkv, attn_sinks, attn_w_o, attn_b_o, conv_w_pw1, conv_b_pw1, conv_w_dw, conv_b_dw, conv_ln_g, conv_ln_b, conv_w_pw2, conv_b_pw2, ffn_w1, ffn_w3, ffn_w2, norm_final, loss_target, m_norm_mix, m_norm_ffn, m_attn_w_qkv, m_attn_b_qkv, m_attn_sinks, m_attn_w_o, m_attn_b_o, m_conv_w_pw1, m_conv_b_pw1, m_conv_w_dw, m_conv_b_dw, m_conv_ln_g, m_conv_ln_b, m_conv_w_pw2, m_conv_b_pw2, m_ffn_w1, m_ffn_w3, m_ffn_w2, m_norm_final, v_norm_mix, v_norm_ffn, v_attn_w_qkv, v_attn_b_qkv, v_attn_sinks, v_attn_w_o, v_attn_b_o, v_conv_w_pw1, v_conv_b_pw1, v_conv_w_dw, v_conv_b_dw, v_conv_ln_g, v_conv_ln_b, v_conv_w_pw2, v_conv_b_pw2, v_ffn_w1, v_ffn_w3, v_ffn_w2, v_norm_final):
    xs = x[0]
    target = loss_target[0]
    seq = xs.shape[0]

    w = dict(attn_w_qkv=attn_w_qkv, attn_w_o=attn_w_o, conv_w_pw1=conv_w_pw1, conv_b_pw1=conv_b_pw1,
             conv_w_dw=conv_w_dw, conv_b_dw=conv_b_dw, conv_ln_g=conv_ln_g, conv_ln_b=conv_ln_b,
             conv_w_pw2=conv_w_pw2, conv_b_pw2=conv_b_pw2, ffn_w1=ffn_w1, ffn_w3=ffn_w3, ffn_w2=ffn_w2)
    m = dict(attn_w_qkv=m_attn_w_qkv, attn_w_o=m_attn_w_o, conv_w_pw1=m_conv_w_pw1, conv_b_pw1=m_conv_b_pw1,
             conv_w_dw=m_conv_w_dw, conv_b_dw=m_conv_b_dw, conv_ln_g=m_conv_ln_g, conv_ln_b=m_conv_ln_b,
             conv_w_pw2=m_conv_w_pw2, conv_b_pw2=m_conv_b_pw2, ffn_w1=m_ffn_w1, ffn_w3=m_ffn_w3, ffn_w2=m_ffn_w2)
    v = dict(attn_w_qkv=v_attn_w_qkv, attn_w_o=v_attn_w_o, conv_w_pw1=v_conv_w_pw1, conv_b_pw1=v_conv_b_pw1,
             conv_w_dw=v_conv_w_dw, conv_b_dw=v_conv_b_dw, conv_ln_g=v_conv_ln_g, conv_ln_b=v_conv_ln_b,
             conv_w_pw2=v_conv_w_pw2, conv_b_pw2=v_conv_b_pw2, ffn_w1=v_ffn_w1, ffn_w3=v_ffn_w3, ffn_w2=v_ffn_w2)

    payloads = _gather_payloads(w)
    (ag_qkv, ag_wo), tok = _gather_start("ag_start_attn", payloads[:2])
    (ag_ffn0, ag_conv, ag_ffn1), tok = _gather_start(
        "ag_start_rest", [payloads[2] + tok[0, 0].astype(BF16), payloads[3], payloads[4]])
    rc, rs = _rope_tables(seq)
    sinks = attn_sinks.reshape(N_Q_HEADS)
    g_mix0, g_mix1 = norm_mix[0:1] + tok[0, 0], norm_mix[1:2]
    g_ffn0, g_ffn1 = norm_ffn[0:1], norm_ffn[1:2]
    g_fin = norm_final.reshape(1, D_MODEL)

    y0 = _rms_fwd(xs, g_mix0)
    w_qkv_t = _gather_wait("ag_wait_qkv", ag_qkv, y0).reshape(QKV_DIM, D_MODEL)
    q, k, vv = _qkv_fwd(y0, w_qkv_t, attn_b_qkv, rc, rs)
    attn, probs, p_sinks = _attn_fwd(sinks, q, k, vv)
    w_o = _gather_wait("ag_wait_wo", ag_wo, attn).reshape(Q_DIM, D_MODEL)
    h1 = _mm_res("attn_out_proj", attn, w_o, attn_b_o, xs)
    w_ffn0 = _gather_wait("ag_wait_ffn0", ag_ffn0, h1)
    f0, u0, p0, s0 = _ffn_up("ffn0_up", h1, g_ffn0, w_ffn0)
    h2 = _ffn_down("ffn0_down", s0, w_ffn0, h1)
    wt = _unpack_conv(_gather_wait("ag_wait_conv", ag_conv, h2))
    wd = jnp.concatenate([wt["w_dw"][::-1], jnp.zeros((TAPS_PAD - CONV_WIDTH, D_MODEL), F32)], axis=0)
    y1, a, dwc, z = _conv_fwd(h2, g_mix1, wt["w_pw1_t"], wt["b_pw1"], wd, wt["b_dw"], wt["ln_g"], wt["ln_b"])
    h3 = _mm_res("conv_out_proj", z, wt["w_pw2"], wt["b_pw2"], h2)
    w_ffn1 = _gather_wait("ag_wait_ffn1", ag_ffn1, h3)
    f1, u1, p1, s1 = _ffn_up("ffn1_up", h3, g_ffn1, w_ffn1)
    dh4, dh4b, sq, dg_fin = _ffn_down_loss(s1, w_ffn1, h3, target, g_fin)

    du1, dp1 = _ffn_bwd_act("ffn1_bwd_act", dh4b, u1, p1, w_ffn1)
    dh3, dh3b, dg_ffn1 = _ffn_bwd_in("ffn1_bwd_in", du1, dp1, w_ffn1, h3, dh4, g_ffn1)
    blocks = _mm_tn("ffn1_dw2", s1, dh4b, tk=FF_TILE, into=(None, W2_SLOT))
    blocks = _mm_tn("ffn1_dw1", du1, f1, tk=FF_TILE, into=(blocks, W1T_SLOT))
    blocks = _mm_tn("ffn1_dw3", dp1, f1, tk=FF_TILE, into=(blocks, W3T_SLOT))
    rs_ffn1, tok = _scatter_start("rs_start_ffn1", blocks)

    da, dlg, dlb, dbdw, dwd, dbpw1, dbpw2 = _conv_bwd(dh3, wt["w_pw2"], dwc, a, wt["ln_g"] + tok[0, 0],
                                                     wt["ln_b"], wd)
    gpw2 = _mm_tn("conv_dw_pw2", z, dh3b, tk=D_MODEL)
    dh2, dh2b, dg_mix1 = _mm_rms_bwd("conv_in_bwd", da, wt["w_pw1_t"], h2, dh3, g_mix1, True)
    gpw1t = _mm_tn("conv_dw_pw1", da, y1, tk=D_MODEL)
    small_rows = _small_grad_rows(dbpw1, dwd[:CONV_WIDTH][::-1], dbdw, dlg, dlb, dbpw2)
    rs_conv, tok = _scatter_start("rs_start_conv", _dest_blocks([gpw1t, gpw2, small_rows]))

    du0, dp0 = _ffn_bwd_act("ffn0_bwd_act", dh2b, u0, p0, w_ffn0)
    dh1, dh1b, dg_ffn0 = _ffn_bwd_in("ffn0_bwd_in", du0, dp0, w_ffn0, h1, dh2, g_ffn0 + tok[0, 0])
    blocks = _mm_tn("ffn0_dw2", s0, dh2b, tk=FF_TILE, into=(None, W2_SLOT))
    blocks = _mm_tn("ffn0_dw1", du0, f0, tk=FF_TILE, into=(blocks, W1T_SLOT))
    blocks = _mm_tn("ffn0_dw3", dp0, f0, tk=FF_TILE, into=(blocks, W3T_SLOT))
    rs_ffn0, tok = _scatter_start("rs_start_ffn0", blocks)

    gwo = _mm_tn("attn_dw_o", attn, dh1b, tk=D_MODEL)
    rs_wo, tok2 = _scatter_start("rs_start_wo", _dest_blocks([gwo]))
    dattn, dbo = _nt_bias("attn_out_bwd", dh1, w_o)
    dq, dkc, dkp, dvc, dvp, dsink = _attn_bwd(probs, p_sinks + (tok[0, 0] + tok2[0, 0]), q, k, vv, dattn)
    dqkv, dbqkv = _rope_bwd(dq, dkc, dkp, dvc, dvp, rc, rs)
    gqkvt = _mm_tn("attn_dw_qkv", dqkv, y0, tk=QKV_DIM)
    rs_qkv, tok = _scatter_start("rs_start_qkv", _dest_blocks([gqkvt]))
    dx, dg_mix0 = _mm_rms_bwd("qkv_in_bwd", dqkv, w_qkv_t, xs, dh1, g_mix0 + tok[0, 0], False)

    p_ffn1, p_conv, p_ffn0, p_wo, p_qkv = _scatter_wait([rs_ffn1, rs_conv, rs_ffn0, rs_wo, rs_qkv], dx)

    tr = lambda a: jnp.swapaxes(a, -1, -2)
    same = lambda a: a
    ffn_slots = [p_ffn0, p_ffn1]
    plan = dict(
        ffn_w1=(ffn_slots, W1T_SLOT, tr), ffn_w3=(ffn_slots, W3T_SLOT, tr), ffn_w2=(ffn_slots, W2_SLOT, same),
        attn_w_qkv=([p_qkv], 0, tr), attn_w_o=([p_wo], 0, same),
        conv_w_pw2=([p_conv], ROWS_PW1 // ROWS_PW2, same))
    sharded = [{}, {}, {}, {}]
    for n, (slots, row_block, view) in plan.items():
        outs_n = _adamw_from_slots("adamw_" + n, slots, row_block, view(w[n]), view(m[n]), view(v[n]))
        for dst, t in zip(sharded, outs_n):
            dst[n] = view(t)
    small_out = _adamw_from_slots("adamw_small", [p_conv], (ROWS_PW1 + ROWS_PW2) // ROWS_SMALL,
                                  _pack_small(w), _pack_small(m), _pack_small(v))
    for dst, t in zip(sharded, small_out):
        dst.update(_unpack_small(t))
    g_pw1 = _sum_slots("rs_sum_pw1", p_conv[:, :ROWS_PW1]).T[None]
    pw1_out = _adamw_native("adamw_conv_w_pw1", g_pw1, w["conv_w_pw1"], m["conv_w_pw1"], v["conv_w_pw1"])
    for dst, t in zip(sharded, (g_pw1,) + tuple(pw1_out)):
        dst["conv_w_pw1"] = t

    part = _pack_replicated(jnp.concatenate([dg_mix0, dg_mix1]), jnp.concatenate([dg_ffn0, dg_ffn1]),
                            dbqkv, -dsink[:, 0], dbo, dg_fin, extra=sq[0, 0:1])
    parts = _all_gather("ag_replicated_grads", part)
    w_rep = _pack_replicated(norm_mix, norm_ffn, attn_b_qkv, attn_sinks, attn_b_o, norm_final)
    m_rep = _pack_replicated(m_norm_mix, m_norm_ffn, m_attn_b_qkv, m_attn_sinks, m_attn_b_o, m_norm_final)
    v_rep = _pack_replicated(v_norm_mix, v_norm_ffn, v_attn_b_qkv, v_attn_sinks, v_attn_b_o, v_norm_final)
    rep_out = _adamw_replicated(parts, w_rep, m_rep, v_rep)
    replicated = [_unpack_replicated(t) for t in rep_out]
    loss = rep_out[0][LOSS_ROW, 0] * (0.5 / D_MODEL)

    outs = [loss, dx.reshape(1, seq, D_MODEL)]
    for sh, rp in zip(sharded, replicated):
        merged = {**sh, **rp}
        outs += [merged[n] for n in WEIGHT_ORDER]
    return tuple(outs)
```

```python
import jax
import jax.numpy as jnp
from jax import lax
from jax.experimental import pallas as pl
from jax.experimental.pallas import tpu as pltpu

F32 = jnp.float32
BF16 = jnp.bfloat16

D_MODEL = 1024
HEAD_DIM = 64
N_Q_HEADS = 16
N_KV_HEADS = 2
Q_PER_KV = 8
Q_DIM = N_Q_HEADS * HEAD_DIM
KV_DIM = N_KV_HEADS * HEAD_DIM
QKV_DIM = Q_DIM + 2 * KV_DIM
BLOCK = 128
CONV_WIDTH = 31
D_FF = 2816
ROPE_THETA = 10000.0
RMS_EPS = 1e-5
LN_EPS = 1e-5
ADAM_LR = 0.001
ADAM_B1 = 0.9
ADAM_B2 = 0.999
ADAM_EPS = 1e-08
ADAM_WD = 0.01
ADAM_STEP = 10
N_DEV = 8

LANES = 128
SUBLANES = 8
TOKEN_TILE = 512
CONV_CHUNK = 64
CONV_HALO = 32
TAPS_PAD = 32
VMEM_LIMIT = 56 * 1024 * 1024
NEG_INF = float(jnp.finfo(jnp.float32).min)

ROWS_FF = D_FF // N_DEV
W1T_SLOT, W3T_SLOT, W2_SLOT = 0, 1, 2
ROWS_PW1 = 2 * D_MODEL // N_DEV
ROWS_PW2 = D_MODEL // N_DEV
ROWS_SMALL = 16
SMALL_USED = 2 * D_MODEL // N_DEV + CONV_WIDTH * (D_MODEL // N_DEV) + 4 * (D_MODEL // N_DEV)
FF_SPLIT = 2
FF_TILE_DEVS = N_DEV // FF_SPLIT
FF_TILE = FF_TILE_DEVS * ROWS_FF
REPL_ROWS = 16


def _call(body, *, name, grid, in_specs, out_specs, out_shape, scratch=(), sem=None, aliases=None):
    return pl.pallas_call(
        body, name=name, grid=grid, in_specs=in_specs, out_specs=out_specs, out_shape=out_shape,
        scratch_shapes=list(scratch), input_output_aliases=aliases or {},
        compiler_params=pltpu.CompilerParams(dimension_semantics=sem, vmem_limit_bytes=VMEM_LIMIT))


def _full(shape):
    return pl.BlockSpec(shape, lambda *_: (0,) * len(shape))


def _resident(shape):
    return pl.BlockSpec(shape, lambda *_: (0,) * len(shape), pipeline_mode=pl.Buffered(1))


def _rows(tm, n):
    return pl.BlockSpec((tm, n), lambda i, *_: (i, 0))


def _sig(x):
    return 1.0 / (1.0 + jnp.exp(-x))


def _sum_rows(x):
    return jnp.sum(x, axis=0, keepdims=True)


def _nt(a, b):
    return lax.dot_general(a, b, (((1,), (1,)), ((), ())), preferred_element_type=F32)


def _tn(a, b):
    return lax.dot_general(a, b, (((0,), (0,)), ((), ())), preferred_element_type=F32)


def _rms_stats(x):
    return lax.rsqrt(jnp.mean(x * x, axis=-1, keepdims=True) + RMS_EPS)


def _rms_bwd(dy, x, g, dres):
    r = _rms_stats(x)
    n = x * r
    dn = dy * g
    dx = dres + r * (dn - n * jnp.mean(dn * n, axis=-1, keepdims=True))
    return dx, _sum_rows(dy * n)


def _rope_tables(seq):
    half = HEAD_DIM // 2
    pos = jnp.arange(seq, dtype=F32)
    inv_freq = ROPE_THETA ** (-jnp.arange(0, HEAD_DIM, 2, dtype=F32) / HEAD_DIM)
    ang = pos[:, None] * inv_freq[None, :]
    spread = lambda t: jnp.broadcast_to(t[:, None, :], (seq, LANES // half, half)).reshape(seq, LANES)
    return spread(jnp.cos(ang)), spread(jnp.sin(ang))


def _signed_sin(sin):
    lane = lax.broadcasted_iota(jnp.int32, sin.shape, 1)
    first_half = (lane & (HEAD_DIM - 1)) < HEAD_DIM // 2
    return jnp.where(first_half, -sin, 0.0), jnp.where(first_half, 0.0, sin)


def _rope(t, c, sa, sb):
    half = HEAD_DIM // 2
    return t * c + pltpu.roll(t, LANES - half, 1) * sa + pltpu.roll(t, half, 1) * sb


def _rope_t(dt, c, sa, sb):
    half = HEAD_DIM // 2
    return dt * c + pltpu.roll(dt * sa, half, 1) + pltpu.roll(dt * sb, LANES - half, 1)


def _rms_fwd(x, g):
    seq = x.shape[0]
    tm = min(2 * TOKEN_TILE, seq)

    def body(x_ref, g_ref, y_ref):
        xv = x_ref[...]
        y_ref[...] = (xv * _rms_stats(xv) * g_ref[...]).astype(BF16)

    return _call(
        body, name="rms_fwd", grid=(seq // tm,), in_specs=[_rows(tm, D_MODEL), _full((1, D_MODEL))],
        out_specs=_rows(tm, D_MODEL), out_shape=jax.ShapeDtypeStruct((seq, D_MODEL), BF16),
        sem=("parallel",))(x, g)


def _qkv_fwd(y, w, b, rc, rs):
    seq = y.shape[0]
    tm = min(TOKEN_TILE, seq)

    def body(y_ref, w_ref, b_ref, c_ref, s_ref, q_ref, k_ref, v_ref):
        qkv = _nt(y_ref[...], w_ref[...]) + b_ref[...]
        c = c_ref[...]
        sa, sb = _signed_sin(s_ref[...])
        for i in range(Q_DIM // LANES):
            blk = _rope(qkv[:, i * LANES:(i + 1) * LANES], c, sa, sb)
            q_ref[:, i * LANES:(i + 1) * LANES] = (blk * (HEAD_DIM ** -0.5)).astype(BF16)
        k_ref[...] = _rope(qkv[:, Q_DIM:Q_DIM + KV_DIM], c, sa, sb).astype(BF16)
        v_ref[...] = qkv[:, Q_DIM + KV_DIM:].astype(BF16)

    return _call(
        body, name="qkv_fwd", grid=(seq // tm,),
        in_specs=[_rows(tm, D_MODEL), _resident((QKV_DIM, D_MODEL)), _full((1, QKV_DIM)),
                  _rows(tm, LANES), _rows(tm, LANES)],
        out_specs=[_rows(tm, Q_DIM), _rows(tm, KV_DIM), _rows(tm, KV_DIM)],
        out_shape=[jax.ShapeDtypeStruct((seq, Q_DIM), BF16),
                   jax.ShapeDtypeStruct((seq, KV_DIM), BF16), jax.ShapeDtypeStruct((seq, KV_DIM), BF16)],
        sem=("parallel",))(y, w, b, rc, rs)


def _band_mask(n):
    row = lax.broadcasted_iota(jnp.int32, (BLOCK, 2 * BLOCK), 0)
    col = lax.broadcasted_iota(jnp.int32, (BLOCK, 2 * BLOCK), 1)
    rel = row + BLOCK - col
    return (rel >= 0) & (rel < BLOCK) & ((col >= BLOCK) | (n > 0))


def _softmax_with_sink(s, mask, sink):
    s = jnp.where(mask, s, NEG_INF)
    m = jnp.maximum(jnp.max(s, axis=-1, keepdims=True), sink)
    p = jnp.exp(s - m)
    e_sink = jnp.exp(sink - m)
    inv = 1.0 / (jnp.sum(p, axis=-1, keepdims=True) + e_sink)
    return p * inv, e_sink * inv


PAIRS_PER_KV = Q_PER_KV // 2


def _kv_specs():
    cur = pl.BlockSpec((BLOCK, KV_DIM), lambda n: (n, 0))
    prev = pl.BlockSpec((BLOCK, KV_DIM), lambda n: (jnp.maximum(n - 1, 0), 0))
    return cur, prev


def _low_lanes():
    return lax.broadcasted_iota(jnp.int32, (2 * BLOCK, KV_DIM), 1) < HEAD_DIM


def _kv_low_high(prev, cur, j, low):
    both = jnp.concatenate([prev, cur], axis=0).astype(F32)
    swapped = pltpu.roll(both, HEAD_DIM, 1)
    at_low, at_high = (both, swapped) if j == 0 else (swapped, both)
    return jnp.where(low, at_low, 0.0).astype(BF16), jnp.where(low, 0.0, at_high).astype(BF16)


def _fold_pair_halves(acc, j, low):
    folded = acc + pltpu.roll(acc, HEAD_DIM, 1)
    return jnp.where(low, folded, 0.0) if j == 0 else jnp.where(low, 0.0, folded)


def _pair_lanes(j, i):
    g = j * PAIRS_PER_KV + i
    return slice(g * LANES, (g + 1) * LANES), 2 * g


def _attn_fwd(sinks, q, k, v):
    seq = q.shape[0]
    cur, prev = _kv_specs()

    def body(sink_ref, q_ref, kc_ref, kp_ref, vc_ref, vp_ref, o_ref, p_ref, ps_ref):
        mask = _band_mask(pl.program_id(0))
        lane = lax.broadcasted_iota(jnp.int32, (BLOCK, LANES), 1)
        p_sinks = jnp.zeros((BLOCK, LANES), F32)
        for j in range(N_KV_HEADS):
            cs = slice(j * HEAD_DIM, (j + 1) * HEAD_DIM)
            kk = jnp.concatenate([kp_ref[:, cs], kc_ref[:, cs]], axis=0)
            vv = jnp.concatenate([vp_ref[:, cs], vc_ref[:, cs]], axis=0)
            for gq in range(Q_PER_KV):
                h = j * Q_PER_KV + gq
                hs = slice(h * HEAD_DIM, (h + 1) * HEAD_DIM)
                probs, p_sink = _softmax_with_sink(_nt(q_ref[:, hs], kk), mask, sink_ref[h])
                pb = probs.astype(BF16)
                p_ref[h] = pb
                p_sinks = jnp.where(lane == h, p_sink, p_sinks)
                o_ref[:, hs] = jnp.dot(pb, vv, preferred_element_type=F32).astype(BF16)
        ps_ref[...] = p_sinks

    return _call(
        body, name="attn_fwd", grid=(seq // BLOCK,),
        in_specs=[pl.BlockSpec(memory_space=pltpu.SMEM), _rows(BLOCK, Q_DIM), cur, prev, cur, prev],
        out_specs=[_rows(BLOCK, Q_DIM), pl.BlockSpec((N_Q_HEADS, BLOCK, 2 * BLOCK), lambda n: (0, n, 0)),
                   _rows(BLOCK, LANES)],
        out_shape=[jax.ShapeDtypeStruct((seq, Q_DIM), BF16),
                   jax.ShapeDtypeStruct((N_Q_HEADS, seq, 2 * BLOCK), BF16),
                   jax.ShapeDtypeStruct((seq, LANES), F32)],
        sem=("parallel",))(sinks, q, k, k, v, v)


def _mm_res(name, a, w, b, res):
    seq, kdim = a.shape
    n = w.shape[1]
    tm = min(2 * TOKEN_TILE, seq)

    def body(a_ref, w_ref, b_ref, r_ref, o_ref):
        o_ref[...] = r_ref[...] + (jnp.dot(a_ref[...], w_ref[...], preferred_element_type=F32) + b_ref[...])

    return _call(
        body, name=name, grid=(seq // tm,),
        in_specs=[_rows(tm, kdim), _resident((kdim, n)), _full((1, n)), _rows(tm, n)],
        out_specs=_rows(tm, n), out_shape=jax.ShapeDtypeStruct((seq, n), F32),
        sem=("parallel",))(a, w, b, res)


def _ff_whole_spec(slot):
    return pl.BlockSpec((N_DEV, ROWS_FF, D_MODEL), lambda i: (0, slot, 0), pipeline_mode=pl.Buffered(1))


def _ffn_up(name, h, g, gathered):
    seq = h.shape[0]
    tm = min(TOKEN_TILE, seq)

    def body(h_ref, g_ref, w1_ref, w3_ref, f_ref, u_ref, w_ref, s_ref):
        hv = h_ref[...]
        f = (hv * _rms_stats(hv) * g_ref[...]).astype(BF16)
        f_ref[...] = f
        for j in range(FF_SPLIT):
            devs = pl.ds(j * FF_TILE_DEVS, FF_TILE_DEVS)
            cols = pl.ds(j * FF_TILE, FF_TILE)
            u = _nt(f, w1_ref[devs].reshape(FF_TILE, D_MODEL))
            w = _nt(f, w3_ref[devs].reshape(FF_TILE, D_MODEL))
            u_ref[:, cols] = u.astype(BF16)
            w_ref[:, cols] = w.astype(BF16)
            s_ref[:, cols] = (u * _sig(u) * w).astype(BF16)

    ff_shape = jax.ShapeDtypeStruct((seq, D_FF), BF16)
    return _call(
        body, name=name, grid=(seq // tm,),
        in_specs=[_rows(tm, D_MODEL), _full((1, D_MODEL)), _ff_whole_spec(W1T_SLOT), _ff_whole_spec(W3T_SLOT)],
        out_specs=[_rows(tm, D_MODEL), _rows(tm, D_FF), _rows(tm, D_FF), _rows(tm, D_FF)],
        out_shape=[jax.ShapeDtypeStruct((seq, D_MODEL), BF16), ff_shape, ff_shape, ff_shape],
        sem=("parallel",))(h, g, gathered, gathered)


def _ffn_down(name, s, gathered, res):
    seq = s.shape[0]
    tm = min(2 * TOKEN_TILE, seq)

    def body(s_ref, w_ref, r_ref, o_ref):
        w2 = w_ref[...].reshape(D_FF, D_MODEL)
        o_ref[...] = r_ref[...] + jnp.dot(s_ref[...], w2, preferred_element_type=F32)

    return _call(
        body, name=name, grid=(seq // tm,),
        in_specs=[_rows(tm, D_FF), _ff_whole_spec(W2_SLOT), _rows(tm, D_MODEL)],
        out_specs=_rows(tm, D_MODEL), out_shape=jax.ShapeDtypeStruct((seq, D_MODEL), F32),
        sem=("parallel",))(s, gathered, res)


def _conv_fwd(h, g, wpw1, bpw1, wd, bdw, lng, lnb):
    seq = h.shape[0]
    tm = min(TOKEN_TILE, seq)
    n_chunks = tm // CONV_CHUNK
    win = CONV_CHUNK + CONV_HALO

    def body(h_ref, g_ref, w_ref, b_ref, wd_ref, bdw_ref, lng_ref, lnb_ref,
             y_ref, a_ref, dwc_ref, z_ref, gbuf):
        i = pl.program_id(0)

        @pl.when(i == 0)
        def _():
            gbuf[0:CONV_HALO, :] = jnp.zeros((CONV_HALO, D_MODEL), F32)

        @pl.when(i > 0)
        def _():
            gbuf[0:CONV_HALO, :] = gbuf[tm:tm + CONV_HALO, :]

        hv = h_ref[...]
        y = (hv * _rms_stats(hv) * g_ref[...]).astype(BF16)
        y_ref[...] = y
        a = _nt(y, w_ref[...]) + b_ref[...]
        a_ref[...] = a.astype(BF16)
        gbuf[CONV_HALO:CONV_HALO + tm, :] = a[:, :D_MODEL] * _sig(a[:, D_MODEL:])

        def chunk(c, carry):
            r0 = pl.multiple_of(c * CONV_CHUNK, CONV_CHUNK)
            for l in range(D_MODEL // LANES):
                ls = slice(l * LANES, (l + 1) * LANES)
                gw = gbuf[pl.ds(r0, win), ls]
                acc = jnp.zeros((CONV_CHUNK, LANES), F32) + bdw_ref[:, ls]
                for s in range(SUBLANES):
                    gs = gw if s == 0 else pltpu.roll(gw, s, 0)
                    for q in range(CONV_HALO // SUBLANES):
                        d = SUBLANES * q + s
                        if d < CONV_WIDTH:
                            lo = CONV_HALO - SUBLANES * q
                            acc = acc + wd_ref[d:d + 1, ls] * gs[lo:lo + CONV_CHUNK]
                dwc_ref[pl.ds(r0, CONV_CHUNK), ls] = acc
            return carry

        lax.fori_loop(0, n_chunks, chunk, 0)

        xv = dwc_ref[...]
        mu = jnp.mean(xv, axis=-1, keepdims=True)
        xc = xv - mu
        var = jnp.mean(xc * xc, axis=-1, keepdims=True)
        ln = xc * lax.rsqrt(var + LN_EPS) * lng_ref[...] + lnb_ref[...]
        z_ref[...] = (ln * _sig(ln)).astype(BF16)

    return _call(
        body, name="conv_fwd", grid=(seq // tm,),
        in_specs=[_rows(tm, D_MODEL), _full((1, D_MODEL)), _full((2 * D_MODEL, D_MODEL)), _full((1, 2 * D_MODEL)),
                  _full((TAPS_PAD, D_MODEL)), _full((1, D_MODEL)), _full((1, D_MODEL)), _full((1, D_MODEL))],
        out_specs=[_rows(tm, D_MODEL), _rows(tm, 2 * D_MODEL), _rows(tm, D_MODEL), _rows(tm, D_MODEL)],
        out_shape=[jax.ShapeDtypeStruct((seq, D_MODEL), BF16), jax.ShapeDtypeStruct((seq, 2 * D_MODEL), BF16),
                   jax.ShapeDtypeStruct((seq, D_MODEL), F32), jax.ShapeDtypeStruct((seq, D_MODEL), BF16)],
        scratch=[pltpu.VMEM((tm + CONV_HALO, D_MODEL), F32)],
        sem=("arbitrary",))(h, g, wpw1, bpw1, wd, bdw, lng, lnb)


def _ffn_down_loss(s, gathered, res, target, g):
    seq = s.shape[0]
    tm = min(TOKEN_TILE, seq)

    def body(s_ref, w_ref, r_ref, t_ref, g_ref, dh_ref, dhb_ref, loss_ref, dg_ref):
        @pl.when(pl.program_id(0) == 0)
        def _():
            loss_ref[...] = jnp.zeros_like(loss_ref)
            dg_ref[...] = jnp.zeros_like(dg_ref)

        hv = r_ref[...] + jnp.dot(s_ref[...], w_ref[...].reshape(D_FF, D_MODEL), preferred_element_type=F32)
        gv = g_ref[...]
        err = hv * _rms_stats(hv) * gv - t_ref[...]
        sq = jnp.sum(jnp.sum(err * err, axis=-1, keepdims=True), axis=0, keepdims=True)
        loss_ref[...] += jnp.broadcast_to(sq, loss_ref.shape)
        dx, dg = _rms_bwd(err * (1.0 / D_MODEL), hv, gv, 0.0)
        dh_ref[...] = dx
        dhb_ref[...] = dx.astype(BF16)
        dg_ref[...] += dg

    return _call(
        body, name="ffn1_down_loss", grid=(seq // tm,),
        in_specs=[_rows(tm, D_FF), _ff_whole_spec(W2_SLOT), _rows(tm, D_MODEL), _rows(tm, D_MODEL),
                  _full((1, D_MODEL))],
        out_specs=[_rows(tm, D_MODEL), _rows(tm, D_MODEL), _full((SUBLANES, LANES)), _full((1, D_MODEL))],
        out_shape=[jax.ShapeDtypeStruct((seq, D_MODEL), F32), jax.ShapeDtypeStruct((seq, D_MODEL), BF16),
                   jax.ShapeDtypeStruct((SUBLANES, LANES), F32), jax.ShapeDtypeStruct((1, D_MODEL), F32)],
        sem=("arbitrary",))(s, gathered, res, target, g)


def _ffn_bwd_act(name, dh, u, w, gathered):
    seq = dh.shape[0]
    tm = min(TOKEN_TILE, seq)

    def body(dh_ref, u_ref, w_ref, w2_ref, du_ref, dw_ref):
        dh_tile = dh_ref[...]
        for j in range(FF_SPLIT):
            cols = pl.ds(j * FF_TILE, FF_TILE)
            ds = _nt(dh_tile, w2_ref[pl.ds(j * FF_TILE_DEVS, FF_TILE_DEVS)].reshape(FF_TILE, D_MODEL))
            uv = u_ref[:, cols].astype(F32)
            sg = _sig(uv)
            dw_ref[:, cols] = (ds * (uv * sg)).astype(BF16)
            du_ref[:, cols] = (ds * w_ref[:, cols].astype(F32) * (sg * (1.0 + uv * (1.0 - sg)))).astype(BF16)

    ff_shape = jax.ShapeDtypeStruct((seq, D_FF), BF16)
    return _call(
        body, name=name, grid=(seq // tm,),
        in_specs=[_rows(tm, D_MODEL), _rows(tm, D_FF), _rows(tm, D_FF), _ff_whole_spec(W2_SLOT)],
        out_specs=[_rows(tm, D_FF), _rows(tm, D_FF)], out_shape=[ff_shape, ff_shape],
        sem=("parallel",))(dh, u, w, gathered)


def _ffn_bwd_in(name, du, dw, gathered, h_in, dres, g):
    seq = du.shape[0]
    tm = min(TOKEN_TILE, seq)

    def body(du_ref, dw_ref, w1_ref, w3_ref, h_ref, dr_ref, g_ref, dx_ref, dxb_ref, dg_ref):
        @pl.when(pl.program_id(0) == 0)
        def _():
            dg_ref[...] = jnp.zeros_like(dg_ref)

        df = jnp.dot(du_ref[...], w1_ref[...].reshape(D_FF, D_MODEL), preferred_element_type=F32)
        df = df + jnp.dot(dw_ref[...], w3_ref[...].reshape(D_FF, D_MODEL), preferred_element_type=F32)
        dx, dg = _rms_bwd(df, h_ref[...], g_ref[...], dr_ref[...])
        dx_ref[...] = dx
        dxb_ref[...] = dx.astype(BF16)
        dg_ref[...] += dg

    return _call(
        body, name=name, grid=(seq // tm,),
        in_specs=[_rows(tm, D_FF), _rows(tm, D_FF), _ff_whole_spec(W1T_SLOT), _ff_whole_spec(W3T_SLOT),
                  _rows(tm, D_MODEL), _rows(tm, D_MODEL), _full((1, D_MODEL))],
        out_specs=[_rows(tm, D_MODEL), _rows(tm, D_MODEL), _full((1, D_MODEL))],
        out_shape=[jax.ShapeDtypeStruct((seq, D_MODEL), F32), jax.ShapeDtypeStruct((seq, D_MODEL), BF16),
                   jax.ShapeDtypeStruct((1, D_MODEL), F32)],
        sem=("arbitrary",))(du, dw, gathered, gathered, h_in, dres, g)


def _mm_tn(name, a, b, *, tk, into=None):
    seq, kdim = a.shape
    n = b.shape[1]
    tt = min((4 if b.dtype == BF16 else 2) * TOKEN_TILE, seq)
    n_t = seq // tt
    devs = tk // ROWS_FF

    def body(a_ref, b_ref, *rest):
        o_ref, acc = rest[-2:]
        t = pl.program_id(1)

        @pl.when(t == 0)
        def _():
            acc[...] = jnp.zeros_like(acc)

        acc[...] += _tn(a_ref[...].astype(BF16), b_ref[...].astype(BF16))

        @pl.when(t == n_t - 1)
        def _():
            out = acc[...].astype(BF16)
            o_ref[...] = out if into is None else out.reshape(devs, ROWS_FF, n)

    in_specs = [pl.BlockSpec((tt, tk), lambda k, t: (t, k)), pl.BlockSpec((tt, n), lambda k, t: (t, 0))]
    args = [a, b]
    aliases = None
    if into is None:
        out_spec = pl.BlockSpec((tk, n), lambda k, t: (k, 0))
        out_shape = jax.ShapeDtypeStruct((kdim, n), BF16)
    else:
        blocks, slot = into
        out_spec = pl.BlockSpec((devs, ROWS_FF, n), lambda k, t: (k, slot, 0))
        out_shape = jax.ShapeDtypeStruct((N_DEV, 3 * ROWS_FF, n), BF16)
        if blocks is not None:
            in_specs.append(_ANY)
            args.append(blocks)
            aliases = {2: 0}
    return _call(
        body, name=name, grid=(kdim // tk, n_t), in_specs=in_specs, out_specs=out_spec, out_shape=out_shape,
        scratch=[pltpu.VMEM((tk, n), F32)], sem=("parallel", "arbitrary"), aliases=aliases)(*args)


def _conv_bwd(dh, wpw2, dwc, a, lng, lnb, wd):
    seq = dh.shape[0]
    tm = min(TOKEN_TILE, seq)
    nt = seq // tm
    n_chunks = tm // CONV_CHUNK
    win = CONV_CHUNK + CONV_HALO
    halo_per_tile = tm // CONV_HALO

    def body(dh_ref, w_ref, dwc_ref, a_ref, ah_ref, lng_ref, lnb_ref, wd_ref,
             da_ref, dlg_ref, dlb_ref, dbdw_ref, dwd_ref, dbpw1_ref, dbpw2_ref,
             gbuf, dbuf, dglu, dwd_part):
        i = pl.program_id(0)
        r = nt - 1 - i

        @pl.when(i == 0)
        def _():
            dlg_ref[...] = jnp.zeros_like(dlg_ref)
            dlb_ref[...] = jnp.zeros_like(dlb_ref)
            dbdw_ref[...] = jnp.zeros_like(dbdw_ref)
            dbpw1_ref[...] = jnp.zeros_like(dbpw1_ref)
            dbpw2_ref[...] = jnp.zeros_like(dbpw2_ref)
            dwd_part[...] = jnp.zeros_like(dwd_part)
            dbuf[tm:tm + CONV_HALO, :] = jnp.zeros((CONV_HALO, D_MODEL), F32)

        @pl.when(i > 0)
        def _():
            dbuf[tm:tm + CONV_HALO, :] = dbuf[0:CONV_HALO, :]

        dhv = dh_ref[...]
        dbpw2_ref[...] += _sum_rows(dhv)
        dz = _nt(dhv.astype(BF16), w_ref[...])
        xv = dwc_ref[...]
        lg = lng_ref[...]
        mu = jnp.mean(xv, axis=-1, keepdims=True)
        xc = xv - mu
        rstd = lax.rsqrt(jnp.mean(xc * xc, axis=-1, keepdims=True) + LN_EPS)
        xhat = xc * rstd
        ln = xhat * lg + lnb_ref[...]
        sg = _sig(ln)
        dln = dz * (sg * (1.0 + ln * (1.0 - sg)))
        dlg_ref[...] += _sum_rows(dln * xhat)
        dlb_ref[...] += _sum_rows(dln)
        dxh = dln * lg
        ddw = rstd * (dxh - jnp.mean(dxh, axis=-1, keepdims=True)
                      - xhat * jnp.mean(dxh * xhat, axis=-1, keepdims=True))
        dbdw_ref[...] += _sum_rows(ddw)
        dbuf[0:tm, :] = ddw

        av = a_ref[...].astype(F32)
        a1 = av[:, :D_MODEL]
        s2 = _sig(av[:, D_MODEL:])
        gbuf[CONV_HALO:CONV_HALO + tm, :] = a1 * s2
        ah = ah_ref[...].astype(F32)
        gh = ah[:, :D_MODEL] * _sig(ah[:, D_MODEL:])
        gbuf[0:CONV_HALO, :] = jnp.where(r > 0, gh, 0.0)

        def chunk(c, carry):
            r0 = pl.multiple_of(c * CONV_CHUNK, CONV_CHUNK)
            for l in range(D_MODEL // LANES):
                ls = slice(l * LANES, (l + 1) * LANES)
                dw_ = dbuf[pl.ds(r0, win), ls]
                gw = gbuf[pl.ds(r0, win), ls]
                acc = jnp.zeros((CONV_CHUNK, LANES), F32)
                for s in range(SUBLANES):
                    ds_ = dw_ if s == 0 else pltpu.roll(dw_, win - s, 0)
                    for q in range(CONV_HALO // SUBLANES):
                        d = SUBLANES * q + s
                        if d < CONV_WIDTH:
                            acc = acc + wd_ref[d:d + 1, ls] * ds_[SUBLANES * q:SUBLANES * q + CONV_CHUNK]
                            lo = CONV_HALO - SUBLANES * q
                            prod = ds_[0:CONV_CHUNK] * gw[lo:lo + CONV_CHUNK]
                            dwd_part[d, :, ls] += jnp.sum(
                                prod.reshape(CONV_CHUNK // SUBLANES, SUBLANES, LANES), axis=0)
                dglu[pl.ds(r0, CONV_CHUNK), ls] = acc
            return carry

        lax.fori_loop(0, n_chunks, chunk, 0)

        dg_ = dglu[...]
        da1 = dg_ * s2
        da2 = dg_ * a1 * s2 * (1.0 - s2)
        da_ref[:, :D_MODEL] = da1.astype(BF16)
        da_ref[:, D_MODEL:] = da2.astype(BF16)
        dbpw1_ref[:, :D_MODEL] += _sum_rows(da1)
        dbpw1_ref[:, D_MODEL:] += _sum_rows(da2)

        @pl.when(i == nt - 1)
        def _():
            dwd_ref[...] = jnp.sum(dwd_part[...], axis=1)

    rev = lambda n: pl.BlockSpec((tm, n), lambda i: (nt - 1 - i, 0))
    halo = pl.BlockSpec((CONV_HALO, 2 * D_MODEL),
                        lambda i: (jnp.maximum((nt - 1 - i) * halo_per_tile - 1, 0), 0))
    vec = lambda n: _full((1, n))
    return _call(
        body, name="conv_bwd", grid=(nt,),
        in_specs=[rev(D_MODEL), _full((D_MODEL, D_MODEL)), rev(D_MODEL), rev(2 * D_MODEL), halo,
                  vec(D_MODEL), vec(D_MODEL), _full((TAPS_PAD, D_MODEL))],
        out_specs=[rev(2 * D_MODEL), vec(D_MODEL), vec(D_MODEL), vec(D_MODEL), _full((TAPS_PAD, D_MODEL)),
                   vec(2 * D_MODEL), vec(D_MODEL)],
        out_shape=[jax.ShapeDtypeStruct((seq, 2 * D_MODEL), BF16), jax.ShapeDtypeStruct((1, D_MODEL), F32),
                   jax.ShapeDtypeStruct((1, D_MODEL), F32), jax.ShapeDtypeStruct((1, D_MODEL), F32),
                   jax.ShapeDtypeStruct((TAPS_PAD, D_MODEL), F32), jax.ShapeDtypeStruct((1, 2 * D_MODEL), F32),
                   jax.ShapeDtypeStruct((1, D_MODEL), F32)],
        scratch=[pltpu.VMEM((tm + CONV_HALO, D_MODEL), F32), pltpu.VMEM((tm + CONV_HALO, D_MODEL), F32),
                 pltpu.VMEM((tm, D_MODEL), F32), pltpu.VMEM((TAPS_PAD, SUBLANES, D_MODEL), F32)],
        sem=("arbitrary",))(dh, wpw2, dwc, a, a, lng, lnb, wd)


def _mm_rms_bwd(name, dact, wt, h_in, dres, g, bf16_copy):
    seq, n = dact.shape
    tm = min(TOKEN_TILE, seq)

    def body(da_ref, w_ref, h_ref, dr_ref, g_ref, dx_ref, *rest):
        dg_ref = rest[-1]

        @pl.when(pl.program_id(0) == 0)
        def _():
            dg_ref[...] = jnp.zeros_like(dg_ref)

        dy = jnp.dot(da_ref[...], w_ref[...], preferred_element_type=F32)
        dx, dg = _rms_bwd(dy, h_ref[...], g_ref[...], dr_ref[...])
        dx_ref[...] = dx
        if bf16_copy:
            rest[0][...] = dx.astype(BF16)
        dg_ref[...] += dg

    copy_spec = [_rows(tm, D_MODEL)] if bf16_copy else []
    copy_shape = [jax.ShapeDtypeStruct((seq, D_MODEL), BF16)] if bf16_copy else []
    return _call(
        body, name=name, grid=(seq // tm,),
        in_specs=[_rows(tm, n), _resident((n, D_MODEL)), _rows(tm, D_MODEL), _rows(tm, D_MODEL), _full((1, D_MODEL))],
        out_specs=[_rows(tm, D_MODEL), *copy_spec, _full((1, D_MODEL))],
        out_shape=[jax.ShapeDtypeStruct((seq, D_MODEL), F32), *copy_shape, jax.ShapeDtypeStruct((1, D_MODEL), F32)],
        sem=("arbitrary",))(dact, wt, h_in, dres, g)


def _nt_bias(name, dy, w):
    seq, n = dy.shape
    kdim = w.shape[0]
    tm = min(2 * TOKEN_TILE, seq)

    def body(dy_ref, w_ref, o_ref, db_ref):
        @pl.when(pl.program_id(0) == 0)
        def _():
            db_ref[...] = jnp.zeros_like(db_ref)

        dyv = dy_ref[...]
        db_ref[...] += _sum_rows(dyv)
        o_ref[...] = _nt(dyv.astype(BF16), w_ref[...]).astype(BF16)

    return _call(
        body, name=name, grid=(seq // tm,),
        in_specs=[_rows(tm, n), _resident((kdim, n))],
        out_specs=[_rows(tm, kdim), _full((1, n))],
        out_shape=[jax.ShapeDtypeStruct((seq, kdim), BF16), jax.ShapeDtypeStruct((1, n), F32)],
        sem=("arbitrary",))(dy, w)


def _attn_bwd(probs, p_sinks, q, k, v, do):
    seq = q.shape[0]
    per = 2 if seq % (2 * BLOCK) == 0 else 1
    tile = per * BLOCK
    cur = pl.BlockSpec((tile, KV_DIM), lambda n: (n, 0))
    prev = pl.BlockSpec((BLOCK, KV_DIM), lambda n: (jnp.maximum(n * per - 1, 0), 0))

    def body(p_ref, ps_ref, q_ref, kc_ref, kp_ref, vc_ref, vp_ref, do_ref,
             dq_ref, dkc_ref, dkp_ref, dvc_ref, dvp_ref, dsink_ref):
        @pl.when(pl.program_id(0) == 0)
        def _():
            dsink_ref[...] = jnp.zeros_like(dsink_ref)

        low = _low_lanes()
        lane = lax.broadcasted_iota(jnp.int32, (BLOCK, LANES), 1)
        for b in range(per):
            rows = slice(b * BLOCK, (b + 1) * BLOCK)
            before = slice((b - 1) * BLOCK, b * BLOCK)
            k_prev, v_prev = (kp_ref[...], vp_ref[...]) if b == 0 else (kc_ref[before, :], vc_ref[before, :])
            k_cur, v_cur = kc_ref[rows, :], vc_ref[rows, :]
            p_sinks_blk = ps_ref[rows, :]
            dk_all = jnp.zeros((2 * BLOCK, KV_DIM), F32)
            dv_all = jnp.zeros((2 * BLOCK, KV_DIM), F32)
            for j in range(N_KV_HEADS):
                k_lo, k_hi = _kv_low_high(k_prev, k_cur, j, low)
                v_lo, v_hi = _kv_low_high(v_prev, v_cur, j, low)
                dk_acc = jnp.zeros((2 * BLOCK, KV_DIM), F32)
                dv_acc = jnp.zeros((2 * BLOCK, KV_DIM), F32)
                for i in range(PAIRS_PER_KV):
                    ls, h = _pair_lanes(j, i)
                    qp = q_ref[rows, ls]
                    dop = do_ref[rows, ls]
                    dsb, pb16 = [], []
                    for t, v_sel in enumerate((v_lo, v_hi)):
                        pb = p_ref[h + t, rows, :]
                        pf = pb.astype(F32)
                        p_sink = jnp.sum(jnp.where(lane == h + t, p_sinks_blk, 0.0), axis=-1, keepdims=True)
                        dp = _nt(dop, v_sel)
                        delta = jnp.sum(pf * dp, axis=-1, keepdims=True)
                        dsb.append((pf * (dp - delta)).astype(BF16))
                        pb16.append(pb)
                        dsink_ref[h + t:h + t + 1, :] += jnp.broadcast_to(_sum_rows(p_sink * delta), (1, LANES))
                    dq = (jnp.dot(dsb[0], k_lo, preferred_element_type=F32)
                          + jnp.dot(dsb[1], k_hi, preferred_element_type=F32))
                    dq_ref[rows, ls] = dq * (HEAD_DIM ** -0.5)
                    dk_acc = dk_acc + jnp.where(low, _tn(dsb[0], qp), _tn(dsb[1], qp))
                    dv_acc = dv_acc + jnp.where(low, _tn(pb16[0], dop), _tn(pb16[1], dop))
                dk_all = dk_all + _fold_pair_halves(dk_acc, j, low)
                dv_all = dv_all + _fold_pair_halves(dv_acc, j, low)
            dkp_ref[rows, :] = dk_all[:BLOCK]
            dkc_ref[rows, :] = dk_all[BLOCK:]
            dvp_ref[rows, :] = dv_all[:BLOCK]
            dvc_ref[rows, :] = dv_all[BLOCK:]

    kv_out = _rows(tile, KV_DIM)
    kv_shape = jax.ShapeDtypeStruct((seq, KV_DIM), F32)
    return _call(
        body, name="attn_bwd", grid=(seq // tile,),
        in_specs=[pl.BlockSpec((N_Q_HEADS, tile, 2 * BLOCK), lambda n: (0, n, 0)), _rows(tile, LANES),
                  _rows(tile, Q_DIM), cur, prev, cur, prev, _rows(tile, Q_DIM)],
        out_specs=[_rows(tile, Q_DIM), kv_out, kv_out, kv_out, kv_out, _full((N_Q_HEADS, LANES))],
        out_shape=[jax.ShapeDtypeStruct((seq, Q_DIM), F32), kv_shape, kv_shape, kv_shape, kv_shape,
                   jax.ShapeDtypeStruct((N_Q_HEADS, LANES), F32)],
        sem=("arbitrary",))(probs, p_sinks, q, k, k, v, v, do)


def _rope_bwd(dq, dkc, dkp, dvc, dvp, rc, rs):
    seq = dq.shape[0]
    nb = seq // BLOCK
    tm = min(TOKEN_TILE, seq)
    nt = seq // tm
    per = tm // BLOCK
    nxt = pl.BlockSpec((BLOCK, KV_DIM), lambda i: (jnp.minimum((i + 1) * per, nb - 1), 0))

    def body(dq_ref, dkc_ref, dkp_ref, dkn_ref, dvc_ref, dvp_ref, dvn_ref, c_ref, s_ref, o_ref, db_ref):
        i = pl.program_id(0)

        @pl.when(i == 0)
        def _():
            db_ref[...] = jnp.zeros_like(db_ref)

        c = c_ref[...]
        sa, sb = _signed_sin(s_ref[...])
        last = i == nt - 1

        def from_next_block(prev_ref, next_ref):
            tail = jnp.where(last, 0.0, next_ref[...])
            return tail if per == 1 else jnp.concatenate([prev_ref[BLOCK:, :], tail], axis=0)

        dk = dkc_ref[...] + from_next_block(dkp_ref, dkn_ref)
        dv = dvc_ref[...] + from_next_block(dvp_ref, dvn_ref)
        for l in range(Q_DIM // LANES):
            ls = slice(l * LANES, (l + 1) * LANES)
            blk = _rope_t(dq_ref[:, ls], c, sa, sb)
            o_ref[:, ls] = blk.astype(BF16)
            db_ref[:, ls] += _sum_rows(blk)
        dkr = _rope_t(dk, c, sa, sb)
        o_ref[:, Q_DIM:Q_DIM + KV_DIM] = dkr.astype(BF16)
        db_ref[:, Q_DIM:Q_DIM + KV_DIM] += _sum_rows(dkr)
        o_ref[:, Q_DIM + KV_DIM:] = dv.astype(BF16)
        db_ref[:, Q_DIM + KV_DIM:] += _sum_rows(dv)

    kv = _rows(tm, KV_DIM)
    tab = _rows(tm, LANES)
    return _call(
        body, name="rope_bwd", grid=(nt,),
        in_specs=[_rows(tm, Q_DIM), kv, kv, nxt, kv, kv, nxt, tab, tab],
        out_specs=[_rows(tm, QKV_DIM), _full((1, QKV_DIM))],
        out_shape=[jax.ShapeDtypeStruct((seq, QKV_DIM), BF16), jax.ShapeDtypeStruct((1, QKV_DIM), F32)],
        sem=("arbitrary",))(dq, dkc, dkp, dkp, dvc, dvp, dvp, rc, rs)


def _adamw(w, g, m, v):
    m = ADAM_B1 * m + (1.0 - ADAM_B1) * g
    v = ADAM_B2 * v + (1.0 - ADAM_B2) * (g * g)
    m_hat = m / (1.0 - ADAM_B1 ** ADAM_STEP)
    v_hat = v / (1.0 - ADAM_B2 ** ADAM_STEP)
    delta = -ADAM_LR * (m_hat / (jnp.sqrt(v_hat) + ADAM_EPS) + ADAM_WD * w)
    return delta, m, v


def _sum_slots(name, parts):
    _, rows, cols = parts.shape
    tr = rows if rows <= 512 else ROWS_FF

    def body(p_ref, g_ref):
        g = p_ref[0].astype(F32)
        for d in range(1, N_DEV):
            g = g + p_ref[d].astype(F32)
        g_ref[...] = g

    return _call(
        body, name=name, grid=(rows // tr,),
        in_specs=[pl.BlockSpec((N_DEV, tr, cols), lambda i: (0, i, 0))],
        out_specs=_rows(tr, cols), out_shape=jax.ShapeDtypeStruct((rows, cols), F32),
        sem=("parallel",))(parts)


def _adamw_native(name, g, w, m, v):
    layers, rows, cols = w.shape
    tr = rows if rows <= 512 else 256

    def body(g_ref, w_ref, m_ref, v_ref, d_ref, nm_ref, nv_ref):
        d_ref[...], nm_ref[...], nv_ref[...] = _adamw(w_ref[...], g_ref[...], m_ref[...], v_ref[...])

    spec = pl.BlockSpec((1, tr, cols), lambda l, i: (l, i, 0))
    shape = jax.ShapeDtypeStruct(w.shape, F32)
    return _call(
        body, name=name, grid=(layers, rows // tr), in_specs=[spec, spec, spec, spec],
        out_specs=[spec, spec, spec], out_shape=[shape, shape, shape],
        sem=("parallel", "parallel"))(g, w, m, v)


def _adamw_from_slots(name, slots, row_block, w, m, v):
    layers, rows, cols = w.shape

    def body(*refs):
        slot_refs = refs[:layers]
        w_ref, m_ref, v_ref, g_ref, d_ref, nm_ref, nv_ref = refs[layers:]
        for l in range(layers):
            @pl.when(pl.program_id(0) == l)
            def _(p_ref=slot_refs[l]):
                g = p_ref[0].astype(F32)
                for d in range(1, N_DEV):
                    g = g + p_ref[d].astype(F32)
                g_ref[0] = g
        d, nm, nv = _adamw(w_ref[0], g_ref[0], m_ref[0], v_ref[0])
        d_ref[0], nm_ref[0], nv_ref[0] = d, nm, nv

    slot_spec = pl.BlockSpec((N_DEV, rows, cols), lambda l: (0, row_block, 0))
    spec = pl.BlockSpec((1, rows, cols), lambda l: (l, 0, 0))
    shape = jax.ShapeDtypeStruct(w.shape, F32)
    return _call(
        body, name=name, grid=(layers,), in_specs=[slot_spec] * layers + [spec, spec, spec],
        out_specs=[spec, spec, spec, spec], out_shape=[shape, shape, shape, shape],
        sem=("arbitrary",))(*slots, w, m, v)


def _adamw_replicated(parts, w, m, v):
    def body(p_ref, w_ref, m_ref, v_ref, g_ref, d_ref, nm_ref, nv_ref):
        g = p_ref[0]
        for j in range(1, N_DEV):
            g = g + p_ref[j]
        g_ref[...] = g
        d_ref[...], nm_ref[...], nv_ref[...] = _adamw(w_ref[...], g, m_ref[...], v_ref[...])

    spec = _full((REPL_ROWS, D_MODEL))
    shape = jax.ShapeDtypeStruct((REPL_ROWS, D_MODEL), F32)
    return _call(
        body, name="adamw_replicated", grid=(1,),
        in_specs=[_full((N_DEV, REPL_ROWS, D_MODEL)), spec, spec, spec],
        out_specs=[spec, spec, spec, spec], out_shape=[shape, shape, shape, shape],
        sem=("arbitrary",))(parts, w, m, v)


_MESH = pl.DeviceIdType.MESH
_ANY = pl.BlockSpec(memory_space=pl.ANY)


def _all_gather(name, xs):
    rows, cols = xs.shape

    def body(x_ref, out_ref, send_sems, recv_sems, local_sem):
        x, y, c = lax.axis_index("x"), lax.axis_index("y"), lax.axis_index("c")
        me, sibling = (x, y, c), (x, y, 1 - c)
        chips = [(1 - x, y), (x, 1 - y), (1 - x, 1 - y)]

        def slot(px, py, pc):
            return out_ref.at[4 * px + 2 * py + pc]

        def copy(k, block, to, src=None):
            return pltpu.make_async_remote_copy(
                src_ref=slot(*block) if src is None else src, dst_ref=slot(*block),
                send_sem=send_sems.at[k], recv_sem=recv_sems.at[k], device_id=to, device_id_type=_MESH)

        mine = pltpu.make_async_copy(x_ref, slot(*me), local_sem)
        mine.start()
        first = [copy(0, me, sibling, src=x_ref)]
        first += [copy(1 + j, me, (*chip, c), src=x_ref) for j, chip in enumerate(chips)]
        for cp in first:
            cp.start()
        passed = [copy(4 + j, (*chip, c), sibling) for j, chip in enumerate(chips)]
        for j, chip in enumerate(chips):
            copy(1 + j, (*chip, c), me).wait_recv()
            passed[j].start()
        copy(0, sibling, me).wait_recv()
        for j, chip in enumerate(chips):
            copy(4 + j, (*chip, 1 - c), me).wait_recv()
        for cp in first + passed:
            cp.wait_send()
        mine.wait()

    return pl.pallas_call(
        body, name=name, out_shape=jax.ShapeDtypeStruct((N_DEV, rows, cols), xs.dtype),
        in_specs=[_ANY], out_specs=_ANY,
        scratch_shapes=[pltpu.SemaphoreType.DMA((7,)), pltpu.SemaphoreType.DMA((7,)), pltpu.SemaphoreType.DMA],
    )(xs)


N_PEERS = N_DEV - 1
_HBM = pl.BlockSpec(memory_space=pltpu.HBM)
_SEM = pl.BlockSpec(memory_space=pltpu.SEMAPHORE)
_DATAFLOW = pltpu.SideEffectType.DATAFLOW_SIDE_EFFECTING
_TOKEN = jax.ShapeDtypeStruct((SUBLANES, LANES), F32)


def _peers():
    x, y, c = lax.axis_index("x"), lax.axis_index("y"), lax.axis_index("c")
    out = []
    for k in range(1, N_DEV):
        px = 1 - x if k & 4 else x
        py = 1 - y if k & 2 else y
        pc = 1 - c if k & 1 else c
        out.append(((px, py, pc), 4 * px + 2 * py + pc))
    return 4 * x + 2 * y + c, out


def _in_hbm(a):
    return pltpu.with_memory_space_constraint(a, pltpu.HBM)


def _landing(rows):
    return _in_hbm(lax.empty((N_DEV, rows, D_MODEL), BF16))


SEMS_PER_GROUP = 3


def _group_sems():
    return pltpu.SemaphoreType.DMA((N_PEERS,)), pltpu.SemaphoreType.DMA((N_PEERS,)), pltpu.SemaphoreType.DMA(())


def _gather_start(name, payloads):
    n = len(payloads)
    ns = SEMS_PER_GROUP

    def body(*refs):
        src, land = refs[:n], refs[n:2 * n]
        sems = refs[2 * n:(2 + ns) * n]
        token = refs[-1]
        me, peers = _peers()
        for g in range(n):
            send_sems, recv_sems, own_sem = sems[ns * g:ns * (g + 1)]
            for k, (pos, _) in enumerate(peers):
                pltpu.make_async_remote_copy(
                    src_ref=src[g], dst_ref=land[g].at[me], send_sem=send_sems.at[k],
                    recv_sem=recv_sems.at[k], device_id=pos, device_id_type=_MESH).start()
            pltpu.make_async_copy(src[g], land[g].at[me], own_sem).start()
        token[...] = jnp.zeros_like(token)

    lands = [_landing(p.shape[0]) for p in payloads]
    sem_shapes = [s for _ in payloads for s in _group_sems()]
    hbm_shapes = [pltpu.HBM(a.shape, a.dtype) for a in list(payloads) + lands]
    out = pl.pallas_call(
        body, name=name, out_shape=(*sem_shapes, *hbm_shapes, _TOKEN),
        in_specs=[_HBM] * (2 * n),
        out_specs=(*[_SEM] * (ns * n), *[_HBM] * (2 * n), pl.BlockSpec(memory_space=pltpu.VMEM)),
        input_output_aliases={i: ns * n + i for i in range(2 * n)},
        compiler_params=pltpu.CompilerParams(has_side_effects=_DATAFLOW),
    )(*[_in_hbm(p) for p in payloads], *lands)
    sems, thru = out[:ns * n], out[ns * n:(ns + 2) * n]
    return [(thru[g], thru[n + g], *sems[ns * g:ns * (g + 1)]) for g in range(n)], out[-1]


def _gather_wait(name, group, after):
    payload, land, send_sems, recv_sems, own_sem = group

    def body(src_ref, land_ref, send_ref, recv_ref, own_ref, after_ref, src_out, land_out):
        me, peers = _peers()
        for k, (pos, idx) in enumerate(peers):
            cp = pltpu.make_async_remote_copy(
                src_ref=src_ref, dst_ref=land_ref.at[idx], send_sem=send_ref.at[k], recv_sem=recv_ref.at[k],
                device_id=pos, device_id_type=_MESH)
            cp.wait_send()
            cp.wait_recv()
        pltpu.make_async_copy(src_ref, land_ref.at[me], own_ref).wait()

    _, land = pl.pallas_call(
        body, name=name, out_shape=(pltpu.HBM(payload.shape, payload.dtype), pltpu.HBM(land.shape, land.dtype)),
        in_specs=[_HBM, _HBM, _SEM, _SEM, _SEM, _ANY], out_specs=(_HBM, _HBM), input_output_aliases={0: 0, 1: 1},
        compiler_params=pltpu.CompilerParams(has_side_effects=_DATAFLOW),
    )(payload, land, send_sems, recv_sems, own_sem, after)
    return land


def _scatter_start(name, blocks):
    rows = blocks.shape[1]

    def body(blocks_ref, land_ref, send_sems, recv_sems, own_sem, blocks_out, land_out, token):
        me, peers = _peers()
        for k, (pos, idx) in enumerate(peers):
            pltpu.make_async_remote_copy(
                src_ref=blocks_ref.at[idx], dst_ref=land_ref.at[me], send_sem=send_sems.at[k],
                recv_sem=recv_sems.at[k], device_id=pos, device_id_type=_MESH).start()
        pltpu.make_async_copy(blocks_ref.at[me], land_ref.at[me], own_sem).start()
        token[...] = jnp.zeros_like(token)

    land = _landing(rows)
    send_sems, recv_sems, own_sem, blocks_thru, land_thru, token = pl.pallas_call(
        body, name=name,
        out_shape=(*_group_sems(), pltpu.HBM(blocks.shape, blocks.dtype), pltpu.HBM(land.shape, land.dtype), _TOKEN),
        in_specs=[_HBM, _HBM], out_specs=(_SEM, _SEM, _SEM, _HBM, _HBM, pl.BlockSpec(memory_space=pltpu.VMEM)),
        input_output_aliases={0: 3, 1: 4},
        compiler_params=pltpu.CompilerParams(has_side_effects=_DATAFLOW),
    )(_in_hbm(blocks), land)
    return (blocks_thru, land_thru, send_sems, recv_sems, own_sem), token


def _scatter_wait(groups, after):
    n = len(groups)
    ns = SEMS_PER_GROUP

    def body(*refs):
        blocks, land = refs[:n], refs[n:2 * n]
        sems = refs[2 * n:(2 + ns) * n]
        me, peers = _peers()
        for g in range(n):
            send_sems, recv_sems, own_sem = sems[ns * g:ns * (g + 1)]
            for k, (pos, idx) in enumerate(peers):
                cp = pltpu.make_async_remote_copy(
                    src_ref=blocks[g].at[idx], dst_ref=land[g].at[idx], send_sem=send_sems.at[k],
                    recv_sem=recv_sems.at[k], device_id=pos, device_id_type=_MESH)
                cp.wait_send()
                cp.wait_recv()
            pltpu.make_async_copy(blocks[g].at[me], land[g].at[me], own_sem).wait()

    hbm = [grp[0] for grp in groups] + [grp[1] for grp in groups]
    sems = [s for grp in groups for s in grp[2:]]
    out = pl.pallas_call(
        body, name="rs_wait", out_shape=tuple(pltpu.HBM(a.shape, a.dtype) for a in hbm),
        in_specs=[_HBM] * (2 * n) + [_SEM] * (ns * n) + [_ANY], out_specs=tuple([_HBM] * (2 * n)),
        input_output_aliases={i: i for i in range(2 * n)},
        compiler_params=pltpu.CompilerParams(has_side_effects=_DATAFLOW),
    )(*hbm, *sems, after)
    return list(out[n:])


def _pad_rows(flat, rows):
    return jnp.pad(flat, (0, rows * D_MODEL - flat.shape[0])).reshape(rows, D_MODEL)


SMALL_NAMES = ("conv_b_pw1", "conv_w_dw", "conv_b_dw", "conv_ln_g", "conv_ln_b", "conv_b_pw2")


def _pack_small(p):
    flat = jnp.concatenate([p[n].reshape(-1) for n in SMALL_NAMES])
    return _pad_rows(flat, ROWS_SMALL).reshape(1, ROWS_SMALL, D_MODEL)


def _unpack_small(packed):
    flat = packed.reshape(-1)
    c = D_MODEL // N_DEV
    shapes = ((1, 2 * c), (1, CONV_WIDTH, c), (1, c), (1, c), (1, c), (1, c))
    out, o = {}, 0
    for n, shape in zip(SMALL_NAMES, shapes):
        size = shape[-1] * (shape[1] if len(shape) == 3 else 1)
        out[n] = flat[o:o + size].reshape(shape)
        o += size
    return out


def _gather_payloads(p):
    t = lambda a: jnp.swapaxes(a, -1, -2).astype(BF16)
    w1t, w3t, w2 = t(p["ffn_w1"]), t(p["ffn_w3"]), p["ffn_w2"].astype(BF16)
    bits = lax.bitcast_convert_type(_pack_small(p).reshape(-1)[:ROWS_SMALL * D_MODEL // 2], jnp.uint32)
    halves = [(bits >> 16).astype(jnp.uint16), (bits & 0xFFFF).astype(jnp.uint16)]
    small = lax.bitcast_convert_type(jnp.concatenate(halves), BF16).reshape(ROWS_SMALL, D_MODEL)
    conv = jnp.concatenate([t(p["conv_w_pw1"][0]), p["conv_w_pw2"][0].astype(BF16), small], axis=0)
    ffn = [jnp.concatenate([w1t[l], w3t[l], w2[l]], axis=0) for l in range(2)]
    return [t(p["attn_w_qkv"][0]), p["attn_w_o"][0].astype(BF16), ffn[0], conv, ffn[1]]


def _device_rows(land, lo, n):
    return land[:, lo:lo + n].reshape(N_DEV * n, D_MODEL)


def _unpack_conv(land):
    words = lax.bitcast_convert_type(land[:, ROWS_PW1 + ROWS_PW2:], jnp.uint16).astype(jnp.uint32)
    words = words.reshape(N_DEV, 2, ROWS_SMALL * D_MODEL // 2)
    small = lax.bitcast_convert_type((words[:, 0] << 16) | words[:, 1], F32)
    c = D_MODEL // N_DEV
    b_pw1 = small[:, :2 * c].reshape(1, 2 * D_MODEL)
    s = 2 * c
    w_dw = small[:, s:s + CONV_WIDTH * c].reshape(N_DEV, CONV_WIDTH, c).transpose(1, 0, 2).reshape(CONV_WIDTH, D_MODEL)
    s += CONV_WIDTH * c
    b_dw, ln_g, ln_b, b_pw2 = (small[:, s + i * c:s + (i + 1) * c].reshape(1, D_MODEL) for i in range(4))
    return dict(w_pw1_t=_device_rows(land, 0, ROWS_PW1), w_pw2=_device_rows(land, ROWS_PW1, ROWS_PW2),
                b_pw1=b_pw1, w_dw=w_dw, b_dw=b_dw, ln_g=ln_g, ln_b=ln_b, b_pw2=b_pw2)


def _dest_blocks(mats):
    return jnp.concatenate([a.reshape(N_DEV, -1, D_MODEL) for a in mats], axis=1)


def _small_grad_rows(g_bpw1, g_dw, g_bdw, g_lng, g_lnb, g_bpw2):
    c = D_MODEL // N_DEV
    small = jnp.concatenate(
        [g_bpw1.reshape(N_DEV, 2 * c), g_dw.reshape(CONV_WIDTH, N_DEV, c).transpose(1, 0, 2).reshape(N_DEV, -1),
         g_bdw.reshape(N_DEV, c), g_lng.reshape(N_DEV, c), g_lnb.reshape(N_DEV, c), g_bpw2.reshape(N_DEV, c)], axis=1)
    small = jnp.pad(small, ((0, 0), (0, ROWS_SMALL * D_MODEL - SMALL_USED)))
    return small.reshape(N_DEV * ROWS_SMALL, D_MODEL).astype(BF16)


LOSS_ROW = 9


def _pack_replicated(norm_mix, norm_ffn, b_qkv, sinks, b_o, norm_final, extra=None):
    rows = [norm_mix.reshape(2, D_MODEL), norm_ffn.reshape(2, D_MODEL), _pad_rows(b_qkv.reshape(-1), 2),
            _pad_rows(sinks.reshape(-1), 1), b_o.reshape(1, D_MODEL), norm_final.reshape(1, D_MODEL)]
    if extra is not None:
        rows.append(_pad_rows(extra.reshape(-1), 1))
    p = jnp.concatenate(rows, axis=0)
    return jnp.pad(p, ((0, REPL_ROWS - p.shape[0]), (0, 0)))


def _unpack_replicated(p):
    return dict(norm_mix=p[0:2], norm_ffn=p[2:4], attn_b_qkv=p[4:6].reshape(-1)[:QKV_DIM].reshape(1, QKV_DIM),
                attn_sinks=p[6, :N_Q_HEADS].reshape(1, N_Q_HEADS), attn_b_o=p[7:8], norm_final=p[8])


WEIGHT_ORDER = ['norm_mix', 'norm_ffn', 'attn_w_qkv', 'attn_b_qkv', 'attn_sinks', 'attn_w_o', 'attn_b_o',
                'conv_w_pw1', 'conv_b_pw1', 'conv_w_dw', 'conv_b_dw', 'conv_ln_g', 'conv_ln_b', 'conv_w_pw2',
                'conv_b_pw2', 'ffn_w1', 'ffn_w3', 'ffn_w2', 'norm_final']


def kernel(x, norm_mix, norm_ffn, attn_w_qkv, attn_b_qkv, attn_sinks, attn_w_o, attn_b_o, conv_w_pw1, conv_b_pw1, conv_w_dw, conv_b_dw, conv_ln_g, conv_ln_b, conv_w_pw2, conv_b_pw2, ffn_w1, ffn_w3, ffn_w2, norm_final, loss_target, m_norm_mix, m_norm_ffn, m_attn_w_qkv, m_attn_b_qkv, m_attn_sinks, m_attn_w_o, m_attn_b_o, m_conv_w_pw1, m_conv_b_pw1, m_conv_w_dw, m_conv_b_dw, m_conv_ln_g, m_conv_ln_b, m_conv_w_pw2, m_conv_b_pw2, m_ffn_w1, m_ffn_w3, m_ffn_w2, m_norm_final, v_norm_mix, v_norm_ffn, v_attn_w_qkv, v_attn_b_qkv, v_attn_sinks, v_attn_w_o, v_attn_b_o, v_conv_w_pw1, v_conv_b_pw1, v_conv_w_dw, v_conv_b_dw, v_conv_ln_g, v_conv_ln_b, v_conv_w_pw2, v_conv_b_pw2, v_ffn_w1, v_ffn_w3, v_ffn_w2, v_norm_final):
    xs = x[0]
    target = loss_target[0]
    seq = xs.shape[0]

    w = dict(attn_w_qkv=attn_w_qkv, attn_w_o=attn_w_o, conv_w_pw1=conv_w_pw1, conv_b_pw1=conv_b_pw1,
             conv_w_dw=conv_w_dw, conv_b_dw=conv_b_dw, conv_ln_g=conv_ln_g, conv_ln_b=conv_ln_b,
             conv_w_pw2=conv_w_pw2, conv_b_pw2=conv_b_pw2, ffn_w1=ffn_w1, ffn_w3=ffn_w3, ffn_w2=ffn_w2)
    m = dict(attn_w_qkv=m_attn_w_qkv, attn_w_o=m_attn_w_o, conv_w_pw1=m_conv_w_pw1, conv_b_pw1=m_conv_b_pw1,
             conv_w_dw=m_conv_w_dw, conv_b_dw=m_conv_b_dw, conv_ln_g=m_conv_ln_g, conv_ln_b=m_conv_ln_b,
             conv_w_pw2=m_conv_w_pw2, conv_b_pw2=m_conv_b_pw2, ffn_w1=m_ffn_w1, ffn_w3=m_ffn_w3, ffn_w2=m_ffn_w2)
    v = dict(attn_w_qkv=v_attn_w_qkv, attn_w_o=v_attn_w_o, conv_w_pw1=v_conv_w_pw1, conv_b_pw1=v_conv_b_pw1,
             conv_w_dw=v_conv_w_dw, conv_b_dw=v_conv_b_dw, conv_ln_g=v_conv_ln_g, conv_ln_b=v_conv_ln_b,
             conv_w_pw2=v_conv_w_pw2, conv_b_pw2=v_conv_b_pw2, ffn_w1=v_ffn_w1, ffn_w3=v_ffn_w3, ffn_w2=v_ffn_w2)

    payloads = _gather_payloads(w)
    (ag_qkv, ag_wo), tok = _gather_start("ag_start_attn", payloads[:2])
    (ag_ffn0, ag_conv, ag_ffn1), tok = _gather_start(
        "ag_start_rest", [payloads[2] + tok[0, 0].astype(BF16), payloads[3], payloads[4]])
    rc, rs = _rope_tables(seq)
    sinks = attn_sinks.reshape(N_Q_HEADS)
    g_mix0, g_mix1 = norm_mix[0:1] + tok[0, 0], norm_mix[1:2]
    g_ffn0, g_ffn1 = norm_ffn[0:1], norm_ffn[1:2]
    g_fin = norm_final.reshape(1, D_MODEL)

    y0 = _rms_fwd(xs, g_mix0)
    w_qkv_t = _gather_wait("ag_wait_qkv", ag_qkv, y0).reshape(QKV_DIM, D_MODEL)
    q, k, vv = _qkv_fwd(y0, w_qkv_t, attn_b_qkv, rc, rs)
    attn, probs, p_sinks = _attn_fwd(sinks, q, k, vv)
    w_o = _gather_wait("ag_wait_wo", ag_wo, attn).reshape(Q_DIM, D_MODEL)
    h1 = _mm_res("attn_out_proj", attn, w_o, attn_b_o, xs)
    w_ffn0 = _gather_wait("ag_wait_ffn0", ag_ffn0, h1)
    f0, u0, p0, s0 = _ffn_up("ffn0_up", h1, g_ffn0, w_ffn0)
    h2 = _ffn_down("ffn0_down", s0, w_ffn0, h1)
    wt = _unpack_conv(_gather_wait("ag_wait_conv", ag_conv, h2))
    wd = jnp.concatenate([wt["w_dw"][::-1], jnp.zeros((TAPS_PAD - CONV_WIDTH, D_MODEL), F32)], axis=0)
    y1, a, dwc, z = _conv_fwd(h2, g_mix1, wt["w_pw1_t"], wt["b_pw1"], wd, wt["b_dw"], wt["ln_g"], wt["ln_b"])
    h3 = _mm_res("conv_out_proj", z, wt["w_pw2"], wt["b_pw2"], h2)
    w_ffn1 = _gather_wait("ag_wait_ffn1", ag_ffn1, h3)
    f1, u1, p1, s1 = _ffn_up("ffn1_up", h3, g_ffn1, w_ffn1)
    dh4, dh4b, sq, dg_fin = _ffn_down_loss(s1, w_ffn1, h3, target, g_fin)

    du1, dp1 = _ffn_bwd_act("ffn1_bwd_act", dh4b, u1, p1, w_ffn1)
    dh3, dh3b, dg_ffn1 = _ffn_bwd_in("ffn1_bwd_in", du1, dp1, w_ffn1, h3, dh4, g_ffn1)
    blocks = _mm_tn("ffn1_dw2", s1, dh4b, tk=FF_TILE, into=(None, W2_SLOT))
    blocks = _mm_tn("ffn1_dw1", du1, f1, tk=FF_TILE, into=(blocks, W1T_SLOT))
    blocks = _mm_tn("ffn1_dw3", dp1, f1, tk=FF_TILE, into=(blocks, W3T_SLOT))
    rs_ffn1, tok = _scatter_start("rs_start_ffn1", blocks)

    da, dlg, dlb, dbdw, dwd, dbpw1, dbpw2 = _conv_bwd(dh3, wt["w_pw2"], dwc, a, wt["ln_g"] + tok[0, 0],
                                                     wt["ln_b"], wd)
    gpw2 = _mm_tn("conv_dw_pw2", z, dh3b, tk=D_MODEL)
    dh2, dh2b, dg_mix1 = _mm_rms_bwd("conv_in_bwd", da, wt["w_pw1_t"], h2, dh3, g_mix1, True)
    gpw1t = _mm_tn("conv_dw_pw1", da, y1, tk=D_MODEL)
    small_rows = _small_grad_rows(dbpw1, dwd[:CONV_WIDTH][::-1], dbdw, dlg, dlb, dbpw2)
    rs_conv, tok = _scatter_start("rs_start_conv", _dest_blocks([gpw1t, gpw2, small_rows]))

    du0, dp0 = _ffn_bwd_act("ffn0_bwd_act", dh2b, u0, p0, w_ffn0)
    dh1, dh1b, dg_ffn0 = _ffn_bwd_in("ffn0_bwd_in", du0, dp0, w_ffn0, h1, dh2, g_ffn0 + tok[0, 0])
    blocks = _mm_tn("ffn0_dw2", s0, dh2b, tk=FF_TILE, into=(None, W2_SLOT))
    blocks = _mm_tn("ffn0_dw1", du0, f0, tk=FF_TILE, into=(blocks, W1T_SLOT))
    blocks = _mm_tn("ffn0_dw3", dp0, f0, tk=FF_TILE, into=(blocks, W3T_SLOT))
    rs_ffn0, tok = _scatter_start("rs_start_ffn0", blocks)

    gwo = _mm_tn("attn_dw_o", attn, dh1b, tk=D_MODEL)
    rs_wo, tok2 = _scatter_start("rs_start_wo", _dest_blocks([gwo]))
    dattn, dbo = _nt_bias("attn_out_bwd", dh1, w_o)
    dq, dkc, dkp, dvc, dvp, dsink = _attn_bwd(probs, p_sinks + (tok[0, 0] + tok2[0, 0]), q, k, vv, dattn)
    dqkv, dbqkv = _rope_bwd(dq, dkc, dkp, dvc, dvp, rc, rs)
    gqkvt = _mm_tn("attn_dw_qkv", dqkv, y0, tk=QKV_DIM)
    rs_qkv, tok = _scatter_start("rs_start_qkv", _dest_blocks([gqkvt]))
    dx, dg_mix0 = _mm_rms_bwd("qkv_in_bwd", dqkv, w_qkv_t, xs, dh1, g_mix0 + tok[0, 0], False)

    p_ffn1, p_conv, p_ffn0, p_wo, p_qkv = _scatter_wait([rs_ffn1, rs_conv, rs_ffn0, rs_wo, rs_qkv], dx)

    tr = lambda a: jnp.swapaxes(a, -1, -2)
    same = lambda a: a
    ffn_slots = [p_ffn0, p_ffn1]
    plan = dict(
        ffn_w1=(ffn_slots, W1T_SLOT, tr), ffn_w3=(ffn_slots, W3T_SLOT, tr), ffn_w2=(ffn_slots, W2_SLOT, same),
        attn_w_qkv=([p_qkv], 0, tr), attn_w_o=([p_wo], 0, same),
        conv_w_pw2=([p_conv], ROWS_PW1 // ROWS_PW2, same))
    sharded = [{}, {}, {}, {}]
    for n, (slots, row_block, view) in plan.items():
        outs_n = _adamw_from_slots("adamw_" + n, slots, row_block, view(w[n]), view(m[n]), view(v[n]))
        for dst, t in zip(sharded, outs_n):
            dst[n] = view(t)
    small_out = _adamw_from_slots("adamw_small", [p_conv], (ROWS_PW1 + ROWS_PW2) // ROWS_SMALL,
                                  _pack_small(w), _pack_small(m), _pack_small(v))
    for dst, t in zip(sharded, small_out):
        dst.update(_unpack_small(t))
    g_pw1 = _sum_slots("rs_sum_pw1", p_conv[:, :ROWS_PW1]).T[None]
    pw1_out = _adamw_native("adamw_conv_w_pw1", g_pw1, w["conv_w_pw1"], m["conv_w_pw1"], v["conv_w_pw1"])
    for dst, t in zip(sharded, (g_pw1,) + tuple(pw1_out)):
        dst["conv_w_pw1"] = t

    part = _pack_replicated(jnp.concatenate([dg_mix0, dg_mix1]), jnp.concatenate([dg_ffn0, dg_ffn1]),
                            dbqkv, -dsink[:, 0], dbo, dg_fin, extra=sq[0, 0:1])
    parts = _all_gather("ag_replicated_grads", part)
    w_rep = _pack_replicated(norm_mix, norm_ffn, attn_b_qkv, attn_sinks, attn_b_o, norm_final)
    m_rep = _pack_replicated(m_norm_mix, m_norm_ffn, m_attn_b_qkv, m_attn_sinks, m_attn_b_o, m_norm_final)
    v_rep = _pack_replicated(v_norm_mix, v_norm_ffn, v_attn_b_qkv, v_attn_sinks, v_attn_b_o, v_norm_final)
    rep_out = _adamw_replicated(parts, w_rep, m_rep, v_rep)
    replicated = [_unpack_replicated(t) for t in rep_out]
    loss = rep_out[0][LOSS_ROW, 0] * (0.5 / D_MODEL)

    outs = [loss, dx.reshape(1, seq, D_MODEL)]
    for sh, rp in zip(sharded, replicated):
        merged = {**sh, **rp}
        outs += [merged[n] for n in WEIGHT_ORDER]
    return tuple(outs)
```
